```python
import jax, jax.numpy as jnp
from jax import lax
import numpy as np

D_MODEL = 1024
BATCH = 8
SEQ = 8192
DEPTH = 2

D_MIX = D_MODEL
SC_WIDTH = D_MIX // 4
SC_GROUPS = 4
SC_KERNEL = 3
GDN_WIDTH = D_MIX // 2
GDN_HEADS = 4
GDN_HEAD_DIM = GDN_WIDTH // GDN_HEADS
GDN_CONV = 4
GDN_CHUNK = 64
SB_WIDTH = D_MIX - SC_WIDTH - GDN_WIDTH
SB_HEADS = 4
SB_HEAD_DIM = SB_WIDTH // SB_HEADS
SB_BLOCK = 128
D_FF = 256 * ((8 * D_MODEL // 3 + 255) // 256)
FFN_CONV = 3
NORM_EPS = 1e-6
SPLIT_SIZES = (SC_WIDTH, SC_WIDTH, SC_WIDTH,
               GDN_WIDTH, GDN_WIDTH, GDN_WIDTH, GDN_WIDTH, GDN_HEADS, GDN_HEADS,
               SB_WIDTH, SB_WIDTH, SB_WIDTH)
D_IN_PROJ = 3 * SC_WIDTH + 4 * GDN_WIDTH + 2 * GDN_HEADS + 3 * SB_WIDTH

kernel_name = 'hymba_style_conv_gdn_stickbreaking_hybrid'


def rmsnorm(x, w):
    xf = x.astype(jnp.float32)
    y = xf * lax.rsqrt(jnp.mean(xf * xf, axis=-1, keepdims=True) + NORM_EPS) * w.astype(jnp.float32)
    return y.astype(x.dtype)


def l2norm(x):
    xf = x.astype(jnp.float32)
    return xf * lax.rsqrt(jnp.sum(xf * xf, axis=-1, keepdims=True) + NORM_EPS)


def causal_dwconv(x, w):
    K, C = w.shape
    return lax.conv_general_dilated(
        x, w[:, None, :].astype(x.dtype), window_strides=(1,), padding=[(K - 1, 0)],
        dimension_numbers=('NWC', 'WIO', 'NWC'), feature_group_count=C)


def split_columns(proj):
    points = [int(p) for p in np.cumsum(np.array(SPLIT_SIZES))[:-1]]
    return jnp.split(proj, points, axis=-1)


def gated_delta_rule_chunked(q, k, v, g, beta):
    f32 = jnp.float32
    Bsz, L, H, Dk = q.shape
    Dv = v.shape[-1]
    N = L // GDN_CHUNK

    def to_chunks(t):
        t = t.reshape((Bsz, N, GDN_CHUNK, H) + t.shape[3:])
        return jnp.moveaxis(t, 3, 1)

    q = to_chunks(q.astype(f32)) * (Dk ** -0.5)
    k = to_chunks(k.astype(f32))
    v = to_chunks(v.astype(f32))
    beta = to_chunks(beta.astype(f32))
    g = jnp.cumsum(to_chunks(g.astype(f32)), axis=-1)
    idx = jnp.arange(GDN_CHUNK)
    causal = idx[:, None] >= idx[None, :]
    strict = idx[:, None] > idx[None, :]
    decay = jnp.exp(jnp.where(causal, g[..., :, None] - g[..., None, :], -jnp.inf))
    k_beta = k * beta[..., None]
    lower = jnp.where(strict, jnp.einsum('bhncd,bhnsd->bhncs', k_beta, k) * decay, 0.0)
    lhs = lower + jnp.eye(GDN_CHUNK, dtype=f32)
    rhs = jnp.concatenate([v * beta[..., None], k_beta * jnp.exp(g)[..., None]], axis=-1)
    sol = lax.linalg.triangular_solve(lhs, rhs, left_side=True, lower=True, unit_diagonal=True)
    u, w = sol[..., :Dv], sol[..., Dv:]
    attn_intra = jnp.where(causal, jnp.einsum('bhncd,bhnsd->bhncs', q, k) * decay, 0.0)

    def step(S, inp):
        q_i, k_i, u_i, w_i, g_i, a_i = inp
        v_new = u_i - jnp.einsum('bhcd,bhde->bhce', w_i, S)
        o = jnp.einsum('bhcd,bhde->bhce', q_i * jnp.exp(g_i)[..., None], S) + \
            jnp.einsum('bhcs,bhse->bhce', a_i, v_new)
        g_last = g_i[..., -1]
        S = S * jnp.exp(g_last)[..., None, None] + \
            jnp.einsum('bhcd,bhce->bhde', k_i * jnp.exp(g_last[..., None] - g_i)[..., None], v_new)
        return S, o

    xs = tuple(jnp.moveaxis(t, 2, 0) for t in (q, k, u, w, g, attn_intra))
    S0 = jnp.zeros((Bsz, H, Dk, Dv), f32)
    _, outs = lax.scan(step, S0, xs)
    return outs.transpose(1, 0, 3, 2, 4).reshape(Bsz, L, H, Dv)


def stick_breaking_attention(q, k, v):
    L, D = q.shape[1], q.shape[-1]
    scale = D ** -0.5
    q_off = jnp.arange(SB_BLOCK)
    outs = []
    for blk in range(L // SB_BLOCK):
        start, end = blk * SB_BLOCK, (blk + 1) * SB_BLOCK
        z = jnp.einsum('bqhd,bkhd->bhqk', q[:, start:end], k[:, :end]).astype(jnp.float32) * scale
        strict = jnp.arange(end)[None, :] < (start + q_off)[:, None]
        log_beta = jax.nn.log_sigmoid(z)
        log_one_minus = jnp.where(strict, jax.nn.log_sigmoid(-z), 0.0)
        tail = lax.cumsum(log_one_minus, axis=3, reverse=True) - log_one_minus
        A = jnp.where(strict, jnp.exp(log_beta + tail), 0.0)
        outs.append(jnp.einsum('bhqk,bkhd->bqhd', A.astype(v.dtype), v[:, :end]))
    return jnp.concatenate(outs, axis=1)


def _fwd_setup_inputs(seed: int = 0) -> dict:
    key = jax.random.key(seed)
    ks = jax.random.split(key, 16)
    f32 = jnp.float32
    nrm = lambda k, shape: jax.random.normal(k, shape, f32)
    dt = jnp.exp(jax.random.uniform(ks[6], (DEPTH, GDN_HEADS), f32, np.log(1e-3), np.log(1e-1)))
    return {
        'x': nrm(ks[0], (BATCH, SEQ, D_MODEL)),
        'w_norm_mix': 1.0 + 0.02 * nrm(ks[1], (DEPTH, D_MODEL)),
        'w_mix_in': nrm(ks[2], (DEPTH, D_MODEL, D_IN_PROJ)) * D_MODEL ** -0.5,
        'w_sconv': nrm(ks[3], (DEPTH, SC_KERNEL, SC_WIDTH)) * SC_KERNEL ** -0.5,
        'w_gdn_conv': nrm(ks[4], (DEPTH, GDN_CONV, 3 * GDN_WIDTH)) * GDN_CONV ** -0.5,
        'gdn_a_log': jnp.log(jax.random.uniform(ks[5], (DEPTH, GDN_HEADS), f32, 1.0, 16.0)),
        'gdn_dt_bias': dt + jnp.log(-jnp.expm1(-dt)),
        'w_gdn_norm': 1.0 + 0.02 * nrm(ks[7], (DEPTH, GDN_HEAD_DIM)),
        'w_mix_out': nrm(ks[8], (DEPTH, D_MIX, D_MODEL)) * D_MIX ** -0.5,
        'w_norm_ffn': 1.0 + 0.02 * nrm(ks[9], (DEPTH, D_MODEL)),
        'w_ffn_up': nrm(ks[10], (DEPTH, D_MODEL, 2 * D_FF)) * D_MODEL ** -0.5,
        'w_ffn_conv': nrm(ks[11], (DEPTH, FFN_CONV, 2 * D_FF)) * FFN_CONV ** -0.5,
        'w_ffn_down': nrm(ks[12], (DEPTH, D_FF, D_MODEL)) * D_FF ** -0.5,
        'w_norm_final': 1.0 + 0.02 * nrm(ks[13], (D_MODEL,)),
    }


def _fwd_reference(x, w_norm_mix, w_mix_in, w_sconv, w_gdn_conv, gdn_a_log, gdn_dt_bias, w_gdn_norm,
              w_mix_out, w_norm_ffn, w_ffn_up, w_ffn_conv, w_ffn_down, w_norm_final):
    Bsz, L, _ = x.shape
    f32 = jnp.float32
    for l in range(DEPTH):
        h = rmsnorm(x, w_norm_mix[l])
        proj = h @ w_mix_in[l]
        (sc_b, sc_c, sc_h, gq, gk, gv, gz, ga, gb, sq, sk, sv) = split_columns(proj)

        y_sc = sc_b * causal_dwconv(sc_c * sc_h, w_sconv[l])

        qkv = jax.nn.silu(causal_dwconv(jnp.concatenate([gq, gk, gv], axis=-1), w_gdn_conv[l]))
        gq, gk, gv = jnp.split(qkv, 3, axis=-1)
        heads = lambda t: t.reshape(Bsz, L, GDN_HEADS, GDN_HEAD_DIM)
        beta = jax.nn.sigmoid(gb.astype(f32))
        g = -jnp.exp(gdn_a_log[l].astype(f32)) * jax.nn.softplus(ga.astype(f32) + gdn_dt_bias[l].astype(f32))
        o = gated_delta_rule_chunked(l2norm(heads(gq)), l2norm(heads(gk)), heads(gv), g, beta)
        o = rmsnorm(o, w_gdn_norm[l]) * jax.nn.silu(heads(gz).astype(f32))
        y_gdn = o.reshape(Bsz, L, GDN_WIDTH).astype(x.dtype)

        sb_heads = lambda t: t.reshape(Bsz, L, SB_HEADS, SB_HEAD_DIM)
        y_sb = stick_breaking_attention(sb_heads(sq), sb_heads(sk), sb_heads(sv)).reshape(Bsz, L, SB_WIDTH)

        x = x + jnp.concatenate([y_sc, y_gdn, y_sb], axis=-1) @ w_mix_out[l]

        h = rmsnorm(x, w_norm_ffn[l])
        u = causal_dwconv(h @ w_ffn_up[l], w_ffn_conv[l])
        gate, val = jnp.split(u, 2, axis=-1)
        x = x + (jax.nn.silu(gate) * val) @ w_ffn_down[l]
    return rmsnorm(x, w_norm_final)


import jax as _jax
import jax.numpy as _jnp

TWIN_FORMAT = 'train_step'
FWD_PARAMS = ['x', 'w_norm_mix', 'w_mix_in', 'w_sconv', 'w_gdn_conv', 'gdn_a_log', 'gdn_dt_bias', 'w_gdn_norm', 'w_mix_out', 'w_norm_ffn', 'w_ffn_up', 'w_ffn_conv', 'w_ffn_down', 'w_norm_final']
TWIN_WEIGHTS = ['w_norm_mix', 'w_mix_in', 'w_sconv', 'w_gdn_conv', 'gdn_a_log', 'gdn_dt_bias', 'w_gdn_norm', 'w_mix_out', 'w_norm_ffn', 'w_ffn_up', 'w_ffn_conv', 'w_ffn_down', 'w_norm_final']
TWIN_DIFF_INPUT = 'x'
TWIN_INPUTS = ['x', 'w_norm_mix', 'w_mix_in', 'w_sconv', 'w_gdn_conv', 'gdn_a_log', 'gdn_dt_bias', 'w_gdn_norm', 'w_mix_out', 'w_norm_ffn', 'w_ffn_up', 'w_ffn_conv', 'w_ffn_down', 'w_norm_final', 'loss_target', 'm_w_norm_mix', 'm_w_mix_in', 'm_w_sconv', 'm_w_gdn_conv', 'm_gdn_a_log', 'm_gdn_dt_bias', 'm_w_gdn_norm', 'm_w_mix_out', 'm_w_norm_ffn', 'm_w_ffn_up', 'm_w_ffn_conv', 'm_w_ffn_down', 'm_w_norm_final', 'v_w_norm_mix', 'v_w_mix_in', 'v_w_sconv', 'v_w_gdn_conv', 'v_gdn_a_log', 'v_gdn_dt_bias', 'v_w_gdn_norm', 'v_w_mix_out', 'v_w_norm_ffn', 'v_w_ffn_up', 'v_w_ffn_conv', 'v_w_ffn_down', 'v_w_norm_final']
TWIN_OUTPUTS = ['loss', 'grad_x', 'grad_w_norm_mix', 'grad_w_mix_in', 'grad_w_sconv', 'grad_w_gdn_conv', 'grad_gdn_a_log', 'grad_gdn_dt_bias', 'grad_w_gdn_norm', 'grad_w_mix_out', 'grad_w_norm_ffn', 'grad_w_ffn_up', 'grad_w_ffn_conv', 'grad_w_ffn_down', 'grad_w_norm_final', 'delta_w_norm_mix', 'delta_w_mix_in', 'delta_w_sconv', 'delta_w_gdn_conv', 'delta_gdn_a_log', 'delta_gdn_dt_bias', 'delta_w_gdn_norm', 'delta_w_mix_out', 'delta_w_norm_ffn', 'delta_w_ffn_up', 'delta_w_ffn_conv', 'delta_w_ffn_down', 'delta_w_norm_final', 'new_m_w_norm_mix', 'new_m_w_mix_in', 'new_m_w_sconv', 'new_m_w_gdn_conv', 'new_m_gdn_a_log', 'new_m_gdn_dt_bias', 'new_m_w_gdn_norm', 'new_m_w_mix_out', 'new_m_w_norm_ffn', 'new_m_w_ffn_up', 'new_m_w_ffn_conv', 'new_m_w_ffn_down', 'new_m_w_norm_final', 'new_v_w_norm_mix', 'new_v_w_mix_in', 'new_v_w_sconv', 'new_v_w_gdn_conv', 'new_v_gdn_a_log', 'new_v_gdn_dt_bias', 'new_v_w_gdn_norm', 'new_v_w_mix_out', 'new_v_w_norm_ffn', 'new_v_w_ffn_up', 'new_v_w_ffn_conv', 'new_v_w_ffn_down', 'new_v_w_norm_final']
TWIN_LEAF_KINDS = {'loss': 'loss', 'grad_x': 'grad_x', 'grad_w_norm_mix': 'grad_w', 'grad_w_mix_in': 'grad_w', 'grad_w_sconv': 'grad_w', 'grad_w_gdn_conv': 'grad_w', 'grad_gdn_a_log': 'grad_w', 'grad_gdn_dt_bias': 'grad_w', 'grad_w_gdn_norm': 'grad_w', 'grad_w_mix_out': 'grad_w', 'grad_w_norm_ffn': 'grad_w', 'grad_w_ffn_up': 'grad_w', 'grad_w_ffn_conv': 'grad_w', 'grad_w_ffn_down': 'grad_w', 'grad_w_norm_final': 'grad_w', 'delta_w_norm_mix': 'delta_w', 'delta_w_mix_in': 'delta_w', 'delta_w_sconv': 'delta_w', 'delta_w_gdn_conv': 'delta_w', 'delta_gdn_a_log': 'delta_w', 'delta_gdn_dt_bias': 'delta_w', 'delta_w_gdn_norm': 'delta_w', 'delta_w_mix_out': 'delta_w', 'delta_w_norm_ffn': 'delta_w', 'delta_w_ffn_up': 'delta_w', 'delta_w_ffn_conv': 'delta_w', 'delta_w_ffn_down': 'delta_w', 'delta_w_norm_final': 'delta_w', 'new_m_w_norm_mix': 'new_m', 'new_m_w_mix_in': 'new_m', 'new_m_w_sconv': 'new_m', 'new_m_w_gdn_conv': 'new_m', 'new_m_gdn_a_log': 'new_m', 'new_m_gdn_dt_bias': 'new_m', 'new_m_w_gdn_norm': 'new_m', 'new_m_w_mix_out': 'new_m', 'new_m_w_norm_ffn': 'new_m', 'new_m_w_ffn_up': 'new_m', 'new_m_w_ffn_conv': 'new_m', 'new_m_w_ffn_down': 'new_m', 'new_m_w_norm_final': 'new_m', 'new_v_w_norm_mix': 'new_v', 'new_v_w_mix_in': 'new_v', 'new_v_w_sconv': 'new_v', 'new_v_w_gdn_conv': 'new_v', 'new_v_gdn_a_log': 'new_v', 'new_v_gdn_dt_bias': 'new_v', 'new_v_w_gdn_norm': 'new_v', 'new_v_w_mix_out': 'new_v', 'new_v_w_norm_ffn': 'new_v', 'new_v_w_ffn_up': 'new_v', 'new_v_w_ffn_conv': 'new_v', 'new_v_w_ffn_down': 'new_v', 'new_v_w_norm_final': 'new_v'}


def _forward(args):
    return _fwd_reference(*[args[k] for k in FWD_PARAMS])


def _output_shape():
    def fwd():
        inp = _fwd_setup_inputs(0)
        return _fwd_reference(*[inp[k] for k in FWD_PARAMS])
    out = _jax.eval_shape(fwd)
    return out.shape, out.dtype

N_MICROBATCH = 1
ADAM_LR = 0.001
ADAM_B1 = 0.9
ADAM_B2 = 0.999
ADAM_EPS = 1e-08
ADAM_WD = 0.01
ADAM_STEP = 10
PER_EXAMPLE_BATCH_AXIS = {'x': 0, 'loss_target': 0}
SHARED_INPUTS = []
_WEIGHT_DTYPES = {'w_norm_mix': _jnp.float32, 'w_mix_in': _jnp.float32, 'w_sconv': _jnp.float32, 'w_gdn_conv': _jnp.float32, 'gdn_a_log': _jnp.float32, 'gdn_dt_bias': _jnp.float32, 'w_gdn_norm': _jnp.float32, 'w_mix_out': _jnp.float32, 'w_norm_ffn': _jnp.float32, 'w_ffn_up': _jnp.float32, 'w_ffn_conv': _jnp.float32, 'w_ffn_down': _jnp.float32, 'w_norm_final': _jnp.float32}
MOMENT_SCALE = {'w_norm_mix': 2.789003e-01, 'w_mix_in': 1.443401e-01, 'w_sconv': 2.392243e-01, 'w_gdn_conv': 1.086066e-01, 'gdn_a_log': 1.135362e+00, 'gdn_dt_bias': 1.076780e+00, 'w_gdn_norm': 2.913665e-01, 'w_mix_out': 1.657930e-01, 'w_norm_ffn': 1.746628e-01, 'w_ffn_up': 7.339242e-02, 'w_ffn_conv': 7.182577e-02, 'w_ffn_down': 1.201965e-01, 'w_norm_final': 6.397979e+01}


def _to_microbatches(a, axis):
    t = _jnp.moveaxis(a, axis, 0)
    t = t.reshape((N_MICROBATCH, t.shape[0] // N_MICROBATCH) + t.shape[1:])
    return _jnp.moveaxis(t, 1, axis + 1)


def setup_inputs(seed: int = 0) -> dict:
    inp = _fwd_setup_inputs(seed)
    key = _jax.random.fold_in(_jax.random.key(seed), 7919)
    shape, _ = _output_shape()
    out = dict(inp)
    out["loss_target"] = _jax.random.normal(_jax.random.fold_in(key, 0), shape, _jnp.float32)
    for i, name in enumerate(TWIN_WEIGHTS):
        w = inp[name].astype(_jnp.float32)
        if MOMENT_SCALE is None:
            s = _jnp.sqrt(_jnp.mean(_jnp.square(w)) + 1e-30)
        else:
            s = MOMENT_SCALE[name]
        km, kv = _jax.random.split(_jax.random.fold_in(key, i + 1))
        out[name] = w
        out["m_" + name] = s * _jax.random.normal(km, w.shape, _jnp.float32)
        out["v_" + name] = (s * s) * _jax.random.uniform(kv, w.shape, _jnp.float32, 0.5, 1.5)
    if N_MICROBATCH > 1:
        for name, axis in PER_EXAMPLE_BATCH_AXIS.items():
            out[name] = _to_microbatches(out[name], axis)
    return {'x': out['x'], 'w_norm_mix': out['w_norm_mix'], 'w_mix_in': out['w_mix_in'], 'w_sconv': out['w_sconv'], 'w_gdn_conv': out['w_gdn_conv'], 'gdn_a_log': out['gdn_a_log'], 'gdn_dt_bias': out['gdn_dt_bias'], 'w_gdn_norm': out['w_gdn_norm'], 'w_mix_out': out['w_mix_out'], 'w_norm_ffn': out['w_norm_ffn'], 'w_ffn_up': out['w_ffn_up'], 'w_ffn_conv': out['w_ffn_conv'], 'w_ffn_down': out['w_ffn_down'], 'w_norm_final': out['w_norm_final'], 'loss_target': out['loss_target'], 'm_w_norm_mix': out['m_w_norm_mix'], 'm_w_mix_in': out['m_w_mix_in'], 'm_w_sconv': out['m_w_sconv'], 'm_w_gdn_conv': out['m_w_gdn_conv'], 'm_gdn_a_log': out['m_gdn_a_log'], 'm_gdn_dt_bias': out['m_gdn_dt_bias'], 'm_w_gdn_norm': out['m_w_gdn_norm'], 'm_w_mix_out': out['m_w_mix_out'], 'm_w_norm_ffn': out['m_w_norm_ffn'], 'm_w_ffn_up': out['m_w_ffn_up'], 'm_w_ffn_conv': out['m_w_ffn_conv'], 'm_w_ffn_down': out['m_w_ffn_down'], 'm_w_norm_final': out['m_w_norm_final'], 'v_w_norm_mix': out['v_w_norm_mix'], 'v_w_mix_in': out['v_w_mix_in'], 'v_w_sconv': out['v_w_sconv'], 'v_w_gdn_conv': out['v_w_gdn_conv'], 'v_gdn_a_log': out['v_gdn_a_log'], 'v_gdn_dt_bias': out['v_gdn_dt_bias'], 'v_w_gdn_norm': out['v_w_gdn_norm'], 'v_w_mix_out': out['v_w_mix_out'], 'v_w_norm_ffn': out['v_w_norm_ffn'], 'v_w_ffn_up': out['v_w_ffn_up'], 'v_w_ffn_conv': out['v_w_ffn_conv'], 'v_w_ffn_down': out['v_w_ffn_down'], 'v_w_norm_final': out['v_w_norm_final']}


def _loss(weights, diff, rest, loss_target):
    with _jax.named_scope("forward"):
        args = {**rest, TWIN_DIFF_INPUT: diff, **{k: w.astype(_WEIGHT_DTYPES[k]) for k, w in weights.items()}}
        y = _forward(args)
    with _jax.named_scope("loss_head"):
        err = _jnp.square(y.astype(_jnp.float32) - loss_target)
        return 0.5 * _jnp.sum(_jnp.mean(err, axis=-1)) if err.ndim else 0.5 * err


def _adamw(w, g, m, v):
    m = ADAM_B1 * m + (1.0 - ADAM_B1) * g
    v = ADAM_B2 * v + (1.0 - ADAM_B2) * _jnp.square(g)
    m_hat = m / (1.0 - ADAM_B1 ** ADAM_STEP)
    v_hat = v / (1.0 - ADAM_B2 ** ADAM_STEP)
    delta = -ADAM_LR * (m_hat / (_jnp.sqrt(v_hat) + ADAM_EPS) + ADAM_WD * w)
    return delta, m, v


def reference(x, w_norm_mix, w_mix_in, w_sconv, w_gdn_conv, gdn_a_log, gdn_dt_bias, w_gdn_norm, w_mix_out, w_norm_ffn, w_ffn_up, w_ffn_conv, w_ffn_down, w_norm_final, loss_target, m_w_norm_mix, m_w_mix_in, m_w_sconv, m_w_gdn_conv, m_gdn_a_log, m_gdn_dt_bias, m_w_gdn_norm, m_w_mix_out, m_w_norm_ffn, m_w_ffn_up, m_w_ffn_conv, m_w_ffn_down, m_w_norm_final, v_w_norm_mix, v_w_mix_in, v_w_sconv, v_w_gdn_conv, v_gdn_a_log, v_gdn_dt_bias, v_w_gdn_norm, v_w_mix_out, v_w_norm_ffn, v_w_ffn_up, v_w_ffn_conv, v_w_ffn_down, v_w_norm_final):
    given = dict(x=x, w_norm_mix=w_norm_mix, w_mix_in=w_mix_in, w_sconv=w_sconv, w_gdn_conv=w_gdn_conv, gdn_a_log=gdn_a_log, gdn_dt_bias=gdn_dt_bias, w_gdn_norm=w_gdn_norm, w_mix_out=w_mix_out, w_norm_ffn=w_norm_ffn, w_ffn_up=w_ffn_up, w_ffn_conv=w_ffn_conv, w_ffn_down=w_ffn_down, w_norm_final=w_norm_final, loss_target=loss_target, m_w_norm_mix=m_w_norm_mix, m_w_mix_in=m_w_mix_in, m_w_sconv=m_w_sconv, m_w_gdn_conv=m_w_gdn_conv, m_gdn_a_log=m_gdn_a_log, m_gdn_dt_bias=m_gdn_dt_bias, m_w_gdn_norm=m_w_gdn_norm, m_w_mix_out=m_w_mix_out, m_w_norm_ffn=m_w_norm_ffn, m_w_ffn_up=m_w_ffn_up, m_w_ffn_conv=m_w_ffn_conv, m_w_ffn_down=m_w_ffn_down, m_w_norm_final=m_w_norm_final, v_w_norm_mix=v_w_norm_mix, v_w_mix_in=v_w_mix_in, v_w_sconv=v_w_sconv, v_w_gdn_conv=v_w_gdn_conv, v_gdn_a_log=v_gdn_a_log, v_gdn_dt_bias=v_gdn_dt_bias, v_w_gdn_norm=v_w_gdn_norm, v_w_mix_out=v_w_mix_out, v_w_norm_ffn=v_w_norm_ffn, v_w_ffn_up=v_w_ffn_up, v_w_ffn_conv=v_w_ffn_conv, v_w_ffn_down=v_w_ffn_down, v_w_norm_final=v_w_norm_final)
    weights = {n: given[n] for n in TWIN_WEIGHTS}
    shared = {n: given[n] for n in SHARED_INPUTS}
    per_example = {n: given[n] for n in ['x']}
    grad_fn = _jax.value_and_grad(_loss, argnums=(0, 1))

    def one_microbatch(ex, loss_target):
        ex = dict(ex)
        diff = ex.pop(TWIN_DIFF_INPUT)
        return grad_fn(weights, diff, {**shared, **ex}, loss_target)

    if N_MICROBATCH == 1:
        loss, (grad_w, grad_x) = one_microbatch(per_example, given["loss_target"])
    else:
        def body(carry, xs):
            loss_sum, grad_sum = carry
            l_k, (gw_k, gx_k) = one_microbatch(xs[0], xs[1])
            with _jax.named_scope("update"):
                return (loss_sum + l_k, _jax.tree.map(_jnp.add, grad_sum, gw_k)), gx_k

        init = (_jnp.zeros((), _jnp.float32), _jax.tree.map(_jnp.zeros_like, weights))
        (loss, grad_w), grad_x = _jax.lax.scan(body, init, (per_example, given["loss_target"]))
    with _jax.named_scope("update"):
        delta_w, new_m, new_v = {}, {}, {}
        for n in TWIN_WEIGHTS:
            delta_w[n], new_m[n], new_v[n] = _adamw(weights[n], grad_w[n], given["m_" + n], given["v_" + n])
    return (loss, grad_x, *[grad_w[n] for n in TWIN_WEIGHTS], *[delta_w[n] for n in TWIN_WEIGHTS],
            *[new_m[n] for n in TWIN_WEIGHTS], *[new_v[n] for n in TWIN_WEIGHTS])
```

```python
import functools

import jax
import jax.numpy as jnp
import numpy as np
from jax import lax
from jax.experimental import pallas as pl
from jax.experimental.pallas import tpu as pltpu

F32 = jnp.float32
BF16 = jnp.bfloat16
MESH = pl.DeviceIdType.MESH

NORM_EPS = 1e-6
GDN_HEADS = 4
GDN_CHUNK = 64
GDN_CONV = 4
SB_HEADS = 4
SC_KERNEL = 3
FFN_CONV = 3
ADAM_LR = 0.001
ADAM_B1 = 0.9
ADAM_B2 = 0.999
ADAM_EPS = 1e-08
ADAM_WD = 0.01
ADAM_STEP = 10

VMEM_LIMIT_BYTES = 48 * 1024 * 1024
HALO = 8
LANES = 128
N_CHIPS = 4


def _pc(body, *, name, grid, in_specs, out_specs, out_shape, scratch_shapes=(), dims=None):
    params = dict(vmem_limit_bytes=VMEM_LIMIT_BYTES)
    if dims is not None:
        params["dimension_semantics"] = dims
    return pl.pallas_call(body, name=name, grid=grid, in_specs=in_specs, out_specs=out_specs, out_shape=out_shape,
                          scratch_shapes=list(scratch_shapes), compiler_params=pltpu.CompilerParams(**params))


def _pc_prefetch(body, *, name, grid_spec, out_shape, dims):
    return pl.pallas_call(body, name=name, grid_spec=grid_spec, out_shape=out_shape,
                          compiler_params=pltpu.CompilerParams(vmem_limit_bytes=VMEM_LIMIT_BYTES,
                                                               dimension_semantics=dims))


def _pc_comm(body, *, name, in_specs, out_specs, out_shape, scratch_shapes):
    return pl.pallas_call(body, name=name, in_specs=in_specs, out_specs=out_specs, out_shape=out_shape,
                          scratch_shapes=list(scratch_shapes),
                          compiler_params=pltpu.CompilerParams(vmem_limit_bytes=VMEM_LIMIT_BYTES))


_DIMS = {"nn": (((1,), (0,)), ((), ())), "nt": (((1,), (1,)), ((), ())), "tn": (((0,), (0,)), ((), ()))}


def _matmul(a, b, mode, *, name, tm=512, tn=512, tk=512, out_dtype=F32, res=None):
    if mode == "nn":
        (M, K), (K2, N) = a.shape, b.shape
    elif mode == "nt":
        (M, K), (N, K2) = a.shape, b.shape
    else:
        (K, M), (K2, N) = a.shape, b.shape
    assert K == K2, (a.shape, b.shape, mode)
    tm, tn, tk = min(tm, M), min(tn, N), min(tk, K)
    assert M % tm == 0 and N % tn == 0 and K % tk == 0, (M, N, K, tm, tn, tk)
    nk = K // tk
    if mode == "tn":
        a_spec = pl.BlockSpec((tk, tm), lambda i, j, k: (k, i))
    else:
        a_spec = pl.BlockSpec((tm, tk), lambda i, j, k: (i, k))
    if mode == "nt":
        b_spec = pl.BlockSpec((tn, tk), lambda i, j, k: (j, k))
    else:
        b_spec = pl.BlockSpec((tk, tn), lambda i, j, k: (k, j))
    o_spec = pl.BlockSpec((tm, tn), lambda i, j, k: (i, j))
    has_res = res is not None
    dn = _DIMS[mode]

    def body(*refs):
        if has_res:
            a_ref, b_ref, r_ref, o_ref, acc = refs
        else:
            a_ref, b_ref, o_ref, acc = refs
        k = pl.program_id(2)
        p = lax.dot_general(a_ref[...].astype(BF16), b_ref[...].astype(BF16), dn, preferred_element_type=F32)

        def finish(total):
            if has_res:
                total = total + r_ref[...].astype(F32)
            o_ref[...] = total.astype(out_dtype)

        if nk == 1:
            finish(p)
        else:
            @pl.when(k == 0)
            def _():
                acc[...] = p

            @pl.when(k > 0)
            def _():
                acc[...] += p

            @pl.when(k == nk - 1)
            def _():
                finish(acc[...])

    in_specs = [a_spec, b_spec] + ([o_spec] if has_res else [])
    args = (a, b) + ((res,) if has_res else ())
    return _pc(body, name=name, grid=(M // tm, N // tn, nk), in_specs=in_specs, out_specs=o_spec,
               out_shape=jax.ShapeDtypeStruct((M, N), out_dtype), scratch_shapes=[pltpu.VMEM((tm, tn), F32)],
               dims=("parallel", "parallel", "arbitrary"))(*args)


def _rmsnorm_fwd(x, w, *, name, tm=512):
    L, D = x.shape
    tm = min(tm, L)

    def body(x_ref, w_ref, h_ref):
        xv = x_ref[...]
        r = lax.rsqrt(jnp.mean(xv * xv, axis=-1, keepdims=True) + NORM_EPS)
        h_ref[...] = (xv * r * w_ref[...]).astype(BF16)

    return _pc(body, name=name, grid=(L // tm,),
               in_specs=[pl.BlockSpec((tm, D), lambda i: (i, 0)), pl.BlockSpec((1, D), lambda i: (0, 0))],
               out_specs=pl.BlockSpec((tm, D), lambda i: (i, 0)), out_shape=jax.ShapeDtypeStruct((L, D), BF16),
               dims=("parallel",))(x, w.reshape(1, D))


def _rmsnorm_bwd(dh, x, w, dres, *, name, tm=512):
    L, D = x.shape
    tm = min(tm, L)

    def body(dh_ref, x_ref, w_ref, dres_ref, dx_ref, dw_ref):
        xv = x_ref[...]
        r = lax.rsqrt(jnp.mean(xv * xv, axis=-1, keepdims=True) + NORM_EPS)
        xhat = xv * r
        dhv = dh_ref[...]
        g = dhv * w_ref[...]
        dx_ref[...] = dres_ref[...] + r * (g - xhat * jnp.mean(g * xhat, axis=-1, keepdims=True))
        part = jnp.sum(dhv * xhat, axis=0, keepdims=True)

        @pl.when(pl.program_id(0) == 0)
        def _():
            dw_ref[...] = part

        @pl.when(pl.program_id(0) > 0)
        def _():
            dw_ref[...] += part

    row = pl.BlockSpec((tm, D), lambda i: (i, 0))
    vec = pl.BlockSpec((1, D), lambda i: (0, 0))
    return _pc(body, name=name, grid=(L // tm,), in_specs=[row, row, vec, row], out_specs=[row, vec],
               out_shape=[jax.ShapeDtypeStruct((L, D), F32), jax.ShapeDtypeStruct((1, D), F32)],
               dims=("arbitrary",))(dh, x, w.reshape(1, D), dres)


def _final_loss(x, w, tgt, *, name, tm=512):
    L, D = x.shape
    tm = min(tm, L)

    def body(x_ref, w_ref, t_ref, loss_ref, dx_ref, dw_ref):
        xv = x_ref[...]
        r = lax.rsqrt(jnp.mean(xv * xv, axis=-1, keepdims=True) + NORM_EPS)
        xhat = xv * r
        e = xhat * w_ref[...] - t_ref[...]
        lpart = jnp.broadcast_to(0.5 * jnp.sum(jnp.mean(e * e, axis=-1, keepdims=True), axis=0, keepdims=True),
                                 (1, LANES))
        dy = e * (1.0 / D)
        g = dy * w_ref[...]
        dx_ref[...] = r * (g - xhat * jnp.mean(g * xhat, axis=-1, keepdims=True))
        part = jnp.sum(dy * xhat, axis=0, keepdims=True)

        @pl.when(pl.program_id(0) == 0)
        def _():
            dw_ref[...] = part
            loss_ref[...] = lpart

        @pl.when(pl.program_id(0) > 0)
        def _():
            dw_ref[...] += part
            loss_ref[...] += lpart

    row = pl.BlockSpec((tm, D), lambda i: (i, 0))
    vec = pl.BlockSpec((1, D), lambda i: (0, 0))
    lsp = pl.BlockSpec((1, LANES), lambda i: (0, 0))
    return _pc(body, name=name, grid=(L // tm,), in_specs=[row, vec, row], out_specs=[lsp, row, vec],
               out_shape=[jax.ShapeDtypeStruct((1, LANES), F32), jax.ShapeDtypeStruct((L, D), F32),
                          jax.ShapeDtypeStruct((1, D), F32)],
               dims=("arbitrary",))(x, w.reshape(1, D), tgt)


def _shift_down(x, prev, k):
    if k == 0:
        return x
    r = pltpu.roll(x, k, 0)
    p = pltpu.roll(prev, k, 0)
    row = lax.broadcasted_iota(jnp.int32, p.shape, 0)
    head = jnp.where(row < k, p, r[:HALO])
    return jnp.concatenate([head, r[HALO:]], axis=0)


def _shift_up(x, j):
    if j == 0:
        return x
    return pltpu.roll(x, x.shape[0] - j, 0)


def _silu(x):
    return x * jax.nn.sigmoid(x)


def _conv_taps(p, p_prev, w):
    K = w.shape[0]
    out = None
    for k in range(K):
        term = w[k:k + 1, :] * _shift_down(p, p_prev, K - 1 - k)
        out = term if out is None else out + term
    return out


def _conv_pointwise_fwd(xs, ws, es, pre, post, outs, *, tc, tm, name):
    L = xs[0][0].shape[0]
    tm = min(tm, L)
    ncol = outs[0][0] // tc
    nrow = L // tm
    hb = tm // HALO
    nx, nw, ne, no = len(xs), len(ws), len(es), len(outs)
    K = ws[0][0].shape[0]

    def body(*refs):
        xc = [refs[2 * n][...] for n in range(nx)]
        i = pl.program_id(1)
        first = (i > 0).astype(F32)
        xp = [refs[2 * n + 1][...] * first for n in range(nx)]
        wv = [refs[2 * nx + n][...] for n in range(nw)]
        ev = [refs[2 * nx + nw + n][...] for n in range(ne)]
        o_refs = refs[2 * nx + nw + ne:]
        ps, pps = pre(*xc), pre(*xp)
        us = [_conv_taps(p, pp, w) for p, pp, w in zip(ps, pps, wv)]
        for o_ref, val in zip(o_refs, post(us, ev)):
            o_ref[...] = val.astype(o_ref.dtype)

    in_specs, args = [], []
    for arr, c0 in xs:
        off = c0 // tc
        in_specs.append(pl.BlockSpec((tm, tc), lambda j, i, off=off: (i, j + off)))
        in_specs.append(pl.BlockSpec((HALO, tc), lambda j, i, off=off: (jnp.maximum(i * hb - 1, 0), j + off)))
        args += [arr, arr]
    for arr, c0 in ws:
        off = c0 // tc
        in_specs.append(pl.BlockSpec((K, tc), lambda j, i, off=off: (0, j + off)))
        args.append(arr)
    for arr, c0 in es:
        off = c0 // tc
        in_specs.append(pl.BlockSpec((tm, tc), lambda j, i, off=off: (i, j + off)))
        args.append(arr)
    out_specs = [pl.BlockSpec((tm, tc), lambda j, i: (i, j)) for _ in range(no)]
    out_shape = [jax.ShapeDtypeStruct((L, c), dt) for c, dt in outs]
    return _pc(body, name=name, grid=(ncol, nrow), in_specs=in_specs, out_specs=out_specs, out_shape=out_shape,
               dims=("parallel", "parallel"))(*args)


def _conv_pointwise_bwd(xs, ws, es, dys, pre, post, width, *, tc, tm, name):
    L = xs[0][0].shape[0]
    tm = min(tm, L)
    ncol = width // tc
    nrow = L // tm
    hb = tm // HALO
    nx, nw, ne, ny = len(xs), len(ws), len(es), len(dys)
    K = ws[0][0].shape[0]

    def body(*refs):
        i = pl.program_id(1)
        first = (i > 0).astype(F32)
        more = (i < nrow - 1).astype(F32)
        pos = 0
        xc, xp, xe = [], [], []
        for n in range(nx):
            cur, prv, nxt = refs[pos][...], refs[pos + 1][...], refs[pos + 2][...]
            pos += 3
            xc.append(cur)
            xp.append(prv * first)
            xe.append(jnp.concatenate([cur, nxt], axis=0))
        wv = [refs[pos + n][...] for n in range(nw)]
        pos += nw
        ee = []
        for n in range(ne):
            ee.append(jnp.concatenate([refs[pos][...], refs[pos + 1][...]], axis=0))
            pos += 2
        dye = []
        for n in range(ny):
            dye.append(jnp.concatenate([refs[pos][...].astype(F32), refs[pos + 1][...].astype(F32) * more], axis=0))
            pos += 2
        dx_refs = refs[pos:pos + nx]
        de_refs = refs[pos + nx:pos + nx + ne]
        dw_refs = refs[pos + nx + ne:pos + nx + ne + nw]

        ps, pps = pre(*xe), pre(*xp)
        shifted = [[_shift_down(p, pp, K - 1 - k) for k in range(K)] for p, pp in zip(ps, pps)]
        us = []
        for n in range(nw):
            u = None
            for k in range(K):
                term = wv[n][k:k + 1, :] * shifted[n][k]
                u = term if u is None else u + term
            us.append(u)
        _, post_vjp = jax.vjp(lambda u_, e_: post(u_, e_), us, ee)
        dus, des = post_vjp(dye)
        dps = []
        for n in range(nw):
            dp = None
            for k in range(K):
                term = wv[n][k:k + 1, :] * _shift_up(dus[n], K - 1 - k)[:tm]
                dp = term if dp is None else dp + term
            dps.append(dp)
            for k in range(K):
                part = jnp.sum(dus[n][:tm] * shifted[n][k][:tm], axis=0, keepdims=True)

                @pl.when(i == 0)
                def _(part=part, n=n, k=k):
                    dw_refs[n][k:k + 1, :] = part

                @pl.when(i > 0)
                def _(part=part, n=n, k=k):
                    dw_refs[n][k:k + 1, :] += part
        _, pre_vjp = jax.vjp(lambda *x_: pre(*x_), *xc)
        dxs = pre_vjp(dps)
        for r, v in zip(dx_refs, dxs):
            r[...] = v
        for r, v in zip(de_refs, des):
            r[...] = v[:tm]

    in_specs, args = [], []

    def add_rows(arr, c0, prev, nxt):
        off = c0 // tc
        in_specs.append(pl.BlockSpec((tm, tc), lambda j, i, off=off: (i, j + off)))
        args.append(arr)
        if prev:
            in_specs.append(pl.BlockSpec((HALO, tc), lambda j, i, off=off: (jnp.maximum(i * hb - 1, 0), j + off)))
            args.append(arr)
        if nxt:
            last = L // HALO - 1
            in_specs.append(pl.BlockSpec((HALO, tc), lambda j, i, off=off: (jnp.minimum((i + 1) * hb, last), j + off)))
            args.append(arr)

    for arr, c0 in xs:
        add_rows(arr, c0, True, True)
    for arr, c0 in ws:
        off = c0 // tc
        in_specs.append(pl.BlockSpec((K, tc), lambda j, i, off=off: (0, j + off)))
        args.append(arr)
    for arr, c0 in es:
        add_rows(arr, c0, False, True)
    for arr, c0 in dys:
        add_rows(arr, c0, False, True)
    tile = pl.BlockSpec((tm, tc), lambda j, i: (i, j))
    wtile = pl.BlockSpec((K, tc), lambda j, i: (0, j))
    out_specs = [tile] * (nx + ne) + [wtile] * nw
    out_shape = [jax.ShapeDtypeStruct((L, width), F32)] * (nx + ne) + [jax.ShapeDtypeStruct((K, width), F32)] * nw
    res = _pc(body, name=name, grid=(ncol, nrow), in_specs=in_specs, out_specs=out_specs, out_shape=out_shape,
              dims=("parallel", "arbitrary"))(*args)
    return res[:nx], res[nx:nx + ne], res[nx + ne:]


def _pre_identity(*x):
    return list(x)


def _pre_product(c, h):
    return [c * h]


def _post_silu(us, es):
    return [_silu(us[0])]


def _post_gate_mul(us, es):
    return [es[0] * us[0]]


def _post_swiglu(us, es):
    return [_silu(us[0]) * us[1]]


def _make_dot(passes):
    def raw(a, b, dn):
        a_hi = a.astype(BF16)
        b_hi = b.astype(BF16)
        out = lax.dot_general(a_hi, b_hi, dn, preferred_element_type=F32)
        if passes == 3:
            a_lo = (a - a_hi.astype(F32)).astype(BF16)
            b_lo = (b - b_hi.astype(F32)).astype(BF16)
            out = out + lax.dot_general(a_hi, b_lo, dn, preferred_element_type=F32)
            out = out + lax.dot_general(a_lo, b_hi, dn, preferred_element_type=F32)
        return out

    @jax.custom_vjp
    def nn(a, b):
        return raw(a, b, _DIMS["nn"])

    @jax.custom_vjp
    def nt(a, b):
        return raw(a, b, _DIMS["nt"])

    @jax.custom_vjp
    def tn(a, b):
        return raw(a, b, _DIMS["tn"])

    nn.defvjp(lambda a, b: (nn(a, b), (a, b)), lambda r, g: (nt(g, r[1]), tn(r[0], g)))
    nt.defvjp(lambda a, b: (nt(a, b), (a, b)), lambda r, g: (nn(g, r[1]), tn(g, r[0])))
    tn.defvjp(lambda a, b: (tn(a, b), (a, b)), lambda r, g: (nt(r[1], g), nn(r[0], g)))
    return nn, nt, tn


_NN1, _NT1, _TN1 = _make_dot(1)
_NN3, _NT3, _TN3 = _make_dot(3)


def _l2norm(x):
    return x * lax.rsqrt(jnp.sum(x * x, axis=-1, keepdims=True) + NORM_EPS)


def _gdn_prep(qr, kr, v, ga, gb, a_log, dt_bias):
    C, Dh = qr.shape
    q = _l2norm(qr) * (Dh ** -0.5)
    k = _l2norm(kr)
    beta = jax.nn.sigmoid(gb)
    g = -jnp.exp(a_log) * jax.nn.softplus(ga + dt_bias)
    ii = lax.broadcasted_iota(jnp.int32, (C, C), 0)
    jj = lax.broadcasted_iota(jnp.int32, (C, C), 1)
    gc_row = jnp.sum(jnp.where(ii <= jj, g, 0.0), axis=0, keepdims=True)
    gc_col = jnp.sum(jnp.where(ii == jj, gc_row, 0.0), axis=1, keepdims=True)
    causal = ii >= jj
    strict = ii > jj
    decay = jnp.where(causal, jnp.exp(jnp.where(causal, gc_col - gc_row, 0.0)), 0.0)
    kb = k * beta
    a = jnp.where(strict, _NT1(kb, k) * decay, 0.0)
    eye = jnp.where(ii == jj, 1.0, 0.0)
    t = eye - a
    p = a
    n = 2
    while n < C:
        p = _NN3(p, p)
        t = t + _NN3(t, p)
        n *= 2
    eg_col = jnp.exp(gc_col)
    u = _NN3(t, v * beta)
    w = _NN3(t, kb * eg_col)
    attn = jnp.where(causal, _NT1(q, k) * decay, 0.0)
    qe = q * eg_col
    lane = lax.broadcasted_iota(jnp.int32, (1, C), 1)
    g_last = jnp.sum(jnp.where(lane == C - 1, gc_row, 0.0), axis=1, keepdims=True)
    ke = k * jnp.exp(g_last - gc_col)
    eg = jnp.broadcast_to(jnp.exp(g_last), (1, Dh))
    return qe, ke, u, w, attn, eg


def _gdn_step(S, qe, ke, u, w, attn, eg, gz, wgn):
    v_new = u - _NN1(w, S)
    o = _NN1(qe, S) + _NN1(attn, v_new)
    S_new = S * eg + _TN1(ke, v_new)
    y = o * lax.rsqrt(jnp.mean(o * o, axis=-1, keepdims=True) + NORM_EPS) * wgn * _silu(gz)
    return y, S_new


def _gdn_prep_fwd(qkv, ga, gb, a_log, dt_bias, *, name):
    L = qkv.shape[0]
    H, C = GDN_HEADS, GDN_CHUNK
    W = qkv.shape[1] // 3
    Dh = W // H
    N = L // C

    def body(q_ref, k_ref, v_ref, ga_ref, gb_ref, al_ref, dt_ref, qe_ref, ke_ref, u_ref, w_ref, at_ref, eg_ref):
        for h in range(H):
            sl = slice(h * Dh, (h + 1) * Dh)
            qe, ke, u, w, attn, eg = _gdn_prep(q_ref[:, sl], k_ref[:, sl], v_ref[:, sl], ga_ref[h], gb_ref[h],
                                               al_ref[h], dt_ref[h])
            qe_ref[:, sl] = qe
            ke_ref[:, sl] = ke
            u_ref[:, sl] = u
            w_ref[:, sl] = w
            at_ref[h] = attn
            eg_ref[0, h:h + 1, :] = eg

    col = lambda c: pl.BlockSpec((C, W), lambda n, c=c: (n, c))
    tok = pl.BlockSpec((H, C, 1), lambda n: (0, n, 0))
    par = pl.BlockSpec((H, 1, 1), lambda n: (0, 0, 0))
    wide = pl.BlockSpec((C, W), lambda n: (n, 0))
    return _pc(body, name=name, grid=(N,), in_specs=[col(0), col(1), col(2), tok, tok, par, par],
               out_specs=[wide, wide, wide, wide, pl.BlockSpec((H, C, C), lambda n: (0, n, 0)),
                          pl.BlockSpec((1, H, Dh), lambda n: (n, 0, 0))],
               out_shape=[jax.ShapeDtypeStruct((L, W), F32)] * 4 + [jax.ShapeDtypeStruct((H, L, C), F32),
                                                                   jax.ShapeDtypeStruct((N, H, Dh), F32)],
               dims=("parallel",))(qkv, qkv, qkv, ga, gb, a_log, dt_bias)


def _gdn_prep_bwd(qkv, ga, gb, a_log, dt_bias, dqe, dke, du, dw, dattn, deg, *, name):
    L = qkv.shape[0]
    H, C = GDN_HEADS, GDN_CHUNK
    W = qkv.shape[1] // 3
    Dh = W // H
    N = L // C

    def body(q_ref, k_ref, v_ref, ga_ref, gb_ref, al_ref, dt_ref, dqe_ref, dke_ref, du_ref, dw_ref, dat_ref, deg_ref,
             dq_ref, dk_ref, dv_ref, dga_ref, dgb_ref, dal_ref, ddt_ref):
        first = pl.program_id(0) == 0
        for h in range(H):
            sl = slice(h * Dh, (h + 1) * Dh)
            _, vjp = jax.vjp(_gdn_prep, q_ref[:, sl], k_ref[:, sl], v_ref[:, sl], ga_ref[h], gb_ref[h], al_ref[h],
                             dt_ref[h])
            dq, dk, dv, dga, dgb, dal, ddt = vjp((dqe_ref[:, sl], dke_ref[:, sl], du_ref[:, sl], dw_ref[:, sl],
                                                  dat_ref[h], deg_ref[0, h:h + 1, :]))
            dq_ref[:, sl] = dq
            dk_ref[:, sl] = dk
            dv_ref[:, sl] = dv
            dga_ref[h] = dga
            dgb_ref[h] = dgb

            @pl.when(first)
            def _(h=h, dal=dal, ddt=ddt):
                dal_ref[h] = dal
                ddt_ref[h] = ddt

            @pl.when(jnp.logical_not(first))
            def _(h=h, dal=dal, ddt=ddt):
                dal_ref[h] += dal
                ddt_ref[h] += ddt

    col = lambda c: pl.BlockSpec((C, W), lambda n, c=c: (n, c))
    tok = pl.BlockSpec((H, C, 1), lambda n: (0, n, 0))
    par = pl.BlockSpec((H, 1, 1), lambda n: (0, 0, 0))
    wide = pl.BlockSpec((C, W), lambda n: (n, 0))
    att = pl.BlockSpec((H, C, C), lambda n: (0, n, 0))
    egs = pl.BlockSpec((1, H, Dh), lambda n: (n, 0, 0))
    return _pc(body, name=name, grid=(N,),
               in_specs=[col(0), col(1), col(2), tok, tok, par, par, wide, wide, wide, wide, att, egs],
               out_specs=[wide, wide, wide, tok, tok, par, par],
               out_shape=[jax.ShapeDtypeStruct((L, W), F32)] * 3 + [jax.ShapeDtypeStruct((H, L, 1), F32)] * 2
               + [jax.ShapeDtypeStruct((H, 1, 1), F32)] * 2,
               dims=("arbitrary",))(qkv, qkv, qkv, ga, gb, a_log, dt_bias, dqe, dke, du, dw, dattn, deg)


def _gdn_scan_fwd(qe, ke, u, w, attn, eg, gz, gz_col, wgn, *, name):
    L, W = qe.shape
    H, C = GDN_HEADS, GDN_CHUNK
    Dh = W // H
    N = L // C
    gz_off = gz_col // W

    def body(qe_ref, ke_ref, u_ref, w_ref, at_ref, eg_ref, gz_ref, wgn_ref, y_ref, st_ref, s_scr):
        @pl.when(pl.program_id(0) == 0)
        def _():
            s_scr[...] = jnp.zeros_like(s_scr)

        for h in range(H):
            sl = slice(h * Dh, (h + 1) * Dh)
            S = s_scr[h]
            st_ref[0, h] = S
            y, S_new = _gdn_step(S, qe_ref[:, sl], ke_ref[:, sl], u_ref[:, sl], w_ref[:, sl], at_ref[h],
                                 eg_ref[0, h:h + 1, :], gz_ref[:, sl], wgn_ref[...])
            y_ref[:, sl] = y
            s_scr[h] = S_new

    wide = pl.BlockSpec((C, W), lambda n: (n, 0))
    return _pc(body, name=name, grid=(N,),
               in_specs=[wide, wide, wide, wide, pl.BlockSpec((H, C, C), lambda n: (0, n, 0)),
                         pl.BlockSpec((1, H, Dh), lambda n: (n, 0, 0)),
                         pl.BlockSpec((C, W), lambda n: (n, gz_off)), pl.BlockSpec((1, Dh), lambda n: (0, 0))],
               out_specs=[wide, pl.BlockSpec((1, H, Dh, Dh), lambda n: (n, 0, 0, 0))],
               out_shape=[jax.ShapeDtypeStruct((L, W), F32), jax.ShapeDtypeStruct((N, H, Dh, Dh), F32)],
               scratch_shapes=[pltpu.VMEM((H, Dh, Dh), F32)],
               dims=("arbitrary",))(qe, ke, u, w, attn, eg, gz, wgn.reshape(1, Dh))


def _gdn_scan_bwd(qe, ke, u, w, attn, eg, gz, gz_col, wgn, states, dy, dy_col, *, name):
    L, W = qe.shape
    H, C = GDN_HEADS, GDN_CHUNK
    Dh = W // H
    N = L // C
    gz_off = gz_col // W
    dy_off = dy_col // W

    def body(qe_ref, ke_ref, u_ref, w_ref, at_ref, eg_ref, gz_ref, wgn_ref, st_ref, dy_ref,
             dqe_ref, dke_ref, du_ref, dw_ref, dat_ref, deg_ref, dgz_ref, dwgn_ref, ds_scr):
        first = pl.program_id(0) == 0

        @pl.when(first)
        def _():
            ds_scr[...] = jnp.zeros_like(ds_scr)

        dwgn = None
        for h in range(H):
            sl = slice(h * Dh, (h + 1) * Dh)
            _, vjp = jax.vjp(_gdn_step, st_ref[0, h], qe_ref[:, sl], ke_ref[:, sl], u_ref[:, sl], w_ref[:, sl],
                             at_ref[h], eg_ref[0, h:h + 1, :], gz_ref[:, sl], wgn_ref[...])
            dS, dqe, dke, du, dw, dat, deg, dgz, dwg = vjp((dy_ref[:, sl], ds_scr[h]))
            ds_scr[h] = dS
            dqe_ref[:, sl] = dqe
            dke_ref[:, sl] = dke
            du_ref[:, sl] = du
            dw_ref[:, sl] = dw
            dat_ref[h] = dat
            deg_ref[0, h:h + 1, :] = deg
            dgz_ref[:, sl] = dgz
            dwgn = dwg if dwgn is None else dwgn + dwg

        @pl.when(first)
        def _():
            dwgn_ref[...] = dwgn

        @pl.when(jnp.logical_not(first))
        def _():
            dwgn_ref[...] += dwgn

    rev = lambda n: N - 1 - n
    wide = pl.BlockSpec((C, W), lambda n: (rev(n), 0))
    att = pl.BlockSpec((H, C, C), lambda n: (0, rev(n), 0))
    egs = pl.BlockSpec((1, H, Dh), lambda n: (rev(n), 0, 0))
    vec = pl.BlockSpec((1, Dh), lambda n: (0, 0))
    return _pc(body, name=name, grid=(N,),
               in_specs=[wide, wide, wide, wide, att, egs, pl.BlockSpec((C, W), lambda n: (rev(n), gz_off)), vec,
                         pl.BlockSpec((1, H, Dh, Dh), lambda n: (rev(n), 0, 0, 0)),
                         pl.BlockSpec((C, W), lambda n: (rev(n), dy_off))],
               out_specs=[wide, wide, wide, wide, att, egs, wide, vec],
               out_shape=[jax.ShapeDtypeStruct((L, W), F32)] * 4 + [jax.ShapeDtypeStruct((H, L, C), F32),
                                                                   jax.ShapeDtypeStruct((N, H, Dh), F32),
                                                                   jax.ShapeDtypeStruct((L, W), F32),
                                                                   jax.ShapeDtypeStruct((1, Dh), F32)],
               scratch_shapes=[pltpu.VMEM((H, Dh, Dh), F32)],
               dims=("arbitrary",))(qe, ke, u, w, attn, eg, gz, wgn.reshape(1, Dh), states, dy)


def _split_dot(x, tri):
    hi = x.astype(BF16)
    lo = (x - hi.astype(F32)).astype(BF16)
    return (jnp.dot(hi, tri, preferred_element_type=F32) + jnp.dot(lo, tri, preferred_element_type=F32))


def _sb_scores(q, kj, row0, col0):
    z = lax.dot_general(q, kj, _DIMS["nt"], preferred_element_type=F32)
    tq, tk = z.shape
    rows = row0 + lax.broadcasted_iota(jnp.int32, (tq, tk), 0)
    cols = col0 + lax.broadcasted_iota(jnp.int32, (tq, tk), 1)
    mask = cols < rows
    sp = jnp.maximum(z, 0.0) + jnp.log(1.0 + jnp.exp(-jnp.abs(z)))
    lom = jnp.where(mask, -sp, 0.0)
    return z, lom, z - sp, mask


def _sb_fwd(q, k, v, *, name, tq=256):
    H, L, Dh = q.shape
    tq = min(tq, L)
    tk = tq
    nq = L // tq

    def body(q_ref, k_ref, v_ref, o_ref, c_ref):
        i = pl.program_id(1)
        qv = q_ref[...]
        rr = lax.broadcasted_iota(jnp.int32, (tk, tk), 0)
        cc = lax.broadcasted_iota(jnp.int32, (tk, tk), 1)
        tri = jnp.where(rr > cc, 1.0, 0.0).astype(BF16)

        def step(t, carry):
            c, acc = carry
            j = i - t
            start = pl.multiple_of(j * tk, tk)
            kj = k_ref[pl.ds(start, tk), :]
            vj = v_ref[pl.ds(start, tk), :]
            z, lom, lb, mask = _sb_scores(qv, kj, i * tq, j * tk)
            tail = c + _split_dot(lom, tri)
            a = jnp.where(mask, jnp.exp(lb + tail), 0.0)
            acc = acc + jnp.dot(a.astype(BF16), vj, preferred_element_type=F32)
            return c + jnp.sum(lom, axis=1, keepdims=True), acc

        c, acc = lax.fori_loop(0, i + 1, step, (jnp.zeros((tq, 1), F32), jnp.zeros((tq, Dh), F32)))
        o_ref[...] = acc
        c_ref[...] = c

    qs = pl.BlockSpec((None, tq, Dh), lambda h, i: (h, i, 0))
    full = pl.BlockSpec((None, L, Dh), lambda h, i: (h, 0, 0))
    return _pc(body, name=name, grid=(H, nq), in_specs=[qs, full, full],
               out_specs=[qs, pl.BlockSpec((None, tq, 1), lambda h, i: (h, i, 0))],
               out_shape=[jax.ShapeDtypeStruct((H, L, Dh), F32), jax.ShapeDtypeStruct((H, L, 1), F32)],
               dims=("parallel", "parallel"))(q, k, v)


def _sb_bwd(q, k, v, ctot, do, scale, *, name, tq=256):
    H, L, Dh = q.shape
    tq = min(tq, L)
    tk = tq
    nq = L // tq

    def body(q_ref, k_ref, v_ref, c_ref, do_ref, dq_ref, dk_ref, dv_ref):
        i = pl.program_id(1)

        @pl.when(i == 0)
        def _():
            dk_ref[...] = jnp.zeros_like(dk_ref)
            dv_ref[...] = jnp.zeros_like(dv_ref)

        qv = q_ref[...]
        dov = do_ref[...].astype(BF16)
        ctotv = c_ref[...]
        rr = lax.broadcasted_iota(jnp.int32, (tk, tk), 0)
        cc = lax.broadcasted_iota(jnp.int32, (tk, tk), 1)
        tri_le = jnp.where(rr <= cc, 1.0, 0.0).astype(BF16)
        tri_lt = jnp.where(rr < cc, 1.0, 0.0).astype(BF16)

        def step(j, carry):
            pc, pg, dq = carry
            start = pl.multiple_of(j * tk, tk)
            kj = k_ref[pl.ds(start, tk), :]
            vj = v_ref[pl.ds(start, tk), :]
            z, lom, lb, mask = _sb_scores(qv, kj, i * tq, j * tk)
            tail = ctotv - (pc + _split_dot(lom, tri_le))
            a = jnp.where(mask, jnp.exp(lb + tail), 0.0)
            da = lax.dot_general(dov, vj, _DIMS["nt"], preferred_element_type=F32)
            g = da * a
            hh = pg + _split_dot(g, tri_lt)
            sig = jnp.exp(lb)
            dz = jnp.where(mask, g * (1.0 - sig) - hh * sig, 0.0).astype(BF16)
            dq = dq + jnp.dot(dz, kj, preferred_element_type=F32)
            dk_ref[pl.ds(start, tk), :] += lax.dot_general(dz, qv, _DIMS["tn"], preferred_element_type=F32)
            dv_ref[pl.ds(start, tk), :] += lax.dot_general(a.astype(BF16), dov, _DIMS["tn"],
                                                           preferred_element_type=F32)
            return pc + jnp.sum(lom, axis=1, keepdims=True), pg + jnp.sum(g, axis=1, keepdims=True), dq

        zero = jnp.zeros((tq, 1), F32)
        _, _, dq = lax.fori_loop(0, i + 1, step, (zero, zero, jnp.zeros((tq, Dh), F32)))
        dq_ref[...] = dq * scale

    qs = pl.BlockSpec((None, tq, Dh), lambda h, i: (h, i, 0))
    full = pl.BlockSpec((None, L, Dh), lambda h, i: (h, 0, 0))
    cs = pl.BlockSpec((None, tq, 1), lambda h, i: (h, i, 0))
    sds = jax.ShapeDtypeStruct((H, L, Dh), F32)
    return _pc(body, name=name, grid=(H, nq), in_specs=[qs, full, full, cs, qs], out_specs=[qs, full, full],
               out_shape=[sds, sds, sds], dims=("parallel", "arbitrary"))(q, k, v, ctot, do)


def _adamw(w, g, m, v, *, name, tm=256):
    R, C = w.shape
    tm = min(tm, R)
    assert R % tm == 0, (R, tm)
    c1 = 1.0 - ADAM_B1 ** ADAM_STEP
    c2 = 1.0 - ADAM_B2 ** ADAM_STEP

    def body(w_ref, g_ref, m_ref, v_ref, d_ref, nm_ref, nv_ref):
        gv = g_ref[...]
        nm = ADAM_B1 * m_ref[...] + (1.0 - ADAM_B1) * gv
        nv = ADAM_B2 * v_ref[...] + (1.0 - ADAM_B2) * (gv * gv)
        d_ref[...] = -ADAM_LR * ((nm / c1) / (jnp.sqrt(nv / c2) + ADAM_EPS) + ADAM_WD * w_ref[...])
        nm_ref[...] = nm
        nv_ref[...] = nv

    blk = pl.BlockSpec((tm, C), lambda i: (i, 0))
    sds = jax.ShapeDtypeStruct((R, C), F32)
    return _pc(body, name=name, grid=(R // tm,), in_specs=[blk] * 4, out_specs=[blk] * 3, out_shape=[sds] * 3,
               dims=("parallel",))(w, g, m, v)


def _add_halves(gbuf, ra, c, *, name, tm=256):
    _, S, Rh, C = gbuf.shape
    assert Rh % tm == 0, (Rh, tm)

    def body(c_ref, g_ref, r_ref, o_ref):
        o_ref[...] = g_ref[...] + r_ref[...]

    grid_spec = pltpu.PrefetchScalarGridSpec(
        num_scalar_prefetch=1, grid=(S, Rh // tm),
        in_specs=[pl.BlockSpec((None, None, tm, C), lambda s, i, c_ref: (c_ref[0], s, i, 0)),
                  pl.BlockSpec((None, tm, C), lambda s, i, c_ref: (s, i, 0))],
        out_specs=pl.BlockSpec((None, tm, C), lambda s, i, c_ref: (s, i, 0)))
    return _pc_prefetch(body, name=name, grid_spec=grid_spec, out_shape=jax.ShapeDtypeStruct((S, Rh, C), F32),
                        dims=("parallel", "parallel"))(c.reshape(1).astype(jnp.int32), gbuf, ra)


def _add_chips(p, rb, chip, *, name, tm=256):
    S, Rh, C = p.shape
    assert Rh % tm == 0, (Rh, tm)

    def body(s_ref, p_ref, r_ref, o_ref):
        o_ref[...] = ((p_ref[...] + r_ref[0]) + r_ref[1]) + r_ref[2]

    grid_spec = pltpu.PrefetchScalarGridSpec(
        num_scalar_prefetch=1, grid=(Rh // tm,),
        in_specs=[pl.BlockSpec((None, tm, C), lambda i, s_ref: (s_ref[0], i, 0)),
                  pl.BlockSpec((3, tm, C), lambda i, s_ref: (0, i, 0))],
        out_specs=pl.BlockSpec((tm, C), lambda i, s_ref: (i, 0)))
    return _pc_prefetch(body, name=name, grid_spec=grid_spec, out_shape=jax.ShapeDtypeStruct((Rh, C), F32),
                        dims=("parallel",))(chip.reshape(1).astype(jnp.int32), p, rb)


def _sum_slots(g, *, name):
    n, R, C = g.shape

    def body(g_ref, o_ref):
        acc = g_ref[0]
        for s in range(1, n):
            acc = acc + g_ref[s]
        o_ref[...] = acc

    return _pc(body, name=name, grid=(1,), in_specs=[pl.BlockSpec((n, R, C), lambda i: (0, 0, 0))],
               out_specs=pl.BlockSpec((R, C), lambda i: (0, 0)), out_shape=jax.ShapeDtypeStruct((R, C), F32),
               dims=("arbitrary",))(g)


ANY = pl.BlockSpec(memory_space=pl.ANY)


def _place():
    return lax.axis_index("x"), lax.axis_index("y"), lax.axis_index("c")


def _other_chips(x, y):
    return [(1 - x, y), (x, 1 - y), (1 - x, 1 - y)]


def _allgather_chips(w, *, name):
    _, Rh, C = w.shape

    def body(w_ref, out_ref, send_sems, recv_sems, local_sem):
        x, y, c = _place()
        sib = (x, y, 1 - c)
        chips = _other_chips(x, y)
        mine = pltpu.make_async_copy(w_ref, out_ref.at[2 * x + y], local_sem)
        mine.start()

        def copy(k, chip_id, half, to):
            return pltpu.make_async_remote_copy(src_ref=w_ref.at[half] if k < 3 else out_ref.at[chip_id, half],
                                                dst_ref=out_ref.at[chip_id, half], send_sem=send_sems.at[k],
                                                recv_sem=recv_sems.at[k], device_id=to, device_id_type=MESH)

        sends = [copy(j, 2 * x + y, c, (px, py, c)) for j, (px, py) in enumerate(chips)]
        for cp in sends:
            cp.start()
        passed = []
        for j, (px, py) in enumerate(chips):
            copy(j, 2 * px + py, c, (px, py, c)).wait_recv()
            fwd = copy(3 + j, 2 * px + py, c, sib)
            fwd.start()
            passed.append(fwd)
        for j, (px, py) in enumerate(chips):
            copy(3 + j, 2 * px + py, 1 - c, sib).wait_recv()
        for cp in sends + passed:
            cp.wait_send()
        mine.wait()

    return _pc_comm(body, name=name, in_specs=[ANY], out_specs=ANY,
                    out_shape=jax.ShapeDtypeStruct((N_CHIPS, 2, Rh, C), w.dtype),
                    scratch_shapes=[pltpu.SemaphoreType.DMA((6,)), pltpu.SemaphoreType.DMA((6,)),
                                    pltpu.SemaphoreType.DMA])(w)


def _send_half_to_sibling(gbuf, *, name):
    _, S, Rh, C = gbuf.shape

    def body(g_ref, ra_ref, send_sem, recv_sem):
        x, y, c = _place()
        cp = pltpu.make_async_remote_copy(src_ref=g_ref.at[1 - c], dst_ref=ra_ref, send_sem=send_sem,
                                          recv_sem=recv_sem, device_id=(x, y, 1 - c), device_id_type=MESH)
        cp.start()
        cp.wait()

    return _pc_comm(body, name=name, in_specs=[ANY], out_specs=ANY, out_shape=jax.ShapeDtypeStruct((S, Rh, C), F32),
                    scratch_shapes=[pltpu.SemaphoreType.DMA, pltpu.SemaphoreType.DMA])(gbuf)


def _scatter_to_chips(p, *, name):
    S, Rh, C = p.shape

    def body(p_ref, rb_ref, send_sems, recv_sems):
        x, y, c = _place()
        chips = _other_chips(x, y)

        def copy(j, shard, to):
            return pltpu.make_async_remote_copy(src_ref=p_ref.at[shard], dst_ref=rb_ref.at[j], send_sem=send_sems.at[j],
                                                recv_sem=recv_sems.at[j], device_id=to, device_id_type=MESH)

        sends = [copy(j, 2 * px + py, (px, py, c)) for j, (px, py) in enumerate(chips)]
        for cp in sends:
            cp.start()
        for cp in sends:
            cp.wait()

    return _pc_comm(body, name=name, in_specs=[ANY], out_specs=ANY, out_shape=jax.ShapeDtypeStruct((3, Rh, C), F32),
                    scratch_shapes=[pltpu.SemaphoreType.DMA((3,)), pltpu.SemaphoreType.DMA((3,))])(p)


def _exchange_halves(f, *, name):
    Rh, C = f.shape

    def body(f_ref, out_ref, send_sem, recv_sem, local_sem):
        x, y, c = _place()
        mine = pltpu.make_async_copy(f_ref, out_ref.at[c], local_sem)
        mine.start()
        to_sib = pltpu.make_async_remote_copy(src_ref=f_ref, dst_ref=out_ref.at[c], send_sem=send_sem,
                                              recv_sem=recv_sem, device_id=(x, y, 1 - c), device_id_type=MESH)
        to_sib.start()
        to_sib.wait_send()
        pltpu.make_async_remote_copy(src_ref=f_ref, dst_ref=out_ref.at[1 - c], send_sem=send_sem, recv_sem=recv_sem,
                                     device_id=(x, y, 1 - c), device_id_type=MESH).wait_recv()
        mine.wait()

    return _pc_comm(body, name=name, in_specs=[ANY], out_specs=ANY, out_shape=jax.ShapeDtypeStruct((2, Rh, C), F32),
                    scratch_shapes=[pltpu.SemaphoreType.DMA, pltpu.SemaphoreType.DMA, pltpu.SemaphoreType.DMA])(f)


def _allgather_devices(v, *, name):
    R, C = v.shape

    def body(v_ref, out_ref, send_sems, recv_sems):
        x, y, c = _place()
        me = 4 * x + 2 * y + c
        out_ref[me] = v_ref[...]
        peers = []
        for k in range(1, 8):
            fx, fy, fc = (k >> 2) & 1, (k >> 1) & 1, k & 1
            px = 1 - x if fx else x
            py = 1 - y if fy else y
            pcc = 1 - c if fc else c
            peers.append((px, py, pcc))
        sends = []
        for k, peer in enumerate(peers):
            cp = pltpu.make_async_remote_copy(src_ref=v_ref, dst_ref=out_ref.at[me], send_sem=send_sems.at[k],
                                              recv_sem=recv_sems.at[k], device_id=peer, device_id_type=MESH)
            cp.start()
            sends.append(cp)
        for k, (px, py, pcc) in enumerate(peers):
            pltpu.make_async_remote_copy(src_ref=v_ref, dst_ref=out_ref.at[4 * px + 2 * py + pcc],
                                         send_sem=send_sems.at[k], recv_sem=recv_sems.at[k], device_id=peers[k],
                                         device_id_type=MESH).wait_recv()
        for cp in sends:
            cp.wait_send()

    vm = pl.BlockSpec(memory_space=pltpu.VMEM)
    return _pc_comm(body, name=name, in_specs=[vm], out_specs=vm, out_shape=jax.ShapeDtypeStruct((8, R, C), F32),
                    scratch_shapes=[pltpu.SemaphoreType.DMA((7,)), pltpu.SemaphoreType.DMA((7,))])(v)


D_MODEL = 1024
SC_W = D_MODEL // 4
GDN_W = D_MODEL // 2
SB_W = D_MODEL - SC_W - GDN_W
D_FF = 256 * ((8 * D_MODEL // 3 + 255) // 256)
O_SC, O_GQKV, O_GZ, O_GA, O_GB, O_SB = 0, 3 * SC_W, 3 * SC_W + 3 * GDN_W, 3 * SC_W + 4 * GDN_W, \
    3 * SC_W + 4 * GDN_W + GDN_HEADS, 3 * SC_W + 4 * GDN_W + 2 * GDN_HEADS
D_IN_PROJ = O_SB + 3 * SB_W
P_GQKV, P_GZ, P_SC, P_SB, P_GAB = 0, 3 * GDN_W, 4 * GDN_W, 4 * GDN_W + 3 * SC_W, 4 * GDN_W + 3 * SC_W + 3 * SB_W
P_PAD = 256
P_WIDTH = P_GAB + P_PAD


def _proj_to_kernel_layout(w):
    pad = jnp.zeros((w.shape[0], P_PAD - 2 * GDN_HEADS), w.dtype)
    return jnp.concatenate([w[:, O_GQKV:O_GA], w[:, O_SC:O_GQKV], w[:, O_SB:], w[:, O_GA:O_SB], pad], axis=1)


def _proj_from_kernel_layout(g):
    return jnp.concatenate([g[:, P_SC:P_SB], g[:, P_GQKV:P_SC], g[:, P_GAB:P_GAB + 2 * GDN_HEADS], g[:, P_SB:P_GAB]],
                           axis=1)


def _mixout_to_kernel_layout(w):
    return jnp.concatenate([w[SC_W:SC_W + GDN_W], w[:SC_W], w[SC_W + GDN_W:]], axis=0)


def _mixout_from_kernel_layout(g):
    return jnp.concatenate([g[GDN_W:GDN_W + SC_W], g[:GDN_W], g[GDN_W + SC_W:]], axis=0)


def _pack_rows(parts, width, rows_to):
    flat = jnp.concatenate([p.reshape(-1, width) for p in parts], axis=0)
    return jnp.pad(flat, ((0, rows_to - flat.shape[0]), (0, 0)))


def _unpack_rows(flat, shapes, width):
    out, r = [], 0
    for shp in shapes:
        n = int(np.prod(shp)) // width
        out.append(flat[r:r + n].reshape(shp))
        r += n
    return out


def _pack_vec(parts, rows_to):
    flat = jnp.concatenate([p.reshape(-1) for p in parts])
    return jnp.pad(flat, (0, rows_to * LANES - flat.shape[0])).reshape(rows_to, LANES)


def _unpack_vec(mat, shapes):
    flat = mat.reshape(-1)
    out, r = [], 0
    for shp in shapes:
        n = int(np.prod(shp))
        out.append(flat[r:r + n].reshape(shp))
        r += n
    return out


def _round_up(n, m):
    return (n + m - 1) // m * m


def _heads_major(a, heads):
    L = a.shape[0]
    return a.reshape(L, heads, -1).transpose(1, 0, 2)


def _heads_minor(a):
    return a.transpose(1, 0, 2).reshape(a.shape[1], -1)


def _layer_fwd(x, p, l):
    L = x.shape[0]
    tag = "l%d_" % l
    h = _rmsnorm_fwd(x, p["wn_mix"], name=tag + "norm_mix")
    proj = _matmul(h, p["w_in"], "nn", tm=512, tn=768, tk=D_MODEL, name=tag + "proj")
    (y_sc,) = _conv_pointwise_fwd([(proj, P_SC + SC_W), (proj, P_SC + 2 * SC_W)], [(p["w_sconv"], 0)], [(proj, P_SC)],
                                  _pre_product, _post_gate_mul, [(SC_W, BF16)], tc=SC_W, tm=512, name=tag + "sconv")
    (qkv,) = _conv_pointwise_fwd([(proj, P_GQKV)], [(p["w_gdn_conv"], 0)], [], _pre_identity, _post_silu,
                                 [(3 * GDN_W, F32)], tc=GDN_W, tm=512, name=tag + "gdn_conv")
    ga = proj[:, P_GAB:P_GAB + GDN_HEADS].T.reshape(GDN_HEADS, L, 1)
    gb = proj[:, P_GAB + GDN_HEADS:P_GAB + 2 * GDN_HEADS].T.reshape(GDN_HEADS, L, 1)
    qe, ke, u, w, attn, eg = _gdn_prep_fwd(qkv, ga, gb, p["a_log"], p["dt_bias"], name=tag + "gdn_prep")
    y_gdn, states = _gdn_scan_fwd(qe, ke, u, w, attn, eg, proj, P_GZ, p["wgn"], name=tag + "gdn_scan")
    sb_scale = (SB_W // SB_HEADS) ** -0.5
    sq = _heads_major((proj[:, P_SB:P_SB + SB_W] * sb_scale).astype(BF16), SB_HEADS)
    sk = _heads_major(proj[:, P_SB + SB_W:P_SB + 2 * SB_W].astype(BF16), SB_HEADS)
    sv = _heads_major(proj[:, P_SB + 2 * SB_W:P_SB + 3 * SB_W].astype(BF16), SB_HEADS)
    o_sb, ctot = _sb_fwd(sq, sk, sv, name=tag + "sb_fwd")
    y_cat = jnp.concatenate([y_gdn.astype(BF16), y_sc, _heads_minor(o_sb).astype(BF16)], axis=1)
    x2 = _matmul(y_cat, p["w_out"], "nn", tm=512, tn=D_MODEL, tk=D_MODEL, res=x, name=tag + "mix_out")
    h2 = _rmsnorm_fwd(x2, p["wn_ffn"], name=tag + "norm_ffn")
    up_g = _matmul(h2, p["w_up_g"], "nn", tm=512, tn=D_FF // 2, tk=D_MODEL, name=tag + "up_gate")
    up_v = _matmul(h2, p["w_up_v"], "nn", tm=512, tn=D_FF // 2, tk=D_MODEL, name=tag + "up_val")
    (act,) = _conv_pointwise_fwd([(up_g, 0), (up_v, 0)], [(p["w_fconv_g"], 0), (p["w_fconv_v"], 0)], [],
                                 _pre_identity, _post_swiglu, [(D_FF, BF16)], tc=256, tm=512, name=tag + "ffn_act")
    x3 = _matmul(act, p["w_down"], "nn", tm=512, tn=D_MODEL, tk=D_FF // 2, res=x2, name=tag + "ffn_down")
    saved = dict(x=x, h=h, proj=proj, qkv=qkv, ga=ga, gb=gb, qe=qe, ke=ke, u=u, w=w, attn=attn, eg=eg, states=states,
                 sq=sq, sk=sk, sv=sv, ctot=ctot, y_cat=y_cat, x2=x2, h2=h2, up_g=up_g, up_v=up_v, act=act)
    return x3, saved


def _layer_bwd(dx3, p, s, l):
    L = dx3.shape[0]
    tag = "l%d_b_" % l
    g = {}
    dact = _matmul(dx3, p["w_down"], "nt", tm=512, tn=D_FF // 2, tk=D_MODEL, name=tag + "dact")
    g["w_down"] = _matmul(s["act"], dx3, "tn", tm=D_FF // 2, tn=D_MODEL, tk=512, name=tag + "dw_down")
    (dup_g, dup_v), _, (g["w_fconv_g"], g["w_fconv_v"]) = _conv_pointwise_bwd(
        [(s["up_g"], 0), (s["up_v"], 0)], [(p["w_fconv_g"], 0), (p["w_fconv_v"], 0)], [], [(dact, 0)],
        _pre_identity, _post_swiglu, D_FF, tc=256, tm=512, name=tag + "ffn_act")
    dh2 = _matmul(dup_g, p["w_up_g"], "nt", tm=512, tn=D_MODEL, tk=D_FF // 2, name=tag + "dh2_gate")
    dh2 = _matmul(dup_v, p["w_up_v"], "nt", tm=512, tn=D_MODEL, tk=D_FF // 2, res=dh2, name=tag + "dh2_val")
    g["w_up_g"] = _matmul(s["h2"], dup_g, "tn", tm=D_MODEL, tn=D_FF // 2, tk=512, name=tag + "dw_up_gate")
    g["w_up_v"] = _matmul(s["h2"], dup_v, "tn", tm=D_MODEL, tn=D_FF // 2, tk=512, name=tag + "dw_up_val")
    dx2, g["wn_ffn"] = _rmsnorm_bwd(dh2, s["x2"], p["wn_ffn"], dx3, name=tag + "norm_ffn")
    dycat = _matmul(dx2, p["w_out"], "nt", tm=512, tn=D_MODEL, tk=D_MODEL, name=tag + "dycat")
    g["w_out"] = _matmul(s["y_cat"], dx2, "tn", tm=D_MODEL, tn=D_MODEL, tk=512, name=tag + "dw_out")
    sb_scale = (SB_W // SB_HEADS) ** -0.5
    do_sb = _heads_major(dycat[:, GDN_W + SC_W:], SB_HEADS)
    dsq, dsk, dsv = _sb_bwd(s["sq"], s["sk"], s["sv"], s["ctot"], do_sb, sb_scale, name=tag + "sb_bwd")
    dqe, dke, du, dw, dattn, deg, dgz, g["wgn"] = _gdn_scan_bwd(
        s["qe"], s["ke"], s["u"], s["w"], s["attn"], s["eg"], s["proj"], P_GZ, p["wgn"], s["states"], dycat, 0,
        name=tag + "gdn_scan")
    dq, dk, dv, dga, dgb, g["a_log"], g["dt_bias"] = _gdn_prep_bwd(
        s["qkv"], s["ga"], s["gb"], p["a_log"], p["dt_bias"], dqe, dke, du, dw, dattn, deg, name=tag + "gdn_prep")
    dqkv_act = jnp.concatenate([dq, dk, dv], axis=1)
    (dqkv,), _, (g["w_gdn_conv"],) = _conv_pointwise_bwd(
        [(s["proj"], P_GQKV)], [(p["w_gdn_conv"], 0)], [], [(dqkv_act, 0)], _pre_identity, _post_silu, 3 * GDN_W,
        tc=GDN_W, tm=512, name=tag + "gdn_conv")
    (dsc_c, dsc_h), (dsc_b,), (g["w_sconv"],) = _conv_pointwise_bwd(
        [(s["proj"], P_SC + SC_W), (s["proj"], P_SC + 2 * SC_W)], [(p["w_sconv"], 0)], [(s["proj"], P_SC)],
        [(dycat, GDN_W)], _pre_product, _post_gate_mul, SC_W, tc=SC_W, tm=512, name=tag + "sconv")
    dgab = jnp.concatenate([dga.reshape(GDN_HEADS, L).T, dgb.reshape(GDN_HEADS, L).T,
                            jnp.zeros((L, P_PAD - 2 * GDN_HEADS), F32)], axis=1)
    dproj = jnp.concatenate([dqkv, dgz, dsc_b, dsc_c, dsc_h, _heads_minor(dsq), _heads_minor(dsk), _heads_minor(dsv),
                             dgab], axis=1)
    dh = _matmul(dproj, p["w_in"], "nt", tm=512, tn=D_MODEL, tk=768, name=tag + "dh")
    g["w_in"] = _matmul(s["h"], dproj, "tn", tm=D_MODEL, tn=768, tk=512, name=tag + "dw_in")
    dx, g["wn_mix"] = _rmsnorm_bwd(dh, s["x"], p["wn_mix"], dx2, name=tag + "norm_mix")
    return dx, g


BIG = ("w_mix_in", "w_mix_out", "w_ffn_up", "w_ffn_down")
BIG_AXIS = {"w_mix_in": 2, "w_mix_out": 1, "w_ffn_up": 2, "w_ffn_down": 1}
SMALL_SHARDED = ("w_sconv", "w_gdn_conv", "w_ffn_conv")
SMALL_REPLICATED = ("w_norm_mix", "gdn_a_log", "gdn_dt_bias", "w_gdn_norm", "w_norm_ffn", "w_norm_final")
WEIGHTS = ("w_norm_mix", "w_mix_in", "w_sconv", "w_gdn_conv", "gdn_a_log", "gdn_dt_bias", "w_gdn_norm", "w_mix_out",
           "w_norm_ffn", "w_ffn_up", "w_ffn_conv", "w_ffn_down", "w_norm_final")


def _big_parts(d):
    depth = d["w_mix_in"].shape[0]
    return [d[n][l] for l in range(depth) for n in BIG]


def kernel(x, w_norm_mix, w_mix_in, w_sconv, w_gdn_conv, gdn_a_log, gdn_dt_bias, w_gdn_norm, w_mix_out, w_norm_ffn, w_ffn_up, w_ffn_conv, w_ffn_down, w_norm_final, loss_target, m_w_norm_mix, m_w_mix_in, m_w_sconv, m_w_gdn_conv, m_gdn_a_log, m_gdn_dt_bias, m_w_gdn_norm, m_w_mix_out, m_w_norm_ffn, m_w_ffn_up, m_w_ffn_conv, m_w_ffn_down, m_w_norm_final, v_w_norm_mix, v_w_mix_in, v_w_sconv, v_w_gdn_conv, v_gdn_a_log, v_gdn_dt_bias, v_w_gdn_norm, v_w_mix_out, v_w_norm_ffn, v_w_ffn_up, v_w_ffn_conv, v_w_ffn_down, v_w_norm_final):
    W = dict(w_norm_mix=w_norm_mix, w_mix_in=w_mix_in, w_sconv=w_sconv, w_gdn_conv=w_gdn_conv, gdn_a_log=gdn_a_log,
             gdn_dt_bias=gdn_dt_bias, w_gdn_norm=w_gdn_norm, w_mix_out=w_mix_out, w_norm_ffn=w_norm_ffn,
             w_ffn_up=w_ffn_up, w_ffn_conv=w_ffn_conv, w_ffn_down=w_ffn_down, w_norm_final=w_norm_final)
    M = dict(w_norm_mix=m_w_norm_mix, w_mix_in=m_w_mix_in, w_sconv=m_w_sconv, w_gdn_conv=m_w_gdn_conv,
             gdn_a_log=m_gdn_a_log, gdn_dt_bias=m_gdn_dt_bias, w_gdn_norm=m_w_gdn_norm, w_mix_out=m_w_mix_out,
             w_norm_ffn=m_w_norm_ffn, w_ffn_up=m_w_ffn_up, w_ffn_conv=m_w_ffn_conv, w_ffn_down=m_w_ffn_down,
             w_norm_final=m_w_norm_final)
    V = dict(w_norm_mix=v_w_norm_mix, w_mix_in=v_w_mix_in, w_sconv=v_w_sconv, w_gdn_conv=v_w_gdn_conv,
             gdn_a_log=v_gdn_a_log, gdn_dt_bias=v_gdn_dt_bias, w_gdn_norm=v_w_gdn_norm, w_mix_out=v_w_mix_out,
             w_norm_ffn=v_w_norm_ffn, w_ffn_up=v_w_ffn_up, w_ffn_conv=v_w_ffn_conv, w_ffn_down=v_w_ffn_down,
             w_norm_final=v_w_norm_final)
    depth = w_mix_in.shape[0]
    L = x.shape[1]
    mx, my, mc = lax.axis_index("x"), lax.axis_index("y"), lax.axis_index("c")
    chip = 2 * mx + my

    big_shapes = [a.shape for a in _big_parts(W)]
    rows = sum(int(np.prod(s)) // D_MODEL for s in big_shapes)
    rows_pad = _round_up(rows, 512)
    rh = rows_pad // 2
    w_flat = _pack_rows(_big_parts(W), D_MODEL, rows_pad)
    gathered = _allgather_chips(w_flat.astype(BF16).reshape(2, rh, D_MODEL), name="gather_big")
    gathered = gathered.reshape(N_CHIPS, rows_pad, D_MODEL)
    per_chip = [_unpack_rows(gathered[b], big_shapes, D_MODEL) for b in range(N_CHIPS)]
    full_big = []
    for l in range(depth):
        lay = {}
        for n_i, n in enumerate(BIG):
            lay[n] = jnp.concatenate([per_chip[b][l * len(BIG) + n_i] for b in range(N_CHIPS)], axis=BIG_AXIS[n] - 1)
        full_big.append(lay)

    small_sh_shapes = [W[n].shape for n in SMALL_SHARDED]
    n_small_sh = sum(int(np.prod(s)) for s in small_sh_shapes)
    small_rows = _round_up(n_small_sh, 8 * LANES) // LANES
    small_all = _allgather_devices(_pack_vec([W[n] for n in SMALL_SHARDED], small_rows), name="gather_small")
    small_chip = [_unpack_vec(small_all[2 * b], small_sh_shapes) for b in range(N_CHIPS)]
    full_small = {n: jnp.concatenate([small_chip[b][i] for b in range(N_CHIPS)], axis=2)
                  for i, n in enumerate(SMALL_SHARDED)}

    params = []
    for l in range(depth):
        w_up = full_big[l]["w_ffn_up"]
        fconv = full_small["w_ffn_conv"][l]
        params.append(dict(
            wn_mix=w_norm_mix[l], w_in=_proj_to_kernel_layout(full_big[l]["w_mix_in"]),
            w_sconv=full_small["w_sconv"][l], w_gdn_conv=full_small["w_gdn_conv"][l],
            a_log=gdn_a_log[l].reshape(GDN_HEADS, 1, 1), dt_bias=gdn_dt_bias[l].reshape(GDN_HEADS, 1, 1),
            wgn=w_gdn_norm[l], w_out=_mixout_to_kernel_layout(full_big[l]["w_mix_out"]), wn_ffn=w_norm_ffn[l],
            w_up_g=w_up[:, :D_FF], w_up_v=w_up[:, D_FF:], w_fconv_g=fconv[:, :D_FF], w_fconv_v=fconv[:, D_FF:],
            w_down=full_big[l]["w_ffn_down"]))

    xs = x[0]
    saved = []
    for l in range(depth):
        xs, s = _layer_fwd(xs, params[l], l)
        saved.append(s)
    loss_row, dx, g_norm_final = _final_loss(xs, w_norm_final, loss_target[0], name="final_loss")
    grads = [None] * depth
    for l in reversed(range(depth)):
        dx, grads[l] = _layer_bwd(dx, params[l], saved[l], l)
    loss = lax.psum(loss_row[0, 0], ("x", "y", "c"))

    G = {
        "w_mix_in": jnp.stack([_proj_from_kernel_layout(grads[l]["w_in"]) for l in range(depth)]),
        "w_mix_out": jnp.stack([_mixout_from_kernel_layout(grads[l]["w_out"]) for l in range(depth)]),
        "w_ffn_up": jnp.stack([jnp.concatenate([grads[l]["w_up_g"], grads[l]["w_up_v"]], axis=1)
                               for l in range(depth)]),
        "w_ffn_down": jnp.stack([grads[l]["w_down"] for l in range(depth)]),
        "w_sconv": jnp.stack([grads[l]["w_sconv"] for l in range(depth)]),
        "w_gdn_conv": jnp.stack([grads[l]["w_gdn_conv"] for l in range(depth)]),
        "w_ffn_conv": jnp.stack([jnp.concatenate([grads[l]["w_fconv_g"], grads[l]["w_fconv_v"]], axis=1)
                                 for l in range(depth)]),
        "w_norm_mix": jnp.stack([grads[l]["wn_mix"].reshape(-1) for l in range(depth)]),
        "gdn_a_log": jnp.stack([grads[l]["a_log"].reshape(-1) for l in range(depth)]),
        "gdn_dt_bias": jnp.stack([grads[l]["dt_bias"].reshape(-1) for l in range(depth)]),
        "w_gdn_norm": jnp.stack([grads[l]["wgn"].reshape(-1) for l in range(depth)]),
        "w_norm_ffn": jnp.stack([grads[l]["wn_ffn"].reshape(-1) for l in range(depth)]),
        "w_norm_final": g_norm_final.reshape(-1),
    }

    def shard_of(a, n, b):
        width = a.shape[BIG_AXIS[n]] // N_CHIPS
        return lax.slice_in_dim(a, b * width, (b + 1) * width, axis=BIG_AXIS[n])

    gbuf = jnp.stack([_pack_rows(_big_parts({n: shard_of(G[n], n, b) for n in BIG}), D_MODEL, rows_pad)
                      for b in range(N_CHIPS)])
    gbuf = gbuf.reshape(N_CHIPS, 2, rh, D_MODEL).transpose(1, 0, 2, 3)
    from_sibling = _send_half_to_sibling(gbuf, name="rs_sibling")
    chip_sum = _add_halves(gbuf, from_sibling, mc, name="rs_add_halves")
    from_chips = _scatter_to_chips(chip_sum, name="rs_chips")
    my_half = _add_chips(chip_sum, from_chips, chip, name="rs_add_chips")
    g_flat = _exchange_halves(my_half, name="rs_result").reshape(rows_pad, D_MODEL)
    m_flat = _pack_rows(_big_parts(M), D_MODEL, rows_pad)
    v_flat = _pack_rows(_big_parts(V), D_MODEL, rows_pad)
    d_flat, nm_flat, nv_flat = _adamw(w_flat, g_flat, m_flat, v_flat, name="adamw_big", tm=256)
    out_g, out_d, out_m, out_v = {}, {}, {}, {}
    for flat, dst in ((g_flat, out_g), (d_flat, out_d), (nm_flat, out_m), (nv_flat, out_v)):
        parts = _unpack_rows(flat, big_shapes, D_MODEL)
        for n_i, n in enumerate(BIG):
            dst[n] = jnp.stack([parts[l * len(BIG) + n_i] for l in range(depth)])

    small_names = SMALL_SHARDED + SMALL_REPLICATED
    small_full_shapes = [G[n].shape for n in small_names]
    n_small = sum(int(np.prod(s)) for s in small_full_shapes)
    red_rows = _round_up(n_small, 8 * LANES) // LANES
    partials = _allgather_devices(_pack_vec([G[n] for n in small_names], red_rows), name="reduce_small")
    summed = _unpack_vec(_sum_slots(partials, name="reduce_small_sum"), small_full_shapes)
    g_small = {}
    for n, a in zip(small_names, summed):
        if n in SMALL_SHARDED:
            width = a.shape[2] // N_CHIPS
            a = lax.dynamic_slice_in_dim(a, chip * width, width, axis=2)
        g_small[n] = a
    own_shapes = [W[n].shape for n in small_names]
    n_own = sum(int(np.prod(s)) for s in own_shapes)
    own_rows = _round_up(n_own, 8 * LANES) // LANES
    packed = [_pack_vec([src[n] for n in small_names], own_rows) for src in (W, g_small, M, V)]
    d_s, nm_s, nv_s = _adamw(*packed, name="adamw_small", tm=own_rows)
    for mat, dst in ((packed[1], out_g), (d_s, out_d), (nm_s, out_m), (nv_s, out_v)):
        for n, a in zip(small_names, _unpack_vec(mat, own_shapes)):
            dst[n] = a

    outs = [loss, dx[None]]
    for dst in (out_g, out_d, out_m, out_v):
        outs += [dst[n] for n in WEIGHTS]
    return tuple(outs)
```

```python
import functools

import jax
import jax.numpy as jnp
import numpy as np
from jax import lax
from jax.experimental import pallas as pl
from jax.experimental.pallas import tpu as pltpu

F32 = jnp.float32
BF16 = jnp.bfloat16
MESH = pl.DeviceIdType.MESH

NORM_EPS = 1e-6
GDN_HEADS = 4
GDN_CHUNK = 64
GDN_CONV = 4
SB_HEADS = 4
SC_KERNEL = 3
FFN_CONV = 3
ADAM_LR = 0.001
ADAM_B1 = 0.9
ADAM_B2 = 0.999
ADAM_EPS = 1e-08
ADAM_WD = 0.01
ADAM_STEP = 10

VMEM_LIMIT_BYTES = 48 * 1024 * 1024
HALO = 8
LANES = 128
N_CHIPS = 4


def _pc(body, *, name, grid, in_specs, out_specs, out_shape, scratch_shapes=(), dims=None):
    params = dict(vmem_limit_bytes=VMEM_LIMIT_BYTES)
    if dims is not None:
        params["dimension_semantics"] = dims
    return pl.pallas_call(body, name=name, grid=grid, in_specs=in_specs, out_specs=out_specs, out_shape=out_shape,
                          scratch_shapes=list(scratch_shapes), compiler_params=pltpu.CompilerParams(**params))


def _pc_prefetch(body, *, name, grid_spec, out_shape, dims):
    return pl.pallas_call(body, name=name, grid_spec=grid_spec, out_shape=out_shape,
                          compiler_params=pltpu.CompilerParams(vmem_limit_bytes=VMEM_LIMIT_BYTES,
                                                               dimension_semantics=dims))


def _pc_comm(body, *, name, in_specs, out_specs, out_shape, scratch_shapes):
    return pl.pallas_call(body, name=name, in_specs=in_specs, out_specs=out_specs, out_shape=out_shape,
                          scratch_shapes=list(scratch_shapes),
                          compiler_params=pltpu.CompilerParams(vmem_limit_bytes=VMEM_LIMIT_BYTES))


_DIMS = {"nn": (((1,), (0,)), ((), ())), "nt": (((1,), (1,)), ((), ())), "tn": (((0,), (0,)), ((), ()))}


def _matmul(a, b, mode, *, name, tm=512, tn=512, tk=512, out_dtype=F32, res=None):
    a_parts = list(a) if isinstance(a, (list, tuple)) else [a]
    b_parts = list(b) if isinstance(b, (list, tuple)) else [b]
    assert len(a_parts) == 1 or len(b_parts) == 1
    a_rows, a_cols = a_parts[0].shape[0], sum(p.shape[1] for p in a_parts)
    b_rows, b_cols = b_parts[0].shape[0], sum(p.shape[1] for p in b_parts)
    if mode == "nn":
        M, K, K2, N = a_rows, a_cols, b_rows, b_cols
    elif mode == "nt":
        assert len(b_parts) == 1
        M, K, N, K2 = a_rows, a_cols, b_rows, b_cols
    else:
        K, M, K2, N = a_rows, a_cols, b_rows, b_cols
    assert K == K2, (mode, a_rows, a_cols, b_rows, b_cols)
    tm, tn, tk = min(tm, M), min(tn, N), min(tk, K)
    assert M % tm == 0 and N % tn == 0 and K % tk == 0, (M, N, K, tm, tn, tk)
    nk = K // tk
    a_tile = tm if mode == "tn" else tk
    a_axis = 0 if mode == "tn" else 2

    def ranges(parts, tile):
        out, start = [], 0
        for p in parts:
            assert p.shape[1] % tile == 0, (name, p.shape, tile)
            out.append((start, start + p.shape[1] // tile))
            start += p.shape[1] // tile
        return out

    a_rng = ranges(a_parts, a_tile)
    b_rng = ranges(b_parts, tk if mode == "nt" else tn)

    def a_spec(lo, hi):
        pick = lambda i, j, k: jnp.clip((i, j, k)[a_axis] - lo, 0, hi - lo - 1)
        if mode == "tn":
            return pl.BlockSpec((tk, tm), lambda i, j, k: (k, pick(i, j, k)))
        return pl.BlockSpec((tm, tk), lambda i, j, k: (i, pick(i, j, k)))

    def b_spec(lo, hi):
        if mode == "nt":
            return pl.BlockSpec((tn, tk), lambda i, j, k: (j, k))
        return pl.BlockSpec((tk, tn), lambda i, j, k: (k, jnp.clip(j - lo, 0, hi - lo - 1)))

    o_spec = pl.BlockSpec((tm, tn), lambda i, j, k: (i, j))
    has_res = res is not None
    dn = _DIMS[mode]
    na, nb = len(a_parts), len(b_parts)

    def body(*refs):
        a_refs, b_refs = refs[:na], refs[na:na + nb]
        r_ref = refs[na + nb] if has_res else None
        o_ref, acc = refs[-2], refs[-1]
        ids = (pl.program_id(0), pl.program_id(1), pl.program_id(2))
        k = ids[2]

        def finish(total):
            if has_res:
                total = total + r_ref[...].astype(F32)
            o_ref[...] = total.astype(out_dtype)

        def product(a_ref, b_ref):
            p = lax.dot_general(a_ref[...].astype(BF16), b_ref[...].astype(BF16), dn, preferred_element_type=F32)
            if nk == 1:
                finish(p)
            else:
                @pl.when(k == 0)
                def _():
                    acc[...] = p

                @pl.when(k > 0)
                def _():
                    acc[...] += p

        for a_ref, (alo, ahi) in zip(a_refs, a_rng):
            for b_ref, (blo, bhi) in zip(b_refs, b_rng):
                if na == 1 and nb == 1:
                    product(a_ref, b_ref)
                elif na > 1:
                    pl.when(jnp.logical_and(ids[a_axis] >= alo, ids[a_axis] < ahi))(
                        functools.partial(product, a_ref, b_ref))
                else:
                    pl.when(jnp.logical_and(ids[1] >= blo, ids[1] < bhi))(functools.partial(product, a_ref, b_ref))
        if nk > 1:
            @pl.when(k == nk - 1)
            def _():
                finish(acc[...])

    in_specs = [a_spec(lo, hi) for lo, hi in a_rng] + [b_spec(lo, hi) for lo, hi in b_rng] + \
        ([o_spec] if has_res else [])
    args = tuple(a_parts) + tuple(b_parts) + ((res,) if has_res else ())
    return _pc(body, name=name, grid=(M // tm, N // tn, nk), in_specs=in_specs, out_specs=o_spec,
               out_shape=jax.ShapeDtypeStruct((M, N), out_dtype), scratch_shapes=[pltpu.VMEM((tm, tn), F32)],
               dims=("parallel", "parallel", "arbitrary"))(*args)


def _rmsnorm_fwd(x, w, *, name, tm=512):
    L, D = x.shape
    tm = min(tm, L)

    def body(x_ref, w_ref, h_ref):
        xv = x_ref[...]
        r = lax.rsqrt(jnp.mean(xv * xv, axis=-1, keepdims=True) + NORM_EPS)
        h_ref[...] = (xv * r * w_ref[...]).astype(BF16)

    return _pc(body, name=name, grid=(L // tm,),
               in_specs=[pl.BlockSpec((tm, D), lambda i: (i, 0)), pl.BlockSpec((1, D), lambda i: (0, 0))],
               out_specs=pl.BlockSpec((tm, D), lambda i: (i, 0)), out_shape=jax.ShapeDtypeStruct((L, D), BF16),
               dims=("parallel",))(x, w.reshape(1, D))


def _rmsnorm_bwd(dh, x, w, dres, *, name, tm=512):
    L, D = x.shape
    tm = min(tm, L)

    def body(dh_ref, x_ref, w_ref, dres_ref, dx_ref, dw_ref):
        xv = x_ref[...]
        r = lax.rsqrt(jnp.mean(xv * xv, axis=-1, keepdims=True) + NORM_EPS)
        xhat = xv * r
        dhv = dh_ref[...]
        g = dhv * w_ref[...]
        dx_ref[...] = dres_ref[...] + r * (g - xhat * jnp.mean(g * xhat, axis=-1, keepdims=True))
        part = jnp.sum(dhv * xhat, axis=0, keepdims=True)

        @pl.when(pl.program_id(0) == 0)
        def _():
            dw_ref[...] = part

        @pl.when(pl.program_id(0) > 0)
        def _():
            dw_ref[...] += part

    row = pl.BlockSpec((tm, D), lambda i: (i, 0))
    vec = pl.BlockSpec((1, D), lambda i: (0, 0))
    return _pc(body, name=name, grid=(L // tm,), in_specs=[row, row, vec, row], out_specs=[row, vec],
               out_shape=[jax.ShapeDtypeStruct((L, D), F32), jax.ShapeDtypeStruct((1, D), F32)],
               dims=("arbitrary",))(dh, x, w.reshape(1, D), dres)


def _final_loss(x, w, tgt, *, name, tm=512):
    L, D = x.shape
    tm = min(tm, L)

    def body(x_ref, w_ref, t_ref, loss_ref, dx_ref, dw_ref):
        xv = x_ref[...]
        r = lax.rsqrt(jnp.mean(xv * xv, axis=-1, keepdims=True) + NORM_EPS)
        xhat = xv * r
        e = xhat * w_ref[...] - t_ref[...]
        lpart = jnp.broadcast_to(0.5 * jnp.sum(jnp.mean(e * e, axis=-1, keepdims=True), axis=0, keepdims=True),
                                 (1, LANES))
        dy = e * (1.0 / D)
        g = dy * w_ref[...]
        dx_ref[...] = r * (g - xhat * jnp.mean(g * xhat, axis=-1, keepdims=True))
        part = jnp.sum(dy * xhat, axis=0, keepdims=True)

        @pl.when(pl.program_id(0) == 0)
        def _():
            dw_ref[...] = part
            loss_ref[...] = lpart

        @pl.when(pl.program_id(0) > 0)
        def _():
            dw_ref[...] += part
            loss_ref[...] += lpart

    row = pl.BlockSpec((tm, D), lambda i: (i, 0))
    vec = pl.BlockSpec((1, D), lambda i: (0, 0))
    lsp = pl.BlockSpec((1, LANES), lambda i: (0, 0))
    return _pc(body, name=name, grid=(L // tm,), in_specs=[row, vec, row], out_specs=[lsp, row, vec],
               out_shape=[jax.ShapeDtypeStruct((1, LANES), F32), jax.ShapeDtypeStruct((L, D), F32),
                          jax.ShapeDtypeStruct((1, D), F32)],
               dims=("arbitrary",))(x, w.reshape(1, D), tgt)


def _shift_down(x, prev, k):
    if k == 0:
        return x
    r = pltpu.roll(x, k, 0)
    p = pltpu.roll(prev, k, 0)
    row = lax.broadcasted_iota(jnp.int32, p.shape, 0)
    head = jnp.where(row < k, p, r[:HALO])
    return jnp.concatenate([head, r[HALO:]], axis=0)


def _shift_up(x, j):
    if j == 0:
        return x
    return pltpu.roll(x, x.shape[0] - j, 0)


def _silu(x):
    return x * jax.nn.sigmoid(x)


def _conv_taps(p, p_prev, w):
    K = w.shape[0]
    out = None
    for k in range(K):
        term = w[k:k + 1, :] * _shift_down(p, p_prev, K - 1 - k)
        out = term if out is None else out + term
    return out


def _conv_pointwise_fwd(xs, ws, es, pre, post, outs, *, tc, tm, name):
    L = xs[0][0].shape[0]
    tm = min(tm, L)
    ncol = outs[0][0] // tc
    nrow = L // tm
    hb = tm // HALO
    nx, nw, ne, no = len(xs), len(ws), len(es), len(outs)
    K = ws[0][0].shape[0]

    def body(*refs):
        xc = [refs[2 * n][...] for n in range(nx)]
        i = pl.program_id(1)
        first = (i > 0).astype(F32)
        xp = [refs[2 * n + 1][...] * first for n in range(nx)]
        wv = [refs[2 * nx + n][...] for n in range(nw)]
        ev = [refs[2 * nx + nw + n][...] for n in range(ne)]
        o_refs = refs[2 * nx + nw + ne:]
        ps, pps = pre(*xc), pre(*xp)
        us = [_conv_taps(p, pp, w) for p, pp, w in zip(ps, pps, wv)]
        for o_ref, val in zip(o_refs, post(us, ev)):
            o_ref[...] = val.astype(o_ref.dtype)

    in_specs, args = [], []
    for arr, c0 in xs:
        off = c0 // tc
        in_specs.append(pl.BlockSpec((tm, tc), lambda j, i, off=off: (i, j + off)))
        in_specs.append(pl.BlockSpec((HALO, tc), lambda j, i, off=off: (jnp.maximum(i * hb - 1, 0), j + off)))
        args += [arr, arr]
    for arr, c0 in ws:
        off = c0 // tc
        in_specs.append(pl.BlockSpec((K, tc), lambda j, i, off=off: (0, j + off)))
        args.append(arr)
    for arr, c0 in es:
        off = c0 // tc
        in_specs.append(pl.BlockSpec((tm, tc), lambda j, i, off=off: (i, j + off)))
        args.append(arr)
    out_specs = [pl.BlockSpec((tm, tc), lambda j, i: (i, j)) for _ in range(no)]
    out_shape = [jax.ShapeDtypeStruct((L, c), dt) for c, dt in outs]
    return _pc(body, name=name, grid=(ncol, nrow), in_specs=in_specs, out_specs=out_specs, out_shape=out_shape,
               dims=("parallel", "parallel"))(*args)


def _conv_pointwise_bwd(xs, ws, es, dys, pre, post, width, *, tc, tm, name, out_dtype=BF16):
    L = xs[0][0].shape[0]
    tm = min(tm, L)
    ncol = width // tc
    nrow = L // tm
    hb = tm // HALO
    nx, nw, ne, ny = len(xs), len(ws), len(es), len(dys)
    K = ws[0][0].shape[0]

    def body(*refs):
        i = pl.program_id(1)
        first = (i > 0).astype(F32)
        more = (i < nrow - 1).astype(F32)
        pos = 0
        xc, xp, xe = [], [], []
        for n in range(nx):
            cur, prv, nxt = refs[pos][...], refs[pos + 1][...], refs[pos + 2][...]
            pos += 3
            xc.append(cur)
            xp.append(prv * first)
            xe.append(jnp.concatenate([cur, nxt], axis=0))
        wv = [refs[pos + n][...] for n in range(nw)]
        pos += nw
        ee = []
        for n in range(ne):
            ee.append(jnp.concatenate([refs[pos][...], refs[pos + 1][...]], axis=0))
            pos += 2
        dye = []
        for n in range(ny):
            dye.append(jnp.concatenate([refs[pos][...].astype(F32), refs[pos + 1][...].astype(F32) * more], axis=0))
            pos += 2
        dx_refs = refs[pos:pos + nx]
        de_refs = refs[pos + nx:pos + nx + ne]
        dw_refs = refs[pos + nx + ne:pos + nx + ne + nw]

        ps, pps = pre(*xe), pre(*xp)
        shifted = [[_shift_down(p, pp, K - 1 - k) for k in range(K)] for p, pp in zip(ps, pps)]
        us = []
        for n in range(nw):
            u = None
            for k in range(K):
                term = wv[n][k:k + 1, :] * shifted[n][k]
                u = term if u is None else u + term
            us.append(u)
        _, post_vjp = jax.vjp(lambda u_, e_: post(u_, e_), us, ee)
        dus, des = post_vjp(dye)
        dps = []
        for n in range(nw):
            dp = None
            for k in range(K):
                term = wv[n][k:k + 1, :] * _shift_up(dus[n], K - 1 - k)[:tm]
                dp = term if dp is None else dp + term
            dps.append(dp)
            for k in range(K):
                part = jnp.sum(dus[n][:tm] * shifted[n][k][:tm], axis=0, keepdims=True)

                @pl.when(i == 0)
                def _(part=part, n=n, k=k):
                    dw_refs[n][k:k + 1, :] = part

                @pl.when(i > 0)
                def _(part=part, n=n, k=k):
                    dw_refs[n][k:k + 1, :] += part
        _, pre_vjp = jax.vjp(lambda *x_: pre(*x_), *xc)
        dxs = pre_vjp(dps)
        for r, v in zip(dx_refs, dxs):
            r[...] = v.astype(out_dtype)
        for r, v in zip(de_refs, des):
            r[...] = v[:tm].astype(out_dtype)

    in_specs, args = [], []

    def add_rows(arr, c0, prev, nxt):
        off = c0 // tc
        in_specs.append(pl.BlockSpec((tm, tc), lambda j, i, off=off: (i, j + off)))
        args.append(arr)
        if prev:
            in_specs.append(pl.BlockSpec((HALO, tc), lambda j, i, off=off: (jnp.maximum(i * hb - 1, 0), j + off)))
            args.append(arr)
        if nxt:
            last = L // HALO - 1
            in_specs.append(pl.BlockSpec((HALO, tc), lambda j, i, off=off: (jnp.minimum((i + 1) * hb, last), j + off)))
            args.append(arr)

    for arr, c0 in xs:
        add_rows(arr, c0, True, True)
    for arr, c0 in ws:
        off = c0 // tc
        in_specs.append(pl.BlockSpec((K, tc), lambda j, i, off=off: (0, j + off)))
        args.append(arr)
    for arr, c0 in es:
        add_rows(arr, c0, False, True)
    for arr, c0 in dys:
        add_rows(arr, c0, False, True)
    tile = pl.BlockSpec((tm, tc), lambda j, i: (i, j))
    wtile = pl.BlockSpec((K, tc), lambda j, i: (0, j))
    out_specs = [tile] * (nx + ne) + [wtile] * nw
    out_shape = [jax.ShapeDtypeStruct((L, width), out_dtype)] * (nx + ne) + \
        [jax.ShapeDtypeStruct((K, width), F32)] * nw
    res = _pc(body, name=name, grid=(ncol, nrow), in_specs=in_specs, out_specs=out_specs, out_shape=out_shape,
              dims=("parallel", "arbitrary"))(*args)
    return res[:nx], res[nx:nx + ne], res[nx + ne:]


def _pre_identity(*x):
    return list(x)


def _pre_product(c, h):
    return [c * h]


def _post_silu(us, es):
    return [_silu(us[0])]


def _post_gate_mul(us, es):
    return [es[0] * us[0]]


def _post_swiglu(us, es):
    return [_silu(us[0]) * us[1]]


def _make_dot(passes):
    def raw(a, b, dn):
        a_hi = a.astype(BF16)
        b_hi = b.astype(BF16)
        out = lax.dot_general(a_hi, b_hi, dn, preferred_element_type=F32)
        if passes == 3:
            a_lo = (a - a_hi.astype(F32)).astype(BF16)
            b_lo = (b - b_hi.astype(F32)).astype(BF16)
            out = out + lax.dot_general(a_hi, b_lo, dn, preferred_element_type=F32)
            out = out + lax.dot_general(a_lo, b_hi, dn, preferred_element_type=F32)
        return out

    @jax.custom_vjp
    def nn(a, b):
        return raw(a, b, _DIMS["nn"])

    @jax.custom_vjp
    def nt(a, b):
        return raw(a, b, _DIMS["nt"])

    @jax.custom_vjp
    def tn(a, b):
        return raw(a, b, _DIMS["tn"])

    nn.defvjp(lambda a, b: (nn(a, b), (a, b)), lambda r, g: (nt(g, r[1]), tn(r[0], g)))
    nt.defvjp(lambda a, b: (nt(a, b), (a, b)), lambda r, g: (nn(g, r[1]), tn(g, r[0])))
    tn.defvjp(lambda a, b: (tn(a, b), (a, b)), lambda r, g: (nt(r[1], g), nn(r[0], g)))
    return nn, nt, tn


_NN1, _NT1, _TN1 = _make_dot(1)
_NN3, _NT3, _TN3 = _make_dot(3)


def _l2norm(x):
    return x * lax.rsqrt(jnp.sum(x * x, axis=-1, keepdims=True) + NORM_EPS)


def _unit_lower_inverse_raw(a):
    C = a.shape[0]
    ii = lax.broadcasted_iota(jnp.int32, (C, C), 0)
    jj = lax.broadcasted_iota(jnp.int32, (C, C), 1)
    t = jnp.where(ii == jj, 1.0, 0.0) - a
    p = a
    n = 2
    while n < C:
        p = _NN3(p, p)
        t = t + _NN3(t, p)
        n *= 2
    return t


@jax.custom_vjp
def _unit_lower_inverse(a):
    return _unit_lower_inverse_raw(a)


def _unit_lower_inverse_fwd(a):
    t = _unit_lower_inverse_raw(a)
    return t, t


def _unit_lower_inverse_bwd(t, g):
    return (-_NT3(_TN3(t, g), t),)


_unit_lower_inverse.defvjp(_unit_lower_inverse_fwd, _unit_lower_inverse_bwd)


def _gdn_prep(qr, kr, v, ga, gb, a_log, dt_bias):
    C, Dh = qr.shape
    q = _l2norm(qr) * (Dh ** -0.5)
    k = _l2norm(kr)
    beta = jax.nn.sigmoid(gb)
    g = -jnp.exp(a_log) * jax.nn.softplus(ga + dt_bias)
    ii = lax.broadcasted_iota(jnp.int32, (C, C), 0)
    jj = lax.broadcasted_iota(jnp.int32, (C, C), 1)
    gc_row = jnp.sum(jnp.where(ii <= jj, g, 0.0), axis=0, keepdims=True)
    gc_col = jnp.sum(jnp.where(ii == jj, gc_row, 0.0), axis=1, keepdims=True)
    causal = ii >= jj
    strict = ii > jj
    decay = jnp.where(causal, jnp.exp(jnp.where(causal, gc_col - gc_row, 0.0)), 0.0)
    kb = k * beta
    a = jnp.where(strict, _NT1(kb, k) * decay, 0.0)
    t = _unit_lower_inverse(a)
    eg_col = jnp.exp(gc_col)
    uw = _NN3(t, jnp.concatenate([v * beta, kb * eg_col], axis=1))
    u, w = uw[:, :Dh], uw[:, Dh:]
    attn = jnp.where(causal, _NT1(q, k) * decay, 0.0)
    qe = q * eg_col
    lane = lax.broadcasted_iota(jnp.int32, (1, C), 1)
    g_last = jnp.sum(jnp.where(lane == C - 1, gc_row, 0.0), axis=1, keepdims=True)
    ke = k * jnp.exp(g_last - gc_col)
    eg = jnp.broadcast_to(jnp.exp(g_last), (1, Dh))
    return qe, ke, u, w, attn, eg


def _gdn_step(S, qe, ke, u, w, attn, eg, gz, wgn):
    v_new = u - _NN1(w, S)
    o = _NN1(qe, S) + _NN1(attn, v_new)
    S_new = S * eg + _TN1(ke, v_new)
    y = o * lax.rsqrt(jnp.mean(o * o, axis=-1, keepdims=True) + NORM_EPS) * wgn * _silu(gz)
    return y, S_new


def _gdn_prep_fwd(qkv, gab, gab_col, a_log, dt_bias, *, name, chunks=4):
    L = qkv.shape[0]
    H, C = GDN_HEADS, GDN_CHUNK
    W = qkv.shape[1] // 3
    Dh = W // H
    N = L // C
    chunks = min(chunks, N)
    R = chunks * C
    gab_off = gab_col // LANES

    def body(q_ref, k_ref, v_ref, gab_ref, al_ref, dt_ref, qe_ref, ke_ref, u_ref, w_ref, at_ref, eg_ref):
        for cc in range(chunks):
            rows = slice(cc * C, (cc + 1) * C)
            for h in range(H):
                sl = slice(h * Dh, (h + 1) * Dh)
                qe, ke, u, w, attn, eg = _gdn_prep(q_ref[rows, sl], k_ref[rows, sl], v_ref[rows, sl],
                                                   gab_ref[rows, h:h + 1], gab_ref[rows, H + h:H + h + 1],
                                                   al_ref[h], dt_ref[h])
                qe_ref[rows, sl] = qe
                ke_ref[rows, sl] = ke
                u_ref[rows, sl] = u
                w_ref[rows, sl] = w
                at_ref[h, rows, :] = attn
                eg_ref[cc, h:h + 1, :] = eg

    col = lambda c: pl.BlockSpec((R, W), lambda n, c=c: (n, c))
    tok = pl.BlockSpec((R, LANES), lambda n: (n, gab_off))
    par = pl.BlockSpec((H, 1, 1), lambda n: (0, 0, 0))
    wide = pl.BlockSpec((R, W), lambda n: (n, 0))
    return _pc(body, name=name, grid=(N // chunks,), in_specs=[col(0), col(1), col(2), tok, par, par],
               out_specs=[wide, wide, wide, wide, pl.BlockSpec((H, R, C), lambda n: (0, n, 0)),
                          pl.BlockSpec((chunks, H, Dh), lambda n: (n, 0, 0))],
               out_shape=[jax.ShapeDtypeStruct((L, W), F32)] * 4 + [jax.ShapeDtypeStruct((H, L, C), F32),
                                                                   jax.ShapeDtypeStruct((N, H, Dh), F32)],
               dims=("parallel",))(qkv, qkv, qkv, gab, a_log, dt_bias)


def _gdn_prep_bwd(qkv, gab, gab_col, a_log, dt_bias, dqe, dke, du, dw, dattn, deg, gab_width, *, name, chunks=2):
    L = qkv.shape[0]
    H, C = GDN_HEADS, GDN_CHUNK
    W = qkv.shape[1] // 3
    Dh = W // H
    N = L // C
    chunks = min(chunks, N)
    R = chunks * C
    gab_off = gab_col // LANES

    def body(q_ref, k_ref, v_ref, gab_ref, al_ref, dt_ref, dqe_ref, dke_ref, du_ref, dw_ref, dat_ref, deg_ref,
             dqkv_ref, dgab_ref, dal_ref, ddt_ref):
        first = pl.program_id(0) == 0
        lane = lax.broadcasted_iota(jnp.int32, (C, gab_width), 1)
        dal_sum, ddt_sum = [None] * H, [None] * H
        for cc in range(chunks):
            rows = slice(cc * C, (cc + 1) * C)
            dgab = jnp.zeros((C, gab_width), F32)
            for h in range(H):
                sl = slice(h * Dh, (h + 1) * Dh)
                _, vjp = jax.vjp(_gdn_prep, q_ref[rows, sl], k_ref[rows, sl], v_ref[rows, sl], gab_ref[rows, h:h + 1],
                                 gab_ref[rows, H + h:H + h + 1], al_ref[h], dt_ref[h])
                dq, dk, dv, dga, dgb, dal, ddt = vjp((dqe_ref[rows, sl], dke_ref[rows, sl], du_ref[rows, sl],
                                                      dw_ref[rows, sl], dat_ref[h, rows, :], deg_ref[cc, h:h + 1, :]))
                dqkv_ref[rows, h * Dh:(h + 1) * Dh] = dq
                dqkv_ref[rows, W + h * Dh:W + (h + 1) * Dh] = dk
                dqkv_ref[rows, 2 * W + h * Dh:2 * W + (h + 1) * Dh] = dv
                dgab = dgab + jnp.where(lane == h, dga, 0.0) + jnp.where(lane == H + h, dgb, 0.0)
                dal_sum[h] = dal if dal_sum[h] is None else dal_sum[h] + dal
                ddt_sum[h] = ddt if ddt_sum[h] is None else ddt_sum[h] + ddt
            dgab_ref[rows, :] = dgab.astype(BF16)

        @pl.when(first)
        def _():
            for h in range(H):
                dal_ref[h] = dal_sum[h]
                ddt_ref[h] = ddt_sum[h]

        @pl.when(jnp.logical_not(first))
        def _():
            for h in range(H):
                dal_ref[h] += dal_sum[h]
                ddt_ref[h] += ddt_sum[h]

    col = lambda c: pl.BlockSpec((R, W), lambda n, c=c: (n, c))
    tok = pl.BlockSpec((R, LANES), lambda n: (n, gab_off))
    par = pl.BlockSpec((H, 1, 1), lambda n: (0, 0, 0))
    wide = pl.BlockSpec((R, W), lambda n: (n, 0))
    att = pl.BlockSpec((H, R, C), lambda n: (0, n, 0))
    egs = pl.BlockSpec((chunks, H, Dh), lambda n: (n, 0, 0))
    return _pc(body, name=name, grid=(N // chunks,),
               in_specs=[col(0), col(1), col(2), tok, par, par, wide, wide, wide, wide, att, egs],
               out_specs=[pl.BlockSpec((R, 3 * W), lambda n: (n, 0)), pl.BlockSpec((R, gab_width), lambda n: (n, 0)),
                          par, par],
               out_shape=[jax.ShapeDtypeStruct((L, 3 * W), F32), jax.ShapeDtypeStruct((L, gab_width), BF16)]
               + [jax.ShapeDtypeStruct((H, 1, 1), F32)] * 2,
               dims=("arbitrary",))(qkv, qkv, qkv, gab, a_log, dt_bias, dqe, dke, du, dw, dattn, deg)


def _gdn_scan_fwd(qe, ke, u, w, attn, eg, gz, gz_col, wgn, *, name):
    L, W = qe.shape
    H, C = GDN_HEADS, GDN_CHUNK
    Dh = W // H
    N = L // C
    gz_off = gz_col // W

    def body(qe_ref, ke_ref, u_ref, w_ref, at_ref, eg_ref, gz_ref, wgn_ref, y_ref, st_ref, s_scr):
        @pl.when(pl.program_id(0) == 0)
        def _():
            s_scr[...] = jnp.zeros_like(s_scr)

        for h in range(H):
            sl = slice(h * Dh, (h + 1) * Dh)
            S = s_scr[h]
            st_ref[0, h] = S
            y, S_new = _gdn_step(S, qe_ref[:, sl], ke_ref[:, sl], u_ref[:, sl], w_ref[:, sl], at_ref[h],
                                 eg_ref[0, h:h + 1, :], gz_ref[:, sl], wgn_ref[...])
            y_ref[:, sl] = y.astype(BF16)
            s_scr[h] = S_new

    wide = pl.BlockSpec((C, W), lambda n: (n, 0))
    return _pc(body, name=name, grid=(N,),
               in_specs=[wide, wide, wide, wide, pl.BlockSpec((H, C, C), lambda n: (0, n, 0)),
                         pl.BlockSpec((1, H, Dh), lambda n: (n, 0, 0)),
                         pl.BlockSpec((C, W), lambda n: (n, gz_off)), pl.BlockSpec((1, Dh), lambda n: (0, 0))],
               out_specs=[wide, pl.BlockSpec((1, H, Dh, Dh), lambda n: (n, 0, 0, 0))],
               out_shape=[jax.ShapeDtypeStruct((L, W), BF16), jax.ShapeDtypeStruct((N, H, Dh, Dh), F32)],
               scratch_shapes=[pltpu.VMEM((H, Dh, Dh), F32)],
               dims=("arbitrary",))(qe, ke, u, w, attn, eg, gz, wgn.reshape(1, Dh))


def _gdn_scan_bwd(qe, ke, u, w, attn, eg, gz, gz_col, wgn, states, dy, dy_col, *, name):
    L, W = qe.shape
    H, C = GDN_HEADS, GDN_CHUNK
    Dh = W // H
    N = L // C
    gz_off = gz_col // W
    dy_off = dy_col // W

    def body(qe_ref, ke_ref, u_ref, w_ref, at_ref, eg_ref, gz_ref, wgn_ref, st_ref, dy_ref,
             dqe_ref, dke_ref, du_ref, dw_ref, dat_ref, deg_ref, dgz_ref, dwgn_ref, ds_scr):
        first = pl.program_id(0) == 0

        @pl.when(first)
        def _():
            ds_scr[...] = jnp.zeros_like(ds_scr)

        dwgn = None
        for h in range(H):
            sl = slice(h * Dh, (h + 1) * Dh)
            _, vjp = jax.vjp(_gdn_step, st_ref[0, h], qe_ref[:, sl], ke_ref[:, sl], u_ref[:, sl], w_ref[:, sl],
                             at_ref[h], eg_ref[0, h:h + 1, :], gz_ref[:, sl], wgn_ref[...])
            dS, dqe, dke, du, dw, dat, deg, dgz, dwg = vjp((dy_ref[:, sl], ds_scr[h]))
            ds_scr[h] = dS
            dqe_ref[:, sl] = dqe
            dke_ref[:, sl] = dke
            du_ref[:, sl] = du
            dw_ref[:, sl] = dw
            dat_ref[h] = dat
            deg_ref[0, h:h + 1, :] = deg
            dgz_ref[:, sl] = dgz.astype(BF16)
            dwgn = dwg if dwgn is None else dwgn + dwg

        @pl.when(first)
        def _():
            dwgn_ref[...] = dwgn

        @pl.when(jnp.logical_not(first))
        def _():
            dwgn_ref[...] += dwgn

    rev = lambda n: N - 1 - n
    wide = pl.BlockSpec((C, W), lambda n: (rev(n), 0))
    att = pl.BlockSpec((H, C, C), lambda n: (0, rev(n), 0))
    egs = pl.BlockSpec((1, H, Dh), lambda n: (rev(n), 0, 0))
    vec = pl.BlockSpec((1, Dh), lambda n: (0, 0))
    return _pc(body, name=name, grid=(N,),
               in_specs=[wide, wide, wide, wide, att, egs, pl.BlockSpec((C, W), lambda n: (rev(n), gz_off)), vec,
                         pl.BlockSpec((1, H, Dh, Dh), lambda n: (rev(n), 0, 0, 0)),
                         pl.BlockSpec((C, W), lambda n: (rev(n), dy_off))],
               out_specs=[wide, wide, wide, wide, att, egs, wide, vec],
               out_shape=[jax.ShapeDtypeStruct((L, W), F32)] * 4 + [jax.ShapeDtypeStruct((H, L, C), F32),
                                                                   jax.ShapeDtypeStruct((N, H, Dh), F32),
                                                                   jax.ShapeDtypeStruct((L, W), BF16),
                                                                   jax.ShapeDtypeStruct((1, Dh), F32)],
               scratch_shapes=[pltpu.VMEM((H, Dh, Dh), F32)],
               dims=("arbitrary",))(qe, ke, u, w, attn, eg, gz, wgn.reshape(1, Dh), states, dy)


def _sb_scores(q, kj, mask):
    z = lax.dot_general(q, kj, _DIMS["nt"], preferred_element_type=F32)
    sp = jnp.maximum(z, 0.0) + jnp.log(1.0 + jnp.exp(-jnp.abs(z)))
    lom = -sp if mask is None else jnp.where(mask, -sp, 0.0)
    return lom, z - sp


def _sb_masks(tq, width, dh):
    rr = lax.broadcasted_iota(jnp.int32, (tq, tq), 0)
    cc = lax.broadcasted_iota(jnp.int32, (tq, tq), 1)
    first_head = lax.broadcasted_iota(jnp.int32, (tq, width), 1) < dh
    return cc < rr, jnp.where(rr > cc, 1.0, 0.0).astype(BF16), first_head


def _sb_fwd(qkv, *, name, tq=256):
    L = qkv.shape[0]
    H = SB_HEADS
    width = 2 * (qkv.shape[1] // 3 // H)
    dh = width // 2
    npair = H // 2
    tq = min(tq, L)
    nq = L // tq

    def body(q_ref, k_ref, v_ref, o_ref, c_ref):
        i = pl.program_id(1)
        diag, tri, first_head = _sb_masks(tq, width, dh)
        qp = q_ref[...]
        zero = jnp.zeros_like(qp)
        qs = (jnp.where(first_head, qp, zero), jnp.where(first_head, zero, qp))

        def block(j, carry, mask):
            start = pl.multiple_of(j * tq, tq)
            kj = k_ref[pl.ds(start, tq), :]
            vj = v_ref[pl.ds(start, tq), :]
            out = []
            for hd in range(2):
                c, acc = carry[hd]
                lom, lb = _sb_scores(qs[hd], kj, mask)
                a = jnp.exp(lb + (c + jnp.dot(lom.astype(BF16), tri, preferred_element_type=F32)))
                if mask is not None:
                    a = jnp.where(mask, a, 0.0)
                acc = acc + jnp.dot(a.astype(BF16), vj, preferred_element_type=F32)
                out.append((c + jnp.sum(lom, axis=1, keepdims=True), acc))
            return tuple(out)

        init = tuple((jnp.zeros((tq, 1), F32), jnp.zeros((tq, width), F32)) for _ in range(2))
        carry = block(i, init, diag)
        carry = lax.fori_loop(0, i, lambda t, cr: block(i - 1 - t, cr, None), carry)
        o_ref[...] = jnp.where(first_head, carry[0][1], carry[1][1]).astype(BF16)
        c_ref[0] = carry[0][0]
        c_ref[1] = carry[1][0]

    return _pc(body, name=name, grid=(npair, nq),
               in_specs=[pl.BlockSpec((tq, width), lambda p, i: (i, p)),
                         pl.BlockSpec((L, width), lambda p, i: (0, npair + p)),
                         pl.BlockSpec((L, width), lambda p, i: (0, 2 * npair + p))],
               out_specs=[pl.BlockSpec((tq, width), lambda p, i: (i, p)),
                          pl.BlockSpec((2, tq, 1), lambda p, i: (p, i, 0))],
               out_shape=[jax.ShapeDtypeStruct((L, npair * width), BF16), jax.ShapeDtypeStruct((H, L, 1), F32)],
               dims=("parallel", "parallel"))(qkv, qkv, qkv)


def _sb_bwd(qkv, ctot, do, do_col, scale, *, name, tq=256):
    L = qkv.shape[0]
    H = SB_HEADS
    width = 2 * (qkv.shape[1] // 3 // H)
    dh = width // 2
    npair = H // 2
    tq = min(tq, L)
    nq = L // tq
    do_off = do_col // width

    def body(q_ref, k_ref, v_ref, c_ref, do_ref, dq_ref, dk_ref, dv_ref):
        i = pl.program_id(1)

        @pl.when(i == 0)
        def _():
            dk_ref[...] = jnp.zeros_like(dk_ref)
            dv_ref[...] = jnp.zeros_like(dv_ref)

        diag, tri_later, first_head = _sb_masks(tq, width, dh)
        rr = lax.broadcasted_iota(jnp.int32, (tq, tq), 0)
        cc = lax.broadcasted_iota(jnp.int32, (tq, tq), 1)
        tri_before = jnp.where(rr < cc, 1.0, 0.0).astype(BF16)
        qp = q_ref[...]
        dop = do_ref[...].astype(BF16)
        zero = jnp.zeros_like(qp)
        qs = (jnp.where(first_head, qp, zero), jnp.where(first_head, zero, qp))
        dos = (jnp.where(first_head, dop, zero), jnp.where(first_head, zero, dop))
        ctots = (c_ref[0], c_ref[1])

        def block(j, carry, mask):
            start = pl.multiple_of(j * tq, tq)
            kj = k_ref[pl.ds(start, tq), :]
            vj = v_ref[pl.ds(start, tq), :]
            out = []
            dk_add, dv_add = None, None
            for hd in range(2):
                pc, pg, dq = carry[hd]
                lom, lb = _sb_scores(qs[hd], kj, mask)
                pc = pc + jnp.sum(lom, axis=1, keepdims=True)
                tail = (ctots[hd] - pc) + jnp.dot(lom.astype(BF16), tri_later, preferred_element_type=F32)
                a = jnp.exp(lb + tail)
                if mask is not None:
                    a = jnp.where(mask, a, 0.0)
                da = lax.dot_general(dos[hd], vj, _DIMS["nt"], preferred_element_type=F32)
                g = da * a
                hh = pg + jnp.dot(g.astype(BF16), tri_before, preferred_element_type=F32)
                sig = jnp.exp(lb)
                dz = g * (1.0 - sig) - hh * sig
                if mask is not None:
                    dz = jnp.where(mask, dz, 0.0)
                dz = dz.astype(BF16)
                dq = dq + jnp.dot(dz, kj, preferred_element_type=F32)
                dk_h = lax.dot_general(dz, qs[hd], _DIMS["tn"], preferred_element_type=F32)
                dv_h = lax.dot_general(a.astype(BF16), dos[hd], _DIMS["tn"], preferred_element_type=F32)
                dk_add = dk_h if dk_add is None else dk_add + dk_h
                dv_add = dv_h if dv_add is None else dv_add + dv_h
                out.append((pc, pg + jnp.sum(g, axis=1, keepdims=True), dq))
            dk_ref[pl.ds(start, tq), :] += dk_add
            dv_ref[pl.ds(start, tq), :] += dv_add
            return tuple(out)

        col = jnp.zeros((tq, 1), F32)
        init = tuple((col, col, jnp.zeros((tq, width), F32)) for _ in range(2))
        carry = lax.fori_loop(0, i, lambda j, cr: block(j, cr, None), init)
        carry = block(i, carry, diag)
        dq_ref[...] = (jnp.where(first_head, carry[0][2], carry[1][2]) * scale).astype(BF16)

    tile = pl.BlockSpec((tq, width), lambda p, i: (i, p))
    full = pl.BlockSpec((L, width), lambda p, i: (0, p))
    sds = jax.ShapeDtypeStruct((L, npair * width), F32)
    return _pc(body, name=name, grid=(npair, nq),
               in_specs=[tile, pl.BlockSpec((L, width), lambda p, i: (0, npair + p)),
                         pl.BlockSpec((L, width), lambda p, i: (0, 2 * npair + p)),
                         pl.BlockSpec((2, tq, 1), lambda p, i: (p, i, 0)),
                         pl.BlockSpec((tq, width), lambda p, i: (i, do_off + p))],
               out_specs=[tile, full, full],
               out_shape=[jax.ShapeDtypeStruct((L, npair * width), BF16), sds, sds],
               dims=("parallel", "arbitrary"))(qkv, qkv, qkv, ctot, do)


def _adamw(w, g, m, v, *, name, tm=256):
    R, C = w.shape
    tm = min(tm, R)
    assert R % tm == 0, (R, tm)
    c1 = 1.0 - ADAM_B1 ** ADAM_STEP
    c2 = 1.0 - ADAM_B2 ** ADAM_STEP

    def body(w_ref, g_ref, m_ref, v_ref, d_ref, nm_ref, nv_ref):
        gv = g_ref[...]
        nm = ADAM_B1 * m_ref[...] + (1.0 - ADAM_B1) * gv
        nv = ADAM_B2 * v_ref[...] + (1.0 - ADAM_B2) * (gv * gv)
        d_ref[...] = -ADAM_LR * ((nm / c1) / (jnp.sqrt(nv / c2) + ADAM_EPS) + ADAM_WD * w_ref[...])
        nm_ref[...] = nm
        nv_ref[...] = nv

    blk = pl.BlockSpec((tm, C), lambda i: (i, 0))
    sds = jax.ShapeDtypeStruct((R, C), F32)
    return _pc(body, name=name, grid=(R // tm,), in_specs=[blk] * 4, out_specs=[blk] * 3, out_shape=[sds] * 3,
               dims=("parallel",))(w, g, m, v)


def _add_halves(gbuf, ra, c, *, name, tm=256):
    _, S, Rh, C = gbuf.shape
    assert Rh % tm == 0, (Rh, tm)

    def body(c_ref, g_ref, r_ref, o_ref):
        o_ref[...] = (g_ref[...] + r_ref[...]).astype(BF16)

    grid_spec = pltpu.PrefetchScalarGridSpec(
        num_scalar_prefetch=1, grid=(S, Rh // tm),
        in_specs=[pl.BlockSpec((None, None, tm, C), lambda s, i, c_ref: (c_ref[0], s, i, 0)),
                  pl.BlockSpec((None, tm, C), lambda s, i, c_ref: (s, i, 0))],
        out_specs=pl.BlockSpec((None, tm, C), lambda s, i, c_ref: (s, i, 0)))
    return _pc_prefetch(body, name=name, grid_spec=grid_spec, out_shape=jax.ShapeDtypeStruct((S, Rh, C), BF16),
                        dims=("parallel", "parallel"))(c.reshape(1).astype(jnp.int32), gbuf, ra)


def _add_chips(p, rb, chip, *, name, tm=256):
    S, Rh, C = p.shape
    assert Rh % tm == 0, (Rh, tm)

    def body(s_ref, p_ref, r_ref, o_ref):
        o_ref[...] = ((p_ref[...].astype(F32) + r_ref[0].astype(F32)) + r_ref[1].astype(F32)) + r_ref[2].astype(F32)

    grid_spec = pltpu.PrefetchScalarGridSpec(
        num_scalar_prefetch=1, grid=(Rh // tm,),
        in_specs=[pl.BlockSpec((None, tm, C), lambda i, s_ref: (s_ref[0], i, 0)),
                  pl.BlockSpec((3, tm, C), lambda i, s_ref: (0, i, 0))],
        out_specs=pl.BlockSpec((tm, C), lambda i, s_ref: (i, 0)))
    return _pc_prefetch(body, name=name, grid_spec=grid_spec, out_shape=jax.ShapeDtypeStruct((Rh, C), F32),
                        dims=("parallel",))(chip.reshape(1).astype(jnp.int32), p, rb)


def _sum_slots(g, *, name):
    n, R, C = g.shape

    def body(g_ref, o_ref):
        acc = g_ref[0]
        for s in range(1, n):
            acc = acc + g_ref[s]
        o_ref[...] = acc

    return _pc(body, name=name, grid=(1,), in_specs=[pl.BlockSpec((n, R, C), lambda i: (0, 0, 0))],
               out_specs=pl.BlockSpec((R, C), lambda i: (0, 0)), out_shape=jax.ShapeDtypeStruct((R, C), F32),
               dims=("arbitrary",))(g)


ANY = pl.BlockSpec(memory_space=pl.ANY)


def _place():
    return lax.axis_index("x"), lax.axis_index("y"), lax.axis_index("c")


def _other_chips(x, y):
    return [(1 - x, y), (x, 1 - y), (1 - x, 1 - y)]


def _allgather_chips(w, *, name):
    _, Rh, C = w.shape

    def body(w_ref, out_ref, send_sems, recv_sems, local_sem):
        x, y, c = _place()
        sib = (x, y, 1 - c)
        chips = _other_chips(x, y)
        mine = pltpu.make_async_copy(w_ref, out_ref.at[2 * x + y], local_sem)
        mine.start()

        def copy(k, chip_id, half, to):
            return pltpu.make_async_remote_copy(src_ref=w_ref.at[half] if k < 3 else out_ref.at[chip_id, half],
                                                dst_ref=out_ref.at[chip_id, half], send_sem=send_sems.at[k],
                                                recv_sem=recv_sems.at[k], device_id=to, device_id_type=MESH)

        sends = [copy(j, 2 * x + y, c, (px, py, c)) for j, (px, py) in enumerate(chips)]
        for cp in sends:
            cp.start()
        passed = []
        for j, (px, py) in enumerate(chips):
            copy(j, 2 * px + py, c, (px, py, c)).wait_recv()
            fwd = copy(3 + j, 2 * px + py, c, sib)
            fwd.start()
            passed.append(fwd)
        for j, (px, py) in enumerate(chips):
            copy(3 + j, 2 * px + py, 1 - c, sib).wait_recv()
        for cp in sends + passed:
            cp.wait_send()
        mine.wait()

    return _pc_comm(body, name=name, in_specs=[ANY], out_specs=ANY,
                    out_shape=jax.ShapeDtypeStruct((N_CHIPS, 2, Rh, C), w.dtype),
                    scratch_shapes=[pltpu.SemaphoreType.DMA((6,)), pltpu.SemaphoreType.DMA((6,)),
                                    pltpu.SemaphoreType.DMA])(w)


def _send_half_to_sibling(gbuf, *, name):
    _, S, Rh, C = gbuf.shape

    def body(g_ref, ra_ref, send_sem, recv_sem):
        x, y, c = _place()
        cp = pltpu.make_async_remote_copy(src_ref=g_ref.at[1 - c], dst_ref=ra_ref, send_sem=send_sem,
                                          recv_sem=recv_sem, device_id=(x, y, 1 - c), device_id_type=MESH)
        cp.start()
        cp.wait()

    return _pc_comm(body, name=name, in_specs=[ANY], out_specs=ANY, out_shape=jax.ShapeDtypeStruct((S, Rh, C), F32),
                    scratch_shapes=[pltpu.SemaphoreType.DMA, pltpu.SemaphoreType.DMA])(gbuf)


def _scatter_to_chips(p, *, name):
    S, Rh, C = p.shape

    def body(p_ref, rb_ref, send_sems, recv_sems):
        x, y, c = _place()
        chips = _other_chips(x, y)

        def copy(j, shard, to):
            return pltpu.make_async_remote_copy(src_ref=p_ref.at[shard], dst_ref=rb_ref.at[j], send_sem=send_sems.at[j],
                                                recv_sem=recv_sems.at[j], device_id=to, device_id_type=MESH)

        sends = [copy(j, 2 * px + py, (px, py, c)) for j, (px, py) in enumerate(chips)]
        for cp in sends:
            cp.start()
        for cp in sends:
            cp.wait()

    return _pc_comm(body, name=name, in_specs=[ANY], out_specs=ANY, out_shape=jax.ShapeDtypeStruct((3, Rh, C), p.dtype),
                    scratch_shapes=[pltpu.SemaphoreType.DMA((3,)), pltpu.SemaphoreType.DMA((3,))])(p)


def _exchange_halves(f, *, name):
    Rh, C = f.shape

    def body(f_ref, out_ref, send_sem, recv_sem, local_sem):
        x, y, c = _place()
        mine = pltpu.make_async_copy(f_ref, out_ref.at[c], local_sem)
        mine.start()
        to_sib = pltpu.make_async_remote_copy(src_ref=f_ref, dst_ref=out_ref.at[c], send_sem=send_sem,
                                              recv_sem=recv_sem, device_id=(x, y, 1 - c), device_id_type=MESH)
        to_sib.start()
        to_sib.wait_send()
        pltpu.make_async_remote_copy(src_ref=f_ref, dst_ref=out_ref.at[1 - c], send_sem=send_sem, recv_sem=recv_sem,
                                     device_id=(x, y, 1 - c), device_id_type=MESH).wait_recv()
        mine.wait()

    return _pc_comm(body, name=name, in_specs=[ANY], out_specs=ANY, out_shape=jax.ShapeDtypeStruct((2, Rh, C), F32),
                    scratch_shapes=[pltpu.SemaphoreType.DMA, pltpu.SemaphoreType.DMA, pltpu.SemaphoreType.DMA])(f)


def _allgather_devices(v, *, name):
    R, C = v.shape

    def body(v_ref, out_ref, send_sems, recv_sems):
        x, y, c = _place()
        me = 4 * x + 2 * y + c
        out_ref[me] = v_ref[...]
        peers = []
        for k in range(1, 8):
            fx, fy, fc = (k >> 2) & 1, (k >> 1) & 1, k & 1
            px = 1 - x if fx else x
            py = 1 - y if fy else y
            pcc = 1 - c if fc else c
            peers.append((px, py, pcc))
        sends = []
        for k, peer in enumerate(peers):
            cp = pltpu.make_async_remote_copy(src_ref=v_ref, dst_ref=out_ref.at[me], send_sem=send_sems.at[k],
                                              recv_sem=recv_sems.at[k], device_id=peer, device_id_type=MESH)
            cp.start()
            sends.append(cp)
        for k, (px, py, pcc) in enumerate(peers):
            pltpu.make_async_remote_copy(src_ref=v_ref, dst_ref=out_ref.at[4 * px + 2 * py + pcc],
                                         send_sem=send_sems.at[k], recv_sem=recv_sems.at[k], device_id=peers[k],
                                         device_id_type=MESH).wait_recv()
        for cp in sends:
            cp.wait_send()

    vm = pl.BlockSpec(memory_space=pltpu.VMEM)
    return _pc_comm(body, name=name, in_specs=[vm], out_specs=vm, out_shape=jax.ShapeDtypeStruct((8, R, C), F32),
                    scratch_shapes=[pltpu.SemaphoreType.DMA((7,)), pltpu.SemaphoreType.DMA((7,))])(v)


D_MODEL = 1024
SC_W = D_MODEL // 4
GDN_W = D_MODEL // 2
SB_W = D_MODEL - SC_W - GDN_W
D_FF = 256 * ((8 * D_MODEL // 3 + 255) // 256)
O_SC, O_GQKV, O_GZ, O_GA, O_GB, O_SB = 0, 3 * SC_W, 3 * SC_W + 3 * GDN_W, 3 * SC_W + 4 * GDN_W, \
    3 * SC_W + 4 * GDN_W + GDN_HEADS, 3 * SC_W + 4 * GDN_W + 2 * GDN_HEADS
D_IN_PROJ = O_SB + 3 * SB_W
P_GQKV, P_GZ, P_SC, P_SB, P_GAB = 0, 3 * GDN_W, 4 * GDN_W, 4 * GDN_W + 3 * SC_W, 4 * GDN_W + 3 * SC_W + 3 * SB_W
P_PAD = 256
P_WIDTH = P_GAB + P_PAD


def _proj_to_kernel_layout(w):
    pad = jnp.zeros((w.shape[0], P_PAD - 2 * GDN_HEADS), w.dtype)
    return jnp.concatenate([w[:, O_GQKV:O_GA], w[:, O_SC:O_GQKV], w[:, O_SB:], w[:, O_GA:O_SB], pad], axis=1)


def _proj_from_kernel_layout(g):
    return jnp.concatenate([g[:, P_SC:P_SB], g[:, P_GQKV:P_SC], g[:, P_GAB:P_GAB + 2 * GDN_HEADS], g[:, P_SB:P_GAB]],
                           axis=1)


def _mixout_to_kernel_layout(w):
    return jnp.concatenate([w[SC_W:SC_W + GDN_W], w[:SC_W], w[SC_W + GDN_W:]], axis=0)


def _mixout_from_kernel_layout(g):
    return jnp.concatenate([g[GDN_W:GDN_W + SC_W], g[:GDN_W], g[GDN_W + SC_W:]], axis=0)


def _pack_rows(parts, width, rows_to):
    flat = jnp.concatenate([p.reshape(-1, width) for p in parts], axis=0)
    return jnp.pad(flat, ((0, rows_to - flat.shape[0]), (0, 0)))


def _unpack_rows(flat, shapes, width):
    out, r = [], 0
    for shp in shapes:
        n = int(np.prod(shp)) // width
        out.append(flat[r:r + n].reshape(shp))
        r += n
    return out


def _pack_vec(parts, rows_to):
    flat = jnp.concatenate([p.reshape(-1) for p in parts])
    return jnp.pad(flat, (0, rows_to * LANES - flat.shape[0])).reshape(rows_to, LANES)


def _unpack_vec(mat, shapes):
    flat = mat.reshape(-1)
    out, r = [], 0
    for shp in shapes:
        n = int(np.prod(shp))
        out.append(flat[r:r + n].reshape(shp))
        r += n
    return out


def _round_up(n, m):
    return (n + m - 1) // m * m


def _layer_fwd(x, p, l):
    L = x.shape[0]
    tag = "l%d_" % l
    h = _rmsnorm_fwd(x, p["wn_mix"], name=tag + "norm_mix")
    proj = _matmul(h, p["w_in"], "nn", tm=512, tn=768, tk=D_MODEL, name=tag + "proj")
    (y_sc,) = _conv_pointwise_fwd([(proj, P_SC + SC_W), (proj, P_SC + 2 * SC_W)], [(p["w_sconv"], 0)], [(proj, P_SC)],
                                  _pre_product, _post_gate_mul, [(SC_W, BF16)], tc=SC_W, tm=512, name=tag + "sconv")
    (qkv,) = _conv_pointwise_fwd([(proj, P_GQKV)], [(p["w_gdn_conv"], 0)], [], _pre_identity, _post_silu,
                                 [(3 * GDN_W, F32)], tc=GDN_W, tm=512, name=tag + "gdn_conv")
    qe, ke, u, w, attn, eg = _gdn_prep_fwd(qkv, proj, P_GAB, p["a_log"], p["dt_bias"], name=tag + "gdn_prep")
    y_gdn, states = _gdn_scan_fwd(qe, ke, u, w, attn, eg, proj, P_GZ, p["wgn"], name=tag + "gdn_scan")
    sb_scale = (SB_W // SB_HEADS) ** -0.5
    sbqkv = jnp.concatenate([proj[:, P_SB:P_SB + SB_W] * sb_scale, proj[:, P_SB + SB_W:P_SB + 3 * SB_W]],
                            axis=1).astype(BF16)
    y_sb, ctot = _sb_fwd(sbqkv, name=tag + "sb_fwd")
    y_cat = [y_gdn, y_sc, y_sb]
    x2 = _matmul(y_cat, p["w_out"], "nn", tm=512, tn=D_MODEL, tk=SC_W, res=x, name=tag + "mix_out")
    h2 = _rmsnorm_fwd(x2, p["wn_ffn"], name=tag + "norm_ffn")
    up_g = _matmul(h2, p["w_up_g"], "nn", tm=512, tn=D_FF // 2, tk=D_MODEL, name=tag + "up_gate")
    up_v = _matmul(h2, p["w_up_v"], "nn", tm=512, tn=D_FF // 2, tk=D_MODEL, name=tag + "up_val")
    (act,) = _conv_pointwise_fwd([(up_g, 0), (up_v, 0)], [(p["w_fconv_g"], 0), (p["w_fconv_v"], 0)], [],
                                 _pre_identity, _post_swiglu, [(D_FF, BF16)], tc=256, tm=512, name=tag + "ffn_act")
    x3 = _matmul(act, p["w_down"], "nn", tm=512, tn=D_MODEL, tk=D_FF // 2, res=x2, name=tag + "ffn_down")
    saved = dict(x=x, h=h, proj=proj, qkv=qkv, qe=qe, ke=ke, u=u, w=w, attn=attn, eg=eg, states=states,
                 sbqkv=sbqkv, ctot=ctot, y_cat=y_cat, x2=x2, h2=h2, up_g=up_g, up_v=up_v, act=act)
    return x3, saved


def _layer_bwd(dx3, p, s, l):
    L = dx3.shape[0]
    tag = "l%d_b_" % l
    g = {}
    dact = _matmul(dx3, p["w_down"], "nt", tm=512, tn=D_FF // 2, tk=D_MODEL, name=tag + "dact")
    g["w_down"] = _matmul(s["act"], dx3, "tn", tm=D_FF // 2, tn=D_MODEL, tk=512, name=tag + "dw_down")
    (dup_g, dup_v), _, (g["w_fconv_g"], g["w_fconv_v"]) = _conv_pointwise_bwd(
        [(s["up_g"], 0), (s["up_v"], 0)], [(p["w_fconv_g"], 0), (p["w_fconv_v"], 0)], [], [(dact, 0)],
        _pre_identity, _post_swiglu, D_FF, tc=256, tm=512, name=tag + "ffn_act")
    dh2 = _matmul(dup_g, p["w_up_g"], "nt", tm=512, tn=D_MODEL, tk=D_FF // 2, name=tag + "dh2_gate")
    dh2 = _matmul(dup_v, p["w_up_v"], "nt", tm=512, tn=D_MODEL, tk=D_FF // 2, res=dh2, name=tag + "dh2_val")
    g["w_up_g"] = _matmul(s["h2"], dup_g, "tn", tm=D_MODEL, tn=D_FF // 2, tk=512, name=tag + "dw_up_gate")
    g["w_up_v"] = _matmul(s["h2"], dup_v, "tn", tm=D_MODEL, tn=D_FF // 2, tk=512, name=tag + "dw_up_val")
    dx2, g["wn_ffn"] = _rmsnorm_bwd(dh2, s["x2"], p["wn_ffn"], dx3, name=tag + "norm_ffn")
    dycat = _matmul(dx2, p["w_out"], "nt", tm=512, tn=D_MODEL, tk=D_MODEL, name=tag + "dycat")
    g["w_out"] = _matmul(s["y_cat"], dx2, "tn", tm=SC_W, tn=D_MODEL, tk=512, name=tag + "dw_out")
    sb_scale = (SB_W // SB_HEADS) ** -0.5
    dsq, dsk, dsv = _sb_bwd(s["sbqkv"], s["ctot"], dycat, GDN_W + SC_W, sb_scale, name=tag + "sb_bwd")
    dqe, dke, du, dw, dattn, deg, dgz, g["wgn"] = _gdn_scan_bwd(
        s["qe"], s["ke"], s["u"], s["w"], s["attn"], s["eg"], s["proj"], P_GZ, p["wgn"], s["states"], dycat, 0,
        name=tag + "gdn_scan")
    dqkv_act, dgab, g["a_log"], g["dt_bias"] = _gdn_prep_bwd(
        s["qkv"], s["proj"], P_GAB, p["a_log"], p["dt_bias"], dqe, dke, du, dw, dattn, deg, P_PAD,
        name=tag + "gdn_prep")
    (dqkv,), _, (g["w_gdn_conv"],) = _conv_pointwise_bwd(
        [(s["proj"], P_GQKV)], [(p["w_gdn_conv"], 0)], [], [(dqkv_act, 0)], _pre_identity, _post_silu, 3 * GDN_W,
        tc=GDN_W, tm=512, name=tag + "gdn_conv")
    (dsc_c, dsc_h), (dsc_b,), (g["w_sconv"],) = _conv_pointwise_bwd(
        [(s["proj"], P_SC + SC_W), (s["proj"], P_SC + 2 * SC_W)], [(p["w_sconv"], 0)], [(s["proj"], P_SC)],
        [(dycat, GDN_W)], _pre_product, _post_gate_mul, SC_W, tc=SC_W, tm=512, name=tag + "sconv")
    dproj = [dqkv, dgz, dsc_b, dsc_c, dsc_h, dsq, dsk, dsv, dgab]
    dh = _matmul(dproj, p["w_in"], "nt", tm=512, tn=D_MODEL, tk=SC_W, name=tag + "dh")
    g["w_in"] = _matmul(s["h"], dproj, "tn", tm=D_MODEL, tn=SC_W, tk=512, name=tag + "dw_in")
    dx, g["wn_mix"] = _rmsnorm_bwd(dh, s["x"], p["wn_mix"], dx2, name=tag + "norm_mix")
    return dx, g


BIG = ("w_mix_in", "w_mix_out", "w_ffn_up", "w_ffn_down")
BIG_AXIS = {"w_mix_in": 2, "w_mix_out": 1, "w_ffn_up": 2, "w_ffn_down": 1}
SMALL_SHARDED = ("w_sconv", "w_gdn_conv", "w_ffn_conv")
SMALL_REPLICATED = ("w_norm_mix", "gdn_a_log", "gdn_dt_bias", "w_gdn_norm", "w_norm_ffn", "w_norm_final")
WEIGHTS = ("w_norm_mix", "w_mix_in", "w_sconv", "w_gdn_conv", "gdn_a_log", "gdn_dt_bias", "w_gdn_norm", "w_mix_out",
           "w_norm_ffn", "w_ffn_up", "w_ffn_conv", "w_ffn_down", "w_norm_final")


def _big_parts(d):
    depth = d["w_mix_in"].shape[0]
    return [d[n][l] for l in range(depth) for n in BIG]


def kernel(x, w_norm_mix, w_mix_in, w_sconv, w_gdn_conv, gdn_a_log, gdn_dt_bias, w_gdn_norm, w_mix_out, w_norm_ffn, w_ffn_up, w_ffn_conv, w_ffn_down, w_norm_final, loss_target, m_w_norm_mix, m_w_mix_in, m_w_sconv, m_w_gdn_conv, m_gdn_a_log, m_gdn_dt_bias, m_w_gdn_norm, m_w_mix_out, m_w_norm_ffn, m_w_ffn_up, m_w_ffn_conv, m_w_ffn_down, m_w_norm_final, v_w_norm_mix, v_w_mix_in, v_w_sconv, v_w_gdn_conv, v_gdn_a_log, v_gdn_dt_bias, v_w_gdn_norm, v_w_mix_out, v_w_norm_ffn, v_w_ffn_up, v_w_ffn_conv, v_w_ffn_down, v_w_norm_final):
    W = dict(w_norm_mix=w_norm_mix, w_mix_in=w_mix_in, w_sconv=w_sconv, w_gdn_conv=w_gdn_conv, gdn_a_log=gdn_a_log,
             gdn_dt_bias=gdn_dt_bias, w_gdn_norm=w_gdn_norm, w_mix_out=w_mix_out, w_norm_ffn=w_norm_ffn,
             w_ffn_up=w_ffn_up, w_ffn_conv=w_ffn_conv, w_ffn_down=w_ffn_down, w_norm_final=w_norm_final)
    M = dict(w_norm_mix=m_w_norm_mix, w_mix_in=m_w_mix_in, w_sconv=m_w_sconv, w_gdn_conv=m_w_gdn_conv,
             gdn_a_log=m_gdn_a_log, gdn_dt_bias=m_gdn_dt_bias, w_gdn_norm=m_w_gdn_norm, w_mix_out=m_w_mix_out,
             w_norm_ffn=m_w_norm_ffn, w_ffn_up=m_w_ffn_up, w_ffn_conv=m_w_ffn_conv, w_ffn_down=m_w_ffn_down,
             w_norm_final=m_w_norm_final)
    V = dict(w_norm_mix=v_w_norm_mix, w_mix_in=v_w_mix_in, w_sconv=v_w_sconv, w_gdn_conv=v_w_gdn_conv,
             gdn_a_log=v_gdn_a_log, gdn_dt_bias=v_gdn_dt_bias, w_gdn_norm=v_w_gdn_norm, w_mix_out=v_w_mix_out,
             w_norm_ffn=v_w_norm_ffn, w_ffn_up=v_w_ffn_up, w_ffn_conv=v_w_ffn_conv, w_ffn_down=v_w_ffn_down,
             w_norm_final=v_w_norm_final)
    depth = w_mix_in.shape[0]
    L = x.shape[1]
    mx, my, mc = lax.axis_index("x"), lax.axis_index("y"), lax.axis_index("c")
    chip = 2 * mx + my

    big_shapes = [a.shape for a in _big_parts(W)]
    rows = sum(int(np.prod(s)) // D_MODEL for s in big_shapes)
    rows_pad = _round_up(rows, 512)
    rh = rows_pad // 2
    w_flat = _pack_rows(_big_parts(W), D_MODEL, rows_pad)
    gathered = _allgather_chips(w_flat.astype(BF16).reshape(2, rh, D_MODEL), name="gather_big")
    gathered = gathered.reshape(N_CHIPS, rows_pad, D_MODEL)
    per_chip = [_unpack_rows(gathered[b], big_shapes, D_MODEL) for b in range(N_CHIPS)]
    full_big = []
    for l in range(depth):
        lay = {}
        for n_i, n in enumerate(BIG):
            lay[n] = jnp.concatenate([per_chip[b][l * len(BIG) + n_i] for b in range(N_CHIPS)], axis=BIG_AXIS[n] - 1)
        full_big.append(lay)

    small_sh_shapes = [W[n].shape for n in SMALL_SHARDED]
    n_small_sh = sum(int(np.prod(s)) for s in small_sh_shapes)
    small_rows = _round_up(n_small_sh, 8 * LANES) // LANES
    small_all = _allgather_devices(_pack_vec([W[n] for n in SMALL_SHARDED], small_rows), name="gather_small")
    small_chip = [_unpack_vec(small_all[2 * b], small_sh_shapes) for b in range(N_CHIPS)]
    full_small = {n: jnp.concatenate([small_chip[b][i] for b in range(N_CHIPS)], axis=2)
                  for i, n in enumerate(SMALL_SHARDED)}

    params = []
    for l in range(depth):
        w_up = full_big[l]["w_ffn_up"]
        fconv = full_small["w_ffn_conv"][l]
        params.append(dict(
            wn_mix=w_norm_mix[l], w_in=_proj_to_kernel_layout(full_big[l]["w_mix_in"]),
            w_sconv=full_small["w_sconv"][l], w_gdn_conv=full_small["w_gdn_conv"][l],
            a_log=gdn_a_log[l].reshape(GDN_HEADS, 1, 1), dt_bias=gdn_dt_bias[l].reshape(GDN_HEADS, 1, 1),
            wgn=w_gdn_norm[l], w_out=_mixout_to_kernel_layout(full_big[l]["w_mix_out"]), wn_ffn=w_norm_ffn[l],
            w_up_g=w_up[:, :D_FF], w_up_v=w_up[:, D_FF:], w_fconv_g=fconv[:, :D_FF], w_fconv_v=fconv[:, D_FF:],
            w_down=full_big[l]["w_ffn_down"]))

    xs = x[0]
    saved = []
    for l in range(depth):
        xs, s = _layer_fwd(xs, params[l], l)
        saved.append(s)
    loss_row, dx, g_norm_final = _final_loss(xs, w_norm_final, loss_target[0], name="final_loss")
    grads = [None] * depth
    for l in reversed(range(depth)):
        dx, grads[l] = _layer_bwd(dx, params[l], saved[l], l)
    loss = lax.psum(loss_row[0, 0], ("x", "y", "c"))

    G = {
        "w_mix_in": jnp.stack([_proj_from_kernel_layout(grads[l]["w_in"]) for l in range(depth)]),
        "w_mix_out": jnp.stack([_mixout_from_kernel_layout(grads[l]["w_out"]) for l in range(depth)]),
        "w_ffn_up": jnp.stack([jnp.concatenate([grads[l]["w_up_g"], grads[l]["w_up_v"]], axis=1)
                               for l in range(depth)]),
        "w_ffn_down": jnp.stack([grads[l]["w_down"] for l in range(depth)]),
        "w_sconv": jnp.stack([grads[l]["w_sconv"] for l in range(depth)]),
        "w_gdn_conv": jnp.stack([grads[l]["w_gdn_conv"] for l in range(depth)]),
        "w_ffn_conv": jnp.stack([jnp.concatenate([grads[l]["w_fconv_g"], grads[l]["w_fconv_v"]], axis=1)
                                 for l in range(depth)]),
        "w_norm_mix": jnp.stack([grads[l]["wn_mix"].reshape(-1) for l in range(depth)]),
        "gdn_a_log": jnp.stack([grads[l]["a_log"].reshape(-1) for l in range(depth)]),
        "gdn_dt_bias": jnp.stack([grads[l]["dt_bias"].reshape(-1) for l in range(depth)]),
        "w_gdn_norm": jnp.stack([grads[l]["wgn"].reshape(-1) for l in range(depth)]),
        "w_norm_ffn": jnp.stack([grads[l]["wn_ffn"].reshape(-1) for l in range(depth)]),
        "w_norm_final": g_norm_final.reshape(-1),
    }

    def shard_of(a, n, b):
        width = a.shape[BIG_AXIS[n]] // N_CHIPS
        return lax.slice_in_dim(a, b * width, (b + 1) * width, axis=BIG_AXIS[n])

    gbuf = jnp.stack([_pack_rows(_big_parts({n: shard_of(G[n], n, b) for n in BIG}), D_MODEL, rows_pad)
                      for b in range(N_CHIPS)])
    gbuf = gbuf.reshape(N_CHIPS, 2, rh, D_MODEL).transpose(1, 0, 2, 3)
    from_sibling = _send_half_to_sibling(gbuf, name="rs_sibling")
    chip_sum = _add_halves(gbuf, from_sibling, mc, name="rs_add_halves")
    from_chips = _scatter_to_chips(chip_sum, name="rs_chips")
    my_half = _add_chips(chip_sum, from_chips, chip, name="rs_add_chips")
    g_flat = _exchange_halves(my_half, name="rs_result").reshape(rows_pad, D_MODEL)
    m_flat = _pack_rows(_big_parts(M), D_MODEL, rows_pad)
    v_flat = _pack_rows(_big_parts(V), D_MODEL, rows_pad)
    d_flat, nm_flat, nv_flat = _adamw(w_flat, g_flat, m_flat, v_flat, name="adamw_big", tm=256)
    out_g, out_d, out_m, out_v = {}, {}, {}, {}
    for flat, dst in ((g_flat, out_g), (d_flat, out_d), (nm_flat, out_m), (nv_flat, out_v)):
        parts = _unpack_rows(flat, big_shapes, D_MODEL)
        for n_i, n in enumerate(BIG):
            dst[n] = jnp.stack([parts[l * len(BIG) + n_i] for l in range(depth)])

    small_names = SMALL_SHARDED + SMALL_REPLICATED
    small_full_shapes = [G[n].shape for n in small_names]
    n_small = sum(int(np.prod(s)) for s in small_full_shapes)
    red_rows = _round_up(n_small, 8 * LANES) // LANES
    partials = _allgather_devices(_pack_vec([G[n] for n in small_names], red_rows), name="reduce_small")
    summed = _unpack_vec(_sum_slots(partials, name="reduce_small_sum"), small_full_shapes)
    g_small = {}
    for n, a in zip(small_names, summed):
        if n in SMALL_SHARDED:
            width = a.shape[2] // N_CHIPS
            a = lax.dynamic_slice_in_dim(a, chip * width, width, axis=2)
        g_small[n] = a
    own_shapes = [W[n].shape for n in small_names]
    n_own = sum(int(np.prod(s)) for s in own_shapes)
    own_rows = _round_up(n_own, 8 * LANES) // LANES
    packed = [_pack_vec([src[n] for n in small_names], own_rows) for src in (W, g_small, M, V)]
    d_s, nm_s, nv_s = _adamw(*packed, name="adamw_small", tm=own_rows)
    for mat, dst in ((packed[1], out_g), (d_s, out_d), (nm_s, out_m), (nv_s, out_v)):
        for n, a in zip(small_names, _unpack_vec(mat, own_shapes)):
            dst[n] = a

    outs = [loss, dx[None]]
    for dst in (out_g, out_d, out_m, out_v):
        outs += [dst[n] for n in WEIGHTS]
    return tuple(outs)
```

```python
import functools

import jax
import jax.numpy as jnp
import numpy as np
from jax import lax
from jax.experimental import pallas as pl
from jax.experimental.pallas import tpu as pltpu

F32 = jnp.float32
BF16 = jnp.bfloat16
MESH = pl.DeviceIdType.MESH

NORM_EPS = 1e-6
GDN_HEADS = 4
GDN_CHUNK = 64
GDN_CONV = 4
SB_HEADS = 4
SC_KERNEL = 3
FFN_CONV = 3
ADAM_LR = 0.001
ADAM_B1 = 0.9
ADAM_B2 = 0.999
ADAM_EPS = 1e-08
ADAM_WD = 0.01
ADAM_STEP = 10

VMEM_LIMIT_BYTES = 48 * 1024 * 1024
HALO = 8
LANES = 128
N_CHIPS = 4


def _pc(body, *, name, grid, in_specs, out_specs, out_shape, scratch_shapes=(), dims=None):
    params = dict(vmem_limit_bytes=VMEM_LIMIT_BYTES)
    if dims is not None:
        params["dimension_semantics"] = dims
    return pl.pallas_call(body, name=name, grid=grid, in_specs=in_specs, out_specs=out_specs, out_shape=out_shape,
                          scratch_shapes=list(scratch_shapes), compiler_params=pltpu.CompilerParams(**params))


def _pc_prefetch(body, *, name, grid_spec, out_shape, dims):
    return pl.pallas_call(body, name=name, grid_spec=grid_spec, out_shape=out_shape,
                          compiler_params=pltpu.CompilerParams(vmem_limit_bytes=VMEM_LIMIT_BYTES,
                                                               dimension_semantics=dims))


def _pc_comm(body, *, name, in_specs, out_specs, out_shape, scratch_shapes):
    return pl.pallas_call(body, name=name, in_specs=in_specs, out_specs=out_specs, out_shape=out_shape,
                          scratch_shapes=list(scratch_shapes),
                          compiler_params=pltpu.CompilerParams(vmem_limit_bytes=VMEM_LIMIT_BYTES))


_DIMS = {"nn": (((1,), (0,)), ((), ())), "nt": (((1,), (1,)), ((), ())), "tn": (((0,), (0,)), ((), ()))}


def _matmul(a, b, mode, *, name, tm=512, tn=512, tk=512, out_dtype=F32, res=None):
    if mode == "nn":
        (M, K), (K2, N) = a.shape, b.shape
    elif mode == "nt":
        (M, K), (N, K2) = a.shape, b.shape
    else:
        (K, M), (K2, N) = a.shape, b.shape
    assert K == K2, (a.shape, b.shape, mode)
    tm, tn, tk = min(tm, M), min(tn, N), min(tk, K)
    assert M % tm == 0 and N % tn == 0 and K % tk == 0, (M, N, K, tm, tn, tk)
    nk = K // tk
    if mode == "tn":
        a_spec = pl.BlockSpec((tk, tm), lambda i, j, k: (k, i))
    else:
        a_spec = pl.BlockSpec((tm, tk), lambda i, j, k: (i, k))
    if mode == "nt":
        b_spec = pl.BlockSpec((tn, tk), lambda i, j, k: (j, k))
    else:
        b_spec = pl.BlockSpec((tk, tn), lambda i, j, k: (k, j))
    o_spec = pl.BlockSpec((tm, tn), lambda i, j, k: (i, j))
    has_res = res is not None
    dn = _DIMS[mode]

    def body(*refs):
        if has_res:
            a_ref, b_ref, r_ref, o_ref, acc = refs
        else:
            a_ref, b_ref, o_ref, acc = refs
        k = pl.program_id(2)
        p = lax.dot_general(a_ref[...].astype(BF16), b_ref[...].astype(BF16), dn, preferred_element_type=F32)

        def finish(total):
            if has_res:
                total = total + r_ref[...].astype(F32)
            o_ref[...] = total.astype(out_dtype)

        if nk == 1:
            finish(p)
        else:
            @pl.when(k == 0)
            def _():
                acc[...] = p

            @pl.when(k > 0)
            def _():
                acc[...] += p

            @pl.when(k == nk - 1)
            def _():
                finish(acc[...])

    in_specs = [a_spec, b_spec] + ([o_spec] if has_res else [])
    args = (a, b) + ((res,) if has_res else ())
    return _pc(body, name=name, grid=(M // tm, N // tn, nk), in_specs=in_specs, out_specs=o_spec,
               out_shape=jax.ShapeDtypeStruct((M, N), out_dtype), scratch_shapes=[pltpu.VMEM((tm, tn), F32)],
               dims=("parallel", "parallel", "arbitrary"))(*args)


def _offsets(parts, own_width_aligned):
    offs, at = [], 0
    for p in parts:
        assert at % (p.shape[1] if own_width_aligned else LANES) == 0, (at, p.shape)
        offs.append(at)
        at += p.shape[1]
    return offs, at


def _matmul_rows_parts(parts, w, mode, *, name, tm=512, res=None):
    M = parts[0].shape[0]
    offs, K = _offsets(parts, True)
    tm = min(tm, M)
    N = w.shape[1] if mode == "nn" else w.shape[0]
    assert (w.shape[0] if mode == "nn" else w.shape[1]) == K
    has_res = res is not None
    n = len(parts)

    def body(*refs):
        o_ref = refs[-1]
        total = None
        for s in range(n):
            p = lax.dot_general(refs[s][...].astype(BF16), refs[n + s][...].astype(BF16), _DIMS[mode],
                                preferred_element_type=F32)
            total = p if total is None else total + p
        if has_res:
            total = total + refs[2 * n][...]
        o_ref[...] = total

    in_specs = [pl.BlockSpec((tm, p.shape[1]), lambda i: (i, 0)) for p in parts]
    for p, off in zip(parts, offs):
        blk = off // p.shape[1]
        if mode == "nn":
            in_specs.append(pl.BlockSpec((p.shape[1], N), lambda i, blk=blk: (blk, 0)))
        else:
            in_specs.append(pl.BlockSpec((N, p.shape[1]), lambda i, blk=blk: (0, blk)))
    o_spec = pl.BlockSpec((tm, N), lambda i: (i, 0))
    args = tuple(parts) + (w,) * n + ((res,) if has_res else ())
    return _pc(body, name=name, grid=(M // tm,), in_specs=in_specs + ([o_spec] if has_res else []), out_specs=o_spec,
               out_shape=jax.ShapeDtypeStruct((M, N), F32), dims=("parallel",))(*args)


def _matmul_tn_parts(a, b, *, name, tk=512):
    a_parts = list(a) if isinstance(a, (list, tuple)) else [a]
    b_parts = list(b) if isinstance(b, (list, tuple)) else [b]
    assert len(a_parts) == 1 or len(b_parts) == 1
    a_offs, M = _offsets(a_parts, False)
    b_offs, N = _offsets(b_parts, False)
    K = a_parts[0].shape[0]
    tk = min(tk, K)
    na, nb = len(a_parts), len(b_parts)

    def body(*refs):
        o_ref = refs[-1]
        first = pl.program_id(0) == 0
        for s in range(na):
            for t in range(nb):
                p = lax.dot_general(refs[s][...].astype(BF16), refs[na + t][...].astype(BF16), _DIMS["tn"],
                                    preferred_element_type=F32)
                rows = slice(a_offs[s], a_offs[s] + a_parts[s].shape[1])
                cols = slice(b_offs[t], b_offs[t] + b_parts[t].shape[1])

                @pl.when(first)
                def _(p=p, rows=rows, cols=cols):
                    o_ref[rows, cols] = p

                @pl.when(jnp.logical_not(first))
                def _(p=p, rows=rows, cols=cols):
                    o_ref[rows, cols] += p

    in_specs = [pl.BlockSpec((tk, p.shape[1]), lambda k: (k, 0)) for p in a_parts + b_parts]
    return _pc(body, name=name, grid=(K // tk,), in_specs=in_specs, out_specs=pl.BlockSpec((M, N), lambda k: (0, 0)),
               out_shape=jax.ShapeDtypeStruct((M, N), F32), dims=("arbitrary",))(*a_parts, *b_parts)


def _rmsnorm_fwd(x, w, *, name, tm=512):
    L, D = x.shape
    tm = min(tm, L)

    def body(x_ref, w_ref, h_ref):
        xv = x_ref[...]
        r = lax.rsqrt(jnp.mean(xv * xv, axis=-1, keepdims=True) + NORM_EPS)
        h_ref[...] = (xv * r * w_ref[...]).astype(BF16)

    return _pc(body, name=name, grid=(L // tm,),
               in_specs=[pl.BlockSpec((tm, D), lambda i: (i, 0)), pl.BlockSpec((1, D), lambda i: (0, 0))],
               out_specs=pl.BlockSpec((tm, D), lambda i: (i, 0)), out_shape=jax.ShapeDtypeStruct((L, D), BF16),
               dims=("parallel",))(x, w.reshape(1, D))


def _rmsnorm_bwd(dh, x, w, dres, *, name, tm=512):
    L, D = x.shape
    tm = min(tm, L)

    def body(dh_ref, x_ref, w_ref, dres_ref, dx_ref, dw_ref):
        xv = x_ref[...]
        r = lax.rsqrt(jnp.mean(xv * xv, axis=-1, keepdims=True) + NORM_EPS)
        xhat = xv * r
        dhv = dh_ref[...]
        g = dhv * w_ref[...]
        dx_ref[...] = dres_ref[...] + r * (g - xhat * jnp.mean(g * xhat, axis=-1, keepdims=True))
        part = jnp.sum(dhv * xhat, axis=0, keepdims=True)

        @pl.when(pl.program_id(0) == 0)
        def _():
            dw_ref[...] = part

        @pl.when(pl.program_id(0) > 0)
        def _():
            dw_ref[...] += part

    row = pl.BlockSpec((tm, D), lambda i: (i, 0))
    vec = pl.BlockSpec((1, D), lambda i: (0, 0))
    return _pc(body, name=name, grid=(L // tm,), in_specs=[row, row, vec, row], out_specs=[row, vec],
               out_shape=[jax.ShapeDtypeStruct((L, D), F32), jax.ShapeDtypeStruct((1, D), F32)],
               dims=("arbitrary",))(dh, x, w.reshape(1, D), dres)


def _final_loss(x, w, tgt, *, name, tm=512):
    L, D = x.shape
    tm = min(tm, L)

    def body(x_ref, w_ref, t_ref, loss_ref, dx_ref, dw_ref):
        xv = x_ref[...]
        r = lax.rsqrt(jnp.mean(xv * xv, axis=-1, keepdims=True) + NORM_EPS)
        xhat = xv * r
        e = xhat * w_ref[...] - t_ref[...]
        lpart = jnp.broadcast_to(0.5 * jnp.sum(jnp.mean(e * e, axis=-1, keepdims=True), axis=0, keepdims=True),
                                 (1, LANES))
        dy = e * (1.0 / D)
        g = dy * w_ref[...]
        dx_ref[...] = r * (g - xhat * jnp.mean(g * xhat, axis=-1, keepdims=True))
        part = jnp.sum(dy * xhat, axis=0, keepdims=True)

        @pl.when(pl.program_id(0) == 0)
        def _():
            dw_ref[...] = part
            loss_ref[...] = lpart

        @pl.when(pl.program_id(0) > 0)
        def _():
            dw_ref[...] += part
            loss_ref[...] += lpart

    row = pl.BlockSpec((tm, D), lambda i: (i, 0))
    vec = pl.BlockSpec((1, D), lambda i: (0, 0))
    lsp = pl.BlockSpec((1, LANES), lambda i: (0, 0))
    return _pc(body, name=name, grid=(L // tm,), in_specs=[row, vec, row], out_specs=[lsp, row, vec],
               out_shape=[jax.ShapeDtypeStruct((1, LANES), F32), jax.ShapeDtypeStruct((L, D), F32),
                          jax.ShapeDtypeStruct((1, D), F32)],
               dims=("arbitrary",))(x, w.reshape(1, D), tgt)


def _shift_down(x, prev, k):
    if k == 0:
        return x
    r = pltpu.roll(x, k, 0)
    p = pltpu.roll(prev, k, 0)
    row = lax.broadcasted_iota(jnp.int32, p.shape, 0)
    head = jnp.where(row < k, p, r[:HALO])
    return jnp.concatenate([head, r[HALO:]], axis=0)


def _shift_up(x, j):
    if j == 0:
        return x
    return pltpu.roll(x, x.shape[0] - j, 0)


def _silu(x):
    return x * jax.nn.sigmoid(x)


def _conv_taps(p, p_prev, w):
    K = w.shape[0]
    out = None
    for k in range(K):
        term = w[k:k + 1, :] * _shift_down(p, p_prev, K - 1 - k)
        out = term if out is None else out + term
    return out


def _conv_pointwise_fwd(xs, ws, es, pre, post, outs, *, tc, tm, name):
    L = xs[0][0].shape[0]
    tm = min(tm, L)
    ncol = outs[0][0] // tc
    nrow = L // tm
    hb = tm // HALO
    nx, nw, ne, no = len(xs), len(ws), len(es), len(outs)
    K = ws[0][0].shape[0]

    def body(*refs):
        xc = [refs[2 * n][...] for n in range(nx)]
        i = pl.program_id(1)
        first = (i > 0).astype(F32)
        xp = [refs[2 * n + 1][...] * first for n in range(nx)]
        wv = [refs[2 * nx + n][...] for n in range(nw)]
        ev = [refs[2 * nx + nw + n][...] for n in range(ne)]
        o_refs = refs[2 * nx + nw + ne:]
        ps, pps = pre(*xc), pre(*xp)
        us = [_conv_taps(p, pp, w) for p, pp, w in zip(ps, pps, wv)]
        for o_ref, val in zip(o_refs, post(us, ev)):
            o_ref[...] = val.astype(o_ref.dtype)

    in_specs, args = [], []
    for arr, c0 in xs:
        off = c0 // tc
        in_specs.append(pl.BlockSpec((tm, tc), lambda j, i, off=off: (i, j + off)))
        in_specs.append(pl.BlockSpec((HALO, tc), lambda j, i, off=off: (jnp.maximum(i * hb - 1, 0), j + off)))
        args += [arr, arr]
    for arr, c0 in ws:
        off = c0 // tc
        in_specs.append(pl.BlockSpec((K, tc), lambda j, i, off=off: (0, j + off)))
        args.append(arr)
    for arr, c0 in es:
        off = c0 // tc
        in_specs.append(pl.BlockSpec((tm, tc), lambda j, i, off=off: (i, j + off)))
        args.append(arr)
    out_specs = [pl.BlockSpec((tm, tc), lambda j, i: (i, j)) for _ in range(no)]
    out_shape = [jax.ShapeDtypeStruct((L, c), dt) for c, dt in outs]
    return _pc(body, name=name, grid=(ncol, nrow), in_specs=in_specs, out_specs=out_specs, out_shape=out_shape,
               dims=("parallel", "parallel"))(*args)


def _conv_pointwise_bwd(xs, ws, es, dys, pre, post, width, *, tc, tm, name, out_dtype=BF16):
    L = xs[0][0].shape[0]
    tm = min(tm, L)
    ncol = width // tc
    nrow = L // tm
    hb = tm // HALO
    nx, nw, ne, ny = len(xs), len(ws), len(es), len(dys)
    K = ws[0][0].shape[0]

    def body(*refs):
        i = pl.program_id(1)
        first = (i > 0).astype(F32)
        more = (i < nrow - 1).astype(F32)
        pos = 0
        xc, xp, xe = [], [], []
        for n in range(nx):
            cur, prv, nxt = refs[pos][...], refs[pos + 1][...], refs[pos + 2][...]
            pos += 3
            xc.append(cur)
            xp.append(prv * first)
            xe.append(jnp.concatenate([cur, nxt], axis=0))
        wv = [refs[pos + n][...] for n in range(nw)]
        pos += nw
        ee = []
        for n in range(ne):
            ee.append(jnp.concatenate([refs[pos][...], refs[pos + 1][...]], axis=0))
            pos += 2
        dye = []
        for n in range(ny):
            dye.append(jnp.concatenate([refs[pos][...].astype(F32), refs[pos + 1][...].astype(F32) * more], axis=0))
            pos += 2
        dx_refs = refs[pos:pos + nx]
        de_refs = refs[pos + nx:pos + nx + ne]
        dw_refs = refs[pos + nx + ne:pos + nx + ne + nw]

        ps, pps = pre(*xe), pre(*xp)
        shifted = [[_shift_down(p, pp, K - 1 - k) for k in range(K)] for p, pp in zip(ps, pps)]
        us = []
        for n in range(nw):
            u = None
            for k in range(K):
                term = wv[n][k:k + 1, :] * shifted[n][k]
                u = term if u is None else u + term
            us.append(u)
        _, post_vjp = jax.vjp(lambda u_, e_: post(u_, e_), us, ee)
        dus, des = post_vjp(dye)
        dps = []
        for n in range(nw):
            dp = None
            for k in range(K):
                term = wv[n][k:k + 1, :] * _shift_up(dus[n], K - 1 - k)[:tm]
                dp = term if dp is None else dp + term
            dps.append(dp)
            for k in range(K):
                part = jnp.sum(dus[n][:tm] * shifted[n][k][:tm], axis=0, keepdims=True)

                @pl.when(i == 0)
                def _(part=part, n=n, k=k):
                    dw_refs[n][k:k + 1, :] = part

                @pl.when(i > 0)
                def _(part=part, n=n, k=k):
                    dw_refs[n][k:k + 1, :] += part
        _, pre_vjp = jax.vjp(lambda *x_: pre(*x_), *xc)
        dxs = pre_vjp(dps)
        for r, v in zip(dx_refs, dxs):
            r[...] = v.astype(out_dtype)
        for r, v in zip(de_refs, des):
            r[...] = v[:tm].astype(out_dtype)

    in_specs, args = [], []

    def add_rows(arr, c0, prev, nxt):
        off = c0 // tc
        in_specs.append(pl.BlockSpec((tm, tc), lambda j, i, off=off: (i, j + off)))
        args.append(arr)
        if prev:
            in_specs.append(pl.BlockSpec((HALO, tc), lambda j, i, off=off: (jnp.maximum(i * hb - 1, 0), j + off)))
            args.append(arr)
        if nxt:
            last = L // HALO - 1
            in_specs.append(pl.BlockSpec((HALO, tc), lambda j, i, off=off: (jnp.minimum((i + 1) * hb, last), j + off)))
            args.append(arr)

    for arr, c0 in xs:
        add_rows(arr, c0, True, True)
    for arr, c0 in ws:
        off = c0 // tc
        in_specs.append(pl.BlockSpec((K, tc), lambda j, i, off=off: (0, j + off)))
        args.append(arr)
    for arr, c0 in es:
        add_rows(arr, c0, False, True)
    for arr, c0 in dys:
        add_rows(arr, c0, False, True)
    tile = pl.BlockSpec((tm, tc), lambda j, i: (i, j))
    wtile = pl.BlockSpec((K, tc), lambda j, i: (0, j))
    out_specs = [tile] * (nx + ne) + [wtile] * nw
    out_shape = [jax.ShapeDtypeStruct((L, width), out_dtype)] * (nx + ne) + \
        [jax.ShapeDtypeStruct((K, width), F32)] * nw
    res = _pc(body, name=name, grid=(ncol, nrow), in_specs=in_specs, out_specs=out_specs, out_shape=out_shape,
              dims=("parallel", "arbitrary"))(*args)
    return res[:nx], res[nx:nx + ne], res[nx + ne:]


def _pre_identity(*x):
    return list(x)


def _pre_product(c, h):
    return [c * h]


def _post_silu(us, es):
    return [_silu(us[0])]


def _post_gate_mul(us, es):
    return [es[0] * us[0]]


def _post_swiglu(us, es):
    return [_silu(us[0]) * us[1]]


def _make_dot(passes):
    def raw(a, b, dn):
        a_hi = a.astype(BF16)
        b_hi = b.astype(BF16)
        out = lax.dot_general(a_hi, b_hi, dn, preferred_element_type=F32)
        if passes == 3:
            a_lo = (a - a_hi.astype(F32)).astype(BF16)
            b_lo = (b - b_hi.astype(F32)).astype(BF16)
            out = out + lax.dot_general(a_hi, b_lo, dn, preferred_element_type=F32)
            out = out + lax.dot_general(a_lo, b_hi, dn, preferred_element_type=F32)
        return out

    @jax.custom_vjp
    def nn(a, b):
        return raw(a, b, _DIMS["nn"])

    @jax.custom_vjp
    def nt(a, b):
        return raw(a, b, _DIMS["nt"])

    @jax.custom_vjp
    def tn(a, b):
        return raw(a, b, _DIMS["tn"])

    nn.defvjp(lambda a, b: (nn(a, b), (a, b)), lambda r, g: (nt(g, r[1]), tn(r[0], g)))
    nt.defvjp(lambda a, b: (nt(a, b), (a, b)), lambda r, g: (nn(g, r[1]), tn(g, r[0])))
    tn.defvjp(lambda a, b: (tn(a, b), (a, b)), lambda r, g: (nt(r[1], g), nn(r[0], g)))
    return nn, nt, tn


_NN1, _NT1, _TN1 = _make_dot(1)
_NN3, _NT3, _TN3 = _make_dot(3)


def _l2norm(x):
    return x * lax.rsqrt(jnp.sum(x * x, axis=-1, keepdims=True) + NORM_EPS)


def _unit_lower_inverse_raw(a_list):
    C = a_list[0].shape[0]
    ii = lax.broadcasted_iota(jnp.int32, (C, C), 0)
    jj = lax.broadcasted_iota(jnp.int32, (C, C), 1)
    eye = jnp.where(ii == jj, 1.0, 0.0)
    ts = [eye - a for a in a_list]
    ps = list(a_list)
    n = 2
    while n < C:
        ps = [_NN3(p, p) for p in ps]
        ts = [t + _NN3(t, p) for t, p in zip(ts, ps)]
        n *= 2
    return ts


@jax.custom_vjp
def _unit_lower_inverse(a_list):
    return _unit_lower_inverse_raw(a_list)


def _unit_lower_inverse_fwd(a_list):
    ts = _unit_lower_inverse_raw(a_list)
    return ts, ts


def _unit_lower_inverse_bwd(ts, gs):
    xs = [_TN3(t, g) for t, g in zip(ts, gs)]
    return ([-_NT3(x, t) for x, t in zip(xs, ts)],)


_unit_lower_inverse.defvjp(_unit_lower_inverse_fwd, _unit_lower_inverse_bwd)


def _gdn_prep(units):
    C, Dh = units[0][0].shape
    ii = lax.broadcasted_iota(jnp.int32, (C, C), 0)
    jj = lax.broadcasted_iota(jnp.int32, (C, C), 1)
    lane = lax.broadcasted_iota(jnp.int32, (1, C), 1)
    causal = ii >= jj
    strict = ii > jj
    qs = [_l2norm(un[0]) * (Dh ** -0.5) for un in units]
    ks = [_l2norm(un[1]) for un in units]
    betas = [jax.nn.sigmoid(un[4]) for un in units]
    gs = [-jnp.exp(un[5]) * jax.nn.softplus(un[3] + un[6]) for un in units]
    gc_rows = [jnp.sum(jnp.where(ii <= jj, g, 0.0), axis=0, keepdims=True) for g in gs]
    gc_cols = [jnp.sum(jnp.where(ii == jj, r, 0.0), axis=1, keepdims=True) for r in gc_rows]
    decays = [jnp.where(causal, jnp.exp(jnp.where(causal, c - r, 0.0)), 0.0) for c, r in zip(gc_cols, gc_rows)]
    kbs = [k * b for k, b in zip(ks, betas)]
    kks = [_NT1(kb, k) for kb, k in zip(kbs, ks)]
    qks = [_NT1(q, k) for q, k in zip(qs, ks)]
    ts = _unit_lower_inverse([jnp.where(strict, kk * d, 0.0) for kk, d in zip(kks, decays)])
    eg_cols = [jnp.exp(c) for c in gc_cols]
    uws = [_NN3(t, jnp.concatenate([un[2] * b, kb * e], axis=1))
           for t, un, b, kb, e in zip(ts, units, betas, kbs, eg_cols)]
    out = []
    for q, k, qk, d, uw, e, r, c in zip(qs, ks, qks, decays, uws, eg_cols, gc_rows, gc_cols):
        g_last = jnp.sum(jnp.where(lane == C - 1, r, 0.0), axis=1, keepdims=True)
        out.append((q * e, k * jnp.exp(g_last - c), uw[:, :Dh], uw[:, Dh:], jnp.where(causal, qk * d, 0.0),
                    jnp.broadcast_to(jnp.exp(g_last), (1, Dh))))
    return out


def _gdn_step(units):
    v_news = [un[3] - _NN1(un[4], un[0]) for un in units]
    o_state = [_NN1(un[1], un[0]) for un in units]
    o_intra = [_NN1(un[5], vn) for un, vn in zip(units, v_news)]
    s_adds = [_TN1(un[2], vn) for un, vn in zip(units, v_news)]
    out = []
    for un, a, b, s_add in zip(units, o_state, o_intra, s_adds):
        o = a + b
        y = o * lax.rsqrt(jnp.mean(o * o, axis=-1, keepdims=True) + NORM_EPS) * un[8] * _silu(un[7])
        out.append((y, un[0] * un[6] + s_add))
    return out


def _gdn_prep_fwd(qkv, gab, gab_col, a_log, dt_bias, *, name, chunks=4):
    L = qkv.shape[0]
    H, C = GDN_HEADS, GDN_CHUNK
    W = qkv.shape[1] // 3
    Dh = W // H
    N = L // C
    chunks = min(chunks, N)
    R = chunks * C
    gab_off = gab_col // LANES

    def body(q_ref, k_ref, v_ref, gab_ref, al_ref, dt_ref, qe_ref, ke_ref, u_ref, w_ref, at_ref, eg_ref):
        where = [(cc, h) for cc in range(chunks) for h in range(H)]
        units = []
        for cc, h in where:
            rows, sl = slice(cc * C, (cc + 1) * C), slice(h * Dh, (h + 1) * Dh)
            units.append((q_ref[rows, sl], k_ref[rows, sl], v_ref[rows, sl], gab_ref[rows, h:h + 1],
                          gab_ref[rows, H + h:H + h + 1], al_ref[h], dt_ref[h]))
        for (cc, h), (qe, ke, u, w, attn, eg) in zip(where, _gdn_prep(units)):
            rows, sl = slice(cc * C, (cc + 1) * C), slice(h * Dh, (h + 1) * Dh)
            qe_ref[rows, sl] = qe
            ke_ref[rows, sl] = ke
            u_ref[rows, sl] = u
            w_ref[rows, sl] = w
            at_ref[h, rows, :] = attn
            eg_ref[cc, h:h + 1, :] = eg

    col = lambda c: pl.BlockSpec((R, W), lambda n, c=c: (n, c))
    tok = pl.BlockSpec((R, LANES), lambda n: (n, gab_off))
    par = pl.BlockSpec((H, 1, 1), lambda n: (0, 0, 0))
    wide = pl.BlockSpec((R, W), lambda n: (n, 0))
    return _pc(body, name=name, grid=(N // chunks,), in_specs=[col(0), col(1), col(2), tok, par, par],
               out_specs=[wide, wide, wide, wide, pl.BlockSpec((H, R, C), lambda n: (0, n, 0)),
                          pl.BlockSpec((chunks, H, Dh), lambda n: (n, 0, 0))],
               out_shape=[jax.ShapeDtypeStruct((L, W), F32)] * 4 + [jax.ShapeDtypeStruct((H, L, C), F32),
                                                                   jax.ShapeDtypeStruct((N, H, Dh), F32)],
               dims=("parallel",))(qkv, qkv, qkv, gab, a_log, dt_bias)


def _gdn_prep_bwd(qkv, gab, gab_col, a_log, dt_bias, dqe, dke, du, dw, dattn, deg, gab_width, *, name, chunks=4):
    L = qkv.shape[0]
    H, C = GDN_HEADS, GDN_CHUNK
    W = qkv.shape[1] // 3
    Dh = W // H
    N = L // C
    chunks = min(chunks, N)
    R = chunks * C
    gab_off = gab_col // LANES

    def body(q_ref, k_ref, v_ref, gab_ref, al_ref, dt_ref, dqe_ref, dke_ref, du_ref, dw_ref, dat_ref, deg_ref,
             dqkv_ref, dgab_ref, dal_ref, ddt_ref):
        first = pl.program_id(0) == 0
        lane = lax.broadcasted_iota(jnp.int32, (C, gab_width), 1)
        dal_sum, ddt_sum = [None] * H, [None] * H
        where = [(cc, h) for cc in range(chunks) for h in range(H)]
        units, cots = [], []
        for cc, h in where:
            rows, sl = slice(cc * C, (cc + 1) * C), slice(h * Dh, (h + 1) * Dh)
            units.append((q_ref[rows, sl], k_ref[rows, sl], v_ref[rows, sl], gab_ref[rows, h:h + 1],
                          gab_ref[rows, H + h:H + h + 1], al_ref[h], dt_ref[h]))
            cots.append((dqe_ref[rows, sl], dke_ref[rows, sl], du_ref[rows, sl], dw_ref[rows, sl],
                         dat_ref[h, rows, :], deg_ref[cc, h:h + 1, :]))
        _, vjp = jax.vjp(_gdn_prep, units)
        (d_units,) = vjp(cots)
        dgabs = [jnp.zeros((C, gab_width), F32) for _ in range(chunks)]
        for (cc, h), (dq, dk, dv, dga, dgb, dal, ddt) in zip(where, d_units):
            rows = slice(cc * C, (cc + 1) * C)
            dqkv_ref[rows, h * Dh:(h + 1) * Dh] = dq
            dqkv_ref[rows, W + h * Dh:W + (h + 1) * Dh] = dk
            dqkv_ref[rows, 2 * W + h * Dh:2 * W + (h + 1) * Dh] = dv
            dgabs[cc] = dgabs[cc] + jnp.where(lane == h, dga, 0.0) + jnp.where(lane == H + h, dgb, 0.0)
            dal_sum[h] = dal if dal_sum[h] is None else dal_sum[h] + dal
            ddt_sum[h] = ddt if ddt_sum[h] is None else ddt_sum[h] + ddt
        for cc in range(chunks):
            dgab_ref[cc * C:(cc + 1) * C, :] = dgabs[cc].astype(BF16)

        @pl.when(first)
        def _():
            for h in range(H):
                dal_ref[h] = dal_sum[h]
                ddt_ref[h] = ddt_sum[h]

        @pl.when(jnp.logical_not(first))
        def _():
            for h in range(H):
                dal_ref[h] += dal_sum[h]
                ddt_ref[h] += ddt_sum[h]

    col = lambda c: pl.BlockSpec((R, W), lambda n, c=c: (n, c))
    tok = pl.BlockSpec((R, LANES), lambda n: (n, gab_off))
    par = pl.BlockSpec((H, 1, 1), lambda n: (0, 0, 0))
    wide = pl.BlockSpec((R, W), lambda n: (n, 0))
    att = pl.BlockSpec((H, R, C), lambda n: (0, n, 0))
    egs = pl.BlockSpec((chunks, H, Dh), lambda n: (n, 0, 0))
    return _pc(body, name=name, grid=(N // chunks,),
               in_specs=[col(0), col(1), col(2), tok, par, par, wide, wide, wide, wide, att, egs],
               out_specs=[pl.BlockSpec((R, 3 * W), lambda n: (n, 0)), pl.BlockSpec((R, gab_width), lambda n: (n, 0)),
                          par, par],
               out_shape=[jax.ShapeDtypeStruct((L, 3 * W), F32), jax.ShapeDtypeStruct((L, gab_width), BF16)]
               + [jax.ShapeDtypeStruct((H, 1, 1), F32)] * 2,
               dims=("arbitrary",))(qkv, qkv, qkv, gab, a_log, dt_bias, dqe, dke, du, dw, dattn, deg)


def _gdn_scan_fwd(qe, ke, u, w, attn, eg, gz, gz_col, wgn, *, name):
    L, W = qe.shape
    H, C = GDN_HEADS, GDN_CHUNK
    Dh = W // H
    N = L // C
    gz_off = gz_col // W

    def body(qe_ref, ke_ref, u_ref, w_ref, at_ref, eg_ref, gz_ref, wgn_ref, y_ref, st_ref, s_scr):
        @pl.when(pl.program_id(0) == 0)
        def _():
            s_scr[...] = jnp.zeros_like(s_scr)

        units = []
        for h in range(H):
            sl = slice(h * Dh, (h + 1) * Dh)
            st_ref[0, h] = s_scr[h]
            units.append((s_scr[h], qe_ref[:, sl], ke_ref[:, sl], u_ref[:, sl], w_ref[:, sl], at_ref[h],
                          eg_ref[0, h:h + 1, :], gz_ref[:, sl], wgn_ref[...]))
        for h, (y, S_new) in enumerate(_gdn_step(units)):
            y_ref[:, h * Dh:(h + 1) * Dh] = y.astype(BF16)
            s_scr[h] = S_new

    wide = pl.BlockSpec((C, W), lambda n: (n, 0))
    return _pc(body, name=name, grid=(N,),
               in_specs=[wide, wide, wide, wide, pl.BlockSpec((H, C, C), lambda n: (0, n, 0)),
                         pl.BlockSpec((1, H, Dh), lambda n: (n, 0, 0)),
                         pl.BlockSpec((C, W), lambda n: (n, gz_off)), pl.BlockSpec((1, Dh), lambda n: (0, 0))],
               out_specs=[wide, pl.BlockSpec((1, H, Dh, Dh), lambda n: (n, 0, 0, 0))],
               out_shape=[jax.ShapeDtypeStruct((L, W), BF16), jax.ShapeDtypeStruct((N, H, Dh, Dh), F32)],
               scratch_shapes=[pltpu.VMEM((H, Dh, Dh), F32)],
               dims=("arbitrary",))(qe, ke, u, w, attn, eg, gz, wgn.reshape(1, Dh))


def _gdn_scan_bwd(qe, ke, u, w, attn, eg, gz, gz_col, wgn, states, dy, dy_col, *, name):
    L, W = qe.shape
    H, C = GDN_HEADS, GDN_CHUNK
    Dh = W // H
    N = L // C
    gz_off = gz_col // W
    dy_off = dy_col // W

    def body(qe_ref, ke_ref, u_ref, w_ref, at_ref, eg_ref, gz_ref, wgn_ref, st_ref, dy_ref,
             dqe_ref, dke_ref, du_ref, dw_ref, dat_ref, deg_ref, dgz_ref, dwgn_ref, ds_scr):
        first = pl.program_id(0) == 0

        @pl.when(first)
        def _():
            ds_scr[...] = jnp.zeros_like(ds_scr)

        dwgn = None
        units, cots = [], []
        for h in range(H):
            sl = slice(h * Dh, (h + 1) * Dh)
            units.append((st_ref[0, h], qe_ref[:, sl], ke_ref[:, sl], u_ref[:, sl], w_ref[:, sl], at_ref[h],
                          eg_ref[0, h:h + 1, :], gz_ref[:, sl], wgn_ref[...]))
            cots.append((dy_ref[:, sl].astype(F32), ds_scr[h]))
        _, vjp = jax.vjp(_gdn_step, units)
        (d_units,) = vjp(cots)
        for h, (dS, dqe, dke, du, dw, dat, deg, dgz, dwg) in enumerate(d_units):
            sl = slice(h * Dh, (h + 1) * Dh)
            ds_scr[h] = dS
            dqe_ref[:, sl] = dqe
            dke_ref[:, sl] = dke
            du_ref[:, sl] = du
            dw_ref[:, sl] = dw
            dat_ref[h] = dat
            deg_ref[0, h:h + 1, :] = deg
            dgz_ref[:, sl] = dgz.astype(BF16)
            dwgn = dwg if dwgn is None else dwgn + dwg

        @pl.when(first)
        def _():
            dwgn_ref[...] = dwgn

        @pl.when(jnp.logical_not(first))
        def _():
            dwgn_ref[...] += dwgn

    rev = lambda n: N - 1 - n
    wide = pl.BlockSpec((C, W), lambda n: (rev(n), 0))
    att = pl.BlockSpec((H, C, C), lambda n: (0, rev(n), 0))
    egs = pl.BlockSpec((1, H, Dh), lambda n: (rev(n), 0, 0))
    vec = pl.BlockSpec((1, Dh), lambda n: (0, 0))
    return _pc(body, name=name, grid=(N,),
               in_specs=[wide, wide, wide, wide, att, egs, pl.BlockSpec((C, W), lambda n: (rev(n), gz_off)), vec,
                         pl.BlockSpec((1, H, Dh, Dh), lambda n: (rev(n), 0, 0, 0)),
                         pl.BlockSpec((C, W), lambda n: (rev(n), dy_off))],
               out_specs=[wide, wide, wide, wide, att, egs, wide, vec],
               out_shape=[jax.ShapeDtypeStruct((L, W), F32)] * 4 + [jax.ShapeDtypeStruct((H, L, C), F32),
                                                                   jax.ShapeDtypeStruct((N, H, Dh), F32),
                                                                   jax.ShapeDtypeStruct((L, W), BF16),
                                                                   jax.ShapeDtypeStruct((1, Dh), F32)],
               scratch_shapes=[pltpu.VMEM((H, Dh, Dh), F32)],
               dims=("arbitrary",))(qe, ke, u, w, attn, eg, gz, wgn.reshape(1, Dh), states, dy)


def _sb_scores(z, mask):
    sp = jnp.maximum(z, 0.0) + jnp.log(1.0 + jnp.exp(-jnp.abs(z)))
    lom = -sp if mask is None else jnp.where(mask, -sp, 0.0)
    return lom, z - sp


def _sb_masks(tq, width, dh):
    rr = lax.broadcasted_iota(jnp.int32, (tq, tq), 0)
    cc = lax.broadcasted_iota(jnp.int32, (tq, tq), 1)
    first_head = lax.broadcasted_iota(jnp.int32, (tq, width), 1) < dh
    return cc < rr, jnp.where(rr > cc, 1.0, 0.0).astype(BF16), first_head


def _sb_fwd(qkv, *, name, tq=256):
    L = qkv.shape[0]
    H = SB_HEADS
    width = 2 * (qkv.shape[1] // 3 // H)
    dh = width // 2
    npair = H // 2
    tq = min(tq, L)
    nq = L // tq

    def body(q_ref, k_ref, v_ref, o_ref, c_ref):
        i = pl.program_id(1)
        diag, tri, first_head = _sb_masks(tq, width, dh)
        qp = q_ref[...]
        zero = jnp.zeros_like(qp)
        qs = (jnp.where(first_head, qp, zero), jnp.where(first_head, zero, qp))

        def block(j, carry, mask):
            start = pl.multiple_of(j * tq, tq)
            kj = k_ref[pl.ds(start, tq), :]
            vj = v_ref[pl.ds(start, tq), :]
            zs = [lax.dot_general(qs[hd], kj, _DIMS["nt"], preferred_element_type=F32) for hd in range(2)]
            scores = [_sb_scores(z, mask) for z in zs]
            later = [jnp.dot(lom.astype(BF16), tri, preferred_element_type=F32) for lom, _ in scores]
            out = []
            for hd in range(2):
                c, acc = carry[hd]
                lom, lb = scores[hd]
                a = jnp.exp(lb + (c + later[hd]))
                if mask is not None:
                    a = jnp.where(mask, a, 0.0)
                acc = acc + jnp.dot(a.astype(BF16), vj, preferred_element_type=F32)
                out.append((c + jnp.sum(lom, axis=1, keepdims=True), acc))
            return tuple(out)

        init = tuple((jnp.zeros((tq, 1), F32), jnp.zeros((tq, width), F32)) for _ in range(2))
        carry = block(i, init, diag)
        carry = lax.fori_loop(0, i, lambda t, cr: block(i - 1 - t, cr, None), carry)
        o_ref[...] = jnp.where(first_head, carry[0][1], carry[1][1]).astype(BF16)
        c_ref[0] = carry[0][0]
        c_ref[1] = carry[1][0]

    return _pc(body, name=name, grid=(npair, nq),
               in_specs=[pl.BlockSpec((tq, width), lambda p, i: (i, p)),
                         pl.BlockSpec((L, width), lambda p, i: (0, npair + p)),
                         pl.BlockSpec((L, width), lambda p, i: (0, 2 * npair + p))],
               out_specs=[pl.BlockSpec((tq, width), lambda p, i: (i, p)),
                          pl.BlockSpec((2, tq, 1), lambda p, i: (p, i, 0))],
               out_shape=[jax.ShapeDtypeStruct((L, npair * width), BF16), jax.ShapeDtypeStruct((H, L, 1), F32)],
               dims=("parallel", "parallel"))(qkv, qkv, qkv)


def _sb_bwd(qkv, ctot, do, do_col, scale, *, name, tq=256):
    L = qkv.shape[0]
    H = SB_HEADS
    width = 2 * (qkv.shape[1] // 3 // H)
    dh = width // 2
    npair = H // 2
    tq = min(tq, L)
    nq = L // tq
    do_off = do_col // width

    def body(q_ref, k_ref, v_ref, c_ref, do_ref, dq_ref, dk_ref, dv_ref):
        i = pl.program_id(1)

        @pl.when(i == 0)
        def _():
            dk_ref[...] = jnp.zeros_like(dk_ref)
            dv_ref[...] = jnp.zeros_like(dv_ref)

        diag, tri_later, first_head = _sb_masks(tq, width, dh)
        rr = lax.broadcasted_iota(jnp.int32, (tq, tq), 0)
        cc = lax.broadcasted_iota(jnp.int32, (tq, tq), 1)
        tri_before = jnp.where(rr < cc, 1.0, 0.0).astype(BF16)
        qp = q_ref[...]
        dop = do_ref[...].astype(BF16)
        zero = jnp.zeros_like(qp)
        qs = (jnp.where(first_head, qp, zero), jnp.where(first_head, zero, qp))
        dos = (jnp.where(first_head, dop, zero), jnp.where(first_head, zero, dop))
        ctots = (c_ref[0], c_ref[1])

        def block(j, carry, mask):
            start = pl.multiple_of(j * tq, tq)
            kj = k_ref[pl.ds(start, tq), :]
            vj = v_ref[pl.ds(start, tq), :]
            heads = range(2)
            zs = [lax.dot_general(qs[hd], kj, _DIMS["nt"], preferred_element_type=F32) for hd in heads]
            das = [lax.dot_general(dos[hd], vj, _DIMS["nt"], preferred_element_type=F32) for hd in heads]
            scores = [_sb_scores(z, mask) for z in zs]
            later = [jnp.dot(lom.astype(BF16), tri_later, preferred_element_type=F32) for lom, _ in scores]
            pcs = [carry[hd][0] + jnp.sum(scores[hd][0], axis=1, keepdims=True) for hd in heads]
            avals = []
            for hd in heads:
                a = jnp.exp(scores[hd][1] + ((ctots[hd] - pcs[hd]) + later[hd]))
                avals.append(a if mask is None else jnp.where(mask, a, 0.0))
            gs = [das[hd] * avals[hd] for hd in heads]
            before = [jnp.dot(g.astype(BF16), tri_before, preferred_element_type=F32) for g in gs]
            dzs = []
            for hd in heads:
                sig = jnp.exp(scores[hd][1])
                dz = gs[hd] * (1.0 - sig) - (carry[hd][1] + before[hd]) * sig
                dzs.append((dz if mask is None else jnp.where(mask, dz, 0.0)).astype(BF16))
            dqs = [carry[hd][2] + jnp.dot(dzs[hd], kj, preferred_element_type=F32) for hd in heads]
            dk_add = sum(lax.dot_general(dzs[hd], qs[hd], _DIMS["tn"], preferred_element_type=F32) for hd in heads)
            dv_add = sum(lax.dot_general(avals[hd].astype(BF16), dos[hd], _DIMS["tn"], preferred_element_type=F32)
                         for hd in heads)
            dk_ref[pl.ds(start, tq), :] += dk_add
            dv_ref[pl.ds(start, tq), :] += dv_add
            return tuple((pcs[hd], carry[hd][1] + jnp.sum(gs[hd], axis=1, keepdims=True), dqs[hd]) for hd in heads)

        col = jnp.zeros((tq, 1), F32)
        init = tuple((col, col, jnp.zeros((tq, width), F32)) for _ in range(2))
        carry = lax.fori_loop(0, i, lambda j, cr: block(j, cr, None), init)
        carry = block(i, carry, diag)
        dq_ref[...] = (jnp.where(first_head, carry[0][2], carry[1][2]) * scale).astype(BF16)

    tile = pl.BlockSpec((tq, width), lambda p, i: (i, p))
    full = pl.BlockSpec((L, width), lambda p, i: (0, p))
    sds = jax.ShapeDtypeStruct((L, npair * width), F32)
    return _pc(body, name=name, grid=(npair, nq),
               in_specs=[tile, pl.BlockSpec((L, width), lambda p, i: (0, npair + p)),
                         pl.BlockSpec((L, width), lambda p, i: (0, 2 * npair + p)),
                         pl.BlockSpec((2, tq, 1), lambda p, i: (p, i, 0)),
                         pl.BlockSpec((tq, width), lambda p, i: (i, do_off + p))],
               out_specs=[tile, full, full],
               out_shape=[jax.ShapeDtypeStruct((L, npair * width), BF16), sds, sds],
               dims=("parallel", "arbitrary"))(qkv, qkv, qkv, ctot, do)


def _adamw(w, g, m, v, *, name, tm=256):
    R, C = w.shape
    tm = min(tm, R)
    assert R % tm == 0, (R, tm)
    c1 = 1.0 - ADAM_B1 ** ADAM_STEP
    c2 = 1.0 - ADAM_B2 ** ADAM_STEP

    def body(w_ref, g_ref, m_ref, v_ref, d_ref, nm_ref, nv_ref):
        gv = g_ref[...]
        nm = ADAM_B1 * m_ref[...] + (1.0 - ADAM_B1) * gv
        nv = ADAM_B2 * v_ref[...] + (1.0 - ADAM_B2) * (gv * gv)
        d_ref[...] = -ADAM_LR * ((nm / c1) / (jnp.sqrt(nv / c2) + ADAM_EPS) + ADAM_WD * w_ref[...])
        nm_ref[...] = nm
        nv_ref[...] = nv

    blk = pl.BlockSpec((tm, C), lambda i: (i, 0))
    sds = jax.ShapeDtypeStruct((R, C), F32)
    return _pc(body, name=name, grid=(R // tm,), in_specs=[blk] * 4, out_specs=[blk] * 3, out_shape=[sds] * 3,
               dims=("parallel",))(w, g, m, v)


def _add_halves(gbuf, ra, c, *, name, tm=256):
    _, S, Rh, C = gbuf.shape
    assert Rh % tm == 0, (Rh, tm)

    def body(c_ref, g_ref, r_ref, o_ref):
        o_ref[...] = (g_ref[...] + r_ref[...]).astype(BF16)

    grid_spec = pltpu.PrefetchScalarGridSpec(
        num_scalar_prefetch=1, grid=(S, Rh // tm),
        in_specs=[pl.BlockSpec((None, None, tm, C), lambda s, i, c_ref: (c_ref[0], s, i, 0)),
                  pl.BlockSpec((None, tm, C), lambda s, i, c_ref: (s, i, 0))],
        out_specs=pl.BlockSpec((None, tm, C), lambda s, i, c_ref: (s, i, 0)))
    return _pc_prefetch(body, name=name, grid_spec=grid_spec, out_shape=jax.ShapeDtypeStruct((S, Rh, C), BF16),
                        dims=("parallel", "parallel"))(c.reshape(1).astype(jnp.int32), gbuf, ra)


def _add_chips(p, rb, chip, *, name, tm=256):
    S, Rh, C = p.shape
    assert Rh % tm == 0, (Rh, tm)

    def body(s_ref, p_ref, r_ref, o_ref):
        o_ref[...] = ((p_ref[...].astype(F32) + r_ref[0].astype(F32)) + r_ref[1].astype(F32)) + r_ref[2].astype(F32)

    grid_spec = pltpu.PrefetchScalarGridSpec(
        num_scalar_prefetch=1, grid=(Rh // tm,),
        in_specs=[pl.BlockSpec((None, tm, C), lambda i, s_ref: (s_ref[0], i, 0)),
                  pl.BlockSpec((3, tm, C), lambda i, s_ref: (0, i, 0))],
        out_specs=pl.BlockSpec((tm, C), lambda i, s_ref: (i, 0)))
    return _pc_prefetch(body, name=name, grid_spec=grid_spec, out_shape=jax.ShapeDtypeStruct((Rh, C), F32),
                        dims=("parallel",))(chip.reshape(1).astype(jnp.int32), p, rb)


def _sum_slots(g, *, name):
    n, R, C = g.shape

    def body(g_ref, o_ref):
        acc = g_ref[0]
        for s in range(1, n):
            acc = acc + g_ref[s]
        o_ref[...] = acc

    return _pc(body, name=name, grid=(1,), in_specs=[pl.BlockSpec((n, R, C), lambda i: (0, 0, 0))],
               out_specs=pl.BlockSpec((R, C), lambda i: (0, 0)), out_shape=jax.ShapeDtypeStruct((R, C), F32),
               dims=("arbitrary",))(g)


ANY = pl.BlockSpec(memory_space=pl.ANY)


def _place():
    return lax.axis_index("x"), lax.axis_index("y"), lax.axis_index("c")


def _other_chips(x, y):
    return [(1 - x, y), (x, 1 - y), (1 - x, 1 - y)]


def _allgather_chips(w, *, name):
    _, Rh, C = w.shape

    def body(w_ref, out_ref, send_sems, recv_sems):
        x, y, c = _place()
        sib = (x, y, 1 - c)
        chips = _other_chips(x, y)

        def copy(k, chip_id, half, to):
            return pltpu.make_async_remote_copy(src_ref=w_ref.at[half] if k < 3 else out_ref.at[chip_id, half],
                                                dst_ref=out_ref.at[chip_id, half], send_sem=send_sems.at[k],
                                                recv_sem=recv_sems.at[k], device_id=to, device_id_type=MESH)

        sends = [copy(j, 2 * x + y, c, (px, py, c)) for j, (px, py) in enumerate(chips)]
        for cp in sends:
            cp.start()
        passed = []
        for j, (px, py) in enumerate(chips):
            copy(j, 2 * px + py, c, (px, py, c)).wait_recv()
            fwd = copy(3 + j, 2 * px + py, c, sib)
            fwd.start()
            passed.append(fwd)
        for j, (px, py) in enumerate(chips):
            copy(3 + j, 2 * px + py, 1 - c, sib).wait_recv()
        for cp in sends + passed:
            cp.wait_send()

    return _pc_comm(body, name=name, in_specs=[ANY], out_specs=ANY,
                    out_shape=jax.ShapeDtypeStruct((N_CHIPS, 2, Rh, C), w.dtype),
                    scratch_shapes=[pltpu.SemaphoreType.DMA((6,)), pltpu.SemaphoreType.DMA((6,))])(w)


def _send_half_to_sibling(gbuf, *, name):
    _, S, Rh, C = gbuf.shape

    def body(g_ref, ra_ref, send_sem, recv_sem):
        x, y, c = _place()
        cp = pltpu.make_async_remote_copy(src_ref=g_ref.at[1 - c], dst_ref=ra_ref, send_sem=send_sem,
                                          recv_sem=recv_sem, device_id=(x, y, 1 - c), device_id_type=MESH)
        cp.start()
        cp.wait()

    return _pc_comm(body, name=name, in_specs=[ANY], out_specs=ANY, out_shape=jax.ShapeDtypeStruct((S, Rh, C), F32),
                    scratch_shapes=[pltpu.SemaphoreType.DMA, pltpu.SemaphoreType.DMA])(gbuf)


def _scatter_to_chips(p, *, name):
    S, Rh, C = p.shape

    def body(p_ref, rb_ref, send_sems, recv_sems):
        x, y, c = _place()
        chips = _other_chips(x, y)

        def copy(j, shard, to):
            return pltpu.make_async_remote_copy(src_ref=p_ref.at[shard], dst_ref=rb_ref.at[j], send_sem=send_sems.at[j],
                                                recv_sem=recv_sems.at[j], device_id=to, device_id_type=MESH)

        sends = [copy(j, 2 * px + py, (px, py, c)) for j, (px, py) in enumerate(chips)]
        for cp in sends:
            cp.start()
        for cp in sends:
            cp.wait()

    return _pc_comm(body, name=name, in_specs=[ANY], out_specs=ANY, out_shape=jax.ShapeDtypeStruct((3, Rh, C), p.dtype),
                    scratch_shapes=[pltpu.SemaphoreType.DMA((3,)), pltpu.SemaphoreType.DMA((3,))])(p)


def _swap_with_sibling(f, *, name):
    Rh, C = f.shape

    def body(f_ref, out_ref, send_sem, recv_sem):
        x, y, c = _place()
        cp = pltpu.make_async_remote_copy(src_ref=f_ref, dst_ref=out_ref, send_sem=send_sem, recv_sem=recv_sem,
                                          device_id=(x, y, 1 - c), device_id_type=MESH)
        cp.start()
        cp.wait()

    return _pc_comm(body, name=name, in_specs=[ANY], out_specs=ANY, out_shape=jax.ShapeDtypeStruct((Rh, C), F32),
                    scratch_shapes=[pltpu.SemaphoreType.DMA, pltpu.SemaphoreType.DMA])(f)


def _allgather_devices(v, *, name):
    R, C = v.shape

    def body(v_ref, out_ref, send_sems, recv_sems):
        x, y, c = _place()
        me = 4 * x + 2 * y + c
        out_ref[me] = v_ref[...]
        peers = []
        for k in range(1, 8):
            fx, fy, fc = (k >> 2) & 1, (k >> 1) & 1, k & 1
            px = 1 - x if fx else x
            py = 1 - y if fy else y
            pcc = 1 - c if fc else c
            peers.append((px, py, pcc))
        sends = []
        for k, peer in enumerate(peers):
            cp = pltpu.make_async_remote_copy(src_ref=v_ref, dst_ref=out_ref.at[me], send_sem=send_sems.at[k],
                                              recv_sem=recv_sems.at[k], device_id=peer, device_id_type=MESH)
            cp.start()
            sends.append(cp)
        for k, (px, py, pcc) in enumerate(peers):
            pltpu.make_async_remote_copy(src_ref=v_ref, dst_ref=out_ref.at[4 * px + 2 * py + pcc],
                                         send_sem=send_sems.at[k], recv_sem=recv_sems.at[k], device_id=peers[k],
                                         device_id_type=MESH).wait_recv()
        for cp in sends:
            cp.wait_send()

    vm = pl.BlockSpec(memory_space=pltpu.VMEM)
    return _pc_comm(body, name=name, in_specs=[vm], out_specs=vm, out_shape=jax.ShapeDtypeStruct((8, R, C), F32),
                    scratch_shapes=[pltpu.SemaphoreType.DMA((7,)), pltpu.SemaphoreType.DMA((7,))])(v)


D_MODEL = 1024
SC_W = D_MODEL // 4
GDN_W = D_MODEL // 2
SB_W = D_MODEL - SC_W - GDN_W
D_FF = 256 * ((8 * D_MODEL // 3 + 255) // 256)
O_SC, O_GQKV, O_GZ, O_GA, O_GB, O_SB = 0, 3 * SC_W, 3 * SC_W + 3 * GDN_W, 3 * SC_W + 4 * GDN_W, \
    3 * SC_W + 4 * GDN_W + GDN_HEADS, 3 * SC_W + 4 * GDN_W + 2 * GDN_HEADS
D_IN_PROJ = O_SB + 3 * SB_W
P_GQKV, P_SC, P_SB = 0, 3 * GDN_W, 3 * GDN_W + 3 * SC_W
P_GZ = P_SB + 3 * SB_W
P_GAB = P_GZ + GDN_W
P_PAD = 256
P_WIDTH = P_GAB + P_PAD


def _proj_to_kernel_layout(w):
    pad = jnp.zeros((w.shape[0], P_PAD - 2 * GDN_HEADS), w.dtype)
    return jnp.concatenate([w[:, O_GQKV:O_GZ], w[:, O_SC:O_GQKV], w[:, O_SB:], w[:, O_GZ:O_GA], w[:, O_GA:O_SB], pad],
                           axis=1)


def _proj_from_kernel_layout(g):
    return jnp.concatenate([g[:, P_SC:P_SB], g[:, P_GQKV:P_SC], g[:, P_GZ:P_GAB], g[:, P_GAB:P_GAB + 2 * GDN_HEADS],
                            g[:, P_SB:P_GZ]], axis=1)


def _mixout_to_kernel_layout(w):
    return jnp.concatenate([w[SC_W:SC_W + GDN_W], w[:SC_W], w[SC_W + GDN_W:]], axis=0)


def _mixout_from_kernel_layout(g):
    return jnp.concatenate([g[GDN_W:GDN_W + SC_W], g[:GDN_W], g[GDN_W + SC_W:]], axis=0)


def _pack_rows(parts, width, rows_to):
    flat = jnp.concatenate([p.reshape(-1, width) for p in parts], axis=0)
    return jnp.pad(flat, ((0, rows_to - flat.shape[0]), (0, 0)))


def _unpack_rows(flat, shapes, width):
    out, r = [], 0
    for shp in shapes:
        n = int(np.prod(shp)) // width
        out.append(flat[r:r + n].reshape(shp))
        r += n
    return out


def _pack_vec(parts, rows_to):
    flat = jnp.concatenate([p.reshape(-1) for p in parts])
    return jnp.pad(flat, (0, rows_to * LANES - flat.shape[0])).reshape(rows_to, LANES)


def _unpack_vec(mat, shapes):
    flat = mat.reshape(-1)
    out, r = [], 0
    for shp in shapes:
        n = int(np.prod(shp))
        out.append(flat[r:r + n].reshape(shp))
        r += n
    return out


def _round_up(n, m):
    return (n + m - 1) // m * m


def _layer_fwd(x, p, l):
    L = x.shape[0]
    tag = "l%d_" % l
    h = _rmsnorm_fwd(x, p["wn_mix"], name=tag + "norm_mix")
    proj = _matmul(h, p["w_in"], "nn", tm=512, tn=768, tk=D_MODEL, name=tag + "proj")
    (y_sc,) = _conv_pointwise_fwd([(proj, P_SC + SC_W), (proj, P_SC + 2 * SC_W)], [(p["w_sconv"], 0)], [(proj, P_SC)],
                                  _pre_product, _post_gate_mul, [(SC_W, BF16)], tc=SC_W, tm=512, name=tag + "sconv")
    (qkv,) = _conv_pointwise_fwd([(proj, P_GQKV)], [(p["w_gdn_conv"], 0)], [], _pre_identity, _post_silu,
                                 [(3 * GDN_W, F32)], tc=GDN_W, tm=512, name=tag + "gdn_conv")
    qe, ke, u, w, attn, eg = _gdn_prep_fwd(qkv, proj, P_GAB, p["a_log"], p["dt_bias"], name=tag + "gdn_prep")
    y_gdn, states = _gdn_scan_fwd(qe, ke, u, w, attn, eg, proj, P_GZ, p["wgn"], name=tag + "gdn_scan")
    sb_scale = (SB_W // SB_HEADS) ** -0.5
    sbqkv = jnp.concatenate([proj[:, P_SB:P_SB + SB_W] * sb_scale, proj[:, P_SB + SB_W:P_SB + 3 * SB_W]],
                            axis=1).astype(BF16)
    y_sb, ctot = _sb_fwd(sbqkv, name=tag + "sb_fwd")
    y_cat = [y_gdn, y_sc, y_sb]
    x2 = _matmul_rows_parts(y_cat, p["w_out"], "nn", res=x, name=tag + "mix_out")
    h2 = _rmsnorm_fwd(x2, p["wn_ffn"], name=tag + "norm_ffn")
    up_g = _matmul(h2, p["w_up_g"], "nn", tm=512, tn=D_FF // 2, tk=D_MODEL, name=tag + "up_gate")
    up_v = _matmul(h2, p["w_up_v"], "nn", tm=512, tn=D_FF // 2, tk=D_MODEL, name=tag + "up_val")
    (act,) = _conv_pointwise_fwd([(up_g, 0), (up_v, 0)], [(p["w_fconv_g"], 0), (p["w_fconv_v"], 0)], [],
                                 _pre_identity, _post_swiglu, [(D_FF, BF16)], tc=256, tm=512, name=tag + "ffn_act")
    x3 = _matmul(act, p["w_down"], "nn", tm=512, tn=D_MODEL, tk=D_FF // 2, res=x2, name=tag + "ffn_down")
    saved = dict(x=x, h=h, proj=proj, qkv=qkv, qe=qe, ke=ke, u=u, w=w, attn=attn, eg=eg, states=states,
                 sbqkv=sbqkv, ctot=ctot, y_cat=y_cat, x2=x2, h2=h2, up_g=up_g, up_v=up_v, act=act)
    return x3, saved


def _layer_bwd(dx3, p, s, l):
    L = dx3.shape[0]
    tag = "l%d_b_" % l
    g = {}
    dact = _matmul(dx3, p["w_down"], "nt", tm=512, tn=D_FF // 2, tk=D_MODEL, name=tag + "dact")
    g["w_down"] = _matmul(s["act"], dx3, "tn", tm=D_FF // 2, tn=D_MODEL, tk=512, name=tag + "dw_down")
    (dup_g, dup_v), _, (g["w_fconv_g"], g["w_fconv_v"]) = _conv_pointwise_bwd(
        [(s["up_g"], 0), (s["up_v"], 0)], [(p["w_fconv_g"], 0), (p["w_fconv_v"], 0)], [], [(dact, 0)],
        _pre_identity, _post_swiglu, D_FF, tc=256, tm=512, name=tag + "ffn_act")
    dh2 = _matmul(dup_g, p["w_up_g"], "nt", tm=512, tn=D_MODEL, tk=D_FF // 2, name=tag + "dh2_gate")
    dh2 = _matmul(dup_v, p["w_up_v"], "nt", tm=512, tn=D_MODEL, tk=D_FF // 2, res=dh2, name=tag + "dh2_val")
    g["w_up_g"] = _matmul(s["h2"], dup_g, "tn", tm=D_MODEL, tn=D_FF // 2, tk=512, name=tag + "dw_up_gate")
    g["w_up_v"] = _matmul(s["h2"], dup_v, "tn", tm=D_MODEL, tn=D_FF // 2, tk=512, name=tag + "dw_up_val")
    dx2, g["wn_ffn"] = _rmsnorm_bwd(dh2, s["x2"], p["wn_ffn"], dx3, name=tag + "norm_ffn")
    dycat = _matmul(dx2, p["w_out"], "nt", tm=512, tn=D_MODEL, tk=D_MODEL, name=tag + "dycat")
    g["w_out"] = _matmul_tn_parts(s["y_cat"], dx2, name=tag + "dw_out")
    sb_scale = (SB_W // SB_HEADS) ** -0.5
    dsq, dsk, dsv = _sb_bwd(s["sbqkv"], s["ctot"], dycat, GDN_W + SC_W, sb_scale, name=tag + "sb_bwd")
    dqe, dke, du, dw, dattn, deg, dgz, g["wgn"] = _gdn_scan_bwd(
        s["qe"], s["ke"], s["u"], s["w"], s["attn"], s["eg"], s["proj"], P_GZ, p["wgn"], s["states"], dycat, 0,
        name=tag + "gdn_scan")
    dqkv_act, dgab, g["a_log"], g["dt_bias"] = _gdn_prep_bwd(
        s["qkv"], s["proj"], P_GAB, p["a_log"], p["dt_bias"], dqe, dke, du, dw, dattn, deg, P_PAD,
        name=tag + "gdn_prep")
    (dqkv,), _, (g["w_gdn_conv"],) = _conv_pointwise_bwd(
        [(s["proj"], P_GQKV)], [(p["w_gdn_conv"], 0)], [], [(dqkv_act, 0)], _pre_identity, _post_silu, 3 * GDN_W,
        tc=GDN_W, tm=512, name=tag + "gdn_conv")
    (dsc_c, dsc_h), (dsc_b,), (g["w_sconv"],) = _conv_pointwise_bwd(
        [(s["proj"], P_SC + SC_W), (s["proj"], P_SC + 2 * SC_W)], [(p["w_sconv"], 0)], [(s["proj"], P_SC)],
        [(dycat, GDN_W)], _pre_product, _post_gate_mul, SC_W, tc=SC_W, tm=512, name=tag + "sconv")
    dproj = [dqkv, dsc_b, dsc_c, dsc_h, dsq, dsk, dsv, dgz, dgab]
    dh = _matmul_rows_parts(dproj, p["w_in"], "nt", name=tag + "dh")
    g["w_in"] = jnp.concatenate([_matmul_tn_parts(s["h"], dproj[:4], name=tag + "dw_in_a"),
                                 _matmul_tn_parts(s["h"], dproj[4:], name=tag + "dw_in_b")], axis=1)
    dx, g["wn_mix"] = _rmsnorm_bwd(dh, s["x"], p["wn_mix"], dx2, name=tag + "norm_mix")
    return dx, g


BIG = ("w_mix_in", "w_mix_out", "w_ffn_up", "w_ffn_down")
BIG_AXIS = {"w_mix_in": 2, "w_mix_out": 1, "w_ffn_up": 2, "w_ffn_down": 1}
SMALL_SHARDED = ("w_sconv", "w_gdn_conv", "w_ffn_conv")
SMALL_REPLICATED = ("w_norm_mix", "gdn_a_log", "gdn_dt_bias", "w_gdn_norm", "w_norm_ffn", "w_norm_final")
WEIGHTS = ("w_norm_mix", "w_mix_in", "w_sconv", "w_gdn_conv", "gdn_a_log", "gdn_dt_bias", "w_gdn_norm", "w_mix_out",
           "w_norm_ffn", "w_ffn_up", "w_ffn_conv", "w_ffn_down", "w_norm_final")


def _big_parts(d):
    depth = d["w_mix_in"].shape[0]
    return [d[n][l] for l in range(depth) for n in BIG]


def kernel(x, w_norm_mix, w_mix_in, w_sconv, w_gdn_conv, gdn_a_log, gdn_dt_bias, w_gdn_norm, w_mix_out, w_norm_ffn, w_ffn_up, w_ffn_conv, w_ffn_down, w_norm_final, loss_target, m_w_norm_mix, m_w_mix_in, m_w_sconv, m_w_gdn_conv, m_gdn_a_log, m_gdn_dt_bias, m_w_gdn_norm, m_w_mix_out, m_w_norm_ffn, m_w_ffn_up, m_w_ffn_conv, m_w_ffn_down, m_w_norm_final, v_w_norm_mix, v_w_mix_in, v_w_sconv, v_w_gdn_conv, v_gdn_a_log, v_gdn_dt_bias, v_w_gdn_norm, v_w_mix_out, v_w_norm_ffn, v_w_ffn_up, v_w_ffn_conv, v_w_ffn_down, v_w_norm_final):
    W = dict(w_norm_mix=w_norm_mix, w_mix_in=w_mix_in, w_sconv=w_sconv, w_gdn_conv=w_gdn_conv, gdn_a_log=gdn_a_log,
             gdn_dt_bias=gdn_dt_bias, w_gdn_norm=w_gdn_norm, w_mix_out=w_mix_out, w_norm_ffn=w_norm_ffn,
             w_ffn_up=w_ffn_up, w_ffn_conv=w_ffn_conv, w_ffn_down=w_ffn_down, w_norm_final=w_norm_final)
    M = dict(w_norm_mix=m_w_norm_mix, w_mix_in=m_w_mix_in, w_sconv=m_w_sconv, w_gdn_conv=m_w_gdn_conv,
             gdn_a_log=m_gdn_a_log, gdn_dt_bias=m_gdn_dt_bias, w_gdn_norm=m_w_gdn_norm, w_mix_out=m_w_mix_out,
             w_norm_ffn=m_w_norm_ffn, w_ffn_up=m_w_ffn_up, w_ffn_conv=m_w_ffn_conv, w_ffn_down=m_w_ffn_down,
             w_norm_final=m_w_norm_final)
    V = dict(w_norm_mix=v_w_norm_mix, w_mix_in=v_w_mix_in, w_sconv=v_w_sconv, w_gdn_conv=v_w_gdn_conv,
             gdn_a_log=v_gdn_a_log, gdn_dt_bias=v_gdn_dt_bias, w_gdn_norm=v_w_gdn_norm, w_mix_out=v_w_mix_out,
             w_norm_ffn=v_w_norm_ffn, w_ffn_up=v_w_ffn_up, w_ffn_conv=v_w_ffn_conv, w_ffn_down=v_w_ffn_down,
             w_norm_final=v_w_norm_final)
    depth = w_mix_in.shape[0]
    L = x.shape[1]
    mx, my, mc = lax.axis_index("x"), lax.axis_index("y"), lax.axis_index("c")
    chip = 2 * mx + my

    big_shapes = [a.shape for a in _big_parts(W)]
    rows = sum(int(np.prod(s)) // D_MODEL for s in big_shapes)
    rows_pad = _round_up(rows, 512)
    rh = rows_pad // 2
    w_flat = _pack_rows(_big_parts(W), D_MODEL, rows_pad)
    w_own = w_flat.astype(BF16).reshape(2, rh, D_MODEL)
    gathered = _allgather_chips(w_own, name="gather_big")
    gathered = lax.dynamic_update_slice(gathered, w_own[None], (chip, 0, 0, 0))
    gathered = gathered.reshape(N_CHIPS, rows_pad, D_MODEL)
    per_chip = [_unpack_rows(gathered[b], big_shapes, D_MODEL) for b in range(N_CHIPS)]
    full_big = []
    for l in range(depth):
        lay = {}
        for n_i, n in enumerate(BIG):
            lay[n] = jnp.concatenate([per_chip[b][l * len(BIG) + n_i] for b in range(N_CHIPS)], axis=BIG_AXIS[n] - 1)
        full_big.append(lay)

    small_sh_shapes = [W[n].shape for n in SMALL_SHARDED]
    n_small_sh = sum(int(np.prod(s)) for s in small_sh_shapes)
    small_rows = _round_up(n_small_sh, 8 * LANES) // LANES
    small_all = _allgather_devices(_pack_vec([W[n] for n in SMALL_SHARDED], small_rows), name="gather_small")
    small_chip = [_unpack_vec(small_all[2 * b], small_sh_shapes) for b in range(N_CHIPS)]
    full_small = {n: jnp.concatenate([small_chip[b][i] for b in range(N_CHIPS)], axis=2)
                  for i, n in enumerate(SMALL_SHARDED)}

    params = []
    for l in range(depth):
        w_up = full_big[l]["w_ffn_up"]
        fconv = full_small["w_ffn_conv"][l]
        params.append(dict(
            wn_mix=w_norm_mix[l], w_in=_proj_to_kernel_layout(full_big[l]["w_mix_in"]),
            w_sconv=full_small["w_sconv"][l], w_gdn_conv=full_small["w_gdn_conv"][l],
            a_log=gdn_a_log[l].reshape(GDN_HEADS, 1, 1), dt_bias=gdn_dt_bias[l].reshape(GDN_HEADS, 1, 1),
            wgn=w_gdn_norm[l], w_out=_mixout_to_kernel_layout(full_big[l]["w_mix_out"]), wn_ffn=w_norm_ffn[l],
            w_up_g=w_up[:, :D_FF], w_up_v=w_up[:, D_FF:], w_fconv_g=fconv[:, :D_FF], w_fconv_v=fconv[:, D_FF:],
            w_down=full_big[l]["w_ffn_down"]))

    xs = x[0]
    saved = []
    for l in range(depth):
        xs, s = _layer_fwd(xs, params[l], l)
        saved.append(s)
    loss_row, dx, g_norm_final = _final_loss(xs, w_norm_final, loss_target[0], name="final_loss")
    grads = [None] * depth
    for l in reversed(range(depth)):
        dx, grads[l] = _layer_bwd(dx, params[l], saved[l], l)
    loss = lax.psum(loss_row[0, 0], ("x", "y", "c"))

    G = {
        "w_mix_in": jnp.stack([_proj_from_kernel_layout(grads[l]["w_in"]) for l in range(depth)]),
        "w_mix_out": jnp.stack([_mixout_from_kernel_layout(grads[l]["w_out"]) for l in range(depth)]),
        "w_ffn_up": jnp.stack([jnp.concatenate([grads[l]["w_up_g"], grads[l]["w_up_v"]], axis=1)
                               for l in range(depth)]),
        "w_ffn_down": jnp.stack([grads[l]["w_down"] for l in range(depth)]),
        "w_sconv": jnp.stack([grads[l]["w_sconv"] for l in range(depth)]),
        "w_gdn_conv": jnp.stack([grads[l]["w_gdn_conv"] for l in range(depth)]),
        "w_ffn_conv": jnp.stack([jnp.concatenate([grads[l]["w_fconv_g"], grads[l]["w_fconv_v"]], axis=1)
                                 for l in range(depth)]),
        "w_norm_mix": jnp.stack([grads[l]["wn_mix"].reshape(-1) for l in range(depth)]),
        "gdn_a_log": jnp.stack([grads[l]["a_log"].reshape(-1) for l in range(depth)]),
        "gdn_dt_bias": jnp.stack([grads[l]["dt_bias"].reshape(-1) for l in range(depth)]),
        "w_gdn_norm": jnp.stack([grads[l]["wgn"].reshape(-1) for l in range(depth)]),
        "w_norm_ffn": jnp.stack([grads[l]["wn_ffn"].reshape(-1) for l in range(depth)]),
        "w_norm_final": g_norm_final.reshape(-1),
    }

    def shard_of(a, n, b):
        width = a.shape[BIG_AXIS[n]] // N_CHIPS
        return lax.slice_in_dim(a, b * width, (b + 1) * width, axis=BIG_AXIS[n])

    gbuf = jnp.stack([_pack_rows(_big_parts({n: shard_of(G[n], n, b) for n in BIG}), D_MODEL, rows_pad)
                      for b in range(N_CHIPS)])
    gbuf = gbuf.reshape(N_CHIPS, 2, rh, D_MODEL).transpose(1, 0, 2, 3)
    from_sibling = _send_half_to_sibling(gbuf, name="rs_sibling")
    chip_sum = _add_halves(gbuf, from_sibling, mc, name="rs_add_halves")
    from_chips = _scatter_to_chips(chip_sum, name="rs_chips")
    my_half = _add_chips(chip_sum, from_chips, chip, name="rs_add_chips")
    sib_half = _swap_with_sibling(my_half, name="rs_result")
    g_flat = jnp.where(mc == 0, jnp.stack([my_half, sib_half]), jnp.stack([sib_half, my_half]))
    g_flat = g_flat.reshape(rows_pad, D_MODEL)
    m_flat = _pack_rows(_big_parts(M), D_MODEL, rows_pad)
    v_flat = _pack_rows(_big_parts(V), D_MODEL, rows_pad)
    d_flat, nm_flat, nv_flat = _adamw(w_flat, g_flat, m_flat, v_flat, name="adamw_big", tm=256)
    out_g, out_d, out_m, out_v = {}, {}, {}, {}
    for flat, dst in ((g_flat, out_g), (d_flat, out_d), (nm_flat, out_m), (nv_flat, out_v)):
        parts = _unpack_rows(flat, big_shapes, D_MODEL)
        for n_i, n in enumerate(BIG):
            dst[n] = jnp.stack([parts[l * len(BIG) + n_i] for l in range(depth)])

    small_names = SMALL_SHARDED + SMALL_REPLICATED
    small_full_shapes = [G[n].shape for n in small_names]
    n_small = sum(int(np.prod(s)) for s in small_full_shapes)
    red_rows = _round_up(n_small, 8 * LANES) // LANES
    partials = _allgather_devices(_pack_vec([G[n] for n in small_names], red_rows), name="reduce_small")
    summed = _unpack_vec(_sum_slots(partials, name="reduce_small_sum"), small_full_shapes)
    g_small = {}
    for n, a in zip(small_names, summed):
        if n in SMALL_SHARDED:
            width = a.shape[2] // N_CHIPS
            a = lax.dynamic_slice_in_dim(a, chip * width, width, axis=2)
        g_small[n] = a
    own_shapes = [W[n].shape for n in small_names]
    n_own = sum(int(np.prod(s)) for s in own_shapes)
    own_rows = _round_up(n_own, 8 * LANES) // LANES
    packed = [_pack_vec([src[n] for n in small_names], own_rows) for src in (W, g_small, M, V)]
    d_s, nm_s, nv_s = _adamw(*packed, name="adamw_small", tm=own_rows)
    for mat, dst in ((packed[1], out_g), (d_s, out_d), (nm_s, out_m), (nv_s, out_v)):
        for n, a in zip(small_names, _unpack_vec(mat, own_shapes)):
            dst[n] = a

    outs = [loss, dx[None]]
    for dst in (out_g, out_d, out_m, out_v):
        outs += [dst[n] for n in WEIGHTS]
    return tuple(outs)
```

```python
import functools

import jax
import jax.numpy as jnp
import numpy as np
from jax import lax
from jax.experimental import pallas as pl
from jax.experimental.pallas import tpu as pltpu

F32 = jnp.float32
BF16 = jnp.bfloat16
MESH = pl.DeviceIdType.MESH

NORM_EPS = 1e-6
GDN_HEADS = 4
GDN_CHUNK = 64
GDN_CONV = 4
SB_HEADS = 4
SB_UNROLL = 4
SC_KERNEL = 3
FFN_CONV = 3
ADAM_LR = 0.001
ADAM_B1 = 0.9
ADAM_B2 = 0.999
ADAM_EPS = 1e-08
ADAM_WD = 0.01
ADAM_STEP = 10

VMEM_LIMIT_BYTES = 48 * 1024 * 1024
HALO = 8
LANES = 128
N_CHIPS = 4


def _pc(body, *, name, grid, in_specs, out_specs, out_shape, scratch_shapes=(), dims=None):
    params = dict(vmem_limit_bytes=VMEM_LIMIT_BYTES)
    if dims is not None:
        params["dimension_semantics"] = dims
    return pl.pallas_call(body, name=name, grid=grid, in_specs=in_specs, out_specs=out_specs, out_shape=out_shape,
                          scratch_shapes=list(scratch_shapes), compiler_params=pltpu.CompilerParams(**params))


def _pc_prefetch(body, *, name, grid_spec, out_shape, dims):
    return pl.pallas_call(body, name=name, grid_spec=grid_spec, out_shape=out_shape,
                          compiler_params=pltpu.CompilerParams(vmem_limit_bytes=VMEM_LIMIT_BYTES,
                                                               dimension_semantics=dims))


def _pc_comm(body, *, name, in_specs, out_specs, out_shape, scratch_shapes):
    return pl.pallas_call(body, name=name, in_specs=in_specs, out_specs=out_specs, out_shape=out_shape,
                          scratch_shapes=list(scratch_shapes),
                          compiler_params=pltpu.CompilerParams(vmem_limit_bytes=VMEM_LIMIT_BYTES))


_DIMS = {"nn": (((1,), (0,)), ((), ())), "nt": (((1,), (1,)), ((), ())), "tn": (((0,), (0,)), ((), ()))}


def _matmul(a, b, mode, *, name, tm=512, tn=512, tk=512, out_dtype=F32, res=None):
    if mode == "nn":
        (M, K), (K2, N) = a.shape, b.shape
    elif mode == "nt":
        (M, K), (N, K2) = a.shape, b.shape
    else:
        (K, M), (K2, N) = a.shape, b.shape
    assert K == K2, (a.shape, b.shape, mode)
    tm, tn, tk = min(tm, M), min(tn, N), min(tk, K)
    assert M % tm == 0 and N % tn == 0 and K % tk == 0, (M, N, K, tm, tn, tk)
    nk = K // tk
    if mode == "tn":
        a_spec = pl.BlockSpec((tk, tm), lambda i, j, k: (k, i))
    else:
        a_spec = pl.BlockSpec((tm, tk), lambda i, j, k: (i, k))
    if mode == "nt":
        b_spec = pl.BlockSpec((tn, tk), lambda i, j, k: (j, k))
    else:
        b_spec = pl.BlockSpec((tk, tn), lambda i, j, k: (k, j))
    o_spec = pl.BlockSpec((tm, tn), lambda i, j, k: (i, j))
    has_res = res is not None
    dn = _DIMS[mode]

    def body(*refs):
        if has_res:
            a_ref, b_ref, r_ref, o_ref, acc = refs
        else:
            a_ref, b_ref, o_ref, acc = refs
        k = pl.program_id(2)
        p = lax.dot_general(a_ref[...].astype(BF16), b_ref[...].astype(BF16), dn, preferred_element_type=F32)

        def finish(total):
            if has_res:
                total = total + r_ref[...].astype(F32)
            o_ref[...] = total.astype(out_dtype)

        if nk == 1:
            finish(p)
        else:
            @pl.when(k == 0)
            def _():
                acc[...] = p

            @pl.when(k > 0)
            def _():
                acc[...] += p

            @pl.when(k == nk - 1)
            def _():
                finish(acc[...])

    in_specs = [a_spec, b_spec] + ([o_spec] if has_res else [])
    args = (a, b) + ((res,) if has_res else ())
    return _pc(body, name=name, grid=(M // tm, N // tn, nk), in_specs=in_specs, out_specs=o_spec,
               out_shape=jax.ShapeDtypeStruct((M, N), out_dtype), scratch_shapes=[pltpu.VMEM((tm, tn), F32)],
               dims=("parallel", "parallel", "arbitrary"))(*args)


def _offsets(parts, own_width_aligned):
    offs, at = [], 0
    for p in parts:
        assert at % (p.shape[1] if own_width_aligned else LANES) == 0, (at, p.shape)
        offs.append(at)
        at += p.shape[1]
    return offs, at


def _matmul_rows_parts(parts, w, mode, *, name, tm=512, res=None):
    M = parts[0].shape[0]
    offs, K = _offsets(parts, True)
    tm = min(tm, M)
    N = w.shape[1] if mode == "nn" else w.shape[0]
    assert (w.shape[0] if mode == "nn" else w.shape[1]) == K
    has_res = res is not None
    n = len(parts)

    def body(*refs):
        o_ref = refs[-1]
        total = None
        for s in range(n):
            p = lax.dot_general(refs[s][...].astype(BF16), refs[n + s][...].astype(BF16), _DIMS[mode],
                                preferred_element_type=F32)
            total = p if total is None else total + p
        if has_res:
            total = total + refs[2 * n][...]
        o_ref[...] = total

    in_specs = [pl.BlockSpec((tm, p.shape[1]), lambda i: (i, 0)) for p in parts]
    for p, off in zip(parts, offs):
        blk = off // p.shape[1]
        if mode == "nn":
            in_specs.append(pl.BlockSpec((p.shape[1], N), lambda i, blk=blk: (blk, 0)))
        else:
            in_specs.append(pl.BlockSpec((N, p.shape[1]), lambda i, blk=blk: (0, blk)))
    o_spec = pl.BlockSpec((tm, N), lambda i: (i, 0))
    args = tuple(parts) + (w,) * n + ((res,) if has_res else ())
    return _pc(body, name=name, grid=(M // tm,), in_specs=in_specs + ([o_spec] if has_res else []), out_specs=o_spec,
               out_shape=jax.ShapeDtypeStruct((M, N), F32), dims=("parallel",))(*args)


def _matmul_tn_parts(a, b, *, name, tk=512):
    a_parts = list(a) if isinstance(a, (list, tuple)) else [a]
    b_parts = list(b) if isinstance(b, (list, tuple)) else [b]
    assert len(a_parts) == 1 or len(b_parts) == 1
    a_offs, M = _offsets(a_parts, False)
    b_offs, N = _offsets(b_parts, False)
    K = a_parts[0].shape[0]
    tk = min(tk, K)
    na, nb = len(a_parts), len(b_parts)

    def body(*refs):
        o_ref = refs[-1]
        first = pl.program_id(0) == 0
        for s in range(na):
            for t in range(nb):
                p = lax.dot_general(refs[s][...].astype(BF16), refs[na + t][...].astype(BF16), _DIMS["tn"],
                                    preferred_element_type=F32)
                rows = slice(a_offs[s], a_offs[s] + a_parts[s].shape[1])
                cols = slice(b_offs[t], b_offs[t] + b_parts[t].shape[1])

                @pl.when(first)
                def _(p=p, rows=rows, cols=cols):
                    o_ref[rows, cols] = p

                @pl.when(jnp.logical_not(first))
                def _(p=p, rows=rows, cols=cols):
                    o_ref[rows, cols] += p

    in_specs = [pl.BlockSpec((tk, p.shape[1]), lambda k: (k, 0)) for p in a_parts + b_parts]
    return _pc(body, name=name, grid=(K // tk,), in_specs=in_specs, out_specs=pl.BlockSpec((M, N), lambda k: (0, 0)),
               out_shape=jax.ShapeDtypeStruct((M, N), F32), dims=("arbitrary",))(*a_parts, *b_parts)


def _rmsnorm_fwd(x, w, *, name, tm=512):
    L, D = x.shape
    tm = min(tm, L)

    def body(x_ref, w_ref, h_ref):
        xv = x_ref[...]
        r = lax.rsqrt(jnp.mean(xv * xv, axis=-1, keepdims=True) + NORM_EPS)
        h_ref[...] = (xv * r * w_ref[...]).astype(BF16)

    return _pc(body, name=name, grid=(L // tm,),
               in_specs=[pl.BlockSpec((tm, D), lambda i: (i, 0)), pl.BlockSpec((1, D), lambda i: (0, 0))],
               out_specs=pl.BlockSpec((tm, D), lambda i: (i, 0)), out_shape=jax.ShapeDtypeStruct((L, D), BF16),
               dims=("parallel",))(x, w.reshape(1, D))


def _rmsnorm_bwd(dh, x, w, dres, *, name, tm=512):
    L, D = x.shape
    tm = min(tm, L)

    def body(dh_ref, x_ref, w_ref, dres_ref, dx_ref, dw_ref):
        xv = x_ref[...]
        r = lax.rsqrt(jnp.mean(xv * xv, axis=-1, keepdims=True) + NORM_EPS)
        xhat = xv * r
        dhv = dh_ref[...]
        g = dhv * w_ref[...]
        dx_ref[...] = dres_ref[...] + r * (g - xhat * jnp.mean(g * xhat, axis=-1, keepdims=True))
        part = jnp.sum(dhv * xhat, axis=0, keepdims=True)

        @pl.when(pl.program_id(0) == 0)
        def _():
            dw_ref[...] = part

        @pl.when(pl.program_id(0) > 0)
        def _():
            dw_ref[...] += part

    row = pl.BlockSpec((tm, D), lambda i: (i, 0))
    vec = pl.BlockSpec((1, D), lambda i: (0, 0))
    return _pc(body, name=name, grid=(L // tm,), in_specs=[row, row, vec, row], out_specs=[row, vec],
               out_shape=[jax.ShapeDtypeStruct((L, D), F32), jax.ShapeDtypeStruct((1, D), F32)],
               dims=("arbitrary",))(dh, x, w.reshape(1, D), dres)


def _final_loss(x, w, tgt, *, name, tm=512):
    L, D = x.shape
    tm = min(tm, L)

    def body(x_ref, w_ref, t_ref, loss_ref, dx_ref, dw_ref):
        xv = x_ref[...]
        r = lax.rsqrt(jnp.mean(xv * xv, axis=-1, keepdims=True) + NORM_EPS)
        xhat = xv * r
        e = xhat * w_ref[...] - t_ref[...]
        lpart = jnp.broadcast_to(0.5 * jnp.sum(jnp.mean(e * e, axis=-1, keepdims=True), axis=0, keepdims=True),
                                 (1, LANES))
        dy = e * (1.0 / D)
        g = dy * w_ref[...]
        dx_ref[...] = r * (g - xhat * jnp.mean(g * xhat, axis=-1, keepdims=True))
        part = jnp.sum(dy * xhat, axis=0, keepdims=True)

        @pl.when(pl.program_id(0) == 0)
        def _():
            dw_ref[...] = part
            loss_ref[...] = lpart

        @pl.when(pl.program_id(0) > 0)
        def _():
            dw_ref[...] += part
            loss_ref[...] += lpart

    row = pl.BlockSpec((tm, D), lambda i: (i, 0))
    vec = pl.BlockSpec((1, D), lambda i: (0, 0))
    lsp = pl.BlockSpec((1, LANES), lambda i: (0, 0))
    return _pc(body, name=name, grid=(L // tm,), in_specs=[row, vec, row], out_specs=[lsp, row, vec],
               out_shape=[jax.ShapeDtypeStruct((1, LANES), F32), jax.ShapeDtypeStruct((L, D), F32),
                          jax.ShapeDtypeStruct((1, D), F32)],
               dims=("arbitrary",))(x, w.reshape(1, D), tgt)


def _shift_down(x, prev, k):
    if k == 0:
        return x
    r = pltpu.roll(x, k, 0)
    p = pltpu.roll(prev, k, 0)
    row = lax.broadcasted_iota(jnp.int32, p.shape, 0)
    head = jnp.where(row < k, p, r[:HALO])
    return jnp.concatenate([head, r[HALO:]], axis=0)


def _shift_up(x, j):
    if j == 0:
        return x
    return pltpu.roll(x, x.shape[0] - j, 0)


def _silu(x):
    return x * jax.nn.sigmoid(x)


def _conv_taps(p, p_prev, w):
    K = w.shape[0]
    out = None
    for k in range(K):
        term = w[k:k + 1, :] * _shift_down(p, p_prev, K - 1 - k)
        out = term if out is None else out + term
    return out


def _conv_pointwise_fwd(xs, ws, es, pre, post, outs, *, tc, tm, name):
    L = xs[0][0].shape[0]
    tm = min(tm, L)
    ncol = outs[0][0] // tc
    nrow = L // tm
    hb = tm // HALO
    nx, nw, ne, no = len(xs), len(ws), len(es), len(outs)
    K = ws[0][0].shape[0]

    def body(*refs):
        xc = [refs[2 * n][...] for n in range(nx)]
        i = pl.program_id(1)
        first = (i > 0).astype(F32)
        xp = [refs[2 * n + 1][...] * first for n in range(nx)]
        wv = [refs[2 * nx + n][...] for n in range(nw)]
        ev = [refs[2 * nx + nw + n][...] for n in range(ne)]
        o_refs = refs[2 * nx + nw + ne:]
        ps, pps = pre(*xc), pre(*xp)
        us = [_conv_taps(p, pp, w) for p, pp, w in zip(ps, pps, wv)]
        for o_ref, val in zip(o_refs, post(us, ev)):
            o_ref[...] = val.astype(o_ref.dtype)

    in_specs, args = [], []
    for arr, c0 in xs:
        off = c0 // tc
        in_specs.append(pl.BlockSpec((tm, tc), lambda j, i, off=off: (i, j + off)))
        in_specs.append(pl.BlockSpec((HALO, tc), lambda j, i, off=off: (jnp.maximum(i * hb - 1, 0), j + off)))
        args += [arr, arr]
    for arr, c0 in ws:
        off = c0 // tc
        in_specs.append(pl.BlockSpec((K, tc), lambda j, i, off=off: (0, j + off)))
        args.append(arr)
    for arr, c0 in es:
        off = c0 // tc
        in_specs.append(pl.BlockSpec((tm, tc), lambda j, i, off=off: (i, j + off)))
        args.append(arr)
    out_specs = [pl.BlockSpec((tm, tc), lambda j, i: (i, j)) for _ in range(no)]
    out_shape = [jax.ShapeDtypeStruct((L, c), dt) for c, dt in outs]
    return _pc(body, name=name, grid=(ncol, nrow), in_specs=in_specs, out_specs=out_specs, out_shape=out_shape,
               dims=("parallel", "parallel"))(*args)


def _conv_pointwise_bwd(xs, ws, es, dys, pre, post, width, *, tc, tm, name, out_dtype=BF16):
    L = xs[0][0].shape[0]
    tm = min(tm, L)
    ncol = width // tc
    nrow = L // tm
    hb = tm // HALO
    nx, nw, ne, ny = len(xs), len(ws), len(es), len(dys)
    K = ws[0][0].shape[0]

    def body(*refs):
        i = pl.program_id(1)
        first = (i > 0).astype(F32)
        more = (i < nrow - 1).astype(F32)
        pos = 0
        xc, xp, xe = [], [], []
        for n in range(nx):
            cur, prv, nxt = refs[pos][...], refs[pos + 1][...], refs[pos + 2][...]
            pos += 3
            xc.append(cur)
            xp.append(prv * first)
            xe.append(jnp.concatenate([cur, nxt], axis=0))
        wv = [refs[pos + n][...] for n in range(nw)]
        pos += nw
        ee = []
        for n in range(ne):
            ee.append(jnp.concatenate([refs[pos][...], refs[pos + 1][...]], axis=0))
            pos += 2
        dye = []
        for n in range(ny):
            dye.append(jnp.concatenate([refs[pos][...].astype(F32), refs[pos + 1][...].astype(F32) * more], axis=0))
            pos += 2
        dx_refs = refs[pos:pos + nx]
        de_refs = refs[pos + nx:pos + nx + ne]
        dw_refs = refs[pos + nx + ne:pos + nx + ne + nw]

        ps, pps = pre(*xe), pre(*xp)
        shifted = [[_shift_down(p, pp, K - 1 - k) for k in range(K)] for p, pp in zip(ps, pps)]
        us = []
        for n in range(nw):
            u = None
            for k in range(K):
                term = wv[n][k:k + 1, :] * shifted[n][k]
                u = term if u is None else u + term
            us.append(u)
        _, post_vjp = jax.vjp(lambda u_, e_: post(u_, e_), us, ee)
        dus, des = post_vjp(dye)
        dps = []
        for n in range(nw):
            dp = None
            for k in range(K):
                term = wv[n][k:k + 1, :] * _shift_up(dus[n], K - 1 - k)[:tm]
                dp = term if dp is None else dp + term
            dps.append(dp)
            for k in range(K):
                part = jnp.sum(dus[n][:tm] * shifted[n][k][:tm], axis=0, keepdims=True)

                @pl.when(i == 0)
                def _(part=part, n=n, k=k):
                    dw_refs[n][k:k + 1, :] = part

                @pl.when(i > 0)
                def _(part=part, n=n, k=k):
                    dw_refs[n][k:k + 1, :] += part
        _, pre_vjp = jax.vjp(lambda *x_: pre(*x_), *xc)
        dxs = pre_vjp(dps)
        for r, v in zip(dx_refs, dxs):
            r[...] = v.astype(out_dtype)
        for r, v in zip(de_refs, des):
            r[...] = v[:tm].astype(out_dtype)

    in_specs, args = [], []

    def add_rows(arr, c0, prev, nxt):
        off = c0 // tc
        in_specs.append(pl.BlockSpec((tm, tc), lambda j, i, off=off: (i, j + off)))
        args.append(arr)
        if prev:
            in_specs.append(pl.BlockSpec((HALO, tc), lambda j, i, off=off: (jnp.maximum(i * hb - 1, 0), j + off)))
            args.append(arr)
        if nxt:
            last = L // HALO - 1
            in_specs.append(pl.BlockSpec((HALO, tc), lambda j, i, off=off: (jnp.minimum((i + 1) * hb, last), j + off)))
            args.append(arr)

    for arr, c0 in xs:
        add_rows(arr, c0, True, True)
    for arr, c0 in ws:
        off = c0 // tc
        in_specs.append(pl.BlockSpec((K, tc), lambda j, i, off=off: (0, j + off)))
        args.append(arr)
    for arr, c0 in es:
        add_rows(arr, c0, False, True)
    for arr, c0 in dys:
        add_rows(arr, c0, False, True)
    tile = pl.BlockSpec((tm, tc), lambda j, i: (i, j))
    wtile = pl.BlockSpec((K, tc), lambda j, i: (0, j))
    out_specs = [tile] * (nx + ne) + [wtile] * nw
    out_shape = [jax.ShapeDtypeStruct((L, width), out_dtype)] * (nx + ne) + \
        [jax.ShapeDtypeStruct((K, width), F32)] * nw
    res = _pc(body, name=name, grid=(ncol, nrow), in_specs=in_specs, out_specs=out_specs, out_shape=out_shape,
              dims=("parallel", "arbitrary"))(*args)
    return res[:nx], res[nx:nx + ne], res[nx + ne:]


def _pre_identity(*x):
    return list(x)


def _pre_product(c, h):
    return [c * h]


def _post_silu(us, es):
    return [_silu(us[0])]


def _post_gate_mul(us, es):
    return [es[0] * us[0]]


def _post_swiglu(us, es):
    return [_silu(us[0]) * us[1]]


def _make_dot(passes):
    def raw(a, b, dn):
        a_hi = a.astype(BF16)
        b_hi = b.astype(BF16)
        out = lax.dot_general(a_hi, b_hi, dn, preferred_element_type=F32)
        if passes == 3:
            a_lo = (a - a_hi.astype(F32)).astype(BF16)
            b_lo = (b - b_hi.astype(F32)).astype(BF16)
            out = out + lax.dot_general(a_hi, b_lo, dn, preferred_element_type=F32)
            out = out + lax.dot_general(a_lo, b_hi, dn, preferred_element_type=F32)
        return out

    @jax.custom_vjp
    def nn(a, b):
        return raw(a, b, _DIMS["nn"])

    @jax.custom_vjp
    def nt(a, b):
        return raw(a, b, _DIMS["nt"])

    @jax.custom_vjp
    def tn(a, b):
        return raw(a, b, _DIMS["tn"])

    nn.defvjp(lambda a, b: (nn(a, b), (a, b)), lambda r, g: (nt(g, r[1]), tn(r[0], g)))
    nt.defvjp(lambda a, b: (nt(a, b), (a, b)), lambda r, g: (nn(g, r[1]), tn(g, r[0])))
    tn.defvjp(lambda a, b: (tn(a, b), (a, b)), lambda r, g: (nt(r[1], g), nn(r[0], g)))
    return nn, nt, tn


_NN1, _NT1, _TN1 = _make_dot(1)
_NN3, _NT3, _TN3 = _make_dot(3)


def _l2norm(x):
    return x * lax.rsqrt(jnp.sum(x * x, axis=-1, keepdims=True) + NORM_EPS)


def _unit_lower_inverse_raw(a_list):
    C = a_list[0].shape[0]
    ii = lax.broadcasted_iota(jnp.int32, (C, C), 0)
    jj = lax.broadcasted_iota(jnp.int32, (C, C), 1)
    eye = jnp.where(ii == jj, 1.0, 0.0)
    ts = [eye - a for a in a_list]
    ps = list(a_list)
    n = 2
    while n < C:
        ps = [_NN3(p, p) for p in ps]
        ts = [t + _NN3(t, p) for t, p in zip(ts, ps)]
        n *= 2
    return ts


@jax.custom_vjp
def _unit_lower_inverse(a_list):
    return _unit_lower_inverse_raw(a_list)


def _unit_lower_inverse_fwd(a_list):
    ts = _unit_lower_inverse_raw(a_list)
    return ts, ts


def _unit_lower_inverse_bwd(ts, gs):
    xs = [_TN3(t, g) for t, g in zip(ts, gs)]
    return ([-_NT3(x, t) for x, t in zip(xs, ts)],)


_unit_lower_inverse.defvjp(_unit_lower_inverse_fwd, _unit_lower_inverse_bwd)


def _gdn_prep(units):
    C, Dh = units[0][0].shape
    ii = lax.broadcasted_iota(jnp.int32, (C, C), 0)
    jj = lax.broadcasted_iota(jnp.int32, (C, C), 1)
    lane = lax.broadcasted_iota(jnp.int32, (1, C), 1)
    causal = ii >= jj
    strict = ii > jj
    qs = [_l2norm(un[0]) * (Dh ** -0.5) for un in units]
    ks = [_l2norm(un[1]) for un in units]
    betas = [jax.nn.sigmoid(un[4]) for un in units]
    gs = [-jnp.exp(un[5]) * jax.nn.softplus(un[3] + un[6]) for un in units]
    gc_rows = [jnp.sum(jnp.where(ii <= jj, g, 0.0), axis=0, keepdims=True) for g in gs]
    gc_cols = [jnp.sum(jnp.where(ii == jj, r, 0.0), axis=1, keepdims=True) for r in gc_rows]
    decays = [jnp.where(causal, jnp.exp(jnp.where(causal, c - r, 0.0)), 0.0) for c, r in zip(gc_cols, gc_rows)]
    kbs = [k * b for k, b in zip(ks, betas)]
    kks = [_NT1(kb, k) for kb, k in zip(kbs, ks)]
    qks = [_NT1(q, k) for q, k in zip(qs, ks)]
    ts = _unit_lower_inverse([jnp.where(strict, kk * d, 0.0) for kk, d in zip(kks, decays)])
    eg_cols = [jnp.exp(c) for c in gc_cols]
    uws = [_NN3(t, jnp.concatenate([un[2] * b, kb * e], axis=1))
           for t, un, b, kb, e in zip(ts, units, betas, kbs, eg_cols)]
    out = []
    for q, k, qk, d, uw, e, r, c in zip(qs, ks, qks, decays, uws, eg_cols, gc_rows, gc_cols):
        g_last = jnp.sum(jnp.where(lane == C - 1, r, 0.0), axis=1, keepdims=True)
        out.append((q * e, k * jnp.exp(g_last - c), uw[:, :Dh], uw[:, Dh:], jnp.where(causal, qk * d, 0.0),
                    jnp.broadcast_to(jnp.exp(g_last), (1, Dh))))
    return out


def _gdn_step(units):
    v_news = [un[3] - _NN1(un[4], un[0]) for un in units]
    o_state = [_NN1(un[1], un[0]) for un in units]
    o_intra = [_NN1(un[5], vn) for un, vn in zip(units, v_news)]
    s_adds = [_TN1(un[2], vn) for un, vn in zip(units, v_news)]
    out = []
    for un, a, b, s_add in zip(units, o_state, o_intra, s_adds):
        o = a + b
        y = o * lax.rsqrt(jnp.mean(o * o, axis=-1, keepdims=True) + NORM_EPS) * un[8] * _silu(un[7])
        out.append((y, un[0] * un[6] + s_add))
    return out


def _gdn_prep_fwd(qkv, gab, gab_col, a_log, dt_bias, *, name, chunks=4):
    L = qkv.shape[0]
    H, C = GDN_HEADS, GDN_CHUNK
    W = qkv.shape[1] // 3
    Dh = W // H
    N = L // C
    chunks = min(chunks, N)
    R = chunks * C
    gab_off = gab_col // LANES

    def body(q_ref, k_ref, v_ref, gab_ref, al_ref, dt_ref, qe_ref, ke_ref, u_ref, w_ref, at_ref, eg_ref):
        where = [(cc, h) for cc in range(chunks) for h in range(H)]
        units = []
        for cc, h in where:
            rows, sl = slice(cc * C, (cc + 1) * C), slice(h * Dh, (h + 1) * Dh)
            units.append((q_ref[rows, sl], k_ref[rows, sl], v_ref[rows, sl], gab_ref[rows, h:h + 1],
                          gab_ref[rows, H + h:H + h + 1], al_ref[h], dt_ref[h]))
        for (cc, h), (qe, ke, u, w, attn, eg) in zip(where, _gdn_prep(units)):
            rows, sl = slice(cc * C, (cc + 1) * C), slice(h * Dh, (h + 1) * Dh)
            qe_ref[rows, sl] = qe
            ke_ref[rows, sl] = ke
            u_ref[rows, sl] = u
            w_ref[rows, sl] = w
            at_ref[h, rows, :] = attn
            eg_ref[cc, h:h + 1, :] = eg

    col = lambda c: pl.BlockSpec((R, W), lambda n, c=c: (n, c))
    tok = pl.BlockSpec((R, LANES), lambda n: (n, gab_off))
    par = pl.BlockSpec((H, 1, 1), lambda n: (0, 0, 0))
    wide = pl.BlockSpec((R, W), lambda n: (n, 0))
    return _pc(body, name=name, grid=(N // chunks,), in_specs=[col(0), col(1), col(2), tok, par, par],
               out_specs=[wide, wide, wide, wide, pl.BlockSpec((H, R, C), lambda n: (0, n, 0)),
                          pl.BlockSpec((chunks, H, Dh), lambda n: (n, 0, 0))],
               out_shape=[jax.ShapeDtypeStruct((L, W), F32)] * 4 + [jax.ShapeDtypeStruct((H, L, C), F32),
                                                                   jax.ShapeDtypeStruct((N, H, Dh), F32)],
               dims=("parallel",))(qkv, qkv, qkv, gab, a_log, dt_bias)


def _gdn_prep_bwd(qkv, gab, gab_col, a_log, dt_bias, dqe, dke, du, dw, dattn, deg, gab_width, *, name, chunks=4):
    L = qkv.shape[0]
    H, C = GDN_HEADS, GDN_CHUNK
    W = qkv.shape[1] // 3
    Dh = W // H
    N = L // C
    chunks = min(chunks, N)
    R = chunks * C
    gab_off = gab_col // LANES

    def body(q_ref, k_ref, v_ref, gab_ref, al_ref, dt_ref, dqe_ref, dke_ref, du_ref, dw_ref, dat_ref, deg_ref,
             dqkv_ref, dgab_ref, dal_ref, ddt_ref):
        first = pl.program_id(0) == 0
        lane = lax.broadcasted_iota(jnp.int32, (C, gab_width), 1)
        dal_sum, ddt_sum = [None] * H, [None] * H
        where = [(cc, h) for cc in range(chunks) for h in range(H)]
        units, cots = [], []
        for cc, h in where:
            rows, sl = slice(cc * C, (cc + 1) * C), slice(h * Dh, (h + 1) * Dh)
            units.append((q_ref[rows, sl], k_ref[rows, sl], v_ref[rows, sl], gab_ref[rows, h:h + 1],
                          gab_ref[rows, H + h:H + h + 1], al_ref[h], dt_ref[h]))
            cots.append((dqe_ref[rows, sl], dke_ref[rows, sl], du_ref[rows, sl], dw_ref[rows, sl],
                         dat_ref[h, rows, :], deg_ref[cc, h:h + 1, :]))
        _, vjp = jax.vjp(_gdn_prep, units)
        (d_units,) = vjp(cots)
        dgabs = [jnp.zeros((C, gab_width), F32) for _ in range(chunks)]
        for (cc, h), (dq, dk, dv, dga, dgb, dal, ddt) in zip(where, d_units):
            rows = slice(cc * C, (cc + 1) * C)
            dqkv_ref[rows, h * Dh:(h + 1) * Dh] = dq
            dqkv_ref[rows, W + h * Dh:W + (h + 1) * Dh] = dk
            dqkv_ref[rows, 2 * W + h * Dh:2 * W + (h + 1) * Dh] = dv
            dgabs[cc] = dgabs[cc] + jnp.where(lane == h, dga, 0.0) + jnp.where(lane == H + h, dgb, 0.0)
            dal_sum[h] = dal if dal_sum[h] is None else dal_sum[h] + dal
            ddt_sum[h] = ddt if ddt_sum[h] is None else ddt_sum[h] + ddt
        for cc in range(chunks):
            dgab_ref[cc * C:(cc + 1) * C, :] = dgabs[cc].astype(BF16)

        @pl.when(first)
        def _():
            for h in range(H):
                dal_ref[h] = dal_sum[h]
                ddt_ref[h] = ddt_sum[h]

        @pl.when(jnp.logical_not(first))
        def _():
            for h in range(H):
                dal_ref[h] += dal_sum[h]
                ddt_ref[h] += ddt_sum[h]

    col = lambda c: pl.BlockSpec((R, W), lambda n, c=c: (n, c))
    tok = pl.BlockSpec((R, LANES), lambda n: (n, gab_off))
    par = pl.BlockSpec((H, 1, 1), lambda n: (0, 0, 0))
    wide = pl.BlockSpec((R, W), lambda n: (n, 0))
    att = pl.BlockSpec((H, R, C), lambda n: (0, n, 0))
    egs = pl.BlockSpec((chunks, H, Dh), lambda n: (n, 0, 0))
    return _pc(body, name=name, grid=(N // chunks,),
               in_specs=[col(0), col(1), col(2), tok, par, par, wide, wide, wide, wide, att, egs],
               out_specs=[pl.BlockSpec((R, 3 * W), lambda n: (n, 0)), pl.BlockSpec((R, gab_width), lambda n: (n, 0)),
                          par, par],
               out_shape=[jax.ShapeDtypeStruct((L, 3 * W), F32), jax.ShapeDtypeStruct((L, gab_width), BF16)]
               + [jax.ShapeDtypeStruct((H, 1, 1), F32)] * 2,
               dims=("arbitrary",))(qkv, qkv, qkv, gab, a_log, dt_bias, dqe, dke, du, dw, dattn, deg)


def _gdn_scan_fwd(qe, ke, u, w, attn, eg, gz, gz_col, wgn, *, name):
    L, W = qe.shape
    H, C = GDN_HEADS, GDN_CHUNK
    Dh = W // H
    N = L // C
    gz_off = gz_col // W

    def body(qe_ref, ke_ref, u_ref, w_ref, at_ref, eg_ref, gz_ref, wgn_ref, y_ref, st_ref, s_scr):
        @pl.when(pl.program_id(0) == 0)
        def _():
            s_scr[...] = jnp.zeros_like(s_scr)

        units = []
        for h in range(H):
            sl = slice(h * Dh, (h + 1) * Dh)
            st_ref[0, h] = s_scr[h]
            units.append((s_scr[h], qe_ref[:, sl], ke_ref[:, sl], u_ref[:, sl], w_ref[:, sl], at_ref[h],
                          eg_ref[0, h:h + 1, :], gz_ref[:, sl], wgn_ref[...]))
        for h, (y, S_new) in enumerate(_gdn_step(units)):
            y_ref[:, h * Dh:(h + 1) * Dh] = y.astype(BF16)
            s_scr[h] = S_new

    wide = pl.BlockSpec((C, W), lambda n: (n, 0))
    return _pc(body, name=name, grid=(N,),
               in_specs=[wide, wide, wide, wide, pl.BlockSpec((H, C, C), lambda n: (0, n, 0)),
                         pl.BlockSpec((1, H, Dh), lambda n: (n, 0, 0)),
                         pl.BlockSpec((C, W), lambda n: (n, gz_off)), pl.BlockSpec((1, Dh), lambda n: (0, 0))],
               out_specs=[wide, pl.BlockSpec((1, H, Dh, Dh), lambda n: (n, 0, 0, 0))],
               out_shape=[jax.ShapeDtypeStruct((L, W), BF16), jax.ShapeDtypeStruct((N, H, Dh, Dh), F32)],
               scratch_shapes=[pltpu.VMEM((H, Dh, Dh), F32)],
               dims=("arbitrary",))(qe, ke, u, w, attn, eg, gz, wgn.reshape(1, Dh))


def _gdn_scan_bwd(qe, ke, u, w, attn, eg, gz, gz_col, wgn, states, dy, dy_col, *, name):
    L, W = qe.shape
    H, C = GDN_HEADS, GDN_CHUNK
    Dh = W // H
    N = L // C
    gz_off = gz_col // W
    dy_off = dy_col // W

    def body(qe_ref, ke_ref, u_ref, w_ref, at_ref, eg_ref, gz_ref, wgn_ref, st_ref, dy_ref,
             dqe_ref, dke_ref, du_ref, dw_ref, dat_ref, deg_ref, dgz_ref, dwgn_ref, ds_scr):
        first = pl.program_id(0) == 0

        @pl.when(first)
        def _():
            ds_scr[...] = jnp.zeros_like(ds_scr)

        dwgn = None
        units, cots = [], []
        for h in range(H):
            sl = slice(h * Dh, (h + 1) * Dh)
            units.append((st_ref[0, h], qe_ref[:, sl], ke_ref[:, sl], u_ref[:, sl], w_ref[:, sl], at_ref[h],
                          eg_ref[0, h:h + 1, :], gz_ref[:, sl], wgn_ref[...]))
            cots.append((dy_ref[:, sl].astype(F32), ds_scr[h]))
        _, vjp = jax.vjp(_gdn_step, units)
        (d_units,) = vjp(cots)
        for h, (dS, dqe, dke, du, dw, dat, deg, dgz, dwg) in enumerate(d_units):
            sl = slice(h * Dh, (h + 1) * Dh)
            ds_scr[h] = dS
            dqe_ref[:, sl] = dqe
            dke_ref[:, sl] = dke
            du_ref[:, sl] = du
            dw_ref[:, sl] = dw
            dat_ref[h] = dat
            deg_ref[0, h:h + 1, :] = deg
            dgz_ref[:, sl] = dgz.astype(BF16)
            dwgn = dwg if dwgn is None else dwgn + dwg

        @pl.when(first)
        def _():
            dwgn_ref[...] = dwgn

        @pl.when(jnp.logical_not(first))
        def _():
            dwgn_ref[...] += dwgn

    rev = lambda n: N - 1 - n
    wide = pl.BlockSpec((C, W), lambda n: (rev(n), 0))
    att = pl.BlockSpec((H, C, C), lambda n: (0, rev(n), 0))
    egs = pl.BlockSpec((1, H, Dh), lambda n: (rev(n), 0, 0))
    vec = pl.BlockSpec((1, Dh), lambda n: (0, 0))
    return _pc(body, name=name, grid=(N,),
               in_specs=[wide, wide, wide, wide, att, egs, pl.BlockSpec((C, W), lambda n: (rev(n), gz_off)), vec,
                         pl.BlockSpec((1, H, Dh, Dh), lambda n: (rev(n), 0, 0, 0)),
                         pl.BlockSpec((C, W), lambda n: (rev(n), dy_off))],
               out_specs=[wide, wide, wide, wide, att, egs, wide, vec],
               out_shape=[jax.ShapeDtypeStruct((L, W), F32)] * 4 + [jax.ShapeDtypeStruct((H, L, C), F32),
                                                                   jax.ShapeDtypeStruct((N, H, Dh), F32),
                                                                   jax.ShapeDtypeStruct((L, W), BF16),
                                                                   jax.ShapeDtypeStruct((1, Dh), F32)],
               scratch_shapes=[pltpu.VMEM((H, Dh, Dh), F32)],
               dims=("arbitrary",))(qe, ke, u, w, attn, eg, gz, wgn.reshape(1, Dh), states, dy)


def _sb_scores(z, mask):
    sp = jnp.maximum(z, 0.0) + jnp.log(1.0 + jnp.exp(-jnp.abs(z)))
    lom = -sp if mask is None else jnp.where(mask, -sp, 0.0)
    return lom, z - sp


def _sb_masks(tq, width, dh):
    rr = lax.broadcasted_iota(jnp.int32, (tq, tq), 0)
    cc = lax.broadcasted_iota(jnp.int32, (tq, tq), 1)
    first_head = lax.broadcasted_iota(jnp.int32, (tq, width), 1) < dh
    return cc < rr, jnp.where(rr > cc, 1.0, 0.0).astype(BF16), first_head


def _sb_fwd(qkv, *, name, tq=256):
    L = qkv.shape[0]
    H = SB_HEADS
    width = 2 * (qkv.shape[1] // 3 // H)
    dh = width // 2
    npair = H // 2
    tq = min(tq, L)
    nq = L // tq

    def body(q_ref, k_ref, v_ref, o_ref, c_ref):
        i = pl.program_id(1)
        diag, tri, first_head = _sb_masks(tq, width, dh)
        qp = q_ref[...]
        zero = jnp.zeros_like(qp)
        qs = (jnp.where(first_head, qp, zero), jnp.where(first_head, zero, qp))

        def blocks(js, carry, mask):
            units = [(b, hd) for b in range(len(js)) for hd in range(2)]
            starts = [pl.multiple_of(j * tq, tq) for j in js]
            ks = [k_ref[pl.ds(st, tq), :] for st in starts]
            vs = [v_ref[pl.ds(st, tq), :] for st in starts]
            zs = {(b, hd): lax.dot_general(qs[hd], ks[b], _DIMS["nt"], preferred_element_type=F32)
                  for b, hd in units}
            scores = {un: _sb_scores(zs[un], mask) for un in units}
            later = {un: jnp.dot(scores[un][0].astype(BF16), tri, preferred_element_type=F32) for un in units}
            cs = [carry[hd][0] for hd in range(2)]
            accs = [carry[hd][1] for hd in range(2)]
            for b, hd in units:
                lom, lb = scores[(b, hd)]
                a = jnp.exp(lb + (cs[hd] + later[(b, hd)]))
                if mask is not None:
                    a = jnp.where(mask, a, 0.0)
                accs[hd] = accs[hd] + jnp.dot(a.astype(BF16), vs[b], preferred_element_type=F32)
                cs[hd] = cs[hd] + jnp.sum(lom, axis=1, keepdims=True)
            return tuple((cs[hd], accs[hd]) for hd in range(2))

        init = tuple((jnp.zeros((tq, 1), F32), jnp.zeros((tq, width), F32)) for _ in range(2))
        carry = blocks([i], init, diag)
        carry = lax.fori_loop(0, i % SB_UNROLL, lambda t, cr: blocks([i - 1 - t], cr, None), carry)
        left = i - i % SB_UNROLL
        carry = lax.fori_loop(0, left // SB_UNROLL,
                              lambda t, cr: blocks([left - 1 - SB_UNROLL * t - b for b in range(SB_UNROLL)], cr, None),
                              carry)
        o_ref[...] = jnp.where(first_head, carry[0][1], carry[1][1]).astype(BF16)
        c_ref[0] = carry[0][0]
        c_ref[1] = carry[1][0]

    return _pc(body, name=name, grid=(npair, nq),
               in_specs=[pl.BlockSpec((tq, width), lambda p, i: (i, p)),
                         pl.BlockSpec((L, width), lambda p, i: (0, npair + p)),
                         pl.BlockSpec((L, width), lambda p, i: (0, 2 * npair + p))],
               out_specs=[pl.BlockSpec((tq, width), lambda p, i: (i, p)),
                          pl.BlockSpec((2, tq, 1), lambda p, i: (p, i, 0))],
               out_shape=[jax.ShapeDtypeStruct((L, npair * width), BF16), jax.ShapeDtypeStruct((H, L, 1), F32)],
               dims=("parallel", "parallel"))(qkv, qkv, qkv)


def _sb_bwd(qkv, ctot, do, do_col, scale, *, name, tq=256):
    L = qkv.shape[0]
    H = SB_HEADS
    width = 2 * (qkv.shape[1] // 3 // H)
    dh = width // 2
    npair = H // 2
    tq = min(tq, L)
    nq = L // tq
    do_off = do_col // width

    def body(q_ref, k_ref, v_ref, c_ref, do_ref, dq_ref, dk_ref, dv_ref):
        i = pl.program_id(1)

        @pl.when(i == 0)
        def _():
            dk_ref[...] = jnp.zeros_like(dk_ref)
            dv_ref[...] = jnp.zeros_like(dv_ref)

        diag, tri_later, first_head = _sb_masks(tq, width, dh)
        rr = lax.broadcasted_iota(jnp.int32, (tq, tq), 0)
        cc = lax.broadcasted_iota(jnp.int32, (tq, tq), 1)
        tri_before = jnp.where(rr < cc, 1.0, 0.0).astype(BF16)
        qp = q_ref[...]
        dop = do_ref[...].astype(BF16)
        zero = jnp.zeros_like(qp)
        qs = (jnp.where(first_head, qp, zero), jnp.where(first_head, zero, qp))
        dos = (jnp.where(first_head, dop, zero), jnp.where(first_head, zero, dop))
        ctots = (c_ref[0], c_ref[1])

        def blocks(js, carry, mask):
            nb = len(js)
            units = [(b, hd) for b in range(nb) for hd in range(2)]
            starts = [pl.multiple_of(j * tq, tq) for j in js]
            ks = [k_ref[pl.ds(st, tq), :] for st in starts]
            vs = [v_ref[pl.ds(st, tq), :] for st in starts]
            zs = {(b, hd): lax.dot_general(qs[hd], ks[b], _DIMS["nt"], preferred_element_type=F32)
                  for b, hd in units}
            das = {(b, hd): lax.dot_general(dos[hd], vs[b], _DIMS["nt"], preferred_element_type=F32)
                   for b, hd in units}
            scores = {un: _sb_scores(zs[un], mask) for un in units}
            later = {un: jnp.dot(scores[un][0].astype(BF16), tri_later, preferred_element_type=F32) for un in units}
            pcs = [carry[hd][0] for hd in range(2)]
            avals = {}
            for b, hd in units:
                pcs[hd] = pcs[hd] + jnp.sum(scores[(b, hd)][0], axis=1, keepdims=True)
                a = jnp.exp(scores[(b, hd)][1] + ((ctots[hd] - pcs[hd]) + later[(b, hd)]))
                avals[(b, hd)] = a if mask is None else jnp.where(mask, a, 0.0)
            gs = {un: das[un] * avals[un] for un in units}
            before = {un: jnp.dot(gs[un].astype(BF16), tri_before, preferred_element_type=F32) for un in units}
            pgs = [carry[hd][1] for hd in range(2)]
            dzs = {}
            for b, hd in units:
                sig = jnp.exp(scores[(b, hd)][1])
                dz = gs[(b, hd)] * (1.0 - sig) - (pgs[hd] + before[(b, hd)]) * sig
                dzs[(b, hd)] = (dz if mask is None else jnp.where(mask, dz, 0.0)).astype(BF16)
                pgs[hd] = pgs[hd] + jnp.sum(gs[(b, hd)], axis=1, keepdims=True)
            dqs = [carry[hd][2] for hd in range(2)]
            for b, hd in units:
                dqs[hd] = dqs[hd] + jnp.dot(dzs[(b, hd)], ks[b], preferred_element_type=F32)
            for b in range(nb):
                dk_ref[pl.ds(starts[b], tq), :] += sum(
                    lax.dot_general(dzs[(b, hd)], qs[hd], _DIMS["tn"], preferred_element_type=F32) for hd in range(2))
                dv_ref[pl.ds(starts[b], tq), :] += sum(
                    lax.dot_general(avals[(b, hd)].astype(BF16), dos[hd], _DIMS["tn"], preferred_element_type=F32)
                    for hd in range(2))
            return tuple((pcs[hd], pgs[hd], dqs[hd]) for hd in range(2))

        col = jnp.zeros((tq, 1), F32)
        init = tuple((col, col, jnp.zeros((tq, width), F32)) for _ in range(2))
        carry = lax.fori_loop(0, i // SB_UNROLL,
                              lambda t, cr: blocks([SB_UNROLL * t + b for b in range(SB_UNROLL)], cr, None), init)
        carry = lax.fori_loop(0, i % SB_UNROLL, lambda t, cr: blocks([i - i % SB_UNROLL + t], cr, None), carry)
        carry = blocks([i], carry, diag)
        dq_ref[...] = (jnp.where(first_head, carry[0][2], carry[1][2]) * scale).astype(BF16)

    tile = pl.BlockSpec((tq, width), lambda p, i: (i, p))
    full = pl.BlockSpec((L, width), lambda p, i: (0, p))
    sds = jax.ShapeDtypeStruct((L, npair * width), F32)
    return _pc(body, name=name, grid=(npair, nq),
               in_specs=[tile, pl.BlockSpec((L, width), lambda p, i: (0, npair + p)),
                         pl.BlockSpec((L, width), lambda p, i: (0, 2 * npair + p)),
                         pl.BlockSpec((2, tq, 1), lambda p, i: (p, i, 0)),
                         pl.BlockSpec((tq, width), lambda p, i: (i, do_off + p))],
               out_specs=[tile, full, full],
               out_shape=[jax.ShapeDtypeStruct((L, npair * width), BF16), sds, sds],
               dims=("parallel", "arbitrary"))(qkv, qkv, qkv, ctot, do)


def _adamw(w, g, m, v, *, name, tm=256):
    R, C = w.shape
    tm = min(tm, R)
    assert R % tm == 0, (R, tm)
    c1 = 1.0 - ADAM_B1 ** ADAM_STEP
    c2 = 1.0 - ADAM_B2 ** ADAM_STEP

    def body(w_ref, g_ref, m_ref, v_ref, d_ref, nm_ref, nv_ref):
        gv = g_ref[...]
        nm = ADAM_B1 * m_ref[...] + (1.0 - ADAM_B1) * gv
        nv = ADAM_B2 * v_ref[...] + (1.0 - ADAM_B2) * (gv * gv)
        d_ref[...] = -ADAM_LR * ((nm / c1) / (jnp.sqrt(nv / c2) + ADAM_EPS) + ADAM_WD * w_ref[...])
        nm_ref[...] = nm
        nv_ref[...] = nv

    blk = pl.BlockSpec((tm, C), lambda i: (i, 0))
    sds = jax.ShapeDtypeStruct((R, C), F32)
    return _pc(body, name=name, grid=(R // tm,), in_specs=[blk] * 4, out_specs=[blk] * 3, out_shape=[sds] * 3,
               dims=("parallel",))(w, g, m, v)


def _add_halves(gbuf, ra, c, *, name, tm=256):
    _, S, Rh, C = gbuf.shape
    assert Rh % tm == 0, (Rh, tm)

    def body(c_ref, g_ref, r_ref, o_ref):
        o_ref[...] = (g_ref[...] + r_ref[...]).astype(BF16)

    grid_spec = pltpu.PrefetchScalarGridSpec(
        num_scalar_prefetch=1, grid=(S, Rh // tm),
        in_specs=[pl.BlockSpec((None, None, tm, C), lambda s, i, c_ref: (c_ref[0], s, i, 0)),
                  pl.BlockSpec((None, tm, C), lambda s, i, c_ref: (s, i, 0))],
        out_specs=pl.BlockSpec((None, tm, C), lambda s, i, c_ref: (s, i, 0)))
    return _pc_prefetch(body, name=name, grid_spec=grid_spec, out_shape=jax.ShapeDtypeStruct((S, Rh, C), BF16),
                        dims=("parallel", "parallel"))(c.reshape(1).astype(jnp.int32), gbuf, ra)


def _add_chips(p, rb, chip, *, name, tm=256):
    S, Rh, C = p.shape
    assert Rh % tm == 0, (Rh, tm)

    def body(s_ref, p_ref, r_ref, o_ref):
        o_ref[...] = ((p_ref[...].astype(F32) + r_ref[0].astype(F32)) + r_ref[1].astype(F32)) + r_ref[2].astype(F32)

    grid_spec = pltpu.PrefetchScalarGridSpec(
        num_scalar_prefetch=1, grid=(Rh // tm,),
        in_specs=[pl.BlockSpec((None, tm, C), lambda i, s_ref: (s_ref[0], i, 0)),
                  pl.BlockSpec((3, tm, C), lambda i, s_ref: (0, i, 0))],
        out_specs=pl.BlockSpec((tm, C), lambda i, s_ref: (i, 0)))
    return _pc_prefetch(body, name=name, grid_spec=grid_spec, out_shape=jax.ShapeDtypeStruct((Rh, C), F32),
                        dims=("parallel",))(chip.reshape(1).astype(jnp.int32), p, rb)


def _sum_slots(g, *, name):
    n, R, C = g.shape

    def body(g_ref, o_ref):
        acc = g_ref[0]
        for s in range(1, n):
            acc = acc + g_ref[s]
        o_ref[...] = acc

    return _pc(body, name=name, grid=(1,), in_specs=[pl.BlockSpec((n, R, C), lambda i: (0, 0, 0))],
               out_specs=pl.BlockSpec((R, C), lambda i: (0, 0)), out_shape=jax.ShapeDtypeStruct((R, C), F32),
               dims=("arbitrary",))(g)


ANY = pl.BlockSpec(memory_space=pl.ANY)


def _place():
    return lax.axis_index("x"), lax.axis_index("y"), lax.axis_index("c")


def _other_chips(x, y):
    return [(1 - x, y), (x, 1 - y), (1 - x, 1 - y)]


def _allgather_chips(w, *, name):
    _, Rh, C = w.shape

    def body(w_ref, out_ref, send_sems, recv_sems):
        x, y, c = _place()
        sib = (x, y, 1 - c)
        chips = _other_chips(x, y)

        def copy(k, chip_id, half, to):
            return pltpu.make_async_remote_copy(src_ref=w_ref.at[half] if k < 3 else out_ref.at[chip_id, half],
                                                dst_ref=out_ref.at[chip_id, half], send_sem=send_sems.at[k],
                                                recv_sem=recv_sems.at[k], device_id=to, device_id_type=MESH)

        sends = [copy(j, 2 * x + y, c, (px, py, c)) for j, (px, py) in enumerate(chips)]
        for cp in sends:
            cp.start()
        passed = []
        for j, (px, py) in enumerate(chips):
            copy(j, 2 * px + py, c, (px, py, c)).wait_recv()
            fwd = copy(3 + j, 2 * px + py, c, sib)
            fwd.start()
            passed.append(fwd)
        for j, (px, py) in enumerate(chips):
            copy(3 + j, 2 * px + py, 1 - c, sib).wait_recv()
        for cp in sends + passed:
            cp.wait_send()

    return _pc_comm(body, name=name, in_specs=[ANY], out_specs=ANY,
                    out_shape=jax.ShapeDtypeStruct((N_CHIPS, 2, Rh, C), w.dtype),
                    scratch_shapes=[pltpu.SemaphoreType.DMA((6,)), pltpu.SemaphoreType.DMA((6,))])(w)


def _send_half_to_sibling(gbuf, *, name):
    _, S, Rh, C = gbuf.shape

    def body(g_ref, ra_ref, send_sem, recv_sem):
        x, y, c = _place()
        cp = pltpu.make_async_remote_copy(src_ref=g_ref.at[1 - c], dst_ref=ra_ref, send_sem=send_sem,
                                          recv_sem=recv_sem, device_id=(x, y, 1 - c), device_id_type=MESH)
        cp.start()
        cp.wait()

    return _pc_comm(body, name=name, in_specs=[ANY], out_specs=ANY, out_shape=jax.ShapeDtypeStruct((S, Rh, C), F32),
                    scratch_shapes=[pltpu.SemaphoreType.DMA, pltpu.SemaphoreType.DMA])(gbuf)


def _scatter_to_chips(p, *, name):
    S, Rh, C = p.shape

    def body(p_ref, rb_ref, send_sems, recv_sems):
        x, y, c = _place()
        chips = _other_chips(x, y)

        def copy(j, shard, to):
            return pltpu.make_async_remote_copy(src_ref=p_ref.at[shard], dst_ref=rb_ref.at[j], send_sem=send_sems.at[j],
                                                recv_sem=recv_sems.at[j], device_id=to, device_id_type=MESH)

        sends = [copy(j, 2 * px + py, (px, py, c)) for j, (px, py) in enumerate(chips)]
        for cp in sends:
            cp.start()
        for cp in sends:
            cp.wait()

    return _pc_comm(body, name=name, in_specs=[ANY], out_specs=ANY, out_shape=jax.ShapeDtypeStruct((3, Rh, C), p.dtype),
                    scratch_shapes=[pltpu.SemaphoreType.DMA((3,)), pltpu.SemaphoreType.DMA((3,))])(p)


def _swap_with_sibling(f, *, name):
    Rh, C = f.shape

    def body(f_ref, out_ref, send_sem, recv_sem):
        x, y, c = _place()
        cp = pltpu.make_async_remote_copy(src_ref=f_ref, dst_ref=out_ref, send_sem=send_sem, recv_sem=recv_sem,
                                          device_id=(x, y, 1 - c), device_id_type=MESH)
        cp.start()
        cp.wait()

    return _pc_comm(body, name=name, in_specs=[ANY], out_specs=ANY, out_shape=jax.ShapeDtypeStruct((Rh, C), F32),
                    scratch_shapes=[pltpu.SemaphoreType.DMA, pltpu.SemaphoreType.DMA])(f)


def _allgather_devices(v, *, name):
    R, C = v.shape

    def body(v_ref, out_ref, send_sems, recv_sems):
        x, y, c = _place()
        me = 4 * x + 2 * y + c
        out_ref[me] = v_ref[...]
        peers = []
        for k in range(1, 8):
            fx, fy, fc = (k >> 2) & 1, (k >> 1) & 1, k & 1
            px = 1 - x if fx else x
            py = 1 - y if fy else y
            pcc = 1 - c if fc else c
            peers.append((px, py, pcc))
        sends = []
        for k, peer in enumerate(peers):
            cp = pltpu.make_async_remote_copy(src_ref=v_ref, dst_ref=out_ref.at[me], send_sem=send_sems.at[k],
                                              recv_sem=recv_sems.at[k], device_id=peer, device_id_type=MESH)
            cp.start()
            sends.append(cp)
        for k, (px, py, pcc) in enumerate(peers):
            pltpu.make_async_remote_copy(src_ref=v_ref, dst_ref=out_ref.at[4 * px + 2 * py + pcc],
                                         send_sem=send_sems.at[k], recv_sem=recv_sems.at[k], device_id=peers[k],
                                         device_id_type=MESH).wait_recv()
        for cp in sends:
            cp.wait_send()

    vm = pl.BlockSpec(memory_space=pltpu.VMEM)
    return _pc_comm(body, name=name, in_specs=[vm], out_specs=vm, out_shape=jax.ShapeDtypeStruct((8, R, C), F32),
                    scratch_shapes=[pltpu.SemaphoreType.DMA((7,)), pltpu.SemaphoreType.DMA((7,))])(v)


D_MODEL = 1024
SC_W = D_MODEL // 4
GDN_W = D_MODEL // 2
SB_W = D_MODEL - SC_W - GDN_W
D_FF = 256 * ((8 * D_MODEL // 3 + 255) // 256)
O_SC, O_GQKV, O_GZ, O_GA, O_GB, O_SB = 0, 3 * SC_W, 3 * SC_W + 3 * GDN_W, 3 * SC_W + 4 * GDN_W, \
    3 * SC_W + 4 * GDN_W + GDN_HEADS, 3 * SC_W + 4 * GDN_W + 2 * GDN_HEADS
D_IN_PROJ = O_SB + 3 * SB_W
P_GQKV, P_SC, P_SB = 0, 3 * GDN_W, 3 * GDN_W + 3 * SC_W
P_GZ = P_SB + 3 * SB_W
P_GAB = P_GZ + GDN_W
P_PAD = 256
P_WIDTH = P_GAB + P_PAD


def _proj_to_kernel_layout(w):
    pad = jnp.zeros((w.shape[0], P_PAD - 2 * GDN_HEADS), w.dtype)
    return jnp.concatenate([w[:, O_GQKV:O_GZ], w[:, O_SC:O_GQKV], w[:, O_SB:], w[:, O_GZ:O_GA], w[:, O_GA:O_SB], pad],
                           axis=1)


def _proj_from_kernel_layout(g):
    return jnp.concatenate([g[:, P_SC:P_SB], g[:, P_GQKV:P_SC], g[:, P_GZ:P_GAB], g[:, P_GAB:P_GAB + 2 * GDN_HEADS],
                            g[:, P_SB:P_GZ]], axis=1)


def _mixout_to_kernel_layout(w):
    return jnp.concatenate([w[SC_W:SC_W + GDN_W], w[:SC_W], w[SC_W + GDN_W:]], axis=0)


def _mixout_from_kernel_layout(g):
    return jnp.concatenate([g[GDN_W:GDN_W + SC_W], g[:GDN_W], g[GDN_W + SC_W:]], axis=0)


def _pack_rows(parts, width, rows_to):
    flat = jnp.concatenate([p.reshape(-1, width) for p in parts], axis=0)
    return jnp.pad(flat, ((0, rows_to - flat.shape[0]), (0, 0)))


def _unpack_rows(flat, shapes, width):
    out, r = [], 0
    for shp in shapes:
        n = int(np.prod(shp)) // width
        out.append(flat[r:r + n].reshape(shp))
        r += n
    return out


def _pack_vec(parts, rows_to):
    flat = jnp.concatenate([p.reshape(-1) for p in parts])
    return jnp.pad(flat, (0, rows_to * LANES - flat.shape[0])).reshape(rows_to, LANES)


def _unpack_vec(mat, shapes):
    flat = mat.reshape(-1)
    out, r = [], 0
    for shp in shapes:
        n = int(np.prod(shp))
        out.append(flat[r:r + n].reshape(shp))
        r += n
    return out


def _round_up(n, m):
    return (n + m - 1) // m * m


def _layer_fwd(x, p, l):
    L = x.shape[0]
    tag = "l%d_" % l
    h = _rmsnorm_fwd(x, p["wn_mix"], name=tag + "norm_mix")
    proj = _matmul(h, p["w_in"], "nn", tm=512, tn=768, tk=D_MODEL, name=tag + "proj")
    (y_sc,) = _conv_pointwise_fwd([(proj, P_SC + SC_W), (proj, P_SC + 2 * SC_W)], [(p["w_sconv"], 0)], [(proj, P_SC)],
                                  _pre_product, _post_gate_mul, [(SC_W, BF16)], tc=SC_W, tm=512, name=tag + "sconv")
    (qkv,) = _conv_pointwise_fwd([(proj, P_GQKV)], [(p["w_gdn_conv"], 0)], [], _pre_identity, _post_silu,
                                 [(3 * GDN_W, F32)], tc=GDN_W, tm=512, name=tag + "gdn_conv")
    qe, ke, u, w, attn, eg = _gdn_prep_fwd(qkv, proj, P_GAB, p["a_log"], p["dt_bias"], name=tag + "gdn_prep")
    y_gdn, states = _gdn_scan_fwd(qe, ke, u, w, attn, eg, proj, P_GZ, p["wgn"], name=tag + "gdn_scan")
    sb_scale = (SB_W // SB_HEADS) ** -0.5
    sbqkv = jnp.concatenate([proj[:, P_SB:P_SB + SB_W] * sb_scale, proj[:, P_SB + SB_W:P_SB + 3 * SB_W]],
                            axis=1).astype(BF16)
    y_sb, ctot = _sb_fwd(sbqkv, name=tag + "sb_fwd")
    y_cat = [y_gdn, y_sc, y_sb]
    x2 = _matmul_rows_parts(y_cat, p["w_out"], "nn", res=x, name=tag + "mix_out")
    h2 = _rmsnorm_fwd(x2, p["wn_ffn"], name=tag + "norm_ffn")
    up_g = _matmul(h2, p["w_up_g"], "nn", tm=512, tn=D_FF // 2, tk=D_MODEL, name=tag + "up_gate")
    up_v = _matmul(h2, p["w_up_v"], "nn", tm=512, tn=D_FF // 2, tk=D_MODEL, name=tag + "up_val")
    (act,) = _conv_pointwise_fwd([(up_g, 0), (up_v, 0)], [(p["w_fconv_g"], 0), (p["w_fconv_v"], 0)], [],
                                 _pre_identity, _post_swiglu, [(D_FF, BF16)], tc=256, tm=512, name=tag + "ffn_act")
    x3 = _matmul(act, p["w_down"], "nn", tm=512, tn=D_MODEL, tk=D_FF // 2, res=x2, name=tag + "ffn_down")
    saved = dict(x=x, h=h, proj=proj, qkv=qkv, qe=qe, ke=ke, u=u, w=w, attn=attn, eg=eg, states=states,
                 sbqkv=sbqkv, ctot=ctot, y_cat=y_cat, x2=x2, h2=h2, up_g=up_g, up_v=up_v, act=act)
    return x3, saved


def _layer_bwd(dx3, p, s, l):
    L = dx3.shape[0]
    tag = "l%d_b_" % l
    g = {}
    dact = _matmul(dx3, p["w_down"], "nt", tm=512, tn=D_FF // 2, tk=D_MODEL, name=tag + "dact")
    g["w_down"] = _matmul(s["act"], dx3, "tn", tm=D_FF // 2, tn=D_MODEL, tk=512, name=tag + "dw_down")
    (dup_g, dup_v), _, (g["w_fconv_g"], g["w_fconv_v"]) = _conv_pointwise_bwd(
        [(s["up_g"], 0), (s["up_v"], 0)], [(p["w_fconv_g"], 0), (p["w_fconv_v"], 0)], [], [(dact, 0)],
        _pre_identity, _post_swiglu, D_FF, tc=256, tm=512, name=tag + "ffn_act")
    dh2 = _matmul(dup_g, p["w_up_g"], "nt", tm=512, tn=D_MODEL, tk=D_FF // 2, name=tag + "dh2_gate")
    dh2 = _matmul(dup_v, p["w_up_v"], "nt", tm=512, tn=D_MODEL, tk=D_FF // 2, res=dh2, name=tag + "dh2_val")
    g["w_up_g"] = _matmul(s["h2"], dup_g, "tn", tm=D_MODEL, tn=D_FF // 2, tk=512, name=tag + "dw_up_gate")
    g["w_up_v"] = _matmul(s["h2"], dup_v, "tn", tm=D_MODEL, tn=D_FF // 2, tk=512, name=tag + "dw_up_val")
    dx2, g["wn_ffn"] = _rmsnorm_bwd(dh2, s["x2"], p["wn_ffn"], dx3, name=tag + "norm_ffn")
    dycat = _matmul(dx2, p["w_out"], "nt", tm=512, tn=D_MODEL, tk=D_MODEL, name=tag + "dycat")
    g["w_out"] = _matmul_tn_parts(s["y_cat"], dx2, name=tag + "dw_out")
    sb_scale = (SB_W // SB_HEADS) ** -0.5
    dsq, dsk, dsv = _sb_bwd(s["sbqkv"], s["ctot"], dycat, GDN_W + SC_W, sb_scale, name=tag + "sb_bwd")
    dqe, dke, du, dw, dattn, deg, dgz, g["wgn"] = _gdn_scan_bwd(
        s["qe"], s["ke"], s["u"], s["w"], s["attn"], s["eg"], s["proj"], P_GZ, p["wgn"], s["states"], dycat, 0,
        name=tag + "gdn_scan")
    dqkv_act, dgab, g["a_log"], g["dt_bias"] = _gdn_prep_bwd(
        s["qkv"], s["proj"], P_GAB, p["a_log"], p["dt_bias"], dqe, dke, du, dw, dattn, deg, P_PAD,
        name=tag + "gdn_prep")
    (dqkv,), _, (g["w_gdn_conv"],) = _conv_pointwise_bwd(
        [(s["proj"], P_GQKV)], [(p["w_gdn_conv"], 0)], [], [(dqkv_act, 0)], _pre_identity, _post_silu, 3 * GDN_W,
        tc=GDN_W, tm=512, name=tag + "gdn_conv")
    (dsc_c, dsc_h), (dsc_b,), (g["w_sconv"],) = _conv_pointwise_bwd(
        [(s["proj"], P_SC + SC_W), (s["proj"], P_SC + 2 * SC_W)], [(p["w_sconv"], 0)], [(s["proj"], P_SC)],
        [(dycat, GDN_W)], _pre_product, _post_gate_mul, SC_W, tc=SC_W, tm=512, name=tag + "sconv")
    dproj = [dqkv, dsc_b, dsc_c, dsc_h, dsq, dsk, dsv, dgz, dgab]
    dh = _matmul_rows_parts(dproj, p["w_in"], "nt", name=tag + "dh")
    g["w_in"] = jnp.concatenate([_matmul_tn_parts(s["h"], dproj[:4], name=tag + "dw_in_a"),
                                 _matmul_tn_parts(s["h"], dproj[4:], name=tag + "dw_in_b")], axis=1)
    dx, g["wn_mix"] = _rmsnorm_bwd(dh, s["x"], p["wn_mix"], dx2, name=tag + "norm_mix")
    return dx, g


BIG = ("w_mix_in", "w_mix_out", "w_ffn_up", "w_ffn_down")
BIG_AXIS = {"w_mix_in": 2, "w_mix_out": 1, "w_ffn_up": 2, "w_ffn_down": 1}
SMALL_SHARDED = ("w_sconv", "w_gdn_conv", "w_ffn_conv")
SMALL_REPLICATED = ("w_norm_mix", "gdn_a_log", "gdn_dt_bias", "w_gdn_norm", "w_norm_ffn", "w_norm_final")
WEIGHTS = ("w_norm_mix", "w_mix_in", "w_sconv", "w_gdn_conv", "gdn_a_log", "gdn_dt_bias", "w_gdn_norm", "w_mix_out",
           "w_norm_ffn", "w_ffn_up", "w_ffn_conv", "w_ffn_down", "w_norm_final")


def _big_parts(d):
    depth = d["w_mix_in"].shape[0]
    return [d[n][l] for l in range(depth) for n in BIG]


def kernel(x, w_norm_mix, w_mix_in, w_sconv, w_gdn_conv, gdn_a_log, gdn_dt_bias, w_gdn_norm, w_mix_out, w_norm_ffn, w_ffn_up, w_ffn_conv, w_ffn_down, w_norm_final, loss_target, m_w_norm_mix, m_w_mix_in, m_w_sconv, m_w_gdn_conv, m_gdn_a_log, m_gdn_dt_bias, m_w_gdn_norm, m_w_mix_out, m_w_norm_ffn, m_w_ffn_up, m_w_ffn_conv, m_w_ffn_down, m_w_norm_final, v_w_norm_mix, v_w_mix_in, v_w_sconv, v_w_gdn_conv, v_gdn_a_log, v_gdn_dt_bias, v_w_gdn_norm, v_w_mix_out, v_w_norm_ffn, v_w_ffn_up, v_w_ffn_conv, v_w_ffn_down, v_w_norm_final):
    W = dict(w_norm_mix=w_norm_mix, w_mix_in=w_mix_in, w_sconv=w_sconv, w_gdn_conv=w_gdn_conv, gdn_a_log=gdn_a_log,
             gdn_dt_bias=gdn_dt_bias, w_gdn_norm=w_gdn_norm, w_mix_out=w_mix_out, w_norm_ffn=w_norm_ffn,
             w_ffn_up=w_ffn_up, w_ffn_conv=w_ffn_conv, w_ffn_down=w_ffn_down, w_norm_final=w_norm_final)
    M = dict(w_norm_mix=m_w_norm_mix, w_mix_in=m_w_mix_in, w_sconv=m_w_sconv, w_gdn_conv=m_w_gdn_conv,
             gdn_a_log=m_gdn_a_log, gdn_dt_bias=m_gdn_dt_bias, w_gdn_norm=m_w_gdn_norm, w_mix_out=m_w_mix_out,
             w_norm_ffn=m_w_norm_ffn, w_ffn_up=m_w_ffn_up, w_ffn_conv=m_w_ffn_conv, w_ffn_down=m_w_ffn_down,
             w_norm_final=m_w_norm_final)
    V = dict(w_norm_mix=v_w_norm_mix, w_mix_in=v_w_mix_in, w_sconv=v_w_sconv, w_gdn_conv=v_w_gdn_conv,
             gdn_a_log=v_gdn_a_log, gdn_dt_bias=v_gdn_dt_bias, w_gdn_norm=v_w_gdn_norm, w_mix_out=v_w_mix_out,
             w_norm_ffn=v_w_norm_ffn, w_ffn_up=v_w_ffn_up, w_ffn_conv=v_w_ffn_conv, w_ffn_down=v_w_ffn_down,
             w_norm_final=v_w_norm_final)
    depth = w_mix_in.shape[0]
    L = x.shape[1]
    mx, my, mc = lax.axis_index("x"), lax.axis_index("y"), lax.axis_index("c")
    chip = 2 * mx + my

    big_shapes = [a.shape for a in _big_parts(W)]
    rows = sum(int(np.prod(s)) // D_MODEL for s in big_shapes)
    rows_pad = _round_up(rows, 512)
    rh = rows_pad // 2
    w_flat = _pack_rows(_big_parts(W), D_MODEL, rows_pad)
    w_own = w_flat.astype(BF16).reshape(2, rh, D_MODEL)
    gathered = _allgather_chips(w_own, name="gather_big")
    gathered = lax.dynamic_update_slice(gathered, w_own[None], (chip, 0, 0, 0))
    gathered = gathered.reshape(N_CHIPS, rows_pad, D_MODEL)
    per_chip = [_unpack_rows(gathered[b], big_shapes, D_MODEL) for b in range(N_CHIPS)]
    full_big = []
    for l in range(depth):
        lay = {}
        for n_i, n in enumerate(BIG):
            lay[n] = jnp.concatenate([per_chip[b][l * len(BIG) + n_i] for b in range(N_CHIPS)], axis=BIG_AXIS[n] - 1)
        full_big.append(lay)

    small_sh_shapes = [W[n].shape for n in SMALL_SHARDED]
    n_small_sh = sum(int(np.prod(s)) for s in small_sh_shapes)
    small_rows = _round_up(n_small_sh, 8 * LANES) // LANES
    small_all = _allgather_devices(_pack_vec([W[n] for n in SMALL_SHARDED], small_rows), name="gather_small")
    small_chip = [_unpack_vec(small_all[2 * b], small_sh_shapes) for b in range(N_CHIPS)]
    full_small = {n: jnp.concatenate([small_chip[b][i] for b in range(N_CHIPS)], axis=2)
                  for i, n in enumerate(SMALL_SHARDED)}

    params = []
    for l in range(depth):
        w_up = full_big[l]["w_ffn_up"]
        fconv = full_small["w_ffn_conv"][l]
        params.append(dict(
            wn_mix=w_norm_mix[l], w_in=_proj_to_kernel_layout(full_big[l]["w_mix_in"]),
            w_sconv=full_small["w_sconv"][l], w_gdn_conv=full_small["w_gdn_conv"][l],
            a_log=gdn_a_log[l].reshape(GDN_HEADS, 1, 1), dt_bias=gdn_dt_bias[l].reshape(GDN_HEADS, 1, 1),
            wgn=w_gdn_norm[l], w_out=_mixout_to_kernel_layout(full_big[l]["w_mix_out"]), wn_ffn=w_norm_ffn[l],
            w_up_g=w_up[:, :D_FF], w_up_v=w_up[:, D_FF:], w_fconv_g=fconv[:, :D_FF], w_fconv_v=fconv[:, D_FF:],
            w_down=full_big[l]["w_ffn_down"]))

    xs = x[0]
    saved = []
    for l in range(depth):
        xs, s = _layer_fwd(xs, params[l], l)
        saved.append(s)
    loss_row, dx, g_norm_final = _final_loss(xs, w_norm_final, loss_target[0], name="final_loss")
    grads = [None] * depth
    for l in reversed(range(depth)):
        dx, grads[l] = _layer_bwd(dx, params[l], saved[l], l)
    loss = lax.psum(loss_row[0, 0], ("x", "y", "c"))

    G = {
        "w_mix_in": jnp.stack([_proj_from_kernel_layout(grads[l]["w_in"]) for l in range(depth)]),
        "w_mix_out": jnp.stack([_mixout_from_kernel_layout(grads[l]["w_out"]) for l in range(depth)]),
        "w_ffn_up": jnp.stack([jnp.concatenate([grads[l]["w_up_g"], grads[l]["w_up_v"]], axis=1)
                               for l in range(depth)]),
        "w_ffn_down": jnp.stack([grads[l]["w_down"] for l in range(depth)]),
        "w_sconv": jnp.stack([grads[l]["w_sconv"] for l in range(depth)]),
        "w_gdn_conv": jnp.stack([grads[l]["w_gdn_conv"] for l in range(depth)]),
        "w_ffn_conv": jnp.stack([jnp.concatenate([grads[l]["w_fconv_g"], grads[l]["w_fconv_v"]], axis=1)
                                 for l in range(depth)]),
        "w_norm_mix": jnp.stack([grads[l]["wn_mix"].reshape(-1) for l in range(depth)]),
        "gdn_a_log": jnp.stack([grads[l]["a_log"].reshape(-1) for l in range(depth)]),
        "gdn_dt_bias": jnp.stack([grads[l]["dt_bias"].reshape(-1) for l in range(depth)]),
        "w_gdn_norm": jnp.stack([grads[l]["wgn"].reshape(-1) for l in range(depth)]),
        "w_norm_ffn": jnp.stack([grads[l]["wn_ffn"].reshape(-1) for l in range(depth)]),
        "w_norm_final": g_norm_final.reshape(-1),
    }

    def shard_of(a, n, b):
        width = a.shape[BIG_AXIS[n]] // N_CHIPS
        return lax.slice_in_dim(a, b * width, (b + 1) * width, axis=BIG_AXIS[n])

    gbuf = jnp.stack([_pack_rows(_big_parts({n: shard_of(G[n], n, b) for n in BIG}), D_MODEL, rows_pad)
                      for b in range(N_CHIPS)])
    gbuf = gbuf.reshape(N_CHIPS, 2, rh, D_MODEL).transpose(1, 0, 2, 3)
    from_sibling = _send_half_to_sibling(gbuf, name="rs_sibling")
    chip_sum = _add_halves(gbuf, from_sibling, mc, name="rs_add_halves")
    from_chips = _scatter_to_chips(chip_sum, name="rs_chips")
    my_half = _add_chips(chip_sum, from_chips, chip, name="rs_add_chips")
    sib_half = _swap_with_sibling(my_half, name="rs_result")
    g_flat = jnp.where(mc == 0, jnp.stack([my_half, sib_half]), jnp.stack([sib_half, my_half]))
    g_flat = g_flat.reshape(rows_pad, D_MODEL)
    m_flat = _pack_rows(_big_parts(M), D_MODEL, rows_pad)
    v_flat = _pack_rows(_big_parts(V), D_MODEL, rows_pad)
    d_flat, nm_flat, nv_flat = _adamw(w_flat, g_flat, m_flat, v_flat, name="adamw_big", tm=256)
    out_g, out_d, out_m, out_v = {}, {}, {}, {}
    for flat, dst in ((g_flat, out_g), (d_flat, out_d), (nm_flat, out_m), (nv_flat, out_v)):
        parts = _unpack_rows(flat, big_shapes, D_MODEL)
        for n_i, n in enumerate(BIG):
            dst[n] = jnp.stack([parts[l * len(BIG) + n_i] for l in range(depth)])

    small_names = SMALL_SHARDED + SMALL_REPLICATED
    small_full_shapes = [G[n].shape for n in small_names]
    n_small = sum(int(np.prod(s)) for s in small_full_shapes)
    red_rows = _round_up(n_small, 8 * LANES) // LANES
    partials = _allgather_devices(_pack_vec([G[n] for n in small_names], red_rows), name="reduce_small")
    summed = _unpack_vec(_sum_slots(partials, name="reduce_small_sum"), small_full_shapes)
    g_small = {}
    for n, a in zip(small_names, summed):
        if n in SMALL_SHARDED:
            width = a.shape[2] // N_CHIPS
            a = lax.dynamic_slice_in_dim(a, chip * width, width, axis=2)
        g_small[n] = a
    own_shapes = [W[n].shape for n in small_names]
    n_own = sum(int(np.prod(s)) for s in own_shapes)
    own_rows = _round_up(n_own, 8 * LANES) // LANES
    packed = [_pack_vec([src[n] for n in small_names], own_rows) for src in (W, g_small, M, V)]
    d_s, nm_s, nv_s = _adamw(*packed, name="adamw_small", tm=own_rows)
    for mat, dst in ((packed[1], out_g), (d_s, out_d), (nm_s, out_m), (nv_s, out_v)):
        for n, a in zip(small_names, _unpack_vec(mat, own_shapes)):
            dst[n] = a

    outs = [loss, dx[None]]
    for dst in (out_g, out_d, out_m, out_v):
        outs += [dst[n] for n in WEIGHTS]
    return tuple(outs)
```

```python
import functools

import jax
import jax.numpy as jnp
import numpy as np
from jax import lax
from jax.experimental import pallas as pl
from jax.experimental.pallas import tpu as pltpu

F32 = jnp.float32
BF16 = jnp.bfloat16
MESH = pl.DeviceIdType.MESH

NORM_EPS = 1e-6
GDN_HEADS = 4
GDN_CHUNK = 64
GDN_CONV = 4
SB_HEADS = 4
SB_UNROLL = 4
SC_KERNEL = 3
FFN_CONV = 3
ADAM_LR = 0.001
ADAM_B1 = 0.9
ADAM_B2 = 0.999
ADAM_EPS = 1e-08
ADAM_WD = 0.01
ADAM_STEP = 10

VMEM_LIMIT_BYTES = 48 * 1024 * 1024
HALO = 8
LANES = 128
N_CHIPS = 4


def _pc(body, *, name, grid, in_specs, out_specs, out_shape, scratch_shapes=(), dims=None, aliases=None):
    params = dict(vmem_limit_bytes=VMEM_LIMIT_BYTES)
    if dims is not None:
        params["dimension_semantics"] = dims
    return pl.pallas_call(body, name=name, grid=grid, in_specs=in_specs, out_specs=out_specs, out_shape=out_shape,
                          scratch_shapes=list(scratch_shapes), input_output_aliases=aliases or {},
                          compiler_params=pltpu.CompilerParams(**params))


def _pc_prefetch(body, *, name, grid_spec, out_shape, dims):
    return pl.pallas_call(body, name=name, grid_spec=grid_spec, out_shape=out_shape,
                          compiler_params=pltpu.CompilerParams(vmem_limit_bytes=VMEM_LIMIT_BYTES,
                                                               dimension_semantics=dims))


def _pc_comm(body, *, name, in_specs, out_specs, out_shape, scratch_shapes):
    return pl.pallas_call(body, name=name, in_specs=in_specs, out_specs=out_specs, out_shape=out_shape,
                          scratch_shapes=list(scratch_shapes),
                          compiler_params=pltpu.CompilerParams(vmem_limit_bytes=VMEM_LIMIT_BYTES))


_DIMS = {"nn": (((1,), (0,)), ((), ())), "nt": (((1,), (1,)), ((), ())), "tn": (((0,), (0,)), ((), ()))}


def _matmul(a, b, mode, *, name, tm=512, tn=512, tk=512, out_dtype=F32, res=None, slabs=None):
    if mode == "nn":
        (M, K), (K2, N) = a.shape, b.shape
    elif mode == "nt":
        (M, K), (N, K2) = a.shape, b.shape
    else:
        (K, M), (K2, N) = a.shape, b.shape
    assert K == K2, (a.shape, b.shape, mode)
    tm, tn, tk = min(tm, M), min(tn, N), min(tk, K)
    assert M % tm == 0 and N % tn == 0 and K % tk == 0, (M, N, K, tm, tn, tk)
    nk = K // tk
    if mode == "tn":
        a_spec = pl.BlockSpec((tk, tm), lambda i, j, k: (k, i))
    else:
        a_spec = pl.BlockSpec((tm, tk), lambda i, j, k: (i, k))
    if mode == "nt":
        b_spec = pl.BlockSpec((tn, tk), lambda i, j, k: (j, k))
    else:
        b_spec = pl.BlockSpec((tk, tn), lambda i, j, k: (k, j))
    o_spec = pl.BlockSpec((tm, tn), lambda i, j, k: (i, j))
    has_res = res is not None
    dn = _DIMS[mode]

    def body(*refs):
        if has_res:
            a_ref, b_ref, r_ref, o_ref, acc = refs
        else:
            a_ref, b_ref, o_ref, acc = refs
        k = pl.program_id(2)
        p = lax.dot_general(a_ref[...].astype(BF16), b_ref[...].astype(BF16), dn, preferred_element_type=F32)

        def finish(total):
            if has_res:
                total = total + r_ref[...].astype(F32)
            o_ref[...] = total.astype(out_dtype)

        if nk == 1:
            finish(p)
        else:
            @pl.when(k == 0)
            def _():
                acc[...] = p

            @pl.when(k > 0)
            def _():
                acc[...] += p

            @pl.when(k == nk - 1)
            def _():
                finish(acc[...])

    in_specs = [a_spec, b_spec] + ([o_spec] if has_res else [])
    args = (a, b) + ((res,) if has_res else ())
    out_shape = jax.ShapeDtypeStruct((M, N), out_dtype)
    aliases = None
    if slabs is not None:
        n_slabs, first, into = slabs
        assert not has_res and tm == M
        o_spec = pl.BlockSpec((None, tm, tn), lambda i, j, k: (j + first, i, 0))
        out_shape = jax.ShapeDtypeStruct((n_slabs, M, tn), out_dtype)
        if into is not None:
            in_specs.append(pl.BlockSpec(memory_space=pl.ANY))
            args = args + (into,)
            aliases = {2: 0}
            inner = body

            def body(a_ref, b_ref, into_ref, o_ref, acc):
                inner(a_ref, b_ref, o_ref, acc)
    return _pc(body, name=name, grid=(M // tm, N // tn, nk), in_specs=in_specs, out_specs=o_spec,
               out_shape=out_shape, scratch_shapes=[pltpu.VMEM((tm, tn), F32)],
               dims=("parallel", "parallel", "arbitrary"), aliases=aliases)(*args)


def _offsets(parts, own_width_aligned):
    offs, at = [], 0
    for p in parts:
        assert at % (p.shape[1] if own_width_aligned else LANES) == 0, (at, p.shape)
        offs.append(at)
        at += p.shape[1]
    return offs, at


def _matmul_rows_parts(parts, w, mode, *, name, tm=512, res=None):
    M = parts[0].shape[0]
    offs, K = _offsets(parts, True)
    tm = min(tm, M)
    N = w.shape[1] if mode == "nn" else w.shape[0]
    assert (w.shape[0] if mode == "nn" else w.shape[1]) == K
    has_res = res is not None
    n = len(parts)

    def body(*refs):
        o_ref = refs[-1]
        total = None
        for s in range(n):
            p = lax.dot_general(refs[s][...].astype(BF16), refs[n + s][...].astype(BF16), _DIMS[mode],
                                preferred_element_type=F32)
            total = p if total is None else total + p
        if has_res:
            total = total + refs[2 * n][...]
        o_ref[...] = total

    in_specs = [pl.BlockSpec((tm, p.shape[1]), lambda i: (i, 0)) for p in parts]
    for p, off in zip(parts, offs):
        blk = off // p.shape[1]
        if mode == "nn":
            in_specs.append(pl.BlockSpec((p.shape[1], N), lambda i, blk=blk: (blk, 0)))
        else:
            in_specs.append(pl.BlockSpec((N, p.shape[1]), lambda i, blk=blk: (0, blk)))
    o_spec = pl.BlockSpec((tm, N), lambda i: (i, 0))
    args = tuple(parts) + (w,) * n + ((res,) if has_res else ())
    return _pc(body, name=name, grid=(M // tm,), in_specs=in_specs + ([o_spec] if has_res else []), out_specs=o_spec,
               out_shape=jax.ShapeDtypeStruct((M, N), F32), dims=("parallel",))(*args)


def _matmul_tn_parts(a, b, *, name, tk=512):
    a_parts = list(a) if isinstance(a, (list, tuple)) else [a]
    b_parts = list(b) if isinstance(b, (list, tuple)) else [b]
    assert len(a_parts) == 1 or len(b_parts) == 1
    a_offs, M = _offsets(a_parts, False)
    b_offs, N = _offsets(b_parts, False)
    K = a_parts[0].shape[0]
    tk = min(tk, K)
    na, nb = len(a_parts), len(b_parts)

    def body(*refs):
        o_ref = refs[-1]
        first = pl.program_id(0) == 0
        for s in range(na):
            for t in range(nb):
                p = lax.dot_general(refs[s][...].astype(BF16), refs[na + t][...].astype(BF16), _DIMS["tn"],
                                    preferred_element_type=F32)
                rows = slice(a_offs[s], a_offs[s] + a_parts[s].shape[1])
                cols = slice(b_offs[t], b_offs[t] + b_parts[t].shape[1])

                @pl.when(first)
                def _(p=p, rows=rows, cols=cols):
                    o_ref[rows, cols] = p

                @pl.when(jnp.logical_not(first))
                def _(p=p, rows=rows, cols=cols):
                    o_ref[rows, cols] += p

    in_specs = [pl.BlockSpec((tk, p.shape[1]), lambda k: (k, 0)) for p in a_parts + b_parts]
    return _pc(body, name=name, grid=(K // tk,), in_specs=in_specs, out_specs=pl.BlockSpec((M, N), lambda k: (0, 0)),
               out_shape=jax.ShapeDtypeStruct((M, N), F32), dims=("arbitrary",))(*a_parts, *b_parts)


def _rmsnorm_fwd(x, w, *, name, tm=512):
    L, D = x.shape
    tm = min(tm, L)

    def body(x_ref, w_ref, h_ref):
        xv = x_ref[...]
        r = lax.rsqrt(jnp.mean(xv * xv, axis=-1, keepdims=True) + NORM_EPS)
        h_ref[...] = (xv * r * w_ref[...]).astype(BF16)

    return _pc(body, name=name, grid=(L // tm,),
               in_specs=[pl.BlockSpec((tm, D), lambda i: (i, 0)), pl.BlockSpec((1, D), lambda i: (0, 0))],
               out_specs=pl.BlockSpec((tm, D), lambda i: (i, 0)), out_shape=jax.ShapeDtypeStruct((L, D), BF16),
               dims=("parallel",))(x, w.reshape(1, D))


def _rmsnorm_bwd(dh, x, w, dres, *, name, tm=512):
    L, D = x.shape
    tm = min(tm, L)

    def body(dh_ref, x_ref, w_ref, dres_ref, dx_ref, dw_ref):
        xv = x_ref[...]
        r = lax.rsqrt(jnp.mean(xv * xv, axis=-1, keepdims=True) + NORM_EPS)
        xhat = xv * r
        dhv = dh_ref[...]
        g = dhv * w_ref[...]
        dx_ref[...] = dres_ref[...] + r * (g - xhat * jnp.mean(g * xhat, axis=-1, keepdims=True))
        part = jnp.sum(dhv * xhat, axis=0, keepdims=True)

        @pl.when(pl.program_id(0) == 0)
        def _():
            dw_ref[...] = part

        @pl.when(pl.program_id(0) > 0)
        def _():
            dw_ref[...] += part

    row = pl.BlockSpec((tm, D), lambda i: (i, 0))
    vec = pl.BlockSpec((1, D), lambda i: (0, 0))
    return _pc(body, name=name, grid=(L // tm,), in_specs=[row, row, vec, row], out_specs=[row, vec],
               out_shape=[jax.ShapeDtypeStruct((L, D), F32), jax.ShapeDtypeStruct((1, D), F32)],
               dims=("arbitrary",))(dh, x, w.reshape(1, D), dres)


def _final_loss(x, w, tgt, *, name, tm=512):
    L, D = x.shape
    tm = min(tm, L)

    def body(x_ref, w_ref, t_ref, loss_ref, dx_ref, dw_ref):
        xv = x_ref[...]
        r = lax.rsqrt(jnp.mean(xv * xv, axis=-1, keepdims=True) + NORM_EPS)
        xhat = xv * r
        e = xhat * w_ref[...] - t_ref[...]
        lpart = jnp.broadcast_to(0.5 * jnp.sum(jnp.mean(e * e, axis=-1, keepdims=True), axis=0, keepdims=True),
                                 (1, LANES))
        dy = e * (1.0 / D)
        g = dy * w_ref[...]
        dx_ref[...] = r * (g - xhat * jnp.mean(g * xhat, axis=-1, keepdims=True))
        part = jnp.sum(dy * xhat, axis=0, keepdims=True)

        @pl.when(pl.program_id(0) == 0)
        def _():
            dw_ref[...] = part
            loss_ref[...] = lpart

        @pl.when(pl.program_id(0) > 0)
        def _():
            dw_ref[...] += part
            loss_ref[...] += lpart

    row = pl.BlockSpec((tm, D), lambda i: (i, 0))
    vec = pl.BlockSpec((1, D), lambda i: (0, 0))
    lsp = pl.BlockSpec((1, LANES), lambda i: (0, 0))
    return _pc(body, name=name, grid=(L // tm,), in_specs=[row, vec, row], out_specs=[lsp, row, vec],
               out_shape=[jax.ShapeDtypeStruct((1, LANES), F32), jax.ShapeDtypeStruct((L, D), F32),
                          jax.ShapeDtypeStruct((1, D), F32)],
               dims=("arbitrary",))(x, w.reshape(1, D), tgt)


def _shift_down(x, prev, k):
    if k == 0:
        return x
    r = pltpu.roll(x, k, 0)
    p = pltpu.roll(prev, k, 0)
    row = lax.broadcasted_iota(jnp.int32, p.shape, 0)
    head = jnp.where(row < k, p, r[:HALO])
    return jnp.concatenate([head, r[HALO:]], axis=0)


def _shift_up(x, j):
    if j == 0:
        return x
    return pltpu.roll(x, x.shape[0] - j, 0)


def _silu(x):
    return x * jax.nn.sigmoid(x)


def _conv_taps(p, p_prev, w):
    K = w.shape[0]
    out = None
    for k in range(K):
        term = w[k:k + 1, :] * _shift_down(p, p_prev, K - 1 - k)
        out = term if out is None else out + term
    return out


def _conv_pointwise_fwd(xs, ws, es, pre, post, outs, *, tc, tm, name):
    L = xs[0][0].shape[0]
    tm = min(tm, L)
    ncol = outs[0][0] // tc
    nrow = L // tm
    hb = tm // HALO
    nx, nw, ne, no = len(xs), len(ws), len(es), len(outs)
    K = ws[0][0].shape[0]

    def body(*refs):
        xc = [refs[2 * n][...] for n in range(nx)]
        i = pl.program_id(1)
        first = (i > 0).astype(F32)
        xp = [refs[2 * n + 1][...] * first for n in range(nx)]
        wv = [refs[2 * nx + n][...] for n in range(nw)]
        ev = [refs[2 * nx + nw + n][...] for n in range(ne)]
        o_refs = refs[2 * nx + nw + ne:]
        ps, pps = pre(*xc), pre(*xp)
        us = [_conv_taps(p, pp, w) for p, pp, w in zip(ps, pps, wv)]
        for o_ref, val in zip(o_refs, post(us, ev)):
            o_ref[...] = val.astype(o_ref.dtype)

    in_specs, args = [], []
    for arr, c0 in xs:
        off = c0 // tc
        in_specs.append(pl.BlockSpec((tm, tc), lambda j, i, off=off: (i, j + off)))
        in_specs.append(pl.BlockSpec((HALO, tc), lambda j, i, off=off: (jnp.maximum(i * hb - 1, 0), j + off)))
        args += [arr, arr]
    for arr, c0 in ws:
        off = c0 // tc
        in_specs.append(pl.BlockSpec((K, tc), lambda j, i, off=off: (0, j + off)))
        args.append(arr)
    for arr, c0 in es:
        off = c0 // tc
        in_specs.append(pl.BlockSpec((tm, tc), lambda j, i, off=off: (i, j + off)))
        args.append(arr)
    out_specs = [pl.BlockSpec((tm, tc), lambda j, i: (i, j)) for _ in range(no)]
    out_shape = [jax.ShapeDtypeStruct((L, c), dt) for c, dt in outs]
    return _pc(body, name=name, grid=(ncol, nrow), in_specs=in_specs, out_specs=out_specs, out_shape=out_shape,
               dims=("parallel", "parallel"))(*args)


def _conv_pointwise_bwd(xs, ws, es, dys, pre, post, width, *, tc, tm, name, out_dtype=BF16):
    L = xs[0][0].shape[0]
    tm = min(tm, L)
    ncol = width // tc
    nrow = L // tm
    hb = tm // HALO
    nx, nw, ne, ny = len(xs), len(ws), len(es), len(dys)
    K = ws[0][0].shape[0]

    def body(*refs):
        i = pl.program_id(1)
        first = (i > 0).astype(F32)
        more = (i < nrow - 1).astype(F32)
        pos = 0
        xc, xp, xe = [], [], []
        for n in range(nx):
            cur, prv, nxt = refs[pos][...], refs[pos + 1][...], refs[pos + 2][...]
            pos += 3
            xc.append(cur)
            xp.append(prv * first)
            xe.append(jnp.concatenate([cur, nxt], axis=0))
        wv = [refs[pos + n][...] for n in range(nw)]
        pos += nw
        ee = []
        for n in range(ne):
            ee.append(jnp.concatenate([refs[pos][...], refs[pos + 1][...]], axis=0))
            pos += 2
        dye = []
        for n in range(ny):
            dye.append(jnp.concatenate([refs[pos][...].astype(F32), refs[pos + 1][...].astype(F32) * more], axis=0))
            pos += 2
        dx_refs = refs[pos:pos + nx]
        de_refs = refs[pos + nx:pos + nx + ne]
        dw_refs = refs[pos + nx + ne:pos + nx + ne + nw]

        ps, pps = pre(*xe), pre(*xp)
        shifted = [[_shift_down(p, pp, K - 1 - k) for k in range(K)] for p, pp in zip(ps, pps)]
        us = []
        for n in range(nw):
            u = None
            for k in range(K):
                term = wv[n][k:k + 1, :] * shifted[n][k]
                u = term if u is None else u + term
            us.append(u)
        _, post_vjp = jax.vjp(lambda u_, e_: post(u_, e_), us, ee)
        dus, des = post_vjp(dye)
        dps = []
        for n in range(nw):
            dp = None
            for k in range(K):
                term = wv[n][k:k + 1, :] * _shift_up(dus[n], K - 1 - k)[:tm]
                dp = term if dp is None else dp + term
            dps.append(dp)
            for k in range(K):
                part = jnp.sum(dus[n][:tm] * shifted[n][k][:tm], axis=0, keepdims=True)

                @pl.when(i == 0)
                def _(part=part, n=n, k=k):
                    dw_refs[n][k:k + 1, :] = part

                @pl.when(i > 0)
                def _(part=part, n=n, k=k):
                    dw_refs[n][k:k + 1, :] += part
        _, pre_vjp = jax.vjp(lambda *x_: pre(*x_), *xc)
        dxs = pre_vjp(dps)
        for r, v in zip(dx_refs, dxs):
            r[...] = v.astype(out_dtype)
        for r, v in zip(de_refs, des):
            r[...] = v[:tm].astype(out_dtype)

    in_specs, args = [], []

    def add_rows(arr, c0, prev, nxt):
        off = c0 // tc
        in_specs.append(pl.BlockSpec((tm, tc), lambda j, i, off=off: (i, j + off)))
        args.append(arr)
        if prev:
            in_specs.append(pl.BlockSpec((HALO, tc), lambda j, i, off=off: (jnp.maximum(i * hb - 1, 0), j + off)))
            args.append(arr)
        if nxt:
            last = L // HALO - 1
            in_specs.append(pl.BlockSpec((HALO, tc), lambda j, i, off=off: (jnp.minimum((i + 1) * hb, last), j + off)))
            args.append(arr)

    for arr, c0 in xs:
        add_rows(arr, c0, True, True)
    for arr, c0 in ws:
        off = c0 // tc
        in_specs.append(pl.BlockSpec((K, tc), lambda j, i, off=off: (0, j + off)))
        args.append(arr)
    for arr, c0 in es:
        add_rows(arr, c0, False, True)
    for arr, c0 in dys:
        add_rows(arr, c0, False, True)
    tile = pl.BlockSpec((tm, tc), lambda j, i: (i, j))
    wtile = pl.BlockSpec((K, tc), lambda j, i: (0, j))
    out_specs = [tile] * (nx + ne) + [wtile] * nw
    out_shape = [jax.ShapeDtypeStruct((L, width), out_dtype)] * (nx + ne) + \
        [jax.ShapeDtypeStruct((K, width), F32)] * nw
    res = _pc(body, name=name, grid=(ncol, nrow), in_specs=in_specs, out_specs=out_specs, out_shape=out_shape,
              dims=("parallel", "arbitrary"))(*args)
    return res[:nx], res[nx:nx + ne], res[nx + ne:]


def _pre_identity(*x):
    return list(x)


def _pre_product(c, h):
    return [c * h]


def _post_silu(us, es):
    return [_silu(us[0])]


def _post_gate_mul(us, es):
    return [es[0] * us[0]]


def _post_swiglu(us, es):
    return [_silu(us[0]) * us[1]]


def _make_dot(passes):
    def raw(a, b, dn):
        a_hi = a.astype(BF16)
        b_hi = b.astype(BF16)
        out = lax.dot_general(a_hi, b_hi, dn, preferred_element_type=F32)
        if passes == 3:
            a_lo = (a - a_hi.astype(F32)).astype(BF16)
            b_lo = (b - b_hi.astype(F32)).astype(BF16)
            out = out + lax.dot_general(a_hi, b_lo, dn, preferred_element_type=F32)
            out = out + lax.dot_general(a_lo, b_hi, dn, preferred_element_type=F32)
        return out

    @jax.custom_vjp
    def nn(a, b):
        return raw(a, b, _DIMS["nn"])

    @jax.custom_vjp
    def nt(a, b):
        return raw(a, b, _DIMS["nt"])

    @jax.custom_vjp
    def tn(a, b):
        return raw(a, b, _DIMS["tn"])

    nn.defvjp(lambda a, b: (nn(a, b), (a, b)), lambda r, g: (nt(g, r[1]), tn(r[0], g)))
    nt.defvjp(lambda a, b: (nt(a, b), (a, b)), lambda r, g: (nn(g, r[1]), tn(g, r[0])))
    tn.defvjp(lambda a, b: (tn(a, b), (a, b)), lambda r, g: (nt(r[1], g), nn(r[0], g)))
    return nn, nt, tn


_NN1, _NT1, _TN1 = _make_dot(1)
_NN3, _NT3, _TN3 = _make_dot(3)


def _l2norm(x):
    return x * lax.rsqrt(jnp.sum(x * x, axis=-1, keepdims=True) + NORM_EPS)


def _unit_lower_inverse_raw(a_list):
    C = a_list[0].shape[0]
    ii = lax.broadcasted_iota(jnp.int32, (C, C), 0)
    jj = lax.broadcasted_iota(jnp.int32, (C, C), 1)
    eye = jnp.where(ii == jj, 1.0, 0.0)
    ts = [eye - a for a in a_list]
    ps = list(a_list)
    n = 2
    while n < C:
        ps = [_NN3(p, p) for p in ps]
        ts = [t + _NN3(t, p) for t, p in zip(ts, ps)]
        n *= 2
    return ts


@jax.custom_vjp
def _unit_lower_inverse(a_list):
    return _unit_lower_inverse_raw(a_list)


def _unit_lower_inverse_fwd(a_list):
    ts = _unit_lower_inverse_raw(a_list)
    return ts, ts


def _unit_lower_inverse_bwd(ts, gs):
    xs = [_TN3(t, g) for t, g in zip(ts, gs)]
    return ([-_NT3(x, t) for x, t in zip(xs, ts)],)


_unit_lower_inverse.defvjp(_unit_lower_inverse_fwd, _unit_lower_inverse_bwd)


def _gdn_prep(units):
    C, Dh = units[0][0].shape
    ii = lax.broadcasted_iota(jnp.int32, (C, C), 0)
    jj = lax.broadcasted_iota(jnp.int32, (C, C), 1)
    lane = lax.broadcasted_iota(jnp.int32, (1, C), 1)
    causal = ii >= jj
    strict = ii > jj
    qs = [_l2norm(un[0]) * (Dh ** -0.5) for un in units]
    ks = [_l2norm(un[1]) for un in units]
    betas = [jax.nn.sigmoid(un[4]) for un in units]
    gs = [-jnp.exp(un[5]) * jax.nn.softplus(un[3] + un[6]) for un in units]
    gc_rows = [jnp.sum(jnp.where(ii <= jj, g, 0.0), axis=0, keepdims=True) for g in gs]
    gc_cols = [jnp.sum(jnp.where(ii == jj, r, 0.0), axis=1, keepdims=True) for r in gc_rows]
    decays = [jnp.where(causal, jnp.exp(jnp.where(causal, c - r, 0.0)), 0.0) for c, r in zip(gc_cols, gc_rows)]
    kbs = [k * b for k, b in zip(ks, betas)]
    kks = [_NT1(kb, k) for kb, k in zip(kbs, ks)]
    qks = [_NT1(q, k) for q, k in zip(qs, ks)]
    ts = _unit_lower_inverse([jnp.where(strict, kk * d, 0.0) for kk, d in zip(kks, decays)])
    eg_cols = [jnp.exp(c) for c in gc_cols]
    uws = [_NN3(t, jnp.concatenate([un[2] * b, kb * e], axis=1))
           for t, un, b, kb, e in zip(ts, units, betas, kbs, eg_cols)]
    out = []
    for q, k, qk, d, uw, e, r, c in zip(qs, ks, qks, decays, uws, eg_cols, gc_rows, gc_cols):
        g_last = jnp.sum(jnp.where(lane == C - 1, r, 0.0), axis=1, keepdims=True)
        out.append((q * e, k * jnp.exp(g_last - c), uw[:, :Dh], uw[:, Dh:], jnp.where(causal, qk * d, 0.0),
                    jnp.broadcast_to(jnp.exp(g_last), (1, Dh))))
    return out


def _gdn_step(units):
    v_news = [un[3] - _NN1(un[4], un[0]) for un in units]
    o_state = [_NN1(un[1], un[0]) for un in units]
    o_intra = [_NN1(un[5], vn) for un, vn in zip(units, v_news)]
    s_adds = [_TN1(un[2], vn) for un, vn in zip(units, v_news)]
    out = []
    for un, a, b, s_add in zip(units, o_state, o_intra, s_adds):
        o = a + b
        y = o * lax.rsqrt(jnp.mean(o * o, axis=-1, keepdims=True) + NORM_EPS) * un[8] * _silu(un[7])
        out.append((y, un[0] * un[6] + s_add))
    return out


def _gdn_prep_fwd(qkv, gab, gab_col, a_log, dt_bias, *, name, chunks=4):
    L = qkv.shape[0]
    H, C = GDN_HEADS, GDN_CHUNK
    W = qkv.shape[1] // 3
    Dh = W // H
    N = L // C
    chunks = min(chunks, N)
    R = chunks * C
    gab_off = gab_col // LANES

    def body(q_ref, k_ref, v_ref, gab_ref, al_ref, dt_ref, qe_ref, ke_ref, u_ref, w_ref, at_ref, eg_ref):
        where = [(cc, h) for cc in range(chunks) for h in range(H)]
        units = []
        for cc, h in where:
            rows, sl = slice(cc * C, (cc + 1) * C), slice(h * Dh, (h + 1) * Dh)
            units.append((q_ref[rows, sl], k_ref[rows, sl], v_ref[rows, sl], gab_ref[rows, h:h + 1],
                          gab_ref[rows, H + h:H + h + 1], al_ref[h], dt_ref[h]))
        for (cc, h), (qe, ke, u, w, attn, eg) in zip(where, _gdn_prep(units)):
            rows, sl = slice(cc * C, (cc + 1) * C), slice(h * Dh, (h + 1) * Dh)
            qe_ref[rows, sl] = qe
            ke_ref[rows, sl] = ke
            u_ref[rows, sl] = u
            w_ref[rows, sl] = w
            at_ref[h, rows, :] = attn
            eg_ref[cc, h:h + 1, :] = eg

    col = lambda c: pl.BlockSpec((R, W), lambda n, c=c: (n, c))
    tok = pl.BlockSpec((R, LANES), lambda n: (n, gab_off))
    par = pl.BlockSpec((H, 1, 1), lambda n: (0, 0, 0))
    wide = pl.BlockSpec((R, W), lambda n: (n, 0))
    return _pc(body, name=name, grid=(N // chunks,), in_specs=[col(0), col(1), col(2), tok, par, par],
               out_specs=[wide, wide, wide, wide, pl.BlockSpec((H, R, C), lambda n: (0, n, 0)),
                          pl.BlockSpec((chunks, H, Dh), lambda n: (n, 0, 0))],
               out_shape=[jax.ShapeDtypeStruct((L, W), F32)] * 4 + [jax.ShapeDtypeStruct((H, L, C), F32),
                                                                   jax.ShapeDtypeStruct((N, H, Dh), F32)],
               dims=("parallel",))(qkv, qkv, qkv, gab, a_log, dt_bias)


def _gdn_prep_bwd(qkv, gab, gab_col, a_log, dt_bias, dqe, dke, du, dw, dattn, deg, gab_width, *, name, chunks=4):
    L = qkv.shape[0]
    H, C = GDN_HEADS, GDN_CHUNK
    W = qkv.shape[1] // 3
    Dh = W // H
    N = L // C
    chunks = min(chunks, N)
    R = chunks * C
    gab_off = gab_col // LANES

    def body(q_ref, k_ref, v_ref, gab_ref, al_ref, dt_ref, dqe_ref, dke_ref, du_ref, dw_ref, dat_ref, deg_ref,
             dqkv_ref, dgab_ref, dal_ref, ddt_ref):
        first = pl.program_id(0) == 0
        lane = lax.broadcasted_iota(jnp.int32, (C, gab_width), 1)
        dal_sum, ddt_sum = [None] * H, [None] * H
        where = [(cc, h) for cc in range(chunks) for h in range(H)]
        units, cots = [], []
        for cc, h in where:
            rows, sl = slice(cc * C, (cc + 1) * C), slice(h * Dh, (h + 1) * Dh)
            units.append((q_ref[rows, sl], k_ref[rows, sl], v_ref[rows, sl], gab_ref[rows, h:h + 1],
                          gab_ref[rows, H + h:H + h + 1], al_ref[h], dt_ref[h]))
            cots.append((dqe_ref[rows, sl], dke_ref[rows, sl], du_ref[rows, sl], dw_ref[rows, sl],
                         dat_ref[h, rows, :], deg_ref[cc, h:h + 1, :]))
        _, vjp = jax.vjp(_gdn_prep, units)
        (d_units,) = vjp(cots)
        dgabs = [jnp.zeros((C, gab_width), F32) for _ in range(chunks)]
        for (cc, h), (dq, dk, dv, dga, dgb, dal, ddt) in zip(where, d_units):
            rows = slice(cc * C, (cc + 1) * C)
            dqkv_ref[rows, h * Dh:(h + 1) * Dh] = dq
            dqkv_ref[rows, W + h * Dh:W + (h + 1) * Dh] = dk
            dqkv_ref[rows, 2 * W + h * Dh:2 * W + (h + 1) * Dh] = dv
            dgabs[cc] = dgabs[cc] + jnp.where(lane == h, dga, 0.0) + jnp.where(lane == H + h, dgb, 0.0)
            dal_sum[h] = dal if dal_sum[h] is None else dal_sum[h] + dal
            ddt_sum[h] = ddt if ddt_sum[h] is None else ddt_sum[h] + ddt
        for cc in range(chunks):
            dgab_ref[cc * C:(cc + 1) * C, :] = dgabs[cc].astype(BF16)

        @pl.when(first)
        def _():
            for h in range(H):
                dal_ref[h] = dal_sum[h]
                ddt_ref[h] = ddt_sum[h]

        @pl.when(jnp.logical_not(first))
        def _():
            for h in range(H):
                dal_ref[h] += dal_sum[h]
                ddt_ref[h] += ddt_sum[h]

    col = lambda c: pl.BlockSpec((R, W), lambda n, c=c: (n, c))
    tok = pl.BlockSpec((R, LANES), lambda n: (n, gab_off))
    par = pl.BlockSpec((H, 1, 1), lambda n: (0, 0, 0))
    wide = pl.BlockSpec((R, W), lambda n: (n, 0))
    att = pl.BlockSpec((H, R, C), lambda n: (0, n, 0))
    egs = pl.BlockSpec((chunks, H, Dh), lambda n: (n, 0, 0))
    return _pc(body, name=name, grid=(N // chunks,),
               in_specs=[col(0), col(1), col(2), tok, par, par, wide, wide, wide, wide, att, egs],
               out_specs=[pl.BlockSpec((R, 3 * W), lambda n: (n, 0)), pl.BlockSpec((R, gab_width), lambda n: (n, 0)),
                          par, par],
               out_shape=[jax.ShapeDtypeStruct((L, 3 * W), F32), jax.ShapeDtypeStruct((L, gab_width), BF16)]
               + [jax.ShapeDtypeStruct((H, 1, 1), F32)] * 2,
               dims=("arbitrary",))(qkv, qkv, qkv, gab, a_log, dt_bias, dqe, dke, du, dw, dattn, deg)


def _gdn_scan_fwd(qe, ke, u, w, attn, eg, gz, gz_col, wgn, *, name):
    L, W = qe.shape
    H, C = GDN_HEADS, GDN_CHUNK
    Dh = W // H
    N = L // C
    gz_off = gz_col // W

    def body(qe_ref, ke_ref, u_ref, w_ref, at_ref, eg_ref, gz_ref, wgn_ref, y_ref, st_ref, s_scr):
        @pl.when(pl.program_id(0) == 0)
        def _():
            s_scr[...] = jnp.zeros_like(s_scr)

        units = []
        for h in range(H):
            sl = slice(h * Dh, (h + 1) * Dh)
            st_ref[0, h] = s_scr[h]
            units.append((s_scr[h], qe_ref[:, sl], ke_ref[:, sl], u_ref[:, sl], w_ref[:, sl], at_ref[h],
                          eg_ref[0, h:h + 1, :], gz_ref[:, sl], wgn_ref[...]))
        for h, (y, S_new) in enumerate(_gdn_step(units)):
            y_ref[:, h * Dh:(h + 1) * Dh] = y.astype(BF16)
            s_scr[h] = S_new

    wide = pl.BlockSpec((C, W), lambda n: (n, 0))
    return _pc(body, name=name, grid=(N,),
               in_specs=[wide, wide, wide, wide, pl.BlockSpec((H, C, C), lambda n: (0, n, 0)),
                         pl.BlockSpec((1, H, Dh), lambda n: (n, 0, 0)),
                         pl.BlockSpec((C, W), lambda n: (n, gz_off)), pl.BlockSpec((1, Dh), lambda n: (0, 0))],
               out_specs=[wide, pl.BlockSpec((1, H, Dh, Dh), lambda n: (n, 0, 0, 0))],
               out_shape=[jax.ShapeDtypeStruct((L, W), BF16), jax.ShapeDtypeStruct((N, H, Dh, Dh), F32)],
               scratch_shapes=[pltpu.VMEM((H, Dh, Dh), F32)],
               dims=("arbitrary",))(qe, ke, u, w, attn, eg, gz, wgn.reshape(1, Dh))


def _gdn_scan_bwd(qe, ke, u, w, attn, eg, gz, gz_col, wgn, states, dy, dy_col, *, name):
    L, W = qe.shape
    H, C = GDN_HEADS, GDN_CHUNK
    Dh = W // H
    N = L // C
    gz_off = gz_col // W
    dy_off = dy_col // W

    def body(qe_ref, ke_ref, u_ref, w_ref, at_ref, eg_ref, gz_ref, wgn_ref, st_ref, dy_ref,
             dqe_ref, dke_ref, du_ref, dw_ref, dat_ref, deg_ref, dgz_ref, dwgn_ref, ds_scr):
        first = pl.program_id(0) == 0

        @pl.when(first)
        def _():
            ds_scr[...] = jnp.zeros_like(ds_scr)

        dwgn = None
        units, cots = [], []
        for h in range(H):
            sl = slice(h * Dh, (h + 1) * Dh)
            units.append((st_ref[0, h], qe_ref[:, sl], ke_ref[:, sl], u_ref[:, sl], w_ref[:, sl], at_ref[h],
                          eg_ref[0, h:h + 1, :], gz_ref[:, sl], wgn_ref[...]))
            cots.append((dy_ref[:, sl].astype(F32), ds_scr[h]))
        _, vjp = jax.vjp(_gdn_step, units)
        (d_units,) = vjp(cots)
        for h, (dS, dqe, dke, du, dw, dat, deg, dgz, dwg) in enumerate(d_units):
            sl = slice(h * Dh, (h + 1) * Dh)
            ds_scr[h] = dS
            dqe_ref[:, sl] = dqe
            dke_ref[:, sl] = dke
            du_ref[:, sl] = du
            dw_ref[:, sl] = dw
            dat_ref[h] = dat
            deg_ref[0, h:h + 1, :] = deg
            dgz_ref[:, sl] = dgz.astype(BF16)
            dwgn = dwg if dwgn is None else dwgn + dwg

        @pl.when(first)
        def _():
            dwgn_ref[...] = dwgn

        @pl.when(jnp.logical_not(first))
        def _():
            dwgn_ref[...] += dwgn

    rev = lambda n: N - 1 - n
    wide = pl.BlockSpec((C, W), lambda n: (rev(n), 0))
    att = pl.BlockSpec((H, C, C), lambda n: (0, rev(n), 0))
    egs = pl.BlockSpec((1, H, Dh), lambda n: (rev(n), 0, 0))
    vec = pl.BlockSpec((1, Dh), lambda n: (0, 0))
    return _pc(body, name=name, grid=(N,),
               in_specs=[wide, wide, wide, wide, att, egs, pl.BlockSpec((C, W), lambda n: (rev(n), gz_off)), vec,
                         pl.BlockSpec((1, H, Dh, Dh), lambda n: (rev(n), 0, 0, 0)),
                         pl.BlockSpec((C, W), lambda n: (rev(n), dy_off))],
               out_specs=[wide, wide, wide, wide, att, egs, wide, vec],
               out_shape=[jax.ShapeDtypeStruct((L, W), F32)] * 4 + [jax.ShapeDtypeStruct((H, L, C), F32),
                                                                   jax.ShapeDtypeStruct((N, H, Dh), F32),
                                                                   jax.ShapeDtypeStruct((L, W), BF16),
                                                                   jax.ShapeDtypeStruct((1, Dh), F32)],
               scratch_shapes=[pltpu.VMEM((H, Dh, Dh), F32)],
               dims=("arbitrary",))(qe, ke, u, w, attn, eg, gz, wgn.reshape(1, Dh), states, dy)


def _sb_scores(z, mask):
    sp = jnp.maximum(z, 0.0) + jnp.log(1.0 + jnp.exp(-jnp.abs(z)))
    lom = -sp if mask is None else jnp.where(mask, -sp, 0.0)
    return lom, z - sp


def _sb_masks(tq, width, dh):
    rr = lax.broadcasted_iota(jnp.int32, (tq, tq), 0)
    cc = lax.broadcasted_iota(jnp.int32, (tq, tq), 1)
    first_head = lax.broadcasted_iota(jnp.int32, (tq, width), 1) < dh
    return cc < rr, jnp.where(rr > cc, 1.0, 0.0).astype(BF16), first_head


def _sb_fwd(qkv, *, name, tq=256):
    L = qkv.shape[0]
    H = SB_HEADS
    width = 2 * (qkv.shape[1] // 3 // H)
    dh = width // 2
    npair = H // 2
    tq = min(tq, L)
    nq = L // tq

    def body(q_ref, k_ref, v_ref, o_ref, c_ref):
        i = pl.program_id(1)
        diag, tri, first_head = _sb_masks(tq, width, dh)
        qp = q_ref[...]
        zero = jnp.zeros_like(qp)
        qs = (jnp.where(first_head, qp, zero), jnp.where(first_head, zero, qp))

        def blocks(js, carry, mask):
            units = [(b, hd) for b in range(len(js)) for hd in range(2)]
            starts = [pl.multiple_of(j * tq, tq) for j in js]
            ks = [k_ref[pl.ds(st, tq), :] for st in starts]
            vs = [v_ref[pl.ds(st, tq), :] for st in starts]
            zs = {(b, hd): lax.dot_general(qs[hd], ks[b], _DIMS["nt"], preferred_element_type=F32)
                  for b, hd in units}
            scores = {un: _sb_scores(zs[un], mask) for un in units}
            later = {un: jnp.dot(scores[un][0].astype(BF16), tri, preferred_element_type=F32) for un in units}
            cs = [carry[hd][0] for hd in range(2)]
            accs = [carry[hd][1] for hd in range(2)]
            for b, hd in units:
                lom, lb = scores[(b, hd)]
                a = jnp.exp(lb + (cs[hd] + later[(b, hd)]))
                if mask is not None:
                    a = jnp.where(mask, a, 0.0)
                accs[hd] = accs[hd] + jnp.dot(a.astype(BF16), vs[b], preferred_element_type=F32)
                cs[hd] = cs[hd] + jnp.sum(lom, axis=1, keepdims=True)
            return tuple((cs[hd], accs[hd]) for hd in range(2))

        init = tuple((jnp.zeros((tq, 1), F32), jnp.zeros((tq, width), F32)) for _ in range(2))
        carry = blocks([i], init, diag)
        carry = lax.fori_loop(0, i % SB_UNROLL, lambda t, cr: blocks([i - 1 - t], cr, None), carry)
        left = i - i % SB_UNROLL
        carry = lax.fori_loop(0, left // SB_UNROLL,
                              lambda t, cr: blocks([left - 1 - SB_UNROLL * t - b for b in range(SB_UNROLL)], cr, None),
                              carry)
        o_ref[...] = jnp.where(first_head, carry[0][1], carry[1][1]).astype(BF16)
        c_ref[0] = carry[0][0]
        c_ref[1] = carry[1][0]

    return _pc(body, name=name, grid=(npair, nq),
               in_specs=[pl.BlockSpec((tq, width), lambda p, i: (i, p)),
                         pl.BlockSpec((L, width), lambda p, i: (0, npair + p)),
                         pl.BlockSpec((L, width), lambda p, i: (0, 2 * npair + p))],
               out_specs=[pl.BlockSpec((tq, width), lambda p, i: (i, p)),
                          pl.BlockSpec((2, tq, 1), lambda p, i: (p, i, 0))],
               out_shape=[jax.ShapeDtypeStruct((L, npair * width), BF16), jax.ShapeDtypeStruct((H, L, 1), F32)],
               dims=("parallel", "parallel"))(qkv, qkv, qkv)


def _sb_bwd(qkv, ctot, do, do_col, scale, *, name, tq=256):
    L = qkv.shape[0]
    H = SB_HEADS
    width = 2 * (qkv.shape[1] // 3 // H)
    dh = width // 2
    npair = H // 2
    tq = min(tq, L)
    nq = L // tq
    do_off = do_col // width

    def body(q_ref, k_ref, v_ref, c_ref, do_ref, dq_ref, dk_ref, dv_ref):
        i = pl.program_id(1)

        @pl.when(i == 0)
        def _():
            dk_ref[...] = jnp.zeros_like(dk_ref)
            dv_ref[...] = jnp.zeros_like(dv_ref)

        diag, tri_later, first_head = _sb_masks(tq, width, dh)
        rr = lax.broadcasted_iota(jnp.int32, (tq, tq), 0)
        cc = lax.broadcasted_iota(jnp.int32, (tq, tq), 1)
        tri_before = jnp.where(rr < cc, 1.0, 0.0).astype(BF16)
        qp = q_ref[...]
        dop = do_ref[...].astype(BF16)
        zero = jnp.zeros_like(qp)
        qs = (jnp.where(first_head, qp, zero), jnp.where(first_head, zero, qp))
        dos = (jnp.where(first_head, dop, zero), jnp.where(first_head, zero, dop))
        ctots = (c_ref[0], c_ref[1])

        def blocks(js, carry, mask):
            nb = len(js)
            units = [(b, hd) for b in range(nb) for hd in range(2)]
            starts = [pl.multiple_of(j * tq, tq) for j in js]
            ks = [k_ref[pl.ds(st, tq), :] for st in starts]
            vs = [v_ref[pl.ds(st, tq), :] for st in starts]
            zs = {(b, hd): lax.dot_general(qs[hd], ks[b], _DIMS["nt"], preferred_element_type=F32)
                  for b, hd in units}
            das = {(b, hd): lax.dot_general(dos[hd], vs[b], _DIMS["nt"], preferred_element_type=F32)
                   for b, hd in units}
            scores = {un: _sb_scores(zs[un], mask) for un in units}
            later = {un: jnp.dot(scores[un][0].astype(BF16), tri_later, preferred_element_type=F32) for un in units}
            pcs = [carry[hd][0] for hd in range(2)]
            avals = {}
            for b, hd in units:
                pcs[hd] = pcs[hd] + jnp.sum(scores[(b, hd)][0], axis=1, keepdims=True)
                a = jnp.exp(scores[(b, hd)][1] + ((ctots[hd] - pcs[hd]) + later[(b, hd)]))
                avals[(b, hd)] = a if mask is None else jnp.where(mask, a, 0.0)
            gs = {un: das[un] * avals[un] for un in units}
            before = {un: jnp.dot(gs[un].astype(BF16), tri_before, preferred_element_type=F32) for un in units}
            pgs = [carry[hd][1] for hd in range(2)]
            dzs = {}
            for b, hd in units:
                sig = jnp.exp(scores[(b, hd)][1])
                dz = gs[(b, hd)] * (1.0 - sig) - (pgs[hd] + before[(b, hd)]) * sig
                dzs[(b, hd)] = (dz if mask is None else jnp.where(mask, dz, 0.0)).astype(BF16)
                pgs[hd] = pgs[hd] + jnp.sum(gs[(b, hd)], axis=1, keepdims=True)
            dqs = [carry[hd][2] for hd in range(2)]
            for b, hd in units:
                dqs[hd] = dqs[hd] + jnp.dot(dzs[(b, hd)], ks[b], preferred_element_type=F32)
            for b in range(nb):
                dk_ref[pl.ds(starts[b], tq), :] += sum(
                    lax.dot_general(dzs[(b, hd)], qs[hd], _DIMS["tn"], preferred_element_type=F32) for hd in range(2))
                dv_ref[pl.ds(starts[b], tq), :] += sum(
                    lax.dot_general(avals[(b, hd)].astype(BF16), dos[hd], _DIMS["tn"], preferred_element_type=F32)
                    for hd in range(2))
            return tuple((pcs[hd], pgs[hd], dqs[hd]) for hd in range(2))

        col = jnp.zeros((tq, 1), F32)
        init = tuple((col, col, jnp.zeros((tq, width), F32)) for _ in range(2))
        carry = lax.fori_loop(0, i // SB_UNROLL,
                              lambda t, cr: blocks([SB_UNROLL * t + b for b in range(SB_UNROLL)], cr, None), init)
        carry = lax.fori_loop(0, i % SB_UNROLL, lambda t, cr: blocks([i - i % SB_UNROLL + t], cr, None), carry)
        carry = blocks([i], carry, diag)
        dq_ref[...] = (jnp.where(first_head, carry[0][2], carry[1][2]) * scale).astype(BF16)

    tile = pl.BlockSpec((tq, width), lambda p, i: (i, p))
    full = pl.BlockSpec((L, width), lambda p, i: (0, p))
    sds = jax.ShapeDtypeStruct((L, npair * width), F32)
    return _pc(body, name=name, grid=(npair, nq),
               in_specs=[tile, pl.BlockSpec((L, width), lambda p, i: (0, npair + p)),
                         pl.BlockSpec((L, width), lambda p, i: (0, 2 * npair + p)),
                         pl.BlockSpec((2, tq, 1), lambda p, i: (p, i, 0)),
                         pl.BlockSpec((tq, width), lambda p, i: (i, do_off + p))],
               out_specs=[tile, full, full],
               out_shape=[jax.ShapeDtypeStruct((L, npair * width), BF16), sds, sds],
               dims=("parallel", "arbitrary"))(qkv, qkv, qkv, ctot, do)


def _adamw(w, g, m, v, *, name, tm=256):
    R, C = w.shape
    tm = min(tm, R)
    assert R % tm == 0, (R, tm)
    c1 = 1.0 - ADAM_B1 ** ADAM_STEP
    c2 = 1.0 - ADAM_B2 ** ADAM_STEP

    def body(w_ref, g_ref, m_ref, v_ref, d_ref, nm_ref, nv_ref):
        gv = g_ref[...]
        nm = ADAM_B1 * m_ref[...] + (1.0 - ADAM_B1) * gv
        nv = ADAM_B2 * v_ref[...] + (1.0 - ADAM_B2) * (gv * gv)
        d_ref[...] = -ADAM_LR * ((nm / c1) / (jnp.sqrt(nv / c2) + ADAM_EPS) + ADAM_WD * w_ref[...])
        nm_ref[...] = nm
        nv_ref[...] = nv

    blk = pl.BlockSpec((tm, C), lambda i: (i, 0))
    sds = jax.ShapeDtypeStruct((R, C), F32)
    return _pc(body, name=name, grid=(R // tm,), in_specs=[blk] * 4, out_specs=[blk] * 3, out_shape=[sds] * 3,
               dims=("parallel",))(w, g, m, v)


ELEMENTWISE_BLOCK_BYTES = 1 << 20


def _row_tile(rows, cols):
    for t in (512, 384, 352, 256, 176, 128, 88, 64, 32, 16, 8):
        if rows % t == 0 and t * cols * 4 <= ELEMENTWISE_BLOCK_BYTES:
            return t
    raise ValueError((rows, cols))


def _adamw_layers(w, g_mine, g_other, m, v, c, *, name):
    _, R, C = w.shape
    tm = _row_tile(R, C)
    c1 = 1.0 - ADAM_B1 ** ADAM_STEP
    c2 = 1.0 - ADAM_B2 ** ADAM_STEP

    def body(c_ref, w_ref, gm_ref, go_ref, m_ref, v_ref, g_ref, d_ref, nm_ref, nv_ref):
        gv = jnp.where(pl.program_id(0) == c_ref[0], gm_ref[...], go_ref[...])
        nm = ADAM_B1 * m_ref[...] + (1.0 - ADAM_B1) * gv
        nv = ADAM_B2 * v_ref[...] + (1.0 - ADAM_B2) * (gv * gv)
        g_ref[...] = gv
        d_ref[...] = -ADAM_LR * ((nm / c1) / (jnp.sqrt(nv / c2) + ADAM_EPS) + ADAM_WD * w_ref[...])
        nm_ref[...] = nm
        nv_ref[...] = nv

    slab = pl.BlockSpec((None, tm, C), lambda l, i, c_ref: (l, i, 0))
    mine = pl.BlockSpec((tm, C), lambda l, i, c_ref: (jnp.where(l == c_ref[0], i, 0), 0))
    other = pl.BlockSpec((tm, C), lambda l, i, c_ref: (jnp.where(l == c_ref[0], 0, i), 0))
    grid_spec = pltpu.PrefetchScalarGridSpec(num_scalar_prefetch=1, grid=(2, R // tm),
                                             in_specs=[slab, mine, other, slab, slab], out_specs=[slab] * 4)
    return _pc_prefetch(body, name=name, grid_spec=grid_spec, out_shape=[jax.ShapeDtypeStruct(w.shape, F32)] * 4,
                        dims=("parallel", "parallel"))(c.reshape(1).astype(jnp.int32), w, g_mine, g_other, m, v)


def _add_layers(g0, g1, ra, c, *, name):
    S, R, C = ra.shape
    tm = _row_tile(R, C)

    def body(c_ref, g0_ref, g1_ref, r_ref, o_ref):
        mine = jnp.where(c_ref[0] == 0, g0_ref[...], g1_ref[...])
        o_ref[...] = (mine + r_ref[...]).astype(BF16)

    def walked_if(layer):
        return lambda s, i, c_ref: (jnp.where(c_ref[0] == layer, s, 0), jnp.where(c_ref[0] == layer, i, 0), 0)

    blk = lambda s, i, c_ref: (s, i, 0)
    grid_spec = pltpu.PrefetchScalarGridSpec(
        num_scalar_prefetch=1, grid=(S, R // tm),
        in_specs=[pl.BlockSpec((None, tm, C), walked_if(0)), pl.BlockSpec((None, tm, C), walked_if(1)),
                  pl.BlockSpec((None, tm, C), blk)],
        out_specs=pl.BlockSpec((None, tm, C), blk))
    return _pc_prefetch(body, name=name, grid_spec=grid_spec, out_shape=jax.ShapeDtypeStruct((S, R, C), BF16),
                        dims=("parallel", "parallel"))(c.reshape(1).astype(jnp.int32), g0, g1, ra)


def _add_chips(p, rb, chip, *, name):
    S, Rh, C = p.shape
    tm = _row_tile(Rh, C)

    def body(s_ref, p_ref, r_ref, o_ref):
        o_ref[...] = ((p_ref[...].astype(F32) + r_ref[0].astype(F32)) + r_ref[1].astype(F32)) + r_ref[2].astype(F32)

    grid_spec = pltpu.PrefetchScalarGridSpec(
        num_scalar_prefetch=1, grid=(Rh // tm,),
        in_specs=[pl.BlockSpec((None, tm, C), lambda i, s_ref: (s_ref[0], i, 0)),
                  pl.BlockSpec((3, tm, C), lambda i, s_ref: (0, i, 0))],
        out_specs=pl.BlockSpec((tm, C), lambda i, s_ref: (i, 0)))
    return _pc_prefetch(body, name=name, grid_spec=grid_spec, out_shape=jax.ShapeDtypeStruct((Rh, C), F32),
                        dims=("parallel",))(chip.reshape(1).astype(jnp.int32), p, rb)


def _sum_slots(g, *, name):
    n, R, C = g.shape

    def body(g_ref, o_ref):
        acc = g_ref[0]
        for s in range(1, n):
            acc = acc + g_ref[s]
        o_ref[...] = acc

    return _pc(body, name=name, grid=(1,), in_specs=[pl.BlockSpec((n, R, C), lambda i: (0, 0, 0))],
               out_specs=pl.BlockSpec((R, C), lambda i: (0, 0)), out_shape=jax.ShapeDtypeStruct((R, C), F32),
               dims=("arbitrary",))(g)


ANY = pl.BlockSpec(memory_space=pl.ANY)


def _place():
    return lax.axis_index("x"), lax.axis_index("y"), lax.axis_index("c")


def _other_chips(x, y):
    return [(1 - x, y), (x, 1 - y), (1 - x, 1 - y)]


def _allgather_chips(ws, *, name):
    n = len(ws)

    def body(*refs):
        w_refs, out_refs, send_sems, recv_sems = refs[:n], refs[n:2 * n], refs[2 * n], refs[2 * n + 1]
        x, y, c = _place()
        sib = (x, y, 1 - c)
        chips = _other_chips(x, y)

        def copy(a, k, chip_id, layer, to):
            src = w_refs[a].at[layer] if k < 3 else out_refs[a].at[chip_id, layer]
            return pltpu.make_async_remote_copy(src_ref=src, dst_ref=out_refs[a].at[chip_id, layer],
                                                send_sem=send_sems.at[k * n + a], recv_sem=recv_sems.at[k * n + a],
                                                device_id=to, device_id_type=MESH)

        sends = [copy(a, j, 2 * x + y, c, (px, py, c)) for j, (px, py) in enumerate(chips) for a in range(n)]
        for cp in sends:
            cp.start()
        passed = []
        for j, (px, py) in enumerate(chips):
            for a in range(n):
                copy(a, j, 2 * px + py, c, (px, py, c)).wait_recv()
                fwd = copy(a, 3 + j, 2 * px + py, c, sib)
                fwd.start()
                passed.append(fwd)
        for j, (px, py) in enumerate(chips):
            for a in range(n):
                copy(a, 3 + j, 2 * px + py, 1 - c, sib).wait_recv()
        for cp in sends + passed:
            cp.wait_send()

    return _pc_comm(body, name=name, in_specs=[ANY] * n, out_specs=[ANY] * n,
                    out_shape=[jax.ShapeDtypeStruct((N_CHIPS,) + w.shape, w.dtype) for w in ws],
                    scratch_shapes=[pltpu.SemaphoreType.DMA((6 * n,)), pltpu.SemaphoreType.DMA((6 * n,))])(*ws)


def _send_other_layer_to_sibling(g0s, g1s, *, name):
    n = len(g0s)

    def body(*refs):
        g_refs = (refs[:n], refs[n:2 * n])
        out_refs, send_sems, recv_sems = refs[2 * n:3 * n], refs[3 * n], refs[3 * n + 1]
        x, y, c = _place()

        def copy(a, layer):
            return pltpu.make_async_remote_copy(src_ref=g_refs[layer][a], dst_ref=out_refs[a], send_sem=send_sems.at[a],
                                                recv_sem=recv_sems.at[a], device_id=(x, y, 1 - c), device_id_type=MESH)

        for layer in range(2):
            @pl.when(c == 1 - layer)
            def _(layer=layer):
                for a in range(n):
                    copy(a, layer).start()
        for a in range(n):
            copy(a, 0).wait()

    return _pc_comm(body, name=name, in_specs=[ANY] * (2 * n), out_specs=[ANY] * n,
                    out_shape=[jax.ShapeDtypeStruct(g.shape, g.dtype) for g in g0s],
                    scratch_shapes=[pltpu.SemaphoreType.DMA((n,)), pltpu.SemaphoreType.DMA((n,))])(*g0s, *g1s)


def _scatter_to_chips(ps, *, name):
    n = len(ps)

    def body(*refs):
        p_refs, rb_refs, send_sems, recv_sems = refs[:n], refs[n:2 * n], refs[2 * n], refs[2 * n + 1]
        x, y, c = _place()
        chips = _other_chips(x, y)
        sends = [pltpu.make_async_remote_copy(src_ref=p_refs[a].at[2 * px + py], dst_ref=rb_refs[a].at[j],
                                              send_sem=send_sems.at[j * n + a], recv_sem=recv_sems.at[j * n + a],
                                              device_id=(px, py, c), device_id_type=MESH)
                 for j, (px, py) in enumerate(chips) for a in range(n)]
        for cp in sends:
            cp.start()
        for cp in sends:
            cp.wait()

    return _pc_comm(body, name=name, in_specs=[ANY] * n, out_specs=[ANY] * n,
                    out_shape=[jax.ShapeDtypeStruct((3,) + p.shape[1:], p.dtype) for p in ps],
                    scratch_shapes=[pltpu.SemaphoreType.DMA((3 * n,)), pltpu.SemaphoreType.DMA((3 * n,))])(*ps)


def _swap_with_sibling(fs, *, name):
    n = len(fs)

    def body(*refs):
        f_refs, out_refs, send_sems, recv_sems = refs[:n], refs[n:2 * n], refs[2 * n], refs[2 * n + 1]
        x, y, c = _place()
        copies = [pltpu.make_async_remote_copy(src_ref=f_refs[a], dst_ref=out_refs[a], send_sem=send_sems.at[a],
                                               recv_sem=recv_sems.at[a], device_id=(x, y, 1 - c), device_id_type=MESH)
                  for a in range(n)]
        for cp in copies:
            cp.start()
        for cp in copies:
            cp.wait()

    return _pc_comm(body, name=name, in_specs=[ANY] * n, out_specs=[ANY] * n,
                    out_shape=[jax.ShapeDtypeStruct(f.shape, f.dtype) for f in fs],
                    scratch_shapes=[pltpu.SemaphoreType.DMA((n,)), pltpu.SemaphoreType.DMA((n,))])(*fs)


def _allgather_devices(v, *, name):
    R, C = v.shape

    def body(v_ref, out_ref, send_sems, recv_sems):
        x, y, c = _place()
        me = 4 * x + 2 * y + c
        out_ref[me] = v_ref[...]
        peers = []
        for k in range(1, 8):
            fx, fy, fc = (k >> 2) & 1, (k >> 1) & 1, k & 1
            px = 1 - x if fx else x
            py = 1 - y if fy else y
            pcc = 1 - c if fc else c
            peers.append((px, py, pcc))
        sends = []
        for k, peer in enumerate(peers):
            cp = pltpu.make_async_remote_copy(src_ref=v_ref, dst_ref=out_ref.at[me], send_sem=send_sems.at[k],
                                              recv_sem=recv_sems.at[k], device_id=peer, device_id_type=MESH)
            cp.start()
            sends.append(cp)
        for k, (px, py, pcc) in enumerate(peers):
            pltpu.make_async_remote_copy(src_ref=v_ref, dst_ref=out_ref.at[4 * px + 2 * py + pcc],
                                         send_sem=send_sems.at[k], recv_sem=recv_sems.at[k], device_id=peers[k],
                                         device_id_type=MESH).wait_recv()
        for cp in sends:
            cp.wait_send()

    vm = pl.BlockSpec(memory_space=pltpu.VMEM)
    return _pc_comm(body, name=name, in_specs=[vm], out_specs=vm, out_shape=jax.ShapeDtypeStruct((8, R, C), F32),
                    scratch_shapes=[pltpu.SemaphoreType.DMA((7,)), pltpu.SemaphoreType.DMA((7,))])(v)


D_MODEL = 1024
SC_W = D_MODEL // 4
GDN_W = D_MODEL // 2
SB_W = D_MODEL - SC_W - GDN_W
D_FF = 256 * ((8 * D_MODEL // 3 + 255) // 256)
O_SC, O_GQKV, O_GZ, O_GA, O_GB, O_SB = 0, 3 * SC_W, 3 * SC_W + 3 * GDN_W, 3 * SC_W + 4 * GDN_W, \
    3 * SC_W + 4 * GDN_W + GDN_HEADS, 3 * SC_W + 4 * GDN_W + 2 * GDN_HEADS
D_IN_PROJ = O_SB + 3 * SB_W
P_GQKV, P_SC, P_SB = 0, 3 * GDN_W, 3 * GDN_W + 3 * SC_W
P_GZ = P_SB + 3 * SB_W
P_GAB = P_GZ + GDN_W
P_PAD = 256
P_WIDTH = P_GAB + P_PAD


def _proj_to_kernel_layout(w):
    pad = jnp.zeros((w.shape[0], P_PAD - 2 * GDN_HEADS), w.dtype)
    return jnp.concatenate([w[:, O_GQKV:O_GZ], w[:, O_SC:O_GQKV], w[:, O_SB:], w[:, O_GZ:O_GA], w[:, O_GA:O_SB], pad],
                           axis=1)


def _proj_from_kernel_layout(g):
    return jnp.concatenate([g[:, P_SC:P_SB], g[:, P_GQKV:P_SC], g[:, P_GZ:P_GAB], g[:, P_GAB:P_GAB + 2 * GDN_HEADS],
                            g[:, P_SB:P_GZ]], axis=1)


def _mixout_to_kernel_layout(w):
    return jnp.concatenate([w[SC_W:SC_W + GDN_W], w[:SC_W], w[SC_W + GDN_W:]], axis=0)


def _pack_vec(parts, rows_to):
    flat = jnp.concatenate([p.reshape(-1) for p in parts])
    return jnp.pad(flat, (0, rows_to * LANES - flat.shape[0])).reshape(rows_to, LANES)


def _unpack_vec(mat, shapes):
    flat = mat.reshape(-1)
    out, r = [], 0
    for shp in shapes:
        n = int(np.prod(shp))
        out.append(flat[r:r + n].reshape(shp))
        r += n
    return out


def _round_up(n, m):
    return (n + m - 1) // m * m


def _layer_fwd(x, p, l):
    L = x.shape[0]
    tag = "l%d_" % l
    h = _rmsnorm_fwd(x, p["wn_mix"], name=tag + "norm_mix")
    proj = _matmul(h, p["w_in"], "nn", tm=512, tn=768, tk=D_MODEL, name=tag + "proj")
    (y_sc,) = _conv_pointwise_fwd([(proj, P_SC + SC_W), (proj, P_SC + 2 * SC_W)], [(p["w_sconv"], 0)], [(proj, P_SC)],
                                  _pre_product, _post_gate_mul, [(SC_W, BF16)], tc=SC_W, tm=512, name=tag + "sconv")
    (qkv,) = _conv_pointwise_fwd([(proj, P_GQKV)], [(p["w_gdn_conv"], 0)], [], _pre_identity, _post_silu,
                                 [(3 * GDN_W, F32)], tc=GDN_W, tm=512, name=tag + "gdn_conv")
    qe, ke, u, w, attn, eg = _gdn_prep_fwd(qkv, proj, P_GAB, p["a_log"], p["dt_bias"], name=tag + "gdn_prep")
    y_gdn, states = _gdn_scan_fwd(qe, ke, u, w, attn, eg, proj, P_GZ, p["wgn"], name=tag + "gdn_scan")
    sb_scale = (SB_W // SB_HEADS) ** -0.5
    sbqkv = jnp.concatenate([proj[:, P_SB:P_SB + SB_W] * sb_scale, proj[:, P_SB + SB_W:P_SB + 3 * SB_W]],
                            axis=1).astype(BF16)
    y_sb, ctot = _sb_fwd(sbqkv, name=tag + "sb_fwd")
    y_cat = [y_gdn, y_sc, y_sb]
    x2 = _matmul_rows_parts(y_cat, p["w_out"], "nn", res=x, name=tag + "mix_out")
    h2 = _rmsnorm_fwd(x2, p["wn_ffn"], name=tag + "norm_ffn")
    up_g = _matmul(h2, p["w_up_g"], "nn", tm=512, tn=D_FF // 2, tk=D_MODEL, name=tag + "up_gate")
    up_v = _matmul(h2, p["w_up_v"], "nn", tm=512, tn=D_FF // 2, tk=D_MODEL, name=tag + "up_val")
    (act,) = _conv_pointwise_fwd([(up_g, 0), (up_v, 0)], [(p["w_fconv_g"], 0), (p["w_fconv_v"], 0)], [],
                                 _pre_identity, _post_swiglu, [(D_FF, BF16)], tc=256, tm=512, name=tag + "ffn_act")
    x3 = _matmul(act, p["w_down"], "nn", tm=512, tn=D_MODEL, tk=D_FF // 2, res=x2, name=tag + "ffn_down")
    saved = dict(x=x, h=h, proj=proj, qkv=qkv, qe=qe, ke=ke, u=u, w=w, attn=attn, eg=eg, states=states,
                 sbqkv=sbqkv, ctot=ctot, y_cat=y_cat, x2=x2, h2=h2, up_g=up_g, up_v=up_v, act=act)
    return x3, saved


def _layer_bwd(dx3, p, s, l):
    L = dx3.shape[0]
    tag = "l%d_b_" % l
    g = {}
    dact = _matmul(dx3, p["w_down"], "nt", tm=512, tn=D_FF // 2, tk=D_MODEL, name=tag + "dact")
    g["w_down"] = _matmul(s["act"], dx3, "tn", tm=D_FF // 2, tn=D_MODEL, tk=512, name=tag + "dw_down")
    (dup_g, dup_v), _, (g["w_fconv_g"], g["w_fconv_v"]) = _conv_pointwise_bwd(
        [(s["up_g"], 0), (s["up_v"], 0)], [(p["w_fconv_g"], 0), (p["w_fconv_v"], 0)], [], [(dact, 0)],
        _pre_identity, _post_swiglu, D_FF, tc=256, tm=512, name=tag + "ffn_act")
    dh2 = _matmul(dup_g, p["w_up_g"], "nt", tm=512, tn=D_MODEL, tk=D_FF // 2, name=tag + "dh2_gate")
    dh2 = _matmul(dup_v, p["w_up_v"], "nt", tm=512, tn=D_MODEL, tk=D_FF // 2, res=dh2, name=tag + "dh2_val")
    g["w_up"] = _matmul(s["h2"], dup_g, "tn", tm=D_MODEL, tn=D_FF // 2, tk=512, slabs=(N_CHIPS, 0, None),
                        name=tag + "dw_up_gate")
    g["w_up"] = _matmul(s["h2"], dup_v, "tn", tm=D_MODEL, tn=D_FF // 2, tk=512, slabs=(N_CHIPS, 2, g["w_up"]),
                        name=tag + "dw_up_val")
    dx2, g["wn_ffn"] = _rmsnorm_bwd(dh2, s["x2"], p["wn_ffn"], dx3, name=tag + "norm_ffn")
    dycat = _matmul(dx2, p["w_out"], "nt", tm=512, tn=D_MODEL, tk=D_MODEL, name=tag + "dycat")
    y_gdn, y_sc, y_sb = s["y_cat"]
    g["w_out"] = _matmul_tn_parts([y_sc, y_gdn, y_sb], dx2, name=tag + "dw_out")
    sb_scale = (SB_W // SB_HEADS) ** -0.5
    dsq, dsk, dsv = _sb_bwd(s["sbqkv"], s["ctot"], dycat, GDN_W + SC_W, sb_scale, name=tag + "sb_bwd")
    dqe, dke, du, dw, dattn, deg, dgz, g["wgn"] = _gdn_scan_bwd(
        s["qe"], s["ke"], s["u"], s["w"], s["attn"], s["eg"], s["proj"], P_GZ, p["wgn"], s["states"], dycat, 0,
        name=tag + "gdn_scan")
    dqkv_act, dgab, g["a_log"], g["dt_bias"] = _gdn_prep_bwd(
        s["qkv"], s["proj"], P_GAB, p["a_log"], p["dt_bias"], dqe, dke, du, dw, dattn, deg, P_PAD,
        name=tag + "gdn_prep")
    (dqkv,), _, (g["w_gdn_conv"],) = _conv_pointwise_bwd(
        [(s["proj"], P_GQKV)], [(p["w_gdn_conv"], 0)], [], [(dqkv_act, 0)], _pre_identity, _post_silu, 3 * GDN_W,
        tc=GDN_W, tm=512, name=tag + "gdn_conv")
    (dsc_c, dsc_h), (dsc_b,), (g["w_sconv"],) = _conv_pointwise_bwd(
        [(s["proj"], P_SC + SC_W), (s["proj"], P_SC + 2 * SC_W)], [(p["w_sconv"], 0)], [(s["proj"], P_SC)],
        [(dycat, GDN_W)], _pre_product, _post_gate_mul, SC_W, tc=SC_W, tm=512, name=tag + "sconv")
    dproj = [dqkv, dsc_b, dsc_c, dsc_h, dsq, dsk, dsv, dgz, dgab]
    dh = _matmul_rows_parts(dproj, p["w_in"], "nt", name=tag + "dh")
    g["w_in"] = jnp.concatenate([_matmul_tn_parts(s["h"], dproj[:4], name=tag + "dw_in_a"),
                                 _matmul_tn_parts(s["h"], dproj[4:], name=tag + "dw_in_b")], axis=1)
    dx, g["wn_mix"] = _rmsnorm_bwd(dh, s["x"], p["wn_mix"], dx2, name=tag + "norm_mix")
    return dx, g


BIG = ("w_mix_in", "w_mix_out", "w_ffn_up", "w_ffn_down")
BIG_AXIS = {"w_mix_in": 2, "w_mix_out": 1, "w_ffn_up": 2, "w_ffn_down": 1}
SMALL_SHARDED = ("w_sconv", "w_gdn_conv", "w_ffn_conv")
SMALL_REPLICATED = ("w_norm_mix", "gdn_a_log", "gdn_dt_bias", "w_gdn_norm", "w_norm_ffn", "w_norm_final")
WEIGHTS = ("w_norm_mix", "w_mix_in", "w_sconv", "w_gdn_conv", "gdn_a_log", "gdn_dt_bias", "w_gdn_norm", "w_mix_out",
           "w_norm_ffn", "w_ffn_up", "w_ffn_conv", "w_ffn_down", "w_norm_final")


def kernel(x, w_norm_mix, w_mix_in, w_sconv, w_gdn_conv, gdn_a_log, gdn_dt_bias, w_gdn_norm, w_mix_out, w_norm_ffn, w_ffn_up, w_ffn_conv, w_ffn_down, w_norm_final, loss_target, m_w_norm_mix, m_w_mix_in, m_w_sconv, m_w_gdn_conv, m_gdn_a_log, m_gdn_dt_bias, m_w_gdn_norm, m_w_mix_out, m_w_norm_ffn, m_w_ffn_up, m_w_ffn_conv, m_w_ffn_down, m_w_norm_final, v_w_norm_mix, v_w_mix_in, v_w_sconv, v_w_gdn_conv, v_gdn_a_log, v_gdn_dt_bias, v_w_gdn_norm, v_w_mix_out, v_w_norm_ffn, v_w_ffn_up, v_w_ffn_conv, v_w_ffn_down, v_w_norm_final):
    W = dict(w_norm_mix=w_norm_mix, w_mix_in=w_mix_in, w_sconv=w_sconv, w_gdn_conv=w_gdn_conv, gdn_a_log=gdn_a_log,
             gdn_dt_bias=gdn_dt_bias, w_gdn_norm=w_gdn_norm, w_mix_out=w_mix_out, w_norm_ffn=w_norm_ffn,
             w_ffn_up=w_ffn_up, w_ffn_conv=w_ffn_conv, w_ffn_down=w_ffn_down, w_norm_final=w_norm_final)
    M = dict(w_norm_mix=m_w_norm_mix, w_mix_in=m_w_mix_in, w_sconv=m_w_sconv, w_gdn_conv=m_w_gdn_conv,
             gdn_a_log=m_gdn_a_log, gdn_dt_bias=m_gdn_dt_bias, w_gdn_norm=m_w_gdn_norm, w_mix_out=m_w_mix_out,
             w_norm_ffn=m_w_norm_ffn, w_ffn_up=m_w_ffn_up, w_ffn_conv=m_w_ffn_conv, w_ffn_down=m_w_ffn_down,
             w_norm_final=m_w_norm_final)
    V = dict(w_norm_mix=v_w_norm_mix, w_mix_in=v_w_mix_in, w_sconv=v_w_sconv, w_gdn_conv=v_w_gdn_conv,
             gdn_a_log=v_gdn_a_log, gdn_dt_bias=v_gdn_dt_bias, w_gdn_norm=v_w_gdn_norm, w_mix_out=v_w_mix_out,
             w_norm_ffn=v_w_norm_ffn, w_ffn_up=v_w_ffn_up, w_ffn_conv=v_w_ffn_conv, w_ffn_down=v_w_ffn_down,
             w_norm_final=v_w_norm_final)
    depth = w_mix_in.shape[0]
    L = x.shape[1]
    mx, my, mc = lax.axis_index("x"), lax.axis_index("y"), lax.axis_index("c")
    chip = 2 * mx + my

    assert depth == 2
    own = [W[n].astype(BF16) for n in BIG]
    gathered = _allgather_chips(own, name="gather_big")
    gathered = [lax.dynamic_update_slice(g, o[None], (chip, 0, 0, 0)) for g, o in zip(gathered, own)]
    full_big = [{n: jnp.concatenate([g[b, l] for b in range(N_CHIPS)], axis=BIG_AXIS[n] - 1)
                 for n, g in zip(BIG, gathered)} for l in range(depth)]

    small_sh_shapes = [W[n].shape for n in SMALL_SHARDED]
    n_small_sh = sum(int(np.prod(s)) for s in small_sh_shapes)
    small_rows = _round_up(n_small_sh, 8 * LANES) // LANES
    small_all = _allgather_devices(_pack_vec([W[n] for n in SMALL_SHARDED], small_rows), name="gather_small")
    small_chip = [_unpack_vec(small_all[2 * b], small_sh_shapes) for b in range(N_CHIPS)]
    full_small = {n: jnp.concatenate([small_chip[b][i] for b in range(N_CHIPS)], axis=2)
                  for i, n in enumerate(SMALL_SHARDED)}

    params = []
    for l in range(depth):
        w_up = full_big[l]["w_ffn_up"]
        fconv = full_small["w_ffn_conv"][l]
        params.append(dict(
            wn_mix=w_norm_mix[l], w_in=_proj_to_kernel_layout(full_big[l]["w_mix_in"]),
            w_sconv=full_small["w_sconv"][l], w_gdn_conv=full_small["w_gdn_conv"][l],
            a_log=gdn_a_log[l].reshape(GDN_HEADS, 1, 1), dt_bias=gdn_dt_bias[l].reshape(GDN_HEADS, 1, 1),
            wgn=w_gdn_norm[l], w_out=_mixout_to_kernel_layout(full_big[l]["w_mix_out"]), wn_ffn=w_norm_ffn[l],
            w_up_g=w_up[:, :D_FF], w_up_v=w_up[:, D_FF:], w_fconv_g=fconv[:, :D_FF], w_fconv_v=fconv[:, D_FF:],
            w_down=full_big[l]["w_ffn_down"]))

    xs = x[0]
    saved = []
    for l in range(depth):
        xs, s = _layer_fwd(xs, params[l], l)
        saved.append(s)
    loss_row, dx, g_norm_final = _final_loss(xs, w_norm_final, loss_target[0], name="final_loss")
    grads = [None] * depth
    for l in reversed(range(depth)):
        dx, grads[l] = _layer_bwd(dx, params[l], saved[l], l)
    loss = lax.psum(loss_row[0, 0], ("x", "y", "c"))

    G = {
        "w_sconv": jnp.stack([grads[l]["w_sconv"] for l in range(depth)]),
        "w_gdn_conv": jnp.stack([grads[l]["w_gdn_conv"] for l in range(depth)]),
        "w_ffn_conv": jnp.stack([jnp.concatenate([grads[l]["w_fconv_g"], grads[l]["w_fconv_v"]], axis=1)
                                 for l in range(depth)]),
        "w_norm_mix": jnp.stack([grads[l]["wn_mix"].reshape(-1) for l in range(depth)]),
        "gdn_a_log": jnp.stack([grads[l]["a_log"].reshape(-1) for l in range(depth)]),
        "gdn_dt_bias": jnp.stack([grads[l]["dt_bias"].reshape(-1) for l in range(depth)]),
        "w_gdn_norm": jnp.stack([grads[l]["wgn"].reshape(-1) for l in range(depth)]),
        "w_norm_ffn": jnp.stack([grads[l]["wn_ffn"].reshape(-1) for l in range(depth)]),
        "w_norm_final": g_norm_final.reshape(-1),
    }

    def by_shard(l):
        g_in = _proj_from_kernel_layout(grads[l]["w_in"])
        g_in = g_in.reshape(D_MODEL, N_CHIPS, -1).transpose(1, 0, 2)
        return [g_in, grads[l]["w_out"].reshape(N_CHIPS, -1, D_MODEL), grads[l]["w_up"],
                grads[l]["w_down"].reshape(N_CHIPS, -1, D_MODEL)]

    g_layers = [by_shard(l) for l in range(depth)]
    from_sibling = _send_other_layer_to_sibling(g_layers[0], g_layers[1], name="rs_sibling")
    chip_sums = [_add_layers(g0, g1, ra, mc, name="rs_add_layers_" + n)
                 for n, g0, g1, ra in zip(BIG, g_layers[0], g_layers[1], from_sibling)]
    from_chips = _scatter_to_chips(chip_sums, name="rs_chips")
    mine = [_add_chips(p, rb, chip, name="rs_add_chips_" + n) for n, p, rb in zip(BIG, chip_sums, from_chips)]
    other = _swap_with_sibling(mine, name="rs_result")
    out_g, out_d, out_m, out_v = {}, {}, {}, {}
    for n, g_mine, g_other in zip(BIG, mine, other):
        out_g[n], out_d[n], out_m[n], out_v[n] = _adamw_layers(W[n], g_mine, g_other, M[n], V[n], mc,
                                                               name="adamw_" + n)

    small_names = SMALL_SHARDED + SMALL_REPLICATED
    small_full_shapes = [G[n].shape for n in small_names]
    n_small = sum(int(np.prod(s)) for s in small_full_shapes)
    red_rows = _round_up(n_small, 8 * LANES) // LANES
    partials = _allgather_devices(_pack_vec([G[n] for n in small_names], red_rows), name="reduce_small")
    summed = _unpack_vec(_sum_slots(partials, name="reduce_small_sum"), small_full_shapes)
    g_small = {}
    for n, a in zip(small_names, summed):
        if n in SMALL_SHARDED:
            width = a.shape[2] // N_CHIPS
            a = lax.dynamic_slice_in_dim(a, chip * width, width, axis=2)
        g_small[n] = a
    own_shapes = [W[n].shape for n in small_names]
    n_own = sum(int(np.prod(s)) for s in own_shapes)
    own_rows = _round_up(n_own, 8 * LANES) // LANES
    packed = [_pack_vec([src[n] for n in small_names], own_rows) for src in (W, g_small, M, V)]
    d_s, nm_s, nv_s = _adamw(*packed, name="adamw_small", tm=own_rows)
    for mat, dst in ((packed[1], out_g), (d_s, out_d), (nm_s, out_m), (nv_s, out_v)):
        for n, a in zip(small_names, _unpack_vec(mat, own_shapes)):
            dst[n] = a

    outs = [loss, dx[None]]
    for dst in (out_g, out_d, out_m, out_v):
        outs += [dst[n] for n in WEIGHTS]
    return tuple(outs)
```

```python
import functools

import jax
import jax.numpy as jnp
import numpy as np
from jax import lax
from jax.experimental import pallas as pl
from jax.experimental.pallas import tpu as pltpu

F32 = jnp.float32
BF16 = jnp.bfloat16
MESH = pl.DeviceIdType.MESH

NORM_EPS = 1e-6
GDN_HEADS = 4
GDN_CHUNK = 64
GDN_CONV = 4
GDN_SCAN_CHUNKS = 4
SB_HEADS = 4
SB_UNROLL = 4
SC_KERNEL = 3
FFN_CONV = 3
ADAM_LR = 0.001
ADAM_B1 = 0.9
ADAM_B2 = 0.999
ADAM_EPS = 1e-08
ADAM_WD = 0.01
ADAM_STEP = 10

VMEM_LIMIT_BYTES = 48 * 1024 * 1024
HALO = 8
CONV_BWD_SLAB = 128
LANES = 128
N_CHIPS = 4


def _pc(body, *, name, grid, in_specs, out_specs, out_shape, scratch_shapes=(), dims=None, aliases=None):
    params = dict(vmem_limit_bytes=VMEM_LIMIT_BYTES)
    if dims is not None:
        params["dimension_semantics"] = dims
    return pl.pallas_call(body, name=name, grid=grid, in_specs=in_specs, out_specs=out_specs, out_shape=out_shape,
                          scratch_shapes=list(scratch_shapes), input_output_aliases=aliases or {},
                          compiler_params=pltpu.CompilerParams(**params))


def _pc_prefetch(body, *, name, grid_spec, out_shape, dims):
    return pl.pallas_call(body, name=name, grid_spec=grid_spec, out_shape=out_shape,
                          compiler_params=pltpu.CompilerParams(vmem_limit_bytes=VMEM_LIMIT_BYTES,
                                                               dimension_semantics=dims))


def _pc_comm(body, *, name, in_specs, out_specs, out_shape, scratch_shapes):
    return pl.pallas_call(body, name=name, in_specs=in_specs, out_specs=out_specs, out_shape=out_shape,
                          scratch_shapes=list(scratch_shapes),
                          compiler_params=pltpu.CompilerParams(vmem_limit_bytes=VMEM_LIMIT_BYTES))


_DIMS = {"nn": (((1,), (0,)), ((), ())), "nt": (((1,), (1,)), ((), ())), "tn": (((0,), (0,)), ((), ()))}


def _matmul(a, b, mode, *, name, tm=512, tn=512, tk=512, out_dtype=F32, res=None, slabs=None):
    if mode == "nn":
        (M, K), (K2, N) = a.shape, b.shape
    elif mode == "nt":
        (M, K), (N, K2) = a.shape, b.shape
    else:
        (K, M), (K2, N) = a.shape, b.shape
    assert K == K2, (a.shape, b.shape, mode)
    tm, tn, tk = min(tm, M), min(tn, N), min(tk, K)
    assert M % tm == 0 and N % tn == 0 and K % tk == 0, (M, N, K, tm, tn, tk)
    nk = K // tk
    if mode == "tn":
        a_spec = pl.BlockSpec((tk, tm), lambda i, j, k: (k, i))
    else:
        a_spec = pl.BlockSpec((tm, tk), lambda i, j, k: (i, k))
    if mode == "nt":
        b_spec = pl.BlockSpec((tn, tk), lambda i, j, k: (j, k))
    else:
        b_spec = pl.BlockSpec((tk, tn), lambda i, j, k: (k, j))
    o_spec = pl.BlockSpec((tm, tn), lambda i, j, k: (i, j))
    has_res = res is not None
    dn = _DIMS[mode]

    def body(*refs):
        if has_res:
            a_ref, b_ref, r_ref, o_ref, acc = refs
        else:
            a_ref, b_ref, o_ref, acc = refs
        k = pl.program_id(2)
        p = lax.dot_general(a_ref[...].astype(BF16), b_ref[...].astype(BF16), dn, preferred_element_type=F32)

        def finish(total):
            if has_res:
                total = total + r_ref[...].astype(F32)
            o_ref[...] = total.astype(out_dtype)

        if nk == 1:
            finish(p)
        else:
            @pl.when(k == 0)
            def _():
                acc[...] = p

            @pl.when(k > 0)
            def _():
                acc[...] += p

            @pl.when(k == nk - 1)
            def _():
                finish(acc[...])

    in_specs = [a_spec, b_spec] + ([o_spec] if has_res else [])
    args = (a, b) + ((res,) if has_res else ())
    out_shape = jax.ShapeDtypeStruct((M, N), out_dtype)
    aliases = None
    if slabs is not None:
        n_slabs, first, into = slabs
        assert not has_res and tm == M
        o_spec = pl.BlockSpec((None, tm, tn), lambda i, j, k: (j + first, i, 0))
        out_shape = jax.ShapeDtypeStruct((n_slabs, M, tn), out_dtype)
        if into is not None:
            in_specs.append(pl.BlockSpec(memory_space=pl.ANY))
            args = args + (into,)
            aliases = {2: 0}
            inner = body

            def body(a_ref, b_ref, into_ref, o_ref, acc):
                inner(a_ref, b_ref, o_ref, acc)
    return _pc(body, name=name, grid=(M // tm, N // tn, nk), in_specs=in_specs, out_specs=o_spec,
               out_shape=out_shape, scratch_shapes=[pltpu.VMEM((tm, tn), F32)],
               dims=("parallel", "parallel", "arbitrary"), aliases=aliases)(*args)


def _offsets(parts, own_width_aligned):
    offs, at = [], 0
    for p in parts:
        assert at % (p.shape[1] if own_width_aligned else LANES) == 0, (at, p.shape)
        offs.append(at)
        at += p.shape[1]
    return offs, at


def _matmul_rows_parts(parts, w, mode, *, name, tm=512, res=None):
    M = parts[0].shape[0]
    offs, K = _offsets(parts, True)
    tm = min(tm, M)
    N = w.shape[1] if mode == "nn" else w.shape[0]
    assert (w.shape[0] if mode == "nn" else w.shape[1]) == K
    has_res = res is not None
    n = len(parts)

    def body(*refs):
        o_ref = refs[-1]
        total = None
        for s in range(n):
            p = lax.dot_general(refs[s][...].astype(BF16), refs[n + s][...].astype(BF16), _DIMS[mode],
                                preferred_element_type=F32)
            total = p if total is None else total + p
        if has_res:
            total = total + refs[2 * n][...]
        o_ref[...] = total

    in_specs = [pl.BlockSpec((tm, p.shape[1]), lambda i: (i, 0)) for p in parts]
    for p, off in zip(parts, offs):
        blk = off // p.shape[1]
        if mode == "nn":
            in_specs.append(pl.BlockSpec((p.shape[1], N), lambda i, blk=blk: (blk, 0)))
        else:
            in_specs.append(pl.BlockSpec((N, p.shape[1]), lambda i, blk=blk: (0, blk)))
    o_spec = pl.BlockSpec((tm, N), lambda i: (i, 0))
    args = tuple(parts) + (w,) * n + ((res,) if has_res else ())
    return _pc(body, name=name, grid=(M // tm,), in_specs=in_specs + ([o_spec] if has_res else []), out_specs=o_spec,
               out_shape=jax.ShapeDtypeStruct((M, N), F32), dims=("parallel",))(*args)


def _matmul_tn_parts(a, b, *, name, tk=512):
    a_parts = list(a) if isinstance(a, (list, tuple)) else [a]
    b_parts = list(b) if isinstance(b, (list, tuple)) else [b]
    assert len(a_parts) == 1 or len(b_parts) == 1
    a_offs, M = _offsets(a_parts, False)
    b_offs, N = _offsets(b_parts, False)
    K = a_parts[0].shape[0]
    tk = min(tk, K)
    na, nb = len(a_parts), len(b_parts)

    def body(*refs):
        o_ref = refs[-1]
        first = pl.program_id(0) == 0
        for s in range(na):
            for t in range(nb):
                p = lax.dot_general(refs[s][...].astype(BF16), refs[na + t][...].astype(BF16), _DIMS["tn"],
                                    preferred_element_type=F32)
                rows = slice(a_offs[s], a_offs[s] + a_parts[s].shape[1])
                cols = slice(b_offs[t], b_offs[t] + b_parts[t].shape[1])

                @pl.when(first)
                def _(p=p, rows=rows, cols=cols):
                    o_ref[rows, cols] = p

                @pl.when(jnp.logical_not(first))
                def _(p=p, rows=rows, cols=cols):
                    o_ref[rows, cols] += p

    in_specs = [pl.BlockSpec((tk, p.shape[1]), lambda k: (k, 0)) for p in a_parts + b_parts]
    return _pc(body, name=name, grid=(K // tk,), in_specs=in_specs, out_specs=pl.BlockSpec((M, N), lambda k: (0, 0)),
               out_shape=jax.ShapeDtypeStruct((M, N), F32), dims=("arbitrary",))(*a_parts, *b_parts)


def _rmsnorm_fwd(x, w, *, name, tm=512):
    L, D = x.shape
    tm = min(tm, L)

    def body(x_ref, w_ref, h_ref):
        xv = x_ref[...]
        r = lax.rsqrt(jnp.mean(xv * xv, axis=-1, keepdims=True) + NORM_EPS)
        h_ref[...] = (xv * r * w_ref[...]).astype(BF16)

    return _pc(body, name=name, grid=(L // tm,),
               in_specs=[pl.BlockSpec((tm, D), lambda i: (i, 0)), pl.BlockSpec((1, D), lambda i: (0, 0))],
               out_specs=pl.BlockSpec((tm, D), lambda i: (i, 0)), out_shape=jax.ShapeDtypeStruct((L, D), BF16),
               dims=("parallel",))(x, w.reshape(1, D))


def _rmsnorm_bwd(dh, x, w, dres, *, name, tm=512):
    L, D = x.shape
    tm = min(tm, L)

    def body(dh_ref, x_ref, w_ref, dres_ref, dx_ref, dw_ref):
        xv = x_ref[...]
        r = lax.rsqrt(jnp.mean(xv * xv, axis=-1, keepdims=True) + NORM_EPS)
        xhat = xv * r
        dhv = dh_ref[...]
        g = dhv * w_ref[...]
        dx_ref[...] = dres_ref[...] + r * (g - xhat * jnp.mean(g * xhat, axis=-1, keepdims=True))
        part = jnp.sum(dhv * xhat, axis=0, keepdims=True)

        @pl.when(pl.program_id(0) == 0)
        def _():
            dw_ref[...] = part

        @pl.when(pl.program_id(0) > 0)
        def _():
            dw_ref[...] += part

    row = pl.BlockSpec((tm, D), lambda i: (i, 0))
    vec = pl.BlockSpec((1, D), lambda i: (0, 0))
    return _pc(body, name=name, grid=(L // tm,), in_specs=[row, row, vec, row], out_specs=[row, vec],
               out_shape=[jax.ShapeDtypeStruct((L, D), F32), jax.ShapeDtypeStruct((1, D), F32)],
               dims=("arbitrary",))(dh, x, w.reshape(1, D), dres)


def _final_loss(x, w, tgt, *, name, tm=512):
    L, D = x.shape
    tm = min(tm, L)

    def body(x_ref, w_ref, t_ref, loss_ref, dx_ref, dw_ref):
        xv = x_ref[...]
        r = lax.rsqrt(jnp.mean(xv * xv, axis=-1, keepdims=True) + NORM_EPS)
        xhat = xv * r
        e = xhat * w_ref[...] - t_ref[...]
        lpart = jnp.broadcast_to(0.5 * jnp.sum(jnp.mean(e * e, axis=-1, keepdims=True), axis=0, keepdims=True),
                                 (1, LANES))
        dy = e * (1.0 / D)
        g = dy * w_ref[...]
        dx_ref[...] = r * (g - xhat * jnp.mean(g * xhat, axis=-1, keepdims=True))
        part = jnp.sum(dy * xhat, axis=0, keepdims=True)

        @pl.when(pl.program_id(0) == 0)
        def _():
            dw_ref[...] = part
            loss_ref[...] = lpart

        @pl.when(pl.program_id(0) > 0)
        def _():
            dw_ref[...] += part
            loss_ref[...] += lpart

    row = pl.BlockSpec((tm, D), lambda i: (i, 0))
    vec = pl.BlockSpec((1, D), lambda i: (0, 0))
    lsp = pl.BlockSpec((1, LANES), lambda i: (0, 0))
    return _pc(body, name=name, grid=(L // tm,), in_specs=[row, vec, row], out_specs=[lsp, row, vec],
               out_shape=[jax.ShapeDtypeStruct((1, LANES), F32), jax.ShapeDtypeStruct((L, D), F32),
                          jax.ShapeDtypeStruct((1, D), F32)],
               dims=("arbitrary",))(x, w.reshape(1, D), tgt)


def _shift_down(x, prev, k):
    if k == 0:
        return x
    r = pltpu.roll(x, k, 0)
    p = pltpu.roll(prev, k, 0)
    row = lax.broadcasted_iota(jnp.int32, p.shape, 0)
    head = jnp.where(row < k, p, r[:HALO])
    return jnp.concatenate([head, r[HALO:]], axis=0)


def _shift_up(x, j):
    if j == 0:
        return x
    return pltpu.roll(x, x.shape[0] - j, 0)


def _silu(x):
    return x * jax.nn.sigmoid(x)


def _conv_taps(p, p_prev, w):
    K = w.shape[0]
    out = None
    for k in range(K):
        term = w[k:k + 1, :] * _shift_down(p, p_prev, K - 1 - k)
        out = term if out is None else out + term
    return out


def _conv_pointwise_fwd(xs, ws, es, pre, post, outs, *, tc, tm, name):
    L = xs[0][0].shape[0]
    tm = min(tm, L)
    ncol = outs[0][0] // tc
    nrow = L // tm
    hb = tm // HALO
    nx, nw, ne, no = len(xs), len(ws), len(es), len(outs)
    K = ws[0][0].shape[0]

    def body(*refs):
        xc = [refs[2 * n][...] for n in range(nx)]
        i = pl.program_id(1)
        first = (i > 0).astype(F32)
        xp = [refs[2 * n + 1][...] * first for n in range(nx)]
        wv = [refs[2 * nx + n][...] for n in range(nw)]
        ev = [refs[2 * nx + nw + n][...] for n in range(ne)]
        o_refs = refs[2 * nx + nw + ne:]
        ps, pps = pre(*xc), pre(*xp)
        us = [_conv_taps(p, pp, w) for p, pp, w in zip(ps, pps, wv)]
        for o_ref, val in zip(o_refs, post(us, ev)):
            o_ref[...] = val.astype(o_ref.dtype)

    in_specs, args = [], []
    for arr, c0 in xs:
        off = c0 // tc
        in_specs.append(pl.BlockSpec((tm, tc), lambda j, i, off=off: (i, j + off)))
        in_specs.append(pl.BlockSpec((HALO, tc), lambda j, i, off=off: (jnp.maximum(i * hb - 1, 0), j + off)))
        args += [arr, arr]
    for arr, c0 in ws:
        off = c0 // tc
        in_specs.append(pl.BlockSpec((K, tc), lambda j, i, off=off: (0, j + off)))
        args.append(arr)
    for arr, c0 in es:
        off = c0 // tc
        in_specs.append(pl.BlockSpec((tm, tc), lambda j, i, off=off: (i, j + off)))
        args.append(arr)
    out_specs = [pl.BlockSpec((tm, tc), lambda j, i: (i, j)) for _ in range(no)]
    out_shape = [jax.ShapeDtypeStruct((L, c), dt) for c, dt in outs]
    return _pc(body, name=name, grid=(ncol, nrow), in_specs=in_specs, out_specs=out_specs, out_shape=out_shape,
               dims=("parallel", "parallel"))(*args)


def _conv_pointwise_bwd(xs, ws, es, dys, pre, post, width, *, tc, tm, name, out_dtype=BF16):
    L = xs[0][0].shape[0]
    tm = min(tm, L)
    ncol = width // tc
    nrow = L // tm
    hb = tm // HALO
    nx, nw, ne, ny = len(xs), len(ws), len(es), len(dys)
    K = ws[0][0].shape[0]

    slab = min(CONV_BWD_SLAB, tm)
    win = slab + 2 * HALO
    assert tm % slab == 0 and K - 1 <= HALO

    def body(*refs):
        i = pl.program_id(1)
        first = (i > 0).astype(F32)
        more = (i < nrow - 1).astype(F32)
        n_in = 3 * nx + nw + 2 * ne + 2 * ny
        n_out = nx + ne + nw
        dx_refs = refs[n_in:n_in + nx]
        de_refs = refs[n_in + nx:n_in + nx + ne]
        dw_refs = refs[n_in + nx + ne:n_in + n_out]
        pads = refs[n_in + n_out:]
        x_pads, e_pads, dy_pads = pads[:nx], pads[nx:nx + ne], pads[nx + ne:]
        pos = 0
        for n in range(nx):
            x_pads[n][0:HALO, :] = refs[pos + 1][...] * first
            x_pads[n][HALO:HALO + tm, :] = refs[pos][...]
            x_pads[n][HALO + tm:, :] = refs[pos + 2][...]
            pos += 3
        wv = [refs[pos + n][...] for n in range(nw)]
        pos += nw
        for n in range(ne):
            e_pads[n][0:HALO, :] = jnp.zeros((HALO, tc), F32)
            e_pads[n][HALO:HALO + tm, :] = refs[pos][...]
            e_pads[n][HALO + tm:, :] = refs[pos + 1][...]
            pos += 2
        for n in range(ny):
            dy_pads[n][0:HALO, :] = jnp.zeros((HALO, tc), F32)
            dy_pads[n][HALO:HALO + tm, :] = refs[pos][...].astype(F32)
            dy_pads[n][HALO + tm:, :] = refs[pos + 1][...].astype(F32) * more
            pos += 2

        def one_slab(t, dw_acc):
            r0 = pl.multiple_of(t * slab, HALO)
            xw = [x_pads[n][pl.ds(r0, win), :] for n in range(nx)]
            ew = [e_pads[n][pl.ds(r0, win), :] for n in range(ne)]
            dyw = [dy_pads[n][pl.ds(r0, win), :] for n in range(ny)]
            ps, pre_vjp = jax.vjp(lambda *x_: pre(*x_), *xw)
            shifted = [[p if k == K - 1 else pltpu.roll(p, K - 1 - k, 0) for k in range(K)] for p in ps]
            us = []
            for n in range(nw):
                u = None
                for k in range(K):
                    term = wv[n][k:k + 1, :] * shifted[n][k]
                    u = term if u is None else u + term
                us.append(u)
            _, post_vjp = jax.vjp(lambda u_, e_: post(u_, e_), us, ew)
            dus, des = post_vjp(dyw)
            dps, dw_new = [], []
            for n in range(nw):
                dp = None
                for k in range(K):
                    term = wv[n][k:k + 1, :] * (dus[n] if k == K - 1 else pltpu.roll(dus[n], win - (K - 1 - k), 0))
                    dp = term if dp is None else dp + term
                dps.append(dp)
                inner = dus[n][HALO:HALO + slab]
                dw_new.append([dw_acc[n][k] + jnp.sum(inner * shifted[n][k][HALO:HALO + slab], axis=0, keepdims=True)
                               for k in range(K)])
            dxs = pre_vjp(dps)
            rows = pl.ds(r0, slab)
            for r, v in zip(dx_refs, dxs):
                r[rows, :] = v[HALO:HALO + slab].astype(out_dtype)
            for r, v in zip(de_refs, des):
                r[rows, :] = v[HALO:HALO + slab].astype(out_dtype)
            return dw_new

        zero = [[jnp.zeros((1, tc), F32) for _ in range(K)] for _ in range(nw)]
        dw_tile = lax.fori_loop(0, tm // slab, one_slab, zero)
        for n in range(nw):
            for k in range(K):
                @pl.when(i == 0)
                def _(n=n, k=k):
                    dw_refs[n][k:k + 1, :] = dw_tile[n][k]

                @pl.when(i > 0)
                def _(n=n, k=k):
                    dw_refs[n][k:k + 1, :] += dw_tile[n][k]

    in_specs, args = [], []

    def add_rows(arr, c0, prev, nxt):
        off = c0 // tc
        in_specs.append(pl.BlockSpec((tm, tc), lambda j, i, off=off: (i, j + off)))
        args.append(arr)
        if prev:
            in_specs.append(pl.BlockSpec((HALO, tc), lambda j, i, off=off: (jnp.maximum(i * hb - 1, 0), j + off)))
            args.append(arr)
        if nxt:
            last = L // HALO - 1
            in_specs.append(pl.BlockSpec((HALO, tc), lambda j, i, off=off: (jnp.minimum((i + 1) * hb, last), j + off)))
            args.append(arr)

    for arr, c0 in xs:
        add_rows(arr, c0, True, True)
    for arr, c0 in ws:
        off = c0 // tc
        in_specs.append(pl.BlockSpec((K, tc), lambda j, i, off=off: (0, j + off)))
        args.append(arr)
    for arr, c0 in es:
        add_rows(arr, c0, False, True)
    for arr, c0 in dys:
        add_rows(arr, c0, False, True)
    tile = pl.BlockSpec((tm, tc), lambda j, i: (i, j))
    wtile = pl.BlockSpec((K, tc), lambda j, i: (0, j))
    out_specs = [tile] * (nx + ne) + [wtile] * nw
    out_shape = [jax.ShapeDtypeStruct((L, width), out_dtype)] * (nx + ne) + \
        [jax.ShapeDtypeStruct((K, width), F32)] * nw
    res = _pc(body, name=name, grid=(ncol, nrow), in_specs=in_specs, out_specs=out_specs, out_shape=out_shape,
              scratch_shapes=[pltpu.VMEM((tm + 2 * HALO, tc), F32)] * (nx + ne + ny),
              dims=("parallel", "arbitrary"))(*args)
    return res[:nx], res[nx:nx + ne], res[nx + ne:]


def _pre_identity(*x):
    return list(x)


def _pre_product(c, h):
    return [c * h]


def _post_silu(us, es):
    return [_silu(us[0])]


def _post_gate_mul(us, es):
    return [es[0] * us[0]]


def _post_swiglu(us, es):
    return [_silu(us[0]) * us[1]]


def _make_dot(passes):
    def raw(a, b, dn):
        a_hi = a.astype(BF16)
        b_hi = b.astype(BF16)
        out = lax.dot_general(a_hi, b_hi, dn, preferred_element_type=F32)
        if passes == 3:
            a_lo = (a - a_hi.astype(F32)).astype(BF16)
            b_lo = (b - b_hi.astype(F32)).astype(BF16)
            out = out + lax.dot_general(a_hi, b_lo, dn, preferred_element_type=F32)
            out = out + lax.dot_general(a_lo, b_hi, dn, preferred_element_type=F32)
        return out

    @jax.custom_vjp
    def nn(a, b):
        return raw(a, b, _DIMS["nn"])

    @jax.custom_vjp
    def nt(a, b):
        return raw(a, b, _DIMS["nt"])

    @jax.custom_vjp
    def tn(a, b):
        return raw(a, b, _DIMS["tn"])

    nn.defvjp(lambda a, b: (nn(a, b), (a, b)), lambda r, g: (nt(g, r[1]), tn(r[0], g)))
    nt.defvjp(lambda a, b: (nt(a, b), (a, b)), lambda r, g: (nn(g, r[1]), tn(g, r[0])))
    tn.defvjp(lambda a, b: (tn(a, b), (a, b)), lambda r, g: (nt(r[1], g), nn(r[0], g)))
    return nn, nt, tn


_NN1, _NT1, _TN1 = _make_dot(1)
_NN3, _NT3, _TN3 = _make_dot(3)


def _l2norm(x):
    return x * lax.rsqrt(jnp.sum(x * x, axis=-1, keepdims=True) + NORM_EPS)


def _unit_lower_inverse_raw(a_list):
    C = a_list[0].shape[0]
    ii = lax.broadcasted_iota(jnp.int32, (C, C), 0)
    jj = lax.broadcasted_iota(jnp.int32, (C, C), 1)
    eye = jnp.where(ii == jj, 1.0, 0.0)
    ts = [eye - a for a in a_list]
    ps = list(a_list)
    n = 2
    while n < C:
        ps = [_NN3(p, p) for p in ps]
        ts = [t + _NN3(t, p) for t, p in zip(ts, ps)]
        n *= 2
    return ts


@jax.custom_vjp
def _unit_lower_inverse(a_list):
    return _unit_lower_inverse_raw(a_list)


def _unit_lower_inverse_fwd(a_list):
    ts = _unit_lower_inverse_raw(a_list)
    return ts, ts


def _unit_lower_inverse_bwd(ts, gs):
    xs = [_TN3(t, g) for t, g in zip(ts, gs)]
    return ([-_NT3(x, t) for x, t in zip(xs, ts)],)


_unit_lower_inverse.defvjp(_unit_lower_inverse_fwd, _unit_lower_inverse_bwd)


def _gdn_prep(units):
    C, Dh = units[0][0].shape
    ii = lax.broadcasted_iota(jnp.int32, (C, C), 0)
    jj = lax.broadcasted_iota(jnp.int32, (C, C), 1)
    lane = lax.broadcasted_iota(jnp.int32, (1, C), 1)
    causal = ii >= jj
    strict = ii > jj
    qs = [_l2norm(un[0]) * (Dh ** -0.5) for un in units]
    ks = [_l2norm(un[1]) for un in units]
    betas = [jax.nn.sigmoid(un[4]) for un in units]
    gs = [-jnp.exp(un[5]) * jax.nn.softplus(un[3] + un[6]) for un in units]
    gc_rows = [jnp.sum(jnp.where(ii <= jj, g, 0.0), axis=0, keepdims=True) for g in gs]
    gc_cols = [jnp.sum(jnp.where(ii == jj, r, 0.0), axis=1, keepdims=True) for r in gc_rows]
    decays = [jnp.where(causal, jnp.exp(jnp.where(causal, c - r, 0.0)), 0.0) for c, r in zip(gc_cols, gc_rows)]
    kbs = [k * b for k, b in zip(ks, betas)]
    kks = [_NT1(kb, k) for kb, k in zip(kbs, ks)]
    qks = [_NT1(q, k) for q, k in zip(qs, ks)]
    ts = _unit_lower_inverse([jnp.where(strict, kk * d, 0.0) for kk, d in zip(kks, decays)])
    eg_cols = [jnp.exp(c) for c in gc_cols]
    uws = [_NN3(t, jnp.concatenate([un[2] * b, kb * e], axis=1))
           for t, un, b, kb, e in zip(ts, units, betas, kbs, eg_cols)]
    out = []
    for q, k, qk, d, uw, e, r, c in zip(qs, ks, qks, decays, uws, eg_cols, gc_rows, gc_cols):
        g_last = jnp.sum(jnp.where(lane == C - 1, r, 0.0), axis=1, keepdims=True)
        out.append((q * e, k * jnp.exp(g_last - c), uw[:, :Dh], uw[:, Dh:], jnp.where(causal, qk * d, 0.0),
                    jnp.broadcast_to(jnp.exp(g_last), (1, Dh))))
    return out


def _gdn_step(units):
    v_news = [un[3] - _NN1(un[4], un[0]) for un in units]
    o_state = [_NN1(un[1], un[0]) for un in units]
    o_intra = [_NN1(un[5], vn) for un, vn in zip(units, v_news)]
    s_adds = [_TN1(un[2], vn) for un, vn in zip(units, v_news)]
    out = []
    for un, a, b, s_add in zip(units, o_state, o_intra, s_adds):
        o = a + b
        y = o * lax.rsqrt(jnp.mean(o * o, axis=-1, keepdims=True) + NORM_EPS) * un[8] * _silu(un[7])
        out.append((y, un[0] * un[6] + s_add))
    return out


def _gdn_prep_fwd(qkv, gab, gab_col, a_log, dt_bias, *, name, chunks=4):
    L = qkv.shape[0]
    H, C = GDN_HEADS, GDN_CHUNK
    W = qkv.shape[1] // 3
    Dh = W // H
    N = L // C
    chunks = min(chunks, N)
    R = chunks * C
    gab_off = gab_col // LANES

    def body(q_ref, k_ref, v_ref, gab_ref, al_ref, dt_ref, qe_ref, ke_ref, u_ref, w_ref, at_ref, eg_ref):
        where = [(cc, h) for cc in range(chunks) for h in range(H)]
        units = []
        for cc, h in where:
            rows, sl = slice(cc * C, (cc + 1) * C), slice(h * Dh, (h + 1) * Dh)
            units.append((q_ref[rows, sl], k_ref[rows, sl], v_ref[rows, sl], gab_ref[rows, h:h + 1],
                          gab_ref[rows, H + h:H + h + 1], al_ref[h], dt_ref[h]))
        for (cc, h), (qe, ke, u, w, attn, eg) in zip(where, _gdn_prep(units)):
            rows, sl = slice(cc * C, (cc + 1) * C), slice(h * Dh, (h + 1) * Dh)
            qe_ref[rows, sl] = qe
            ke_ref[rows, sl] = ke
            u_ref[rows, sl] = u
            w_ref[rows, sl] = w
            at_ref[h, rows, :] = attn
            eg_ref[cc, h:h + 1, :] = eg

    col = lambda c: pl.BlockSpec((R, W), lambda n, c=c: (n, c))
    tok = pl.BlockSpec((R, LANES), lambda n: (n, gab_off))
    par = pl.BlockSpec((H, 1, 1), lambda n: (0, 0, 0))
    wide = pl.BlockSpec((R, W), lambda n: (n, 0))
    return _pc(body, name=name, grid=(N // chunks,), in_specs=[col(0), col(1), col(2), tok, par, par],
               out_specs=[wide, wide, wide, wide, pl.BlockSpec((H, R, C), lambda n: (0, n, 0)),
                          pl.BlockSpec((chunks, H, Dh), lambda n: (n, 0, 0))],
               out_shape=[jax.ShapeDtypeStruct((L, W), F32)] * 4 + [jax.ShapeDtypeStruct((H, L, C), F32),
                                                                   jax.ShapeDtypeStruct((N, H, Dh), F32)],
               dims=("parallel",))(qkv, qkv, qkv, gab, a_log, dt_bias)


def _gdn_prep_bwd(qkv, gab, gab_col, a_log, dt_bias, dqe, dke, du, dw, dattn, deg, gab_width, *, name, chunks=4):
    L = qkv.shape[0]
    H, C = GDN_HEADS, GDN_CHUNK
    W = qkv.shape[1] // 3
    Dh = W // H
    N = L // C
    chunks = min(chunks, N)
    R = chunks * C
    gab_off = gab_col // LANES

    def body(q_ref, k_ref, v_ref, gab_ref, al_ref, dt_ref, dqe_ref, dke_ref, du_ref, dw_ref, dat_ref, deg_ref,
             dqkv_ref, dgab_ref, dal_ref, ddt_ref):
        first = pl.program_id(0) == 0
        lane = lax.broadcasted_iota(jnp.int32, (C, gab_width), 1)
        dal_sum, ddt_sum = [None] * H, [None] * H
        where = [(cc, h) for cc in range(chunks) for h in range(H)]
        units, cots = [], []
        for cc, h in where:
            rows, sl = slice(cc * C, (cc + 1) * C), slice(h * Dh, (h + 1) * Dh)
            units.append((q_ref[rows, sl], k_ref[rows, sl], v_ref[rows, sl], gab_ref[rows, h:h + 1],
                          gab_ref[rows, H + h:H + h + 1], al_ref[h], dt_ref[h]))
            cots.append((dqe_ref[rows, sl], dke_ref[rows, sl], du_ref[rows, sl], dw_ref[rows, sl],
                         dat_ref[h, rows, :], deg_ref[cc, h:h + 1, :]))
        _, vjp = jax.vjp(_gdn_prep, units)
        (d_units,) = vjp(cots)
        dgabs = [jnp.zeros((C, gab_width), F32) for _ in range(chunks)]
        for (cc, h), (dq, dk, dv, dga, dgb, dal, ddt) in zip(where, d_units):
            rows = slice(cc * C, (cc + 1) * C)
            dqkv_ref[rows, h * Dh:(h + 1) * Dh] = dq
            dqkv_ref[rows, W + h * Dh:W + (h + 1) * Dh] = dk
            dqkv_ref[rows, 2 * W + h * Dh:2 * W + (h + 1) * Dh] = dv
            dgabs[cc] = dgabs[cc] + jnp.where(lane == h, dga, 0.0) + jnp.where(lane == H + h, dgb, 0.0)
            dal_sum[h] = dal if dal_sum[h] is None else dal_sum[h] + dal
            ddt_sum[h] = ddt if ddt_sum[h] is None else ddt_sum[h] + ddt
        for cc in range(chunks):
            dgab_ref[cc * C:(cc + 1) * C, :] = dgabs[cc].astype(BF16)

        @pl.when(first)
        def _():
            for h in range(H):
                dal_ref[h] = dal_sum[h]
                ddt_ref[h] = ddt_sum[h]

        @pl.when(jnp.logical_not(first))
        def _():
            for h in range(H):
                dal_ref[h] += dal_sum[h]
                ddt_ref[h] += ddt_sum[h]

    col = lambda c: pl.BlockSpec((R, W), lambda n, c=c: (n, c))
    tok = pl.BlockSpec((R, LANES), lambda n: (n, gab_off))
    par = pl.BlockSpec((H, 1, 1), lambda n: (0, 0, 0))
    wide = pl.BlockSpec((R, W), lambda n: (n, 0))
    att = pl.BlockSpec((H, R, C), lambda n: (0, n, 0))
    egs = pl.BlockSpec((chunks, H, Dh), lambda n: (n, 0, 0))
    return _pc(body, name=name, grid=(N // chunks,),
               in_specs=[col(0), col(1), col(2), tok, par, par, wide, wide, wide, wide, att, egs],
               out_specs=[pl.BlockSpec((R, 3 * W), lambda n: (n, 0)), pl.BlockSpec((R, gab_width), lambda n: (n, 0)),
                          par, par],
               out_shape=[jax.ShapeDtypeStruct((L, 3 * W), F32), jax.ShapeDtypeStruct((L, gab_width), BF16)]
               + [jax.ShapeDtypeStruct((H, 1, 1), F32)] * 2,
               dims=("arbitrary",))(qkv, qkv, qkv, gab, a_log, dt_bias, dqe, dke, du, dw, dattn, deg)


def _gdn_scan_fwd(qe, ke, u, w, attn, eg, gz, gz_col, wgn, *, name):
    L, W = qe.shape
    H, C = GDN_HEADS, GDN_CHUNK
    Dh = W // H
    N = L // C
    gz_off = gz_col // W
    cps = min(GDN_SCAN_CHUNKS, N)
    R = cps * C

    def body(qe_ref, ke_ref, u_ref, w_ref, at_ref, eg_ref, gz_ref, wgn_ref, y_ref, st_ref, s_scr):
        @pl.when(pl.program_id(0) == 0)
        def _():
            s_scr[...] = jnp.zeros_like(s_scr)

        S = [s_scr[h] for h in range(H)]
        for cc in range(cps):
            rows = slice(cc * C, (cc + 1) * C)
            units = []
            for h in range(H):
                sl = slice(h * Dh, (h + 1) * Dh)
                st_ref[cc, h] = S[h]
                units.append((S[h], qe_ref[rows, sl], ke_ref[rows, sl], u_ref[rows, sl], w_ref[rows, sl],
                              at_ref[h, rows, :], eg_ref[cc, h:h + 1, :], gz_ref[rows, sl], wgn_ref[...]))
            for h, (y, S_new) in enumerate(_gdn_step(units)):
                y_ref[rows, h * Dh:(h + 1) * Dh] = y.astype(BF16)
                S[h] = S_new
        for h in range(H):
            s_scr[h] = S[h]

    wide = pl.BlockSpec((R, W), lambda n: (n, 0))
    return _pc(body, name=name, grid=(N // cps,),
               in_specs=[wide, wide, wide, wide, pl.BlockSpec((H, R, C), lambda n: (0, n, 0)),
                         pl.BlockSpec((cps, H, Dh), lambda n: (n, 0, 0)),
                         pl.BlockSpec((R, W), lambda n: (n, gz_off)), pl.BlockSpec((1, Dh), lambda n: (0, 0))],
               out_specs=[wide, pl.BlockSpec((cps, H, Dh, Dh), lambda n: (n, 0, 0, 0))],
               out_shape=[jax.ShapeDtypeStruct((L, W), BF16), jax.ShapeDtypeStruct((N, H, Dh, Dh), F32)],
               scratch_shapes=[pltpu.VMEM((H, Dh, Dh), F32)],
               dims=("arbitrary",))(qe, ke, u, w, attn, eg, gz, wgn.reshape(1, Dh))


def _gdn_scan_bwd(qe, ke, u, w, attn, eg, gz, gz_col, wgn, states, dy, dy_col, *, name):
    L, W = qe.shape
    H, C = GDN_HEADS, GDN_CHUNK
    Dh = W // H
    N = L // C
    gz_off = gz_col // W
    dy_off = dy_col // W
    cps = min(GDN_SCAN_CHUNKS, N)
    R = cps * C
    steps = N // cps

    def body(qe_ref, ke_ref, u_ref, w_ref, at_ref, eg_ref, gz_ref, wgn_ref, st_ref, dy_ref,
             dqe_ref, dke_ref, du_ref, dw_ref, dat_ref, deg_ref, dgz_ref, dwgn_ref, ds_scr):
        first = pl.program_id(0) == 0

        @pl.when(first)
        def _():
            ds_scr[...] = jnp.zeros_like(ds_scr)

        dwgn = None
        dS = [ds_scr[h] for h in range(H)]
        for cc in reversed(range(cps)):
            rows = slice(cc * C, (cc + 1) * C)
            units, cots = [], []
            for h in range(H):
                sl = slice(h * Dh, (h + 1) * Dh)
                units.append((st_ref[cc, h], qe_ref[rows, sl], ke_ref[rows, sl], u_ref[rows, sl], w_ref[rows, sl],
                              at_ref[h, rows, :], eg_ref[cc, h:h + 1, :], gz_ref[rows, sl], wgn_ref[...]))
                cots.append((dy_ref[rows, sl].astype(F32), dS[h]))
            _, vjp = jax.vjp(_gdn_step, units)
            (d_units,) = vjp(cots)
            for h, (dS_h, dqe, dke, du, dw, dat, deg, dgz, dwg) in enumerate(d_units):
                sl = slice(h * Dh, (h + 1) * Dh)
                dS[h] = dS_h
                dqe_ref[rows, sl] = dqe
                dke_ref[rows, sl] = dke
                du_ref[rows, sl] = du
                dw_ref[rows, sl] = dw
                dat_ref[h, rows, :] = dat
                deg_ref[cc, h:h + 1, :] = deg
                dgz_ref[rows, sl] = dgz.astype(BF16)
                dwgn = dwg if dwgn is None else dwgn + dwg
        for h in range(H):
            ds_scr[h] = dS[h]

        @pl.when(first)
        def _():
            dwgn_ref[...] = dwgn

        @pl.when(jnp.logical_not(first))
        def _():
            dwgn_ref[...] += dwgn

    rev = lambda n: steps - 1 - n
    wide = pl.BlockSpec((R, W), lambda n: (rev(n), 0))
    att = pl.BlockSpec((H, R, C), lambda n: (0, rev(n), 0))
    egs = pl.BlockSpec((cps, H, Dh), lambda n: (rev(n), 0, 0))
    vec = pl.BlockSpec((1, Dh), lambda n: (0, 0))
    return _pc(body, name=name, grid=(steps,),
               in_specs=[wide, wide, wide, wide, att, egs, pl.BlockSpec((R, W), lambda n: (rev(n), gz_off)), vec,
                         pl.BlockSpec((cps, H, Dh, Dh), lambda n: (rev(n), 0, 0, 0)),
                         pl.BlockSpec((R, W), lambda n: (rev(n), dy_off))],
               out_specs=[wide, wide, wide, wide, att, egs, wide, vec],
               out_shape=[jax.ShapeDtypeStruct((L, W), F32)] * 4 + [jax.ShapeDtypeStruct((H, L, C), F32),
                                                                   jax.ShapeDtypeStruct((N, H, Dh), F32),
                                                                   jax.ShapeDtypeStruct((L, W), BF16),
                                                                   jax.ShapeDtypeStruct((1, Dh), F32)],
               scratch_shapes=[pltpu.VMEM((H, Dh, Dh), F32)],
               dims=("arbitrary",))(qe, ke, u, w, attn, eg, gz, wgn.reshape(1, Dh), states, dy)


def _sb_scores(z, mask):
    sp = jnp.maximum(z, 0.0) + jnp.log(1.0 + jnp.exp(-jnp.abs(z)))
    lom = -sp if mask is None else jnp.where(mask, -sp, 0.0)
    return lom, z - sp


def _sb_masks(tq, width, dh):
    rr = lax.broadcasted_iota(jnp.int32, (tq, tq), 0)
    cc = lax.broadcasted_iota(jnp.int32, (tq, tq), 1)
    first_head = lax.broadcasted_iota(jnp.int32, (tq, width), 1) < dh
    return cc < rr, jnp.where(rr > cc, 1.0, 0.0).astype(BF16), first_head


def _sb_fwd(qkv, *, name, tq=256):
    L = qkv.shape[0]
    H = SB_HEADS
    width = 2 * (qkv.shape[1] // 3 // H)
    dh = width // 2
    npair = H // 2
    tq = min(tq, L)
    nq = L // tq

    def body(q_ref, k_ref, v_ref, o_ref, c_ref):
        i = pl.program_id(1)
        diag, tri, first_head = _sb_masks(tq, width, dh)
        qp = q_ref[...]
        zero = jnp.zeros_like(qp)
        qs = (jnp.where(first_head, qp, zero), jnp.where(first_head, zero, qp))

        def blocks(js, carry, mask):
            units = [(b, hd) for b in range(len(js)) for hd in range(2)]
            starts = [pl.multiple_of(j * tq, tq) for j in js]
            ks = [k_ref[pl.ds(st, tq), :] for st in starts]
            vs = [v_ref[pl.ds(st, tq), :] for st in starts]
            zs = {(b, hd): lax.dot_general(qs[hd], ks[b], _DIMS["nt"], preferred_element_type=F32)
                  for b, hd in units}
            scores = {un: _sb_scores(zs[un], mask) for un in units}
            later = {un: jnp.dot(scores[un][0].astype(BF16), tri, preferred_element_type=F32) for un in units}
            cs = [carry[hd][0] for hd in range(2)]
            accs = [carry[hd][1] for hd in range(2)]
            for b, hd in units:
                lom, lb = scores[(b, hd)]
                a = jnp.exp(lb + (cs[hd] + later[(b, hd)]))
                if mask is not None:
                    a = jnp.where(mask, a, 0.0)
                accs[hd] = accs[hd] + jnp.dot(a.astype(BF16), vs[b], preferred_element_type=F32)
                cs[hd] = cs[hd] + jnp.sum(lom, axis=1, keepdims=True)
            return tuple((cs[hd], accs[hd]) for hd in range(2))

        init = tuple((jnp.zeros((tq, 1), F32), jnp.zeros((tq, width), F32)) for _ in range(2))
        carry = blocks([i], init, diag)
        carry = lax.fori_loop(0, i % SB_UNROLL, lambda t, cr: blocks([i - 1 - t], cr, None), carry)
        left = i - i % SB_UNROLL
        carry = lax.fori_loop(0, left // SB_UNROLL,
                              lambda t, cr: blocks([left - 1 - SB_UNROLL * t - b for b in range(SB_UNROLL)], cr, None),
                              carry)
        o_ref[...] = jnp.where(first_head, carry[0][1], carry[1][1]).astype(BF16)
        c_ref[0] = carry[0][0]
        c_ref[1] = carry[1][0]

    return _pc(body, name=name, grid=(npair, nq),
               in_specs=[pl.BlockSpec((tq, width), lambda p, i: (i, p)),
                         pl.BlockSpec((L, width), lambda p, i: (0, npair + p)),
                         pl.BlockSpec((L, width), lambda p, i: (0, 2 * npair + p))],
               out_specs=[pl.BlockSpec((tq, width), lambda p, i: (i, p)),
                          pl.BlockSpec((2, tq, 1), lambda p, i: (p, i, 0))],
               out_shape=[jax.ShapeDtypeStruct((L, npair * width), BF16), jax.ShapeDtypeStruct((H, L, 1), F32)],
               dims=("parallel", "parallel"))(qkv, qkv, qkv)


def _sb_bwd(qkv, ctot, do, do_col, scale, *, name, tq=256):
    L = qkv.shape[0]
    H = SB_HEADS
    width = 2 * (qkv.shape[1] // 3 // H)
    dh = width // 2
    npair = H // 2
    tq = min(tq, L)
    nq = L // tq
    do_off = do_col // width

    def body(q_ref, k_ref, v_ref, c_ref, do_ref, dq_ref, dk_ref, dv_ref):
        i = pl.program_id(1)

        @pl.when(i == 0)
        def _():
            dk_ref[...] = jnp.zeros_like(dk_ref)
            dv_ref[...] = jnp.zeros_like(dv_ref)

        diag, tri_later, first_head = _sb_masks(tq, width, dh)
        rr = lax.broadcasted_iota(jnp.int32, (tq, tq), 0)
        cc = lax.broadcasted_iota(jnp.int32, (tq, tq), 1)
        tri_before = jnp.where(rr < cc, 1.0, 0.0).astype(BF16)
        qp = q_ref[...]
        dop = do_ref[...].astype(BF16)
        zero = jnp.zeros_like(qp)
        qs = (jnp.where(first_head, qp, zero), jnp.where(first_head, zero, qp))
        dos = (jnp.where(first_head, dop, zero), jnp.where(first_head, zero, dop))
        ctots = (c_ref[0], c_ref[1])

        def blocks(js, carry, mask):
            nb = len(js)
            units = [(b, hd) for b in range(nb) for hd in range(2)]
            starts = [pl.multiple_of(j * tq, tq) for j in js]
            ks = [k_ref[pl.ds(st, tq), :] for st in starts]
            vs = [v_ref[pl.ds(st, tq), :] for st in starts]
            zs = {(b, hd): lax.dot_general(qs[hd], ks[b], _DIMS["nt"], preferred_element_type=F32)
                  for b, hd in units}
            das = {(b, hd): lax.dot_general(dos[hd], vs[b], _DIMS["nt"], preferred_element_type=F32)
                   for b, hd in units}
            scores = {un: _sb_scores(zs[un], mask) for un in units}
            later = {un: jnp.dot(scores[un][0].astype(BF16), tri_later, preferred_element_type=F32) for un in units}
            pcs = [carry[hd][0] for hd in range(2)]
            avals = {}
            for b, hd in units:
                pcs[hd] = pcs[hd] + jnp.sum(scores[(b, hd)][0], axis=1, keepdims=True)
                a = jnp.exp(scores[(b, hd)][1] + ((ctots[hd] - pcs[hd]) + later[(b, hd)]))
                avals[(b, hd)] = a if mask is None else jnp.where(mask, a, 0.0)
            gs = {un: das[un] * avals[un] for un in units}
            before = {un: jnp.dot(gs[un].astype(BF16), tri_before, preferred_element_type=F32) for un in units}
            pgs = [carry[hd][1] for hd in range(2)]
            dzs = {}
            for b, hd in units:
                sig = jnp.exp(scores[(b, hd)][1])
                dz = gs[(b, hd)] * (1.0 - sig) - (pgs[hd] + before[(b, hd)]) * sig
                dzs[(b, hd)] = (dz if mask is None else jnp.where(mask, dz, 0.0)).astype(BF16)
                pgs[hd] = pgs[hd] + jnp.sum(gs[(b, hd)], axis=1, keepdims=True)
            dqs = [carry[hd][2] for hd in range(2)]
            for b, hd in units:
                dqs[hd] = dqs[hd] + jnp.dot(dzs[(b, hd)], ks[b], preferred_element_type=F32)
            for b in range(nb):
                dk_ref[pl.ds(starts[b], tq), :] += sum(
                    lax.dot_general(dzs[(b, hd)], qs[hd], _DIMS["tn"], preferred_element_type=F32) for hd in range(2))
                dv_ref[pl.ds(starts[b], tq), :] += sum(
                    lax.dot_general(avals[(b, hd)].astype(BF16), dos[hd], _DIMS["tn"], preferred_element_type=F32)
                    for hd in range(2))
            return tuple((pcs[hd], pgs[hd], dqs[hd]) for hd in range(2))

        col = jnp.zeros((tq, 1), F32)
        init = tuple((col, col, jnp.zeros((tq, width), F32)) for _ in range(2))
        carry = lax.fori_loop(0, i // SB_UNROLL,
                              lambda t, cr: blocks([SB_UNROLL * t + b for b in range(SB_UNROLL)], cr, None), init)
        carry = lax.fori_loop(0, i % SB_UNROLL, lambda t, cr: blocks([i - i % SB_UNROLL + t], cr, None), carry)
        carry = blocks([i], carry, diag)
        dq_ref[...] = (jnp.where(first_head, carry[0][2], carry[1][2]) * scale).astype(BF16)

    tile = pl.BlockSpec((tq, width), lambda p, i: (i, p))
    full = pl.BlockSpec((L, width), lambda p, i: (0, p))
    sds = jax.ShapeDtypeStruct((L, npair * width), F32)
    return _pc(body, name=name, grid=(npair, nq),
               in_specs=[tile, pl.BlockSpec((L, width), lambda p, i: (0, npair + p)),
                         pl.BlockSpec((L, width), lambda p, i: (0, 2 * npair + p)),
                         pl.BlockSpec((2, tq, 1), lambda p, i: (p, i, 0)),
                         pl.BlockSpec((tq, width), lambda p, i: (i, do_off + p))],
               out_specs=[tile, full, full],
               out_shape=[jax.ShapeDtypeStruct((L, npair * width), BF16), sds, sds],
               dims=("parallel", "arbitrary"))(qkv, qkv, qkv, ctot, do)


def _adamw(w, g, m, v, *, name, tm=256):
    R, C = w.shape
    tm = min(tm, R)
    assert R % tm == 0, (R, tm)
    c1 = 1.0 - ADAM_B1 ** ADAM_STEP
    c2 = 1.0 - ADAM_B2 ** ADAM_STEP

    def body(w_ref, g_ref, m_ref, v_ref, d_ref, nm_ref, nv_ref):
        gv = g_ref[...]
        nm = ADAM_B1 * m_ref[...] + (1.0 - ADAM_B1) * gv
        nv = ADAM_B2 * v_ref[...] + (1.0 - ADAM_B2) * (gv * gv)
        d_ref[...] = -ADAM_LR * ((nm / c1) / (jnp.sqrt(nv / c2) + ADAM_EPS) + ADAM_WD * w_ref[...])
        nm_ref[...] = nm
        nv_ref[...] = nv

    blk = pl.BlockSpec((tm, C), lambda i: (i, 0))
    sds = jax.ShapeDtypeStruct((R, C), F32)
    return _pc(body, name=name, grid=(R // tm,), in_specs=[blk] * 4, out_specs=[blk] * 3, out_shape=[sds] * 3,
               dims=("parallel",))(w, g, m, v)


ELEMENTWISE_BLOCK_BYTES = 1 << 20


def _row_tile(rows, cols):
    for t in (512, 384, 352, 256, 176, 128, 88, 64, 32, 16, 8):
        if rows % t == 0 and t * cols * 4 <= ELEMENTWISE_BLOCK_BYTES:
            return t
    raise ValueError((rows, cols))


def _adamw_layers(w, g_mine, g_other, m, v, c, *, name):
    _, R, C = w.shape
    tm = _row_tile(R, C)
    c1 = 1.0 - ADAM_B1 ** ADAM_STEP
    c2 = 1.0 - ADAM_B2 ** ADAM_STEP

    def body(c_ref, w_ref, gm_ref, go_ref, m_ref, v_ref, g_ref, d_ref, nm_ref, nv_ref):
        gv = jnp.where(pl.program_id(0) == c_ref[0], gm_ref[...], go_ref[...])
        nm = ADAM_B1 * m_ref[...] + (1.0 - ADAM_B1) * gv
        nv = ADAM_B2 * v_ref[...] + (1.0 - ADAM_B2) * (gv * gv)
        g_ref[...] = gv
        d_ref[...] = -ADAM_LR * ((nm / c1) / (jnp.sqrt(nv / c2) + ADAM_EPS) + ADAM_WD * w_ref[...])
        nm_ref[...] = nm
        nv_ref[...] = nv

    slab = pl.BlockSpec((None, tm, C), lambda l, i, c_ref: (l, i, 0))
    mine = pl.BlockSpec((tm, C), lambda l, i, c_ref: (jnp.where(l == c_ref[0], i, 0), 0))
    other = pl.BlockSpec((tm, C), lambda l, i, c_ref: (jnp.where(l == c_ref[0], 0, i), 0))
    grid_spec = pltpu.PrefetchScalarGridSpec(num_scalar_prefetch=1, grid=(2, R // tm),
                                             in_specs=[slab, mine, other, slab, slab], out_specs=[slab] * 4)
    return _pc_prefetch(body, name=name, grid_spec=grid_spec, out_shape=[jax.ShapeDtypeStruct(w.shape, F32)] * 4,
                        dims=("parallel", "parallel"))(c.reshape(1).astype(jnp.int32), w, g_mine, g_other, m, v)


def _add_layers(g0, g1, ra, c, *, name):
    S, R, C = ra.shape
    tm = _row_tile(R, C)

    def body(c_ref, g0_ref, g1_ref, r_ref, o_ref):
        mine = jnp.where(c_ref[0] == 0, g0_ref[...], g1_ref[...])
        o_ref[...] = (mine + r_ref[...]).astype(BF16)

    def walked_if(layer):
        return lambda s, i, c_ref: (jnp.where(c_ref[0] == layer, s, 0), jnp.where(c_ref[0] == layer, i, 0), 0)

    blk = lambda s, i, c_ref: (s, i, 0)
    grid_spec = pltpu.PrefetchScalarGridSpec(
        num_scalar_prefetch=1, grid=(S, R // tm),
        in_specs=[pl.BlockSpec((None, tm, C), walked_if(0)), pl.BlockSpec((None, tm, C), walked_if(1)),
                  pl.BlockSpec((None, tm, C), blk)],
        out_specs=pl.BlockSpec((None, tm, C), blk))
    return _pc_prefetch(body, name=name, grid_spec=grid_spec, out_shape=jax.ShapeDtypeStruct((S, R, C), BF16),
                        dims=("parallel", "parallel"))(c.reshape(1).astype(jnp.int32), g0, g1, ra)


def _add_chips(p, rb, chip, *, name):
    S, Rh, C = p.shape
    tm = _row_tile(Rh, C)

    def body(s_ref, p_ref, r_ref, o_ref):
        o_ref[...] = ((p_ref[...].astype(F32) + r_ref[0].astype(F32)) + r_ref[1].astype(F32)) + r_ref[2].astype(F32)

    grid_spec = pltpu.PrefetchScalarGridSpec(
        num_scalar_prefetch=1, grid=(Rh // tm,),
        in_specs=[pl.BlockSpec((None, tm, C), lambda i, s_ref: (s_ref[0], i, 0)),
                  pl.BlockSpec((3, tm, C), lambda i, s_ref: (0, i, 0))],
        out_specs=pl.BlockSpec((tm, C), lambda i, s_ref: (i, 0)))
    return _pc_prefetch(body, name=name, grid_spec=grid_spec, out_shape=jax.ShapeDtypeStruct((Rh, C), F32),
                        dims=("parallel",))(chip.reshape(1).astype(jnp.int32), p, rb)


def _sum_slots(g, *, name):
    n, R, C = g.shape

    def body(g_ref, o_ref):
        acc = g_ref[0]
        for s in range(1, n):
            acc = acc + g_ref[s]
        o_ref[...] = acc

    return _pc(body, name=name, grid=(1,), in_specs=[pl.BlockSpec((n, R, C), lambda i: (0, 0, 0))],
               out_specs=pl.BlockSpec((R, C), lambda i: (0, 0)), out_shape=jax.ShapeDtypeStruct((R, C), F32),
               dims=("arbitrary",))(g)


ANY = pl.BlockSpec(memory_space=pl.ANY)


def _place():
    return lax.axis_index("x"), lax.axis_index("y"), lax.axis_index("c")


def _other_chips(x, y):
    return [(1 - x, y), (x, 1 - y), (1 - x, 1 - y)]


def _allgather_chips(ws, *, name):
    n = len(ws)

    def body(*refs):
        w_refs, out_refs, send_sems, recv_sems = refs[:n], refs[n:2 * n], refs[2 * n], refs[2 * n + 1]
        x, y, c = _place()
        sib = (x, y, 1 - c)
        chips = _other_chips(x, y)

        def copy(a, k, chip_id, layer, to):
            src = w_refs[a].at[layer] if k < 3 else out_refs[a].at[chip_id, layer]
            return pltpu.make_async_remote_copy(src_ref=src, dst_ref=out_refs[a].at[chip_id, layer],
                                                send_sem=send_sems.at[k * n + a], recv_sem=recv_sems.at[k * n + a],
                                                device_id=to, device_id_type=MESH)

        sends = [copy(a, j, 2 * x + y, c, (px, py, c)) for j, (px, py) in enumerate(chips) for a in range(n)]
        for cp in sends:
            cp.start()
        passed = []
        for j, (px, py) in enumerate(chips):
            for a in range(n):
                copy(a, j, 2 * px + py, c, (px, py, c)).wait_recv()
                fwd = copy(a, 3 + j, 2 * px + py, c, sib)
                fwd.start()
                passed.append(fwd)
        for j, (px, py) in enumerate(chips):
            for a in range(n):
                copy(a, 3 + j, 2 * px + py, 1 - c, sib).wait_recv()
        for cp in sends + passed:
            cp.wait_send()

    return _pc_comm(body, name=name, in_specs=[ANY] * n, out_specs=[ANY] * n,
                    out_shape=[jax.ShapeDtypeStruct((N_CHIPS,) + w.shape, w.dtype) for w in ws],
                    scratch_shapes=[pltpu.SemaphoreType.DMA((6 * n,)), pltpu.SemaphoreType.DMA((6 * n,))])(*ws)


def _send_other_layer_to_sibling(g0s, g1s, *, name):
    n = len(g0s)

    def body(*refs):
        g_refs = (refs[:n], refs[n:2 * n])
        out_refs, send_sems, recv_sems = refs[2 * n:3 * n], refs[3 * n], refs[3 * n + 1]
        x, y, c = _place()

        def copy(a, layer):
            return pltpu.make_async_remote_copy(src_ref=g_refs[layer][a], dst_ref=out_refs[a], send_sem=send_sems.at[a],
                                                recv_sem=recv_sems.at[a], device_id=(x, y, 1 - c), device_id_type=MESH)

        for layer in range(2):
            @pl.when(c == 1 - layer)
            def _(layer=layer):
                for a in range(n):
                    copy(a, layer).start()
        for a in range(n):
            copy(a, 0).wait()

    return _pc_comm(body, name=name, in_specs=[ANY] * (2 * n), out_specs=[ANY] * n,
                    out_shape=[jax.ShapeDtypeStruct(g.shape, g.dtype) for g in g0s],
                    scratch_shapes=[pltpu.SemaphoreType.DMA((n,)), pltpu.SemaphoreType.DMA((n,))])(*g0s, *g1s)


def _scatter_to_chips(ps, *, name):
    n = len(ps)

    def body(*refs):
        p_refs, rb_refs, send_sems, recv_sems = refs[:n], refs[n:2 * n], refs[2 * n], refs[2 * n + 1]
        x, y, c = _place()
        chips = _other_chips(x, y)
        sends = [pltpu.make_async_remote_copy(src_ref=p_refs[a].at[2 * px + py], dst_ref=rb_refs[a].at[j],
                                              send_sem=send_sems.at[j * n + a], recv_sem=recv_sems.at[j * n + a],
                                              device_id=(px, py, c), device_id_type=MESH)
                 for j, (px, py) in enumerate(chips) for a in range(n)]
        for cp in sends:
            cp.start()
        for cp in sends:
            cp.wait()

    return _pc_comm(body, name=name, in_specs=[ANY] * n, out_specs=[ANY] * n,
                    out_shape=[jax.ShapeDtypeStruct((3,) + p.shape[1:], p.dtype) for p in ps],
                    scratch_shapes=[pltpu.SemaphoreType.DMA((3 * n,)), pltpu.SemaphoreType.DMA((3 * n,))])(*ps)


def _swap_with_sibling(fs, *, name):
    n = len(fs)

    def body(*refs):
        f_refs, out_refs, send_sems, recv_sems = refs[:n], refs[n:2 * n], refs[2 * n], refs[2 * n + 1]
        x, y, c = _place()
        copies = [pltpu.make_async_remote_copy(src_ref=f_refs[a], dst_ref=out_refs[a], send_sem=send_sems.at[a],
                                               recv_sem=recv_sems.at[a], device_id=(x, y, 1 - c), device_id_type=MESH)
                  for a in range(n)]
        for cp in copies:
            cp.start()
        for cp in copies:
            cp.wait()

    return _pc_comm(body, name=name, in_specs=[ANY] * n, out_specs=[ANY] * n,
                    out_shape=[jax.ShapeDtypeStruct(f.shape, f.dtype) for f in fs],
                    scratch_shapes=[pltpu.SemaphoreType.DMA((n,)), pltpu.SemaphoreType.DMA((n,))])(*fs)


def _allgather_devices(v, *, name):
    R, C = v.shape

    def body(v_ref, out_ref, send_sems, recv_sems):
        x, y, c = _place()
        me = 4 * x + 2 * y + c
        out_ref[me] = v_ref[...]
        peers = []
        for k in range(1, 8):
            fx, fy, fc = (k >> 2) & 1, (k >> 1) & 1, k & 1
            px = 1 - x if fx else x
            py = 1 - y if fy else y
            pcc = 1 - c if fc else c
            peers.append((px, py, pcc))
        sends = []
        for k, peer in enumerate(peers):
            cp = pltpu.make_async_remote_copy(src_ref=v_ref, dst_ref=out_ref.at[me], send_sem=send_sems.at[k],
                                              recv_sem=recv_sems.at[k], device_id=peer, device_id_type=MESH)
            cp.start()
            sends.append(cp)
        for k, (px, py, pcc) in enumerate(peers):
            pltpu.make_async_remote_copy(src_ref=v_ref, dst_ref=out_ref.at[4 * px + 2 * py + pcc],
                                         send_sem=send_sems.at[k], recv_sem=recv_sems.at[k], device_id=peers[k],
                                         device_id_type=MESH).wait_recv()
        for cp in sends:
            cp.wait_send()

    vm = pl.BlockSpec(memory_space=pltpu.VMEM)
    return _pc_comm(body, name=name, in_specs=[vm], out_specs=vm, out_shape=jax.ShapeDtypeStruct((8, R, C), F32),
                    scratch_shapes=[pltpu.SemaphoreType.DMA((7,)), pltpu.SemaphoreType.DMA((7,))])(v)


D_MODEL = 1024
SC_W = D_MODEL // 4
GDN_W = D_MODEL // 2
SB_W = D_MODEL - SC_W - GDN_W
D_FF = 256 * ((8 * D_MODEL // 3 + 255) // 256)
O_SC, O_GQKV, O_GZ, O_GA, O_GB, O_SB = 0, 3 * SC_W, 3 * SC_W + 3 * GDN_W, 3 * SC_W + 4 * GDN_W, \
    3 * SC_W + 4 * GDN_W + GDN_HEADS, 3 * SC_W + 4 * GDN_W + 2 * GDN_HEADS
D_IN_PROJ = O_SB + 3 * SB_W
P_GQKV, P_SC, P_SB = 0, 3 * GDN_W, 3 * GDN_W + 3 * SC_W
P_GZ = P_SB + 3 * SB_W
P_GAB = P_GZ + GDN_W
P_PAD = 256
P_WIDTH = P_GAB + P_PAD


def _proj_to_kernel_layout(w):
    pad = jnp.zeros((w.shape[0], P_PAD - 2 * GDN_HEADS), w.dtype)
    return jnp.concatenate([w[:, O_GQKV:O_GZ], w[:, O_SC:O_GQKV], w[:, O_SB:], w[:, O_GZ:O_GA], w[:, O_GA:O_SB], pad],
                           axis=1)


def _proj_from_kernel_layout(g):
    return jnp.concatenate([g[:, P_SC:P_SB], g[:, P_GQKV:P_SC], g[:, P_GZ:P_GAB], g[:, P_GAB:P_GAB + 2 * GDN_HEADS],
                            g[:, P_SB:P_GZ]], axis=1)


def _mixout_to_kernel_layout(w):
    return jnp.concatenate([w[SC_W:SC_W + GDN_W], w[:SC_W], w[SC_W + GDN_W:]], axis=0)


def _pack_vec(parts, rows_to):
    flat = jnp.concatenate([p.reshape(-1) for p in parts])
    return jnp.pad(flat, (0, rows_to * LANES - flat.shape[0])).reshape(rows_to, LANES)


def _unpack_vec(mat, shapes):
    flat = mat.reshape(-1)
    out, r = [], 0
    for shp in shapes:
        n = int(np.prod(shp))
        out.append(flat[r:r + n].reshape(shp))
        r += n
    return out


def _round_up(n, m):
    return (n + m - 1) // m * m


def _layer_fwd(x, p, l):
    L = x.shape[0]
    tag = "l%d_" % l
    h = _rmsnorm_fwd(x, p["wn_mix"], name=tag + "norm_mix")
    proj = _matmul(h, p["w_in"], "nn", tm=512, tn=768, tk=D_MODEL, name=tag + "proj")
    (y_sc,) = _conv_pointwise_fwd([(proj, P_SC + SC_W), (proj, P_SC + 2 * SC_W)], [(p["w_sconv"], 0)], [(proj, P_SC)],
                                  _pre_product, _post_gate_mul, [(SC_W, BF16)], tc=SC_W, tm=512, name=tag + "sconv")
    (qkv,) = _conv_pointwise_fwd([(proj, P_GQKV)], [(p["w_gdn_conv"], 0)], [], _pre_identity, _post_silu,
                                 [(3 * GDN_W, F32)], tc=GDN_W, tm=512, name=tag + "gdn_conv")
    qe, ke, u, w, attn, eg = _gdn_prep_fwd(qkv, proj, P_GAB, p["a_log"], p["dt_bias"], name=tag + "gdn_prep")
    y_gdn, states = _gdn_scan_fwd(qe, ke, u, w, attn, eg, proj, P_GZ, p["wgn"], name=tag + "gdn_scan")
    sb_scale = (SB_W // SB_HEADS) ** -0.5
    sbqkv = jnp.concatenate([proj[:, P_SB:P_SB + SB_W] * sb_scale, proj[:, P_SB + SB_W:P_SB + 3 * SB_W]],
                            axis=1).astype(BF16)
    y_sb, ctot = _sb_fwd(sbqkv, name=tag + "sb_fwd")
    y_cat = [y_gdn, y_sc, y_sb]
    x2 = _matmul_rows_parts(y_cat, p["w_out"], "nn", res=x, name=tag + "mix_out")
    h2 = _rmsnorm_fwd(x2, p["wn_ffn"], name=tag + "norm_ffn")
    up_g = _matmul(h2, p["w_up_g"], "nn", tm=512, tn=D_FF // 2, tk=D_MODEL, name=tag + "up_gate")
    up_v = _matmul(h2, p["w_up_v"], "nn", tm=512, tn=D_FF // 2, tk=D_MODEL, name=tag + "up_val")
    (act,) = _conv_pointwise_fwd([(up_g, 0), (up_v, 0)], [(p["w_fconv_g"], 0), (p["w_fconv_v"], 0)], [],
                                 _pre_identity, _post_swiglu, [(D_FF, BF16)], tc=256, tm=512, name=tag + "ffn_act")
    x3 = _matmul(act, p["w_down"], "nn", tm=512, tn=D_MODEL, tk=D_FF // 2, res=x2, name=tag + "ffn_down")
    saved = dict(x=x, h=h, proj=proj, qkv=qkv, qe=qe, ke=ke, u=u, w=w, attn=attn, eg=eg, states=states,
                 sbqkv=sbqkv, ctot=ctot, y_cat=y_cat, x2=x2, h2=h2, up_g=up_g, up_v=up_v, act=act)
    return x3, saved


def _layer_bwd(dx3, p, s, l):
    L = dx3.shape[0]
    tag = "l%d_b_" % l
    g = {}
    dact = _matmul(dx3, p["w_down"], "nt", tm=512, tn=D_FF // 2, tk=D_MODEL, name=tag + "dact")
    g["w_down"] = _matmul(s["act"], dx3, "tn", tm=D_FF // 2, tn=D_MODEL, tk=512, name=tag + "dw_down")
    (dup_g, dup_v), _, (g["w_fconv_g"], g["w_fconv_v"]) = _conv_pointwise_bwd(
        [(s["up_g"], 0), (s["up_v"], 0)], [(p["w_fconv_g"], 0), (p["w_fconv_v"], 0)], [], [(dact, 0)],
        _pre_identity, _post_swiglu, D_FF, tc=256, tm=512, name=tag + "ffn_act")
    dh2 = _matmul(dup_g, p["w_up_g"], "nt", tm=512, tn=D_MODEL, tk=D_FF // 2, name=tag + "dh2_gate")
    dh2 = _matmul(dup_v, p["w_up_v"], "nt", tm=512, tn=D_MODEL, tk=D_FF // 2, res=dh2, name=tag + "dh2_val")
    g["w_up"] = _matmul(s["h2"], dup_g, "tn", tm=D_MODEL, tn=D_FF // 2, tk=512, slabs=(N_CHIPS, 0, None),
                        name=tag + "dw_up_gate")
    g["w_up"] = _matmul(s["h2"], dup_v, "tn", tm=D_MODEL, tn=D_FF // 2, tk=512, slabs=(N_CHIPS, 2, g["w_up"]),
                        name=tag + "dw_up_val")
    dx2, g["wn_ffn"] = _rmsnorm_bwd(dh2, s["x2"], p["wn_ffn"], dx3, name=tag + "norm_ffn")
    dycat = _matmul(dx2, p["w_out"], "nt", tm=512, tn=D_MODEL, tk=D_MODEL, name=tag + "dycat")
    y_gdn, y_sc, y_sb = s["y_cat"]
    g["w_out"] = _matmul_tn_parts([y_sc, y_gdn, y_sb], dx2, name=tag + "dw_out")
    sb_scale = (SB_W // SB_HEADS) ** -0.5
    dsq, dsk, dsv = _sb_bwd(s["sbqkv"], s["ctot"], dycat, GDN_W + SC_W, sb_scale, name=tag + "sb_bwd")
    dqe, dke, du, dw, dattn, deg, dgz, g["wgn"] = _gdn_scan_bwd(
        s["qe"], s["ke"], s["u"], s["w"], s["attn"], s["eg"], s["proj"], P_GZ, p["wgn"], s["states"], dycat, 0,
        name=tag + "gdn_scan")
    dqkv_act, dgab, g["a_log"], g["dt_bias"] = _gdn_prep_bwd(
        s["qkv"], s["proj"], P_GAB, p["a_log"], p["dt_bias"], dqe, dke, du, dw, dattn, deg, P_PAD,
        name=tag + "gdn_prep")
    (dqkv,), _, (g["w_gdn_conv"],) = _conv_pointwise_bwd(
        [(s["proj"], P_GQKV)], [(p["w_gdn_conv"], 0)], [], [(dqkv_act, 0)], _pre_identity, _post_silu, 3 * GDN_W,
        tc=GDN_W, tm=512, name=tag + "gdn_conv")
    (dsc_c, dsc_h), (dsc_b,), (g["w_sconv"],) = _conv_pointwise_bwd(
        [(s["proj"], P_SC + SC_W), (s["proj"], P_SC + 2 * SC_W)], [(p["w_sconv"], 0)], [(s["proj"], P_SC)],
        [(dycat, GDN_W)], _pre_product, _post_gate_mul, SC_W, tc=SC_W, tm=512, name=tag + "sconv")
    dproj = [dqkv, dsc_b, dsc_c, dsc_h, dsq, dsk, dsv, dgz, dgab]
    dh = _matmul_rows_parts(dproj, p["w_in"], "nt", name=tag + "dh")
    g["w_in"] = jnp.concatenate([_matmul_tn_parts(s["h"], dproj[:4], name=tag + "dw_in_a"),
                                 _matmul_tn_parts(s["h"], dproj[4:], name=tag + "dw_in_b")], axis=1)
    dx, g["wn_mix"] = _rmsnorm_bwd(dh, s["x"], p["wn_mix"], dx2, name=tag + "norm_mix")
    return dx, g


BIG = ("w_mix_in", "w_mix_out", "w_ffn_up", "w_ffn_down")
BIG_AXIS = {"w_mix_in": 2, "w_mix_out": 1, "w_ffn_up": 2, "w_ffn_down": 1}
SMALL_SHARDED = ("w_sconv", "w_gdn_conv", "w_ffn_conv")
SMALL_REPLICATED = ("w_norm_mix", "gdn_a_log", "gdn_dt_bias", "w_gdn_norm", "w_norm_ffn", "w_norm_final")
WEIGHTS = ("w_norm_mix", "w_mix_in", "w_sconv", "w_gdn_conv", "gdn_a_log", "gdn_dt_bias", "w_gdn_norm", "w_mix_out",
           "w_norm_ffn", "w_ffn_up", "w_ffn_conv", "w_ffn_down", "w_norm_final")


def kernel(x, w_norm_mix, w_mix_in, w_sconv, w_gdn_conv, gdn_a_log, gdn_dt_bias, w_gdn_norm, w_mix_out, w_norm_ffn, w_ffn_up, w_ffn_conv, w_ffn_down, w_norm_final, loss_target, m_w_norm_mix, m_w_mix_in, m_w_sconv, m_w_gdn_conv, m_gdn_a_log, m_gdn_dt_bias, m_w_gdn_norm, m_w_mix_out, m_w_norm_ffn, m_w_ffn_up, m_w_ffn_conv, m_w_ffn_down, m_w_norm_final, v_w_norm_mix, v_w_mix_in, v_w_sconv, v_w_gdn_conv, v_gdn_a_log, v_gdn_dt_bias, v_w_gdn_norm, v_w_mix_out, v_w_norm_ffn, v_w_ffn_up, v_w_ffn_conv, v_w_ffn_down, v_w_norm_final):
    W = dict(w_norm_mix=w_norm_mix, w_mix_in=w_mix_in, w_sconv=w_sconv, w_gdn_conv=w_gdn_conv, gdn_a_log=gdn_a_log,
             gdn_dt_bias=gdn_dt_bias, w_gdn_norm=w_gdn_norm, w_mix_out=w_mix_out, w_norm_ffn=w_norm_ffn,
             w_ffn_up=w_ffn_up, w_ffn_conv=w_ffn_conv, w_ffn_down=w_ffn_down, w_norm_final=w_norm_final)
    M = dict(w_norm_mix=m_w_norm_mix, w_mix_in=m_w_mix_in, w_sconv=m_w_sconv, w_gdn_conv=m_w_gdn_conv,
             gdn_a_log=m_gdn_a_log, gdn_dt_bias=m_gdn_dt_bias, w_gdn_norm=m_w_gdn_norm, w_mix_out=m_w_mix_out,
             w_norm_ffn=m_w_norm_ffn, w_ffn_up=m_w_ffn_up, w_ffn_conv=m_w_ffn_conv, w_ffn_down=m_w_ffn_down,
             w_norm_final=m_w_norm_final)
    V = dict(w_norm_mix=v_w_norm_mix, w_mix_in=v_w_mix_in, w_sconv=v_w_sconv, w_gdn_conv=v_w_gdn_conv,
             gdn_a_log=v_gdn_a_log, gdn_dt_bias=v_gdn_dt_bias, w_gdn_norm=v_w_gdn_norm, w_mix_out=v_w_mix_out,
             w_norm_ffn=v_w_norm_ffn, w_ffn_up=v_w_ffn_up, w_ffn_conv=v_w_ffn_conv, w_ffn_down=v_w_ffn_down,
             w_norm_final=v_w_norm_final)
    depth = w_mix_in.shape[0]
    L = x.shape[1]
    mx, my, mc = lax.axis_index("x"), lax.axis_index("y"), lax.axis_index("c")
    chip = 2 * mx + my

    assert depth == 2
    own = [W[n].astype(BF16) for n in BIG]
    gathered = _allgather_chips(own, name="gather_big")
    gathered = [lax.dynamic_update_slice(g, o[None], (chip, 0, 0, 0)) for g, o in zip(gathered, own)]
    full_big = [{n: jnp.concatenate([g[b, l] for b in range(N_CHIPS)], axis=BIG_AXIS[n] - 1)
                 for n, g in zip(BIG, gathered)} for l in range(depth)]

    small_sh_shapes = [W[n].shape for n in SMALL_SHARDED]
    n_small_sh = sum(int(np.prod(s)) for s in small_sh_shapes)
    small_rows = _round_up(n_small_sh, 8 * LANES) // LANES
    small_all = _allgather_devices(_pack_vec([W[n] for n in SMALL_SHARDED], small_rows), name="gather_small")
    small_chip = [_unpack_vec(small_all[2 * b], small_sh_shapes) for b in range(N_CHIPS)]
    full_small = {n: jnp.concatenate([small_chip[b][i] for b in range(N_CHIPS)], axis=2)
                  for i, n in enumerate(SMALL_SHARDED)}

    params = []
    for l in range(depth):
        w_up = full_big[l]["w_ffn_up"]
        fconv = full_small["w_ffn_conv"][l]
        params.append(dict(
            wn_mix=w_norm_mix[l], w_in=_proj_to_kernel_layout(full_big[l]["w_mix_in"]),
            w_sconv=full_small["w_sconv"][l], w_gdn_conv=full_small["w_gdn_conv"][l],
            a_log=gdn_a_log[l].reshape(GDN_HEADS, 1, 1), dt_bias=gdn_dt_bias[l].reshape(GDN_HEADS, 1, 1),
            wgn=w_gdn_norm[l], w_out=_mixout_to_kernel_layout(full_big[l]["w_mix_out"]), wn_ffn=w_norm_ffn[l],
            w_up_g=w_up[:, :D_FF], w_up_v=w_up[:, D_FF:], w_fconv_g=fconv[:, :D_FF], w_fconv_v=fconv[:, D_FF:],
            w_down=full_big[l]["w_ffn_down"]))

    xs = x[0]
    saved = []
    for l in range(depth):
        xs, s = _layer_fwd(xs, params[l], l)
        saved.append(s)
    loss_row, dx, g_norm_final = _final_loss(xs, w_norm_final, loss_target[0], name="final_loss")
    grads = [None] * depth
    for l in reversed(range(depth)):
        dx, grads[l] = _layer_bwd(dx, params[l], saved[l], l)
    loss = lax.psum(loss_row[0, 0], ("x", "y", "c"))

    G = {
        "w_sconv": jnp.stack([grads[l]["w_sconv"] for l in range(depth)]),
        "w_gdn_conv": jnp.stack([grads[l]["w_gdn_conv"] for l in range(depth)]),
        "w_ffn_conv": jnp.stack([jnp.concatenate([grads[l]["w_fconv_g"], grads[l]["w_fconv_v"]], axis=1)
                                 for l in range(depth)]),
        "w_norm_mix": jnp.stack([grads[l]["wn_mix"].reshape(-1) for l in range(depth)]),
        "gdn_a_log": jnp.stack([grads[l]["a_log"].reshape(-1) for l in range(depth)]),
        "gdn_dt_bias": jnp.stack([grads[l]["dt_bias"].reshape(-1) for l in range(depth)]),
        "w_gdn_norm": jnp.stack([grads[l]["wgn"].reshape(-1) for l in range(depth)]),
        "w_norm_ffn": jnp.stack([grads[l]["wn_ffn"].reshape(-1) for l in range(depth)]),
        "w_norm_final": g_norm_final.reshape(-1),
    }

    def by_shard(l):
        g_in = _proj_from_kernel_layout(grads[l]["w_in"])
        g_in = g_in.reshape(D_MODEL, N_CHIPS, -1).transpose(1, 0, 2)
        return [g_in, grads[l]["w_out"].reshape(N_CHIPS, -1, D_MODEL), grads[l]["w_up"],
                grads[l]["w_down"].reshape(N_CHIPS, -1, D_MODEL)]

    g_layers = [by_shard(l) for l in range(depth)]
    from_sibling = _send_other_layer_to_sibling(g_layers[0], g_layers[1], name="rs_sibling")
    chip_sums = [_add_layers(g0, g1, ra, mc, name="rs_add_layers_" + n)
                 for n, g0, g1, ra in zip(BIG, g_layers[0], g_layers[1], from_sibling)]
    from_chips = _scatter_to_chips(chip_sums, name="rs_chips")
    mine = [_add_chips(p, rb, chip, name="rs_add_chips_" + n) for n, p, rb in zip(BIG, chip_sums, from_chips)]
    other = _swap_with_sibling(mine, name="rs_result")
    out_g, out_d, out_m, out_v = {}, {}, {}, {}
    for n, g_mine, g_other in zip(BIG, mine, other):
        out_g[n], out_d[n], out_m[n], out_v[n] = _adamw_layers(W[n], g_mine, g_other, M[n], V[n], mc,
                                                               name="adamw_" + n)

    small_names = SMALL_SHARDED + SMALL_REPLICATED
    small_full_shapes = [G[n].shape for n in small_names]
    n_small = sum(int(np.prod(s)) for s in small_full_shapes)
    red_rows = _round_up(n_small, 8 * LANES) // LANES
    partials = _allgather_devices(_pack_vec([G[n] for n in small_names], red_rows), name="reduce_small")
    summed = _unpack_vec(_sum_slots(partials, name="reduce_small_sum"), small_full_shapes)
    g_small = {}
    for n, a in zip(small_names, summed):
        if n in SMALL_SHARDED:
            width = a.shape[2] // N_CHIPS
            a = lax.dynamic_slice_in_dim(a, chip * width, width, axis=2)
        g_small[n] = a
    own_shapes = [W[n].shape for n in small_names]
    n_own = sum(int(np.prod(s)) for s in own_shapes)
    own_rows = _round_up(n_own, 8 * LANES) // LANES
    packed = [_pack_vec([src[n] for n in small_names], own_rows) for src in (W, g_small, M, V)]
    d_s, nm_s, nv_s = _adamw(*packed, name="adamw_small", tm=own_rows)
    for mat, dst in ((packed[1], out_g), (d_s, out_d), (nm_s, out_m), (nv_s, out_v)):
        for n, a in zip(small_names, _unpack_vec(mat, own_shapes)):
            dst[n] = a

    outs = [loss, dx[None]]
    for dst in (out_g, out_d, out_m, out_v):
        outs += [dst[n] for n in WEIGHTS]
    return tuple(outs)
```

```python
import functools

import jax
import jax.numpy as jnp
import numpy as np
from jax import lax
from jax.experimental import pallas as pl
from jax.experimental.pallas import tpu as pltpu

F32 = jnp.float32
BF16 = jnp.bfloat16
MESH = pl.DeviceIdType.MESH

NORM_EPS = 1e-6
GDN_HEADS = 4
GDN_CHUNK = 64
GDN_CONV = 4
GDN_SCAN_CHUNKS = 4
SB_HEADS = 4
SB_DEAD_LOG = -110.0
SC_KERNEL = 3
FFN_CONV = 3
ADAM_LR = 0.001
ADAM_B1 = 0.9
ADAM_B2 = 0.999
ADAM_EPS = 1e-08
ADAM_WD = 0.01
ADAM_STEP = 10

VMEM_LIMIT_BYTES = 48 * 1024 * 1024
HALO = 8
CONV_BWD_SLAB = 128
LANES = 128
N_CHIPS = 4


def _pc(body, *, name, grid, in_specs, out_specs, out_shape, scratch_shapes=(), dims=None, aliases=None):
    params = dict(vmem_limit_bytes=VMEM_LIMIT_BYTES)
    if dims is not None:
        params["dimension_semantics"] = dims
    return pl.pallas_call(body, name=name, grid=grid, in_specs=in_specs, out_specs=out_specs, out_shape=out_shape,
                          scratch_shapes=list(scratch_shapes), input_output_aliases=aliases or {},
                          compiler_params=pltpu.CompilerParams(**params))


def _pc_prefetch(body, *, name, grid_spec, out_shape, dims):
    return pl.pallas_call(body, name=name, grid_spec=grid_spec, out_shape=out_shape,
                          compiler_params=pltpu.CompilerParams(vmem_limit_bytes=VMEM_LIMIT_BYTES,
                                                               dimension_semantics=dims))


def _pc_comm(body, *, name, in_specs, out_specs, out_shape, scratch_shapes):
    return pl.pallas_call(body, name=name, in_specs=in_specs, out_specs=out_specs, out_shape=out_shape,
                          scratch_shapes=list(scratch_shapes),
                          compiler_params=pltpu.CompilerParams(vmem_limit_bytes=VMEM_LIMIT_BYTES))


_DIMS = {"nn": (((1,), (0,)), ((), ())), "nt": (((1,), (1,)), ((), ())), "tn": (((0,), (0,)), ((), ()))}


def _matmul(a, b, mode, *, name, tm=512, tn=512, tk=512, out_dtype=F32, res=None, slabs=None):
    if mode == "nn":
        (M, K), (K2, N) = a.shape, b.shape
    elif mode == "nt":
        (M, K), (N, K2) = a.shape, b.shape
    else:
        (K, M), (K2, N) = a.shape, b.shape
    assert K == K2, (a.shape, b.shape, mode)
    tm, tn, tk = min(tm, M), min(tn, N), min(tk, K)
    assert M % tm == 0 and N % tn == 0 and K % tk == 0, (M, N, K, tm, tn, tk)
    nk = K // tk
    if mode == "tn":
        a_spec = pl.BlockSpec((tk, tm), lambda i, j, k: (k, i))
    else:
        a_spec = pl.BlockSpec((tm, tk), lambda i, j, k: (i, k))
    if mode == "nt":
        b_spec = pl.BlockSpec((tn, tk), lambda i, j, k: (j, k))
    else:
        b_spec = pl.BlockSpec((tk, tn), lambda i, j, k: (k, j))
    o_spec = pl.BlockSpec((tm, tn), lambda i, j, k: (i, j))
    has_res = res is not None
    dn = _DIMS[mode]

    def body(*refs):
        if has_res:
            a_ref, b_ref, r_ref, o_ref, acc = refs
        else:
            a_ref, b_ref, o_ref, acc = refs
        k = pl.program_id(2)
        p = lax.dot_general(a_ref[...].astype(BF16), b_ref[...].astype(BF16), dn, preferred_element_type=F32)

        def finish(total):
            if has_res:
                total = total + r_ref[...].astype(F32)
            o_ref[...] = total.astype(out_dtype)

        if nk == 1:
            finish(p)
        else:
            @pl.when(k == 0)
            def _():
                acc[...] = p

            @pl.when(k > 0)
            def _():
                acc[...] += p

            @pl.when(k == nk - 1)
            def _():
                finish(acc[...])

    in_specs = [a_spec, b_spec] + ([o_spec] if has_res else [])
    args = (a, b) + ((res,) if has_res else ())
    out_shape = jax.ShapeDtypeStruct((M, N), out_dtype)
    aliases = None
    if slabs is not None:
        n_slabs, first, into = slabs
        assert not has_res and tm == M
        o_spec = pl.BlockSpec((None, tm, tn), lambda i, j, k: (j + first, i, 0))
        out_shape = jax.ShapeDtypeStruct((n_slabs, M, tn), out_dtype)
        if into is not None:
            in_specs.append(pl.BlockSpec(memory_space=pl.ANY))
            args = args + (into,)
            aliases = {2: 0}
            inner = body

            def body(a_ref, b_ref, into_ref, o_ref, acc):
                inner(a_ref, b_ref, o_ref, acc)
    return _pc(body, name=name, grid=(M // tm, N // tn, nk), in_specs=in_specs, out_specs=o_spec,
               out_shape=out_shape, scratch_shapes=[pltpu.VMEM((tm, tn), F32)],
               dims=("parallel", "parallel", "arbitrary"), aliases=aliases)(*args)


def _offsets(parts, own_width_aligned):
    offs, at = [], 0
    for p in parts:
        assert at % (p.shape[1] if own_width_aligned else LANES) == 0, (at, p.shape)
        offs.append(at)
        at += p.shape[1]
    return offs, at


def _matmul_rows_parts(parts, w, mode, *, name, tm=512, res=None):
    M = parts[0].shape[0]
    offs, K = _offsets(parts, True)
    tm = min(tm, M)
    N = w.shape[1] if mode == "nn" else w.shape[0]
    assert (w.shape[0] if mode == "nn" else w.shape[1]) == K
    has_res = res is not None
    n = len(parts)

    def body(*refs):
        o_ref = refs[-1]
        total = None
        for s in range(n):
            p = lax.dot_general(refs[s][...].astype(BF16), refs[n + s][...].astype(BF16), _DIMS[mode],
                                preferred_element_type=F32)
            total = p if total is None else total + p
        if has_res:
            total = total + refs[2 * n][...]
        o_ref[...] = total

    in_specs = [pl.BlockSpec((tm, p.shape[1]), lambda i: (i, 0)) for p in parts]
    for p, off in zip(parts, offs):
        blk = off // p.shape[1]
        if mode == "nn":
            in_specs.append(pl.BlockSpec((p.shape[1], N), lambda i, blk=blk: (blk, 0)))
        else:
            in_specs.append(pl.BlockSpec((N, p.shape[1]), lambda i, blk=blk: (0, blk)))
    o_spec = pl.BlockSpec((tm, N), lambda i: (i, 0))
    args = tuple(parts) + (w,) * n + ((res,) if has_res else ())
    return _pc(body, name=name, grid=(M // tm,), in_specs=in_specs + ([o_spec] if has_res else []), out_specs=o_spec,
               out_shape=jax.ShapeDtypeStruct((M, N), F32), dims=("parallel",))(*args)


def _matmul_tn_parts(a, b, *, name, tk=512):
    a_parts = list(a) if isinstance(a, (list, tuple)) else [a]
    b_parts = list(b) if isinstance(b, (list, tuple)) else [b]
    assert len(a_parts) == 1 or len(b_parts) == 1
    a_offs, M = _offsets(a_parts, False)
    b_offs, N = _offsets(b_parts, False)
    K = a_parts[0].shape[0]
    tk = min(tk, K)
    na, nb = len(a_parts), len(b_parts)

    def body(*refs):
        o_ref = refs[-1]
        first = pl.program_id(0) == 0
        for s in range(na):
            for t in range(nb):
                p = lax.dot_general(refs[s][...].astype(BF16), refs[na + t][...].astype(BF16), _DIMS["tn"],
                                    preferred_element_type=F32)
                rows = slice(a_offs[s], a_offs[s] + a_parts[s].shape[1])
                cols = slice(b_offs[t], b_offs[t] + b_parts[t].shape[1])

                @pl.when(first)
                def _(p=p, rows=rows, cols=cols):
                    o_ref[rows, cols] = p

                @pl.when(jnp.logical_not(first))
                def _(p=p, rows=rows, cols=cols):
                    o_ref[rows, cols] += p

    in_specs = [pl.BlockSpec((tk, p.shape[1]), lambda k: (k, 0)) for p in a_parts + b_parts]
    return _pc(body, name=name, grid=(K // tk,), in_specs=in_specs, out_specs=pl.BlockSpec((M, N), lambda k: (0, 0)),
               out_shape=jax.ShapeDtypeStruct((M, N), F32), dims=("arbitrary",))(*a_parts, *b_parts)


def _rmsnorm_fwd(x, w, *, name, tm=512):
    L, D = x.shape
    tm = min(tm, L)

    def body(x_ref, w_ref, h_ref):
        xv = x_ref[...]
        r = lax.rsqrt(jnp.mean(xv * xv, axis=-1, keepdims=True) + NORM_EPS)
        h_ref[...] = (xv * r * w_ref[...]).astype(BF16)

    return _pc(body, name=name, grid=(L // tm,),
               in_specs=[pl.BlockSpec((tm, D), lambda i: (i, 0)), pl.BlockSpec((1, D), lambda i: (0, 0))],
               out_specs=pl.BlockSpec((tm, D), lambda i: (i, 0)), out_shape=jax.ShapeDtypeStruct((L, D), BF16),
               dims=("parallel",))(x, w.reshape(1, D))


def _rmsnorm_bwd(dh, x, w, dres, *, name, tm=512):
    L, D = x.shape
    tm = min(tm, L)

    def body(dh_ref, x_ref, w_ref, dres_ref, dx_ref, dw_ref):
        xv = x_ref[...]
        r = lax.rsqrt(jnp.mean(xv * xv, axis=-1, keepdims=True) + NORM_EPS)
        xhat = xv * r
        dhv = dh_ref[...]
        g = dhv * w_ref[...]
        dx_ref[...] = dres_ref[...] + r * (g - xhat * jnp.mean(g * xhat, axis=-1, keepdims=True))
        part = jnp.sum(dhv * xhat, axis=0, keepdims=True)

        @pl.when(pl.program_id(0) == 0)
        def _():
            dw_ref[...] = part

        @pl.when(pl.program_id(0) > 0)
        def _():
            dw_ref[...] += part

    row = pl.BlockSpec((tm, D), lambda i: (i, 0))
    vec = pl.BlockSpec((1, D), lambda i: (0, 0))
    return _pc(body, name=name, grid=(L // tm,), in_specs=[row, row, vec, row], out_specs=[row, vec],
               out_shape=[jax.ShapeDtypeStruct((L, D), F32), jax.ShapeDtypeStruct((1, D), F32)],
               dims=("arbitrary",))(dh, x, w.reshape(1, D), dres)


def _final_loss(x, w, tgt, *, name, tm=512):
    L, D = x.shape
    tm = min(tm, L)

    def body(x_ref, w_ref, t_ref, loss_ref, dx_ref, dw_ref):
        xv = x_ref[...]
        r = lax.rsqrt(jnp.mean(xv * xv, axis=-1, keepdims=True) + NORM_EPS)
        xhat = xv * r
        e = xhat * w_ref[...] - t_ref[...]
        lpart = jnp.broadcast_to(0.5 * jnp.sum(jnp.mean(e * e, axis=-1, keepdims=True), axis=0, keepdims=True),
                                 (1, LANES))
        dy = e * (1.0 / D)
        g = dy * w_ref[...]
        dx_ref[...] = r * (g - xhat * jnp.mean(g * xhat, axis=-1, keepdims=True))
        part = jnp.sum(dy * xhat, axis=0, keepdims=True)

        @pl.when(pl.program_id(0) == 0)
        def _():
            dw_ref[...] = part
            loss_ref[...] = lpart

        @pl.when(pl.program_id(0) > 0)
        def _():
            dw_ref[...] += part
            loss_ref[...] += lpart

    row = pl.BlockSpec((tm, D), lambda i: (i, 0))
    vec = pl.BlockSpec((1, D), lambda i: (0, 0))
    lsp = pl.BlockSpec((1, LANES), lambda i: (0, 0))
    return _pc(body, name=name, grid=(L // tm,), in_specs=[row, vec, row], out_specs=[lsp, row, vec],
               out_shape=[jax.ShapeDtypeStruct((1, LANES), F32), jax.ShapeDtypeStruct((L, D), F32),
                          jax.ShapeDtypeStruct((1, D), F32)],
               dims=("arbitrary",))(x, w.reshape(1, D), tgt)


def _shift_down(x, prev, k):
    if k == 0:
        return x
    r = pltpu.roll(x, k, 0)
    p = pltpu.roll(prev, k, 0)
    row = lax.broadcasted_iota(jnp.int32, p.shape, 0)
    head = jnp.where(row < k, p, r[:HALO])
    return jnp.concatenate([head, r[HALO:]], axis=0)


def _shift_up(x, j):
    if j == 0:
        return x
    return pltpu.roll(x, x.shape[0] - j, 0)


def _silu(x):
    return x * jax.nn.sigmoid(x)


def _conv_taps(p, p_prev, w):
    K = w.shape[0]
    out = None
    for k in range(K):
        term = w[k:k + 1, :] * _shift_down(p, p_prev, K - 1 - k)
        out = term if out is None else out + term
    return out


def _conv_pointwise_fwd(xs, ws, es, pre, post, outs, *, tc, tm, name):
    L = xs[0][0].shape[0]
    tm = min(tm, L)
    ncol = outs[0][0] // tc
    nrow = L // tm
    hb = tm // HALO
    nx, nw, ne, no = len(xs), len(ws), len(es), len(outs)
    K = ws[0][0].shape[0]

    def body(*refs):
        xc = [refs[2 * n][...] for n in range(nx)]
        i = pl.program_id(1)
        first = (i > 0).astype(F32)
        xp = [refs[2 * n + 1][...] * first for n in range(nx)]
        wv = [refs[2 * nx + n][...] for n in range(nw)]
        ev = [refs[2 * nx + nw + n][...] for n in range(ne)]
        o_refs = refs[2 * nx + nw + ne:]
        ps, pps = pre(*xc), pre(*xp)
        us = [_conv_taps(p, pp, w) for p, pp, w in zip(ps, pps, wv)]
        for o_ref, val in zip(o_refs, post(us, ev)):
            o_ref[...] = val.astype(o_ref.dtype)

    in_specs, args = [], []
    for arr, c0 in xs:
        off = c0 // tc
        in_specs.append(pl.BlockSpec((tm, tc), lambda j, i, off=off: (i, j + off)))
        in_specs.append(pl.BlockSpec((HALO, tc), lambda j, i, off=off: (jnp.maximum(i * hb - 1, 0), j + off)))
        args += [arr, arr]
    for arr, c0 in ws:
        off = c0 // tc
        in_specs.append(pl.BlockSpec((K, tc), lambda j, i, off=off: (0, j + off)))
        args.append(arr)
    for arr, c0 in es:
        off = c0 // tc
        in_specs.append(pl.BlockSpec((tm, tc), lambda j, i, off=off: (i, j + off)))
        args.append(arr)
    out_specs = [pl.BlockSpec((tm, tc), lambda j, i: (i, j)) for _ in range(no)]
    out_shape = [jax.ShapeDtypeStruct((L, c), dt) for c, dt in outs]
    return _pc(body, name=name, grid=(ncol, nrow), in_specs=in_specs, out_specs=out_specs, out_shape=out_shape,
               dims=("parallel", "parallel"))(*args)


def _conv_pointwise_bwd(xs, ws, es, dys, pre, post, width, *, tc, tm, name, out_dtype=BF16):
    L = xs[0][0].shape[0]
    tm = min(tm, L)
    ncol = width // tc
    nrow = L // tm
    hb = tm // HALO
    nx, nw, ne, ny = len(xs), len(ws), len(es), len(dys)
    K = ws[0][0].shape[0]

    slab = min(CONV_BWD_SLAB, tm)
    win = slab + 2 * HALO
    assert tm % slab == 0 and K - 1 <= HALO

    def body(*refs):
        i = pl.program_id(1)
        first = (i > 0).astype(F32)
        more = (i < nrow - 1).astype(F32)
        n_in = 3 * nx + nw + 2 * ne + 2 * ny
        n_out = nx + ne + nw
        dx_refs = refs[n_in:n_in + nx]
        de_refs = refs[n_in + nx:n_in + nx + ne]
        dw_refs = refs[n_in + nx + ne:n_in + n_out]
        pads = refs[n_in + n_out:]
        x_pads, e_pads, dy_pads = pads[:nx], pads[nx:nx + ne], pads[nx + ne:]
        pos = 0
        for n in range(nx):
            x_pads[n][0:HALO, :] = refs[pos + 1][...] * first
            x_pads[n][HALO:HALO + tm, :] = refs[pos][...]
            x_pads[n][HALO + tm:, :] = refs[pos + 2][...]
            pos += 3
        wv = [refs[pos + n][...] for n in range(nw)]
        pos += nw
        for n in range(ne):
            e_pads[n][0:HALO, :] = jnp.zeros((HALO, tc), F32)
            e_pads[n][HALO:HALO + tm, :] = refs[pos][...]
            e_pads[n][HALO + tm:, :] = refs[pos + 1][...]
            pos += 2
        for n in range(ny):
            dy_pads[n][0:HALO, :] = jnp.zeros((HALO, tc), F32)
            dy_pads[n][HALO:HALO + tm, :] = refs[pos][...].astype(F32)
            dy_pads[n][HALO + tm:, :] = refs[pos + 1][...].astype(F32) * more
            pos += 2

        def one_slab(t, dw_acc):
            r0 = pl.multiple_of(t * slab, HALO)
            xw = [x_pads[n][pl.ds(r0, win), :] for n in range(nx)]
            ew = [e_pads[n][pl.ds(r0, win), :] for n in range(ne)]
            dyw = [dy_pads[n][pl.ds(r0, win), :] for n in range(ny)]
            ps, pre_vjp = jax.vjp(lambda *x_: pre(*x_), *xw)
            shifted = [[p if k == K - 1 else pltpu.roll(p, K - 1 - k, 0) for k in range(K)] for p in ps]
            us = []
            for n in range(nw):
                u = None
                for k in range(K):
                    term = wv[n][k:k + 1, :] * shifted[n][k]
                    u = term if u is None else u + term
                us.append(u)
            _, post_vjp = jax.vjp(lambda u_, e_: post(u_, e_), us, ew)
            dus, des = post_vjp(dyw)
            dps, dw_new = [], []
            for n in range(nw):
                dp = None
                for k in range(K):
                    term = wv[n][k:k + 1, :] * (dus[n] if k == K - 1 else pltpu.roll(dus[n], win - (K - 1 - k), 0))
                    dp = term if dp is None else dp + term
                dps.append(dp)
                inner = dus[n][HALO:HALO + slab]
                dw_new.append([dw_acc[n][k] + jnp.sum(inner * shifted[n][k][HALO:HALO + slab], axis=0, keepdims=True)
                               for k in range(K)])
            dxs = pre_vjp(dps)
            rows = pl.ds(r0, slab)
            for r, v in zip(dx_refs, dxs):
                r[rows, :] = v[HALO:HALO + slab].astype(out_dtype)
            for r, v in zip(de_refs, des):
                r[rows, :] = v[HALO:HALO + slab].astype(out_dtype)
            return dw_new

        zero = [[jnp.zeros((1, tc), F32) for _ in range(K)] for _ in range(nw)]
        dw_tile = lax.fori_loop(0, tm // slab, one_slab, zero)
        for n in range(nw):
            for k in range(K):
                @pl.when(i == 0)
                def _(n=n, k=k):
                    dw_refs[n][k:k + 1, :] = dw_tile[n][k]

                @pl.when(i > 0)
                def _(n=n, k=k):
                    dw_refs[n][k:k + 1, :] += dw_tile[n][k]

    in_specs, args = [], []

    def add_rows(arr, c0, prev, nxt):
        off = c0 // tc
        in_specs.append(pl.BlockSpec((tm, tc), lambda j, i, off=off: (i, j + off)))
        args.append(arr)
        if prev:
            in_specs.append(pl.BlockSpec((HALO, tc), lambda j, i, off=off: (jnp.maximum(i * hb - 1, 0), j + off)))
            args.append(arr)
        if nxt:
            last = L // HALO - 1
            in_specs.append(pl.BlockSpec((HALO, tc), lambda j, i, off=off: (jnp.minimum((i + 1) * hb, last), j + off)))
            args.append(arr)

    for arr, c0 in xs:
        add_rows(arr, c0, True, True)
    for arr, c0 in ws:
        off = c0 // tc
        in_specs.append(pl.BlockSpec((K, tc), lambda j, i, off=off: (0, j + off)))
        args.append(arr)
    for arr, c0 in es:
        add_rows(arr, c0, False, True)
    for arr, c0 in dys:
        add_rows(arr, c0, False, True)
    tile = pl.BlockSpec((tm, tc), lambda j, i: (i, j))
    wtile = pl.BlockSpec((K, tc), lambda j, i: (0, j))
    out_specs = [tile] * (nx + ne) + [wtile] * nw
    out_shape = [jax.ShapeDtypeStruct((L, width), out_dtype)] * (nx + ne) + \
        [jax.ShapeDtypeStruct((K, width), F32)] * nw
    res = _pc(body, name=name, grid=(ncol, nrow), in_specs=in_specs, out_specs=out_specs, out_shape=out_shape,
              scratch_shapes=[pltpu.VMEM((tm + 2 * HALO, tc), F32)] * (nx + ne + ny),
              dims=("parallel", "arbitrary"))(*args)
    return res[:nx], res[nx:nx + ne], res[nx + ne:]


def _pre_identity(*x):
    return list(x)


def _pre_product(c, h):
    return [c * h]


def _post_silu(us, es):
    return [_silu(us[0])]


def _post_gate_mul(us, es):
    return [es[0] * us[0]]


def _post_swiglu(us, es):
    return [_silu(us[0]) * us[1]]


def _make_dot(passes):
    def raw(a, b, dn):
        a_hi = a.astype(BF16)
        b_hi = b.astype(BF16)
        out = lax.dot_general(a_hi, b_hi, dn, preferred_element_type=F32)
        if passes == 3:
            a_lo = (a - a_hi.astype(F32)).astype(BF16)
            b_lo = (b - b_hi.astype(F32)).astype(BF16)
            out = out + lax.dot_general(a_hi, b_lo, dn, preferred_element_type=F32)
            out = out + lax.dot_general(a_lo, b_hi, dn, preferred_element_type=F32)
        return out

    @jax.custom_vjp
    def nn(a, b):
        return raw(a, b, _DIMS["nn"])

    @jax.custom_vjp
    def nt(a, b):
        return raw(a, b, _DIMS["nt"])

    @jax.custom_vjp
    def tn(a, b):
        return raw(a, b, _DIMS["tn"])

    nn.defvjp(lambda a, b: (nn(a, b), (a, b)), lambda r, g: (nt(g, r[1]), tn(r[0], g)))
    nt.defvjp(lambda a, b: (nt(a, b), (a, b)), lambda r, g: (nn(g, r[1]), tn(g, r[0])))
    tn.defvjp(lambda a, b: (tn(a, b), (a, b)), lambda r, g: (nt(r[1], g), nn(r[0], g)))
    return nn, nt, tn


_NN1, _NT1, _TN1 = _make_dot(1)
_NN3, _NT3, _TN3 = _make_dot(3)


def _l2norm(x):
    return x * lax.rsqrt(jnp.sum(x * x, axis=-1, keepdims=True) + NORM_EPS)


def _unit_lower_inverse_raw(a_list):
    C = a_list[0].shape[0]
    ii = lax.broadcasted_iota(jnp.int32, (C, C), 0)
    jj = lax.broadcasted_iota(jnp.int32, (C, C), 1)
    eye = jnp.where(ii == jj, 1.0, 0.0)
    ts = [eye - a for a in a_list]
    ps = list(a_list)
    n = 2
    while n < C:
        ps = [_NN3(p, p) for p in ps]
        ts = [t + _NN3(t, p) for t, p in zip(ts, ps)]
        n *= 2
    return ts


@jax.custom_vjp
def _unit_lower_inverse(a_list):
    return _unit_lower_inverse_raw(a_list)


def _unit_lower_inverse_fwd(a_list):
    ts = _unit_lower_inverse_raw(a_list)
    return ts, ts


def _unit_lower_inverse_bwd(ts, gs):
    xs = [_TN3(t, g) for t, g in zip(ts, gs)]
    return ([-_NT3(x, t) for x, t in zip(xs, ts)],)


_unit_lower_inverse.defvjp(_unit_lower_inverse_fwd, _unit_lower_inverse_bwd)


def _gdn_prep(units):
    C, Dh = units[0][0].shape
    ii = lax.broadcasted_iota(jnp.int32, (C, C), 0)
    jj = lax.broadcasted_iota(jnp.int32, (C, C), 1)
    lane = lax.broadcasted_iota(jnp.int32, (1, C), 1)
    causal = ii >= jj
    strict = ii > jj
    qs = [_l2norm(un[0]) * (Dh ** -0.5) for un in units]
    ks = [_l2norm(un[1]) for un in units]
    betas = [jax.nn.sigmoid(un[4]) for un in units]
    gs = [-jnp.exp(un[5]) * jax.nn.softplus(un[3] + un[6]) for un in units]
    gc_rows = [jnp.sum(jnp.where(ii <= jj, g, 0.0), axis=0, keepdims=True) for g in gs]
    gc_cols = [jnp.sum(jnp.where(ii == jj, r, 0.0), axis=1, keepdims=True) for r in gc_rows]
    decays = [jnp.where(causal, jnp.exp(jnp.where(causal, c - r, 0.0)), 0.0) for c, r in zip(gc_cols, gc_rows)]
    kbs = [k * b for k, b in zip(ks, betas)]
    kks = [_NT1(kb, k) for kb, k in zip(kbs, ks)]
    qks = [_NT1(q, k) for q, k in zip(qs, ks)]
    ts = _unit_lower_inverse([jnp.where(strict, kk * d, 0.0) for kk, d in zip(kks, decays)])
    eg_cols = [jnp.exp(c) for c in gc_cols]
    uws = [_NN3(t, jnp.concatenate([un[2] * b, kb * e], axis=1))
           for t, un, b, kb, e in zip(ts, units, betas, kbs, eg_cols)]
    out = []
    for q, k, qk, d, uw, e, r, c in zip(qs, ks, qks, decays, uws, eg_cols, gc_rows, gc_cols):
        g_last = jnp.sum(jnp.where(lane == C - 1, r, 0.0), axis=1, keepdims=True)
        out.append((q * e, k * jnp.exp(g_last - c), uw[:, :Dh], uw[:, Dh:], jnp.where(causal, qk * d, 0.0),
                    jnp.broadcast_to(jnp.exp(g_last), (1, Dh))))
    return out


def _gdn_step(units):
    v_news = [un[3] - _NN1(un[4], un[0]) for un in units]
    o_state = [_NN1(un[1], un[0]) for un in units]
    o_intra = [_NN1(un[5], vn) for un, vn in zip(units, v_news)]
    s_adds = [_TN1(un[2], vn) for un, vn in zip(units, v_news)]
    out = []
    for un, a, b, s_add in zip(units, o_state, o_intra, s_adds):
        o = a + b
        y = o * lax.rsqrt(jnp.mean(o * o, axis=-1, keepdims=True) + NORM_EPS) * un[8] * _silu(un[7])
        out.append((y, un[0] * un[6] + s_add))
    return out


def _gdn_prep_fwd(qkv, gab, gab_col, a_log, dt_bias, *, name, chunks=4):
    L = qkv.shape[0]
    H, C = GDN_HEADS, GDN_CHUNK
    W = qkv.shape[1] // 3
    Dh = W // H
    N = L // C
    chunks = min(chunks, N)
    R = chunks * C
    gab_off = gab_col // LANES

    def body(q_ref, k_ref, v_ref, gab_ref, al_ref, dt_ref, qe_ref, ke_ref, u_ref, w_ref, at_ref, eg_ref):
        where = [(cc, h) for cc in range(chunks) for h in range(H)]
        units = []
        for cc, h in where:
            rows, sl = slice(cc * C, (cc + 1) * C), slice(h * Dh, (h + 1) * Dh)
            units.append((q_ref[rows, sl], k_ref[rows, sl], v_ref[rows, sl], gab_ref[rows, h:h + 1],
                          gab_ref[rows, H + h:H + h + 1], al_ref[h], dt_ref[h]))
        for (cc, h), (qe, ke, u, w, attn, eg) in zip(where, _gdn_prep(units)):
            rows, sl = slice(cc * C, (cc + 1) * C), slice(h * Dh, (h + 1) * Dh)
            qe_ref[rows, sl] = qe
            ke_ref[rows, sl] = ke
            u_ref[rows, sl] = u
            w_ref[rows, sl] = w
            at_ref[h, rows, :] = attn
            eg_ref[cc, h:h + 1, :] = eg

    col = lambda c: pl.BlockSpec((R, W), lambda n, c=c: (n, c))
    tok = pl.BlockSpec((R, LANES), lambda n: (n, gab_off))
    par = pl.BlockSpec((H, 1, 1), lambda n: (0, 0, 0))
    wide = pl.BlockSpec((R, W), lambda n: (n, 0))
    return _pc(body, name=name, grid=(N // chunks,), in_specs=[col(0), col(1), col(2), tok, par, par],
               out_specs=[wide, wide, wide, wide, pl.BlockSpec((H, R, C), lambda n: (0, n, 0)),
                          pl.BlockSpec((chunks, H, Dh), lambda n: (n, 0, 0))],
               out_shape=[jax.ShapeDtypeStruct((L, W), F32)] * 4 + [jax.ShapeDtypeStruct((H, L, C), F32),
                                                                   jax.ShapeDtypeStruct((N, H, Dh), F32)],
               dims=("parallel",))(qkv, qkv, qkv, gab, a_log, dt_bias)


def _gdn_prep_bwd(qkv, gab, gab_col, a_log, dt_bias, dqe, dke, du, dw, dattn, deg, gab_width, *, name, chunks=4):
    L = qkv.shape[0]
    H, C = GDN_HEADS, GDN_CHUNK
    W = qkv.shape[1] // 3
    Dh = W // H
    N = L // C
    chunks = min(chunks, N)
    R = chunks * C
    gab_off = gab_col // LANES

    def body(q_ref, k_ref, v_ref, gab_ref, al_ref, dt_ref, dqe_ref, dke_ref, du_ref, dw_ref, dat_ref, deg_ref,
             dqkv_ref, dgab_ref, dal_ref, ddt_ref):
        first = pl.program_id(0) == 0
        lane = lax.broadcasted_iota(jnp.int32, (C, gab_width), 1)
        dal_sum, ddt_sum = [None] * H, [None] * H
        where = [(cc, h) for cc in range(chunks) for h in range(H)]
        units, cots = [], []
        for cc, h in where:
            rows, sl = slice(cc * C, (cc + 1) * C), slice(h * Dh, (h + 1) * Dh)
            units.append((q_ref[rows, sl], k_ref[rows, sl], v_ref[rows, sl], gab_ref[rows, h:h + 1],
                          gab_ref[rows, H + h:H + h + 1], al_ref[h], dt_ref[h]))
            cots.append((dqe_ref[rows, sl], dke_ref[rows, sl], du_ref[rows, sl], dw_ref[rows, sl],
                         dat_ref[h, rows, :], deg_ref[cc, h:h + 1, :]))
        _, vjp = jax.vjp(_gdn_prep, units)
        (d_units,) = vjp(cots)
        dgabs = [jnp.zeros((C, gab_width), F32) for _ in range(chunks)]
        for (cc, h), (dq, dk, dv, dga, dgb, dal, ddt) in zip(where, d_units):
            rows = slice(cc * C, (cc + 1) * C)
            dqkv_ref[rows, h * Dh:(h + 1) * Dh] = dq
            dqkv_ref[rows, W + h * Dh:W + (h + 1) * Dh] = dk
            dqkv_ref[rows, 2 * W + h * Dh:2 * W + (h + 1) * Dh] = dv
            dgabs[cc] = dgabs[cc] + jnp.where(lane == h, dga, 0.0) + jnp.where(lane == H + h, dgb, 0.0)
            dal_sum[h] = dal if dal_sum[h] is None else dal_sum[h] + dal
            ddt_sum[h] = ddt if ddt_sum[h] is None else ddt_sum[h] + ddt
        for cc in range(chunks):
            dgab_ref[cc * C:(cc + 1) * C, :] = dgabs[cc].astype(BF16)

        @pl.when(first)
        def _():
            for h in range(H):
                dal_ref[h] = dal_sum[h]
                ddt_ref[h] = ddt_sum[h]

        @pl.when(jnp.logical_not(first))
        def _():
            for h in range(H):
                dal_ref[h] += dal_sum[h]
                ddt_ref[h] += ddt_sum[h]

    col = lambda c: pl.BlockSpec((R, W), lambda n, c=c: (n, c))
    tok = pl.BlockSpec((R, LANES), lambda n: (n, gab_off))
    par = pl.BlockSpec((H, 1, 1), lambda n: (0, 0, 0))
    wide = pl.BlockSpec((R, W), lambda n: (n, 0))
    att = pl.BlockSpec((H, R, C), lambda n: (0, n, 0))
    egs = pl.BlockSpec((chunks, H, Dh), lambda n: (n, 0, 0))
    return _pc(body, name=name, grid=(N // chunks,),
               in_specs=[col(0), col(1), col(2), tok, par, par, wide, wide, wide, wide, att, egs],
               out_specs=[pl.BlockSpec((R, 3 * W), lambda n: (n, 0)), pl.BlockSpec((R, gab_width), lambda n: (n, 0)),
                          par, par],
               out_shape=[jax.ShapeDtypeStruct((L, 3 * W), F32), jax.ShapeDtypeStruct((L, gab_width), BF16)]
               + [jax.ShapeDtypeStruct((H, 1, 1), F32)] * 2,
               dims=("arbitrary",))(qkv, qkv, qkv, gab, a_log, dt_bias, dqe, dke, du, dw, dattn, deg)


def _gdn_scan_fwd(qe, ke, u, w, attn, eg, gz, gz_col, wgn, *, name):
    L, W = qe.shape
    H, C = GDN_HEADS, GDN_CHUNK
    Dh = W // H
    N = L // C
    gz_off = gz_col // W
    cps = min(GDN_SCAN_CHUNKS, N)
    R = cps * C

    def body(qe_ref, ke_ref, u_ref, w_ref, at_ref, eg_ref, gz_ref, wgn_ref, y_ref, st_ref, s_scr):
        @pl.when(pl.program_id(0) == 0)
        def _():
            s_scr[...] = jnp.zeros_like(s_scr)

        S = [s_scr[h] for h in range(H)]
        for cc in range(cps):
            rows = slice(cc * C, (cc + 1) * C)
            units = []
            for h in range(H):
                sl = slice(h * Dh, (h + 1) * Dh)
                st_ref[cc, h] = S[h]
                units.append((S[h], qe_ref[rows, sl], ke_ref[rows, sl], u_ref[rows, sl], w_ref[rows, sl],
                              at_ref[h, rows, :], eg_ref[cc, h:h + 1, :], gz_ref[rows, sl], wgn_ref[...]))
            for h, (y, S_new) in enumerate(_gdn_step(units)):
                y_ref[rows, h * Dh:(h + 1) * Dh] = y.astype(BF16)
                S[h] = S_new
        for h in range(H):
            s_scr[h] = S[h]

    wide = pl.BlockSpec((R, W), lambda n: (n, 0))
    return _pc(body, name=name, grid=(N // cps,),
               in_specs=[wide, wide, wide, wide, pl.BlockSpec((H, R, C), lambda n: (0, n, 0)),
                         pl.BlockSpec((cps, H, Dh), lambda n: (n, 0, 0)),
                         pl.BlockSpec((R, W), lambda n: (n, gz_off)), pl.BlockSpec((1, Dh), lambda n: (0, 0))],
               out_specs=[wide, pl.BlockSpec((cps, H, Dh, Dh), lambda n: (n, 0, 0, 0))],
               out_shape=[jax.ShapeDtypeStruct((L, W), BF16), jax.ShapeDtypeStruct((N, H, Dh, Dh), F32)],
               scratch_shapes=[pltpu.VMEM((H, Dh, Dh), F32)],
               dims=("arbitrary",))(qe, ke, u, w, attn, eg, gz, wgn.reshape(1, Dh))


def _gdn_scan_bwd(qe, ke, u, w, attn, eg, gz, gz_col, wgn, states, dy, dy_col, *, name):
    L, W = qe.shape
    H, C = GDN_HEADS, GDN_CHUNK
    Dh = W // H
    N = L // C
    gz_off = gz_col // W
    dy_off = dy_col // W
    cps = min(GDN_SCAN_CHUNKS, N)
    R = cps * C
    steps = N // cps

    def body(qe_ref, ke_ref, u_ref, w_ref, at_ref, eg_ref, gz_ref, wgn_ref, st_ref, dy_ref,
             dqe_ref, dke_ref, du_ref, dw_ref, dat_ref, deg_ref, dgz_ref, dwgn_ref, ds_scr):
        first = pl.program_id(0) == 0

        @pl.when(first)
        def _():
            ds_scr[...] = jnp.zeros_like(ds_scr)

        dwgn = None
        dS = [ds_scr[h] for h in range(H)]
        for cc in reversed(range(cps)):
            rows = slice(cc * C, (cc + 1) * C)
            units, cots = [], []
            for h in range(H):
                sl = slice(h * Dh, (h + 1) * Dh)
                units.append((st_ref[cc, h], qe_ref[rows, sl], ke_ref[rows, sl], u_ref[rows, sl], w_ref[rows, sl],
                              at_ref[h, rows, :], eg_ref[cc, h:h + 1, :], gz_ref[rows, sl], wgn_ref[...]))
                cots.append((dy_ref[rows, sl].astype(F32), dS[h]))
            _, vjp = jax.vjp(_gdn_step, units)
            (d_units,) = vjp(cots)
            for h, (dS_h, dqe, dke, du, dw, dat, deg, dgz, dwg) in enumerate(d_units):
                sl = slice(h * Dh, (h + 1) * Dh)
                dS[h] = dS_h
                dqe_ref[rows, sl] = dqe
                dke_ref[rows, sl] = dke
                du_ref[rows, sl] = du
                dw_ref[rows, sl] = dw
                dat_ref[h, rows, :] = dat
                deg_ref[cc, h:h + 1, :] = deg
                dgz_ref[rows, sl] = dgz.astype(BF16)
                dwgn = dwg if dwgn is None else dwgn + dwg
        for h in range(H):
            ds_scr[h] = dS[h]

        @pl.when(first)
        def _():
            dwgn_ref[...] = dwgn

        @pl.when(jnp.logical_not(first))
        def _():
            dwgn_ref[...] += dwgn

    rev = lambda n: steps - 1 - n
    wide = pl.BlockSpec((R, W), lambda n: (rev(n), 0))
    att = pl.BlockSpec((H, R, C), lambda n: (0, rev(n), 0))
    egs = pl.BlockSpec((cps, H, Dh), lambda n: (rev(n), 0, 0))
    vec = pl.BlockSpec((1, Dh), lambda n: (0, 0))
    return _pc(body, name=name, grid=(steps,),
               in_specs=[wide, wide, wide, wide, att, egs, pl.BlockSpec((R, W), lambda n: (rev(n), gz_off)), vec,
                         pl.BlockSpec((cps, H, Dh, Dh), lambda n: (rev(n), 0, 0, 0)),
                         pl.BlockSpec((R, W), lambda n: (rev(n), dy_off))],
               out_specs=[wide, wide, wide, wide, att, egs, wide, vec],
               out_shape=[jax.ShapeDtypeStruct((L, W), F32)] * 4 + [jax.ShapeDtypeStruct((H, L, C), F32),
                                                                   jax.ShapeDtypeStruct((N, H, Dh), F32),
                                                                   jax.ShapeDtypeStruct((L, W), BF16),
                                                                   jax.ShapeDtypeStruct((1, Dh), F32)],
               scratch_shapes=[pltpu.VMEM((H, Dh, Dh), F32)],
               dims=("arbitrary",))(qe, ke, u, w, attn, eg, gz, wgn.reshape(1, Dh), states, dy)


def _sb_scores(z, mask):
    sp = jnp.maximum(z, 0.0) + jnp.log(1.0 + jnp.exp(-jnp.abs(z)))
    lom = -sp if mask is None else jnp.where(mask, -sp, 0.0)
    return lom, z - sp


def _sb_alive(c_a, c_b):
    return jnp.maximum(jnp.max(c_a), jnp.max(c_b)) >= SB_DEAD_LOG


def _sb_masks(tq, width, dh):
    rr = lax.broadcasted_iota(jnp.int32, (tq, tq), 0)
    cc = lax.broadcasted_iota(jnp.int32, (tq, tq), 1)
    first_head = lax.broadcasted_iota(jnp.int32, (tq, width), 1) < dh
    return cc < rr, jnp.where(rr > cc, 1.0, 0.0).astype(BF16), first_head


def _sb_fwd(qkv, *, name, tq=256):
    L = qkv.shape[0]
    H = SB_HEADS
    width = 2 * (qkv.shape[1] // 3 // H)
    dh = width // 2
    npair = H // 2
    tq = min(tq, L)
    nq = L // tq

    def body(q_ref, k_ref, v_ref, o_ref):
        i = pl.program_id(1)
        diag, tri, first_head = _sb_masks(tq, width, dh)
        qp = q_ref[...]
        zero = jnp.zeros_like(qp)
        qs = (jnp.where(first_head, qp, zero), jnp.where(first_head, zero, qp))

        def blocks(js, carry, mask):
            units = [(b, hd) for b in range(len(js)) for hd in range(2)]
            starts = [pl.multiple_of(j * tq, tq) for j in js]
            ks = [k_ref[pl.ds(st, tq), :] for st in starts]
            vs = [v_ref[pl.ds(st, tq), :] for st in starts]
            zs = {(b, hd): lax.dot_general(qs[hd], ks[b], _DIMS["nt"], preferred_element_type=F32)
                  for b, hd in units}
            scores = {un: _sb_scores(zs[un], mask) for un in units}
            later = {un: jnp.dot(scores[un][0].astype(BF16), tri, preferred_element_type=F32) for un in units}
            cs = [carry[hd][0] for hd in range(2)]
            accs = [carry[hd][1] for hd in range(2)]
            for b, hd in units:
                lom, lb = scores[(b, hd)]
                a = jnp.exp(lb + (cs[hd] + later[(b, hd)]))
                if mask is not None:
                    a = jnp.where(mask, a, 0.0)
                accs[hd] = accs[hd] + jnp.dot(a.astype(BF16), vs[b], preferred_element_type=F32)
                cs[hd] = cs[hd] + jnp.sum(lom, axis=1, keepdims=True)
            return tuple((cs[hd], accs[hd]) for hd in range(2))

        init = tuple((jnp.zeros((tq, 1), F32), jnp.zeros((tq, width), F32)) for _ in range(2))
        carry = blocks([i], init, diag)
        j_end, carry = lax.while_loop(lambda st: jnp.logical_and(st[0] >= 0, _sb_alive(st[1][0][0], st[1][1][0])),
                                      lambda st: (st[0] - 1, blocks([st[0]], st[1], None)), (i - 1, carry))
        o_ref[...] = jnp.where(first_head, carry[0][1], carry[1][1]).astype(BF16)

    return _pc(body, name=name, grid=(npair, nq),
               in_specs=[pl.BlockSpec((tq, width), lambda p, i: (i, p)),
                         pl.BlockSpec((L, width), lambda p, i: (0, npair + p)),
                         pl.BlockSpec((L, width), lambda p, i: (0, 2 * npair + p))],
               out_specs=pl.BlockSpec((tq, width), lambda p, i: (i, p)),
               out_shape=jax.ShapeDtypeStruct((L, npair * width), BF16),
               dims=("parallel", "parallel"))(qkv, qkv, qkv)


def _sb_bwd(qkv, do, do_col, scale, *, name, tq=256):
    L = qkv.shape[0]
    H = SB_HEADS
    width = 2 * (qkv.shape[1] // 3 // H)
    dh = width // 2
    npair = H // 2
    tq = min(tq, L)
    nq = L // tq
    do_off = do_col // width

    def body(q_ref, k_ref, v_ref, do_ref, dq_ref, dk_ref, dv_ref):
        i = pl.program_id(1)

        @pl.when(i == 0)
        def _():
            dk_ref[...] = jnp.zeros_like(dk_ref)
            dv_ref[...] = jnp.zeros_like(dv_ref)

        diag, tri_later, first_head = _sb_masks(tq, width, dh)
        rr = lax.broadcasted_iota(jnp.int32, (tq, tq), 0)
        cc = lax.broadcasted_iota(jnp.int32, (tq, tq), 1)
        tri_before = jnp.where(rr < cc, 1.0, 0.0).astype(BF16)
        qp = q_ref[...]
        dop = do_ref[...].astype(BF16)
        zero = jnp.zeros_like(qp)
        qs = (jnp.where(first_head, qp, zero), jnp.where(first_head, zero, qp))
        dos = (jnp.where(first_head, dop, zero), jnp.where(first_head, zero, dop))
        ctots = []

        def blocks(js, carry, mask):
            nb = len(js)
            units = [(b, hd) for b in range(nb) for hd in range(2)]
            starts = [pl.multiple_of(j * tq, tq) for j in js]
            ks = [k_ref[pl.ds(st, tq), :] for st in starts]
            vs = [v_ref[pl.ds(st, tq), :] for st in starts]
            zs = {(b, hd): lax.dot_general(qs[hd], ks[b], _DIMS["nt"], preferred_element_type=F32)
                  for b, hd in units}
            das = {(b, hd): lax.dot_general(dos[hd], vs[b], _DIMS["nt"], preferred_element_type=F32)
                   for b, hd in units}
            scores = {un: _sb_scores(zs[un], mask) for un in units}
            later = {un: jnp.dot(scores[un][0].astype(BF16), tri_later, preferred_element_type=F32) for un in units}
            pcs = [carry[hd][0] for hd in range(2)]
            avals = {}
            for b, hd in units:
                pcs[hd] = pcs[hd] + jnp.sum(scores[(b, hd)][0], axis=1, keepdims=True)
                a = jnp.exp(scores[(b, hd)][1] + ((ctots[hd] - pcs[hd]) + later[(b, hd)]))
                avals[(b, hd)] = a if mask is None else jnp.where(mask, a, 0.0)
            gs = {un: das[un] * avals[un] for un in units}
            before = {un: jnp.dot(gs[un].astype(BF16), tri_before, preferred_element_type=F32) for un in units}
            pgs = [carry[hd][1] for hd in range(2)]
            dzs = {}
            for b, hd in units:
                sig = jnp.exp(scores[(b, hd)][1])
                dz = gs[(b, hd)] * (1.0 - sig) - (pgs[hd] + before[(b, hd)]) * sig
                dzs[(b, hd)] = (dz if mask is None else jnp.where(mask, dz, 0.0)).astype(BF16)
                pgs[hd] = pgs[hd] + jnp.sum(gs[(b, hd)], axis=1, keepdims=True)
            dqs = [carry[hd][2] for hd in range(2)]
            for b, hd in units:
                dqs[hd] = dqs[hd] + jnp.dot(dzs[(b, hd)], ks[b], preferred_element_type=F32)
            for b in range(nb):
                dk_ref[pl.ds(starts[b], tq), :] += sum(
                    lax.dot_general(dzs[(b, hd)], qs[hd], _DIMS["tn"], preferred_element_type=F32) for hd in range(2))
                dv_ref[pl.ds(starts[b], tq), :] += sum(
                    lax.dot_general(avals[(b, hd)].astype(BF16), dos[hd], _DIMS["tn"], preferred_element_type=F32)
                    for hd in range(2))
            return tuple((pcs[hd], pgs[hd], dqs[hd]) for hd in range(2))

        def row_sums(j, mask):
            kj = k_ref[pl.ds(pl.multiple_of(j * tq, tq), tq), :]
            return tuple(jnp.sum(_sb_scores(lax.dot_general(qs[hd], kj, _DIMS["nt"], preferred_element_type=F32),
                                            mask)[0], axis=1, keepdims=True) for hd in range(2))

        j_dead, live_sums = lax.while_loop(
            lambda st: jnp.logical_and(st[0] >= 0, _sb_alive(st[1][0], st[1][1])),
            lambda st: (st[0] - 1, tuple(a + b for a, b in zip(st[1], row_sums(st[0], None)))),
            (i - 1, row_sums(i, diag)))
        ctots.extend(live_sums)
        col = jnp.zeros((tq, 1), F32)
        init = tuple((col, col, jnp.zeros((tq, width), F32)) for _ in range(2))
        carry = lax.fori_loop(j_dead + 1, i, lambda j, cr: blocks([j], cr, None), init)
        carry = blocks([i], carry, diag)
        dq_ref[...] = (jnp.where(first_head, carry[0][2], carry[1][2]) * scale).astype(BF16)

    tile = pl.BlockSpec((tq, width), lambda p, i: (i, p))
    full = pl.BlockSpec((L, width), lambda p, i: (0, p))
    sds = jax.ShapeDtypeStruct((L, npair * width), F32)
    return _pc(body, name=name, grid=(npair, nq),
               in_specs=[tile, pl.BlockSpec((L, width), lambda p, i: (0, npair + p)),
                         pl.BlockSpec((L, width), lambda p, i: (0, 2 * npair + p)),
                         pl.BlockSpec((tq, width), lambda p, i: (i, do_off + p))],
               out_specs=[tile, full, full],
               out_shape=[jax.ShapeDtypeStruct((L, npair * width), BF16), sds, sds],
               dims=("parallel", "arbitrary"))(qkv, qkv, qkv, do)


def _adamw(w, g, m, v, *, name, tm=256):
    R, C = w.shape
    tm = min(tm, R)
    assert R % tm == 0, (R, tm)
    c1 = 1.0 - ADAM_B1 ** ADAM_STEP
    c2 = 1.0 - ADAM_B2 ** ADAM_STEP

    def body(w_ref, g_ref, m_ref, v_ref, d_ref, nm_ref, nv_ref):
        gv = g_ref[...]
        nm = ADAM_B1 * m_ref[...] + (1.0 - ADAM_B1) * gv
        nv = ADAM_B2 * v_ref[...] + (1.0 - ADAM_B2) * (gv * gv)
        d_ref[...] = -ADAM_LR * ((nm / c1) / (jnp.sqrt(nv / c2) + ADAM_EPS) + ADAM_WD * w_ref[...])
        nm_ref[...] = nm
        nv_ref[...] = nv

    blk = pl.BlockSpec((tm, C), lambda i: (i, 0))
    sds = jax.ShapeDtypeStruct((R, C), F32)
    return _pc(body, name=name, grid=(R // tm,), in_specs=[blk] * 4, out_specs=[blk] * 3, out_shape=[sds] * 3,
               dims=("parallel",))(w, g, m, v)


ELEMENTWISE_BLOCK_BYTES = 1 << 20


def _row_tile(rows, cols):
    for t in (512, 384, 352, 256, 176, 128, 88, 64, 32, 16, 8):
        if rows % t == 0 and t * cols * 4 <= ELEMENTWISE_BLOCK_BYTES:
            return t
    raise ValueError((rows, cols))


def _adamw_layers(w, g_mine, g_other, m, v, c, *, name):
    _, R, C = w.shape
    tm = _row_tile(R, C)
    c1 = 1.0 - ADAM_B1 ** ADAM_STEP
    c2 = 1.0 - ADAM_B2 ** ADAM_STEP

    def body(c_ref, w_ref, gm_ref, go_ref, m_ref, v_ref, g_ref, d_ref, nm_ref, nv_ref):
        gv = jnp.where(pl.program_id(0) == c_ref[0], gm_ref[...], go_ref[...])
        nm = ADAM_B1 * m_ref[...] + (1.0 - ADAM_B1) * gv
        nv = ADAM_B2 * v_ref[...] + (1.0 - ADAM_B2) * (gv * gv)
        g_ref[...] = gv
        d_ref[...] = -ADAM_LR * ((nm / c1) / (jnp.sqrt(nv / c2) + ADAM_EPS) + ADAM_WD * w_ref[...])
        nm_ref[...] = nm
        nv_ref[...] = nv

    slab = pl.BlockSpec((None, tm, C), lambda l, i, c_ref: (l, i, 0))
    mine = pl.BlockSpec((tm, C), lambda l, i, c_ref: (jnp.where(l == c_ref[0], i, 0), 0))
    other = pl.BlockSpec((tm, C), lambda l, i, c_ref: (jnp.where(l == c_ref[0], 0, i), 0))
    grid_spec = pltpu.PrefetchScalarGridSpec(num_scalar_prefetch=1, grid=(2, R // tm),
                                             in_specs=[slab, mine, other, slab, slab], out_specs=[slab] * 4)
    return _pc_prefetch(body, name=name, grid_spec=grid_spec, out_shape=[jax.ShapeDtypeStruct(w.shape, F32)] * 4,
                        dims=("parallel", "parallel"))(c.reshape(1).astype(jnp.int32), w, g_mine, g_other, m, v)


def _add_layers(g0, g1, ra, c, *, name):
    S, R, C = ra.shape
    tm = _row_tile(R, C)

    def body(c_ref, g0_ref, g1_ref, r_ref, o_ref):
        mine = jnp.where(c_ref[0] == 0, g0_ref[...], g1_ref[...])
        o_ref[...] = (mine + r_ref[...]).astype(BF16)

    def walked_if(layer):
        return lambda s, i, c_ref: (jnp.where(c_ref[0] == layer, s, 0), jnp.where(c_ref[0] == layer, i, 0), 0)

    blk = lambda s, i, c_ref: (s, i, 0)
    grid_spec = pltpu.PrefetchScalarGridSpec(
        num_scalar_prefetch=1, grid=(S, R // tm),
        in_specs=[pl.BlockSpec((None, tm, C), walked_if(0)), pl.BlockSpec((None, tm, C), walked_if(1)),
                  pl.BlockSpec((None, tm, C), blk)],
        out_specs=pl.BlockSpec((None, tm, C), blk))
    return _pc_prefetch(body, name=name, grid_spec=grid_spec, out_shape=jax.ShapeDtypeStruct((S, R, C), BF16),
                        dims=("parallel", "parallel"))(c.reshape(1).astype(jnp.int32), g0, g1, ra)


def _add_chips(p, rb, chip, *, name):
    S, Rh, C = p.shape
    tm = _row_tile(Rh, C)

    def body(s_ref, p_ref, r_ref, o_ref):
        o_ref[...] = ((p_ref[...].astype(F32) + r_ref[0].astype(F32)) + r_ref[1].astype(F32)) + r_ref[2].astype(F32)

    grid_spec = pltpu.PrefetchScalarGridSpec(
        num_scalar_prefetch=1, grid=(Rh // tm,),
        in_specs=[pl.BlockSpec((None, tm, C), lambda i, s_ref: (s_ref[0], i, 0)),
                  pl.BlockSpec((3, tm, C), lambda i, s_ref: (0, i, 0))],
        out_specs=pl.BlockSpec((tm, C), lambda i, s_ref: (i, 0)))
    return _pc_prefetch(body, name=name, grid_spec=grid_spec, out_shape=jax.ShapeDtypeStruct((Rh, C), F32),
                        dims=("parallel",))(chip.reshape(1).astype(jnp.int32), p, rb)


def _sum_slots(g, *, name):
    n, R, C = g.shape

    def body(g_ref, o_ref):
        acc = g_ref[0]
        for s in range(1, n):
            acc = acc + g_ref[s]
        o_ref[...] = acc

    return _pc(body, name=name, grid=(1,), in_specs=[pl.BlockSpec((n, R, C), lambda i: (0, 0, 0))],
               out_specs=pl.BlockSpec((R, C), lambda i: (0, 0)), out_shape=jax.ShapeDtypeStruct((R, C), F32),
               dims=("arbitrary",))(g)


ANY = pl.BlockSpec(memory_space=pl.ANY)


def _place():
    return lax.axis_index("x"), lax.axis_index("y"), lax.axis_index("c")


def _other_chips(x, y):
    return [(1 - x, y), (x, 1 - y), (1 - x, 1 - y)]


def _allgather_chips(ws, *, name):
    n = len(ws)

    def body(*refs):
        w_refs, out_refs, send_sems, recv_sems = refs[:n], refs[n:2 * n], refs[2 * n], refs[2 * n + 1]
        x, y, c = _place()
        sib = (x, y, 1 - c)
        chips = _other_chips(x, y)

        def copy(a, k, chip_id, layer, to):
            src = w_refs[a].at[layer] if k < 3 else out_refs[a].at[chip_id, layer]
            return pltpu.make_async_remote_copy(src_ref=src, dst_ref=out_refs[a].at[chip_id, layer],
                                                send_sem=send_sems.at[k * n + a], recv_sem=recv_sems.at[k * n + a],
                                                device_id=to, device_id_type=MESH)

        sends = [copy(a, j, 2 * x + y, c, (px, py, c)) for j, (px, py) in enumerate(chips) for a in range(n)]
        for cp in sends:
            cp.start()
        passed = []
        for j, (px, py) in enumerate(chips):
            for a in range(n):
                copy(a, j, 2 * px + py, c, (px, py, c)).wait_recv()
                fwd = copy(a, 3 + j, 2 * px + py, c, sib)
                fwd.start()
                passed.append(fwd)
        for j, (px, py) in enumerate(chips):
            for a in range(n):
                copy(a, 3 + j, 2 * px + py, 1 - c, sib).wait_recv()
        for cp in sends + passed:
            cp.wait_send()

    return _pc_comm(body, name=name, in_specs=[ANY] * n, out_specs=[ANY] * n,
                    out_shape=[jax.ShapeDtypeStruct((N_CHIPS,) + w.shape, w.dtype) for w in ws],
                    scratch_shapes=[pltpu.SemaphoreType.DMA((6 * n,)), pltpu.SemaphoreType.DMA((6 * n,))])(*ws)


def _send_other_layer_to_sibling(g0s, g1s, *, name):
    n = len(g0s)

    def body(*refs):
        g_refs = (refs[:n], refs[n:2 * n])
        out_refs, send_sems, recv_sems = refs[2 * n:3 * n], refs[3 * n], refs[3 * n + 1]
        x, y, c = _place()

        def copy(a, layer):
            return pltpu.make_async_remote_copy(src_ref=g_refs[layer][a], dst_ref=out_refs[a], send_sem=send_sems.at[a],
                                                recv_sem=recv_sems.at[a], device_id=(x, y, 1 - c), device_id_type=MESH)

        for layer in range(2):
            @pl.when(c == 1 - layer)
            def _(layer=layer):
                for a in range(n):
                    copy(a, layer).start()
        for a in range(n):
            copy(a, 0).wait()

    return _pc_comm(body, name=name, in_specs=[ANY] * (2 * n), out_specs=[ANY] * n,
                    out_shape=[jax.ShapeDtypeStruct(g.shape, g.dtype) for g in g0s],
                    scratch_shapes=[pltpu.SemaphoreType.DMA((n,)), pltpu.SemaphoreType.DMA((n,))])(*g0s, *g1s)


def _scatter_to_chips(ps, *, name):
    n = len(ps)

    def body(*refs):
        p_refs, rb_refs, send_sems, recv_sems = refs[:n], refs[n:2 * n], refs[2 * n], refs[2 * n + 1]
        x, y, c = _place()
        chips = _other_chips(x, y)
        sends = [pltpu.make_async_remote_copy(src_ref=p_refs[a].at[2 * px + py], dst_ref=rb_refs[a].at[j],
                                              send_sem=send_sems.at[j * n + a], recv_sem=recv_sems.at[j * n + a],
                                              device_id=(px, py, c), device_id_type=MESH)
                 for j, (px, py) in enumerate(chips) for a in range(n)]
        for cp in sends:
            cp.start()
        for cp in sends:
            cp.wait()

    return _pc_comm(body, name=name, in_specs=[ANY] * n, out_specs=[ANY] * n,
                    out_shape=[jax.ShapeDtypeStruct((3,) + p.shape[1:], p.dtype) for p in ps],
                    scratch_shapes=[pltpu.SemaphoreType.DMA((3 * n,)), pltpu.SemaphoreType.DMA((3 * n,))])(*ps)


def _swap_with_sibling(fs, *, name):
    n = len(fs)

    def body(*refs):
        f_refs, out_refs, send_sems, recv_sems = refs[:n], refs[n:2 * n], refs[2 * n], refs[2 * n + 1]
        x, y, c = _place()
        copies = [pltpu.make_async_remote_copy(src_ref=f_refs[a], dst_ref=out_refs[a], send_sem=send_sems.at[a],
                                               recv_sem=recv_sems.at[a], device_id=(x, y, 1 - c), device_id_type=MESH)
                  for a in range(n)]
        for cp in copies:
            cp.start()
        for cp in copies:
            cp.wait()

    return _pc_comm(body, name=name, in_specs=[ANY] * n, out_specs=[ANY] * n,
                    out_shape=[jax.ShapeDtypeStruct(f.shape, f.dtype) for f in fs],
                    scratch_shapes=[pltpu.SemaphoreType.DMA((n,)), pltpu.SemaphoreType.DMA((n,))])(*fs)


def _allgather_devices(v, *, name):
    R, C = v.shape

    def body(v_ref, out_ref, send_sems, recv_sems):
        x, y, c = _place()
        me = 4 * x + 2 * y + c
        out_ref[me] = v_ref[...]
        peers = []
        for k in range(1, 8):
            fx, fy, fc = (k >> 2) & 1, (k >> 1) & 1, k & 1
            px = 1 - x if fx else x
            py = 1 - y if fy else y
            pcc = 1 - c if fc else c
            peers.append((px, py, pcc))
        sends = []
        for k, peer in enumerate(peers):
            cp = pltpu.make_async_remote_copy(src_ref=v_ref, dst_ref=out_ref.at[me], send_sem=send_sems.at[k],
                                              recv_sem=recv_sems.at[k], device_id=peer, device_id_type=MESH)
            cp.start()
            sends.append(cp)
        for k, (px, py, pcc) in enumerate(peers):
            pltpu.make_async_remote_copy(src_ref=v_ref, dst_ref=out_ref.at[4 * px + 2 * py + pcc],
                                         send_sem=send_sems.at[k], recv_sem=recv_sems.at[k], device_id=peers[k],
                                         device_id_type=MESH).wait_recv()
        for cp in sends:
            cp.wait_send()

    vm = pl.BlockSpec(memory_space=pltpu.VMEM)
    return _pc_comm(body, name=name, in_specs=[vm], out_specs=vm, out_shape=jax.ShapeDtypeStruct((8, R, C), F32),
                    scratch_shapes=[pltpu.SemaphoreType.DMA((7,)), pltpu.SemaphoreType.DMA((7,))])(v)


D_MODEL = 1024
SC_W = D_MODEL // 4
GDN_W = D_MODEL // 2
SB_W = D_MODEL - SC_W - GDN_W
D_FF = 256 * ((8 * D_MODEL // 3 + 255) // 256)
O_SC, O_GQKV, O_GZ, O_GA, O_GB, O_SB = 0, 3 * SC_W, 3 * SC_W + 3 * GDN_W, 3 * SC_W + 4 * GDN_W, \
    3 * SC_W + 4 * GDN_W + GDN_HEADS, 3 * SC_W + 4 * GDN_W + 2 * GDN_HEADS
D_IN_PROJ = O_SB + 3 * SB_W
P_GQKV, P_SC, P_SB = 0, 3 * GDN_W, 3 * GDN_W + 3 * SC_W
P_GZ = P_SB + 3 * SB_W
P_GAB = P_GZ + GDN_W
P_PAD = 256
P_WIDTH = P_GAB + P_PAD


def _proj_to_kernel_layout(w):
    pad = jnp.zeros((w.shape[0], P_PAD - 2 * GDN_HEADS), w.dtype)
    return jnp.concatenate([w[:, O_GQKV:O_GZ], w[:, O_SC:O_GQKV], w[:, O_SB:], w[:, O_GZ:O_GA], w[:, O_GA:O_SB], pad],
                           axis=1)


def _proj_from_kernel_layout(g):
    return jnp.concatenate([g[:, P_SC:P_SB], g[:, P_GQKV:P_SC], g[:, P_GZ:P_GAB], g[:, P_GAB:P_GAB + 2 * GDN_HEADS],
                            g[:, P_SB:P_GZ]], axis=1)


def _mixout_to_kernel_layout(w):
    return jnp.concatenate([w[SC_W:SC_W + GDN_W], w[:SC_W], w[SC_W + GDN_W:]], axis=0)


def _pack_vec(parts, rows_to):
    flat = jnp.concatenate([p.reshape(-1) for p in parts])
    return jnp.pad(flat, (0, rows_to * LANES - flat.shape[0])).reshape(rows_to, LANES)


def _unpack_vec(mat, shapes):
    flat = mat.reshape(-1)
    out, r = [], 0
    for shp in shapes:
        n = int(np.prod(shp))
        out.append(flat[r:r + n].reshape(shp))
        r += n
    return out


def _round_up(n, m):
    return (n + m - 1) // m * m


def _layer_fwd(x, p, l):
    L = x.shape[0]
    tag = "l%d_" % l
    h = _rmsnorm_fwd(x, p["wn_mix"], name=tag + "norm_mix")
    proj = _matmul(h, p["w_in"], "nn", tm=512, tn=768, tk=D_MODEL, name=tag + "proj")
    (y_sc,) = _conv_pointwise_fwd([(proj, P_SC + SC_W), (proj, P_SC + 2 * SC_W)], [(p["w_sconv"], 0)], [(proj, P_SC)],
                                  _pre_product, _post_gate_mul, [(SC_W, BF16)], tc=SC_W, tm=512, name=tag + "sconv")
    (qkv,) = _conv_pointwise_fwd([(proj, P_GQKV)], [(p["w_gdn_conv"], 0)], [], _pre_identity, _post_silu,
                                 [(3 * GDN_W, F32)], tc=GDN_W, tm=512, name=tag + "gdn_conv")
    qe, ke, u, w, attn, eg = _gdn_prep_fwd(qkv, proj, P_GAB, p["a_log"], p["dt_bias"], name=tag + "gdn_prep")
    y_gdn, states = _gdn_scan_fwd(qe, ke, u, w, attn, eg, proj, P_GZ, p["wgn"], name=tag + "gdn_scan")
    sb_scale = (SB_W // SB_HEADS) ** -0.5
    sbqkv = jnp.concatenate([proj[:, P_SB:P_SB + SB_W] * sb_scale, proj[:, P_SB + SB_W:P_SB + 3 * SB_W]],
                            axis=1).astype(BF16)
    y_sb = _sb_fwd(sbqkv, name=tag + "sb_fwd")
    y_cat = [y_gdn, y_sc, y_sb]
    x2 = _matmul_rows_parts(y_cat, p["w_out"], "nn", res=x, name=tag + "mix_out")
    h2 = _rmsnorm_fwd(x2, p["wn_ffn"], name=tag + "norm_ffn")
    up_g = _matmul(h2, p["w_up_g"], "nn", tm=512, tn=D_FF // 2, tk=D_MODEL, name=tag + "up_gate")
    up_v = _matmul(h2, p["w_up_v"], "nn", tm=512, tn=D_FF // 2, tk=D_MODEL, name=tag + "up_val")
    (act,) = _conv_pointwise_fwd([(up_g, 0), (up_v, 0)], [(p["w_fconv_g"], 0), (p["w_fconv_v"], 0)], [],
                                 _pre_identity, _post_swiglu, [(D_FF, BF16)], tc=256, tm=512, name=tag + "ffn_act")
    x3 = _matmul(act, p["w_down"], "nn", tm=512, tn=D_MODEL, tk=D_FF // 2, res=x2, name=tag + "ffn_down")
    saved = dict(x=x, h=h, proj=proj, qkv=qkv, qe=qe, ke=ke, u=u, w=w, attn=attn, eg=eg, states=states,
                 sbqkv=sbqkv, y_cat=y_cat, x2=x2, h2=h2, up_g=up_g, up_v=up_v, act=act)
    return x3, saved


def _layer_bwd(dx3, p, s, l):
    L = dx3.shape[0]
    tag = "l%d_b_" % l
    g = {}
    dact = _matmul(dx3, p["w_down"], "nt", tm=512, tn=D_FF // 2, tk=D_MODEL, name=tag + "dact")
    g["w_down"] = _matmul(s["act"], dx3, "tn", tm=D_FF // 2, tn=D_MODEL, tk=512, name=tag + "dw_down")
    (dup_g, dup_v), _, (g["w_fconv_g"], g["w_fconv_v"]) = _conv_pointwise_bwd(
        [(s["up_g"], 0), (s["up_v"], 0)], [(p["w_fconv_g"], 0), (p["w_fconv_v"], 0)], [], [(dact, 0)],
        _pre_identity, _post_swiglu, D_FF, tc=256, tm=512, name=tag + "ffn_act")
    dh2 = _matmul(dup_g, p["w_up_g"], "nt", tm=512, tn=D_MODEL, tk=D_FF // 2, name=tag + "dh2_gate")
    dh2 = _matmul(dup_v, p["w_up_v"], "nt", tm=512, tn=D_MODEL, tk=D_FF // 2, res=dh2, name=tag + "dh2_val")
    g["w_up"] = _matmul(s["h2"], dup_g, "tn", tm=D_MODEL, tn=D_FF // 2, tk=512, slabs=(N_CHIPS, 0, None),
                        name=tag + "dw_up_gate")
    g["w_up"] = _matmul(s["h2"], dup_v, "tn", tm=D_MODEL, tn=D_FF // 2, tk=512, slabs=(N_CHIPS, 2, g["w_up"]),
                        name=tag + "dw_up_val")
    dx2, g["wn_ffn"] = _rmsnorm_bwd(dh2, s["x2"], p["wn_ffn"], dx3, name=tag + "norm_ffn")
    dycat = _matmul(dx2, p["w_out"], "nt", tm=512, tn=D_MODEL, tk=D_MODEL, name=tag + "dycat")
    y_gdn, y_sc, y_sb = s["y_cat"]
    g["w_out"] = _matmul_tn_parts([y_sc, y_gdn, y_sb], dx2, name=tag + "dw_out")
    sb_scale = (SB_W // SB_HEADS) ** -0.5
    dsq, dsk, dsv = _sb_bwd(s["sbqkv"], dycat, GDN_W + SC_W, sb_scale, name=tag + "sb_bwd")
    dqe, dke, du, dw, dattn, deg, dgz, g["wgn"] = _gdn_scan_bwd(
        s["qe"], s["ke"], s["u"], s["w"], s["attn"], s["eg"], s["proj"], P_GZ, p["wgn"], s["states"], dycat, 0,
        name=tag + "gdn_scan")
    dqkv_act, dgab, g["a_log"], g["dt_bias"] = _gdn_prep_bwd(
        s["qkv"], s["proj"], P_GAB, p["a_log"], p["dt_bias"], dqe, dke, du, dw, dattn, deg, P_PAD,
        name=tag + "gdn_prep")
    (dqkv,), _, (g["w_gdn_conv"],) = _conv_pointwise_bwd(
        [(s["proj"], P_GQKV)], [(p["w_gdn_conv"], 0)], [], [(dqkv_act, 0)], _pre_identity, _post_silu, 3 * GDN_W,
        tc=GDN_W, tm=512, name=tag + "gdn_conv")
    (dsc_c, dsc_h), (dsc_b,), (g["w_sconv"],) = _conv_pointwise_bwd(
        [(s["proj"], P_SC + SC_W), (s["proj"], P_SC + 2 * SC_W)], [(p["w_sconv"], 0)], [(s["proj"], P_SC)],
        [(dycat, GDN_W)], _pre_product, _post_gate_mul, SC_W, tc=SC_W, tm=512, name=tag + "sconv")
    dproj = [dqkv, dsc_b, dsc_c, dsc_h, dsq, dsk, dsv, dgz, dgab]
    dh = _matmul_rows_parts(dproj, p["w_in"], "nt", name=tag + "dh")
    g["w_in"] = jnp.concatenate([_matmul_tn_parts(s["h"], dproj[:4], name=tag + "dw_in_a"),
                                 _matmul_tn_parts(s["h"], dproj[4:], name=tag + "dw_in_b")], axis=1)
    dx, g["wn_mix"] = _rmsnorm_bwd(dh, s["x"], p["wn_mix"], dx2, name=tag + "norm_mix")
    return dx, g


BIG = ("w_mix_in", "w_mix_out", "w_ffn_up", "w_ffn_down")
BIG_AXIS = {"w_mix_in": 2, "w_mix_out": 1, "w_ffn_up": 2, "w_ffn_down": 1}
SMALL_SHARDED = ("w_sconv", "w_gdn_conv", "w_ffn_conv")
SMALL_REPLICATED = ("w_norm_mix", "gdn_a_log", "gdn_dt_bias", "w_gdn_norm", "w_norm_ffn", "w_norm_final")
WEIGHTS = ("w_norm_mix", "w_mix_in", "w_sconv", "w_gdn_conv", "gdn_a_log", "gdn_dt_bias", "w_gdn_norm", "w_mix_out",
           "w_norm_ffn", "w_ffn_up", "w_ffn_conv", "w_ffn_down", "w_norm_final")


def kernel(x, w_norm_mix, w_mix_in, w_sconv, w_gdn_conv, gdn_a_log, gdn_dt_bias, w_gdn_norm, w_mix_out, w_norm_ffn, w_ffn_up, w_ffn_conv, w_ffn_down, w_norm_final, loss_target, m_w_norm_mix, m_w_mix_in, m_w_sconv, m_w_gdn_conv, m_gdn_a_log, m_gdn_dt_bias, m_w_gdn_norm, m_w_mix_out, m_w_norm_ffn, m_w_ffn_up, m_w_ffn_conv, m_w_ffn_down, m_w_norm_final, v_w_norm_mix, v_w_mix_in, v_w_sconv, v_w_gdn_conv, v_gdn_a_log, v_gdn_dt_bias, v_w_gdn_norm, v_w_mix_out, v_w_norm_ffn, v_w_ffn_up, v_w_ffn_conv, v_w_ffn_down, v_w_norm_final):
    W = dict(w_norm_mix=w_norm_mix, w_mix_in=w_mix_in, w_sconv=w_sconv, w_gdn_conv=w_gdn_conv, gdn_a_log=gdn_a_log,
             gdn_dt_bias=gdn_dt_bias, w_gdn_norm=w_gdn_norm, w_mix_out=w_mix_out, w_norm_ffn=w_norm_ffn,
             w_ffn_up=w_ffn_up, w_ffn_conv=w_ffn_conv, w_ffn_down=w_ffn_down, w_norm_final=w_norm_final)
    M = dict(w_norm_mix=m_w_norm_mix, w_mix_in=m_w_mix_in, w_sconv=m_w_sconv, w_gdn_conv=m_w_gdn_conv,
             gdn_a_log=m_gdn_a_log, gdn_dt_bias=m_gdn_dt_bias, w_gdn_norm=m_w_gdn_norm, w_mix_out=m_w_mix_out,
             w_norm_ffn=m_w_norm_ffn, w_ffn_up=m_w_ffn_up, w_ffn_conv=m_w_ffn_conv, w_ffn_down=m_w_ffn_down,
             w_norm_final=m_w_norm_final)
    V = dict(w_norm_mix=v_w_norm_mix, w_mix_in=v_w_mix_in, w_sconv=v_w_sconv, w_gdn_conv=v_w_gdn_conv,
             gdn_a_log=v_gdn_a_log, gdn_dt_bias=v_gdn_dt_bias, w_gdn_norm=v_w_gdn_norm, w_mix_out=v_w_mix_out,
             w_norm_ffn=v_w_norm_ffn, w_ffn_up=v_w_ffn_up, w_ffn_conv=v_w_ffn_conv, w_ffn_down=v_w_ffn_down,
             w_norm_final=v_w_norm_final)
    depth = w_mix_in.shape[0]
    L = x.shape[1]
    mx, my, mc = lax.axis_index("x"), lax.axis_index("y"), lax.axis_index("c")
    chip = 2 * mx + my

    assert depth == 2
    own = [W[n].astype(BF16) for n in BIG]
    gathered = _allgather_chips(own, name="gather_big")
    gathered = [lax.dynamic_update_slice(g, o[None], (chip, 0, 0, 0)) for g, o in zip(gathered, own)]
    full_big = [{n: jnp.concatenate([g[b, l] for b in range(N_CHIPS)], axis=BIG_AXIS[n] - 1)
                 for n, g in zip(BIG, gathered)} for l in range(depth)]

    small_sh_shapes = [W[n].shape for n in SMALL_SHARDED]
    n_small_sh = sum(int(np.prod(s)) for s in small_sh_shapes)
    small_rows = _round_up(n_small_sh, 8 * LANES) // LANES
    small_all = _allgather_devices(_pack_vec([W[n] for n in SMALL_SHARDED], small_rows), name="gather_small")
    small_chip = [_unpack_vec(small_all[2 * b], small_sh_shapes) for b in range(N_CHIPS)]
    full_small = {n: jnp.concatenate([small_chip[b][i] for b in range(N_CHIPS)], axis=2)
                  for i, n in enumerate(SMALL_SHARDED)}

    params = []
    for l in range(depth):
        w_up = full_big[l]["w_ffn_up"]
        fconv = full_small["w_ffn_conv"][l]
        params.append(dict(
            wn_mix=w_norm_mix[l], w_in=_proj_to_kernel_layout(full_big[l]["w_mix_in"]),
            w_sconv=full_small["w_sconv"][l], w_gdn_conv=full_small["w_gdn_conv"][l],
            a_log=gdn_a_log[l].reshape(GDN_HEADS, 1, 1), dt_bias=gdn_dt_bias[l].reshape(GDN_HEADS, 1, 1),
            wgn=w_gdn_norm[l], w_out=_mixout_to_kernel_layout(full_big[l]["w_mix_out"]), wn_ffn=w_norm_ffn[l],
            w_up_g=w_up[:, :D_FF], w_up_v=w_up[:, D_FF:], w_fconv_g=fconv[:, :D_FF], w_fconv_v=fconv[:, D_FF:],
            w_down=full_big[l]["w_ffn_down"]))

    xs = x[0]
    saved = []
    for l in range(depth):
        xs, s = _layer_fwd(xs, params[l], l)
        saved.append(s)
    loss_row, dx, g_norm_final = _final_loss(xs, w_norm_final, loss_target[0], name="final_loss")
    grads = [None] * depth
    for l in reversed(range(depth)):
        dx, grads[l] = _layer_bwd(dx, params[l], saved[l], l)
    loss = lax.psum(loss_row[0, 0], ("x", "y", "c"))

    G = {
        "w_sconv": jnp.stack([grads[l]["w_sconv"] for l in range(depth)]),
        "w_gdn_conv": jnp.stack([grads[l]["w_gdn_conv"] for l in range(depth)]),
        "w_ffn_conv": jnp.stack([jnp.concatenate([grads[l]["w_fconv_g"], grads[l]["w_fconv_v"]], axis=1)
                                 for l in range(depth)]),
        "w_norm_mix": jnp.stack([grads[l]["wn_mix"].reshape(-1) for l in range(depth)]),
        "gdn_a_log": jnp.stack([grads[l]["a_log"].reshape(-1) for l in range(depth)]),
        "gdn_dt_bias": jnp.stack([grads[l]["dt_bias"].reshape(-1) for l in range(depth)]),
        "w_gdn_norm": jnp.stack([grads[l]["wgn"].reshape(-1) for l in range(depth)]),
        "w_norm_ffn": jnp.stack([grads[l]["wn_ffn"].reshape(-1) for l in range(depth)]),
        "w_norm_final": g_norm_final.reshape(-1),
    }

    def by_shard(l):
        g_in = _proj_from_kernel_layout(grads[l]["w_in"])
        g_in = g_in.reshape(D_MODEL, N_CHIPS, -1).transpose(1, 0, 2)
        return [g_in, grads[l]["w_out"].reshape(N_CHIPS, -1, D_MODEL), grads[l]["w_up"],
                grads[l]["w_down"].reshape(N_CHIPS, -1, D_MODEL)]

    g_layers = [by_shard(l) for l in range(depth)]
    from_sibling = _send_other_layer_to_sibling(g_layers[0], g_layers[1], name="rs_sibling")
    chip_sums = [_add_layers(g0, g1, ra, mc, name="rs_add_layers_" + n)
                 for n, g0, g1, ra in zip(BIG, g_layers[0], g_layers[1], from_sibling)]
    from_chips = _scatter_to_chips(chip_sums, name="rs_chips")
    mine = [_add_chips(p, rb, chip, name="rs_add_chips_" + n) for n, p, rb in zip(BIG, chip_sums, from_chips)]
    other = _swap_with_sibling(mine, name="rs_result")
    out_g, out_d, out_m, out_v = {}, {}, {}, {}
    for n, g_mine, g_other in zip(BIG, mine, other):
        out_g[n], out_d[n], out_m[n], out_v[n] = _adamw_layers(W[n], g_mine, g_other, M[n], V[n], mc,
                                                               name="adamw_" + n)

    small_names = SMALL_SHARDED + SMALL_REPLICATED
    small_full_shapes = [G[n].shape for n in small_names]
    n_small = sum(int(np.prod(s)) for s in small_full_shapes)
    red_rows = _round_up(n_small, 8 * LANES) // LANES
    partials = _allgather_devices(_pack_vec([G[n] for n in small_names], red_rows), name="reduce_small")
    summed = _unpack_vec(_sum_slots(partials, name="reduce_small_sum"), small_full_shapes)
    g_small = {}
    for n, a in zip(small_names, summed):
        if n in SMALL_SHARDED:
            width = a.shape[2] // N_CHIPS
            a = lax.dynamic_slice_in_dim(a, chip * width, width, axis=2)
        g_small[n] = a
    own_shapes = [W[n].shape for n in small_names]
    n_own = sum(int(np.prod(s)) for s in own_shapes)
    own_rows = _round_up(n_own, 8 * LANES) // LANES
    packed = [_pack_vec([src[n] for n in small_names], own_rows) for src in (W, g_small, M, V)]
    d_s, nm_s, nv_s = _adamw(*packed, name="adamw_small", tm=own_rows)
    for mat, dst in ((packed[1], out_g), (d_s, out_d), (nm_s, out_m), (nv_s, out_v)):
        for n, a in zip(small_names, _unpack_vec(mat, own_shapes)):
            dst[n] = a

    outs = [loss, dx[None]]
    for dst in (out_g, out_d, out_m, out_v):
        outs += [dst[n] for n in WEIGHTS]
    return tuple(outs)
```

```python
import functools

import jax
import jax.numpy as jnp
import numpy as np
from jax import lax
from jax.experimental import pallas as pl
from jax.experimental.pallas import tpu as pltpu

F32 = jnp.float32
BF16 = jnp.bfloat16
MESH = pl.DeviceIdType.MESH

NORM_EPS = 1e-6
GDN_HEADS = 4
GDN_CHUNK = 64
GDN_CONV = 4
GDN_SCAN_CHUNKS = 4
SB_HEADS = 4
SB_DEAD_LOG = -110.0
SC_KERNEL = 3
FFN_CONV = 3
ADAM_LR = 0.001
ADAM_B1 = 0.9
ADAM_B2 = 0.999
ADAM_EPS = 1e-08
ADAM_WD = 0.01
ADAM_STEP = 10

VMEM_LIMIT_BYTES = 48 * 1024 * 1024
HALO = 8
CONV_SLAB = 128
LANES = 128
N_CHIPS = 4


def _pc(body, *, name, grid, in_specs, out_specs, out_shape, scratch_shapes=(), dims=None, aliases=None):
    params = dict(vmem_limit_bytes=VMEM_LIMIT_BYTES)
    if dims is not None:
        params["dimension_semantics"] = dims
    return pl.pallas_call(body, name=name, grid=grid, in_specs=in_specs, out_specs=out_specs, out_shape=out_shape,
                          scratch_shapes=list(scratch_shapes), input_output_aliases=aliases or {},
                          compiler_params=pltpu.CompilerParams(**params))


def _pc_prefetch(body, *, name, grid_spec, out_shape, dims):
    return pl.pallas_call(body, name=name, grid_spec=grid_spec, out_shape=out_shape,
                          compiler_params=pltpu.CompilerParams(vmem_limit_bytes=VMEM_LIMIT_BYTES,
                                                               dimension_semantics=dims))


def _pc_comm(body, *, name, in_specs, out_specs, out_shape, scratch_shapes):
    return pl.pallas_call(body, name=name, in_specs=in_specs, out_specs=out_specs, out_shape=out_shape,
                          scratch_shapes=list(scratch_shapes),
                          compiler_params=pltpu.CompilerParams(vmem_limit_bytes=VMEM_LIMIT_BYTES))


_DIMS = {"nn": (((1,), (0,)), ((), ())), "nt": (((1,), (1,)), ((), ())), "tn": (((0,), (0,)), ((), ()))}


def _matmul(a, b, mode, *, name, tm=512, tn=512, tk=512, out_dtype=F32, res=None, slabs=None):
    if mode == "nn":
        (M, K), (K2, N) = a.shape, b.shape
    elif mode == "nt":
        (M, K), (N, K2) = a.shape, b.shape
    else:
        (K, M), (K2, N) = a.shape, b.shape
    assert K == K2, (a.shape, b.shape, mode)
    tm, tn, tk = min(tm, M), min(tn, N), min(tk, K)
    assert M % tm == 0 and N % tn == 0 and K % tk == 0, (M, N, K, tm, tn, tk)
    nk = K // tk
    if mode == "tn":
        a_spec = pl.BlockSpec((tk, tm), lambda i, j, k: (k, i))
    else:
        a_spec = pl.BlockSpec((tm, tk), lambda i, j, k: (i, k))
    if mode == "nt":
        b_spec = pl.BlockSpec((tn, tk), lambda i, j, k: (j, k))
    else:
        b_spec = pl.BlockSpec((tk, tn), lambda i, j, k: (k, j))
    o_spec = pl.BlockSpec((tm, tn), lambda i, j, k: (i, j))
    has_res = res is not None
    dn = _DIMS[mode]

    def body(*refs):
        if has_res:
            a_ref, b_ref, r_ref, o_ref, acc = refs
        else:
            a_ref, b_ref, o_ref, acc = refs
        k = pl.program_id(2)
        p = lax.dot_general(a_ref[...].astype(BF16), b_ref[...].astype(BF16), dn, preferred_element_type=F32)

        def finish(total):
            if has_res:
                total = total + r_ref[...].astype(F32)
            o_ref[...] = total.astype(out_dtype)

        if nk == 1:
            finish(p)
        else:
            @pl.when(k == 0)
            def _():
                acc[...] = p

            @pl.when(k > 0)
            def _():
                acc[...] += p

            @pl.when(k == nk - 1)
            def _():
                finish(acc[...])

    in_specs = [a_spec, b_spec] + ([o_spec] if has_res else [])
    args = (a, b) + ((res,) if has_res else ())
    out_shape = jax.ShapeDtypeStruct((M, N), out_dtype)
    aliases = None
    if slabs is not None:
        n_slabs, first, into = slabs
        assert not has_res and tm == M
        o_spec = pl.BlockSpec((None, tm, tn), lambda i, j, k: (j + first, i, 0))
        out_shape = jax.ShapeDtypeStruct((n_slabs, M, tn), out_dtype)
        if into is not None:
            in_specs.append(pl.BlockSpec(memory_space=pl.ANY))
            args = args + (into,)
            aliases = {2: 0}
            inner = body

            def body(a_ref, b_ref, into_ref, o_ref, acc):
                inner(a_ref, b_ref, o_ref, acc)
    return _pc(body, name=name, grid=(M // tm, N // tn, nk), in_specs=in_specs, out_specs=o_spec,
               out_shape=out_shape, scratch_shapes=[pltpu.VMEM((tm, tn), F32)],
               dims=("parallel", "parallel", "arbitrary"), aliases=aliases)(*args)


def _offsets(parts, own_width_aligned):
    offs, at = [], 0
    for p in parts:
        assert at % (p.shape[1] if own_width_aligned else LANES) == 0, (at, p.shape)
        offs.append(at)
        at += p.shape[1]
    return offs, at


def _matmul_rows_parts(parts, w, mode, *, name, tm=512, res=None):
    M = parts[0].shape[0]
    offs, K = _offsets(parts, True)
    tm = min(tm, M)
    N = w.shape[1] if mode == "nn" else w.shape[0]
    assert (w.shape[0] if mode == "nn" else w.shape[1]) == K
    has_res = res is not None
    n = len(parts)

    def body(*refs):
        o_ref = refs[-1]
        total = None
        for s in range(n):
            p = lax.dot_general(refs[s][...].astype(BF16), refs[n + s][...].astype(BF16), _DIMS[mode],
                                preferred_element_type=F32)
            total = p if total is None else total + p
        if has_res:
            total = total + refs[2 * n][...]
        o_ref[...] = total

    in_specs = [pl.BlockSpec((tm, p.shape[1]), lambda i: (i, 0)) for p in parts]
    for p, off in zip(parts, offs):
        blk = off // p.shape[1]
        if mode == "nn":
            in_specs.append(pl.BlockSpec((p.shape[1], N), lambda i, blk=blk: (blk, 0)))
        else:
            in_specs.append(pl.BlockSpec((N, p.shape[1]), lambda i, blk=blk: (0, blk)))
    o_spec = pl.BlockSpec((tm, N), lambda i: (i, 0))
    args = tuple(parts) + (w,) * n + ((res,) if has_res else ())
    return _pc(body, name=name, grid=(M // tm,), in_specs=in_specs + ([o_spec] if has_res else []), out_specs=o_spec,
               out_shape=jax.ShapeDtypeStruct((M, N), F32), dims=("parallel",))(*args)


def _matmul_tn_parts(a, b, *, name, tk=1024):
    a_parts = list(a) if isinstance(a, (list, tuple)) else [a]
    b_parts = list(b) if isinstance(b, (list, tuple)) else [b]
    assert len(a_parts) == 1 or len(b_parts) == 1
    a_offs, M = _offsets(a_parts, False)
    b_offs, N = _offsets(b_parts, False)
    K = a_parts[0].shape[0]
    tk = min(tk, K)
    na, nb = len(a_parts), len(b_parts)

    def body(*refs):
        o_ref = refs[-1]
        first = pl.program_id(0) == 0
        for s in range(na):
            for t in range(nb):
                p = lax.dot_general(refs[s][...].astype(BF16), refs[na + t][...].astype(BF16), _DIMS["tn"],
                                    preferred_element_type=F32)
                rows = slice(a_offs[s], a_offs[s] + a_parts[s].shape[1])
                cols = slice(b_offs[t], b_offs[t] + b_parts[t].shape[1])

                @pl.when(first)
                def _(p=p, rows=rows, cols=cols):
                    o_ref[rows, cols] = p

                @pl.when(jnp.logical_not(first))
                def _(p=p, rows=rows, cols=cols):
                    o_ref[rows, cols] += p

    in_specs = [pl.BlockSpec((tk, p.shape[1]), lambda k: (k, 0)) for p in a_parts + b_parts]
    return _pc(body, name=name, grid=(K // tk,), in_specs=in_specs, out_specs=pl.BlockSpec((M, N), lambda k: (0, 0)),
               out_shape=jax.ShapeDtypeStruct((M, N), F32), dims=("arbitrary",))(*a_parts, *b_parts)


def _rmsnorm_fwd(x, w, *, name, tm=512):
    L, D = x.shape
    tm = min(tm, L)

    def body(x_ref, w_ref, h_ref):
        xv = x_ref[...]
        r = lax.rsqrt(jnp.mean(xv * xv, axis=-1, keepdims=True) + NORM_EPS)
        h_ref[...] = (xv * r * w_ref[...]).astype(BF16)

    return _pc(body, name=name, grid=(L // tm,),
               in_specs=[pl.BlockSpec((tm, D), lambda i: (i, 0)), pl.BlockSpec((1, D), lambda i: (0, 0))],
               out_specs=pl.BlockSpec((tm, D), lambda i: (i, 0)), out_shape=jax.ShapeDtypeStruct((L, D), BF16),
               dims=("parallel",))(x, w.reshape(1, D))


def _rmsnorm_bwd(dh, x, w, dres, *, name, tm=512):
    L, D = x.shape
    tm = min(tm, L)

    def body(dh_ref, x_ref, w_ref, dres_ref, dx_ref, dw_ref):
        xv = x_ref[...]
        r = lax.rsqrt(jnp.mean(xv * xv, axis=-1, keepdims=True) + NORM_EPS)
        xhat = xv * r
        dhv = dh_ref[...]
        g = dhv * w_ref[...]
        dx_ref[...] = dres_ref[...] + r * (g - xhat * jnp.mean(g * xhat, axis=-1, keepdims=True))
        part = jnp.sum(dhv * xhat, axis=0, keepdims=True)

        @pl.when(pl.program_id(0) == 0)
        def _():
            dw_ref[...] = part

        @pl.when(pl.program_id(0) > 0)
        def _():
            dw_ref[...] += part

    row = pl.BlockSpec((tm, D), lambda i: (i, 0))
    vec = pl.BlockSpec((1, D), lambda i: (0, 0))
    return _pc(body, name=name, grid=(L // tm,), in_specs=[row, row, vec, row], out_specs=[row, vec],
               out_shape=[jax.ShapeDtypeStruct((L, D), F32), jax.ShapeDtypeStruct((1, D), F32)],
               dims=("arbitrary",))(dh, x, w.reshape(1, D), dres)


def _final_loss(x, w, tgt, *, name, tm=512):
    L, D = x.shape
    tm = min(tm, L)

    def body(x_ref, w_ref, t_ref, loss_ref, dx_ref, dw_ref):
        xv = x_ref[...]
        r = lax.rsqrt(jnp.mean(xv * xv, axis=-1, keepdims=True) + NORM_EPS)
        xhat = xv * r
        e = xhat * w_ref[...] - t_ref[...]
        lpart = jnp.broadcast_to(0.5 * jnp.sum(jnp.mean(e * e, axis=-1, keepdims=True), axis=0, keepdims=True),
                                 (1, LANES))
        dy = e * (1.0 / D)
        g = dy * w_ref[...]
        dx_ref[...] = r * (g - xhat * jnp.mean(g * xhat, axis=-1, keepdims=True))
        part = jnp.sum(dy * xhat, axis=0, keepdims=True)

        @pl.when(pl.program_id(0) == 0)
        def _():
            dw_ref[...] = part
            loss_ref[...] = lpart

        @pl.when(pl.program_id(0) > 0)
        def _():
            dw_ref[...] += part
            loss_ref[...] += lpart

    row = pl.BlockSpec((tm, D), lambda i: (i, 0))
    vec = pl.BlockSpec((1, D), lambda i: (0, 0))
    lsp = pl.BlockSpec((1, LANES), lambda i: (0, 0))
    return _pc(body, name=name, grid=(L // tm,), in_specs=[row, vec, row], out_specs=[lsp, row, vec],
               out_shape=[jax.ShapeDtypeStruct((1, LANES), F32), jax.ShapeDtypeStruct((L, D), F32),
                          jax.ShapeDtypeStruct((1, D), F32)],
               dims=("arbitrary",))(x, w.reshape(1, D), tgt)


def _silu(x):
    return x * jax.nn.sigmoid(x)


def _conv_pointwise_fwd(xs, ws, es, pre, post, outs, *, tc, tm, name):
    L = xs[0][0].shape[0]
    tm = min(tm, L)
    ncol = outs[0][0] // tc
    nrow = L // tm
    hb = tm // HALO
    nx, nw, ne, no = len(xs), len(ws), len(es), len(outs)
    K = ws[0][0].shape[0]

    slab = min(CONV_SLAB, tm)
    win = slab + HALO
    assert tm % slab == 0 and K - 1 <= HALO

    def body(*refs):
        i = pl.program_id(1)
        first = (i > 0).astype(F32)
        n_in = 2 * nx + nw + ne
        o_refs = refs[n_in:n_in + no]
        x_pads = refs[n_in + no:]
        for n in range(nx):
            x_pads[n][0:HALO, :] = refs[2 * n + 1][...] * first
            x_pads[n][HALO:, :] = refs[2 * n][...]
        wv = [refs[2 * nx + n][...] for n in range(nw)]
        e_refs = refs[2 * nx + nw:n_in]

        @pl.loop(0, tm // slab)
        def _(t):
            r0 = pl.multiple_of(t * slab, HALO)
            ps = pre(*[x_pads[n][pl.ds(r0, win), :] for n in range(nx)])
            us = []
            for n in range(nw):
                u = None
                for k in range(K):
                    term = wv[n][k:k + 1, :] * (ps[n] if k == K - 1 else pltpu.roll(ps[n], K - 1 - k, 0))
                    u = term if u is None else u + term
                us.append(u[HALO:])
            rows = pl.ds(r0, slab)
            for o_ref, val in zip(o_refs, post(us, [e[rows, :] for e in e_refs])):
                o_ref[rows, :] = val.astype(o_ref.dtype)

    in_specs, args = [], []
    for arr, c0 in xs:
        off = c0 // tc
        in_specs.append(pl.BlockSpec((tm, tc), lambda j, i, off=off: (i, j + off)))
        in_specs.append(pl.BlockSpec((HALO, tc), lambda j, i, off=off: (jnp.maximum(i * hb - 1, 0), j + off)))
        args += [arr, arr]
    for arr, c0 in ws:
        off = c0 // tc
        in_specs.append(pl.BlockSpec((K, tc), lambda j, i, off=off: (0, j + off)))
        args.append(arr)
    for arr, c0 in es:
        off = c0 // tc
        in_specs.append(pl.BlockSpec((tm, tc), lambda j, i, off=off: (i, j + off)))
        args.append(arr)
    out_specs = [pl.BlockSpec((tm, tc), lambda j, i: (i, j)) for _ in range(no)]
    out_shape = [jax.ShapeDtypeStruct((L, c), dt) for c, dt in outs]
    return _pc(body, name=name, grid=(ncol, nrow), in_specs=in_specs, out_specs=out_specs, out_shape=out_shape,
               scratch_shapes=[pltpu.VMEM((tm + HALO, tc), F32)] * nx, dims=("parallel", "parallel"))(*args)


def _conv_pointwise_bwd(xs, ws, es, dys, pre, post, width, *, tc, tm, name, out_dtype=BF16):
    L = xs[0][0].shape[0]
    tm = min(tm, L)
    ncol = width // tc
    nrow = L // tm
    hb = tm // HALO
    nx, nw, ne, ny = len(xs), len(ws), len(es), len(dys)
    K = ws[0][0].shape[0]

    slab = min(CONV_SLAB, tm)
    win = slab + 2 * HALO
    assert tm % slab == 0 and K - 1 <= HALO

    def body(*refs):
        i = pl.program_id(1)
        first = (i > 0).astype(F32)
        more = (i < nrow - 1).astype(F32)
        n_in = 3 * nx + nw + 2 * ne + 2 * ny
        n_out = nx + ne + nw
        dx_refs = refs[n_in:n_in + nx]
        de_refs = refs[n_in + nx:n_in + nx + ne]
        dw_refs = refs[n_in + nx + ne:n_in + n_out]
        pads = refs[n_in + n_out:]
        x_pads, e_pads, dy_pads = pads[:nx], pads[nx:nx + ne], pads[nx + ne:]
        pos = 0
        for n in range(nx):
            x_pads[n][0:HALO, :] = refs[pos + 1][...] * first
            x_pads[n][HALO:HALO + tm, :] = refs[pos][...]
            x_pads[n][HALO + tm:, :] = refs[pos + 2][...]
            pos += 3
        wv = [refs[pos + n][...] for n in range(nw)]
        pos += nw
        for n in range(ne):
            e_pads[n][0:HALO, :] = jnp.zeros((HALO, tc), F32)
            e_pads[n][HALO:HALO + tm, :] = refs[pos][...]
            e_pads[n][HALO + tm:, :] = refs[pos + 1][...]
            pos += 2
        for n in range(ny):
            dy_pads[n][0:HALO, :] = jnp.zeros((HALO, tc), F32)
            dy_pads[n][HALO:HALO + tm, :] = refs[pos][...].astype(F32)
            dy_pads[n][HALO + tm:, :] = refs[pos + 1][...].astype(F32) * more
            pos += 2

        def one_slab(t, dw_acc):
            r0 = pl.multiple_of(t * slab, HALO)
            xw = [x_pads[n][pl.ds(r0, win), :] for n in range(nx)]
            ew = [e_pads[n][pl.ds(r0, win), :] for n in range(ne)]
            dyw = [dy_pads[n][pl.ds(r0, win), :] for n in range(ny)]
            ps, pre_vjp = jax.vjp(lambda *x_: pre(*x_), *xw)
            shifted = [[p if k == K - 1 else pltpu.roll(p, K - 1 - k, 0) for k in range(K)] for p in ps]
            us = []
            for n in range(nw):
                u = None
                for k in range(K):
                    term = wv[n][k:k + 1, :] * shifted[n][k]
                    u = term if u is None else u + term
                us.append(u)
            _, post_vjp = jax.vjp(lambda u_, e_: post(u_, e_), us, ew)
            dus, des = post_vjp(dyw)
            dps, dw_new = [], []
            for n in range(nw):
                dp = None
                for k in range(K):
                    term = wv[n][k:k + 1, :] * (dus[n] if k == K - 1 else pltpu.roll(dus[n], win - (K - 1 - k), 0))
                    dp = term if dp is None else dp + term
                dps.append(dp)
                inner = dus[n][HALO:HALO + slab]
                dw_new.append([dw_acc[n][k] + jnp.sum(inner * shifted[n][k][HALO:HALO + slab], axis=0, keepdims=True)
                               for k in range(K)])
            dxs = pre_vjp(dps)
            rows = pl.ds(r0, slab)
            for r, v in zip(dx_refs, dxs):
                r[rows, :] = v[HALO:HALO + slab].astype(out_dtype)
            for r, v in zip(de_refs, des):
                r[rows, :] = v[HALO:HALO + slab].astype(out_dtype)
            return dw_new

        zero = [[jnp.zeros((1, tc), F32) for _ in range(K)] for _ in range(nw)]
        dw_tile = lax.fori_loop(0, tm // slab, one_slab, zero)
        for n in range(nw):
            for k in range(K):
                @pl.when(i == 0)
                def _(n=n, k=k):
                    dw_refs[n][k:k + 1, :] = dw_tile[n][k]

                @pl.when(i > 0)
                def _(n=n, k=k):
                    dw_refs[n][k:k + 1, :] += dw_tile[n][k]

    in_specs, args = [], []

    def add_rows(arr, c0, prev, nxt):
        off = c0 // tc
        in_specs.append(pl.BlockSpec((tm, tc), lambda j, i, off=off: (i, j + off)))
        args.append(arr)
        if prev:
            in_specs.append(pl.BlockSpec((HALO, tc), lambda j, i, off=off: (jnp.maximum(i * hb - 1, 0), j + off)))
            args.append(arr)
        if nxt:
            last = L // HALO - 1
            in_specs.append(pl.BlockSpec((HALO, tc), lambda j, i, off=off: (jnp.minimum((i + 1) * hb, last), j + off)))
            args.append(arr)

    for arr, c0 in xs:
        add_rows(arr, c0, True, True)
    for arr, c0 in ws:
        off = c0 // tc
        in_specs.append(pl.BlockSpec((K, tc), lambda j, i, off=off: (0, j + off)))
        args.append(arr)
    for arr, c0 in es:
        add_rows(arr, c0, False, True)
    for arr, c0 in dys:
        add_rows(arr, c0, False, True)
    tile = pl.BlockSpec((tm, tc), lambda j, i: (i, j))
    wtile = pl.BlockSpec((K, tc), lambda j, i: (0, j))
    out_specs = [tile] * (nx + ne) + [wtile] * nw
    out_shape = [jax.ShapeDtypeStruct((L, width), out_dtype)] * (nx + ne) + \
        [jax.ShapeDtypeStruct((K, width), F32)] * nw
    res = _pc(body, name=name, grid=(ncol, nrow), in_specs=in_specs, out_specs=out_specs, out_shape=out_shape,
              scratch_shapes=[pltpu.VMEM((tm + 2 * HALO, tc), F32)] * (nx + ne + ny),
              dims=("parallel", "arbitrary"))(*args)
    return res[:nx], res[nx:nx + ne], res[nx + ne:]


def _pre_identity(*x):
    return list(x)


def _pre_product(c, h):
    return [c * h]


def _post_silu(us, es):
    return [_silu(us[0])]


def _post_gate_mul(us, es):
    return [es[0] * us[0]]


def _post_swiglu(us, es):
    return [_silu(us[0]) * us[1]]


def _make_dot(passes):
    def raw(a, b, dn):
        a_hi = a.astype(BF16)
        b_hi = b.astype(BF16)
        out = lax.dot_general(a_hi, b_hi, dn, preferred_element_type=F32)
        if passes == 3:
            a_lo = (a - a_hi.astype(F32)).astype(BF16)
            b_lo = (b - b_hi.astype(F32)).astype(BF16)
            out = out + lax.dot_general(a_hi, b_lo, dn, preferred_element_type=F32)
            out = out + lax.dot_general(a_lo, b_hi, dn, preferred_element_type=F32)
        return out

    @jax.custom_vjp
    def nn(a, b):
        return raw(a, b, _DIMS["nn"])

    @jax.custom_vjp
    def nt(a, b):
        return raw(a, b, _DIMS["nt"])

    @jax.custom_vjp
    def tn(a, b):
        return raw(a, b, _DIMS["tn"])

    nn.defvjp(lambda a, b: (nn(a, b), (a, b)), lambda r, g: (nt(g, r[1]), tn(r[0], g)))
    nt.defvjp(lambda a, b: (nt(a, b), (a, b)), lambda r, g: (nn(g, r[1]), tn(g, r[0])))
    tn.defvjp(lambda a, b: (tn(a, b), (a, b)), lambda r, g: (nt(r[1], g), nn(r[0], g)))
    return nn, nt, tn


_NN1, _NT1, _TN1 = _make_dot(1)
_NN3, _NT3, _TN3 = _make_dot(3)


def _l2norm(x):
    return x * lax.rsqrt(jnp.sum(x * x, axis=-1, keepdims=True) + NORM_EPS)


def _unit_lower_inverse_raw(a_list):
    C = a_list[0].shape[0]
    ii = lax.broadcasted_iota(jnp.int32, (C, C), 0)
    jj = lax.broadcasted_iota(jnp.int32, (C, C), 1)
    eye = jnp.where(ii == jj, 1.0, 0.0)
    ts = [eye - a for a in a_list]
    ps = list(a_list)
    n = 2
    while n < C:
        ps = [_NN3(p, p) for p in ps]
        ts = [t + _NN3(t, p) for t, p in zip(ts, ps)]
        n *= 2
    return ts


@jax.custom_vjp
def _unit_lower_inverse(a_list):
    return _unit_lower_inverse_raw(a_list)


def _unit_lower_inverse_fwd(a_list):
    ts = _unit_lower_inverse_raw(a_list)
    return ts, ts


def _unit_lower_inverse_bwd(ts, gs):
    xs = [_TN3(t, g) for t, g in zip(ts, gs)]
    return ([-_NT3(x, t) for x, t in zip(xs, ts)],)


_unit_lower_inverse.defvjp(_unit_lower_inverse_fwd, _unit_lower_inverse_bwd)


def _gdn_prep(units):
    C, Dh = units[0][0].shape
    ii = lax.broadcasted_iota(jnp.int32, (C, C), 0)
    jj = lax.broadcasted_iota(jnp.int32, (C, C), 1)
    lane = lax.broadcasted_iota(jnp.int32, (1, C), 1)
    causal = ii >= jj
    strict = ii > jj
    qs = [_l2norm(un[0]) * (Dh ** -0.5) for un in units]
    ks = [_l2norm(un[1]) for un in units]
    betas = [jax.nn.sigmoid(un[4]) for un in units]
    gs = [-jnp.exp(un[5]) * jax.nn.softplus(un[3] + un[6]) for un in units]
    gc_rows = [jnp.sum(jnp.where(ii <= jj, g, 0.0), axis=0, keepdims=True) for g in gs]
    gc_cols = [jnp.sum(jnp.where(ii == jj, r, 0.0), axis=1, keepdims=True) for r in gc_rows]
    decays = [jnp.where(causal, jnp.exp(jnp.where(causal, c - r, 0.0)), 0.0) for c, r in zip(gc_cols, gc_rows)]
    kbs = [k * b for k, b in zip(ks, betas)]
    kks = [_NT1(kb, k) for kb, k in zip(kbs, ks)]
    qks = [_NT1(q, k) for q, k in zip(qs, ks)]
    ts = _unit_lower_inverse([jnp.where(strict, kk * d, 0.0) for kk, d in zip(kks, decays)])
    eg_cols = [jnp.exp(c) for c in gc_cols]
    uws = [_NN3(t, jnp.concatenate([un[2] * b, kb * e], axis=1))
           for t, un, b, kb, e in zip(ts, units, betas, kbs, eg_cols)]
    out = []
    for q, k, qk, d, uw, e, r, c in zip(qs, ks, qks, decays, uws, eg_cols, gc_rows, gc_cols):
        g_last = jnp.sum(jnp.where(lane == C - 1, r, 0.0), axis=1, keepdims=True)
        out.append((q * e, k * jnp.exp(g_last - c), uw[:, :Dh], uw[:, Dh:], jnp.where(causal, qk * d, 0.0),
                    jnp.broadcast_to(jnp.exp(g_last), (1, Dh))))
    return out


def _gdn_step(units):
    v_news = [un[3] - _NN1(un[4], un[0]) for un in units]
    o_state = [_NN1(un[1], un[0]) for un in units]
    o_intra = [_NN1(un[5], vn) for un, vn in zip(units, v_news)]
    s_adds = [_TN1(un[2], vn) for un, vn in zip(units, v_news)]
    out = []
    for un, a, b, s_add in zip(units, o_state, o_intra, s_adds):
        o = a + b
        y = o * lax.rsqrt(jnp.mean(o * o, axis=-1, keepdims=True) + NORM_EPS) * un[8] * _silu(un[7])
        out.append((y, un[0] * un[6] + s_add))
    return out


def _gdn_prep_fwd(qkv, gab, gab_col, a_log, dt_bias, *, name, chunks=4):
    L = qkv.shape[0]
    H, C = GDN_HEADS, GDN_CHUNK
    W = qkv.shape[1] // 3
    Dh = W // H
    N = L // C
    chunks = min(chunks, N)
    R = chunks * C
    gab_off = gab_col // LANES

    def body(q_ref, k_ref, v_ref, gab_ref, al_ref, dt_ref, qe_ref, ke_ref, u_ref, w_ref, at_ref, eg_ref):
        where = [(cc, h) for cc in range(chunks) for h in range(H)]
        units = []
        for cc, h in where:
            rows, sl = slice(cc * C, (cc + 1) * C), slice(h * Dh, (h + 1) * Dh)
            units.append((q_ref[rows, sl], k_ref[rows, sl], v_ref[rows, sl], gab_ref[rows, h:h + 1],
                          gab_ref[rows, H + h:H + h + 1], al_ref[h], dt_ref[h]))
        for (cc, h), (qe, ke, u, w, attn, eg) in zip(where, _gdn_prep(units)):
            rows, sl = slice(cc * C, (cc + 1) * C), slice(h * Dh, (h + 1) * Dh)
            qe_ref[rows, sl] = qe
            ke_ref[rows, sl] = ke
            u_ref[rows, sl] = u
            w_ref[rows, sl] = w
            at_ref[h, rows, :] = attn
            eg_ref[cc, h:h + 1, :] = eg

    col = lambda c: pl.BlockSpec((R, W), lambda n, c=c: (n, c))
    tok = pl.BlockSpec((R, LANES), lambda n: (n, gab_off))
    par = pl.BlockSpec((H, 1, 1), lambda n: (0, 0, 0))
    wide = pl.BlockSpec((R, W), lambda n: (n, 0))
    return _pc(body, name=name, grid=(N // chunks,), in_specs=[col(0), col(1), col(2), tok, par, par],
               out_specs=[wide, wide, wide, wide, pl.BlockSpec((H, R, C), lambda n: (0, n, 0)),
                          pl.BlockSpec((chunks, H, Dh), lambda n: (n, 0, 0))],
               out_shape=[jax.ShapeDtypeStruct((L, W), F32)] * 4 + [jax.ShapeDtypeStruct((H, L, C), F32),
                                                                   jax.ShapeDtypeStruct((N, H, Dh), F32)],
               dims=("parallel",))(qkv, qkv, qkv, gab, a_log, dt_bias)


def _gdn_prep_bwd(qkv, gab, gab_col, a_log, dt_bias, dqe, dke, du, dw, dattn, deg, gab_width, *, name, chunks=4):
    L = qkv.shape[0]
    H, C = GDN_HEADS, GDN_CHUNK
    W = qkv.shape[1] // 3
    Dh = W // H
    N = L // C
    chunks = min(chunks, N)
    R = chunks * C
    gab_off = gab_col // LANES

    def body(q_ref, k_ref, v_ref, gab_ref, al_ref, dt_ref, dqe_ref, dke_ref, du_ref, dw_ref, dat_ref, deg_ref,
             dqkv_ref, dgab_ref, dal_ref, ddt_ref):
        first = pl.program_id(0) == 0
        lane = lax.broadcasted_iota(jnp.int32, (C, gab_width), 1)
        dal_sum, ddt_sum = [None] * H, [None] * H
        where = [(cc, h) for cc in range(chunks) for h in range(H)]
        units, cots = [], []
        for cc, h in where:
            rows, sl = slice(cc * C, (cc + 1) * C), slice(h * Dh, (h + 1) * Dh)
            units.append((q_ref[rows, sl], k_ref[rows, sl], v_ref[rows, sl], gab_ref[rows, h:h + 1],
                          gab_ref[rows, H + h:H + h + 1], al_ref[h], dt_ref[h]))
            cots.append((dqe_ref[rows, sl], dke_ref[rows, sl], du_ref[rows, sl], dw_ref[rows, sl],
                         dat_ref[h, rows, :], deg_ref[cc, h:h + 1, :]))
        _, vjp = jax.vjp(_gdn_prep, units)
        (d_units,) = vjp(cots)
        dgabs = [jnp.zeros((C, gab_width), F32) for _ in range(chunks)]
        for (cc, h), (dq, dk, dv, dga, dgb, dal, ddt) in zip(where, d_units):
            rows = slice(cc * C, (cc + 1) * C)
            dqkv_ref[rows, h * Dh:(h + 1) * Dh] = dq
            dqkv_ref[rows, W + h * Dh:W + (h + 1) * Dh] = dk
            dqkv_ref[rows, 2 * W + h * Dh:2 * W + (h + 1) * Dh] = dv
            dgabs[cc] = dgabs[cc] + jnp.where(lane == h, dga, 0.0) + jnp.where(lane == H + h, dgb, 0.0)
            dal_sum[h] = dal if dal_sum[h] is None else dal_sum[h] + dal
            ddt_sum[h] = ddt if ddt_sum[h] is None else ddt_sum[h] + ddt
        for cc in range(chunks):
            dgab_ref[cc * C:(cc + 1) * C, :] = dgabs[cc].astype(BF16)

        @pl.when(first)
        def _():
            for h in range(H):
                dal_ref[h] = dal_sum[h]
                ddt_ref[h] = ddt_sum[h]

        @pl.when(jnp.logical_not(first))
        def _():
            for h in range(H):
                dal_ref[h] += dal_sum[h]
                ddt_ref[h] += ddt_sum[h]

    col = lambda c: pl.BlockSpec((R, W), lambda n, c=c: (n, c))
    tok = pl.BlockSpec((R, LANES), lambda n: (n, gab_off))
    par = pl.BlockSpec((H, 1, 1), lambda n: (0, 0, 0))
    wide = pl.BlockSpec((R, W), lambda n: (n, 0))
    att = pl.BlockSpec((H, R, C), lambda n: (0, n, 0))
    egs = pl.BlockSpec((chunks, H, Dh), lambda n: (n, 0, 0))
    return _pc(body, name=name, grid=(N // chunks,),
               in_specs=[col(0), col(1), col(2), tok, par, par, wide, wide, wide, wide, att, egs],
               out_specs=[pl.BlockSpec((R, 3 * W), lambda n: (n, 0)), pl.BlockSpec((R, gab_width), lambda n: (n, 0)),
                          par, par],
               out_shape=[jax.ShapeDtypeStruct((L, 3 * W), F32), jax.ShapeDtypeStruct((L, gab_width), BF16)]
               + [jax.ShapeDtypeStruct((H, 1, 1), F32)] * 2,
               dims=("arbitrary",))(qkv, qkv, qkv, gab, a_log, dt_bias, dqe, dke, du, dw, dattn, deg)


def _gdn_scan_fwd(qe, ke, u, w, attn, eg, gz, gz_col, wgn, *, name):
    L, W = qe.shape
    H, C = GDN_HEADS, GDN_CHUNK
    Dh = W // H
    N = L // C
    gz_off = gz_col // W
    cps = min(GDN_SCAN_CHUNKS, N)
    R = cps * C

    def body(qe_ref, ke_ref, u_ref, w_ref, at_ref, eg_ref, gz_ref, wgn_ref, y_ref, st_ref, s_scr):
        @pl.when(pl.program_id(0) == 0)
        def _():
            s_scr[...] = jnp.zeros_like(s_scr)

        S = [s_scr[h] for h in range(H)]
        for cc in range(cps):
            rows = slice(cc * C, (cc + 1) * C)
            units = []
            for h in range(H):
                sl = slice(h * Dh, (h + 1) * Dh)
                st_ref[cc, h] = S[h]
                units.append((S[h], qe_ref[rows, sl], ke_ref[rows, sl], u_ref[rows, sl], w_ref[rows, sl],
                              at_ref[h, rows, :], eg_ref[cc, h:h + 1, :], gz_ref[rows, sl], wgn_ref[...]))
            for h, (y, S_new) in enumerate(_gdn_step(units)):
                y_ref[rows, h * Dh:(h + 1) * Dh] = y.astype(BF16)
                S[h] = S_new
        for h in range(H):
            s_scr[h] = S[h]

    wide = pl.BlockSpec((R, W), lambda n: (n, 0))
    return _pc(body, name=name, grid=(N // cps,),
               in_specs=[wide, wide, wide, wide, pl.BlockSpec((H, R, C), lambda n: (0, n, 0)),
                         pl.BlockSpec((cps, H, Dh), lambda n: (n, 0, 0)),
                         pl.BlockSpec((R, W), lambda n: (n, gz_off)), pl.BlockSpec((1, Dh), lambda n: (0, 0))],
               out_specs=[wide, pl.BlockSpec((cps, H, Dh, Dh), lambda n: (n, 0, 0, 0))],
               out_shape=[jax.ShapeDtypeStruct((L, W), BF16), jax.ShapeDtypeStruct((N, H, Dh, Dh), F32)],
               scratch_shapes=[pltpu.VMEM((H, Dh, Dh), F32)],
               dims=("arbitrary",))(qe, ke, u, w, attn, eg, gz, wgn.reshape(1, Dh))


def _gdn_scan_bwd(qe, ke, u, w, attn, eg, gz, gz_col, wgn, states, dy, dy_col, *, name):
    L, W = qe.shape
    H, C = GDN_HEADS, GDN_CHUNK
    Dh = W // H
    N = L // C
    gz_off = gz_col // W
    dy_off = dy_col // W
    cps = min(GDN_SCAN_CHUNKS, N)
    R = cps * C
    steps = N // cps

    def body(qe_ref, ke_ref, u_ref, w_ref, at_ref, eg_ref, gz_ref, wgn_ref, st_ref, dy_ref,
             dqe_ref, dke_ref, du_ref, dw_ref, dat_ref, deg_ref, dgz_ref, dwgn_ref, ds_scr):
        first = pl.program_id(0) == 0

        @pl.when(first)
        def _():
            ds_scr[...] = jnp.zeros_like(ds_scr)

        dwgn = None
        dS = [ds_scr[h] for h in range(H)]
        for cc in reversed(range(cps)):
            rows = slice(cc * C, (cc + 1) * C)
            units, cots = [], []
            for h in range(H):
                sl = slice(h * Dh, (h + 1) * Dh)
                units.append((st_ref[cc, h], qe_ref[rows, sl], ke_ref[rows, sl], u_ref[rows, sl], w_ref[rows, sl],
                              at_ref[h, rows, :], eg_ref[cc, h:h + 1, :], gz_ref[rows, sl], wgn_ref[...]))
                cots.append((dy_ref[rows, sl].astype(F32), dS[h]))
            _, vjp = jax.vjp(_gdn_step, units)
            (d_units,) = vjp(cots)
            for h, (dS_h, dqe, dke, du, dw, dat, deg, dgz, dwg) in enumerate(d_units):
                sl = slice(h * Dh, (h + 1) * Dh)
                dS[h] = dS_h
                dqe_ref[rows, sl] = dqe
                dke_ref[rows, sl] = dke
                du_ref[rows, sl] = du
                dw_ref[rows, sl] = dw
                dat_ref[h, rows, :] = dat
                deg_ref[cc, h:h + 1, :] = deg
                dgz_ref[rows, sl] = dgz.astype(BF16)
                dwgn = dwg if dwgn is None else dwgn + dwg
        for h in range(H):
            ds_scr[h] = dS[h]

        @pl.when(first)
        def _():
            dwgn_ref[...] = dwgn

        @pl.when(jnp.logical_not(first))
        def _():
            dwgn_ref[...] += dwgn

    rev = lambda n: steps - 1 - n
    wide = pl.BlockSpec((R, W), lambda n: (rev(n), 0))
    att = pl.BlockSpec((H, R, C), lambda n: (0, rev(n), 0))
    egs = pl.BlockSpec((cps, H, Dh), lambda n: (rev(n), 0, 0))
    vec = pl.BlockSpec((1, Dh), lambda n: (0, 0))
    return _pc(body, name=name, grid=(steps,),
               in_specs=[wide, wide, wide, wide, att, egs, pl.BlockSpec((R, W), lambda n: (rev(n), gz_off)), vec,
                         pl.BlockSpec((cps, H, Dh, Dh), lambda n: (rev(n), 0, 0, 0)),
                         pl.BlockSpec((R, W), lambda n: (rev(n), dy_off))],
               out_specs=[wide, wide, wide, wide, att, egs, wide, vec],
               out_shape=[jax.ShapeDtypeStruct((L, W), F32)] * 4 + [jax.ShapeDtypeStruct((H, L, C), F32),
                                                                   jax.ShapeDtypeStruct((N, H, Dh), F32),
                                                                   jax.ShapeDtypeStruct((L, W), BF16),
                                                                   jax.ShapeDtypeStruct((1, Dh), F32)],
               scratch_shapes=[pltpu.VMEM((H, Dh, Dh), F32)],
               dims=("arbitrary",))(qe, ke, u, w, attn, eg, gz, wgn.reshape(1, Dh), states, dy)


def _sb_scores(z, mask):
    sp = jnp.maximum(z, 0.0) + jnp.log(1.0 + jnp.exp(-jnp.abs(z)))
    lom = -sp if mask is None else jnp.where(mask, -sp, 0.0)
    return lom, z - sp


def _sb_alive(c_a, c_b):
    return jnp.maximum(jnp.max(c_a), jnp.max(c_b)) >= SB_DEAD_LOG


def _sb_masks(tq, width, dh):
    rr = lax.broadcasted_iota(jnp.int32, (tq, tq), 0)
    cc = lax.broadcasted_iota(jnp.int32, (tq, tq), 1)
    first_head = lax.broadcasted_iota(jnp.int32, (tq, width), 1) < dh
    return cc < rr, jnp.where(rr > cc, 1.0, 0.0).astype(BF16), first_head


def _sb_fwd(qkv, *, name, tq=256):
    L = qkv.shape[0]
    H = SB_HEADS
    width = 2 * (qkv.shape[1] // 3 // H)
    dh = width // 2
    npair = H // 2
    tq = min(tq, L)
    nq = L // tq

    def body(q_ref, k_ref, v_ref, o_ref):
        i = pl.program_id(1)
        diag, tri, first_head = _sb_masks(tq, width, dh)
        qp = q_ref[...]
        zero = jnp.zeros_like(qp)
        qs = (jnp.where(first_head, qp, zero), jnp.where(first_head, zero, qp))

        def blocks(js, carry, mask):
            units = [(b, hd) for b in range(len(js)) for hd in range(2)]
            starts = [pl.multiple_of(j * tq, tq) for j in js]
            ks = [k_ref[pl.ds(st, tq), :] for st in starts]
            vs = [v_ref[pl.ds(st, tq), :] for st in starts]
            zs = {(b, hd): lax.dot_general(qs[hd], ks[b], _DIMS["nt"], preferred_element_type=F32)
                  for b, hd in units}
            scores = {un: _sb_scores(zs[un], mask) for un in units}
            later = {un: jnp.dot(scores[un][0].astype(BF16), tri, preferred_element_type=F32) for un in units}
            cs = [carry[hd][0] for hd in range(2)]
            accs = [carry[hd][1] for hd in range(2)]
            for b, hd in units:
                lom, lb = scores[(b, hd)]
                a = jnp.exp(lb + (cs[hd] + later[(b, hd)]))
                if mask is not None:
                    a = jnp.where(mask, a, 0.0)
                accs[hd] = accs[hd] + jnp.dot(a.astype(BF16), vs[b], preferred_element_type=F32)
                cs[hd] = cs[hd] + jnp.sum(lom, axis=1, keepdims=True)
            return tuple((cs[hd], accs[hd]) for hd in range(2))

        init = tuple((jnp.zeros((tq, 1), F32), jnp.zeros((tq, width), F32)) for _ in range(2))
        carry = blocks([i], init, diag)
        j_end, carry = lax.while_loop(lambda st: jnp.logical_and(st[0] >= 0, _sb_alive(st[1][0][0], st[1][1][0])),
                                      lambda st: (st[0] - 1, blocks([st[0]], st[1], None)), (i - 1, carry))
        o_ref[...] = jnp.where(first_head, carry[0][1], carry[1][1]).astype(BF16)

    return _pc(body, name=name, grid=(npair, nq),
               in_specs=[pl.BlockSpec((tq, width), lambda p, i: (i, p)),
                         pl.BlockSpec((L, width), lambda p, i: (0, npair + p)),
                         pl.BlockSpec((L, width), lambda p, i: (0, 2 * npair + p))],
               out_specs=pl.BlockSpec((tq, width), lambda p, i: (i, p)),
               out_shape=jax.ShapeDtypeStruct((L, npair * width), BF16),
               dims=("parallel", "parallel"))(qkv, qkv, qkv)


def _sb_bwd(qkv, do, do_col, scale, *, name, tq=256):
    L = qkv.shape[0]
    H = SB_HEADS
    width = 2 * (qkv.shape[1] // 3 // H)
    dh = width // 2
    npair = H // 2
    tq = min(tq, L)
    nq = L // tq
    do_off = do_col // width

    def body(q_ref, k_ref, v_ref, do_ref, dq_ref, dk_ref, dv_ref):
        i = pl.program_id(1)

        @pl.when(i == 0)
        def _():
            dk_ref[...] = jnp.zeros_like(dk_ref)
            dv_ref[...] = jnp.zeros_like(dv_ref)

        diag, tri_later, first_head = _sb_masks(tq, width, dh)
        rr = lax.broadcasted_iota(jnp.int32, (tq, tq), 0)
        cc = lax.broadcasted_iota(jnp.int32, (tq, tq), 1)
        tri_before = jnp.where(rr < cc, 1.0, 0.0).astype(BF16)
        qp = q_ref[...]
        dop = do_ref[...].astype(BF16)
        zero = jnp.zeros_like(qp)
        qs = (jnp.where(first_head, qp, zero), jnp.where(first_head, zero, qp))
        dos = (jnp.where(first_head, dop, zero), jnp.where(first_head, zero, dop))
        ctots = []

        def blocks(js, carry, mask):
            nb = len(js)
            units = [(b, hd) for b in range(nb) for hd in range(2)]
            starts = [pl.multiple_of(j * tq, tq) for j in js]
            ks = [k_ref[pl.ds(st, tq), :] for st in starts]
            vs = [v_ref[pl.ds(st, tq), :] for st in starts]
            zs = {(b, hd): lax.dot_general(qs[hd], ks[b], _DIMS["nt"], preferred_element_type=F32)
                  for b, hd in units}
            das = {(b, hd): lax.dot_general(dos[hd], vs[b], _DIMS["nt"], preferred_element_type=F32)
                   for b, hd in units}
            scores = {un: _sb_scores(zs[un], mask) for un in units}
            later = {un: jnp.dot(scores[un][0].astype(BF16), tri_later, preferred_element_type=F32) for un in units}
            pcs = [carry[hd][0] for hd in range(2)]
            avals = {}
            for b, hd in units:
                pcs[hd] = pcs[hd] + jnp.sum(scores[(b, hd)][0], axis=1, keepdims=True)
                a = jnp.exp(scores[(b, hd)][1] + ((ctots[hd] - pcs[hd]) + later[(b, hd)]))
                avals[(b, hd)] = a if mask is None else jnp.where(mask, a, 0.0)
            gs = {un: das[un] * avals[un] for un in units}
            before = {un: jnp.dot(gs[un].astype(BF16), tri_before, preferred_element_type=F32) for un in units}
            pgs = [carry[hd][1] for hd in range(2)]
            dzs = {}
            for b, hd in units:
                sig = jnp.exp(scores[(b, hd)][1])
                dz = gs[(b, hd)] * (1.0 - sig) - (pgs[hd] + before[(b, hd)]) * sig
                dzs[(b, hd)] = (dz if mask is None else jnp.where(mask, dz, 0.0)).astype(BF16)
                pgs[hd] = pgs[hd] + jnp.sum(gs[(b, hd)], axis=1, keepdims=True)
            dqs = [carry[hd][2] for hd in range(2)]
            for b, hd in units:
                dqs[hd] = dqs[hd] + jnp.dot(dzs[(b, hd)], ks[b], preferred_element_type=F32)
            for b in range(nb):
                dk_ref[pl.ds(starts[b], tq), :] += sum(
                    lax.dot_general(dzs[(b, hd)], qs[hd], _DIMS["tn"], preferred_element_type=F32) for hd in range(2))
                dv_ref[pl.ds(starts[b], tq), :] += sum(
                    lax.dot_general(avals[(b, hd)].astype(BF16), dos[hd], _DIMS["tn"], preferred_element_type=F32)
                    for hd in range(2))
            return tuple((pcs[hd], pgs[hd], dqs[hd]) for hd in range(2))

        def row_sums(j, mask):
            kj = k_ref[pl.ds(pl.multiple_of(j * tq, tq), tq), :]
            return tuple(jnp.sum(_sb_scores(lax.dot_general(qs[hd], kj, _DIMS["nt"], preferred_element_type=F32),
                                            mask)[0], axis=1, keepdims=True) for hd in range(2))

        j_dead, live_sums = lax.while_loop(
            lambda st: jnp.logical_and(st[0] >= 0, _sb_alive(st[1][0], st[1][1])),
            lambda st: (st[0] - 1, tuple(a + b for a, b in zip(st[1], row_sums(st[0], None)))),
            (i - 1, row_sums(i, diag)))
        ctots.extend(live_sums)
        col = jnp.zeros((tq, 1), F32)
        init = tuple((col, col, jnp.zeros((tq, width), F32)) for _ in range(2))
        carry = lax.fori_loop(j_dead + 1, i, lambda j, cr: blocks([j], cr, None), init)
        carry = blocks([i], carry, diag)
        dq_ref[...] = (jnp.where(first_head, carry[0][2], carry[1][2]) * scale).astype(BF16)

    tile = pl.BlockSpec((tq, width), lambda p, i: (i, p))
    full = pl.BlockSpec((L, width), lambda p, i: (0, p))
    sds = jax.ShapeDtypeStruct((L, npair * width), F32)
    return _pc(body, name=name, grid=(npair, nq),
               in_specs=[tile, pl.BlockSpec((L, width), lambda p, i: (0, npair + p)),
                         pl.BlockSpec((L, width), lambda p, i: (0, 2 * npair + p)),
                         pl.BlockSpec((tq, width), lambda p, i: (i, do_off + p))],
               out_specs=[tile, full, full],
               out_shape=[jax.ShapeDtypeStruct((L, npair * width), BF16), sds, sds],
               dims=("parallel", "arbitrary"))(qkv, qkv, qkv, do)


def _adamw(w, g, m, v, *, name, tm=256):
    R, C = w.shape
    tm = min(tm, R)
    assert R % tm == 0, (R, tm)
    c1 = 1.0 - ADAM_B1 ** ADAM_STEP
    c2 = 1.0 - ADAM_B2 ** ADAM_STEP

    def body(w_ref, g_ref, m_ref, v_ref, d_ref, nm_ref, nv_ref):
        gv = g_ref[...]
        nm = ADAM_B1 * m_ref[...] + (1.0 - ADAM_B1) * gv
        nv = ADAM_B2 * v_ref[...] + (1.0 - ADAM_B2) * (gv * gv)
        d_ref[...] = -ADAM_LR * ((nm / c1) / (jnp.sqrt(nv / c2) + ADAM_EPS) + ADAM_WD * w_ref[...])
        nm_ref[...] = nm
        nv_ref[...] = nv

    blk = pl.BlockSpec((tm, C), lambda i: (i, 0))
    sds = jax.ShapeDtypeStruct((R, C), F32)
    return _pc(body, name=name, grid=(R // tm,), in_specs=[blk] * 4, out_specs=[blk] * 3, out_shape=[sds] * 3,
               dims=("parallel",))(w, g, m, v)


ELEMENTWISE_BLOCK_BYTES = 1 << 20


def _row_tile(rows, cols):
    for t in (512, 384, 352, 256, 176, 128, 88, 64, 32, 16, 8):
        if rows % t == 0 and t * cols * 4 <= ELEMENTWISE_BLOCK_BYTES:
            return t
    raise ValueError((rows, cols))


def _adamw_layers(w, g_mine, g_other, m, v, c, *, name):
    _, R, C = w.shape
    tm = _row_tile(R, C)
    c1 = 1.0 - ADAM_B1 ** ADAM_STEP
    c2 = 1.0 - ADAM_B2 ** ADAM_STEP

    def body(c_ref, w_ref, gm_ref, go_ref, m_ref, v_ref, g_ref, d_ref, nm_ref, nv_ref):
        gv = jnp.where(pl.program_id(0) == c_ref[0], gm_ref[...], go_ref[...])
        nm = ADAM_B1 * m_ref[...] + (1.0 - ADAM_B1) * gv
        nv = ADAM_B2 * v_ref[...] + (1.0 - ADAM_B2) * (gv * gv)
        g_ref[...] = gv
        d_ref[...] = -ADAM_LR * ((nm / c1) / (jnp.sqrt(nv / c2) + ADAM_EPS) + ADAM_WD * w_ref[...])
        nm_ref[...] = nm
        nv_ref[...] = nv

    slab = pl.BlockSpec((None, tm, C), lambda l, i, c_ref: (l, i, 0))
    mine = pl.BlockSpec((tm, C), lambda l, i, c_ref: (jnp.where(l == c_ref[0], i, 0), 0))
    other = pl.BlockSpec((tm, C), lambda l, i, c_ref: (jnp.where(l == c_ref[0], 0, i), 0))
    grid_spec = pltpu.PrefetchScalarGridSpec(num_scalar_prefetch=1, grid=(2, R // tm),
                                             in_specs=[slab, mine, other, slab, slab], out_specs=[slab] * 4)
    return _pc_prefetch(body, name=name, grid_spec=grid_spec, out_shape=[jax.ShapeDtypeStruct(w.shape, F32)] * 4,
                        dims=("parallel", "parallel"))(c.reshape(1).astype(jnp.int32), w, g_mine, g_other, m, v)


def _add_layers(g0, g1, ra, c, *, name):
    S, R, C = ra.shape
    tm = _row_tile(R, C)

    def body(c_ref, g0_ref, g1_ref, r_ref, o_ref):
        mine = jnp.where(c_ref[0] == 0, g0_ref[...], g1_ref[...])
        o_ref[...] = (mine + r_ref[...]).astype(BF16)

    def walked_if(layer):
        return lambda s, i, c_ref: (jnp.where(c_ref[0] == layer, s, 0), jnp.where(c_ref[0] == layer, i, 0), 0)

    blk = lambda s, i, c_ref: (s, i, 0)
    grid_spec = pltpu.PrefetchScalarGridSpec(
        num_scalar_prefetch=1, grid=(S, R // tm),
        in_specs=[pl.BlockSpec((None, tm, C), walked_if(0)), pl.BlockSpec((None, tm, C), walked_if(1)),
                  pl.BlockSpec((None, tm, C), blk)],
        out_specs=pl.BlockSpec((None, tm, C), blk))
    return _pc_prefetch(body, name=name, grid_spec=grid_spec, out_shape=jax.ShapeDtypeStruct((S, R, C), BF16),
                        dims=("parallel", "parallel"))(c.reshape(1).astype(jnp.int32), g0, g1, ra)


def _add_chips(p, rb, chip, *, name):
    S, Rh, C = p.shape
    tm = _row_tile(Rh, C)

    def body(s_ref, p_ref, r_ref, o_ref):
        o_ref[...] = ((p_ref[...].astype(F32) + r_ref[0].astype(F32)) + r_ref[1].astype(F32)) + r_ref[2].astype(F32)

    grid_spec = pltpu.PrefetchScalarGridSpec(
        num_scalar_prefetch=1, grid=(Rh // tm,),
        in_specs=[pl.BlockSpec((None, tm, C), lambda i, s_ref: (s_ref[0], i, 0)),
                  pl.BlockSpec((3, tm, C), lambda i, s_ref: (0, i, 0))],
        out_specs=pl.BlockSpec((tm, C), lambda i, s_ref: (i, 0)))
    return _pc_prefetch(body, name=name, grid_spec=grid_spec, out_shape=jax.ShapeDtypeStruct((Rh, C), F32),
                        dims=("parallel",))(chip.reshape(1).astype(jnp.int32), p, rb)


def _sum_slots(g, *, name):
    n, R, C = g.shape

    def body(g_ref, o_ref):
        acc = g_ref[0]
        for s in range(1, n):
            acc = acc + g_ref[s]
        o_ref[...] = acc

    return _pc(body, name=name, grid=(1,), in_specs=[pl.BlockSpec((n, R, C), lambda i: (0, 0, 0))],
               out_specs=pl.BlockSpec((R, C), lambda i: (0, 0)), out_shape=jax.ShapeDtypeStruct((R, C), F32),
               dims=("arbitrary",))(g)


ANY = pl.BlockSpec(memory_space=pl.ANY)


def _place():
    return lax.axis_index("x"), lax.axis_index("y"), lax.axis_index("c")


def _other_chips(x, y):
    return [(1 - x, y), (x, 1 - y), (1 - x, 1 - y)]


def _allgather_chips(ws, *, name):
    n = len(ws)

    def body(*refs):
        w_refs, out_refs, send_sems, recv_sems = refs[:n], refs[n:2 * n], refs[2 * n], refs[2 * n + 1]
        x, y, c = _place()
        sib = (x, y, 1 - c)
        chips = _other_chips(x, y)

        def copy(a, k, chip_id, layer, to):
            src = w_refs[a].at[layer] if k < 3 else out_refs[a].at[chip_id, layer]
            return pltpu.make_async_remote_copy(src_ref=src, dst_ref=out_refs[a].at[chip_id, layer],
                                                send_sem=send_sems.at[k * n + a], recv_sem=recv_sems.at[k * n + a],
                                                device_id=to, device_id_type=MESH)

        sends = [copy(a, j, 2 * x + y, c, (px, py, c)) for j, (px, py) in enumerate(chips) for a in range(n)]
        for cp in sends:
            cp.start()
        passed = []
        for j, (px, py) in enumerate(chips):
            for a in range(n):
                copy(a, j, 2 * px + py, c, (px, py, c)).wait_recv()
                fwd = copy(a, 3 + j, 2 * px + py, c, sib)
                fwd.start()
                passed.append(fwd)
        for j, (px, py) in enumerate(chips):
            for a in range(n):
                copy(a, 3 + j, 2 * px + py, 1 - c, sib).wait_recv()
        for cp in sends + passed:
            cp.wait_send()

    return _pc_comm(body, name=name, in_specs=[ANY] * n, out_specs=[ANY] * n,
                    out_shape=[jax.ShapeDtypeStruct((N_CHIPS,) + w.shape, w.dtype) for w in ws],
                    scratch_shapes=[pltpu.SemaphoreType.DMA((6 * n,)), pltpu.SemaphoreType.DMA((6 * n,))])(*ws)


def _send_other_layer_to_sibling(g0s, g1s, *, name):
    n = len(g0s)

    def body(*refs):
        g_refs = (refs[:n], refs[n:2 * n])
        out_refs, send_sems, recv_sems = refs[2 * n:3 * n], refs[3 * n], refs[3 * n + 1]
        x, y, c = _place()

        def copy(a, layer):
            return pltpu.make_async_remote_copy(src_ref=g_refs[layer][a], dst_ref=out_refs[a], send_sem=send_sems.at[a],
                                                recv_sem=recv_sems.at[a], device_id=(x, y, 1 - c), device_id_type=MESH)

        for layer in range(2):
            @pl.when(c == 1 - layer)
            def _(layer=layer):
                for a in range(n):
                    copy(a, layer).start()
        for a in range(n):
            copy(a, 0).wait()

    return _pc_comm(body, name=name, in_specs=[ANY] * (2 * n), out_specs=[ANY] * n,
                    out_shape=[jax.ShapeDtypeStruct(g.shape, g.dtype) for g in g0s],
                    scratch_shapes=[pltpu.SemaphoreType.DMA((n,)), pltpu.SemaphoreType.DMA((n,))])(*g0s, *g1s)


def _scatter_to_chips(ps, *, name):
    n = len(ps)

    def body(*refs):
        p_refs, rb_refs, send_sems, recv_sems = refs[:n], refs[n:2 * n], refs[2 * n], refs[2 * n + 1]
        x, y, c = _place()
        chips = _other_chips(x, y)
        sends = [pltpu.make_async_remote_copy(src_ref=p_refs[a].at[2 * px + py], dst_ref=rb_refs[a].at[j],
                                              send_sem=send_sems.at[j * n + a], recv_sem=recv_sems.at[j * n + a],
                                              device_id=(px, py, c), device_id_type=MESH)
                 for j, (px, py) in enumerate(chips) for a in range(n)]
        for cp in sends:
            cp.start()
        for cp in sends:
            cp.wait()

    return _pc_comm(body, name=name, in_specs=[ANY] * n, out_specs=[ANY] * n,
                    out_shape=[jax.ShapeDtypeStruct((3,) + p.shape[1:], p.dtype) for p in ps],
                    scratch_shapes=[pltpu.SemaphoreType.DMA((3 * n,)), pltpu.SemaphoreType.DMA((3 * n,))])(*ps)


def _swap_with_sibling(fs, *, name):
    n = len(fs)

    def body(*refs):
        f_refs, out_refs, send_sems, recv_sems = refs[:n], refs[n:2 * n], refs[2 * n], refs[2 * n + 1]
        x, y, c = _place()
        copies = [pltpu.make_async_remote_copy(src_ref=f_refs[a], dst_ref=out_refs[a], send_sem=send_sems.at[a],
                                               recv_sem=recv_sems.at[a], device_id=(x, y, 1 - c), device_id_type=MESH)
                  for a in range(n)]
        for cp in copies:
            cp.start()
        for cp in copies:
            cp.wait()

    return _pc_comm(body, name=name, in_specs=[ANY] * n, out_specs=[ANY] * n,
                    out_shape=[jax.ShapeDtypeStruct(f.shape, f.dtype) for f in fs],
                    scratch_shapes=[pltpu.SemaphoreType.DMA((n,)), pltpu.SemaphoreType.DMA((n,))])(*fs)


def _allgather_devices(v, *, name):
    R, C = v.shape

    def body(v_ref, out_ref, send_sems, recv_sems):
        x, y, c = _place()
        me = 4 * x + 2 * y + c
        out_ref[me] = v_ref[...]
        peers = []
        for k in range(1, 8):
            fx, fy, fc = (k >> 2) & 1, (k >> 1) & 1, k & 1
            px = 1 - x if fx else x
            py = 1 - y if fy else y
            pcc = 1 - c if fc else c
            peers.append((px, py, pcc))
        sends = []
        for k, peer in enumerate(peers):
            cp = pltpu.make_async_remote_copy(src_ref=v_ref, dst_ref=out_ref.at[me], send_sem=send_sems.at[k],
                                              recv_sem=recv_sems.at[k], device_id=peer, device_id_type=MESH)
            cp.start()
            sends.append(cp)
        for k, (px, py, pcc) in enumerate(peers):
            pltpu.make_async_remote_copy(src_ref=v_ref, dst_ref=out_ref.at[4 * px + 2 * py + pcc],
                                         send_sem=send_sems.at[k], recv_sem=recv_sems.at[k], device_id=peers[k],
                                         device_id_type=MESH).wait_recv()
        for cp in sends:
            cp.wait_send()

    vm = pl.BlockSpec(memory_space=pltpu.VMEM)
    return _pc_comm(body, name=name, in_specs=[vm], out_specs=vm, out_shape=jax.ShapeDtypeStruct((8, R, C), F32),
                    scratch_shapes=[pltpu.SemaphoreType.DMA((7,)), pltpu.SemaphoreType.DMA((7,))])(v)


D_MODEL = 1024
SC_W = D_MODEL // 4
GDN_W = D_MODEL // 2
SB_W = D_MODEL - SC_W - GDN_W
D_FF = 256 * ((8 * D_MODEL // 3 + 255) // 256)
O_SC, O_GQKV, O_GZ, O_GA, O_GB, O_SB = 0, 3 * SC_W, 3 * SC_W + 3 * GDN_W, 3 * SC_W + 4 * GDN_W, \
    3 * SC_W + 4 * GDN_W + GDN_HEADS, 3 * SC_W + 4 * GDN_W + 2 * GDN_HEADS
D_IN_PROJ = O_SB + 3 * SB_W
P_GQKV, P_SC, P_SB = 0, 3 * GDN_W, 3 * GDN_W + 3 * SC_W
P_GZ = P_SB + 3 * SB_W
P_GAB = P_GZ + GDN_W
P_PAD = 256
P_WIDTH = P_GAB + P_PAD


def _proj_to_kernel_layout(w):
    pad = jnp.zeros((w.shape[0], P_PAD - 2 * GDN_HEADS), w.dtype)
    return jnp.concatenate([w[:, O_GQKV:O_GZ], w[:, O_SC:O_GQKV], w[:, O_SB:], w[:, O_GZ:O_GA], w[:, O_GA:O_SB], pad],
                           axis=1)


def _proj_from_kernel_layout(g):
    return jnp.concatenate([g[:, P_SC:P_SB], g[:, P_GQKV:P_SC], g[:, P_GZ:P_GAB], g[:, P_GAB:P_GAB + 2 * GDN_HEADS],
                            g[:, P_SB:P_GZ]], axis=1)


def _mixout_to_kernel_layout(w):
    return jnp.concatenate([w[SC_W:SC_W + GDN_W], w[:SC_W], w[SC_W + GDN_W:]], axis=0)


def _pack_vec(parts, rows_to):
    flat = jnp.concatenate([p.reshape(-1) for p in parts])
    return jnp.pad(flat, (0, rows_to * LANES - flat.shape[0])).reshape(rows_to, LANES)


def _unpack_vec(mat, shapes):
    flat = mat.reshape(-1)
    out, r = [], 0
    for shp in shapes:
        n = int(np.prod(shp))
        out.append(flat[r:r + n].reshape(shp))
        r += n
    return out


def _round_up(n, m):
    return (n + m - 1) // m * m


def _layer_fwd(x, p, l):
    L = x.shape[0]
    tag = "l%d_" % l
    h = _rmsnorm_fwd(x, p["wn_mix"], name=tag + "norm_mix")
    proj = _matmul(h, p["w_in"], "nn", tm=2048, tn=768, tk=D_MODEL, name=tag + "proj")
    (y_sc,) = _conv_pointwise_fwd([(proj, P_SC + SC_W), (proj, P_SC + 2 * SC_W)], [(p["w_sconv"], 0)], [(proj, P_SC)],
                                  _pre_product, _post_gate_mul, [(SC_W, BF16)], tc=SC_W, tm=512, name=tag + "sconv")
    (qkv,) = _conv_pointwise_fwd([(proj, P_GQKV)], [(p["w_gdn_conv"], 0)], [], _pre_identity, _post_silu,
                                 [(3 * GDN_W, F32)], tc=GDN_W, tm=512, name=tag + "gdn_conv")
    qe, ke, u, w, attn, eg = _gdn_prep_fwd(qkv, proj, P_GAB, p["a_log"], p["dt_bias"], name=tag + "gdn_prep")
    y_gdn, states = _gdn_scan_fwd(qe, ke, u, w, attn, eg, proj, P_GZ, p["wgn"], name=tag + "gdn_scan")
    sb_scale = (SB_W // SB_HEADS) ** -0.5
    sbqkv = jnp.concatenate([proj[:, P_SB:P_SB + SB_W] * sb_scale, proj[:, P_SB + SB_W:P_SB + 3 * SB_W]],
                            axis=1).astype(BF16)
    y_sb = _sb_fwd(sbqkv, name=tag + "sb_fwd")
    y_cat = [y_gdn, y_sc, y_sb]
    x2 = _matmul_rows_parts(y_cat, p["w_out"], "nn", res=x, name=tag + "mix_out")
    h2 = _rmsnorm_fwd(x2, p["wn_ffn"], name=tag + "norm_ffn")
    up_g = _matmul(h2, p["w_up_g"], "nn", tm=1024, tn=D_FF // 2, tk=D_MODEL, name=tag + "up_gate")
    up_v = _matmul(h2, p["w_up_v"], "nn", tm=1024, tn=D_FF // 2, tk=D_MODEL, name=tag + "up_val")
    (act,) = _conv_pointwise_fwd([(up_g, 0), (up_v, 0)], [(p["w_fconv_g"], 0), (p["w_fconv_v"], 0)], [],
                                 _pre_identity, _post_swiglu, [(D_FF, BF16)], tc=256, tm=1024, name=tag + "ffn_act")
    x3 = _matmul(act, p["w_down"], "nn", tm=1024, tn=D_MODEL, tk=D_FF // 2, res=x2, name=tag + "ffn_down")
    saved = dict(x=x, h=h, proj=proj, qkv=qkv, qe=qe, ke=ke, u=u, w=w, attn=attn, eg=eg, states=states,
                 sbqkv=sbqkv, y_cat=y_cat, x2=x2, h2=h2, up_g=up_g, up_v=up_v, act=act)
    return x3, saved


def _layer_bwd(dx3, p, s, l):
    L = dx3.shape[0]
    tag = "l%d_b_" % l
    g = {}
    dact = _matmul(dx3, p["w_down"], "nt", tm=1024, tn=D_FF // 2, tk=D_MODEL, name=tag + "dact")
    g["w_down"] = _matmul(s["act"], dx3, "tn", tm=D_FF // 2, tn=D_MODEL, tk=1024, name=tag + "dw_down")
    (dup_g, dup_v), _, (g["w_fconv_g"], g["w_fconv_v"]) = _conv_pointwise_bwd(
        [(s["up_g"], 0), (s["up_v"], 0)], [(p["w_fconv_g"], 0), (p["w_fconv_v"], 0)], [], [(dact, 0)],
        _pre_identity, _post_swiglu, D_FF, tc=256, tm=1024, name=tag + "ffn_act")
    dh2 = _matmul(dup_g, p["w_up_g"], "nt", tm=1024, tn=D_MODEL, tk=D_FF // 2, name=tag + "dh2_gate")
    dh2 = _matmul(dup_v, p["w_up_v"], "nt", tm=1024, tn=D_MODEL, tk=D_FF // 2, res=dh2, name=tag + "dh2_val")
    g["w_up"] = _matmul(s["h2"], dup_g, "tn", tm=D_MODEL, tn=D_FF // 2, tk=1024, slabs=(N_CHIPS, 0, None),
                        name=tag + "dw_up_gate")
    g["w_up"] = _matmul(s["h2"], dup_v, "tn", tm=D_MODEL, tn=D_FF // 2, tk=1024, slabs=(N_CHIPS, 2, g["w_up"]),
                        name=tag + "dw_up_val")
    dx2, g["wn_ffn"] = _rmsnorm_bwd(dh2, s["x2"], p["wn_ffn"], dx3, name=tag + "norm_ffn")
    dycat = _matmul(dx2, p["w_out"], "nt", tm=512, tn=D_MODEL, tk=D_MODEL, name=tag + "dycat")
    y_gdn, y_sc, y_sb = s["y_cat"]
    g["w_out"] = _matmul_tn_parts([y_sc, y_gdn, y_sb], dx2, name=tag + "dw_out")
    sb_scale = (SB_W // SB_HEADS) ** -0.5
    dsq, dsk, dsv = _sb_bwd(s["sbqkv"], dycat, GDN_W + SC_W, sb_scale, name=tag + "sb_bwd")
    dqe, dke, du, dw, dattn, deg, dgz, g["wgn"] = _gdn_scan_bwd(
        s["qe"], s["ke"], s["u"], s["w"], s["attn"], s["eg"], s["proj"], P_GZ, p["wgn"], s["states"], dycat, 0,
        name=tag + "gdn_scan")
    dqkv_act, dgab, g["a_log"], g["dt_bias"] = _gdn_prep_bwd(
        s["qkv"], s["proj"], P_GAB, p["a_log"], p["dt_bias"], dqe, dke, du, dw, dattn, deg, P_PAD,
        name=tag + "gdn_prep")
    (dqkv,), _, (g["w_gdn_conv"],) = _conv_pointwise_bwd(
        [(s["proj"], P_GQKV)], [(p["w_gdn_conv"], 0)], [], [(dqkv_act, 0)], _pre_identity, _post_silu, 3 * GDN_W,
        tc=GDN_W, tm=512, name=tag + "gdn_conv")
    (dsc_c, dsc_h), (dsc_b,), (g["w_sconv"],) = _conv_pointwise_bwd(
        [(s["proj"], P_SC + SC_W), (s["proj"], P_SC + 2 * SC_W)], [(p["w_sconv"], 0)], [(s["proj"], P_SC)],
        [(dycat, GDN_W)], _pre_product, _post_gate_mul, SC_W, tc=SC_W, tm=512, name=tag + "sconv")
    dproj = [dqkv, dsc_b, dsc_c, dsc_h, dsq, dsk, dsv, dgz, dgab]
    dh = _matmul_rows_parts(dproj, p["w_in"], "nt", name=tag + "dh")
    g["w_in"] = jnp.concatenate([_matmul_tn_parts(s["h"], dproj[:4], name=tag + "dw_in_a"),
                                 _matmul_tn_parts(s["h"], dproj[4:], name=tag + "dw_in_b")], axis=1)
    dx, g["wn_mix"] = _rmsnorm_bwd(dh, s["x"], p["wn_mix"], dx2, name=tag + "norm_mix")
    return dx, g


BIG = ("w_mix_in", "w_mix_out", "w_ffn_up", "w_ffn_down")
BIG_AXIS = {"w_mix_in": 2, "w_mix_out": 1, "w_ffn_up": 2, "w_ffn_down": 1}
SMALL_SHARDED = ("w_sconv", "w_gdn_conv", "w_ffn_conv")
SMALL_REPLICATED = ("w_norm_mix", "gdn_a_log", "gdn_dt_bias", "w_gdn_norm", "w_norm_ffn", "w_norm_final")
WEIGHTS = ("w_norm_mix", "w_mix_in", "w_sconv", "w_gdn_conv", "gdn_a_log", "gdn_dt_bias", "w_gdn_norm", "w_mix_out",
           "w_norm_ffn", "w_ffn_up", "w_ffn_conv", "w_ffn_down", "w_norm_final")


def kernel(x, w_norm_mix, w_mix_in, w_sconv, w_gdn_conv, gdn_a_log, gdn_dt_bias, w_gdn_norm, w_mix_out, w_norm_ffn, w_ffn_up, w_ffn_conv, w_ffn_down, w_norm_final, loss_target, m_w_norm_mix, m_w_mix_in, m_w_sconv, m_w_gdn_conv, m_gdn_a_log, m_gdn_dt_bias, m_w_gdn_norm, m_w_mix_out, m_w_norm_ffn, m_w_ffn_up, m_w_ffn_conv, m_w_ffn_down, m_w_norm_final, v_w_norm_mix, v_w_mix_in, v_w_sconv, v_w_gdn_conv, v_gdn_a_log, v_gdn_dt_bias, v_w_gdn_norm, v_w_mix_out, v_w_norm_ffn, v_w_ffn_up, v_w_ffn_conv, v_w_ffn_down, v_w_norm_final):
    W = dict(w_norm_mix=w_norm_mix, w_mix_in=w_mix_in, w_sconv=w_sconv, w_gdn_conv=w_gdn_conv, gdn_a_log=gdn_a_log,
             gdn_dt_bias=gdn_dt_bias, w_gdn_norm=w_gdn_norm, w_mix_out=w_mix_out, w_norm_ffn=w_norm_ffn,
             w_ffn_up=w_ffn_up, w_ffn_conv=w_ffn_conv, w_ffn_down=w_ffn_down, w_norm_final=w_norm_final)
    M = dict(w_norm_mix=m_w_norm_mix, w_mix_in=m_w_mix_in, w_sconv=m_w_sconv, w_gdn_conv=m_w_gdn_conv,
             gdn_a_log=m_gdn_a_log, gdn_dt_bias=m_gdn_dt_bias, w_gdn_norm=m_w_gdn_norm, w_mix_out=m_w_mix_out,
             w_norm_ffn=m_w_norm_ffn, w_ffn_up=m_w_ffn_up, w_ffn_conv=m_w_ffn_conv, w_ffn_down=m_w_ffn_down,
             w_norm_final=m_w_norm_final)
    V = dict(w_norm_mix=v_w_norm_mix, w_mix_in=v_w_mix_in, w_sconv=v_w_sconv, w_gdn_conv=v_w_gdn_conv,
             gdn_a_log=v_gdn_a_log, gdn_dt_bias=v_gdn_dt_bias, w_gdn_norm=v_w_gdn_norm, w_mix_out=v_w_mix_out,
             w_norm_ffn=v_w_norm_ffn, w_ffn_up=v_w_ffn_up, w_ffn_conv=v_w_ffn_conv, w_ffn_down=v_w_ffn_down,
             w_norm_final=v_w_norm_final)
    depth = w_mix_in.shape[0]
    L = x.shape[1]
    mx, my, mc = lax.axis_index("x"), lax.axis_index("y"), lax.axis_index("c")
    chip = 2 * mx + my

    assert depth == 2
    own = [W[n].astype(BF16) for n in BIG]
    gathered = _allgather_chips(own, name="gather_big")
    gathered = [lax.dynamic_update_slice(g, o[None], (chip, 0, 0, 0)) for g, o in zip(gathered, own)]
    full_big = [{n: jnp.concatenate([g[b, l] for b in range(N_CHIPS)], axis=BIG_AXIS[n] - 1)
                 for n, g in zip(BIG, gathered)} for l in range(depth)]

    small_sh_shapes = [W[n].shape for n in SMALL_SHARDED]
    n_small_sh = sum(int(np.prod(s)) for s in small_sh_shapes)
    small_rows = _round_up(n_small_sh, 8 * LANES) // LANES
    small_all = _allgather_devices(_pack_vec([W[n] for n in SMALL_SHARDED], small_rows), name="gather_small")
    small_chip = [_unpack_vec(small_all[2 * b], small_sh_shapes) for b in range(N_CHIPS)]
    full_small = {n: jnp.concatenate([small_chip[b][i] for b in range(N_CHIPS)], axis=2)
                  for i, n in enumerate(SMALL_SHARDED)}

    params = []
    for l in range(depth):
        w_up = full_big[l]["w_ffn_up"]
        fconv = full_small["w_ffn_conv"][l]
        params.append(dict(
            wn_mix=w_norm_mix[l], w_in=_proj_to_kernel_layout(full_big[l]["w_mix_in"]),
            w_sconv=full_small["w_sconv"][l], w_gdn_conv=full_small["w_gdn_conv"][l],
            a_log=gdn_a_log[l].reshape(GDN_HEADS, 1, 1), dt_bias=gdn_dt_bias[l].reshape(GDN_HEADS, 1, 1),
            wgn=w_gdn_norm[l], w_out=_mixout_to_kernel_layout(full_big[l]["w_mix_out"]), wn_ffn=w_norm_ffn[l],
            w_up_g=w_up[:, :D_FF], w_up_v=w_up[:, D_FF:], w_fconv_g=fconv[:, :D_FF], w_fconv_v=fconv[:, D_FF:],
            w_down=full_big[l]["w_ffn_down"]))

    xs = x[0]
    saved = []
    for l in range(depth):
        xs, s = _layer_fwd(xs, params[l], l)
        saved.append(s)
    loss_row, dx, g_norm_final = _final_loss(xs, w_norm_final, loss_target[0], name="final_loss")
    grads = [None] * depth
    for l in reversed(range(depth)):
        dx, grads[l] = _layer_bwd(dx, params[l], saved[l], l)
    loss = lax.psum(loss_row[0, 0], ("x", "y", "c"))

    G = {
        "w_sconv": jnp.stack([grads[l]["w_sconv"] for l in range(depth)]),
        "w_gdn_conv": jnp.stack([grads[l]["w_gdn_conv"] for l in range(depth)]),
        "w_ffn_conv": jnp.stack([jnp.concatenate([grads[l]["w_fconv_g"], grads[l]["w_fconv_v"]], axis=1)
                                 for l in range(depth)]),
        "w_norm_mix": jnp.stack([grads[l]["wn_mix"].reshape(-1) for l in range(depth)]),
        "gdn_a_log": jnp.stack([grads[l]["a_log"].reshape(-1) for l in range(depth)]),
        "gdn_dt_bias": jnp.stack([grads[l]["dt_bias"].reshape(-1) for l in range(depth)]),
        "w_gdn_norm": jnp.stack([grads[l]["wgn"].reshape(-1) for l in range(depth)]),
        "w_norm_ffn": jnp.stack([grads[l]["wn_ffn"].reshape(-1) for l in range(depth)]),
        "w_norm_final": g_norm_final.reshape(-1),
    }

    def by_shard(l):
        g_in = _proj_from_kernel_layout(grads[l]["w_in"])
        g_in = g_in.reshape(D_MODEL, N_CHIPS, -1).transpose(1, 0, 2)
        return [g_in, grads[l]["w_out"].reshape(N_CHIPS, -1, D_MODEL), grads[l]["w_up"],
                grads[l]["w_down"].reshape(N_CHIPS, -1, D_MODEL)]

    g_layers = [by_shard(l) for l in range(depth)]
    from_sibling = _send_other_layer_to_sibling(g_layers[0], g_layers[1], name="rs_sibling")
    chip_sums = [_add_layers(g0, g1, ra, mc, name="rs_add_layers_" + n)
                 for n, g0, g1, ra in zip(BIG, g_layers[0], g_layers[1], from_sibling)]
    from_chips = _scatter_to_chips(chip_sums, name="rs_chips")
    mine = [_add_chips(p, rb, chip, name="rs_add_chips_" + n) for n, p, rb in zip(BIG, chip_sums, from_chips)]
    other = _swap_with_sibling(mine, name="rs_result")
    out_g, out_d, out_m, out_v = {}, {}, {}, {}
    for n, g_mine, g_other in zip(BIG, mine, other):
        out_g[n], out_d[n], out_m[n], out_v[n] = _adamw_layers(W[n], g_mine, g_other, M[n], V[n], mc,
                                                               name="adamw_" + n)

    small_names = SMALL_SHARDED + SMALL_REPLICATED
    small_full_shapes = [G[n].shape for n in small_names]
    n_small = sum(int(np.prod(s)) for s in small_full_shapes)
    red_rows = _round_up(n_small, 8 * LANES) // LANES
    partials = _allgather_devices(_pack_vec([G[n] for n in small_names], red_rows), name="reduce_small")
    summed = _unpack_vec(_sum_slots(partials, name="reduce_small_sum"), small_full_shapes)
    g_small = {}
    for n, a in zip(small_names, summed):
        if n in SMALL_SHARDED:
            width = a.shape[2] // N_CHIPS
            a = lax.dynamic_slice_in_dim(a, chip * width, width, axis=2)
        g_small[n] = a
    own_shapes = [W[n].shape for n in small_names]
    n_own = sum(int(np.prod(s)) for s in own_shapes)
    own_rows = _round_up(n_own, 8 * LANES) // LANES
    packed = [_pack_vec([src[n] for n in small_names], own_rows) for src in (W, g_small, M, V)]
    d_s, nm_s, nv_s = _adamw(*packed, name="adamw_small", tm=own_rows)
    for mat, dst in ((packed[1], out_g), (d_s, out_d), (nm_s, out_m), (nv_s, out_v)):
        for n, a in zip(small_names, _unpack_vec(mat, own_shapes)):
            dst[n] = a

    outs = [loss, dx[None]]
    for dst in (out_g, out_d, out_m, out_v):
        outs += [dst[n] for n in WEIGHTS]
    return tuple(outs)
```

```python
import jax
import jax.numpy as jnp
import numpy as np
from jax import lax
from jax.experimental import pallas as pl
from jax.experimental.pallas import tpu as pltpu

F32 = jnp.float32
BF16 = jnp.bfloat16
MESH = pl.DeviceIdType.MESH

NORM_EPS = 1e-6
GDN_HEADS = 4
GDN_CHUNK = 64
GDN_SCAN_CHUNKS = 4
SB_HEADS = 4
SB_DEAD_LOG = -110.0
ADAM_LR = 0.001
ADAM_B1 = 0.9
ADAM_B2 = 0.999
ADAM_EPS = 1e-08
ADAM_WD = 0.01
ADAM_STEP = 10

VMEM_LIMIT_BYTES = 48 * 1024 * 1024
HALO = 8
CONV_SLAB = 128
LANES = 128
N_CHIPS = 4


def _pc(body, *, name, grid, in_specs, out_specs, out_shape, scratch_shapes=(), dims=None, aliases=None):
    params = dict(vmem_limit_bytes=VMEM_LIMIT_BYTES)
    if dims is not None:
        params["dimension_semantics"] = dims
    return pl.pallas_call(body, name=name, grid=grid, in_specs=in_specs, out_specs=out_specs, out_shape=out_shape,
                          scratch_shapes=list(scratch_shapes), input_output_aliases=aliases or {},
                          compiler_params=pltpu.CompilerParams(**params))


def _pc_prefetch(body, *, name, grid_spec, out_shape, dims):
    return pl.pallas_call(body, name=name, grid_spec=grid_spec, out_shape=out_shape,
                          compiler_params=pltpu.CompilerParams(vmem_limit_bytes=VMEM_LIMIT_BYTES,
                                                               dimension_semantics=dims))


def _pc_comm(body, *, name, in_specs, out_specs, out_shape, scratch_shapes):
    return pl.pallas_call(body, name=name, in_specs=in_specs, out_specs=out_specs, out_shape=out_shape,
                          scratch_shapes=list(scratch_shapes),
                          compiler_params=pltpu.CompilerParams(vmem_limit_bytes=VMEM_LIMIT_BYTES))


_DIMS = {"nn": (((1,), (0,)), ((), ())), "nt": (((1,), (1,)), ((), ())), "tn": (((0,), (0,)), ((), ()))}


def _matmul(a, b, mode, *, name, tm=512, tn=512, tk=512, out_dtype=F32, res=None, slabs=None):
    if mode == "nn":
        (M, K), (K2, N) = a.shape, b.shape
    elif mode == "nt":
        (M, K), (N, K2) = a.shape, b.shape
    else:
        (K, M), (K2, N) = a.shape, b.shape
    assert K == K2, (a.shape, b.shape, mode)
    tm, tn, tk = min(tm, M), min(tn, N), min(tk, K)
    assert M % tm == 0 and N % tn == 0 and K % tk == 0, (M, N, K, tm, tn, tk)
    nk = K // tk
    if mode == "tn":
        a_spec = pl.BlockSpec((tk, tm), lambda i, j, k: (k, i))
    else:
        a_spec = pl.BlockSpec((tm, tk), lambda i, j, k: (i, k))
    if mode == "nt":
        b_spec = pl.BlockSpec((tn, tk), lambda i, j, k: (j, k))
    else:
        b_spec = pl.BlockSpec((tk, tn), lambda i, j, k: (k, j))
    o_spec = pl.BlockSpec((tm, tn), lambda i, j, k: (i, j))
    has_res = res is not None
    dn = _DIMS[mode]

    def body(*refs):
        if has_res:
            a_ref, b_ref, r_ref, o_ref, acc = refs
        else:
            a_ref, b_ref, o_ref, acc = refs
        k = pl.program_id(2)
        p = lax.dot_general(a_ref[...].astype(BF16), b_ref[...].astype(BF16), dn, preferred_element_type=F32)

        def finish(total):
            if has_res:
                total = total + r_ref[...].astype(F32)
            o_ref[...] = total.astype(out_dtype)

        if nk == 1:
            finish(p)
        else:
            @pl.when(k == 0)
            def _():
                acc[...] = p

            @pl.when(k > 0)
            def _():
                acc[...] += p

            @pl.when(k == nk - 1)
            def _():
                finish(acc[...])

    in_specs = [a_spec, b_spec] + ([o_spec] if has_res else [])
    args = (a, b) + ((res,) if has_res else ())
    out_shape = jax.ShapeDtypeStruct((M, N), out_dtype)
    aliases = None
    if slabs is not None:
        n_slabs, first, into = slabs
        assert not has_res and tm == M
        o_spec = pl.BlockSpec((None, tm, tn), lambda i, j, k: (j + first, i, 0))
        out_shape = jax.ShapeDtypeStruct((n_slabs, M, tn), out_dtype)
        if into is not None:
            in_specs.append(pl.BlockSpec(memory_space=pl.ANY))
            args = args + (into,)
            aliases = {2: 0}
            inner = body

            def body(a_ref, b_ref, into_ref, o_ref, acc):
                inner(a_ref, b_ref, o_ref, acc)
    return _pc(body, name=name, grid=(M // tm, N // tn, nk), in_specs=in_specs, out_specs=o_spec,
               out_shape=out_shape, scratch_shapes=[pltpu.VMEM((tm, tn), F32)],
               dims=("parallel", "parallel", "arbitrary"), aliases=aliases)(*args)


def _offsets(parts, own_width_aligned):
    offs, at = [], 0
    for p in parts:
        assert at % (p.shape[1] if own_width_aligned else LANES) == 0, (at, p.shape)
        offs.append(at)
        at += p.shape[1]
    return offs, at


def _matmul_rows_parts(parts, w, mode, *, name, tm=512, res=None):
    M = parts[0].shape[0]
    offs, K = _offsets(parts, True)
    tm = min(tm, M)
    N = w.shape[1] if mode == "nn" else w.shape[0]
    assert (w.shape[0] if mode == "nn" else w.shape[1]) == K
    has_res = res is not None
    n = len(parts)

    def body(*refs):
        o_ref = refs[-1]
        total = None
        for s in range(n):
            p = lax.dot_general(refs[s][...].astype(BF16), refs[n + s][...].astype(BF16), _DIMS[mode],
                                preferred_element_type=F32)
            total = p if total is None else total + p
        if has_res:
            total = total + refs[2 * n][...]
        o_ref[...] = total

    in_specs = [pl.BlockSpec((tm, p.shape[1]), lambda i: (i, 0)) for p in parts]
    for p, off in zip(parts, offs):
        blk = off // p.shape[1]
        if mode == "nn":
            in_specs.append(pl.BlockSpec((p.shape[1], N), lambda i, blk=blk: (blk, 0)))
        else:
            in_specs.append(pl.BlockSpec((N, p.shape[1]), lambda i, blk=blk: (0, blk)))
    o_spec = pl.BlockSpec((tm, N), lambda i: (i, 0))
    args = tuple(parts) + (w,) * n + ((res,) if has_res else ())
    return _pc(body, name=name, grid=(M // tm,), in_specs=in_specs + ([o_spec] if has_res else []), out_specs=o_spec,
               out_shape=jax.ShapeDtypeStruct((M, N), F32), dims=("parallel",))(*args)


def _matmul_tn_parts(a, b, *, name, tk=1024):
    a_parts = list(a) if isinstance(a, (list, tuple)) else [a]
    b_parts = list(b) if isinstance(b, (list, tuple)) else [b]
    assert len(a_parts) == 1 or len(b_parts) == 1
    a_offs, M = _offsets(a_parts, False)
    b_offs, N = _offsets(b_parts, False)
    K = a_parts[0].shape[0]
    tk = min(tk, K)
    na, nb = len(a_parts), len(b_parts)

    def body(*refs):
        o_ref = refs[-1]
        first = pl.program_id(0) == 0
        for s in range(na):
            for t in range(nb):
                p = lax.dot_general(refs[s][...].astype(BF16), refs[na + t][...].astype(BF16), _DIMS["tn"],
                                    preferred_element_type=F32)
                rows = slice(a_offs[s], a_offs[s] + a_parts[s].shape[1])
                cols = slice(b_offs[t], b_offs[t] + b_parts[t].shape[1])

                @pl.when(first)
                def _(p=p, rows=rows, cols=cols):
                    o_ref[rows, cols] = p

                @pl.when(jnp.logical_not(first))
                def _(p=p, rows=rows, cols=cols):
                    o_ref[rows, cols] += p

    in_specs = [pl.BlockSpec((tk, p.shape[1]), lambda k: (k, 0)) for p in a_parts + b_parts]
    return _pc(body, name=name, grid=(K // tk,), in_specs=in_specs, out_specs=pl.BlockSpec((M, N), lambda k: (0, 0)),
               out_shape=jax.ShapeDtypeStruct((M, N), F32), dims=("arbitrary",))(*a_parts, *b_parts)


def _rmsnorm_fwd(x, w, *, name, tm=512):
    L, D = x.shape
    tm = min(tm, L)

    def body(x_ref, w_ref, h_ref):
        xv = x_ref[...]
        r = lax.rsqrt(jnp.mean(xv * xv, axis=-1, keepdims=True) + NORM_EPS)
        h_ref[...] = (xv * r * w_ref[...]).astype(BF16)

    return _pc(body, name=name, grid=(L // tm,),
               in_specs=[pl.BlockSpec((tm, D), lambda i: (i, 0)), pl.BlockSpec((1, D), lambda i: (0, 0))],
               out_specs=pl.BlockSpec((tm, D), lambda i: (i, 0)), out_shape=jax.ShapeDtypeStruct((L, D), BF16),
               dims=("parallel",))(x, w.reshape(1, D))


def _rmsnorm_bwd(dh, x, w, dres, *, name, tm=512):
    L, D = x.shape
    tm = min(tm, L)

    def body(dh_ref, x_ref, w_ref, dres_ref, dx_ref, dw_ref):
        xv = x_ref[...]
        r = lax.rsqrt(jnp.mean(xv * xv, axis=-1, keepdims=True) + NORM_EPS)
        xhat = xv * r
        dhv = dh_ref[...]
        g = dhv * w_ref[...]
        dx_ref[...] = dres_ref[...] + r * (g - xhat * jnp.mean(g * xhat, axis=-1, keepdims=True))
        part = jnp.sum(dhv * xhat, axis=0, keepdims=True)

        @pl.when(pl.program_id(0) == 0)
        def _():
            dw_ref[...] = part

        @pl.when(pl.program_id(0) > 0)
        def _():
            dw_ref[...] += part

    row = pl.BlockSpec((tm, D), lambda i: (i, 0))
    vec = pl.BlockSpec((1, D), lambda i: (0, 0))
    return _pc(body, name=name, grid=(L // tm,), in_specs=[row, row, vec, row], out_specs=[row, vec],
               out_shape=[jax.ShapeDtypeStruct((L, D), F32), jax.ShapeDtypeStruct((1, D), F32)],
               dims=("arbitrary",))(dh, x, w.reshape(1, D), dres)


def _final_loss(x, w, tgt, *, name, tm=512):
    L, D = x.shape
    tm = min(tm, L)

    def body(x_ref, w_ref, t_ref, loss_ref, dx_ref, dw_ref):
        xv = x_ref[...]
        r = lax.rsqrt(jnp.mean(xv * xv, axis=-1, keepdims=True) + NORM_EPS)
        xhat = xv * r
        e = xhat * w_ref[...] - t_ref[...]
        lpart = jnp.broadcast_to(0.5 * jnp.sum(jnp.mean(e * e, axis=-1, keepdims=True), axis=0, keepdims=True),
                                 (1, LANES))
        dy = e * (1.0 / D)
        g = dy * w_ref[...]
        dx_ref[...] = r * (g - xhat * jnp.mean(g * xhat, axis=-1, keepdims=True))
        part = jnp.sum(dy * xhat, axis=0, keepdims=True)

        @pl.when(pl.program_id(0) == 0)
        def _():
            dw_ref[...] = part
            loss_ref[...] = lpart

        @pl.when(pl.program_id(0) > 0)
        def _():
            dw_ref[...] += part
            loss_ref[...] += lpart

    row = pl.BlockSpec((tm, D), lambda i: (i, 0))
    vec = pl.BlockSpec((1, D), lambda i: (0, 0))
    lsp = pl.BlockSpec((1, LANES), lambda i: (0, 0))
    return _pc(body, name=name, grid=(L // tm,), in_specs=[row, vec, row], out_specs=[lsp, row, vec],
               out_shape=[jax.ShapeDtypeStruct((1, LANES), F32), jax.ShapeDtypeStruct((L, D), F32),
                          jax.ShapeDtypeStruct((1, D), F32)],
               dims=("arbitrary",))(x, w.reshape(1, D), tgt)


def _silu(x):
    return x * jax.nn.sigmoid(x)


def _conv_pointwise_fwd(xs, ws, es, pre, post, outs, *, tc, tm, name):
    L = xs[0][0].shape[0]
    tm = min(tm, L)
    ncol = outs[0][0] // tc
    nrow = L // tm
    hb = tm // HALO
    nx, nw, ne, no = len(xs), len(ws), len(es), len(outs)
    K = ws[0][0].shape[0]

    slab = min(CONV_SLAB, tm)
    win = slab + HALO
    assert tm % slab == 0 and K - 1 <= HALO

    def body(*refs):
        i = pl.program_id(1)
        first = (i > 0).astype(F32)
        n_in = 2 * nx + nw + ne
        o_refs = refs[n_in:n_in + no]
        x_pads = refs[n_in + no:]
        for n in range(nx):
            x_pads[n][0:HALO, :] = refs[2 * n + 1][...] * first
            x_pads[n][HALO:, :] = refs[2 * n][...]
        wv = [refs[2 * nx + n][...] for n in range(nw)]
        e_refs = refs[2 * nx + nw:n_in]

        @pl.loop(0, tm // slab)
        def _(t):
            r0 = pl.multiple_of(t * slab, HALO)
            ps = pre(*[x_pads[n][pl.ds(r0, win), :] for n in range(nx)])
            us = []
            for n in range(nw):
                u = None
                for k in range(K):
                    term = wv[n][k:k + 1, :] * (ps[n] if k == K - 1 else pltpu.roll(ps[n], K - 1 - k, 0))
                    u = term if u is None else u + term
                us.append(u[HALO:])
            rows = pl.ds(r0, slab)
            for o_ref, val in zip(o_refs, post(us, [e[rows, :] for e in e_refs])):
                o_ref[rows, :] = val.astype(o_ref.dtype)

    in_specs, args = [], []
    for arr, c0 in xs:
        off = c0 // tc
        in_specs.append(pl.BlockSpec((tm, tc), lambda j, i, off=off: (i, j + off)))
        in_specs.append(pl.BlockSpec((HALO, tc), lambda j, i, off=off: (jnp.maximum(i * hb - 1, 0), j + off)))
        args += [arr, arr]
    for arr, c0 in ws:
        off = c0 // tc
        in_specs.append(pl.BlockSpec((K, tc), lambda j, i, off=off: (0, j + off)))
        args.append(arr)
    for arr, c0 in es:
        off = c0 // tc
        in_specs.append(pl.BlockSpec((tm, tc), lambda j, i, off=off: (i, j + off)))
        args.append(arr)
    out_specs = [pl.BlockSpec((tm, tc), lambda j, i: (i, j)) for _ in range(no)]
    out_shape = [jax.ShapeDtypeStruct((L, c), dt) for c, dt in outs]
    return _pc(body, name=name, grid=(ncol, nrow), in_specs=in_specs, out_specs=out_specs, out_shape=out_shape,
               scratch_shapes=[pltpu.VMEM((tm + HALO, tc), F32)] * nx, dims=("parallel", "parallel"))(*args)


def _conv_pointwise_bwd(xs, ws, es, dys, pre, post, width, *, tc, tm, name, out_dtype=BF16):
    L = xs[0][0].shape[0]
    tm = min(tm, L)
    ncol = width // tc
    nrow = L // tm
    hb = tm // HALO
    nx, nw, ne, ny = len(xs), len(ws), len(es), len(dys)
    K = ws[0][0].shape[0]

    slab = min(CONV_SLAB, tm)
    win = slab + 2 * HALO
    assert tm % slab == 0 and K - 1 <= HALO

    def body(*refs):
        i = pl.program_id(1)
        first = (i > 0).astype(F32)
        more = (i < nrow - 1).astype(F32)
        n_in = 3 * nx + nw + 2 * ne + 2 * ny
        n_out = nx + ne + nw
        dx_refs = refs[n_in:n_in + nx]
        de_refs = refs[n_in + nx:n_in + nx + ne]
        dw_refs = refs[n_in + nx + ne:n_in + n_out]
        pads = refs[n_in + n_out:]
        x_pads, e_pads, dy_pads = pads[:nx], pads[nx:nx + ne], pads[nx + ne:]
        pos = 0
        for n in range(nx):
            x_pads[n][0:HALO, :] = refs[pos + 1][...] * first
            x_pads[n][HALO:HALO + tm, :] = refs[pos][...]
            x_pads[n][HALO + tm:, :] = refs[pos + 2][...]
            pos += 3
        wv = [refs[pos + n][...] for n in range(nw)]
        pos += nw
        for n in range(ne):
            e_pads[n][0:HALO, :] = jnp.zeros((HALO, tc), F32)
            e_pads[n][HALO:HALO + tm, :] = refs[pos][...]
            e_pads[n][HALO + tm:, :] = refs[pos + 1][...]
            pos += 2
        for n in range(ny):
            dy_pads[n][0:HALO, :] = jnp.zeros((HALO, tc), F32)
            dy_pads[n][HALO:HALO + tm, :] = refs[pos][...].astype(F32)
            dy_pads[n][HALO + tm:, :] = refs[pos + 1][...].astype(F32) * more
            pos += 2

        def one_slab(t, dw_acc):
            r0 = pl.multiple_of(t * slab, HALO)
            xw = [x_pads[n][pl.ds(r0, win), :] for n in range(nx)]
            ew = [e_pads[n][pl.ds(r0, win), :] for n in range(ne)]
            dyw = [dy_pads[n][pl.ds(r0, win), :] for n in range(ny)]
            ps, pre_vjp = jax.vjp(lambda *x_: pre(*x_), *xw)
            shifted = [[p if k == K - 1 else pltpu.roll(p, K - 1 - k, 0) for k in range(K)] for p in ps]
            us = []
            for n in range(nw):
                u = None
                for k in range(K):
                    term = wv[n][k:k + 1, :] * shifted[n][k]
                    u = term if u is None else u + term
                us.append(u)
            _, post_vjp = jax.vjp(lambda u_, e_: post(u_, e_), us, ew)
            dus, des = post_vjp(dyw)
            dps, dw_new = [], []
            for n in range(nw):
                dp = None
                for k in range(K):
                    term = wv[n][k:k + 1, :] * (dus[n] if k == K - 1 else pltpu.roll(dus[n], win - (K - 1 - k), 0))
                    dp = term if dp is None else dp + term
                dps.append(dp)
                inner = dus[n][HALO:HALO + slab]
                dw_new.append([dw_acc[n][k] + jnp.sum(inner * shifted[n][k][HALO:HALO + slab], axis=0, keepdims=True)
                               for k in range(K)])
            dxs = pre_vjp(dps)
            rows = pl.ds(r0, slab)
            for r, v in zip(dx_refs, dxs):
                r[rows, :] = v[HALO:HALO + slab].astype(out_dtype)
            for r, v in zip(de_refs, des):
                r[rows, :] = v[HALO:HALO + slab].astype(out_dtype)
            return dw_new

        zero = [[jnp.zeros((1, tc), F32) for _ in range(K)] for _ in range(nw)]
        dw_tile = lax.fori_loop(0, tm // slab, one_slab, zero)
        for n in range(nw):
            for k in range(K):
                @pl.when(i == 0)
                def _(n=n, k=k):
                    dw_refs[n][k:k + 1, :] = dw_tile[n][k]

                @pl.when(i > 0)
                def _(n=n, k=k):
                    dw_refs[n][k:k + 1, :] += dw_tile[n][k]

    in_specs, args = [], []

    def add_rows(arr, c0, prev, nxt):
        off = c0 // tc
        in_specs.append(pl.BlockSpec((tm, tc), lambda j, i, off=off: (i, j + off)))
        args.append(arr)
        if prev:
            in_specs.append(pl.BlockSpec((HALO, tc), lambda j, i, off=off: (jnp.maximum(i * hb - 1, 0), j + off)))
            args.append(arr)
        if nxt:
            last = L // HALO - 1
            in_specs.append(pl.BlockSpec((HALO, tc), lambda j, i, off=off: (jnp.minimum((i + 1) * hb, last), j + off)))
            args.append(arr)

    for arr, c0 in xs:
        add_rows(arr, c0, True, True)
    for arr, c0 in ws:
        off = c0 // tc
        in_specs.append(pl.BlockSpec((K, tc), lambda j, i, off=off: (0, j + off)))
        args.append(arr)
    for arr, c0 in es:
        add_rows(arr, c0, False, True)
    for arr, c0 in dys:
        add_rows(arr, c0, False, True)
    tile = pl.BlockSpec((tm, tc), lambda j, i: (i, j))
    wtile = pl.BlockSpec((K, tc), lambda j, i: (0, j))
    out_specs = [tile] * (nx + ne) + [wtile] * nw
    out_shape = [jax.ShapeDtypeStruct((L, width), out_dtype)] * (nx + ne) + \
        [jax.ShapeDtypeStruct((K, width), F32)] * nw
    res = _pc(body, name=name, grid=(ncol, nrow), in_specs=in_specs, out_specs=out_specs, out_shape=out_shape,
              scratch_shapes=[pltpu.VMEM((tm + 2 * HALO, tc), F32)] * (nx + ne + ny),
              dims=("parallel", "arbitrary"))(*args)
    return res[:nx], res[nx:nx + ne], res[nx + ne:]


def _pre_identity(*x):
    return list(x)


def _pre_product(c, h):
    return [c * h]


def _post_silu(us, es):
    return [_silu(us[0])]


def _post_gate_mul(us, es):
    return [es[0] * us[0]]


def _post_swiglu(us, es):
    return [_silu(us[0]) * us[1]]


def _make_dot(passes):
    def raw(a, b, dn):
        a_hi = a.astype(BF16)
        b_hi = b.astype(BF16)
        out = lax.dot_general(a_hi, b_hi, dn, preferred_element_type=F32)
        if passes == 3:
            a_lo = (a - a_hi.astype(F32)).astype(BF16)
            b_lo = (b - b_hi.astype(F32)).astype(BF16)
            out = out + lax.dot_general(a_hi, b_lo, dn, preferred_element_type=F32)
            out = out + lax.dot_general(a_lo, b_hi, dn, preferred_element_type=F32)
        return out

    @jax.custom_vjp
    def nn(a, b):
        return raw(a, b, _DIMS["nn"])

    @jax.custom_vjp
    def nt(a, b):
        return raw(a, b, _DIMS["nt"])

    @jax.custom_vjp
    def tn(a, b):
        return raw(a, b, _DIMS["tn"])

    nn.defvjp(lambda a, b: (nn(a, b), (a, b)), lambda r, g: (nt(g, r[1]), tn(r[0], g)))
    nt.defvjp(lambda a, b: (nt(a, b), (a, b)), lambda r, g: (nn(g, r[1]), tn(g, r[0])))
    tn.defvjp(lambda a, b: (tn(a, b), (a, b)), lambda r, g: (nt(r[1], g), nn(r[0], g)))
    return nn, nt, tn


_NN1, _NT1, _TN1 = _make_dot(1)
_NN3, _NT3, _TN3 = _make_dot(3)


def _l2norm(x):
    return x * lax.rsqrt(jnp.sum(x * x, axis=-1, keepdims=True) + NORM_EPS)


def _split_bf16(x):
    hi = x.astype(BF16)
    return hi, (x - hi.astype(F32)).astype(BF16)


def _products_with(lhs_list, rhs):
    n, rows = len(lhs_list), lhs_list[0].shape[0]
    r_hi, r_lo = _split_bf16(rhs)
    halves = [_split_bf16(l) for l in lhs_list]
    his = [h for h, _ in halves]
    o_hi = jnp.dot(jnp.concatenate(his + [lo for _, lo in halves], axis=0), r_hi, preferred_element_type=F32)
    o_lo = jnp.dot(jnp.concatenate(his, axis=0) if n > 1 else his[0], r_lo, preferred_element_type=F32)
    return [o_hi[i * rows:(i + 1) * rows] + o_hi[(n + i) * rows:(n + i + 1) * rows] + o_lo[i * rows:(i + 1) * rows]
            for i in range(n)]


def _unit_lower_inverse_raw(a_list):
    C = a_list[0].shape[0]
    ii = lax.broadcasted_iota(jnp.int32, (C, C), 0)
    jj = lax.broadcasted_iota(jnp.int32, (C, C), 1)
    eye = jnp.where(ii == jj, 1.0, 0.0)
    qs = [-a for a in a_list]
    ts = [eye + q for q in qs]
    qs = [_products_with([q], q)[0] for q in qs]
    n = 4
    while n <= C:
        last = n == C
        prods = [_products_with([t] if last else [t, q], q) for t, q in zip(ts, qs)]
        ts = [t + pr[0] for t, pr in zip(ts, prods)]
        if not last:
            qs = [pr[1] for pr in prods]
        n *= 2
    return ts


@jax.custom_vjp
def _unit_lower_inverse(a_list):
    return _unit_lower_inverse_raw(a_list)


def _unit_lower_inverse_fwd(a_list):
    ts = _unit_lower_inverse_raw(a_list)
    return ts, ts


def _unit_lower_inverse_bwd(ts, gs):
    xs = [_TN3(t, g) for t, g in zip(ts, gs)]
    return ([-_NT3(x, t) for x, t in zip(xs, ts)],)


_unit_lower_inverse.defvjp(_unit_lower_inverse_fwd, _unit_lower_inverse_bwd)


def _gdn_prep(units):
    C, Dh = units[0][0].shape
    ii = lax.broadcasted_iota(jnp.int32, (C, C), 0)
    jj = lax.broadcasted_iota(jnp.int32, (C, C), 1)
    lane = lax.broadcasted_iota(jnp.int32, (1, C), 1)
    causal = ii >= jj
    strict = ii > jj
    qs = [_l2norm(un[0]) * (Dh ** -0.5) for un in units]
    ks = [_l2norm(un[1]) for un in units]
    betas = [jax.nn.sigmoid(un[4]) for un in units]
    gs = [-jnp.exp(un[5]) * jax.nn.softplus(un[3] + un[6]) for un in units]
    gc_rows = [jnp.sum(jnp.where(ii <= jj, g, 0.0), axis=0, keepdims=True) for g in gs]
    gc_cols = [jnp.sum(jnp.where(ii == jj, r, 0.0), axis=1, keepdims=True) for r in gc_rows]
    decays = [jnp.where(causal, jnp.exp(jnp.where(causal, c - r, 0.0)), 0.0) for c, r in zip(gc_cols, gc_rows)]
    kbs = [k * b for k, b in zip(ks, betas)]
    kks = [_NT1(kb, k) for kb, k in zip(kbs, ks)]
    qks = [_NT1(q, k) for q, k in zip(qs, ks)]
    ts = _unit_lower_inverse([jnp.where(strict, kk * d, 0.0) for kk, d in zip(kks, decays)])
    eg_cols = [jnp.exp(c) for c in gc_cols]
    uws = [_NN3(t, jnp.concatenate([un[2] * b, kb * e], axis=1))
           for t, un, b, kb, e in zip(ts, units, betas, kbs, eg_cols)]
    out = []
    for q, k, qk, d, uw, e, r, c in zip(qs, ks, qks, decays, uws, eg_cols, gc_rows, gc_cols):
        g_last = jnp.sum(jnp.where(lane == C - 1, r, 0.0), axis=1, keepdims=True)
        out.append((q * e, k * jnp.exp(g_last - c), uw[:, :Dh], uw[:, Dh:], jnp.where(causal, qk * d, 0.0),
                    jnp.broadcast_to(jnp.exp(g_last), (1, Dh))))
    return out


def _gdn_step(units):
    v_news = [un[3] - _NN1(un[4], un[0]) for un in units]
    o_state = [_NN1(un[1], un[0]) for un in units]
    o_intra = [_NN1(un[5], vn) for un, vn in zip(units, v_news)]
    s_adds = [_TN1(un[2], vn) for un, vn in zip(units, v_news)]
    out = []
    for un, a, b, s_add in zip(units, o_state, o_intra, s_adds):
        o = a + b
        y = o * lax.rsqrt(jnp.mean(o * o, axis=-1, keepdims=True) + NORM_EPS) * un[8] * _silu(un[7])
        out.append((y, un[0] * un[6] + s_add))
    return out


def _gdn_prep_fwd(qkv, gab, gab_col, a_log, dt_bias, *, name, chunks=4):
    L = qkv.shape[0]
    H, C = GDN_HEADS, GDN_CHUNK
    W = qkv.shape[1] // 3
    Dh = W // H
    N = L // C
    chunks = min(chunks, N)
    R = chunks * C
    gab_off = gab_col // LANES

    def body(q_ref, k_ref, v_ref, gab_ref, al_ref, dt_ref, qe_ref, ke_ref, u_ref, w_ref, at_ref, eg_ref):
        where = [(cc, h) for cc in range(chunks) for h in range(H)]
        units = []
        for cc, h in where:
            rows, sl = slice(cc * C, (cc + 1) * C), slice(h * Dh, (h + 1) * Dh)
            units.append((q_ref[rows, sl], k_ref[rows, sl], v_ref[rows, sl], gab_ref[rows, h:h + 1],
                          gab_ref[rows, H + h:H + h + 1], al_ref[h], dt_ref[h]))
        for (cc, h), (qe, ke, u, w, attn, eg) in zip(where, _gdn_prep(units)):
            rows, sl = slice(cc * C, (cc + 1) * C), slice(h * Dh, (h + 1) * Dh)
            qe_ref[rows, sl] = qe
            ke_ref[rows, sl] = ke
            u_ref[rows, sl] = u
            w_ref[rows, sl] = w
            at_ref[h, rows, :] = attn
            eg_ref[cc, h:h + 1, :] = eg

    col = lambda c: pl.BlockSpec((R, W), lambda n, c=c: (n, c))
    tok = pl.BlockSpec((R, LANES), lambda n: (n, gab_off))
    par = pl.BlockSpec((H, 1, 1), lambda n: (0, 0, 0))
    wide = pl.BlockSpec((R, W), lambda n: (n, 0))
    return _pc(body, name=name, grid=(N // chunks,), in_specs=[col(0), col(1), col(2), tok, par, par],
               out_specs=[wide, wide, wide, wide, pl.BlockSpec((H, R, C), lambda n: (0, n, 0)),
                          pl.BlockSpec((chunks, H, Dh), lambda n: (n, 0, 0))],
               out_shape=[jax.ShapeDtypeStruct((L, W), F32)] * 4 + [jax.ShapeDtypeStruct((H, L, C), F32),
                                                                   jax.ShapeDtypeStruct((N, H, Dh), F32)],
               dims=("parallel",))(qkv, qkv, qkv, gab, a_log, dt_bias)


def _gdn_prep_bwd(qkv, gab, gab_col, a_log, dt_bias, dqe, dke, du, dw, dattn, deg, gab_width, *, name, chunks=4):
    L = qkv.shape[0]
    H, C = GDN_HEADS, GDN_CHUNK
    W = qkv.shape[1] // 3
    Dh = W // H
    N = L // C
    chunks = min(chunks, N)
    R = chunks * C
    gab_off = gab_col // LANES

    def body(q_ref, k_ref, v_ref, gab_ref, al_ref, dt_ref, dqe_ref, dke_ref, du_ref, dw_ref, dat_ref, deg_ref,
             dqkv_ref, dgab_ref, dal_ref, ddt_ref):
        first = pl.program_id(0) == 0
        lane = lax.broadcasted_iota(jnp.int32, (C, gab_width), 1)
        dal_sum, ddt_sum = [None] * H, [None] * H
        where = [(cc, h) for cc in range(chunks) for h in range(H)]
        units, cots = [], []
        for cc, h in where:
            rows, sl = slice(cc * C, (cc + 1) * C), slice(h * Dh, (h + 1) * Dh)
            units.append((q_ref[rows, sl], k_ref[rows, sl], v_ref[rows, sl], gab_ref[rows, h:h + 1],
                          gab_ref[rows, H + h:H + h + 1], al_ref[h], dt_ref[h]))
            cots.append((dqe_ref[rows, sl], dke_ref[rows, sl], du_ref[rows, sl], dw_ref[rows, sl],
                         dat_ref[h, rows, :], deg_ref[cc, h:h + 1, :]))
        _, vjp = jax.vjp(_gdn_prep, units)
        (d_units,) = vjp(cots)
        dgabs = [jnp.zeros((C, gab_width), F32) for _ in range(chunks)]
        for (cc, h), (dq, dk, dv, dga, dgb, dal, ddt) in zip(where, d_units):
            rows = slice(cc * C, (cc + 1) * C)
            dqkv_ref[rows, h * Dh:(h + 1) * Dh] = dq
            dqkv_ref[rows, W + h * Dh:W + (h + 1) * Dh] = dk
            dqkv_ref[rows, 2 * W + h * Dh:2 * W + (h + 1) * Dh] = dv
            dgabs[cc] = dgabs[cc] + jnp.where(lane == h, dga, 0.0) + jnp.where(lane == H + h, dgb, 0.0)
            dal_sum[h] = dal if dal_sum[h] is None else dal_sum[h] + dal
            ddt_sum[h] = ddt if ddt_sum[h] is None else ddt_sum[h] + ddt
        for cc in range(chunks):
            dgab_ref[cc * C:(cc + 1) * C, :] = dgabs[cc].astype(BF16)

        @pl.when(first)
        def _():
            for h in range(H):
                dal_ref[h] = dal_sum[h]
                ddt_ref[h] = ddt_sum[h]

        @pl.when(jnp.logical_not(first))
        def _():
            for h in range(H):
                dal_ref[h] += dal_sum[h]
                ddt_ref[h] += ddt_sum[h]

    col = lambda c: pl.BlockSpec((R, W), lambda n, c=c: (n, c))
    tok = pl.BlockSpec((R, LANES), lambda n: (n, gab_off))
    par = pl.BlockSpec((H, 1, 1), lambda n: (0, 0, 0))
    wide = pl.BlockSpec((R, W), lambda n: (n, 0))
    att = pl.BlockSpec((H, R, C), lambda n: (0, n, 0))
    egs = pl.BlockSpec((chunks, H, Dh), lambda n: (n, 0, 0))
    return _pc(body, name=name, grid=(N // chunks,),
               in_specs=[col(0), col(1), col(2), tok, par, par, wide, wide, wide, wide, att, egs],
               out_specs=[pl.BlockSpec((R, 3 * W), lambda n: (n, 0)), pl.BlockSpec((R, gab_width), lambda n: (n, 0)),
                          par, par],
               out_shape=[jax.ShapeDtypeStruct((L, 3 * W), F32), jax.ShapeDtypeStruct((L, gab_width), BF16)]
               + [jax.ShapeDtypeStruct((H, 1, 1), F32)] * 2,
               dims=("arbitrary",))(qkv, qkv, qkv, gab, a_log, dt_bias, dqe, dke, du, dw, dattn, deg)


def _gdn_scan_fwd(qe, ke, u, w, attn, eg, gz, gz_col, wgn, *, name):
    L, W = qe.shape
    H, C = GDN_HEADS, GDN_CHUNK
    Dh = W // H
    N = L // C
    gz_off = gz_col // W
    cps = min(GDN_SCAN_CHUNKS, N)
    R = cps * C

    def body(qe_ref, ke_ref, u_ref, w_ref, at_ref, eg_ref, gz_ref, wgn_ref, y_ref, st_ref, s_scr):
        @pl.when(pl.program_id(0) == 0)
        def _():
            s_scr[...] = jnp.zeros_like(s_scr)

        S = [s_scr[h] for h in range(H)]
        for cc in range(cps):
            rows = slice(cc * C, (cc + 1) * C)
            units = []
            for h in range(H):
                sl = slice(h * Dh, (h + 1) * Dh)
                st_ref[cc, h] = S[h]
                units.append((S[h], qe_ref[rows, sl], ke_ref[rows, sl], u_ref[rows, sl], w_ref[rows, sl],
                              at_ref[h, rows, :], eg_ref[cc, h:h + 1, :], gz_ref[rows, sl], wgn_ref[...]))
            for h, (y, S_new) in enumerate(_gdn_step(units)):
                y_ref[rows, h * Dh:(h + 1) * Dh] = y.astype(BF16)
                S[h] = S_new
        for h in range(H):
            s_scr[h] = S[h]

    wide = pl.BlockSpec((R, W), lambda n: (n, 0))
    return _pc(body, name=name, grid=(N // cps,),
               in_specs=[wide, wide, wide, wide, pl.BlockSpec((H, R, C), lambda n: (0, n, 0)),
                         pl.BlockSpec((cps, H, Dh), lambda n: (n, 0, 0)),
                         pl.BlockSpec((R, W), lambda n: (n, gz_off)), pl.BlockSpec((1, Dh), lambda n: (0, 0))],
               out_specs=[wide, pl.BlockSpec((cps, H, Dh, Dh), lambda n: (n, 0, 0, 0))],
               out_shape=[jax.ShapeDtypeStruct((L, W), BF16), jax.ShapeDtypeStruct((N, H, Dh, Dh), F32)],
               scratch_shapes=[pltpu.VMEM((H, Dh, Dh), F32)],
               dims=("arbitrary",))(qe, ke, u, w, attn, eg, gz, wgn.reshape(1, Dh))


def _gdn_scan_bwd(qe, ke, u, w, attn, eg, gz, gz_col, wgn, states, dy, dy_col, *, name):
    L, W = qe.shape
    H, C = GDN_HEADS, GDN_CHUNK
    Dh = W // H
    N = L // C
    gz_off = gz_col // W
    dy_off = dy_col // W
    cps = min(GDN_SCAN_CHUNKS, N)
    R = cps * C
    steps = N // cps

    def body(qe_ref, ke_ref, u_ref, w_ref, at_ref, eg_ref, gz_ref, wgn_ref, st_ref, dy_ref,
             dqe_ref, dke_ref, du_ref, dw_ref, dat_ref, deg_ref, dgz_ref, dwgn_ref, ds_scr):
        first = pl.program_id(0) == 0

        @pl.when(first)
        def _():
            ds_scr[...] = jnp.zeros_like(ds_scr)

        dwgn = None
        dS = [ds_scr[h] for h in range(H)]
        for cc in reversed(range(cps)):
            rows = slice(cc * C, (cc + 1) * C)
            units, cots = [], []
            for h in range(H):
                sl = slice(h * Dh, (h + 1) * Dh)
                units.append((st_ref[cc, h], qe_ref[rows, sl], ke_ref[rows, sl], u_ref[rows, sl], w_ref[rows, sl],
                              at_ref[h, rows, :], eg_ref[cc, h:h + 1, :], gz_ref[rows, sl], wgn_ref[...]))
                cots.append((dy_ref[rows, sl].astype(F32), dS[h]))
            _, vjp = jax.vjp(_gdn_step, units)
            (d_units,) = vjp(cots)
            for h, (dS_h, dqe, dke, du, dw, dat, deg, dgz, dwg) in enumerate(d_units):
                sl = slice(h * Dh, (h + 1) * Dh)
                dS[h] = dS_h
                dqe_ref[rows, sl] = dqe
                dke_ref[rows, sl] = dke
                du_ref[rows, sl] = du
                dw_ref[rows, sl] = dw
                dat_ref[h, rows, :] = dat
                deg_ref[cc, h:h + 1, :] = deg
                dgz_ref[rows, sl] = dgz.astype(BF16)
                dwgn = dwg if dwgn is None else dwgn + dwg
        for h in range(H):
            ds_scr[h] = dS[h]

        @pl.when(first)
        def _():
            dwgn_ref[...] = dwgn

        @pl.when(jnp.logical_not(first))
        def _():
            dwgn_ref[...] += dwgn

    rev = lambda n: steps - 1 - n
    wide = pl.BlockSpec((R, W), lambda n: (rev(n), 0))
    att = pl.BlockSpec((H, R, C), lambda n: (0, rev(n), 0))
    egs = pl.BlockSpec((cps, H, Dh), lambda n: (rev(n), 0, 0))
    vec = pl.BlockSpec((1, Dh), lambda n: (0, 0))
    return _pc(body, name=name, grid=(steps,),
               in_specs=[wide, wide, wide, wide, att, egs, pl.BlockSpec((R, W), lambda n: (rev(n), gz_off)), vec,
                         pl.BlockSpec((cps, H, Dh, Dh), lambda n: (rev(n), 0, 0, 0)),
                         pl.BlockSpec((R, W), lambda n: (rev(n), dy_off))],
               out_specs=[wide, wide, wide, wide, att, egs, wide, vec],
               out_shape=[jax.ShapeDtypeStruct((L, W), F32)] * 4 + [jax.ShapeDtypeStruct((H, L, C), F32),
                                                                   jax.ShapeDtypeStruct((N, H, Dh), F32),
                                                                   jax.ShapeDtypeStruct((L, W), BF16),
                                                                   jax.ShapeDtypeStruct((1, Dh), F32)],
               scratch_shapes=[pltpu.VMEM((H, Dh, Dh), F32)],
               dims=("arbitrary",))(qe, ke, u, w, attn, eg, gz, wgn.reshape(1, Dh), states, dy)


def _sb_scores(z, mask):
    sp = jnp.maximum(z, 0.0) + jnp.log(1.0 + jnp.exp(-jnp.abs(z)))
    lom = -sp if mask is None else jnp.where(mask, -sp, 0.0)
    return lom, z - sp


def _sb_alive(c_a, c_b):
    return jnp.maximum(jnp.max(c_a), jnp.max(c_b)) >= SB_DEAD_LOG


def _sb_masks(tq, width, dh):
    rr = lax.broadcasted_iota(jnp.int32, (tq, tq), 0)
    cc = lax.broadcasted_iota(jnp.int32, (tq, tq), 1)
    first_head = lax.broadcasted_iota(jnp.int32, (tq, width), 1) < dh
    return cc < rr, jnp.where(rr > cc, 1.0, 0.0).astype(BF16), first_head


def _sb_fwd(qkv, *, name, tq=256):
    L = qkv.shape[0]
    H = SB_HEADS
    width = 2 * (qkv.shape[1] // 3 // H)
    dh = width // 2
    npair = H // 2
    tq = min(tq, L)
    nq = L // tq

    def body(q_ref, k_ref, v_ref, o_ref):
        i = pl.program_id(1)
        diag, tri, first_head = _sb_masks(tq, width, dh)
        qp = q_ref[...]
        zero = jnp.zeros_like(qp)
        qs = (jnp.where(first_head, qp, zero), jnp.where(first_head, zero, qp))

        def blocks(js, carry, mask):
            units = [(b, hd) for b in range(len(js)) for hd in range(2)]
            starts = [pl.multiple_of(j * tq, tq) for j in js]
            ks = [k_ref[pl.ds(st, tq), :] for st in starts]
            vs = [v_ref[pl.ds(st, tq), :] for st in starts]
            zs = {(b, hd): lax.dot_general(qs[hd], ks[b], _DIMS["nt"], preferred_element_type=F32)
                  for b, hd in units}
            scores = {un: _sb_scores(zs[un], mask) for un in units}
            later = {un: jnp.dot(scores[un][0].astype(BF16), tri, preferred_element_type=F32) for un in units}
            cs = [carry[hd][0] for hd in range(2)]
            accs = [carry[hd][1] for hd in range(2)]
            for b, hd in units:
                lom, lb = scores[(b, hd)]
                a = jnp.exp(lb + (cs[hd] + later[(b, hd)]))
                if mask is not None:
                    a = jnp.where(mask, a, 0.0)
                accs[hd] = accs[hd] + jnp.dot(a.astype(BF16), vs[b], preferred_element_type=F32)
                cs[hd] = cs[hd] + jnp.sum(lom, axis=1, keepdims=True)
            return tuple((cs[hd], accs[hd]) for hd in range(2))

        init = tuple((jnp.zeros((tq, 1), F32), jnp.zeros((tq, width), F32)) for _ in range(2))
        carry = blocks([i], init, diag)
        j_end, carry = lax.while_loop(lambda st: jnp.logical_and(st[0] >= 0, _sb_alive(st[1][0][0], st[1][1][0])),
                                      lambda st: (st[0] - 1, blocks([st[0]], st[1], None)), (i - 1, carry))
        o_ref[...] = jnp.where(first_head, carry[0][1], carry[1][1]).astype(BF16)

    return _pc(body, name=name, grid=(npair, nq),
               in_specs=[pl.BlockSpec((tq, width), lambda p, i: (i, p)),
                         pl.BlockSpec((L, width), lambda p, i: (0, npair + p)),
                         pl.BlockSpec((L, width), lambda p, i: (0, 2 * npair + p))],
               out_specs=pl.BlockSpec((tq, width), lambda p, i: (i, p)),
               out_shape=jax.ShapeDtypeStruct((L, npair * width), BF16),
               dims=("parallel", "parallel"))(qkv, qkv, qkv)


def _sb_bwd(qkv, do, do_col, scale, *, name, tq=256):
    L = qkv.shape[0]
    H = SB_HEADS
    width = 2 * (qkv.shape[1] // 3 // H)
    dh = width // 2
    npair = H // 2
    tq = min(tq, L)
    nq = L // tq
    do_off = do_col // width

    def body(q_ref, k_ref, v_ref, do_ref, dq_ref, dk_ref, dv_ref):
        i = pl.program_id(1)

        @pl.when(i == 0)
        def _():
            dk_ref[...] = jnp.zeros_like(dk_ref)
            dv_ref[...] = jnp.zeros_like(dv_ref)

        diag, tri_later, first_head = _sb_masks(tq, width, dh)
        rr = lax.broadcasted_iota(jnp.int32, (tq, tq), 0)
        cc = lax.broadcasted_iota(jnp.int32, (tq, tq), 1)
        tri_before = jnp.where(rr < cc, 1.0, 0.0).astype(BF16)
        qp = q_ref[...]
        dop = do_ref[...].astype(BF16)
        zero = jnp.zeros_like(qp)
        qs = (jnp.where(first_head, qp, zero), jnp.where(first_head, zero, qp))
        dos = (jnp.where(first_head, dop, zero), jnp.where(first_head, zero, dop))
        ctots = []

        def blocks(js, carry, mask):
            nb = len(js)
            units = [(b, hd) for b in range(nb) for hd in range(2)]
            starts = [pl.multiple_of(j * tq, tq) for j in js]
            ks = [k_ref[pl.ds(st, tq), :] for st in starts]
            vs = [v_ref[pl.ds(st, tq), :] for st in starts]
            zs = {(b, hd): lax.dot_general(qs[hd], ks[b], _DIMS["nt"], preferred_element_type=F32)
                  for b, hd in units}
            das = {(b, hd): lax.dot_general(dos[hd], vs[b], _DIMS["nt"], preferred_element_type=F32)
                   for b, hd in units}
            scores = {un: _sb_scores(zs[un], mask) for un in units}
            later = {un: jnp.dot(scores[un][0].astype(BF16), tri_later, preferred_element_type=F32) for un in units}
            pcs = [carry[hd][0] for hd in range(2)]
            avals = {}
            for b, hd in units:
                pcs[hd] = pcs[hd] + jnp.sum(scores[(b, hd)][0], axis=1, keepdims=True)
                a = jnp.exp(scores[(b, hd)][1] + ((ctots[hd] - pcs[hd]) + later[(b, hd)]))
                avals[(b, hd)] = a if mask is None else jnp.where(mask, a, 0.0)
            gs = {un: das[un] * avals[un] for un in units}
            before = {un: jnp.dot(gs[un].astype(BF16), tri_before, preferred_element_type=F32) for un in units}
            pgs = [carry[hd][1] for hd in range(2)]
            dzs = {}
            for b, hd in units:
                sig = jnp.exp(scores[(b, hd)][1])
                dz = gs[(b, hd)] * (1.0 - sig) - (pgs[hd] + before[(b, hd)]) * sig
                dzs[(b, hd)] = (dz if mask is None else jnp.where(mask, dz, 0.0)).astype(BF16)
                pgs[hd] = pgs[hd] + jnp.sum(gs[(b, hd)], axis=1, keepdims=True)
            dqs = [carry[hd][2] for hd in range(2)]
            for b, hd in units:
                dqs[hd] = dqs[hd] + jnp.dot(dzs[(b, hd)], ks[b], preferred_element_type=F32)
            for b in range(nb):
                dk_ref[pl.ds(starts[b], tq), :] += sum(
                    lax.dot_general(dzs[(b, hd)], qs[hd], _DIMS["tn"], preferred_element_type=F32) for hd in range(2))
                dv_ref[pl.ds(starts[b], tq), :] += sum(
                    lax.dot_general(avals[(b, hd)].astype(BF16), dos[hd], _DIMS["tn"], preferred_element_type=F32)
                    for hd in range(2))
            return tuple((pcs[hd], pgs[hd], dqs[hd]) for hd in range(2))

        def row_sums(j, mask):
            kj = k_ref[pl.ds(pl.multiple_of(j * tq, tq), tq), :]
            return tuple(jnp.sum(_sb_scores(lax.dot_general(qs[hd], kj, _DIMS["nt"], preferred_element_type=F32),
                                            mask)[0], axis=1, keepdims=True) for hd in range(2))

        j_dead, live_sums = lax.while_loop(
            lambda st: jnp.logical_and(st[0] >= 0, _sb_alive(st[1][0], st[1][1])),
            lambda st: (st[0] - 1, tuple(a + b for a, b in zip(st[1], row_sums(st[0], None)))),
            (i - 1, row_sums(i, diag)))
        ctots.extend(live_sums)
        col = jnp.zeros((tq, 1), F32)
        init = tuple((col, col, jnp.zeros((tq, width), F32)) for _ in range(2))
        carry = lax.fori_loop(j_dead + 1, i, lambda j, cr: blocks([j], cr, None), init)
        carry = blocks([i], carry, diag)
        dq_ref[...] = (jnp.where(first_head, carry[0][2], carry[1][2]) * scale).astype(BF16)

    tile = pl.BlockSpec((tq, width), lambda p, i: (i, p))
    full = pl.BlockSpec((L, width), lambda p, i: (0, p))
    sds = jax.ShapeDtypeStruct((L, npair * width), F32)
    return _pc(body, name=name, grid=(npair, nq),
               in_specs=[tile, pl.BlockSpec((L, width), lambda p, i: (0, npair + p)),
                         pl.BlockSpec((L, width), lambda p, i: (0, 2 * npair + p)),
                         pl.BlockSpec((tq, width), lambda p, i: (i, do_off + p))],
               out_specs=[tile, full, full],
               out_shape=[jax.ShapeDtypeStruct((L, npair * width), BF16), sds, sds],
               dims=("parallel", "arbitrary"))(qkv, qkv, qkv, do)


def _adamw(w, g, m, v, *, name, tm=256):
    R, C = w.shape
    tm = min(tm, R)
    assert R % tm == 0, (R, tm)
    c1 = 1.0 - ADAM_B1 ** ADAM_STEP
    c2 = 1.0 - ADAM_B2 ** ADAM_STEP

    def body(w_ref, g_ref, m_ref, v_ref, d_ref, nm_ref, nv_ref):
        gv = g_ref[...]
        nm = ADAM_B1 * m_ref[...] + (1.0 - ADAM_B1) * gv
        nv = ADAM_B2 * v_ref[...] + (1.0 - ADAM_B2) * (gv * gv)
        d_ref[...] = -ADAM_LR * ((nm / c1) / (jnp.sqrt(nv / c2) + ADAM_EPS) + ADAM_WD * w_ref[...])
        nm_ref[...] = nm
        nv_ref[...] = nv

    blk = pl.BlockSpec((tm, C), lambda i: (i, 0))
    sds = jax.ShapeDtypeStruct((R, C), F32)
    return _pc(body, name=name, grid=(R // tm,), in_specs=[blk] * 4, out_specs=[blk] * 3, out_shape=[sds] * 3,
               dims=("parallel",))(w, g, m, v)


ELEMENTWISE_BLOCK_BYTES = 1 << 20


def _row_tile(rows, cols):
    for t in (512, 384, 352, 256, 176, 128, 88, 64, 32, 16, 8):
        if rows % t == 0 and t * cols * 4 <= ELEMENTWISE_BLOCK_BYTES:
            return t
    raise ValueError((rows, cols))


def _adamw_layers(w, g_mine, g_other, m, v, c, *, name):
    _, R, C = w.shape
    tm = _row_tile(R, C)
    c1 = 1.0 - ADAM_B1 ** ADAM_STEP
    c2 = 1.0 - ADAM_B2 ** ADAM_STEP

    def body(c_ref, w_ref, gm_ref, go_ref, m_ref, v_ref, g_ref, d_ref, nm_ref, nv_ref):
        gv = jnp.where(pl.program_id(0) == c_ref[0], gm_ref[...], go_ref[...])
        nm = ADAM_B1 * m_ref[...] + (1.0 - ADAM_B1) * gv
        nv = ADAM_B2 * v_ref[...] + (1.0 - ADAM_B2) * (gv * gv)
        g_ref[...] = gv
        d_ref[...] = -ADAM_LR * ((nm / c1) / (jnp.sqrt(nv / c2) + ADAM_EPS) + ADAM_WD * w_ref[...])
        nm_ref[...] = nm
        nv_ref[...] = nv

    slab = pl.BlockSpec((None, tm, C), lambda l, i, c_ref: (l, i, 0))
    mine = pl.BlockSpec((tm, C), lambda l, i, c_ref: (jnp.where(l == c_ref[0], i, 0), 0))
    other = pl.BlockSpec((tm, C), lambda l, i, c_ref: (jnp.where(l == c_ref[0], 0, i), 0))
    grid_spec = pltpu.PrefetchScalarGridSpec(num_scalar_prefetch=1, grid=(2, R // tm),
                                             in_specs=[slab, mine, other, slab, slab], out_specs=[slab] * 4)
    return _pc_prefetch(body, name=name, grid_spec=grid_spec, out_shape=[jax.ShapeDtypeStruct(w.shape, F32)] * 4,
                        dims=("parallel", "parallel"))(c.reshape(1).astype(jnp.int32), w, g_mine, g_other, m, v)


def _add_layers(g0, g1, ra, c, *, name):
    S, R, C = ra.shape
    tm = _row_tile(R, C)

    def body(c_ref, g0_ref, g1_ref, r_ref, o_ref):
        mine = jnp.where(c_ref[0] == 0, g0_ref[...], g1_ref[...])
        o_ref[...] = (mine + r_ref[...]).astype(BF16)

    def walked_if(layer):
        return lambda s, i, c_ref: (jnp.where(c_ref[0] == layer, s, 0), jnp.where(c_ref[0] == layer, i, 0), 0)

    blk = lambda s, i, c_ref: (s, i, 0)
    grid_spec = pltpu.PrefetchScalarGridSpec(
        num_scalar_prefetch=1, grid=(S, R // tm),
        in_specs=[pl.BlockSpec((None, tm, C), walked_if(0)), pl.BlockSpec((None, tm, C), walked_if(1)),
                  pl.BlockSpec((None, tm, C), blk)],
        out_specs=pl.BlockSpec((None, tm, C), blk))
    return _pc_prefetch(body, name=name, grid_spec=grid_spec, out_shape=jax.ShapeDtypeStruct((S, R, C), BF16),
                        dims=("parallel", "parallel"))(c.reshape(1).astype(jnp.int32), g0, g1, ra)


def _add_chips(p, rb, chip, *, name):
    S, Rh, C = p.shape
    tm = _row_tile(Rh, C)

    def body(s_ref, p_ref, r_ref, o_ref):
        o_ref[...] = ((p_ref[...].astype(F32) + r_ref[0].astype(F32)) + r_ref[1].astype(F32)) + r_ref[2].astype(F32)

    grid_spec = pltpu.PrefetchScalarGridSpec(
        num_scalar_prefetch=1, grid=(Rh // tm,),
        in_specs=[pl.BlockSpec((None, tm, C), lambda i, s_ref: (s_ref[0], i, 0)),
                  pl.BlockSpec((3, tm, C), lambda i, s_ref: (0, i, 0))],
        out_specs=pl.BlockSpec((tm, C), lambda i, s_ref: (i, 0)))
    return _pc_prefetch(body, name=name, grid_spec=grid_spec, out_shape=jax.ShapeDtypeStruct((Rh, C), F32),
                        dims=("parallel",))(chip.reshape(1).astype(jnp.int32), p, rb)


def _sum_slots(g, *, name):
    n, R, C = g.shape

    def body(g_ref, o_ref):
        acc = g_ref[0]
        for s in range(1, n):
            acc = acc + g_ref[s]
        o_ref[...] = acc

    return _pc(body, name=name, grid=(1,), in_specs=[pl.BlockSpec((n, R, C), lambda i: (0, 0, 0))],
               out_specs=pl.BlockSpec((R, C), lambda i: (0, 0)), out_shape=jax.ShapeDtypeStruct((R, C), F32),
               dims=("arbitrary",))(g)


ANY = pl.BlockSpec(memory_space=pl.ANY)


def _place():
    return lax.axis_index("x"), lax.axis_index("y"), lax.axis_index("c")


def _other_chips(x, y):
    return [(1 - x, y), (x, 1 - y), (1 - x, 1 - y)]


def _allgather_chips(ws, *, name):
    n = len(ws)

    def body(*refs):
        w_refs, out_refs, send_sems, recv_sems = refs[:n], refs[n:2 * n], refs[2 * n], refs[2 * n + 1]
        x, y, c = _place()
        sib = (x, y, 1 - c)
        chips = _other_chips(x, y)

        def copy(a, k, chip_id, layer, to):
            src = w_refs[a].at[layer] if k < 3 else out_refs[a].at[chip_id, layer]
            return pltpu.make_async_remote_copy(src_ref=src, dst_ref=out_refs[a].at[chip_id, layer],
                                                send_sem=send_sems.at[k * n + a], recv_sem=recv_sems.at[k * n + a],
                                                device_id=to, device_id_type=MESH)

        sends = [copy(a, j, 2 * x + y, c, (px, py, c)) for j, (px, py) in enumerate(chips) for a in range(n)]
        for cp in sends:
            cp.start()
        passed = []
        for j, (px, py) in enumerate(chips):
            for a in range(n):
                copy(a, j, 2 * px + py, c, (px, py, c)).wait_recv()
                fwd = copy(a, 3 + j, 2 * px + py, c, sib)
                fwd.start()
                passed.append(fwd)
        for j, (px, py) in enumerate(chips):
            for a in range(n):
                copy(a, 3 + j, 2 * px + py, 1 - c, sib).wait_recv()
        for cp in sends + passed:
            cp.wait_send()

    return _pc_comm(body, name=name, in_specs=[ANY] * n, out_specs=[ANY] * n,
                    out_shape=[jax.ShapeDtypeStruct((N_CHIPS,) + w.shape, w.dtype) for w in ws],
                    scratch_shapes=[pltpu.SemaphoreType.DMA((6 * n,)), pltpu.SemaphoreType.DMA((6 * n,))])(*ws)


def _send_other_layer_to_sibling(g0s, g1s, *, name):
    n = len(g0s)

    def body(*refs):
        g_refs = (refs[:n], refs[n:2 * n])
        out_refs, send_sems, recv_sems = refs[2 * n:3 * n], refs[3 * n], refs[3 * n + 1]
        x, y, c = _place()

        def copy(a, layer):
            return pltpu.make_async_remote_copy(src_ref=g_refs[layer][a], dst_ref=out_refs[a], send_sem=send_sems.at[a],
                                                recv_sem=recv_sems.at[a], device_id=(x, y, 1 - c), device_id_type=MESH)

        for layer in range(2):
            @pl.when(c == 1 - layer)
            def _(layer=layer):
                for a in range(n):
                    copy(a, layer).start()
        for a in range(n):
            copy(a, 0).wait()

    return _pc_comm(body, name=name, in_specs=[ANY] * (2 * n), out_specs=[ANY] * n,
                    out_shape=[jax.ShapeDtypeStruct(g.shape, g.dtype) for g in g0s],
                    scratch_shapes=[pltpu.SemaphoreType.DMA((n,)), pltpu.SemaphoreType.DMA((n,))])(*g0s, *g1s)


def _scatter_to_chips(ps, *, name):
    n = len(ps)

    def body(*refs):
        p_refs, rb_refs, send_sems, recv_sems = refs[:n], refs[n:2 * n], refs[2 * n], refs[2 * n + 1]
        x, y, c = _place()
        chips = _other_chips(x, y)
        sends = [pltpu.make_async_remote_copy(src_ref=p_refs[a].at[2 * px + py], dst_ref=rb_refs[a].at[j],
                                              send_sem=send_sems.at[j * n + a], recv_sem=recv_sems.at[j * n + a],
                                              device_id=(px, py, c), device_id_type=MESH)
                 for j, (px, py) in enumerate(chips) for a in range(n)]
        for cp in sends:
            cp.start()
        for cp in sends:
            cp.wait()

    return _pc_comm(body, name=name, in_specs=[ANY] * n, out_specs=[ANY] * n,
                    out_shape=[jax.ShapeDtypeStruct((3,) + p.shape[1:], p.dtype) for p in ps],
                    scratch_shapes=[pltpu.SemaphoreType.DMA((3 * n,)), pltpu.SemaphoreType.DMA((3 * n,))])(*ps)


def _swap_with_sibling(fs, *, name):
    n = len(fs)

    def body(*refs):
        f_refs, out_refs, send_sems, recv_sems = refs[:n], refs[n:2 * n], refs[2 * n], refs[2 * n + 1]
        x, y, c = _place()
        copies = [pltpu.make_async_remote_copy(src_ref=f_refs[a], dst_ref=out_refs[a], send_sem=send_sems.at[a],
                                               recv_sem=recv_sems.at[a], device_id=(x, y, 1 - c), device_id_type=MESH)
                  for a in range(n)]
        for cp in copies:
            cp.start()
        for cp in copies:
            cp.wait()

    return _pc_comm(body, name=name, in_specs=[ANY] * n, out_specs=[ANY] * n,
                    out_shape=[jax.ShapeDtypeStruct(f.shape, f.dtype) for f in fs],
                    scratch_shapes=[pltpu.SemaphoreType.DMA((n,)), pltpu.SemaphoreType.DMA((n,))])(*fs)


def _allgather_devices(v, *, name):
    R, C = v.shape

    def body(v_ref, out_ref, send_sems, recv_sems):
        x, y, c = _place()
        me = 4 * x + 2 * y + c
        out_ref[me] = v_ref[...]
        peers = []
        for k in range(1, 8):
            fx, fy, fc = (k >> 2) & 1, (k >> 1) & 1, k & 1
            px = 1 - x if fx else x
            py = 1 - y if fy else y
            pcc = 1 - c if fc else c
            peers.append((px, py, pcc))
        sends = []
        for k, peer in enumerate(peers):
            cp = pltpu.make_async_remote_copy(src_ref=v_ref, dst_ref=out_ref.at[me], send_sem=send_sems.at[k],
                                              recv_sem=recv_sems.at[k], device_id=peer, device_id_type=MESH)
            cp.start()
            sends.append(cp)
        for k, (px, py, pcc) in enumerate(peers):
            pltpu.make_async_remote_copy(src_ref=v_ref, dst_ref=out_ref.at[4 * px + 2 * py + pcc],
                                         send_sem=send_sems.at[k], recv_sem=recv_sems.at[k], device_id=peers[k],
                                         device_id_type=MESH).wait_recv()
        for cp in sends:
            cp.wait_send()

    vm = pl.BlockSpec(memory_space=pltpu.VMEM)
    return _pc_comm(body, name=name, in_specs=[vm], out_specs=vm, out_shape=jax.ShapeDtypeStruct((8, R, C), F32),
                    scratch_shapes=[pltpu.SemaphoreType.DMA((7,)), pltpu.SemaphoreType.DMA((7,))])(v)


D_MODEL = 1024
SC_W = D_MODEL // 4
GDN_W = D_MODEL // 2
SB_W = D_MODEL - SC_W - GDN_W
D_FF = 256 * ((8 * D_MODEL // 3 + 255) // 256)
O_SC, O_GQKV, O_GZ, O_GA, O_SB = 0, 3 * SC_W, 3 * SC_W + 3 * GDN_W, 3 * SC_W + 4 * GDN_W, \
    3 * SC_W + 4 * GDN_W + 2 * GDN_HEADS
P_GQKV, P_SC, P_SB = 0, 3 * GDN_W, 3 * GDN_W + 3 * SC_W
P_GZ = P_SB + 3 * SB_W
P_GAB = P_GZ + GDN_W
P_PAD = 256


def _proj_to_kernel_layout(w):
    pad = jnp.zeros((w.shape[0], P_PAD - 2 * GDN_HEADS), w.dtype)
    return jnp.concatenate([w[:, O_GQKV:O_GZ], w[:, O_SC:O_GQKV], w[:, O_SB:], w[:, O_GZ:O_GA], w[:, O_GA:O_SB], pad],
                           axis=1)


def _proj_from_kernel_layout(g):
    return jnp.concatenate([g[:, P_SC:P_SB], g[:, P_GQKV:P_SC], g[:, P_GZ:P_GAB], g[:, P_GAB:P_GAB + 2 * GDN_HEADS],
                            g[:, P_SB:P_GZ]], axis=1)


def _mixout_to_kernel_layout(w):
    return jnp.concatenate([w[SC_W:SC_W + GDN_W], w[:SC_W], w[SC_W + GDN_W:]], axis=0)


def _pack_vec(parts, rows_to):
    flat = jnp.concatenate([p.reshape(-1) for p in parts])
    return jnp.pad(flat, (0, rows_to * LANES - flat.shape[0])).reshape(rows_to, LANES)


def _unpack_vec(mat, shapes):
    flat = mat.reshape(-1)
    out, r = [], 0
    for shp in shapes:
        n = int(np.prod(shp))
        out.append(flat[r:r + n].reshape(shp))
        r += n
    return out


def _round_up(n, m):
    return (n + m - 1) // m * m


def _layer_fwd(x, p, l):
    L = x.shape[0]
    tag = "l%d_" % l
    h = _rmsnorm_fwd(x, p["wn_mix"], name=tag + "norm_mix")
    proj = _matmul(h, p["w_in"], "nn", tm=2048, tn=768, tk=D_MODEL, name=tag + "proj")
    (y_sc,) = _conv_pointwise_fwd([(proj, P_SC + SC_W), (proj, P_SC + 2 * SC_W)], [(p["w_sconv"], 0)], [(proj, P_SC)],
                                  _pre_product, _post_gate_mul, [(SC_W, BF16)], tc=SC_W, tm=512, name=tag + "sconv")
    (qkv,) = _conv_pointwise_fwd([(proj, P_GQKV)], [(p["w_gdn_conv"], 0)], [], _pre_identity, _post_silu,
                                 [(3 * GDN_W, F32)], tc=GDN_W, tm=512, name=tag + "gdn_conv")
    qe, ke, u, w, attn, eg = _gdn_prep_fwd(qkv, proj, P_GAB, p["a_log"], p["dt_bias"], name=tag + "gdn_prep")
    y_gdn, states = _gdn_scan_fwd(qe, ke, u, w, attn, eg, proj, P_GZ, p["wgn"], name=tag + "gdn_scan")
    sb_scale = (SB_W // SB_HEADS) ** -0.5
    sbqkv = jnp.concatenate([proj[:, P_SB:P_SB + SB_W] * sb_scale, proj[:, P_SB + SB_W:P_SB + 3 * SB_W]],
                            axis=1).astype(BF16)
    y_sb = _sb_fwd(sbqkv, name=tag + "sb_fwd")
    y_cat = [y_gdn, y_sc, y_sb]
    x2 = _matmul_rows_parts(y_cat, p["w_out"], "nn", res=x, name=tag + "mix_out")
    h2 = _rmsnorm_fwd(x2, p["wn_ffn"], name=tag + "norm_ffn")
    up_g = _matmul(h2, p["w_up_g"], "nn", tm=1024, tn=D_FF // 2, tk=D_MODEL, name=tag + "up_gate")
    up_v = _matmul(h2, p["w_up_v"], "nn", tm=1024, tn=D_FF // 2, tk=D_MODEL, name=tag + "up_val")
    (act,) = _conv_pointwise_fwd([(up_g, 0), (up_v, 0)], [(p["w_fconv_g"], 0), (p["w_fconv_v"], 0)], [],
                                 _pre_identity, _post_swiglu, [(D_FF, BF16)], tc=256, tm=1024, name=tag + "ffn_act")
    x3 = _matmul(act, p["w_down"], "nn", tm=1024, tn=D_MODEL, tk=D_FF // 2, res=x2, name=tag + "ffn_down")
    saved = dict(x=x, h=h, proj=proj, qkv=qkv, qe=qe, ke=ke, u=u, w=w, attn=attn, eg=eg, states=states,
                 sbqkv=sbqkv, y_cat=y_cat, x2=x2, h2=h2, up_g=up_g, up_v=up_v, act=act)
    return x3, saved


def _layer_bwd(dx3, p, s, l):
    L = dx3.shape[0]
    tag = "l%d_b_" % l
    g = {}
    dact = _matmul(dx3, p["w_down"], "nt", tm=1024, tn=D_FF // 2, tk=D_MODEL, name=tag + "dact")
    g["w_down"] = _matmul(s["act"], dx3, "tn", tm=D_FF // 2, tn=D_MODEL, tk=1024, name=tag + "dw_down")
    (dup_g, dup_v), _, (g["w_fconv_g"], g["w_fconv_v"]) = _conv_pointwise_bwd(
        [(s["up_g"], 0), (s["up_v"], 0)], [(p["w_fconv_g"], 0), (p["w_fconv_v"], 0)], [], [(dact, 0)],
        _pre_identity, _post_swiglu, D_FF, tc=256, tm=1024, name=tag + "ffn_act")
    dh2 = _matmul(dup_g, p["w_up_g"], "nt", tm=1024, tn=D_MODEL, tk=D_FF // 2, name=tag + "dh2_gate")
    dh2 = _matmul(dup_v, p["w_up_v"], "nt", tm=1024, tn=D_MODEL, tk=D_FF // 2, res=dh2, name=tag + "dh2_val")
    g["w_up"] = _matmul(s["h2"], dup_g, "tn", tm=D_MODEL, tn=D_FF // 2, tk=1024, slabs=(N_CHIPS, 0, None),
                        name=tag + "dw_up_gate")
    g["w_up"] = _matmul(s["h2"], dup_v, "tn", tm=D_MODEL, tn=D_FF // 2, tk=1024, slabs=(N_CHIPS, 2, g["w_up"]),
                        name=tag + "dw_up_val")
    dx2, g["wn_ffn"] = _rmsnorm_bwd(dh2, s["x2"], p["wn_ffn"], dx3, name=tag + "norm_ffn")
    dycat = _matmul(dx2, p["w_out"], "nt", tm=512, tn=D_MODEL, tk=D_MODEL, name=tag + "dycat")
    y_gdn, y_sc, y_sb = s["y_cat"]
    g["w_out"] = _matmul_tn_parts([y_sc, y_gdn, y_sb], dx2, name=tag + "dw_out")
    sb_scale = (SB_W // SB_HEADS) ** -0.5
    dsq, dsk, dsv = _sb_bwd(s["sbqkv"], dycat, GDN_W + SC_W, sb_scale, name=tag + "sb_bwd")
    dqe, dke, du, dw, dattn, deg, dgz, g["wgn"] = _gdn_scan_bwd(
        s["qe"], s["ke"], s["u"], s["w"], s["attn"], s["eg"], s["proj"], P_GZ, p["wgn"], s["states"], dycat, 0,
        name=tag + "gdn_scan")
    dqkv_act, dgab, g["a_log"], g["dt_bias"] = _gdn_prep_bwd(
        s["qkv"], s["proj"], P_GAB, p["a_log"], p["dt_bias"], dqe, dke, du, dw, dattn, deg, P_PAD,
        name=tag + "gdn_prep")
    (dqkv,), _, (g["w_gdn_conv"],) = _conv_pointwise_bwd(
        [(s["proj"], P_GQKV)], [(p["w_gdn_conv"], 0)], [], [(dqkv_act, 0)], _pre_identity, _post_silu, 3 * GDN_W,
        tc=GDN_W, tm=512, name=tag + "gdn_conv")
    (dsc_c, dsc_h), (dsc_b,), (g["w_sconv"],) = _conv_pointwise_bwd(
        [(s["proj"], P_SC + SC_W), (s["proj"], P_SC + 2 * SC_W)], [(p["w_sconv"], 0)], [(s["proj"], P_SC)],
        [(dycat, GDN_W)], _pre_product, _post_gate_mul, SC_W, tc=SC_W, tm=512, name=tag + "sconv")
    dproj = [dqkv, dsc_b, dsc_c, dsc_h, dsq, dsk, dsv, dgz, dgab]
    dh = _matmul_rows_parts(dproj, p["w_in"], "nt", name=tag + "dh")
    g["w_in"] = jnp.concatenate([_matmul_tn_parts(s["h"], dproj[:4], name=tag + "dw_in_a"),
                                 _matmul_tn_parts(s["h"], dproj[4:], name=tag + "dw_in_b")], axis=1)
    dx, g["wn_mix"] = _rmsnorm_bwd(dh, s["x"], p["wn_mix"], dx2, name=tag + "norm_mix")
    return dx, g


BIG = ("w_mix_in", "w_mix_out", "w_ffn_up", "w_ffn_down")
BIG_AXIS = {"w_mix_in": 2, "w_mix_out": 1, "w_ffn_up": 2, "w_ffn_down": 1}
SMALL_SHARDED = ("w_sconv", "w_gdn_conv", "w_ffn_conv")
SMALL_REPLICATED = ("w_norm_mix", "gdn_a_log", "gdn_dt_bias", "w_gdn_norm", "w_norm_ffn", "w_norm_final")
WEIGHTS = ("w_norm_mix", "w_mix_in", "w_sconv", "w_gdn_conv", "gdn_a_log", "gdn_dt_bias", "w_gdn_norm", "w_mix_out",
           "w_norm_ffn", "w_ffn_up", "w_ffn_conv", "w_ffn_down", "w_norm_final")


def kernel(x, w_norm_mix, w_mix_in, w_sconv, w_gdn_conv, gdn_a_log, gdn_dt_bias, w_gdn_norm, w_mix_out, w_norm_ffn, w_ffn_up, w_ffn_conv, w_ffn_down, w_norm_final, loss_target, m_w_norm_mix, m_w_mix_in, m_w_sconv, m_w_gdn_conv, m_gdn_a_log, m_gdn_dt_bias, m_w_gdn_norm, m_w_mix_out, m_w_norm_ffn, m_w_ffn_up, m_w_ffn_conv, m_w_ffn_down, m_w_norm_final, v_w_norm_mix, v_w_mix_in, v_w_sconv, v_w_gdn_conv, v_gdn_a_log, v_gdn_dt_bias, v_w_gdn_norm, v_w_mix_out, v_w_norm_ffn, v_w_ffn_up, v_w_ffn_conv, v_w_ffn_down, v_w_norm_final):
    W = dict(w_norm_mix=w_norm_mix, w_mix_in=w_mix_in, w_sconv=w_sconv, w_gdn_conv=w_gdn_conv, gdn_a_log=gdn_a_log,
             gdn_dt_bias=gdn_dt_bias, w_gdn_norm=w_gdn_norm, w_mix_out=w_mix_out, w_norm_ffn=w_norm_ffn,
             w_ffn_up=w_ffn_up, w_ffn_conv=w_ffn_conv, w_ffn_down=w_ffn_down, w_norm_final=w_norm_final)
    M = dict(w_norm_mix=m_w_norm_mix, w_mix_in=m_w_mix_in, w_sconv=m_w_sconv, w_gdn_conv=m_w_gdn_conv,
             gdn_a_log=m_gdn_a_log, gdn_dt_bias=m_gdn_dt_bias, w_gdn_norm=m_w_gdn_norm, w_mix_out=m_w_mix_out,
             w_norm_ffn=m_w_norm_ffn, w_ffn_up=m_w_ffn_up, w_ffn_conv=m_w_ffn_conv, w_ffn_down=m_w_ffn_down,
             w_norm_final=m_w_norm_final)
    V = dict(w_norm_mix=v_w_norm_mix, w_mix_in=v_w_mix_in, w_sconv=v_w_sconv, w_gdn_conv=v_w_gdn_conv,
             gdn_a_log=v_gdn_a_log, gdn_dt_bias=v_gdn_dt_bias, w_gdn_norm=v_w_gdn_norm, w_mix_out=v_w_mix_out,
             w_norm_ffn=v_w_norm_ffn, w_ffn_up=v_w_ffn_up, w_ffn_conv=v_w_ffn_conv, w_ffn_down=v_w_ffn_down,
             w_norm_final=v_w_norm_final)
    depth = w_mix_in.shape[0]
    L = x.shape[1]
    mx, my, mc = lax.axis_index("x"), lax.axis_index("y"), lax.axis_index("c")
    chip = 2 * mx + my

    assert depth == 2
    own = [W[n].astype(BF16) for n in BIG]
    gathered = _allgather_chips(own, name="gather_big")
    gathered = [lax.dynamic_update_slice(g, o[None], (chip, 0, 0, 0)) for g, o in zip(gathered, own)]
    full_big = [{n: jnp.concatenate([g[b, l] for b in range(N_CHIPS)], axis=BIG_AXIS[n] - 1)
                 for n, g in zip(BIG, gathered)} for l in range(depth)]

    small_sh_shapes = [W[n].shape for n in SMALL_SHARDED]
    n_small_sh = sum(int(np.prod(s)) for s in small_sh_shapes)
    small_rows = _round_up(n_small_sh, 8 * LANES) // LANES
    small_all = _allgather_devices(_pack_vec([W[n] for n in SMALL_SHARDED], small_rows), name="gather_small")
    small_chip = [_unpack_vec(small_all[2 * b], small_sh_shapes) for b in range(N_CHIPS)]
    full_small = {n: jnp.concatenate([small_chip[b][i] for b in range(N_CHIPS)], axis=2)
                  for i, n in enumerate(SMALL_SHARDED)}

    params = []
    for l in range(depth):
        w_up = full_big[l]["w_ffn_up"]
        fconv = full_small["w_ffn_conv"][l]
        params.append(dict(
            wn_mix=w_norm_mix[l], w_in=_proj_to_kernel_layout(full_big[l]["w_mix_in"]),
            w_sconv=full_small["w_sconv"][l], w_gdn_conv=full_small["w_gdn_conv"][l],
            a_log=gdn_a_log[l].reshape(GDN_HEADS, 1, 1), dt_bias=gdn_dt_bias[l].reshape(GDN_HEADS, 1, 1),
            wgn=w_gdn_norm[l], w_out=_mixout_to_kernel_layout(full_big[l]["w_mix_out"]), wn_ffn=w_norm_ffn[l],
            w_up_g=w_up[:, :D_FF], w_up_v=w_up[:, D_FF:], w_fconv_g=fconv[:, :D_FF], w_fconv_v=fconv[:, D_FF:],
            w_down=full_big[l]["w_ffn_down"]))

    xs = x[0]
    saved = []
    for l in range(depth):
        xs, s = _layer_fwd(xs, params[l], l)
        saved.append(s)
    loss_row, dx, g_norm_final = _final_loss(xs, w_norm_final, loss_target[0], name="final_loss")
    grads = [None] * depth
    for l in reversed(range(depth)):
        dx, grads[l] = _layer_bwd(dx, params[l], saved[l], l)
    loss = lax.psum(loss_row[0, 0], ("x", "y", "c"))

    G = {
        "w_sconv": jnp.stack([grads[l]["w_sconv"] for l in range(depth)]),
        "w_gdn_conv": jnp.stack([grads[l]["w_gdn_conv"] for l in range(depth)]),
        "w_ffn_conv": jnp.stack([jnp.concatenate([grads[l]["w_fconv_g"], grads[l]["w_fconv_v"]], axis=1)
                                 for l in range(depth)]),
        "w_norm_mix": jnp.stack([grads[l]["wn_mix"].reshape(-1) for l in range(depth)]),
        "gdn_a_log": jnp.stack([grads[l]["a_log"].reshape(-1) for l in range(depth)]),
        "gdn_dt_bias": jnp.stack([grads[l]["dt_bias"].reshape(-1) for l in range(depth)]),
        "w_gdn_norm": jnp.stack([grads[l]["wgn"].reshape(-1) for l in range(depth)]),
        "w_norm_ffn": jnp.stack([grads[l]["wn_ffn"].reshape(-1) for l in range(depth)]),
        "w_norm_final": g_norm_final.reshape(-1),
    }

    def by_shard(l):
        g_in = _proj_from_kernel_layout(grads[l]["w_in"])
        g_in = g_in.reshape(D_MODEL, N_CHIPS, -1).transpose(1, 0, 2)
        return [g_in, grads[l]["w_out"].reshape(N_CHIPS, -1, D_MODEL), grads[l]["w_up"],
                grads[l]["w_down"].reshape(N_CHIPS, -1, D_MODEL)]

    g_layers = [by_shard(l) for l in range(depth)]
    from_sibling = _send_other_layer_to_sibling(g_layers[0], g_layers[1], name="rs_sibling")
    chip_sums = [_add_layers(g0, g1, ra, mc, name="rs_add_layers_" + n)
                 for n, g0, g1, ra in zip(BIG, g_layers[0], g_layers[1], from_sibling)]
    from_chips = _scatter_to_chips(chip_sums, name="rs_chips")
    mine = [_add_chips(p, rb, chip, name="rs_add_chips_" + n) for n, p, rb in zip(BIG, chip_sums, from_chips)]
    other = _swap_with_sibling(mine, name="rs_result")
    out_g, out_d, out_m, out_v = {}, {}, {}, {}
    for n, g_mine, g_other in zip(BIG, mine, other):
        out_g[n], out_d[n], out_m[n], out_v[n] = _adamw_layers(W[n], g_mine, g_other, M[n], V[n], mc,
                                                               name="adamw_" + n)

    small_names = SMALL_SHARDED + SMALL_REPLICATED
    small_full_shapes = [G[n].shape for n in small_names]
    n_small = sum(int(np.prod(s)) for s in small_full_shapes)
    red_rows = _round_up(n_small, 8 * LANES) // LANES
    partials = _allgather_devices(_pack_vec([G[n] for n in small_names], red_rows), name="reduce_small")
    summed = _unpack_vec(_sum_slots(partials, name="reduce_small_sum"), small_full_shapes)
    g_small = {}
    for n, a in zip(small_names, summed):
        if n in SMALL_SHARDED:
            width = a.shape[2] // N_CHIPS
            a = lax.dynamic_slice_in_dim(a, chip * width, width, axis=2)
        g_small[n] = a
    own_shapes = [W[n].shape for n in small_names]
    n_own = sum(int(np.prod(s)) for s in own_shapes)
    own_rows = _round_up(n_own, 8 * LANES) // LANES
    packed = [_pack_vec([src[n] for n in small_names], own_rows) for src in (W, g_small, M, V)]
    d_s, nm_s, nv_s = _adamw(*packed, name="adamw_small", tm=own_rows)
    for mat, dst in ((packed[1], out_g), (d_s, out_d), (nm_s, out_m), (nv_s, out_v)):
        for n, a in zip(small_names, _unpack_vec(mat, own_shapes)):
            dst[n] = a

    outs = [loss, dx[None]]
    for dst in (out_g, out_d, out_m, out_v):
        outs += [dst[n] for n in WEIGHTS]
    return tuple(outs)
```

```python
import jax
import jax.numpy as jnp
import numpy as np
from jax import lax
from jax.experimental import pallas as pl
from jax.experimental.pallas import tpu as pltpu

F32 = jnp.float32
BF16 = jnp.bfloat16
MESH = pl.DeviceIdType.MESH

NORM_EPS = 1e-6
GDN_HEADS = 4
GDN_CHUNK = 64
GDN_SCAN_CHUNKS = 4
SB_HEADS = 4
SB_DEAD_LOG = -110.0
ADAM_LR = 0.001
ADAM_B1 = 0.9
ADAM_B2 = 0.999
ADAM_EPS = 1e-08
ADAM_WD = 0.01
ADAM_STEP = 10

VMEM_LIMIT_BYTES = 48 * 1024 * 1024
HALO = 8
CONV_SLAB = 128
LANES = 128
N_CHIPS = 4


def _pc(body, *, name, grid, in_specs, out_specs, out_shape, scratch_shapes=(), dims=None, aliases=None):
    params = dict(vmem_limit_bytes=VMEM_LIMIT_BYTES)
    if dims is not None:
        params["dimension_semantics"] = dims
    return pl.pallas_call(body, name=name, grid=grid, in_specs=in_specs, out_specs=out_specs, out_shape=out_shape,
                          scratch_shapes=list(scratch_shapes), input_output_aliases=aliases or {},
                          compiler_params=pltpu.CompilerParams(**params))


def _pc_prefetch(body, *, name, grid_spec, out_shape, dims):
    return pl.pallas_call(body, name=name, grid_spec=grid_spec, out_shape=out_shape,
                          compiler_params=pltpu.CompilerParams(vmem_limit_bytes=VMEM_LIMIT_BYTES,
                                                               dimension_semantics=dims))


def _pc_comm(body, *, name, in_specs, out_specs, out_shape, scratch_shapes):
    return pl.pallas_call(body, name=name, in_specs=in_specs, out_specs=out_specs, out_shape=out_shape,
                          scratch_shapes=list(scratch_shapes),
                          compiler_params=pltpu.CompilerParams(vmem_limit_bytes=VMEM_LIMIT_BYTES))


_DIMS = {"nn": (((1,), (0,)), ((), ())), "nt": (((1,), (1,)), ((), ())), "tn": (((0,), (0,)), ((), ()))}


def _matmul(a, b, mode, *, name, tm=512, tn=512, tk=512, out_dtype=F32, res=None, slabs=None):
    if mode == "nn":
        (M, K), (K2, N) = a.shape, b.shape
    elif mode == "nt":
        (M, K), (N, K2) = a.shape, b.shape
    else:
        (K, M), (K2, N) = a.shape, b.shape
    assert K == K2, (a.shape, b.shape, mode)
    tm, tn, tk = min(tm, M), min(tn, N), min(tk, K)
    assert M % tm == 0 and N % tn == 0 and K % tk == 0, (M, N, K, tm, tn, tk)
    nk = K // tk
    if mode == "tn":
        a_spec = pl.BlockSpec((tk, tm), lambda i, j, k: (k, i))
    else:
        a_spec = pl.BlockSpec((tm, tk), lambda i, j, k: (i, k))
    if mode == "nt":
        b_spec = pl.BlockSpec((tn, tk), lambda i, j, k: (j, k))
    else:
        b_spec = pl.BlockSpec((tk, tn), lambda i, j, k: (k, j))
    o_spec = pl.BlockSpec((tm, tn), lambda i, j, k: (i, j))
    has_res = res is not None
    dn = _DIMS[mode]

    def body(*refs):
        if has_res:
            a_ref, b_ref, r_ref, o_ref, acc = refs
        else:
            a_ref, b_ref, o_ref, acc = refs
        k = pl.program_id(2)
        p = lax.dot_general(a_ref[...].astype(BF16), b_ref[...].astype(BF16), dn, preferred_element_type=F32)

        def finish(total):
            if has_res:
                total = total + r_ref[...].astype(F32)
            o_ref[...] = total.astype(out_dtype)

        if nk == 1:
            finish(p)
        else:
            @pl.when(k == 0)
            def _():
                acc[...] = p

            @pl.when(k > 0)
            def _():
                acc[...] += p

            @pl.when(k == nk - 1)
            def _():
                finish(acc[...])

    in_specs = [a_spec, b_spec] + ([o_spec] if has_res else [])
    args = (a, b) + ((res,) if has_res else ())
    out_shape = jax.ShapeDtypeStruct((M, N), out_dtype)
    aliases = None
    if slabs is not None:
        n_slabs, first, into = slabs
        assert not has_res and tm == M
        o_spec = pl.BlockSpec((None, tm, tn), lambda i, j, k: (j + first, i, 0))
        out_shape = jax.ShapeDtypeStruct((n_slabs, M, tn), out_dtype)
        if into is not None:
            in_specs.append(pl.BlockSpec(memory_space=pl.ANY))
            args = args + (into,)
            aliases = {2: 0}
            inner = body

            def body(a_ref, b_ref, into_ref, o_ref, acc):
                inner(a_ref, b_ref, o_ref, acc)
    return _pc(body, name=name, grid=(M // tm, N // tn, nk), in_specs=in_specs, out_specs=o_spec,
               out_shape=out_shape, scratch_shapes=[pltpu.VMEM((tm, tn), F32)],
               dims=("parallel", "parallel", "arbitrary"), aliases=aliases)(*args)


def _offsets(parts, own_width_aligned):
    offs, at = [], 0
    for p in parts:
        assert at % (p.shape[1] if own_width_aligned else LANES) == 0, (at, p.shape)
        offs.append(at)
        at += p.shape[1]
    return offs, at


def _matmul_rows_parts(parts, w, mode, *, name, tm=512, res=None):
    M = parts[0].shape[0]
    offs, K = _offsets(parts, True)
    tm = min(tm, M)
    N = w.shape[1] if mode == "nn" else w.shape[0]
    assert (w.shape[0] if mode == "nn" else w.shape[1]) == K
    has_res = res is not None
    n = len(parts)

    def body(*refs):
        o_ref = refs[-1]
        total = None
        for s in range(n):
            p = lax.dot_general(refs[s][...].astype(BF16), refs[n + s][...].astype(BF16), _DIMS[mode],
                                preferred_element_type=F32)
            total = p if total is None else total + p
        if has_res:
            total = total + refs[2 * n][...]
        o_ref[...] = total

    in_specs = [pl.BlockSpec((tm, p.shape[1]), lambda i: (i, 0)) for p in parts]
    for p, off in zip(parts, offs):
        blk = off // p.shape[1]
        if mode == "nn":
            in_specs.append(pl.BlockSpec((p.shape[1], N), lambda i, blk=blk: (blk, 0)))
        else:
            in_specs.append(pl.BlockSpec((N, p.shape[1]), lambda i, blk=blk: (0, blk)))
    o_spec = pl.BlockSpec((tm, N), lambda i: (i, 0))
    args = tuple(parts) + (w,) * n + ((res,) if has_res else ())
    return _pc(body, name=name, grid=(M // tm,), in_specs=in_specs + ([o_spec] if has_res else []), out_specs=o_spec,
               out_shape=jax.ShapeDtypeStruct((M, N), F32), dims=("parallel",))(*args)


def _matmul_tn_parts(a, b, *, name, tk=1024):
    a_parts = list(a) if isinstance(a, (list, tuple)) else [a]
    b_parts = list(b) if isinstance(b, (list, tuple)) else [b]
    assert len(a_parts) == 1 or len(b_parts) == 1
    a_offs, M = _offsets(a_parts, False)
    b_offs, N = _offsets(b_parts, False)
    K = a_parts[0].shape[0]
    tk = min(tk, K)
    na, nb = len(a_parts), len(b_parts)

    def body(*refs):
        o_ref = refs[-1]
        first = pl.program_id(0) == 0
        for s in range(na):
            for t in range(nb):
                p = lax.dot_general(refs[s][...].astype(BF16), refs[na + t][...].astype(BF16), _DIMS["tn"],
                                    preferred_element_type=F32)
                rows = slice(a_offs[s], a_offs[s] + a_parts[s].shape[1])
                cols = slice(b_offs[t], b_offs[t] + b_parts[t].shape[1])

                @pl.when(first)
                def _(p=p, rows=rows, cols=cols):
                    o_ref[rows, cols] = p

                @pl.when(jnp.logical_not(first))
                def _(p=p, rows=rows, cols=cols):
                    o_ref[rows, cols] += p

    in_specs = [pl.BlockSpec((tk, p.shape[1]), lambda k: (k, 0)) for p in a_parts + b_parts]
    return _pc(body, name=name, grid=(K // tk,), in_specs=in_specs, out_specs=pl.BlockSpec((M, N), lambda k: (0, 0)),
               out_shape=jax.ShapeDtypeStruct((M, N), F32), dims=("arbitrary",))(*a_parts, *b_parts)


def _rmsnorm_fwd(x, w, *, name, tm=512):
    L, D = x.shape
    tm = min(tm, L)

    def body(x_ref, w_ref, h_ref):
        xv = x_ref[...]
        r = lax.rsqrt(jnp.mean(xv * xv, axis=-1, keepdims=True) + NORM_EPS)
        h_ref[...] = (xv * r * w_ref[...]).astype(BF16)

    return _pc(body, name=name, grid=(L // tm,),
               in_specs=[pl.BlockSpec((tm, D), lambda i: (i, 0)), pl.BlockSpec((1, D), lambda i: (0, 0))],
               out_specs=pl.BlockSpec((tm, D), lambda i: (i, 0)), out_shape=jax.ShapeDtypeStruct((L, D), BF16),
               dims=("parallel",))(x, w.reshape(1, D))


def _rmsnorm_bwd(dh, x, w, dres, *, name, tm=512):
    L, D = x.shape
    tm = min(tm, L)

    def body(dh_ref, x_ref, w_ref, dres_ref, dx_ref, dw_ref):
        xv = x_ref[...]
        r = lax.rsqrt(jnp.mean(xv * xv, axis=-1, keepdims=True) + NORM_EPS)
        xhat = xv * r
        dhv = dh_ref[...]
        g = dhv * w_ref[...]
        dx_ref[...] = dres_ref[...] + r * (g - xhat * jnp.mean(g * xhat, axis=-1, keepdims=True))
        part = jnp.sum(dhv * xhat, axis=0, keepdims=True)

        @pl.when(pl.program_id(0) == 0)
        def _():
            dw_ref[...] = part

        @pl.when(pl.program_id(0) > 0)
        def _():
            dw_ref[...] += part

    row = pl.BlockSpec((tm, D), lambda i: (i, 0))
    vec = pl.BlockSpec((1, D), lambda i: (0, 0))
    return _pc(body, name=name, grid=(L // tm,), in_specs=[row, row, vec, row], out_specs=[row, vec],
               out_shape=[jax.ShapeDtypeStruct((L, D), F32), jax.ShapeDtypeStruct((1, D), F32)],
               dims=("arbitrary",))(dh, x, w.reshape(1, D), dres)


def _final_loss(x, w, tgt, *, name, tm=512):
    L, D = x.shape
    tm = min(tm, L)

    def body(x_ref, w_ref, t_ref, loss_ref, dx_ref, dw_ref):
        xv = x_ref[...]
        r = lax.rsqrt(jnp.mean(xv * xv, axis=-1, keepdims=True) + NORM_EPS)
        xhat = xv * r
        e = xhat * w_ref[...] - t_ref[...]
        lpart = jnp.broadcast_to(0.5 * jnp.sum(jnp.mean(e * e, axis=-1, keepdims=True), axis=0, keepdims=True),
                                 (1, LANES))
        dy = e * (1.0 / D)
        g = dy * w_ref[...]
        dx_ref[...] = r * (g - xhat * jnp.mean(g * xhat, axis=-1, keepdims=True))
        part = jnp.sum(dy * xhat, axis=0, keepdims=True)

        @pl.when(pl.program_id(0) == 0)
        def _():
            dw_ref[...] = part
            loss_ref[...] = lpart

        @pl.when(pl.program_id(0) > 0)
        def _():
            dw_ref[...] += part
            loss_ref[...] += lpart

    row = pl.BlockSpec((tm, D), lambda i: (i, 0))
    vec = pl.BlockSpec((1, D), lambda i: (0, 0))
    lsp = pl.BlockSpec((1, LANES), lambda i: (0, 0))
    return _pc(body, name=name, grid=(L // tm,), in_specs=[row, vec, row], out_specs=[lsp, row, vec],
               out_shape=[jax.ShapeDtypeStruct((1, LANES), F32), jax.ShapeDtypeStruct((L, D), F32),
                          jax.ShapeDtypeStruct((1, D), F32)],
               dims=("arbitrary",))(x, w.reshape(1, D), tgt)


def _silu(x):
    return x * jax.nn.sigmoid(x)


def _conv_pointwise_fwd(xs, ws, es, pre, post, outs, *, tc, tm, name):
    L = xs[0][0].shape[0]
    tm = min(tm, L)
    ncol = outs[0][0] // tc
    nrow = L // tm
    hb = tm // HALO
    nx, nw, ne, no = len(xs), len(ws), len(es), len(outs)
    K = ws[0][0].shape[0]

    slab = min(CONV_SLAB, tm)
    win = slab + HALO
    assert tm % slab == 0 and K - 1 <= HALO

    def body(*refs):
        i = pl.program_id(1)
        first = (i > 0).astype(F32)
        n_in = 2 * nx + nw + ne
        o_refs = refs[n_in:n_in + no]
        x_pads = refs[n_in + no:]
        for n in range(nx):
            x_pads[n][0:HALO, :] = refs[2 * n + 1][...] * first
            x_pads[n][HALO:, :] = refs[2 * n][...]
        wv = [refs[2 * nx + n][...] for n in range(nw)]
        e_refs = refs[2 * nx + nw:n_in]

        @pl.loop(0, tm // slab)
        def _(t):
            r0 = pl.multiple_of(t * slab, HALO)
            ps = pre(*[x_pads[n][pl.ds(r0, win), :] for n in range(nx)])
            us = []
            for n in range(nw):
                u = None
                for k in range(K):
                    term = wv[n][k:k + 1, :] * (ps[n] if k == K - 1 else pltpu.roll(ps[n], K - 1 - k, 0))
                    u = term if u is None else u + term
                us.append(u[HALO:])
            rows = pl.ds(r0, slab)
            for o_ref, val in zip(o_refs, post(us, [e[rows, :] for e in e_refs])):
                o_ref[rows, :] = val.astype(o_ref.dtype)

    in_specs, args = [], []
    for arr, c0 in xs:
        off = c0 // tc
        in_specs.append(pl.BlockSpec((tm, tc), lambda j, i, off=off: (i, j + off)))
        in_specs.append(pl.BlockSpec((HALO, tc), lambda j, i, off=off: (jnp.maximum(i * hb - 1, 0), j + off)))
        args += [arr, arr]
    for arr, c0 in ws:
        off = c0 // tc
        in_specs.append(pl.BlockSpec((K, tc), lambda j, i, off=off: (0, j + off)))
        args.append(arr)
    for arr, c0 in es:
        off = c0 // tc
        in_specs.append(pl.BlockSpec((tm, tc), lambda j, i, off=off: (i, j + off)))
        args.append(arr)
    out_specs = [pl.BlockSpec((tm, tc), lambda j, i: (i, j)) for _ in range(no)]
    out_shape = [jax.ShapeDtypeStruct((L, c), dt) for c, dt in outs]
    return _pc(body, name=name, grid=(ncol, nrow), in_specs=in_specs, out_specs=out_specs, out_shape=out_shape,
               scratch_shapes=[pltpu.VMEM((tm + HALO, tc), F32)] * nx, dims=("parallel", "parallel"))(*args)


def _conv_pointwise_bwd(xs, ws, es, dys, pre, post, width, *, tc, tm, name, out_dtype=BF16):
    L = xs[0][0].shape[0]
    tm = min(tm, L)
    ncol = width // tc
    nrow = L // tm
    hb = tm // HALO
    nx, nw, ne, ny = len(xs), len(ws), len(es), len(dys)
    K = ws[0][0].shape[0]

    slab = min(CONV_SLAB, tm)
    win = slab + 2 * HALO
    assert tm % slab == 0 and K - 1 <= HALO

    def body(*refs):
        i = pl.program_id(1)
        first = (i > 0).astype(F32)
        more = (i < nrow - 1).astype(F32)
        n_in = 3 * nx + nw + 2 * ne + 2 * ny
        n_out = nx + ne + nw
        dx_refs = refs[n_in:n_in + nx]
        de_refs = refs[n_in + nx:n_in + nx + ne]
        dw_refs = refs[n_in + nx + ne:n_in + n_out]
        pads = refs[n_in + n_out:]
        x_pads, e_pads, dy_pads = pads[:nx], pads[nx:nx + ne], pads[nx + ne:]
        pos = 0
        for n in range(nx):
            x_pads[n][0:HALO, :] = refs[pos + 1][...] * first
            x_pads[n][HALO:HALO + tm, :] = refs[pos][...]
            x_pads[n][HALO + tm:, :] = refs[pos + 2][...]
            pos += 3
        wv = [refs[pos + n][...] for n in range(nw)]
        pos += nw
        for n in range(ne):
            e_pads[n][0:HALO, :] = jnp.zeros((HALO, tc), F32)
            e_pads[n][HALO:HALO + tm, :] = refs[pos][...]
            e_pads[n][HALO + tm:, :] = refs[pos + 1][...]
            pos += 2
        for n in range(ny):
            dy_pads[n][0:HALO, :] = jnp.zeros((HALO, tc), F32)
            dy_pads[n][HALO:HALO + tm, :] = refs[pos][...].astype(F32)
            dy_pads[n][HALO + tm:, :] = refs[pos + 1][...].astype(F32) * more
            pos += 2

        def one_slab(t, dw_acc):
            r0 = pl.multiple_of(t * slab, HALO)
            xw = [x_pads[n][pl.ds(r0, win), :] for n in range(nx)]
            ew = [e_pads[n][pl.ds(r0, win), :] for n in range(ne)]
            dyw = [dy_pads[n][pl.ds(r0, win), :] for n in range(ny)]
            ps, pre_vjp = jax.vjp(lambda *x_: pre(*x_), *xw)
            shifted = [[p if k == K - 1 else pltpu.roll(p, K - 1 - k, 0) for k in range(K)] for p in ps]
            us = []
            for n in range(nw):
                u = None
                for k in range(K):
                    term = wv[n][k:k + 1, :] * shifted[n][k]
                    u = term if u is None else u + term
                us.append(u)
            _, post_vjp = jax.vjp(lambda u_, e_: post(u_, e_), us, ew)
            dus, des = post_vjp(dyw)
            dps, dw_new = [], []
            for n in range(nw):
                dp = None
                for k in range(K):
                    term = wv[n][k:k + 1, :] * (dus[n] if k == K - 1 else pltpu.roll(dus[n], win - (K - 1 - k), 0))
                    dp = term if dp is None else dp + term
                dps.append(dp)
                inner = dus[n][HALO:HALO + slab]
                dw_new.append([dw_acc[n][k] + jnp.sum(inner * shifted[n][k][HALO:HALO + slab], axis=0, keepdims=True)
                               for k in range(K)])
            dxs = pre_vjp(dps)
            rows = pl.ds(r0, slab)
            for r, v in zip(dx_refs, dxs):
                r[rows, :] = v[HALO:HALO + slab].astype(out_dtype)
            for r, v in zip(de_refs, des):
                r[rows, :] = v[HALO:HALO + slab].astype(out_dtype)
            return dw_new

        zero = [[jnp.zeros((1, tc), F32) for _ in range(K)] for _ in range(nw)]
        dw_tile = lax.fori_loop(0, tm // slab, one_slab, zero)
        for n in range(nw):
            for k in range(K):
                @pl.when(i == 0)
                def _(n=n, k=k):
                    dw_refs[n][k:k + 1, :] = dw_tile[n][k]

                @pl.when(i > 0)
                def _(n=n, k=k):
                    dw_refs[n][k:k + 1, :] += dw_tile[n][k]

    in_specs, args = [], []

    def add_rows(arr, c0, prev, nxt):
        off = c0 // tc
        in_specs.append(pl.BlockSpec((tm, tc), lambda j, i, off=off: (i, j + off)))
        args.append(arr)
        if prev:
            in_specs.append(pl.BlockSpec((HALO, tc), lambda j, i, off=off: (jnp.maximum(i * hb - 1, 0), j + off)))
            args.append(arr)
        if nxt:
            last = L // HALO - 1
            in_specs.append(pl.BlockSpec((HALO, tc), lambda j, i, off=off: (jnp.minimum((i + 1) * hb, last), j + off)))
            args.append(arr)

    for arr, c0 in xs:
        add_rows(arr, c0, True, True)
    for arr, c0 in ws:
        off = c0 // tc
        in_specs.append(pl.BlockSpec((K, tc), lambda j, i, off=off: (0, j + off)))
        args.append(arr)
    for arr, c0 in es:
        add_rows(arr, c0, False, True)
    for arr, c0 in dys:
        add_rows(arr, c0, False, True)
    tile = pl.BlockSpec((tm, tc), lambda j, i: (i, j))
    wtile = pl.BlockSpec((K, tc), lambda j, i: (0, j))
    out_specs = [tile] * (nx + ne) + [wtile] * nw
    out_shape = [jax.ShapeDtypeStruct((L, width), out_dtype)] * (nx + ne) + \
        [jax.ShapeDtypeStruct((K, width), F32)] * nw
    res = _pc(body, name=name, grid=(ncol, nrow), in_specs=in_specs, out_specs=out_specs, out_shape=out_shape,
              scratch_shapes=[pltpu.VMEM((tm + 2 * HALO, tc), F32)] * (nx + ne + ny),
              dims=("parallel", "arbitrary"))(*args)
    return res[:nx], res[nx:nx + ne], res[nx + ne:]


def _pre_identity(*x):
    return list(x)


def _pre_product(c, h):
    return [c * h]


def _post_silu(us, es):
    return [_silu(us[0])]


def _post_gate_mul(us, es):
    return [es[0] * us[0]]


def _post_swiglu(us, es):
    return [_silu(us[0]) * us[1]]


def _make_dot(passes):
    def raw(a, b, dn):
        a_hi = a.astype(BF16)
        b_hi = b.astype(BF16)
        out = lax.dot_general(a_hi, b_hi, dn, preferred_element_type=F32)
        if passes == 3:
            a_lo = (a - a_hi.astype(F32)).astype(BF16)
            b_lo = (b - b_hi.astype(F32)).astype(BF16)
            out = out + lax.dot_general(a_hi, b_lo, dn, preferred_element_type=F32)
            out = out + lax.dot_general(a_lo, b_hi, dn, preferred_element_type=F32)
        return out

    @jax.custom_vjp
    def nn(a, b):
        return raw(a, b, _DIMS["nn"])

    @jax.custom_vjp
    def nt(a, b):
        return raw(a, b, _DIMS["nt"])

    @jax.custom_vjp
    def tn(a, b):
        return raw(a, b, _DIMS["tn"])

    nn.defvjp(lambda a, b: (nn(a, b), (a, b)), lambda r, g: (nt(g, r[1]), tn(r[0], g)))
    nt.defvjp(lambda a, b: (nt(a, b), (a, b)), lambda r, g: (nn(g, r[1]), tn(g, r[0])))
    tn.defvjp(lambda a, b: (tn(a, b), (a, b)), lambda r, g: (nt(r[1], g), nn(r[0], g)))
    return nn, nt, tn


_NN1, _NT1, _TN1 = _make_dot(1)
_NN3, _NT3, _TN3 = _make_dot(3)


def _l2norm(x):
    return x * lax.rsqrt(jnp.sum(x * x, axis=-1, keepdims=True) + NORM_EPS)


def _split_bf16(x):
    hi = x.astype(BF16)
    return hi, (x - hi.astype(F32)).astype(BF16)


def _products_with(lhs_list, rhs):
    n, rows = len(lhs_list), lhs_list[0].shape[0]
    r_hi, r_lo = _split_bf16(rhs)
    halves = [_split_bf16(l) for l in lhs_list]
    his = [h for h, _ in halves]
    o_hi = jnp.dot(jnp.concatenate(his + [lo for _, lo in halves], axis=0), r_hi, preferred_element_type=F32)
    o_lo = jnp.dot(jnp.concatenate(his, axis=0) if n > 1 else his[0], r_lo, preferred_element_type=F32)
    return [o_hi[i * rows:(i + 1) * rows] + o_hi[(n + i) * rows:(n + i + 1) * rows] + o_lo[i * rows:(i + 1) * rows]
            for i in range(n)]


def _unit_lower_inverse_raw(a_list):
    C = a_list[0].shape[0]
    ii = lax.broadcasted_iota(jnp.int32, (C, C), 0)
    jj = lax.broadcasted_iota(jnp.int32, (C, C), 1)
    eye = jnp.where(ii == jj, 1.0, 0.0)
    qs = [-a for a in a_list]
    ts = [eye + q for q in qs]
    qs = [_products_with([q], q)[0] for q in qs]
    n = 4
    while n <= C:
        last = n == C
        prods = [_products_with([t] if last else [t, q], q) for t, q in zip(ts, qs)]
        ts = [t + pr[0] for t, pr in zip(ts, prods)]
        if not last:
            qs = [pr[1] for pr in prods]
        n *= 2
    return ts


@jax.custom_vjp
def _unit_lower_inverse(a_list):
    return _unit_lower_inverse_raw(a_list)


def _unit_lower_inverse_fwd(a_list):
    ts = _unit_lower_inverse_raw(a_list)
    return ts, ts


def _unit_lower_inverse_bwd(ts, gs):
    xs = [_TN3(t, g) for t, g in zip(ts, gs)]
    return ([-_NT3(x, t) for x, t in zip(xs, ts)],)


_unit_lower_inverse.defvjp(_unit_lower_inverse_fwd, _unit_lower_inverse_bwd)


def _gdn_prep(units):
    C, Dh = units[0][0].shape
    ii = lax.broadcasted_iota(jnp.int32, (C, C), 0)
    jj = lax.broadcasted_iota(jnp.int32, (C, C), 1)
    lane = lax.broadcasted_iota(jnp.int32, (1, C), 1)
    causal = ii >= jj
    strict = ii > jj
    qs = [_l2norm(un[0]) * (Dh ** -0.5) for un in units]
    ks = [_l2norm(un[1]) for un in units]
    betas = [jax.nn.sigmoid(un[4]) for un in units]
    gs = [-jnp.exp(un[5]) * jax.nn.softplus(un[3] + un[6]) for un in units]
    gc_rows = [jnp.sum(jnp.where(ii <= jj, g, 0.0), axis=0, keepdims=True) for g in gs]
    gc_cols = [jnp.sum(jnp.where(ii == jj, r, 0.0), axis=1, keepdims=True) for r in gc_rows]
    decays = [jnp.where(causal, jnp.exp(jnp.where(causal, c - r, 0.0)), 0.0) for c, r in zip(gc_cols, gc_rows)]
    kbs = [k * b for k, b in zip(ks, betas)]
    kks = [_NT1(kb, k) for kb, k in zip(kbs, ks)]
    qks = [_NT1(q, k) for q, k in zip(qs, ks)]
    ts = _unit_lower_inverse([jnp.where(strict, kk * d, 0.0) for kk, d in zip(kks, decays)])
    eg_cols = [jnp.exp(c) for c in gc_cols]
    uws = [_NN3(t, jnp.concatenate([un[2] * b, kb * e], axis=1))
           for t, un, b, kb, e in zip(ts, units, betas, kbs, eg_cols)]
    out = []
    for q, k, qk, d, uw, e, r, c in zip(qs, ks, qks, decays, uws, eg_cols, gc_rows, gc_cols):
        g_last = jnp.sum(jnp.where(lane == C - 1, r, 0.0), axis=1, keepdims=True)
        out.append((q * e, k * jnp.exp(g_last - c), uw[:, :Dh], uw[:, Dh:], jnp.where(causal, qk * d, 0.0),
                    jnp.broadcast_to(jnp.exp(g_last), (1, Dh))))
    return out


def _gdn_step(units):
    v_news = [un[3] - _NN1(un[4], un[0]) for un in units]
    o_state = [_NN1(un[1], un[0]) for un in units]
    o_intra = [_NN1(un[5], vn) for un, vn in zip(units, v_news)]
    s_adds = [_TN1(un[2], vn) for un, vn in zip(units, v_news)]
    out = []
    for un, a, b, s_add in zip(units, o_state, o_intra, s_adds):
        o = a + b
        y = o * lax.rsqrt(jnp.mean(o * o, axis=-1, keepdims=True) + NORM_EPS) * un[8] * _silu(un[7])
        out.append((y, un[0] * un[6] + s_add))
    return out


def _gdn_prep_fwd(qkv, gab, gab_col, a_log, dt_bias, *, name, chunks=4):
    L = qkv.shape[0]
    H, C = GDN_HEADS, GDN_CHUNK
    W = qkv.shape[1] // 3
    Dh = W // H
    N = L // C
    chunks = min(chunks, N)
    R = chunks * C
    gab_off = gab_col // LANES

    def body(q_ref, k_ref, v_ref, gab_ref, al_ref, dt_ref, qe_ref, ke_ref, u_ref, w_ref, at_ref, eg_ref):
        where = [(cc, h) for cc in range(chunks) for h in range(H)]
        units = []
        for cc, h in where:
            rows, sl = slice(cc * C, (cc + 1) * C), slice(h * Dh, (h + 1) * Dh)
            units.append((q_ref[rows, sl], k_ref[rows, sl], v_ref[rows, sl], gab_ref[rows, h:h + 1],
                          gab_ref[rows, H + h:H + h + 1], al_ref[h], dt_ref[h]))
        for (cc, h), (qe, ke, u, w, attn, eg) in zip(where, _gdn_prep(units)):
            rows, sl = slice(cc * C, (cc + 1) * C), slice(h * Dh, (h + 1) * Dh)
            qe_ref[rows, sl] = qe
            ke_ref[rows, sl] = ke
            u_ref[rows, sl] = u
            w_ref[rows, sl] = w
            at_ref[h, rows, :] = attn
            eg_ref[cc, h:h + 1, :] = eg

    col = lambda c: pl.BlockSpec((R, W), lambda n, c=c: (n, c))
    tok = pl.BlockSpec((R, LANES), lambda n: (n, gab_off))
    par = pl.BlockSpec((H, 1, 1), lambda n: (0, 0, 0))
    wide = pl.BlockSpec((R, W), lambda n: (n, 0))
    return _pc(body, name=name, grid=(N // chunks,), in_specs=[col(0), col(1), col(2), tok, par, par],
               out_specs=[wide, wide, wide, wide, pl.BlockSpec((H, R, C), lambda n: (0, n, 0)),
                          pl.BlockSpec((chunks, H, Dh), lambda n: (n, 0, 0))],
               out_shape=[jax.ShapeDtypeStruct((L, W), F32)] * 4 + [jax.ShapeDtypeStruct((H, L, C), F32),
                                                                   jax.ShapeDtypeStruct((N, H, Dh), F32)],
               dims=("parallel",))(qkv, qkv, qkv, gab, a_log, dt_bias)


def _gdn_prep_bwd(qkv, gab, gab_col, a_log, dt_bias, dqe, dke, du, dw, dattn, deg, gab_width, *, name, chunks=4):
    L = qkv.shape[0]
    H, C = GDN_HEADS, GDN_CHUNK
    W = qkv.shape[1] // 3
    Dh = W // H
    N = L // C
    chunks = min(chunks, N)
    R = chunks * C
    gab_off = gab_col // LANES

    def body(q_ref, k_ref, v_ref, gab_ref, al_ref, dt_ref, dqe_ref, dke_ref, du_ref, dw_ref, dat_ref, deg_ref,
             dqkv_ref, dgab_ref, dal_ref, ddt_ref):
        first = pl.program_id(0) == 0
        lane = lax.broadcasted_iota(jnp.int32, (C, gab_width), 1)
        dal_sum, ddt_sum = [None] * H, [None] * H
        where = [(cc, h) for cc in range(chunks) for h in range(H)]
        units, cots = [], []
        for cc, h in where:
            rows, sl = slice(cc * C, (cc + 1) * C), slice(h * Dh, (h + 1) * Dh)
            units.append((q_ref[rows, sl], k_ref[rows, sl], v_ref[rows, sl], gab_ref[rows, h:h + 1],
                          gab_ref[rows, H + h:H + h + 1], al_ref[h], dt_ref[h]))
            cots.append((dqe_ref[rows, sl], dke_ref[rows, sl], du_ref[rows, sl], dw_ref[rows, sl],
                         dat_ref[h, rows, :], deg_ref[cc, h:h + 1, :]))
        _, vjp = jax.vjp(_gdn_prep, units)
        (d_units,) = vjp(cots)
        dgabs = [jnp.zeros((C, gab_width), F32) for _ in range(chunks)]
        for (cc, h), (dq, dk, dv, dga, dgb, dal, ddt) in zip(where, d_units):
            rows = slice(cc * C, (cc + 1) * C)
            dqkv_ref[rows, h * Dh:(h + 1) * Dh] = dq
            dqkv_ref[rows, W + h * Dh:W + (h + 1) * Dh] = dk
            dqkv_ref[rows, 2 * W + h * Dh:2 * W + (h + 1) * Dh] = dv
            dgabs[cc] = dgabs[cc] + jnp.where(lane == h, dga, 0.0) + jnp.where(lane == H + h, dgb, 0.0)
            dal_sum[h] = dal if dal_sum[h] is None else dal_sum[h] + dal
            ddt_sum[h] = ddt if ddt_sum[h] is None else ddt_sum[h] + ddt
        for cc in range(chunks):
            dgab_ref[cc * C:(cc + 1) * C, :] = dgabs[cc].astype(BF16)

        @pl.when(first)
        def _():
            for h in range(H):
                dal_ref[h] = dal_sum[h]
                ddt_ref[h] = ddt_sum[h]

        @pl.when(jnp.logical_not(first))
        def _():
            for h in range(H):
                dal_ref[h] += dal_sum[h]
                ddt_ref[h] += ddt_sum[h]

    col = lambda c: pl.BlockSpec((R, W), lambda n, c=c: (n, c))
    tok = pl.BlockSpec((R, LANES), lambda n: (n, gab_off))
    par = pl.BlockSpec((H, 1, 1), lambda n: (0, 0, 0))
    wide = pl.BlockSpec((R, W), lambda n: (n, 0))
    att = pl.BlockSpec((H, R, C), lambda n: (0, n, 0))
    egs = pl.BlockSpec((chunks, H, Dh), lambda n: (n, 0, 0))
    return _pc(body, name=name, grid=(N // chunks,),
               in_specs=[col(0), col(1), col(2), tok, par, par, wide, wide, wide, wide, att, egs],
               out_specs=[pl.BlockSpec((R, 3 * W), lambda n: (n, 0)), pl.BlockSpec((R, gab_width), lambda n: (n, 0)),
                          par, par],
               out_shape=[jax.ShapeDtypeStruct((L, 3 * W), F32), jax.ShapeDtypeStruct((L, gab_width), BF16)]
               + [jax.ShapeDtypeStruct((H, 1, 1), F32)] * 2,
               dims=("arbitrary",))(qkv, qkv, qkv, gab, a_log, dt_bias, dqe, dke, du, dw, dattn, deg)


def _gdn_scan_fwd(qe, ke, u, w, attn, eg, gz, gz_col, wgn, *, name):
    L, W = qe.shape
    H, C = GDN_HEADS, GDN_CHUNK
    Dh = W // H
    N = L // C
    gz_off = gz_col // W
    cps = min(GDN_SCAN_CHUNKS, N)
    R = cps * C

    def body(qe_ref, ke_ref, u_ref, w_ref, at_ref, eg_ref, gz_ref, wgn_ref, y_ref, st_ref, s_scr):
        @pl.when(pl.program_id(0) == 0)
        def _():
            s_scr[...] = jnp.zeros_like(s_scr)

        S = [s_scr[h] for h in range(H)]
        for cc in range(cps):
            rows = slice(cc * C, (cc + 1) * C)
            units = []
            for h in range(H):
                sl = slice(h * Dh, (h + 1) * Dh)
                st_ref[cc, h] = S[h]
                units.append((S[h], qe_ref[rows, sl], ke_ref[rows, sl], u_ref[rows, sl], w_ref[rows, sl],
                              at_ref[h, rows, :], eg_ref[cc, h:h + 1, :], gz_ref[rows, sl], wgn_ref[...]))
            for h, (y, S_new) in enumerate(_gdn_step(units)):
                y_ref[rows, h * Dh:(h + 1) * Dh] = y.astype(BF16)
                S[h] = S_new
        for h in range(H):
            s_scr[h] = S[h]

    wide = pl.BlockSpec((R, W), lambda n: (n, 0))
    return _pc(body, name=name, grid=(N // cps,),
               in_specs=[wide, wide, wide, wide, pl.BlockSpec((H, R, C), lambda n: (0, n, 0)),
                         pl.BlockSpec((cps, H, Dh), lambda n: (n, 0, 0)),
                         pl.BlockSpec((R, W), lambda n: (n, gz_off)), pl.BlockSpec((1, Dh), lambda n: (0, 0))],
               out_specs=[wide, pl.BlockSpec((cps, H, Dh, Dh), lambda n: (n, 0, 0, 0))],
               out_shape=[jax.ShapeDtypeStruct((L, W), BF16), jax.ShapeDtypeStruct((N, H, Dh, Dh), F32)],
               scratch_shapes=[pltpu.VMEM((H, Dh, Dh), F32)],
               dims=("arbitrary",))(qe, ke, u, w, attn, eg, gz, wgn.reshape(1, Dh))


def _gdn_scan_bwd(qe, ke, u, w, attn, eg, gz, gz_col, wgn, states, dy, dy_col, *, name):
    L, W = qe.shape
    H, C = GDN_HEADS, GDN_CHUNK
    Dh = W // H
    N = L // C
    gz_off = gz_col // W
    dy_off = dy_col // W
    cps = min(GDN_SCAN_CHUNKS, N)
    R = cps * C
    steps = N // cps

    def body(qe_ref, ke_ref, u_ref, w_ref, at_ref, eg_ref, gz_ref, wgn_ref, st_ref, dy_ref,
             dqe_ref, dke_ref, du_ref, dw_ref, dat_ref, deg_ref, dgz_ref, dwgn_ref, ds_scr):
        first = pl.program_id(0) == 0

        @pl.when(first)
        def _():
            ds_scr[...] = jnp.zeros_like(ds_scr)

        dwgn = None
        dS = [ds_scr[h] for h in range(H)]
        for cc in reversed(range(cps)):
            rows = slice(cc * C, (cc + 1) * C)
            units, cots = [], []
            for h in range(H):
                sl = slice(h * Dh, (h + 1) * Dh)
                units.append((st_ref[cc, h], qe_ref[rows, sl], ke_ref[rows, sl], u_ref[rows, sl], w_ref[rows, sl],
                              at_ref[h, rows, :], eg_ref[cc, h:h + 1, :], gz_ref[rows, sl], wgn_ref[...]))
                cots.append((dy_ref[rows, sl].astype(F32), dS[h]))
            _, vjp = jax.vjp(_gdn_step, units)
            (d_units,) = vjp(cots)
            for h, (dS_h, dqe, dke, du, dw, dat, deg, dgz, dwg) in enumerate(d_units):
                sl = slice(h * Dh, (h + 1) * Dh)
                dS[h] = dS_h
                dqe_ref[rows, sl] = dqe
                dke_ref[rows, sl] = dke
                du_ref[rows, sl] = du
                dw_ref[rows, sl] = dw
                dat_ref[h, rows, :] = dat
                deg_ref[cc, h:h + 1, :] = deg
                dgz_ref[rows, sl] = dgz.astype(BF16)
                dwgn = dwg if dwgn is None else dwgn + dwg
        for h in range(H):
            ds_scr[h] = dS[h]

        @pl.when(first)
        def _():
            dwgn_ref[...] = dwgn

        @pl.when(jnp.logical_not(first))
        def _():
            dwgn_ref[...] += dwgn

    rev = lambda n: steps - 1 - n
    wide = pl.BlockSpec((R, W), lambda n: (rev(n), 0))
    att = pl.BlockSpec((H, R, C), lambda n: (0, rev(n), 0))
    egs = pl.BlockSpec((cps, H, Dh), lambda n: (rev(n), 0, 0))
    vec = pl.BlockSpec((1, Dh), lambda n: (0, 0))
    return _pc(body, name=name, grid=(steps,),
               in_specs=[wide, wide, wide, wide, att, egs, pl.BlockSpec((R, W), lambda n: (rev(n), gz_off)), vec,
                         pl.BlockSpec((cps, H, Dh, Dh), lambda n: (rev(n), 0, 0, 0)),
                         pl.BlockSpec((R, W), lambda n: (rev(n), dy_off))],
               out_specs=[wide, wide, wide, wide, att, egs, wide, vec],
               out_shape=[jax.ShapeDtypeStruct((L, W), F32)] * 4 + [jax.ShapeDtypeStruct((H, L, C), F32),
                                                                   jax.ShapeDtypeStruct((N, H, Dh), F32),
                                                                   jax.ShapeDtypeStruct((L, W), BF16),
                                                                   jax.ShapeDtypeStruct((1, Dh), F32)],
               scratch_shapes=[pltpu.VMEM((H, Dh, Dh), F32)],
               dims=("arbitrary",))(qe, ke, u, w, attn, eg, gz, wgn.reshape(1, Dh), states, dy)


def _sb_scores(z, mask):
    sp = jnp.maximum(z, 0.0) + jnp.log(1.0 + jnp.exp(-jnp.abs(z)))
    lom = -sp if mask is None else jnp.where(mask, -sp, 0.0)
    return lom, z - sp


def _sb_alive(c_a, c_b):
    return jnp.maximum(jnp.max(c_a), jnp.max(c_b)) >= SB_DEAD_LOG


def _sb_masks(tq, width, dh):
    rr = lax.broadcasted_iota(jnp.int32, (tq, tq), 0)
    cc = lax.broadcasted_iota(jnp.int32, (tq, tq), 1)
    first_head = lax.broadcasted_iota(jnp.int32, (tq, width), 1) < dh
    return cc < rr, jnp.where(rr > cc, 1.0, 0.0).astype(BF16), first_head


def _sb_fwd(qkv, *, name, tq=256):
    L = qkv.shape[0]
    H = SB_HEADS
    width = 2 * (qkv.shape[1] // 3 // H)
    dh = width // 2
    npair = H // 2
    tq = min(tq, L)
    nq = L // tq

    def body(q_ref, k_ref, v_ref, o_ref):
        i = pl.program_id(1)
        diag, tri, first_head = _sb_masks(tq, width, dh)
        qp = q_ref[...]
        zero = jnp.zeros_like(qp)
        qs = (jnp.where(first_head, qp, zero), jnp.where(first_head, zero, qp))

        def blocks(js, carry, masks):
            units = [(b, hd) for b in range(len(js)) for hd in range(2)]
            starts = [pl.multiple_of(j * tq, tq) for j in js]
            ks = [k_ref[pl.ds(st, tq), :] for st in starts]
            vs = [v_ref[pl.ds(st, tq), :] for st in starts]
            zs = {(b, hd): lax.dot_general(qs[hd], ks[b], _DIMS["nt"], preferred_element_type=F32)
                  for b, hd in units}
            scores = {(b, hd): _sb_scores(zs[(b, hd)], masks[b]) for b, hd in units}
            later = {un: jnp.dot(scores[un][0].astype(BF16), tri, preferred_element_type=F32) for un in units}
            cs = [carry[hd][0] for hd in range(2)]
            accs = [carry[hd][1] for hd in range(2)]
            for b, hd in units:
                lom, lb = scores[(b, hd)]
                a = jnp.exp(lb + (cs[hd] + later[(b, hd)]))
                if masks[b] is not None:
                    a = jnp.where(masks[b], a, 0.0)
                accs[hd] = accs[hd] + jnp.dot(a.astype(BF16), vs[b], preferred_element_type=F32)
                cs[hd] = cs[hd] + jnp.sum(lom, axis=1, keepdims=True)
            return tuple((cs[hd], accs[hd]) for hd in range(2))

        init = tuple((jnp.zeros((tq, 1), F32), jnp.zeros((tq, width), F32)) for _ in range(2))
        carry = lax.cond(i > 0, lambda: blocks([i, i - 1], init, [diag, None]), lambda: blocks([i], init, [diag]))
        j_end, carry = lax.while_loop(lambda st: jnp.logical_and(st[0] >= 0, _sb_alive(st[1][0][0], st[1][1][0])),
                                      lambda st: (st[0] - 1, blocks([st[0]], st[1], [None])), (i - 2, carry))
        o_ref[...] = jnp.where(first_head, carry[0][1], carry[1][1]).astype(BF16)

    return _pc(body, name=name, grid=(npair, nq),
               in_specs=[pl.BlockSpec((tq, width), lambda p, i: (i, p)),
                         pl.BlockSpec((L, width), lambda p, i: (0, npair + p)),
                         pl.BlockSpec((L, width), lambda p, i: (0, 2 * npair + p))],
               out_specs=pl.BlockSpec((tq, width), lambda p, i: (i, p)),
               out_shape=jax.ShapeDtypeStruct((L, npair * width), BF16),
               dims=("parallel", "parallel"))(qkv, qkv, qkv)


def _sb_bwd(qkv, do, do_col, scale, *, name, tq=256):
    L = qkv.shape[0]
    H = SB_HEADS
    width = 2 * (qkv.shape[1] // 3 // H)
    dh = width // 2
    npair = H // 2
    tq = min(tq, L)
    nq = L // tq
    do_off = do_col // width

    def body(q_ref, k_ref, v_ref, do_ref, dq_ref, dk_ref, dv_ref):
        i = pl.program_id(1)

        @pl.when(i == 0)
        def _():
            dk_ref[...] = jnp.zeros_like(dk_ref)
            dv_ref[...] = jnp.zeros_like(dv_ref)

        diag, tri_later, first_head = _sb_masks(tq, width, dh)
        rr = lax.broadcasted_iota(jnp.int32, (tq, tq), 0)
        cc = lax.broadcasted_iota(jnp.int32, (tq, tq), 1)
        tri_before = jnp.where(rr < cc, 1.0, 0.0).astype(BF16)
        qp = q_ref[...]
        dop = do_ref[...].astype(BF16)
        zero = jnp.zeros_like(qp)
        qs = (jnp.where(first_head, qp, zero), jnp.where(first_head, zero, qp))
        dos = (jnp.where(first_head, dop, zero), jnp.where(first_head, zero, dop))
        ctots = []

        def blocks(js, carry, masks):
            nb = len(js)
            units = [(b, hd) for b in range(nb) for hd in range(2)]
            starts = [pl.multiple_of(j * tq, tq) for j in js]
            ks = [k_ref[pl.ds(st, tq), :] for st in starts]
            vs = [v_ref[pl.ds(st, tq), :] for st in starts]
            zs = {(b, hd): lax.dot_general(qs[hd], ks[b], _DIMS["nt"], preferred_element_type=F32)
                  for b, hd in units}
            das = {(b, hd): lax.dot_general(dos[hd], vs[b], _DIMS["nt"], preferred_element_type=F32)
                   for b, hd in units}
            scores = {(b, hd): _sb_scores(zs[(b, hd)], masks[b]) for b, hd in units}
            later = {un: jnp.dot(scores[un][0].astype(BF16), tri_later, preferred_element_type=F32) for un in units}
            pcs = [carry[hd][0] for hd in range(2)]
            avals = {}
            for b, hd in units:
                pcs[hd] = pcs[hd] + jnp.sum(scores[(b, hd)][0], axis=1, keepdims=True)
                a = jnp.exp(scores[(b, hd)][1] + ((ctots[hd] - pcs[hd]) + later[(b, hd)]))
                avals[(b, hd)] = a if masks[b] is None else jnp.where(masks[b], a, 0.0)
            gs = {un: das[un] * avals[un] for un in units}
            before = {un: jnp.dot(gs[un].astype(BF16), tri_before, preferred_element_type=F32) for un in units}
            pgs = [carry[hd][1] for hd in range(2)]
            dzs = {}
            for b, hd in units:
                sig = jnp.exp(scores[(b, hd)][1])
                dz = gs[(b, hd)] * (1.0 - sig) - (pgs[hd] + before[(b, hd)]) * sig
                dzs[(b, hd)] = (dz if masks[b] is None else jnp.where(masks[b], dz, 0.0)).astype(BF16)
                pgs[hd] = pgs[hd] + jnp.sum(gs[(b, hd)], axis=1, keepdims=True)
            dqs = [carry[hd][2] for hd in range(2)]
            for b, hd in units:
                dqs[hd] = dqs[hd] + jnp.dot(dzs[(b, hd)], ks[b], preferred_element_type=F32)
            for b in range(nb):
                dk_ref[pl.ds(starts[b], tq), :] += sum(
                    lax.dot_general(dzs[(b, hd)], qs[hd], _DIMS["tn"], preferred_element_type=F32) for hd in range(2))
                dv_ref[pl.ds(starts[b], tq), :] += sum(
                    lax.dot_general(avals[(b, hd)].astype(BF16), dos[hd], _DIMS["tn"], preferred_element_type=F32)
                    for hd in range(2))
            return tuple((pcs[hd], pgs[hd], dqs[hd]) for hd in range(2))

        def row_sums(j, mask):
            kj = k_ref[pl.ds(pl.multiple_of(j * tq, tq), tq), :]
            return tuple(jnp.sum(_sb_scores(lax.dot_general(qs[hd], kj, _DIMS["nt"], preferred_element_type=F32),
                                            mask)[0], axis=1, keepdims=True) for hd in range(2))

        near = row_sums(i, diag)
        near = lax.cond(i > 0, lambda: tuple(a + b for a, b in zip(near, row_sums(i - 1, None))), lambda: near)
        j_dead, live_sums = lax.while_loop(
            lambda st: jnp.logical_and(st[0] >= 0, _sb_alive(st[1][0], st[1][1])),
            lambda st: (st[0] - 1, tuple(a + b for a, b in zip(st[1], row_sums(st[0], None)))),
            (i - 2, near))
        ctots.extend(live_sums)
        col = jnp.zeros((tq, 1), F32)
        init = tuple((col, col, jnp.zeros((tq, width), F32)) for _ in range(2))
        carry = lax.fori_loop(jnp.maximum(j_dead, -1) + 1, i - 1, lambda j, cr: blocks([j], cr, [None]), init)
        carry = lax.cond(i > 0, lambda cr: blocks([i - 1, i], cr, [None, diag]), lambda cr: blocks([i], cr, [diag]),
                         carry)
        dq_ref[...] = (jnp.where(first_head, carry[0][2], carry[1][2]) * scale).astype(BF16)

    tile = pl.BlockSpec((tq, width), lambda p, i: (i, p))
    full = pl.BlockSpec((L, width), lambda p, i: (0, p))
    sds = jax.ShapeDtypeStruct((L, npair * width), F32)
    return _pc(body, name=name, grid=(npair, nq),
               in_specs=[tile, pl.BlockSpec((L, width), lambda p, i: (0, npair + p)),
                         pl.BlockSpec((L, width), lambda p, i: (0, 2 * npair + p)),
                         pl.BlockSpec((tq, width), lambda p, i: (i, do_off + p))],
               out_specs=[tile, full, full],
               out_shape=[jax.ShapeDtypeStruct((L, npair * width), BF16), sds, sds],
               dims=("parallel", "arbitrary"))(qkv, qkv, qkv, do)


def _adamw(w, g, m, v, *, name, tm=256):
    R, C = w.shape
    tm = min(tm, R)
    assert R % tm == 0, (R, tm)
    c1 = 1.0 - ADAM_B1 ** ADAM_STEP
    c2 = 1.0 - ADAM_B2 ** ADAM_STEP

    def body(w_ref, g_ref, m_ref, v_ref, d_ref, nm_ref, nv_ref):
        gv = g_ref[...]
        nm = ADAM_B1 * m_ref[...] + (1.0 - ADAM_B1) * gv
        nv = ADAM_B2 * v_ref[...] + (1.0 - ADAM_B2) * (gv * gv)
        d_ref[...] = -ADAM_LR * ((nm / c1) / (jnp.sqrt(nv / c2) + ADAM_EPS) + ADAM_WD * w_ref[...])
        nm_ref[...] = nm
        nv_ref[...] = nv

    blk = pl.BlockSpec((tm, C), lambda i: (i, 0))
    sds = jax.ShapeDtypeStruct((R, C), F32)
    return _pc(body, name=name, grid=(R // tm,), in_specs=[blk] * 4, out_specs=[blk] * 3, out_shape=[sds] * 3,
               dims=("parallel",))(w, g, m, v)


ELEMENTWISE_BLOCK_BYTES = 1 << 20


def _row_tile(rows, cols):
    for t in (512, 384, 352, 256, 176, 128, 88, 64, 32, 16, 8):
        if rows % t == 0 and t * cols * 4 <= ELEMENTWISE_BLOCK_BYTES:
            return t
    raise ValueError((rows, cols))


def _adamw_layers(w, g_mine, g_other, m, v, c, *, name):
    _, R, C = w.shape
    tm = _row_tile(R, C)
    c1 = 1.0 - ADAM_B1 ** ADAM_STEP
    c2 = 1.0 - ADAM_B2 ** ADAM_STEP

    def body(c_ref, w_ref, gm_ref, go_ref, m_ref, v_ref, g_ref, d_ref, nm_ref, nv_ref):
        gv = jnp.where(pl.program_id(0) == c_ref[0], gm_ref[...], go_ref[...])
        nm = ADAM_B1 * m_ref[...] + (1.0 - ADAM_B1) * gv
        nv = ADAM_B2 * v_ref[...] + (1.0 - ADAM_B2) * (gv * gv)
        g_ref[...] = gv
        d_ref[...] = -ADAM_LR * ((nm / c1) / (jnp.sqrt(nv / c2) + ADAM_EPS) + ADAM_WD * w_ref[...])
        nm_ref[...] = nm
        nv_ref[...] = nv

    slab = pl.BlockSpec((None, tm, C), lambda l, i, c_ref: (l, i, 0))
    mine = pl.BlockSpec((tm, C), lambda l, i, c_ref: (jnp.where(l == c_ref[0], i, 0), 0))
    other = pl.BlockSpec((tm, C), lambda l, i, c_ref: (jnp.where(l == c_ref[0], 0, i), 0))
    grid_spec = pltpu.PrefetchScalarGridSpec(num_scalar_prefetch=1, grid=(2, R // tm),
                                             in_specs=[slab, mine, other, slab, slab], out_specs=[slab] * 4)
    return _pc_prefetch(body, name=name, grid_spec=grid_spec, out_shape=[jax.ShapeDtypeStruct(w.shape, F32)] * 4,
                        dims=("parallel", "parallel"))(c.reshape(1).astype(jnp.int32), w, g_mine, g_other, m, v)


def _add_layers(g0, g1, ra, c, *, name):
    S, R, C = ra.shape
    tm = _row_tile(R, C)

    def body(c_ref, g0_ref, g1_ref, r_ref, o_ref):
        mine = jnp.where(c_ref[0] == 0, g0_ref[...], g1_ref[...])
        o_ref[...] = (mine + r_ref[...]).astype(BF16)

    def walked_if(layer):
        return lambda s, i, c_ref: (jnp.where(c_ref[0] == layer, s, 0), jnp.where(c_ref[0] == layer, i, 0), 0)

    blk = lambda s, i, c_ref: (s, i, 0)
    grid_spec = pltpu.PrefetchScalarGridSpec(
        num_scalar_prefetch=1, grid=(S, R // tm),
        in_specs=[pl.BlockSpec((None, tm, C), walked_if(0)), pl.BlockSpec((None, tm, C), walked_if(1)),
                  pl.BlockSpec((None, tm, C), blk)],
        out_specs=pl.BlockSpec((None, tm, C), blk))
    return _pc_prefetch(body, name=name, grid_spec=grid_spec, out_shape=jax.ShapeDtypeStruct((S, R, C), BF16),
                        dims=("parallel", "parallel"))(c.reshape(1).astype(jnp.int32), g0, g1, ra)


def _add_chips(p, rb, chip, *, name):
    S, Rh, C = p.shape
    tm = _row_tile(Rh, C)

    def body(s_ref, p_ref, r_ref, o_ref):
        o_ref[...] = ((p_ref[...].astype(F32) + r_ref[0].astype(F32)) + r_ref[1].astype(F32)) + r_ref[2].astype(F32)

    grid_spec = pltpu.PrefetchScalarGridSpec(
        num_scalar_prefetch=1, grid=(Rh // tm,),
        in_specs=[pl.BlockSpec((None, tm, C), lambda i, s_ref: (s_ref[0], i, 0)),
                  pl.BlockSpec((3, tm, C), lambda i, s_ref: (0, i, 0))],
        out_specs=pl.BlockSpec((tm, C), lambda i, s_ref: (i, 0)))
    return _pc_prefetch(body, name=name, grid_spec=grid_spec, out_shape=jax.ShapeDtypeStruct((Rh, C), F32),
                        dims=("parallel",))(chip.reshape(1).astype(jnp.int32), p, rb)


def _sum_slots(g, *, name):
    n, R, C = g.shape

    def body(g_ref, o_ref):
        acc = g_ref[0]
        for s in range(1, n):
            acc = acc + g_ref[s]
        o_ref[...] = acc

    return _pc(body, name=name, grid=(1,), in_specs=[pl.BlockSpec((n, R, C), lambda i: (0, 0, 0))],
               out_specs=pl.BlockSpec((R, C), lambda i: (0, 0)), out_shape=jax.ShapeDtypeStruct((R, C), F32),
               dims=("arbitrary",))(g)


ANY = pl.BlockSpec(memory_space=pl.ANY)


def _place():
    return lax.axis_index("x"), lax.axis_index("y"), lax.axis_index("c")


def _other_chips(x, y):
    return [(1 - x, y), (x, 1 - y), (1 - x, 1 - y)]


def _allgather_chips(ws, *, name):
    n = len(ws)

    def body(*refs):
        w_refs, out_refs, send_sems, recv_sems = refs[:n], refs[n:2 * n], refs[2 * n], refs[2 * n + 1]
        x, y, c = _place()
        sib = (x, y, 1 - c)
        chips = _other_chips(x, y)

        def copy(a, k, chip_id, layer, to):
            src = w_refs[a].at[layer] if k < 3 else out_refs[a].at[chip_id, layer]
            return pltpu.make_async_remote_copy(src_ref=src, dst_ref=out_refs[a].at[chip_id, layer],
                                                send_sem=send_sems.at[k * n + a], recv_sem=recv_sems.at[k * n + a],
                                                device_id=to, device_id_type=MESH)

        sends = [copy(a, j, 2 * x + y, c, (px, py, c)) for j, (px, py) in enumerate(chips) for a in range(n)]
        for cp in sends:
            cp.start()
        passed = []
        for j, (px, py) in enumerate(chips):
            for a in range(n):
                copy(a, j, 2 * px + py, c, (px, py, c)).wait_recv()
                fwd = copy(a, 3 + j, 2 * px + py, c, sib)
                fwd.start()
                passed.append(fwd)
        for j, (px, py) in enumerate(chips):
            for a in range(n):
                copy(a, 3 + j, 2 * px + py, 1 - c, sib).wait_recv()
        for cp in sends + passed:
            cp.wait_send()

    return _pc_comm(body, name=name, in_specs=[ANY] * n, out_specs=[ANY] * n,
                    out_shape=[jax.ShapeDtypeStruct((N_CHIPS,) + w.shape, w.dtype) for w in ws],
                    scratch_shapes=[pltpu.SemaphoreType.DMA((6 * n,)), pltpu.SemaphoreType.DMA((6 * n,))])(*ws)


def _send_other_layer_to_sibling(g0s, g1s, *, name):
    n = len(g0s)

    def body(*refs):
        g_refs = (refs[:n], refs[n:2 * n])
        out_refs, send_sems, recv_sems = refs[2 * n:3 * n], refs[3 * n], refs[3 * n + 1]
        x, y, c = _place()

        def copy(a, layer):
            return pltpu.make_async_remote_copy(src_ref=g_refs[layer][a], dst_ref=out_refs[a], send_sem=send_sems.at[a],
                                                recv_sem=recv_sems.at[a], device_id=(x, y, 1 - c), device_id_type=MESH)

        for layer in range(2):
            @pl.when(c == 1 - layer)
            def _(layer=layer):
                for a in range(n):
                    copy(a, layer).start()
        for a in range(n):
            copy(a, 0).wait()

    return _pc_comm(body, name=name, in_specs=[ANY] * (2 * n), out_specs=[ANY] * n,
                    out_shape=[jax.ShapeDtypeStruct(g.shape, g.dtype) for g in g0s],
                    scratch_shapes=[pltpu.SemaphoreType.DMA((n,)), pltpu.SemaphoreType.DMA((n,))])(*g0s, *g1s)


def _scatter_to_chips(ps, *, name):
    n = len(ps)

    def body(*refs):
        p_refs, rb_refs, send_sems, recv_sems = refs[:n], refs[n:2 * n], refs[2 * n], refs[2 * n + 1]
        x, y, c = _place()
        chips = _other_chips(x, y)
        sends = [pltpu.make_async_remote_copy(src_ref=p_refs[a].at[2 * px + py], dst_ref=rb_refs[a].at[j],
                                              send_sem=send_sems.at[j * n + a], recv_sem=recv_sems.at[j * n + a],
                                              device_id=(px, py, c), device_id_type=MESH)
                 for j, (px, py) in enumerate(chips) for a in range(n)]
        for cp in sends:
            cp.start()
        for cp in sends:
            cp.wait()

    return _pc_comm(body, name=name, in_specs=[ANY] * n, out_specs=[ANY] * n,
                    out_shape=[jax.ShapeDtypeStruct((3,) + p.shape[1:], p.dtype) for p in ps],
                    scratch_shapes=[pltpu.SemaphoreType.DMA((3 * n,)), pltpu.SemaphoreType.DMA((3 * n,))])(*ps)


def _swap_with_sibling(fs, *, name):
    n = len(fs)

    def body(*refs):
        f_refs, out_refs, send_sems, recv_sems = refs[:n], refs[n:2 * n], refs[2 * n], refs[2 * n + 1]
        x, y, c = _place()
        copies = [pltpu.make_async_remote_copy(src_ref=f_refs[a], dst_ref=out_refs[a], send_sem=send_sems.at[a],
                                               recv_sem=recv_sems.at[a], device_id=(x, y, 1 - c), device_id_type=MESH)
                  for a in range(n)]
        for cp in copies:
            cp.start()
        for cp in copies:
            cp.wait()

    return _pc_comm(body, name=name, in_specs=[ANY] * n, out_specs=[ANY] * n,
                    out_shape=[jax.ShapeDtypeStruct(f.shape, f.dtype) for f in fs],
                    scratch_shapes=[pltpu.SemaphoreType.DMA((n,)), pltpu.SemaphoreType.DMA((n,))])(*fs)


def _allgather_devices(v, *, name):
    R, C = v.shape

    def body(v_ref, out_ref, send_sems, recv_sems):
        x, y, c = _place()
        me = 4 * x + 2 * y + c
        out_ref[me] = v_ref[...]
        peers = []
        for k in range(1, 8):
            fx, fy, fc = (k >> 2) & 1, (k >> 1) & 1, k & 1
            px = 1 - x if fx else x
            py = 1 - y if fy else y
            pcc = 1 - c if fc else c
            peers.append((px, py, pcc))
        sends = []
        for k, peer in enumerate(peers):
            cp = pltpu.make_async_remote_copy(src_ref=v_ref, dst_ref=out_ref.at[me], send_sem=send_sems.at[k],
                                              recv_sem=recv_sems.at[k], device_id=peer, device_id_type=MESH)
            cp.start()
            sends.append(cp)
        for k, (px, py, pcc) in enumerate(peers):
            pltpu.make_async_remote_copy(src_ref=v_ref, dst_ref=out_ref.at[4 * px + 2 * py + pcc],
                                         send_sem=send_sems.at[k], recv_sem=recv_sems.at[k], device_id=peers[k],
                                         device_id_type=MESH).wait_recv()
        for cp in sends:
            cp.wait_send()

    vm = pl.BlockSpec(memory_space=pltpu.VMEM)
    return _pc_comm(body, name=name, in_specs=[vm], out_specs=vm, out_shape=jax.ShapeDtypeStruct((8, R, C), F32),
                    scratch_shapes=[pltpu.SemaphoreType.DMA((7,)), pltpu.SemaphoreType.DMA((7,))])(v)


D_MODEL = 1024
SC_W = D_MODEL // 4
GDN_W = D_MODEL // 2
SB_W = D_MODEL - SC_W - GDN_W
D_FF = 256 * ((8 * D_MODEL // 3 + 255) // 256)
O_SC, O_GQKV, O_GZ, O_GA, O_SB = 0, 3 * SC_W, 3 * SC_W + 3 * GDN_W, 3 * SC_W + 4 * GDN_W, \
    3 * SC_W + 4 * GDN_W + 2 * GDN_HEADS
P_GQKV, P_SC, P_SB = 0, 3 * GDN_W, 3 * GDN_W + 3 * SC_W
P_GZ = P_SB + 3 * SB_W
P_GAB = P_GZ + GDN_W
P_PAD = 256


def _proj_to_kernel_layout(w):
    pad = jnp.zeros((w.shape[0], P_PAD - 2 * GDN_HEADS), w.dtype)
    return jnp.concatenate([w[:, O_GQKV:O_GZ], w[:, O_SC:O_GQKV], w[:, O_SB:], w[:, O_GZ:O_GA], w[:, O_GA:O_SB], pad],
                           axis=1)


def _proj_from_kernel_layout(g):
    return jnp.concatenate([g[:, P_SC:P_SB], g[:, P_GQKV:P_SC], g[:, P_GZ:P_GAB], g[:, P_GAB:P_GAB + 2 * GDN_HEADS],
                            g[:, P_SB:P_GZ]], axis=1)


def _mixout_to_kernel_layout(w):
    return jnp.concatenate([w[SC_W:SC_W + GDN_W], w[:SC_W], w[SC_W + GDN_W:]], axis=0)


def _pack_vec(parts, rows_to):
    flat = jnp.concatenate([p.reshape(-1) for p in parts])
    return jnp.pad(flat, (0, rows_to * LANES - flat.shape[0])).reshape(rows_to, LANES)


def _unpack_vec(mat, shapes):
    flat = mat.reshape(-1)
    out, r = [], 0
    for shp in shapes:
        n = int(np.prod(shp))
        out.append(flat[r:r + n].reshape(shp))
        r += n
    return out


def _round_up(n, m):
    return (n + m - 1) // m * m


def _layer_fwd(x, p, l):
    L = x.shape[0]
    tag = "l%d_" % l
    h = _rmsnorm_fwd(x, p["wn_mix"], name=tag + "norm_mix")
    proj = _matmul(h, p["w_in"], "nn", tm=2048, tn=768, tk=D_MODEL, name=tag + "proj")
    (y_sc,) = _conv_pointwise_fwd([(proj, P_SC + SC_W), (proj, P_SC + 2 * SC_W)], [(p["w_sconv"], 0)], [(proj, P_SC)],
                                  _pre_product, _post_gate_mul, [(SC_W, BF16)], tc=SC_W, tm=512, name=tag + "sconv")
    (qkv,) = _conv_pointwise_fwd([(proj, P_GQKV)], [(p["w_gdn_conv"], 0)], [], _pre_identity, _post_silu,
                                 [(3 * GDN_W, F32)], tc=GDN_W, tm=512, name=tag + "gdn_conv")
    qe, ke, u, w, attn, eg = _gdn_prep_fwd(qkv, proj, P_GAB, p["a_log"], p["dt_bias"], name=tag + "gdn_prep")
    y_gdn, states = _gdn_scan_fwd(qe, ke, u, w, attn, eg, proj, P_GZ, p["wgn"], name=tag + "gdn_scan")
    sb_scale = (SB_W // SB_HEADS) ** -0.5
    sbqkv = jnp.concatenate([proj[:, P_SB:P_SB + SB_W] * sb_scale, proj[:, P_SB + SB_W:P_SB + 3 * SB_W]],
                            axis=1).astype(BF16)
    y_sb = _sb_fwd(sbqkv, name=tag + "sb_fwd")
    y_cat = [y_gdn, y_sc, y_sb]
    x2 = _matmul_rows_parts(y_cat, p["w_out"], "nn", res=x, name=tag + "mix_out")
    h2 = _rmsnorm_fwd(x2, p["wn_ffn"], name=tag + "norm_ffn")
    up_g = _matmul(h2, p["w_up_g"], "nn", tm=1024, tn=D_FF // 2, tk=D_MODEL, name=tag + "up_gate")
    up_v = _matmul(h2, p["w_up_v"], "nn", tm=1024, tn=D_FF // 2, tk=D_MODEL, name=tag + "up_val")
    (act,) = _conv_pointwise_fwd([(up_g, 0), (up_v, 0)], [(p["w_fconv_g"], 0), (p["w_fconv_v"], 0)], [],
                                 _pre_identity, _post_swiglu, [(D_FF, BF16)], tc=256, tm=1024, name=tag + "ffn_act")
    x3 = _matmul(act, p["w_down"], "nn", tm=1024, tn=D_MODEL, tk=D_FF // 2, res=x2, name=tag + "ffn_down")
    saved = dict(x=x, h=h, proj=proj, qkv=qkv, qe=qe, ke=ke, u=u, w=w, attn=attn, eg=eg, states=states,
                 sbqkv=sbqkv, y_cat=y_cat, x2=x2, h2=h2, up_g=up_g, up_v=up_v, act=act)
    return x3, saved


def _layer_bwd(dx3, p, s, l):
    L = dx3.shape[0]
    tag = "l%d_b_" % l
    g = {}
    dact = _matmul(dx3, p["w_down"], "nt", tm=1024, tn=D_FF // 2, tk=D_MODEL, name=tag + "dact")
    g["w_down"] = _matmul(s["act"], dx3, "tn", tm=D_FF // 2, tn=D_MODEL, tk=1024, name=tag + "dw_down")
    (dup_g, dup_v), _, (g["w_fconv_g"], g["w_fconv_v"]) = _conv_pointwise_bwd(
        [(s["up_g"], 0), (s["up_v"], 0)], [(p["w_fconv_g"], 0), (p["w_fconv_v"], 0)], [], [(dact, 0)],
        _pre_identity, _post_swiglu, D_FF, tc=256, tm=1024, name=tag + "ffn_act")
    dh2 = _matmul(dup_g, p["w_up_g"], "nt", tm=1024, tn=D_MODEL, tk=D_FF // 2, name=tag + "dh2_gate")
    dh2 = _matmul(dup_v, p["w_up_v"], "nt", tm=1024, tn=D_MODEL, tk=D_FF // 2, res=dh2, name=tag + "dh2_val")
    g["w_up"] = _matmul(s["h2"], dup_g, "tn", tm=D_MODEL, tn=D_FF // 2, tk=1024, slabs=(N_CHIPS, 0, None),
                        name=tag + "dw_up_gate")
    g["w_up"] = _matmul(s["h2"], dup_v, "tn", tm=D_MODEL, tn=D_FF // 2, tk=1024, slabs=(N_CHIPS, 2, g["w_up"]),
                        name=tag + "dw_up_val")
    dx2, g["wn_ffn"] = _rmsnorm_bwd(dh2, s["x2"], p["wn_ffn"], dx3, name=tag + "norm_ffn")
    dycat = _matmul(dx2, p["w_out"], "nt", tm=512, tn=D_MODEL, tk=D_MODEL, name=tag + "dycat")
    y_gdn, y_sc, y_sb = s["y_cat"]
    g["w_out"] = _matmul_tn_parts([y_sc, y_gdn, y_sb], dx2, name=tag + "dw_out")
    sb_scale = (SB_W // SB_HEADS) ** -0.5
    dsq, dsk, dsv = _sb_bwd(s["sbqkv"], dycat, GDN_W + SC_W, sb_scale, name=tag + "sb_bwd")
    dqe, dke, du, dw, dattn, deg, dgz, g["wgn"] = _gdn_scan_bwd(
        s["qe"], s["ke"], s["u"], s["w"], s["attn"], s["eg"], s["proj"], P_GZ, p["wgn"], s["states"], dycat, 0,
        name=tag + "gdn_scan")
    dqkv_act, dgab, g["a_log"], g["dt_bias"] = _gdn_prep_bwd(
        s["qkv"], s["proj"], P_GAB, p["a_log"], p["dt_bias"], dqe, dke, du, dw, dattn, deg, P_PAD,
        name=tag + "gdn_prep")
    (dqkv,), _, (g["w_gdn_conv"],) = _conv_pointwise_bwd(
        [(s["proj"], P_GQKV)], [(p["w_gdn_conv"], 0)], [], [(dqkv_act, 0)], _pre_identity, _post_silu, 3 * GDN_W,
        tc=GDN_W, tm=512, name=tag + "gdn_conv")
    (dsc_c, dsc_h), (dsc_b,), (g["w_sconv"],) = _conv_pointwise_bwd(
        [(s["proj"], P_SC + SC_W), (s["proj"], P_SC + 2 * SC_W)], [(p["w_sconv"], 0)], [(s["proj"], P_SC)],
        [(dycat, GDN_W)], _pre_product, _post_gate_mul, SC_W, tc=SC_W, tm=512, name=tag + "sconv")
    dproj = [dqkv, dsc_b, dsc_c, dsc_h, dsq, dsk, dsv, dgz, dgab]
    dh = _matmul_rows_parts(dproj, p["w_in"], "nt", name=tag + "dh")
    g["w_in"] = jnp.concatenate([_matmul_tn_parts(s["h"], dproj[:4], name=tag + "dw_in_a"),
                                 _matmul_tn_parts(s["h"], dproj[4:], name=tag + "dw_in_b")], axis=1)
    dx, g["wn_mix"] = _rmsnorm_bwd(dh, s["x"], p["wn_mix"], dx2, name=tag + "norm_mix")
    return dx, g


BIG = ("w_mix_in", "w_mix_out", "w_ffn_up", "w_ffn_down")
BIG_AXIS = {"w_mix_in": 2, "w_mix_out": 1, "w_ffn_up": 2, "w_ffn_down": 1}
SMALL_SHARDED = ("w_sconv", "w_gdn_conv", "w_ffn_conv")
SMALL_REPLICATED = ("w_norm_mix", "gdn_a_log", "gdn_dt_bias", "w_gdn_norm", "w_norm_ffn", "w_norm_final")
WEIGHTS = ("w_norm_mix", "w_mix_in", "w_sconv", "w_gdn_conv", "gdn_a_log", "gdn_dt_bias", "w_gdn_norm", "w_mix_out",
           "w_norm_ffn", "w_ffn_up", "w_ffn_conv", "w_ffn_down", "w_norm_final")


def kernel(x, w_norm_mix, w_mix_in, w_sconv, w_gdn_conv, gdn_a_log, gdn_dt_bias, w_gdn_norm, w_mix_out, w_norm_ffn, w_ffn_up, w_ffn_conv, w_ffn_down, w_norm_final, loss_target, m_w_norm_mix, m_w_mix_in, m_w_sconv, m_w_gdn_conv, m_gdn_a_log, m_gdn_dt_bias, m_w_gdn_norm, m_w_mix_out, m_w_norm_ffn, m_w_ffn_up, m_w_ffn_conv, m_w_ffn_down, m_w_norm_final, v_w_norm_mix, v_w_mix_in, v_w_sconv, v_w_gdn_conv, v_gdn_a_log, v_gdn_dt_bias, v_w_gdn_norm, v_w_mix_out, v_w_norm_ffn, v_w_ffn_up, v_w_ffn_conv, v_w_ffn_down, v_w_norm_final):
    W = dict(w_norm_mix=w_norm_mix, w_mix_in=w_mix_in, w_sconv=w_sconv, w_gdn_conv=w_gdn_conv, gdn_a_log=gdn_a_log,
             gdn_dt_bias=gdn_dt_bias, w_gdn_norm=w_gdn_norm, w_mix_out=w_mix_out, w_norm_ffn=w_norm_ffn,
             w_ffn_up=w_ffn_up, w_ffn_conv=w_ffn_conv, w_ffn_down=w_ffn_down, w_norm_final=w_norm_final)
    M = dict(w_norm_mix=m_w_norm_mix, w_mix_in=m_w_mix_in, w_sconv=m_w_sconv, w_gdn_conv=m_w_gdn_conv,
             gdn_a_log=m_gdn_a_log, gdn_dt_bias=m_gdn_dt_bias, w_gdn_norm=m_w_gdn_norm, w_mix_out=m_w_mix_out,
             w_norm_ffn=m_w_norm_ffn, w_ffn_up=m_w_ffn_up, w_ffn_conv=m_w_ffn_conv, w_ffn_down=m_w_ffn_down,
             w_norm_final=m_w_norm_final)
    V = dict(w_norm_mix=v_w_norm_mix, w_mix_in=v_w_mix_in, w_sconv=v_w_sconv, w_gdn_conv=v_w_gdn_conv,
             gdn_a_log=v_gdn_a_log, gdn_dt_bias=v_gdn_dt_bias, w_gdn_norm=v_w_gdn_norm, w_mix_out=v_w_mix_out,
             w_norm_ffn=v_w_norm_ffn, w_ffn_up=v_w_ffn_up, w_ffn_conv=v_w_ffn_conv, w_ffn_down=v_w_ffn_down,
             w_norm_final=v_w_norm_final)
    depth = w_mix_in.shape[0]
    L = x.shape[1]
    mx, my, mc = lax.axis_index("x"), lax.axis_index("y"), lax.axis_index("c")
    chip = 2 * mx + my

    assert depth == 2
    own = [W[n].astype(BF16) for n in BIG]
    gathered = _allgather_chips(own, name="gather_big")
    gathered = [lax.dynamic_update_slice(g, o[None], (chip, 0, 0, 0)) for g, o in zip(gathered, own)]
    full_big = [{n: jnp.concatenate([g[b, l] for b in range(N_CHIPS)], axis=BIG_AXIS[n] - 1)
                 for n, g in zip(BIG, gathered)} for l in range(depth)]

    small_sh_shapes = [W[n].shape for n in SMALL_SHARDED]
    n_small_sh = sum(int(np.prod(s)) for s in small_sh_shapes)
    small_rows = _round_up(n_small_sh, 8 * LANES) // LANES
    small_all = _allgather_devices(_pack_vec([W[n] for n in SMALL_SHARDED], small_rows), name="gather_small")
    small_chip = [_unpack_vec(small_all[2 * b], small_sh_shapes) for b in range(N_CHIPS)]
    full_small = {n: jnp.concatenate([small_chip[b][i] for b in range(N_CHIPS)], axis=2)
                  for i, n in enumerate(SMALL_SHARDED)}

    params = []
    for l in range(depth):
        w_up = full_big[l]["w_ffn_up"]
        fconv = full_small["w_ffn_conv"][l]
        params.append(dict(
            wn_mix=w_norm_mix[l], w_in=_proj_to_kernel_layout(full_big[l]["w_mix_in"]),
            w_sconv=full_small["w_sconv"][l], w_gdn_conv=full_small["w_gdn_conv"][l],
            a_log=gdn_a_log[l].reshape(GDN_HEADS, 1, 1), dt_bias=gdn_dt_bias[l].reshape(GDN_HEADS, 1, 1),
            wgn=w_gdn_norm[l], w_out=_mixout_to_kernel_layout(full_big[l]["w_mix_out"]), wn_ffn=w_norm_ffn[l],
            w_up_g=w_up[:, :D_FF], w_up_v=w_up[:, D_FF:], w_fconv_g=fconv[:, :D_FF], w_fconv_v=fconv[:, D_FF:],
            w_down=full_big[l]["w_ffn_down"]))

    xs = x[0]
    saved = []
    for l in range(depth):
        xs, s = _layer_fwd(xs, params[l], l)
        saved.append(s)
    loss_row, dx, g_norm_final = _final_loss(xs, w_norm_final, loss_target[0], name="final_loss")
    grads = [None] * depth
    for l in reversed(range(depth)):
        dx, grads[l] = _layer_bwd(dx, params[l], saved[l], l)
    loss = lax.psum(loss_row[0, 0], ("x", "y", "c"))

    G = {
        "w_sconv": jnp.stack([grads[l]["w_sconv"] for l in range(depth)]),
        "w_gdn_conv": jnp.stack([grads[l]["w_gdn_conv"] for l in range(depth)]),
        "w_ffn_conv": jnp.stack([jnp.concatenate([grads[l]["w_fconv_g"], grads[l]["w_fconv_v"]], axis=1)
                                 for l in range(depth)]),
        "w_norm_mix": jnp.stack([grads[l]["wn_mix"].reshape(-1) for l in range(depth)]),
        "gdn_a_log": jnp.stack([grads[l]["a_log"].reshape(-1) for l in range(depth)]),
        "gdn_dt_bias": jnp.stack([grads[l]["dt_bias"].reshape(-1) for l in range(depth)]),
        "w_gdn_norm": jnp.stack([grads[l]["wgn"].reshape(-1) for l in range(depth)]),
        "w_norm_ffn": jnp.stack([grads[l]["wn_ffn"].reshape(-1) for l in range(depth)]),
        "w_norm_final": g_norm_final.reshape(-1),
    }

    def by_shard(l):
        g_in = _proj_from_kernel_layout(grads[l]["w_in"])
        g_in = g_in.reshape(D_MODEL, N_CHIPS, -1).transpose(1, 0, 2)
        return [g_in, grads[l]["w_out"].reshape(N_CHIPS, -1, D_MODEL), grads[l]["w_up"],
                grads[l]["w_down"].reshape(N_CHIPS, -1, D_MODEL)]

    g_layers = [by_shard(l) for l in range(depth)]
    from_sibling = _send_other_layer_to_sibling(g_layers[0], g_layers[1], name="rs_sibling")
    chip_sums = [_add_layers(g0, g1, ra, mc, name="rs_add_layers_" + n)
                 for n, g0, g1, ra in zip(BIG, g_layers[0], g_layers[1], from_sibling)]
    from_chips = _scatter_to_chips(chip_sums, name="rs_chips")
    mine = [_add_chips(p, rb, chip, name="rs_add_chips_" + n) for n, p, rb in zip(BIG, chip_sums, from_chips)]
    other = _swap_with_sibling(mine, name="rs_result")
    out_g, out_d, out_m, out_v = {}, {}, {}, {}
    for n, g_mine, g_other in zip(BIG, mine, other):
        out_g[n], out_d[n], out_m[n], out_v[n] = _adamw_layers(W[n], g_mine, g_other, M[n], V[n], mc,
                                                               name="adamw_" + n)

    small_names = SMALL_SHARDED + SMALL_REPLICATED
    small_full_shapes = [G[n].shape for n in small_names]
    n_small = sum(int(np.prod(s)) for s in small_full_shapes)
    red_rows = _round_up(n_small, 8 * LANES) // LANES
    partials = _allgather_devices(_pack_vec([G[n] for n in small_names], red_rows), name="reduce_small")
    summed = _unpack_vec(_sum_slots(partials, name="reduce_small_sum"), small_full_shapes)
    g_small = {}
    for n, a in zip(small_names, summed):
        if n in SMALL_SHARDED:
            width = a.shape[2] // N_CHIPS
            a = lax.dynamic_slice_in_dim(a, chip * width, width, axis=2)
        g_small[n] = a
    own_shapes = [W[n].shape for n in small_names]
    n_own = sum(int(np.prod(s)) for s in own_shapes)
    own_rows = _round_up(n_own, 8 * LANES) // LANES
    packed = [_pack_vec([src[n] for n in small_names], own_rows) for src in (W, g_small, M, V)]
    d_s, nm_s, nv_s = _adamw(*packed, name="adamw_small", tm=own_rows)
    for mat, dst in ((packed[1], out_g), (d_s, out_d), (nm_s, out_m), (nv_s, out_v)):
        for n, a in zip(small_names, _unpack_vec(mat, own_shapes)):
            dst[n] = a

    outs = [loss, dx[None]]
    for dst in (out_g, out_d, out_m, out_v):
        outs += [dst[n] for n in WEIGHTS]
    return tuple(outs)
```

```python
import jax
import jax.numpy as jnp
import numpy as np
from jax import lax
from jax.experimental import pallas as pl
from jax.experimental.pallas import tpu as pltpu

F32 = jnp.float32
BF16 = jnp.bfloat16
MESH = pl.DeviceIdType.MESH

NORM_EPS = 1e-6
GDN_HEADS = 4
GDN_CHUNK = 64
GDN_SCAN_CHUNKS = 4
SB_HEADS = 4
SB_DEAD_LOG = -110.0
ADAM_LR = 0.001
ADAM_B1 = 0.9
ADAM_B2 = 0.999
ADAM_EPS = 1e-08
ADAM_WD = 0.01
ADAM_STEP = 10

VMEM_LIMIT_BYTES = 48 * 1024 * 1024
HALO = 8
CONV_SLAB = 128
LANES = 128
N_CHIPS = 4


def _pc(body, *, name, grid, in_specs, out_specs, out_shape, scratch_shapes=(), dims=None, aliases=None):
    params = dict(vmem_limit_bytes=VMEM_LIMIT_BYTES)
    if dims is not None:
        params["dimension_semantics"] = dims
    return pl.pallas_call(body, name=name, grid=grid, in_specs=in_specs, out_specs=out_specs, out_shape=out_shape,
                          scratch_shapes=list(scratch_shapes), input_output_aliases=aliases or {},
                          compiler_params=pltpu.CompilerParams(**params))


def _pc_prefetch(body, *, name, grid_spec, out_shape, dims):
    return pl.pallas_call(body, name=name, grid_spec=grid_spec, out_shape=out_shape,
                          compiler_params=pltpu.CompilerParams(vmem_limit_bytes=VMEM_LIMIT_BYTES,
                                                               dimension_semantics=dims))


def _pc_comm(body, *, name, in_specs, out_specs, out_shape, scratch_shapes):
    return pl.pallas_call(body, name=name, in_specs=in_specs, out_specs=out_specs, out_shape=out_shape,
                          scratch_shapes=list(scratch_shapes),
                          compiler_params=pltpu.CompilerParams(vmem_limit_bytes=VMEM_LIMIT_BYTES))


_DIMS = {"nn": (((1,), (0,)), ((), ())), "nt": (((1,), (1,)), ((), ())), "tn": (((0,), (0,)), ((), ()))}


def _matmul(a, b, mode, *, name, tm=512, tn=512, tk=512, out_dtype=F32, res=None, slabs=None):
    if mode == "nn":
        (M, K), (K2, N) = a.shape, b.shape
    elif mode == "nt":
        (M, K), (N, K2) = a.shape, b.shape
    else:
        (K, M), (K2, N) = a.shape, b.shape
    assert K == K2, (a.shape, b.shape, mode)
    tm, tn, tk = min(tm, M), min(tn, N), min(tk, K)
    assert M % tm == 0 and N % tn == 0 and K % tk == 0, (M, N, K, tm, tn, tk)
    nk = K // tk
    if mode == "tn":
        a_spec = pl.BlockSpec((tk, tm), lambda i, j, k: (k, i))
    else:
        a_spec = pl.BlockSpec((tm, tk), lambda i, j, k: (i, k))
    if mode == "nt":
        b_spec = pl.BlockSpec((tn, tk), lambda i, j, k: (j, k))
    else:
        b_spec = pl.BlockSpec((tk, tn), lambda i, j, k: (k, j))
    o_spec = pl.BlockSpec((tm, tn), lambda i, j, k: (i, j))
    has_res = res is not None
    dn = _DIMS[mode]

    def body(*refs):
        if has_res:
            a_ref, b_ref, r_ref, o_ref, acc = refs
        else:
            a_ref, b_ref, o_ref, acc = refs
        k = pl.program_id(2)
        p = lax.dot_general(a_ref[...].astype(BF16), b_ref[...].astype(BF16), dn, preferred_element_type=F32)

        def finish(total):
            if has_res:
                total = total + r_ref[...].astype(F32)
            o_ref[...] = total.astype(out_dtype)

        if nk == 1:
            finish(p)
        else:
            @pl.when(k == 0)
            def _():
                acc[...] = p

            @pl.when(k > 0)
            def _():
                acc[...] += p

            @pl.when(k == nk - 1)
            def _():
                finish(acc[...])

    in_specs = [a_spec, b_spec] + ([o_spec] if has_res else [])
    args = (a, b) + ((res,) if has_res else ())
    out_shape = jax.ShapeDtypeStruct((M, N), out_dtype)
    aliases = None
    if slabs is not None:
        n_slabs, first, into = slabs
        assert not has_res and tm == M
        o_spec = pl.BlockSpec((None, tm, tn), lambda i, j, k: (j + first, i, 0))
        out_shape = jax.ShapeDtypeStruct((n_slabs, M, tn), out_dtype)
        if into is not None:
            in_specs.append(pl.BlockSpec(memory_space=pl.ANY))
            args = args + (into,)
            aliases = {2: 0}
            inner = body

            def body(a_ref, b_ref, into_ref, o_ref, acc):
                inner(a_ref, b_ref, o_ref, acc)
    return _pc(body, name=name, grid=(M // tm, N // tn, nk), in_specs=in_specs, out_specs=o_spec,
               out_shape=out_shape, scratch_shapes=[pltpu.VMEM((tm, tn), F32)],
               dims=("parallel", "parallel", "arbitrary"), aliases=aliases)(*args)


def _offsets(parts, own_width_aligned):
    offs, at = [], 0
    for p in parts:
        assert at % (p.shape[1] if own_width_aligned else LANES) == 0, (at, p.shape)
        offs.append(at)
        at += p.shape[1]
    return offs, at


def _matmul_rows_parts(parts, w, mode, *, name, tm=512, res=None):
    M = parts[0].shape[0]
    offs, K = _offsets(parts, True)
    tm = min(tm, M)
    N = w.shape[1] if mode == "nn" else w.shape[0]
    assert (w.shape[0] if mode == "nn" else w.shape[1]) == K
    has_res = res is not None
    n = len(parts)

    def body(*refs):
        o_ref = refs[-1]
        total = None
        for s in range(n):
            p = lax.dot_general(refs[s][...].astype(BF16), refs[n + s][...].astype(BF16), _DIMS[mode],
                                preferred_element_type=F32)
            total = p if total is None else total + p
        if has_res:
            total = total + refs[2 * n][...]
        o_ref[...] = total

    in_specs = [pl.BlockSpec((tm, p.shape[1]), lambda i: (i, 0)) for p in parts]
    for p, off in zip(parts, offs):
        blk = off // p.shape[1]
        if mode == "nn":
            in_specs.append(pl.BlockSpec((p.shape[1], N), lambda i, blk=blk: (blk, 0)))
        else:
            in_specs.append(pl.BlockSpec((N, p.shape[1]), lambda i, blk=blk: (0, blk)))
    o_spec = pl.BlockSpec((tm, N), lambda i: (i, 0))
    args = tuple(parts) + (w,) * n + ((res,) if has_res else ())
    return _pc(body, name=name, grid=(M // tm,), in_specs=in_specs + ([o_spec] if has_res else []), out_specs=o_spec,
               out_shape=jax.ShapeDtypeStruct((M, N), F32), dims=("parallel",))(*args)


def _matmul_tn_parts(a, b, *, name, tk=1024):
    a_parts = list(a) if isinstance(a, (list, tuple)) else [a]
    b_parts = list(b) if isinstance(b, (list, tuple)) else [b]
    assert len(a_parts) == 1 or len(b_parts) == 1
    a_offs, M = _offsets(a_parts, False)
    b_offs, N = _offsets(b_parts, False)
    K = a_parts[0].shape[0]
    tk = min(tk, K)
    na, nb = len(a_parts), len(b_parts)

    def body(*refs):
        o_ref = refs[-1]
        first = pl.program_id(0) == 0
        for s in range(na):
            for t in range(nb):
                p = lax.dot_general(refs[s][...].astype(BF16), refs[na + t][...].astype(BF16), _DIMS["tn"],
                                    preferred_element_type=F32)
                rows = slice(a_offs[s], a_offs[s] + a_parts[s].shape[1])
                cols = slice(b_offs[t], b_offs[t] + b_parts[t].shape[1])

                @pl.when(first)
                def _(p=p, rows=rows, cols=cols):
                    o_ref[rows, cols] = p

                @pl.when(jnp.logical_not(first))
                def _(p=p, rows=rows, cols=cols):
                    o_ref[rows, cols] += p

    in_specs = [pl.BlockSpec((tk, p.shape[1]), lambda k: (k, 0)) for p in a_parts + b_parts]
    return _pc(body, name=name, grid=(K // tk,), in_specs=in_specs, out_specs=pl.BlockSpec((M, N), lambda k: (0, 0)),
               out_shape=jax.ShapeDtypeStruct((M, N), F32), dims=("arbitrary",))(*a_parts, *b_parts)


def _rmsnorm_fwd(x, w, *, name, tm=512):
    L, D = x.shape
    tm = min(tm, L)

    def body(x_ref, w_ref, h_ref):
        xv = x_ref[...]
        r = lax.rsqrt(jnp.mean(xv * xv, axis=-1, keepdims=True) + NORM_EPS)
        h_ref[...] = (xv * r * w_ref[...]).astype(BF16)

    return _pc(body, name=name, grid=(L // tm,),
               in_specs=[pl.BlockSpec((tm, D), lambda i: (i, 0)), pl.BlockSpec((1, D), lambda i: (0, 0))],
               out_specs=pl.BlockSpec((tm, D), lambda i: (i, 0)), out_shape=jax.ShapeDtypeStruct((L, D), BF16),
               dims=("parallel",))(x, w.reshape(1, D))


def _rmsnorm_bwd(dh, x, w, dres, *, name, tm=512):
    L, D = x.shape
    tm = min(tm, L)

    def body(dh_ref, x_ref, w_ref, dres_ref, dx_ref, dw_ref):
        xv = x_ref[...]
        r = lax.rsqrt(jnp.mean(xv * xv, axis=-1, keepdims=True) + NORM_EPS)
        xhat = xv * r
        dhv = dh_ref[...]
        g = dhv * w_ref[...]
        dx_ref[...] = dres_ref[...] + r * (g - xhat * jnp.mean(g * xhat, axis=-1, keepdims=True))
        part = jnp.sum(dhv * xhat, axis=0, keepdims=True)

        @pl.when(pl.program_id(0) == 0)
        def _():
            dw_ref[...] = part

        @pl.when(pl.program_id(0) > 0)
        def _():
            dw_ref[...] += part

    row = pl.BlockSpec((tm, D), lambda i: (i, 0))
    vec = pl.BlockSpec((1, D), lambda i: (0, 0))
    return _pc(body, name=name, grid=(L // tm,), in_specs=[row, row, vec, row], out_specs=[row, vec],
               out_shape=[jax.ShapeDtypeStruct((L, D), F32), jax.ShapeDtypeStruct((1, D), F32)],
               dims=("arbitrary",))(dh, x, w.reshape(1, D), dres)


def _final_loss(x, w, tgt, *, name, tm=512):
    L, D = x.shape
    tm = min(tm, L)

    def body(x_ref, w_ref, t_ref, loss_ref, dx_ref, dw_ref):
        xv = x_ref[...]
        r = lax.rsqrt(jnp.mean(xv * xv, axis=-1, keepdims=True) + NORM_EPS)
        xhat = xv * r
        e = xhat * w_ref[...] - t_ref[...]
        lpart = jnp.broadcast_to(0.5 * jnp.sum(jnp.mean(e * e, axis=-1, keepdims=True), axis=0, keepdims=True),
                                 (1, LANES))
        dy = e * (1.0 / D)
        g = dy * w_ref[...]
        dx_ref[...] = r * (g - xhat * jnp.mean(g * xhat, axis=-1, keepdims=True))
        part = jnp.sum(dy * xhat, axis=0, keepdims=True)

        @pl.when(pl.program_id(0) == 0)
        def _():
            dw_ref[...] = part
            loss_ref[...] = lpart

        @pl.when(pl.program_id(0) > 0)
        def _():
            dw_ref[...] += part
            loss_ref[...] += lpart

    row = pl.BlockSpec((tm, D), lambda i: (i, 0))
    vec = pl.BlockSpec((1, D), lambda i: (0, 0))
    lsp = pl.BlockSpec((1, LANES), lambda i: (0, 0))
    return _pc(body, name=name, grid=(L // tm,), in_specs=[row, vec, row], out_specs=[lsp, row, vec],
               out_shape=[jax.ShapeDtypeStruct((1, LANES), F32), jax.ShapeDtypeStruct((L, D), F32),
                          jax.ShapeDtypeStruct((1, D), F32)],
               dims=("arbitrary",))(x, w.reshape(1, D), tgt)


def _silu(x):
    return x * jax.nn.sigmoid(x)


def _conv_pointwise_fwd(xs, ws, es, pre, post, outs, *, tc, tm, name):
    L = xs[0][0].shape[0]
    tm = min(tm, L)
    ncol = outs[0][0] // tc
    nrow = L // tm
    hb = tm // HALO
    nx, nw, ne, no = len(xs), len(ws), len(es), len(outs)
    K = ws[0][0].shape[0]

    slab = min(CONV_SLAB, tm)
    win = slab + HALO
    assert tm % slab == 0 and K - 1 <= HALO

    def body(*refs):
        i = pl.program_id(1)
        first = (i > 0).astype(F32)
        n_in = 2 * nx + nw + ne
        o_refs = refs[n_in:n_in + no]
        x_pads = refs[n_in + no:]
        for n in range(nx):
            x_pads[n][0:HALO, :] = refs[2 * n + 1][...] * first
            x_pads[n][HALO:, :] = refs[2 * n][...]
        wv = [refs[2 * nx + n][...] for n in range(nw)]
        e_refs = refs[2 * nx + nw:n_in]

        @pl.loop(0, tm // slab)
        def _(t):
            r0 = pl.multiple_of(t * slab, HALO)
            ps = pre(*[x_pads[n][pl.ds(r0, win), :] for n in range(nx)])
            us = []
            for n in range(nw):
                u = None
                for k in range(K):
                    term = wv[n][k:k + 1, :] * (ps[n] if k == K - 1 else pltpu.roll(ps[n], K - 1 - k, 0))
                    u = term if u is None else u + term
                us.append(u[HALO:])
            rows = pl.ds(r0, slab)
            for o_ref, val in zip(o_refs, post(us, [e[rows, :] for e in e_refs])):
                o_ref[rows, :] = val.astype(o_ref.dtype)

    in_specs, args = [], []
    for arr, c0 in xs:
        off = c0 // tc
        in_specs.append(pl.BlockSpec((tm, tc), lambda j, i, off=off: (i, j + off)))
        in_specs.append(pl.BlockSpec((HALO, tc), lambda j, i, off=off: (jnp.maximum(i * hb - 1, 0), j + off)))
        args += [arr, arr]
    for arr, c0 in ws:
        off = c0 // tc
        in_specs.append(pl.BlockSpec((K, tc), lambda j, i, off=off: (0, j + off)))
        args.append(arr)
    for arr, c0 in es:
        off = c0 // tc
        in_specs.append(pl.BlockSpec((tm, tc), lambda j, i, off=off: (i, j + off)))
        args.append(arr)
    out_specs = [pl.BlockSpec((tm, tc), lambda j, i: (i, j)) for _ in range(no)]
    out_shape = [jax.ShapeDtypeStruct((L, c), dt) for c, dt in outs]
    return _pc(body, name=name, grid=(ncol, nrow), in_specs=in_specs, out_specs=out_specs, out_shape=out_shape,
               scratch_shapes=[pltpu.VMEM((tm + HALO, tc), F32)] * nx, dims=("parallel", "parallel"))(*args)


def _conv_pointwise_bwd(xs, ws, es, dys, pre, post, width, *, tc, tm, name, out_dtype=BF16):
    L = xs[0][0].shape[0]
    tm = min(tm, L)
    ncol = width // tc
    nrow = L // tm
    hb = tm // HALO
    nx, nw, ne, ny = len(xs), len(ws), len(es), len(dys)
    K = ws[0][0].shape[0]

    slab = min(CONV_SLAB, tm)
    win = slab + 2 * HALO
    assert tm % slab == 0 and K - 1 <= HALO

    def body(*refs):
        i = pl.program_id(1)
        first = (i > 0).astype(F32)
        more = (i < nrow - 1).astype(F32)
        n_in = 3 * nx + nw + 2 * ne + 2 * ny
        n_out = nx + ne + nw
        dx_refs = refs[n_in:n_in + nx]
        de_refs = refs[n_in + nx:n_in + nx + ne]
        dw_refs = refs[n_in + nx + ne:n_in + n_out]
        pads = refs[n_in + n_out:]
        x_pads, e_pads, dy_pads = pads[:nx], pads[nx:nx + ne], pads[nx + ne:]
        pos = 0
        for n in range(nx):
            x_pads[n][0:HALO, :] = refs[pos + 1][...] * first
            x_pads[n][HALO:HALO + tm, :] = refs[pos][...]
            x_pads[n][HALO + tm:, :] = refs[pos + 2][...]
            pos += 3
        wv = [refs[pos + n][...] for n in range(nw)]
        pos += nw
        for n in range(ne):
            e_pads[n][0:HALO, :] = jnp.zeros((HALO, tc), F32)
            e_pads[n][HALO:HALO + tm, :] = refs[pos][...]
            e_pads[n][HALO + tm:, :] = refs[pos + 1][...]
            pos += 2
        for n in range(ny):
            dy_pads[n][0:HALO, :] = jnp.zeros((HALO, tc), F32)
            dy_pads[n][HALO:HALO + tm, :] = refs[pos][...].astype(F32)
            dy_pads[n][HALO + tm:, :] = refs[pos + 1][...].astype(F32) * more
            pos += 2

        def one_slab(t, dw_acc):
            r0 = pl.multiple_of(t * slab, HALO)
            xw = [x_pads[n][pl.ds(r0, win), :] for n in range(nx)]
            ew = [e_pads[n][pl.ds(r0, win), :] for n in range(ne)]
            dyw = [dy_pads[n][pl.ds(r0, win), :] for n in range(ny)]
            ps, pre_vjp = jax.vjp(lambda *x_: pre(*x_), *xw)
            shifted = [[p if k == K - 1 else pltpu.roll(p, K - 1 - k, 0) for k in range(K)] for p in ps]
            us = []
            for n in range(nw):
                u = None
                for k in range(K):
                    term = wv[n][k:k + 1, :] * shifted[n][k]
                    u = term if u is None else u + term
                us.append(u)
            _, post_vjp = jax.vjp(lambda u_, e_: post(u_, e_), us, ew)
            dus, des = post_vjp(dyw)
            dps, dw_new = [], []
            for n in range(nw):
                dp = None
                for k in range(K):
                    term = wv[n][k:k + 1, :] * (dus[n] if k == K - 1 else pltpu.roll(dus[n], win - (K - 1 - k), 0))
                    dp = term if dp is None else dp + term
                dps.append(dp)
                inner = dus[n][HALO:HALO + slab]
                dw_new.append([dw_acc[n][k] + jnp.sum(inner * shifted[n][k][HALO:HALO + slab], axis=0, keepdims=True)
                               for k in range(K)])
            dxs = pre_vjp(dps)
            rows = pl.ds(r0, slab)
            for r, v in zip(dx_refs, dxs):
                r[rows, :] = v[HALO:HALO + slab].astype(out_dtype)
            for r, v in zip(de_refs, des):
                r[rows, :] = v[HALO:HALO + slab].astype(out_dtype)
            return dw_new

        zero = [[jnp.zeros((1, tc), F32) for _ in range(K)] for _ in range(nw)]
        dw_tile = lax.fori_loop(0, tm // slab, one_slab, zero)
        for n in range(nw):
            for k in range(K):
                @pl.when(i == 0)
                def _(n=n, k=k):
                    dw_refs[n][k:k + 1, :] = dw_tile[n][k]

                @pl.when(i > 0)
                def _(n=n, k=k):
                    dw_refs[n][k:k + 1, :] += dw_tile[n][k]

    in_specs, args = [], []

    def add_rows(arr, c0, prev, nxt):
        off = c0 // tc
        in_specs.append(pl.BlockSpec((tm, tc), lambda j, i, off=off: (i, j + off)))
        args.append(arr)
        if prev:
            in_specs.append(pl.BlockSpec((HALO, tc), lambda j, i, off=off: (jnp.maximum(i * hb - 1, 0), j + off)))
            args.append(arr)
        if nxt:
            last = L // HALO - 1
            in_specs.append(pl.BlockSpec((HALO, tc), lambda j, i, off=off: (jnp.minimum((i + 1) * hb, last), j + off)))
            args.append(arr)

    for arr, c0 in xs:
        add_rows(arr, c0, True, True)
    for arr, c0 in ws:
        off = c0 // tc
        in_specs.append(pl.BlockSpec((K, tc), lambda j, i, off=off: (0, j + off)))
        args.append(arr)
    for arr, c0 in es:
        add_rows(arr, c0, False, True)
    for arr, c0 in dys:
        add_rows(arr, c0, False, True)
    tile = pl.BlockSpec((tm, tc), lambda j, i: (i, j))
    wtile = pl.BlockSpec((K, tc), lambda j, i: (0, j))
    out_specs = [tile] * (nx + ne) + [wtile] * nw
    out_shape = [jax.ShapeDtypeStruct((L, width), out_dtype)] * (nx + ne) + \
        [jax.ShapeDtypeStruct((K, width), F32)] * nw
    res = _pc(body, name=name, grid=(ncol, nrow), in_specs=in_specs, out_specs=out_specs, out_shape=out_shape,
              scratch_shapes=[pltpu.VMEM((tm + 2 * HALO, tc), F32)] * (nx + ne + ny),
              dims=("parallel", "arbitrary"))(*args)
    return res[:nx], res[nx:nx + ne], res[nx + ne:]


def _pre_identity(*x):
    return list(x)


def _pre_product(c, h):
    return [c * h]


def _post_silu(us, es):
    return [_silu(us[0])]


def _post_gate_mul(us, es):
    return [es[0] * us[0]]


def _post_swiglu(us, es):
    return [_silu(us[0]) * us[1]]


def _make_dot(passes):
    def raw(a, b, dn):
        a_hi = a.astype(BF16)
        b_hi = b.astype(BF16)
        out = lax.dot_general(a_hi, b_hi, dn, preferred_element_type=F32)
        if passes == 3:
            a_lo = (a - a_hi.astype(F32)).astype(BF16)
            b_lo = (b - b_hi.astype(F32)).astype(BF16)
            out = out + lax.dot_general(a_hi, b_lo, dn, preferred_element_type=F32)
            out = out + lax.dot_general(a_lo, b_hi, dn, preferred_element_type=F32)
        return out

    @jax.custom_vjp
    def nn(a, b):
        return raw(a, b, _DIMS["nn"])

    @jax.custom_vjp
    def nt(a, b):
        return raw(a, b, _DIMS["nt"])

    @jax.custom_vjp
    def tn(a, b):
        return raw(a, b, _DIMS["tn"])

    nn.defvjp(lambda a, b: (nn(a, b), (a, b)), lambda r, g: (nt(g, r[1]), tn(r[0], g)))
    nt.defvjp(lambda a, b: (nt(a, b), (a, b)), lambda r, g: (nn(g, r[1]), tn(g, r[0])))
    tn.defvjp(lambda a, b: (tn(a, b), (a, b)), lambda r, g: (nt(r[1], g), nn(r[0], g)))
    return nn, nt, tn


_NN1, _NT1, _TN1 = _make_dot(1)
_NN3, _NT3, _TN3 = _make_dot(3)


def _l2norm(x):
    return x * lax.rsqrt(jnp.sum(x * x, axis=-1, keepdims=True) + NORM_EPS)


def _split_bf16(x):
    hi = x.astype(BF16)
    return hi, (x - hi.astype(F32)).astype(BF16)


def _products_with(lhs_list, rhs):
    n, rows = len(lhs_list), lhs_list[0].shape[0]
    r_hi, r_lo = _split_bf16(rhs)
    halves = [_split_bf16(l) for l in lhs_list]
    his = [h for h, _ in halves]
    o_hi = jnp.dot(jnp.concatenate(his + [lo for _, lo in halves], axis=0), r_hi, preferred_element_type=F32)
    o_lo = jnp.dot(jnp.concatenate(his, axis=0) if n > 1 else his[0], r_lo, preferred_element_type=F32)
    return [o_hi[i * rows:(i + 1) * rows] + o_hi[(n + i) * rows:(n + i + 1) * rows] + o_lo[i * rows:(i + 1) * rows]
            for i in range(n)]


def _unit_lower_inverse_raw(a_list):
    C = a_list[0].shape[0]
    ii = lax.broadcasted_iota(jnp.int32, (C, C), 0)
    jj = lax.broadcasted_iota(jnp.int32, (C, C), 1)
    eye = jnp.where(ii == jj, 1.0, 0.0)
    qs = [-a for a in a_list]
    ts = [eye + q for q in qs]
    qs = [_products_with([q], q)[0] for q in qs]
    n = 4
    while n <= C:
        last = n == C
        prods = [_products_with([t] if last else [t, q], q) for t, q in zip(ts, qs)]
        ts = [t + pr[0] for t, pr in zip(ts, prods)]
        if not last:
            qs = [pr[1] for pr in prods]
        n *= 2
    return ts


@jax.custom_vjp
def _unit_lower_inverse(a_list):
    return _unit_lower_inverse_raw(a_list)


def _unit_lower_inverse_fwd(a_list):
    ts = _unit_lower_inverse_raw(a_list)
    return ts, ts


def _unit_lower_inverse_bwd(ts, gs):
    xs = [_TN3(t, g) for t, g in zip(ts, gs)]
    return ([-_NT3(x, t) for x, t in zip(xs, ts)],)


_unit_lower_inverse.defvjp(_unit_lower_inverse_fwd, _unit_lower_inverse_bwd)


def _gdn_prep(units):
    C, Dh = units[0][0].shape
    ii = lax.broadcasted_iota(jnp.int32, (C, C), 0)
    jj = lax.broadcasted_iota(jnp.int32, (C, C), 1)
    lane = lax.broadcasted_iota(jnp.int32, (1, C), 1)
    causal = ii >= jj
    strict = ii > jj
    qs = [_l2norm(un[0]) * (Dh ** -0.5) for un in units]
    ks = [_l2norm(un[1]) for un in units]
    betas = [jax.nn.sigmoid(un[4]) for un in units]
    gs = [-jnp.exp(un[5]) * jax.nn.softplus(un[3] + un[6]) for un in units]
    gc_rows = [jnp.sum(jnp.where(ii <= jj, g, 0.0), axis=0, keepdims=True) for g in gs]
    gc_cols = [jnp.sum(jnp.where(ii == jj, r, 0.0), axis=1, keepdims=True) for r in gc_rows]
    decays = [jnp.where(causal, jnp.exp(jnp.where(causal, c - r, 0.0)), 0.0) for c, r in zip(gc_cols, gc_rows)]
    kbs = [k * b for k, b in zip(ks, betas)]
    kks = [_NT1(kb, k) for kb, k in zip(kbs, ks)]
    qks = [_NT1(q, k) for q, k in zip(qs, ks)]
    ts = _unit_lower_inverse([jnp.where(strict, kk * d, 0.0) for kk, d in zip(kks, decays)])
    eg_cols = [jnp.exp(c) for c in gc_cols]
    uws = [_NN3(t, jnp.concatenate([un[2] * b, kb * e], axis=1))
           for t, un, b, kb, e in zip(ts, units, betas, kbs, eg_cols)]
    out = []
    for q, k, qk, d, uw, e, r, c in zip(qs, ks, qks, decays, uws, eg_cols, gc_rows, gc_cols):
        g_last = jnp.sum(jnp.where(lane == C - 1, r, 0.0), axis=1, keepdims=True)
        out.append((q * e, k * jnp.exp(g_last - c), uw[:, :Dh], uw[:, Dh:], jnp.where(causal, qk * d, 0.0),
                    jnp.broadcast_to(jnp.exp(g_last), (1, Dh))))
    return out


def _gdn_step(units):
    v_news = [un[3] - _NN1(un[4], un[0]) for un in units]
    o_state = [_NN1(un[1], un[0]) for un in units]
    o_intra = [_NN1(un[5], vn) for un, vn in zip(units, v_news)]
    s_adds = [_TN1(un[2], vn) for un, vn in zip(units, v_news)]
    out = []
    for un, a, b, s_add in zip(units, o_state, o_intra, s_adds):
        o = a + b
        y = o * lax.rsqrt(jnp.mean(o * o, axis=-1, keepdims=True) + NORM_EPS) * un[8] * _silu(un[7])
        out.append((y, un[0] * un[6] + s_add))
    return out


def _gdn_prep_fwd(qkv, gab, gab_col, a_log, dt_bias, *, name, chunks=4):
    L = qkv.shape[0]
    H, C = GDN_HEADS, GDN_CHUNK
    W = qkv.shape[1] // 3
    Dh = W // H
    N = L // C
    chunks = min(chunks, N)
    R = chunks * C
    gab_off = gab_col // LANES

    def body(q_ref, k_ref, v_ref, gab_ref, al_ref, dt_ref, qe_ref, ke_ref, u_ref, w_ref, at_ref, eg_ref):
        where = [(cc, h) for cc in range(chunks) for h in range(H)]
        units = []
        for cc, h in where:
            rows, sl = slice(cc * C, (cc + 1) * C), slice(h * Dh, (h + 1) * Dh)
            units.append((q_ref[rows, sl], k_ref[rows, sl], v_ref[rows, sl], gab_ref[rows, h:h + 1],
                          gab_ref[rows, H + h:H + h + 1], al_ref[h], dt_ref[h]))
        for (cc, h), (qe, ke, u, w, attn, eg) in zip(where, _gdn_prep(units)):
            rows, sl = slice(cc * C, (cc + 1) * C), slice(h * Dh, (h + 1) * Dh)
            qe_ref[rows, sl] = qe
            ke_ref[rows, sl] = ke
            u_ref[rows, sl] = u
            w_ref[rows, sl] = w
            at_ref[h, rows, :] = attn
            eg_ref[cc, h:h + 1, :] = eg

    col = lambda c: pl.BlockSpec((R, W), lambda n, c=c: (n, c))
    tok = pl.BlockSpec((R, LANES), lambda n: (n, gab_off))
    par = pl.BlockSpec((H, 1, 1), lambda n: (0, 0, 0))
    wide = pl.BlockSpec((R, W), lambda n: (n, 0))
    return _pc(body, name=name, grid=(N // chunks,), in_specs=[col(0), col(1), col(2), tok, par, par],
               out_specs=[wide, wide, wide, wide, pl.BlockSpec((H, R, C), lambda n: (0, n, 0)),
                          pl.BlockSpec((chunks, H, Dh), lambda n: (n, 0, 0))],
               out_shape=[jax.ShapeDtypeStruct((L, W), F32)] * 4 + [jax.ShapeDtypeStruct((H, L, C), F32),
                                                                   jax.ShapeDtypeStruct((N, H, Dh), F32)],
               dims=("parallel",))(qkv, qkv, qkv, gab, a_log, dt_bias)


def _gdn_prep_bwd(qkv, gab, gab_col, a_log, dt_bias, dqe, dke, du, dw, dattn, deg, gab_width, *, name, chunks=4):
    L = qkv.shape[0]
    H, C = GDN_HEADS, GDN_CHUNK
    W = qkv.shape[1] // 3
    Dh = W // H
    N = L // C
    chunks = min(chunks, N)
    R = chunks * C
    gab_off = gab_col // LANES

    def body(q_ref, k_ref, v_ref, gab_ref, al_ref, dt_ref, dqe_ref, dke_ref, du_ref, dw_ref, dat_ref, deg_ref,
             dqkv_ref, dgab_ref, dal_ref, ddt_ref):
        first = pl.program_id(0) == 0
        lane = lax.broadcasted_iota(jnp.int32, (C, gab_width), 1)
        dal_sum, ddt_sum = [None] * H, [None] * H
        where = [(cc, h) for cc in range(chunks) for h in range(H)]
        units, cots = [], []
        for cc, h in where:
            rows, sl = slice(cc * C, (cc + 1) * C), slice(h * Dh, (h + 1) * Dh)
            units.append((q_ref[rows, sl], k_ref[rows, sl], v_ref[rows, sl], gab_ref[rows, h:h + 1],
                          gab_ref[rows, H + h:H + h + 1], al_ref[h], dt_ref[h]))
            cots.append((dqe_ref[rows, sl], dke_ref[rows, sl], du_ref[rows, sl], dw_ref[rows, sl],
                         dat_ref[h, rows, :], deg_ref[cc, h:h + 1, :]))
        _, vjp = jax.vjp(_gdn_prep, units)
        (d_units,) = vjp(cots)
        dgabs = [jnp.zeros((C, gab_width), F32) for _ in range(chunks)]
        for (cc, h), (dq, dk, dv, dga, dgb, dal, ddt) in zip(where, d_units):
            rows = slice(cc * C, (cc + 1) * C)
            dqkv_ref[rows, h * Dh:(h + 1) * Dh] = dq
            dqkv_ref[rows, W + h * Dh:W + (h + 1) * Dh] = dk
            dqkv_ref[rows, 2 * W + h * Dh:2 * W + (h + 1) * Dh] = dv
            dgabs[cc] = dgabs[cc] + jnp.where(lane == h, dga, 0.0) + jnp.where(lane == H + h, dgb, 0.0)
            dal_sum[h] = dal if dal_sum[h] is None else dal_sum[h] + dal
            ddt_sum[h] = ddt if ddt_sum[h] is None else ddt_sum[h] + ddt
        for cc in range(chunks):
            dgab_ref[cc * C:(cc + 1) * C, :] = dgabs[cc].astype(BF16)

        @pl.when(first)
        def _():
            for h in range(H):
                dal_ref[h] = dal_sum[h]
                ddt_ref[h] = ddt_sum[h]

        @pl.when(jnp.logical_not(first))
        def _():
            for h in range(H):
                dal_ref[h] += dal_sum[h]
                ddt_ref[h] += ddt_sum[h]

    col = lambda c: pl.BlockSpec((R, W), lambda n, c=c: (n, c))
    tok = pl.BlockSpec((R, LANES), lambda n: (n, gab_off))
    par = pl.BlockSpec((H, 1, 1), lambda n: (0, 0, 0))
    wide = pl.BlockSpec((R, W), lambda n: (n, 0))
    att = pl.BlockSpec((H, R, C), lambda n: (0, n, 0))
    egs = pl.BlockSpec((chunks, H, Dh), lambda n: (n, 0, 0))
    return _pc(body, name=name, grid=(N // chunks,),
               in_specs=[col(0), col(1), col(2), tok, par, par, wide, wide, wide, wide, att, egs],
               out_specs=[pl.BlockSpec((R, 3 * W), lambda n: (n, 0)), pl.BlockSpec((R, gab_width), lambda n: (n, 0)),
                          par, par],
               out_shape=[jax.ShapeDtypeStruct((L, 3 * W), F32), jax.ShapeDtypeStruct((L, gab_width), BF16)]
               + [jax.ShapeDtypeStruct((H, 1, 1), F32)] * 2,
               dims=("arbitrary",))(qkv, qkv, qkv, gab, a_log, dt_bias, dqe, dke, du, dw, dattn, deg)


def _gdn_scan_fwd(qe, ke, u, w, attn, eg, gz, gz_col, wgn, *, name):
    L, W = qe.shape
    H, C = GDN_HEADS, GDN_CHUNK
    Dh = W // H
    N = L // C
    gz_off = gz_col // W
    cps = min(GDN_SCAN_CHUNKS, N)
    R = cps * C

    def body(qe_ref, ke_ref, u_ref, w_ref, at_ref, eg_ref, gz_ref, wgn_ref, y_ref, st_ref, s_scr):
        @pl.when(pl.program_id(0) == 0)
        def _():
            s_scr[...] = jnp.zeros_like(s_scr)

        S = [s_scr[h] for h in range(H)]
        for cc in range(cps):
            rows = slice(cc * C, (cc + 1) * C)
            units = []
            for h in range(H):
                sl = slice(h * Dh, (h + 1) * Dh)
                st_ref[cc, h] = S[h]
                units.append((S[h], qe_ref[rows, sl], ke_ref[rows, sl], u_ref[rows, sl], w_ref[rows, sl],
                              at_ref[h, rows, :], eg_ref[cc, h:h + 1, :], gz_ref[rows, sl], wgn_ref[...]))
            for h, (y, S_new) in enumerate(_gdn_step(units)):
                y_ref[rows, h * Dh:(h + 1) * Dh] = y.astype(BF16)
                S[h] = S_new
        for h in range(H):
            s_scr[h] = S[h]

    wide = pl.BlockSpec((R, W), lambda n: (n, 0))
    return _pc(body, name=name, grid=(N // cps,),
               in_specs=[wide, wide, wide, wide, pl.BlockSpec((H, R, C), lambda n: (0, n, 0)),
                         pl.BlockSpec((cps, H, Dh), lambda n: (n, 0, 0)),
                         pl.BlockSpec((R, W), lambda n: (n, gz_off)), pl.BlockSpec((1, Dh), lambda n: (0, 0))],
               out_specs=[wide, pl.BlockSpec((cps, H, Dh, Dh), lambda n: (n, 0, 0, 0))],
               out_shape=[jax.ShapeDtypeStruct((L, W), BF16), jax.ShapeDtypeStruct((N, H, Dh, Dh), F32)],
               scratch_shapes=[pltpu.VMEM((H, Dh, Dh), F32)],
               dims=("arbitrary",))(qe, ke, u, w, attn, eg, gz, wgn.reshape(1, Dh))


def _gdn_scan_bwd(qe, ke, u, w, attn, eg, gz, gz_col, wgn, states, dy, dy_col, *, name):
    L, W = qe.shape
    H, C = GDN_HEADS, GDN_CHUNK
    Dh = W // H
    N = L // C
    gz_off = gz_col // W
    dy_off = dy_col // W
    cps = min(GDN_SCAN_CHUNKS, N)
    R = cps * C
    steps = N // cps

    def body(qe_ref, ke_ref, u_ref, w_ref, at_ref, eg_ref, gz_ref, wgn_ref, st_ref, dy_ref,
             dqe_ref, dke_ref, du_ref, dw_ref, dat_ref, deg_ref, dgz_ref, dwgn_ref, ds_scr):
        first = pl.program_id(0) == 0

        @pl.when(first)
        def _():
            ds_scr[...] = jnp.zeros_like(ds_scr)

        dwgn = None
        dS = [ds_scr[h] for h in range(H)]
        for cc in reversed(range(cps)):
            rows = slice(cc * C, (cc + 1) * C)
            units, cots = [], []
            for h in range(H):
                sl = slice(h * Dh, (h + 1) * Dh)
                units.append((st_ref[cc, h], qe_ref[rows, sl], ke_ref[rows, sl], u_ref[rows, sl], w_ref[rows, sl],
                              at_ref[h, rows, :], eg_ref[cc, h:h + 1, :], gz_ref[rows, sl], wgn_ref[...]))
                cots.append((dy_ref[rows, sl].astype(F32), dS[h]))
            _, vjp = jax.vjp(_gdn_step, units)
            (d_units,) = vjp(cots)
            for h, (dS_h, dqe, dke, du, dw, dat, deg, dgz, dwg) in enumerate(d_units):
                sl = slice(h * Dh, (h + 1) * Dh)
                dS[h] = dS_h
                dqe_ref[rows, sl] = dqe
                dke_ref[rows, sl] = dke
                du_ref[rows, sl] = du
                dw_ref[rows, sl] = dw
                dat_ref[h, rows, :] = dat
                deg_ref[cc, h:h + 1, :] = deg
                dgz_ref[rows, sl] = dgz.astype(BF16)
                dwgn = dwg if dwgn is None else dwgn + dwg
        for h in range(H):
            ds_scr[h] = dS[h]

        @pl.when(first)
        def _():
            dwgn_ref[...] = dwgn

        @pl.when(jnp.logical_not(first))
        def _():
            dwgn_ref[...] += dwgn

    rev = lambda n: steps - 1 - n
    wide = pl.BlockSpec((R, W), lambda n: (rev(n), 0))
    att = pl.BlockSpec((H, R, C), lambda n: (0, rev(n), 0))
    egs = pl.BlockSpec((cps, H, Dh), lambda n: (rev(n), 0, 0))
    vec = pl.BlockSpec((1, Dh), lambda n: (0, 0))
    return _pc(body, name=name, grid=(steps,),
               in_specs=[wide, wide, wide, wide, att, egs, pl.BlockSpec((R, W), lambda n: (rev(n), gz_off)), vec,
                         pl.BlockSpec((cps, H, Dh, Dh), lambda n: (rev(n), 0, 0, 0)),
                         pl.BlockSpec((R, W), lambda n: (rev(n), dy_off))],
               out_specs=[wide, wide, wide, wide, att, egs, wide, vec],
               out_shape=[jax.ShapeDtypeStruct((L, W), F32)] * 4 + [jax.ShapeDtypeStruct((H, L, C), F32),
                                                                   jax.ShapeDtypeStruct((N, H, Dh), F32),
                                                                   jax.ShapeDtypeStruct((L, W), BF16),
                                                                   jax.ShapeDtypeStruct((1, Dh), F32)],
               scratch_shapes=[pltpu.VMEM((H, Dh, Dh), F32)],
               dims=("arbitrary",))(qe, ke, u, w, attn, eg, gz, wgn.reshape(1, Dh), states, dy)


def _sb_scores(z, mask):
    sp = jnp.maximum(z, 0.0) + jnp.log(1.0 + jnp.exp(-jnp.abs(z)))
    lom = -sp if mask is None else jnp.where(mask, -sp, 0.0)
    return lom, z - sp


def _sb_alive(c_a, c_b):
    return jnp.maximum(jnp.max(c_a), jnp.max(c_b)) >= SB_DEAD_LOG


def _sb_masks(tq, width, dh):
    rr = lax.broadcasted_iota(jnp.int32, (tq, tq), 0)
    cc = lax.broadcasted_iota(jnp.int32, (tq, tq), 1)
    first_head = lax.broadcasted_iota(jnp.int32, (tq, width), 1) < dh
    return cc < rr, jnp.where(rr > cc, 1.0, 0.0).astype(BF16), first_head


def _sb_fwd(qkv, *, name, tq=256):
    L = qkv.shape[0]
    H = SB_HEADS
    width = 2 * (qkv.shape[1] // 3 // H)
    dh = width // 2
    npair = H // 2
    tq = min(tq, L)
    nq = L // tq

    def body(q_ref, k_ref, v_ref, o_ref):
        i = pl.program_id(1)
        diag, tri, first_head = _sb_masks(tq, width, dh)
        qp = q_ref[...]
        zero = jnp.zeros_like(qp)
        qs = (jnp.where(first_head, qp, zero), jnp.where(first_head, zero, qp))

        def blocks(js, carry, masks):
            units = [(b, hd) for b in range(len(js)) for hd in range(2)]
            starts = [pl.multiple_of(j * tq, tq) for j in js]
            ks = [k_ref[pl.ds(st, tq), :] for st in starts]
            vs = [v_ref[pl.ds(st, tq), :] for st in starts]
            zs = {(b, hd): lax.dot_general(qs[hd], ks[b], _DIMS["nt"], preferred_element_type=F32)
                  for b, hd in units}
            scores = {(b, hd): _sb_scores(zs[(b, hd)], masks[b]) for b, hd in units}
            later = {un: jnp.dot(scores[un][0].astype(BF16), tri, preferred_element_type=F32) for un in units}
            cs = [carry[hd][0] for hd in range(2)]
            accs = [carry[hd][1] for hd in range(2)]
            for b, hd in units:
                lom, lb = scores[(b, hd)]
                a = jnp.exp(lb + (cs[hd] + later[(b, hd)]))
                if masks[b] is not None:
                    a = jnp.where(masks[b], a, 0.0)
                accs[hd] = accs[hd] + jnp.dot(a.astype(BF16), vs[b], preferred_element_type=F32)
                cs[hd] = cs[hd] + jnp.sum(lom, axis=1, keepdims=True)
            return tuple((cs[hd], accs[hd]) for hd in range(2))

        init = tuple((jnp.zeros((tq, 1), F32), jnp.zeros((tq, width), F32)) for _ in range(2))
        carry = lax.cond(i > 0, lambda: blocks([i, i - 1], init, [diag, None]), lambda: blocks([i], init, [diag]))
        j_end, carry = lax.while_loop(lambda st: jnp.logical_and(st[0] >= 0, _sb_alive(st[1][0][0], st[1][1][0])),
                                      lambda st: (st[0] - 1, blocks([st[0]], st[1], [None])), (i - 2, carry))
        o_ref[...] = jnp.where(first_head, carry[0][1], carry[1][1]).astype(BF16)

    return _pc(body, name=name, grid=(npair, nq),
               in_specs=[pl.BlockSpec((tq, width), lambda p, i: (i, p)),
                         pl.BlockSpec((L, width), lambda p, i: (0, npair + p)),
                         pl.BlockSpec((L, width), lambda p, i: (0, 2 * npair + p))],
               out_specs=pl.BlockSpec((tq, width), lambda p, i: (i, p)),
               out_shape=jax.ShapeDtypeStruct((L, npair * width), BF16),
               dims=("parallel", "parallel"))(qkv, qkv, qkv)


def _sb_bwd(qkv, do, do_col, scale, *, name, tq=256):
    L = qkv.shape[0]
    H = SB_HEADS
    width = 2 * (qkv.shape[1] // 3 // H)
    dh = width // 2
    npair = H // 2
    tq = min(tq, L)
    nq = L // tq
    do_off = do_col // width

    def body(q_ref, k_ref, v_ref, do_ref, dq_ref, dk_ref, dv_ref):
        i = pl.program_id(1)

        @pl.when(i == 0)
        def _():
            dk_ref[...] = jnp.zeros_like(dk_ref)
            dv_ref[...] = jnp.zeros_like(dv_ref)

        diag, tri_later, first_head = _sb_masks(tq, width, dh)
        rr = lax.broadcasted_iota(jnp.int32, (tq, tq), 0)
        cc = lax.broadcasted_iota(jnp.int32, (tq, tq), 1)
        tri_before = jnp.where(rr < cc, 1.0, 0.0).astype(BF16)
        qp = q_ref[...]
        dop = do_ref[...].astype(BF16)
        zero = jnp.zeros_like(qp)
        qs = (jnp.where(first_head, qp, zero), jnp.where(first_head, zero, qp))
        dos = (jnp.where(first_head, dop, zero), jnp.where(first_head, zero, dop))
        ctots = []

        def blocks(js, carry, masks):
            nb = len(js)
            units = [(b, hd) for b in range(nb) for hd in range(2)]
            starts = [pl.multiple_of(j * tq, tq) for j in js]
            ks = [k_ref[pl.ds(st, tq), :] for st in starts]
            vs = [v_ref[pl.ds(st, tq), :] for st in starts]
            zs = {(b, hd): lax.dot_general(qs[hd], ks[b], _DIMS["nt"], preferred_element_type=F32)
                  for b, hd in units}
            das = {(b, hd): lax.dot_general(dos[hd], vs[b], _DIMS["nt"], preferred_element_type=F32)
                   for b, hd in units}
            scores = {(b, hd): _sb_scores(zs[(b, hd)], masks[b]) for b, hd in units}
            later = {un: jnp.dot(scores[un][0].astype(BF16), tri_later, preferred_element_type=F32) for un in units}
            pcs = [carry[hd][0] for hd in range(2)]
            avals = {}
            for b, hd in units:
                pcs[hd] = pcs[hd] + jnp.sum(scores[(b, hd)][0], axis=1, keepdims=True)
                a = jnp.exp(scores[(b, hd)][1] + ((ctots[hd] - pcs[hd]) + later[(b, hd)]))
                avals[(b, hd)] = a if masks[b] is None else jnp.where(masks[b], a, 0.0)
            gs = {un: das[un] * avals[un] for un in units}
            before = {un: jnp.dot(gs[un].astype(BF16), tri_before, preferred_element_type=F32) for un in units}
            pgs = [carry[hd][1] for hd in range(2)]
            dzs = {}
            for b, hd in units:
                sig = jnp.exp(scores[(b, hd)][1])
                dz = gs[(b, hd)] * (1.0 - sig) - (pgs[hd] + before[(b, hd)]) * sig
                dzs[(b, hd)] = (dz if masks[b] is None else jnp.where(masks[b], dz, 0.0)).astype(BF16)
                pgs[hd] = pgs[hd] + jnp.sum(gs[(b, hd)], axis=1, keepdims=True)
            dqs = [carry[hd][2] for hd in range(2)]
            for b, hd in units:
                dqs[hd] = dqs[hd] + jnp.dot(dzs[(b, hd)], ks[b], preferred_element_type=F32)
            for b in range(nb):
                dk_ref[pl.ds(starts[b], tq), :] += sum(
                    lax.dot_general(dzs[(b, hd)], qs[hd], _DIMS["tn"], preferred_element_type=F32) for hd in range(2))
                dv_ref[pl.ds(starts[b], tq), :] += sum(
                    lax.dot_general(avals[(b, hd)].astype(BF16), dos[hd], _DIMS["tn"], preferred_element_type=F32)
                    for hd in range(2))
            return tuple((pcs[hd], pgs[hd], dqs[hd]) for hd in range(2))

        def row_sums(j, mask):
            kj = k_ref[pl.ds(pl.multiple_of(j * tq, tq), tq), :]
            return tuple(jnp.sum(_sb_scores(lax.dot_general(qs[hd], kj, _DIMS["nt"], preferred_element_type=F32),
                                            mask)[0], axis=1, keepdims=True) for hd in range(2))

        near = row_sums(i, diag)
        near = lax.cond(i > 0, lambda: tuple(a + b for a, b in zip(near, row_sums(i - 1, None))), lambda: near)
        j_dead, live_sums = lax.while_loop(
            lambda st: jnp.logical_and(st[0] >= 0, _sb_alive(st[1][0], st[1][1])),
            lambda st: (st[0] - 1, tuple(a + b for a, b in zip(st[1], row_sums(st[0], None)))),
            (i - 2, near))
        ctots.extend(live_sums)
        col = jnp.zeros((tq, 1), F32)
        init = tuple((col, col, jnp.zeros((tq, width), F32)) for _ in range(2))
        carry = lax.fori_loop(jnp.maximum(j_dead, -1) + 1, i - 1, lambda j, cr: blocks([j], cr, [None]), init)
        carry = lax.cond(i > 0, lambda cr: blocks([i - 1, i], cr, [None, diag]), lambda cr: blocks([i], cr, [diag]),
                         carry)
        dq_ref[...] = (jnp.where(first_head, carry[0][2], carry[1][2]) * scale).astype(BF16)

    tile = pl.BlockSpec((tq, width), lambda p, i: (i, p))
    full = pl.BlockSpec((L, width), lambda p, i: (0, p))
    sds = jax.ShapeDtypeStruct((L, npair * width), F32)
    return _pc(body, name=name, grid=(npair, nq),
               in_specs=[tile, pl.BlockSpec((L, width), lambda p, i: (0, npair + p)),
                         pl.BlockSpec((L, width), lambda p, i: (0, 2 * npair + p)),
                         pl.BlockSpec((tq, width), lambda p, i: (i, do_off + p))],
               out_specs=[tile, full, full],
               out_shape=[jax.ShapeDtypeStruct((L, npair * width), BF16), sds, sds],
               dims=("parallel", "arbitrary"))(qkv, qkv, qkv, do)


def _adamw(w, g, m, v, *, name, tm=256):
    R, C = w.shape
    tm = min(tm, R)
    assert R % tm == 0, (R, tm)
    c1 = 1.0 - ADAM_B1 ** ADAM_STEP
    c2 = 1.0 - ADAM_B2 ** ADAM_STEP

    def body(w_ref, g_ref, m_ref, v_ref, d_ref, nm_ref, nv_ref):
        gv = g_ref[...]
        nm = ADAM_B1 * m_ref[...] + (1.0 - ADAM_B1) * gv
        nv = ADAM_B2 * v_ref[...] + (1.0 - ADAM_B2) * (gv * gv)
        d_ref[...] = -ADAM_LR * ((nm / c1) / (jnp.sqrt(nv / c2) + ADAM_EPS) + ADAM_WD * w_ref[...])
        nm_ref[...] = nm
        nv_ref[...] = nv

    blk = pl.BlockSpec((tm, C), lambda i: (i, 0))
    sds = jax.ShapeDtypeStruct((R, C), F32)
    return _pc(body, name=name, grid=(R // tm,), in_specs=[blk] * 4, out_specs=[blk] * 3, out_shape=[sds] * 3,
               dims=("parallel",))(w, g, m, v)


ELEMENTWISE_BLOCK_BYTES = 1 << 20


def _row_tile(rows, cols):
    for t in (512, 384, 352, 256, 176, 128, 88, 64, 32, 16, 8):
        if rows % t == 0 and t * cols * 4 <= ELEMENTWISE_BLOCK_BYTES:
            return t
    raise ValueError((rows, cols))


def _adamw_layers(w, g_mine, g_other, m, v, c, *, name):
    _, R, C = w.shape
    tm = _row_tile(R, C)
    c1 = 1.0 - ADAM_B1 ** ADAM_STEP
    c2 = 1.0 - ADAM_B2 ** ADAM_STEP

    def body(c_ref, w_ref, gm_ref, go_ref, m_ref, v_ref, g_ref, d_ref, nm_ref, nv_ref):
        gv = jnp.where(pl.program_id(0) == c_ref[0], gm_ref[...], go_ref[...])
        nm = ADAM_B1 * m_ref[...] + (1.0 - ADAM_B1) * gv
        nv = ADAM_B2 * v_ref[...] + (1.0 - ADAM_B2) * (gv * gv)
        g_ref[...] = gv
        d_ref[...] = -ADAM_LR * ((nm / c1) / (jnp.sqrt(nv / c2) + ADAM_EPS) + ADAM_WD * w_ref[...])
        nm_ref[...] = nm
        nv_ref[...] = nv

    slab = pl.BlockSpec((None, tm, C), lambda l, i, c_ref: (l, i, 0))
    mine = pl.BlockSpec((tm, C), lambda l, i, c_ref: (jnp.where(l == c_ref[0], i, 0), 0))
    other = pl.BlockSpec((tm, C), lambda l, i, c_ref: (jnp.where(l == c_ref[0], 0, i), 0))
    grid_spec = pltpu.PrefetchScalarGridSpec(num_scalar_prefetch=1, grid=(2, R // tm),
                                             in_specs=[slab, mine, other, slab, slab], out_specs=[slab] * 4)
    return _pc_prefetch(body, name=name, grid_spec=grid_spec, out_shape=[jax.ShapeDtypeStruct(w.shape, F32)] * 4,
                        dims=("parallel", "parallel"))(c.reshape(1).astype(jnp.int32), w, g_mine, g_other, m, v)


def _add_layers(g0, g1, ra, c, *, name):
    S, R, C = ra.shape
    tm = _row_tile(R, C)

    def body(c_ref, g0_ref, g1_ref, r_ref, o_ref):
        mine = jnp.where(c_ref[0] == 0, g0_ref[...], g1_ref[...])
        o_ref[...] = (mine + r_ref[...]).astype(BF16)

    def walked_if(layer):
        return lambda s, i, c_ref: (jnp.where(c_ref[0] == layer, s, 0), jnp.where(c_ref[0] == layer, i, 0), 0)

    blk = lambda s, i, c_ref: (s, i, 0)
    grid_spec = pltpu.PrefetchScalarGridSpec(
        num_scalar_prefetch=1, grid=(S, R // tm),
        in_specs=[pl.BlockSpec((None, tm, C), walked_if(0)), pl.BlockSpec((None, tm, C), walked_if(1)),
                  pl.BlockSpec((None, tm, C), blk)],
        out_specs=pl.BlockSpec((None, tm, C), blk))
    return _pc_prefetch(body, name=name, grid_spec=grid_spec, out_shape=jax.ShapeDtypeStruct((S, R, C), BF16),
                        dims=("parallel", "parallel"))(c.reshape(1).astype(jnp.int32), g0, g1, ra)


def _add_chips(p, rb, chip, *, name):
    S, Rh, C = p.shape
    tm = _row_tile(Rh, C)

    def body(s_ref, p_ref, r_ref, o_ref):
        o_ref[...] = ((p_ref[...].astype(F32) + r_ref[0].astype(F32)) + r_ref[1].astype(F32)) + r_ref[2].astype(F32)

    grid_spec = pltpu.PrefetchScalarGridSpec(
        num_scalar_prefetch=1, grid=(Rh // tm,),
        in_specs=[pl.BlockSpec((None, tm, C), lambda i, s_ref: (s_ref[0], i, 0)),
                  pl.BlockSpec((3, tm, C), lambda i, s_ref: (0, i, 0))],
        out_specs=pl.BlockSpec((tm, C), lambda i, s_ref: (i, 0)))
    return _pc_prefetch(body, name=name, grid_spec=grid_spec, out_shape=jax.ShapeDtypeStruct((Rh, C), F32),
                        dims=("parallel",))(chip.reshape(1).astype(jnp.int32), p, rb)


def _sum_slots(g, *, name):
    n, R, C = g.shape

    def body(g_ref, o_ref):
        acc = g_ref[0]
        for s in range(1, n):
            acc = acc + g_ref[s]
        o_ref[...] = acc

    return _pc(body, name=name, grid=(1,), in_specs=[pl.BlockSpec((n, R, C), lambda i: (0, 0, 0))],
               out_specs=pl.BlockSpec((R, C), lambda i: (0, 0)), out_shape=jax.ShapeDtypeStruct((R, C), F32),
               dims=("arbitrary",))(g)


ANY = pl.BlockSpec(memory_space=pl.ANY)


def _place():
    return lax.axis_index("x"), lax.axis_index("y"), lax.axis_index("c")


def _other_chips(x, y):
    return [(1 - x, y), (x, 1 - y), (1 - x, 1 - y)]


def _allgather_chips(ws, *, name):
    n = len(ws)

    def body(*refs):
        w_refs, out_refs, send_sems, recv_sems = refs[:n], refs[n:2 * n], refs[2 * n], refs[2 * n + 1]
        x, y, c = _place()
        sib = (x, y, 1 - c)
        chips = _other_chips(x, y)

        def copy(a, k, chip_id, layer, to):
            src = w_refs[a].at[layer] if k < 3 else out_refs[a].at[chip_id, layer]
            return pltpu.make_async_remote_copy(src_ref=src, dst_ref=out_refs[a].at[chip_id, layer],
                                                send_sem=send_sems.at[k * n + a], recv_sem=recv_sems.at[k * n + a],
                                                device_id=to, device_id_type=MESH)

        sends = [copy(a, j, 2 * x + y, c, (px, py, c)) for j, (px, py) in enumerate(chips) for a in range(n)]
        for cp in sends:
            cp.start()
        passed = []
        for j, (px, py) in enumerate(chips):
            for a in range(n):
                copy(a, j, 2 * px + py, c, (px, py, c)).wait_recv()
                fwd = copy(a, 3 + j, 2 * px + py, c, sib)
                fwd.start()
                passed.append(fwd)
        for j, (px, py) in enumerate(chips):
            for a in range(n):
                copy(a, 3 + j, 2 * px + py, 1 - c, sib).wait_recv()
        for cp in sends + passed:
            cp.wait_send()

    return _pc_comm(body, name=name, in_specs=[ANY] * n, out_specs=[ANY] * n,
                    out_shape=[jax.ShapeDtypeStruct((N_CHIPS,) + w.shape, w.dtype) for w in ws],
                    scratch_shapes=[pltpu.SemaphoreType.DMA((6 * n,)), pltpu.SemaphoreType.DMA((6 * n,))])(*ws)


def _send_other_layer_to_sibling(g0s, g1s, *, name):
    n = len(g0s)

    def body(*refs):
        g_refs = (refs[:n], refs[n:2 * n])
        out_refs, send_sems, recv_sems = refs[2 * n:3 * n], refs[3 * n], refs[3 * n + 1]
        x, y, c = _place()

        def copy(a, layer):
            return pltpu.make_async_remote_copy(src_ref=g_refs[layer][a], dst_ref=out_refs[a], send_sem=send_sems.at[a],
                                                recv_sem=recv_sems.at[a], device_id=(x, y, 1 - c), device_id_type=MESH)

        for layer in range(2):
            @pl.when(c == 1 - layer)
            def _(layer=layer):
                for a in range(n):
                    copy(a, layer).start()
        for a in range(n):
            copy(a, 0).wait()

    return _pc_comm(body, name=name, in_specs=[ANY] * (2 * n), out_specs=[ANY] * n,
                    out_shape=[jax.ShapeDtypeStruct(g.shape, g.dtype) for g in g0s],
                    scratch_shapes=[pltpu.SemaphoreType.DMA((n,)), pltpu.SemaphoreType.DMA((n,))])(*g0s, *g1s)


def _scatter_to_chips(ps, *, name):
    n = len(ps)

    def body(*refs):
        p_refs, rb_refs, send_sems, recv_sems = refs[:n], refs[n:2 * n], refs[2 * n], refs[2 * n + 1]
        x, y, c = _place()
        chips = _other_chips(x, y)
        sends = [pltpu.make_async_remote_copy(src_ref=p_refs[a].at[2 * px + py], dst_ref=rb_refs[a].at[j],
                                              send_sem=send_sems.at[j * n + a], recv_sem=recv_sems.at[j * n + a],
                                              device_id=(px, py, c), device_id_type=MESH)
                 for j, (px, py) in enumerate(chips) for a in range(n)]
        for cp in sends:
            cp.start()
        for cp in sends:
            cp.wait()

    return _pc_comm(body, name=name, in_specs=[ANY] * n, out_specs=[ANY] * n,
                    out_shape=[jax.ShapeDtypeStruct((3,) + p.shape[1:], p.dtype) for p in ps],
                    scratch_shapes=[pltpu.SemaphoreType.DMA((3 * n,)), pltpu.SemaphoreType.DMA((3 * n,))])(*ps)


def _swap_with_sibling(fs, *, name):
    n = len(fs)

    def body(*refs):
        f_refs, out_refs, send_sems, recv_sems = refs[:n], refs[n:2 * n], refs[2 * n], refs[2 * n + 1]
        x, y, c = _place()
        copies = [pltpu.make_async_remote_copy(src_ref=f_refs[a], dst_ref=out_refs[a], send_sem=send_sems.at[a],
                                               recv_sem=recv_sems.at[a], device_id=(x, y, 1 - c), device_id_type=MESH)
                  for a in range(n)]
        for cp in copies:
            cp.start()
        for cp in copies:
            cp.wait()

    return _pc_comm(body, name=name, in_specs=[ANY] * n, out_specs=[ANY] * n,
                    out_shape=[jax.ShapeDtypeStruct(f.shape, f.dtype) for f in fs],
                    scratch_shapes=[pltpu.SemaphoreType.DMA((n,)), pltpu.SemaphoreType.DMA((n,))])(*fs)


def _allgather_devices(v, *, name):
    R, C = v.shape

    def body(v_ref, out_ref, send_sems, recv_sems):
        x, y, c = _place()
        me = 4 * x + 2 * y + c
        out_ref[me] = v_ref[...]
        peers = []
        for k in range(1, 8):
            fx, fy, fc = (k >> 2) & 1, (k >> 1) & 1, k & 1
            px = 1 - x if fx else x
            py = 1 - y if fy else y
            pcc = 1 - c if fc else c
            peers.append((px, py, pcc))
        sends = []
        for k, peer in enumerate(peers):
            cp = pltpu.make_async_remote_copy(src_ref=v_ref, dst_ref=out_ref.at[me], send_sem=send_sems.at[k],
                                              recv_sem=recv_sems.at[k], device_id=peer, device_id_type=MESH)
            cp.start()
            sends.append(cp)
        for k, (px, py, pcc) in enumerate(peers):
            pltpu.make_async_remote_copy(src_ref=v_ref, dst_ref=out_ref.at[4 * px + 2 * py + pcc],
                                         send_sem=send_sems.at[k], recv_sem=recv_sems.at[k], device_id=peers[k],
                                         device_id_type=MESH).wait_recv()
        for cp in sends:
            cp.wait_send()

    vm = pl.BlockSpec(memory_space=pltpu.VMEM)
    return _pc_comm(body, name=name, in_specs=[vm], out_specs=vm, out_shape=jax.ShapeDtypeStruct((8, R, C), F32),
                    scratch_shapes=[pltpu.SemaphoreType.DMA((7,)), pltpu.SemaphoreType.DMA((7,))])(v)


D_MODEL = 1024
SC_W = D_MODEL // 4
GDN_W = D_MODEL // 2
SB_W = D_MODEL - SC_W - GDN_W
D_FF = 256 * ((8 * D_MODEL // 3 + 255) // 256)
O_SC, O_GQKV, O_GZ, O_GA, O_SB = 0, 3 * SC_W, 3 * SC_W + 3 * GDN_W, 3 * SC_W + 4 * GDN_W, \
    3 * SC_W + 4 * GDN_W + 2 * GDN_HEADS
P_GQKV, P_SC, P_SB = 0, 3 * GDN_W, 3 * GDN_W + 3 * SC_W
P_GZ = P_SB + 3 * SB_W
P_GAB = P_GZ + GDN_W
P_PAD = 256


def _proj_to_kernel_layout(w):
    pad = jnp.zeros((w.shape[0], P_PAD - 2 * GDN_HEADS), w.dtype)
    return jnp.concatenate([w[:, O_GQKV:O_GZ], w[:, O_SC:O_GQKV], w[:, O_SB:], w[:, O_GZ:O_GA], w[:, O_GA:O_SB], pad],
                           axis=1)


def _proj_from_kernel_layout(g):
    return jnp.concatenate([g[:, P_SC:P_SB], g[:, P_GQKV:P_SC], g[:, P_GZ:P_GAB], g[:, P_GAB:P_GAB + 2 * GDN_HEADS],
                            g[:, P_SB:P_GZ]], axis=1)


def _mixout_to_kernel_layout(w):
    return jnp.concatenate([w[SC_W:SC_W + GDN_W], w[:SC_W], w[SC_W + GDN_W:]], axis=0)


def _pack_vec(parts, rows_to):
    flat = jnp.concatenate([p.reshape(-1) for p in parts])
    return jnp.pad(flat, (0, rows_to * LANES - flat.shape[0])).reshape(rows_to, LANES)


def _unpack_vec(mat, shapes):
    flat = mat.reshape(-1)
    out, r = [], 0
    for shp in shapes:
        n = int(np.prod(shp))
        out.append(flat[r:r + n].reshape(shp))
        r += n
    return out


def _round_up(n, m):
    return (n + m - 1) // m * m


def _layer_fwd(x, p, l):
    L = x.shape[0]
    tag = "l%d_" % l
    h = _rmsnorm_fwd(x, p["wn_mix"], name=tag + "norm_mix")
    proj = _matmul(h, p["w_in"], "nn", tm=2048, tn=768, tk=D_MODEL, name=tag + "proj")
    (y_sc,) = _conv_pointwise_fwd([(proj, P_SC + SC_W), (proj, P_SC + 2 * SC_W)], [(p["w_sconv"], 0)], [(proj, P_SC)],
                                  _pre_product, _post_gate_mul, [(SC_W, BF16)], tc=SC_W, tm=512, name=tag + "sconv")
    (qkv,) = _conv_pointwise_fwd([(proj, P_GQKV)], [(p["w_gdn_conv"], 0)], [], _pre_identity, _post_silu,
                                 [(3 * GDN_W, F32)], tc=GDN_W, tm=1024, name=tag + "gdn_conv")
    qe, ke, u, w, attn, eg = _gdn_prep_fwd(qkv, proj, P_GAB, p["a_log"], p["dt_bias"], name=tag + "gdn_prep")
    y_gdn, states = _gdn_scan_fwd(qe, ke, u, w, attn, eg, proj, P_GZ, p["wgn"], name=tag + "gdn_scan")
    sb_scale = (SB_W // SB_HEADS) ** -0.5
    sbqkv = jnp.concatenate([proj[:, P_SB:P_SB + SB_W] * sb_scale, proj[:, P_SB + SB_W:P_SB + 3 * SB_W]],
                            axis=1).astype(BF16)
    y_sb = _sb_fwd(sbqkv, name=tag + "sb_fwd")
    y_cat = [y_gdn, y_sc, y_sb]
    x2 = _matmul_rows_parts(y_cat, p["w_out"], "nn", res=x, name=tag + "mix_out")
    h2 = _rmsnorm_fwd(x2, p["wn_ffn"], name=tag + "norm_ffn")
    up_g = _matmul(h2, p["w_up_g"], "nn", tm=1024, tn=D_FF // 2, tk=D_MODEL, name=tag + "up_gate")
    up_v = _matmul(h2, p["w_up_v"], "nn", tm=1024, tn=D_FF // 2, tk=D_MODEL, name=tag + "up_val")
    (act,) = _conv_pointwise_fwd([(up_g, 0), (up_v, 0)], [(p["w_fconv_g"], 0), (p["w_fconv_v"], 0)], [],
                                 _pre_identity, _post_swiglu, [(D_FF, BF16)], tc=256, tm=2048, name=tag + "ffn_act")
    x3 = _matmul(act, p["w_down"], "nn", tm=1024, tn=D_MODEL, tk=D_FF // 2, res=x2, name=tag + "ffn_down")
    saved = dict(x=x, h=h, proj=proj, qkv=qkv, qe=qe, ke=ke, u=u, w=w, attn=attn, eg=eg, states=states,
                 sbqkv=sbqkv, y_cat=y_cat, x2=x2, h2=h2, up_g=up_g, up_v=up_v, act=act)
    return x3, saved


def _layer_bwd(dx3, p, s, l):
    L = dx3.shape[0]
    tag = "l%d_b_" % l
    g = {}
    dact = _matmul(dx3, p["w_down"], "nt", tm=1024, tn=D_FF // 2, tk=D_MODEL, name=tag + "dact")
    g["w_down"] = _matmul(s["act"], dx3, "tn", tm=D_FF // 2, tn=D_MODEL, tk=1024, name=tag + "dw_down")
    (dup_g, dup_v), _, (g["w_fconv_g"], g["w_fconv_v"]) = _conv_pointwise_bwd(
        [(s["up_g"], 0), (s["up_v"], 0)], [(p["w_fconv_g"], 0), (p["w_fconv_v"], 0)], [], [(dact, 0)],
        _pre_identity, _post_swiglu, D_FF, tc=256, tm=2048, name=tag + "ffn_act")
    dh2 = _matmul(dup_g, p["w_up_g"], "nt", tm=1024, tn=D_MODEL, tk=D_FF // 2, name=tag + "dh2_gate")
    dh2 = _matmul(dup_v, p["w_up_v"], "nt", tm=1024, tn=D_MODEL, tk=D_FF // 2, res=dh2, name=tag + "dh2_val")
    g["w_up"] = _matmul(s["h2"], dup_g, "tn", tm=D_MODEL, tn=D_FF // 2, tk=1024, slabs=(N_CHIPS, 0, None),
                        name=tag + "dw_up_gate")
    g["w_up"] = _matmul(s["h2"], dup_v, "tn", tm=D_MODEL, tn=D_FF // 2, tk=1024, slabs=(N_CHIPS, 2, g["w_up"]),
                        name=tag + "dw_up_val")
    dx2, g["wn_ffn"] = _rmsnorm_bwd(dh2, s["x2"], p["wn_ffn"], dx3, name=tag + "norm_ffn")
    dycat = _matmul(dx2, p["w_out"], "nt", tm=512, tn=D_MODEL, tk=D_MODEL, name=tag + "dycat")
    y_gdn, y_sc, y_sb = s["y_cat"]
    g["w_out"] = _matmul_tn_parts([y_sc, y_gdn, y_sb], dx2, name=tag + "dw_out")
    sb_scale = (SB_W // SB_HEADS) ** -0.5
    dsq, dsk, dsv = _sb_bwd(s["sbqkv"], dycat, GDN_W + SC_W, sb_scale, name=tag + "sb_bwd")
    dqe, dke, du, dw, dattn, deg, dgz, g["wgn"] = _gdn_scan_bwd(
        s["qe"], s["ke"], s["u"], s["w"], s["attn"], s["eg"], s["proj"], P_GZ, p["wgn"], s["states"], dycat, 0,
        name=tag + "gdn_scan")
    dqkv_act, dgab, g["a_log"], g["dt_bias"] = _gdn_prep_bwd(
        s["qkv"], s["proj"], P_GAB, p["a_log"], p["dt_bias"], dqe, dke, du, dw, dattn, deg, P_PAD,
        name=tag + "gdn_prep")
    (dqkv,), _, (g["w_gdn_conv"],) = _conv_pointwise_bwd(
        [(s["proj"], P_GQKV)], [(p["w_gdn_conv"], 0)], [], [(dqkv_act, 0)], _pre_identity, _post_silu, 3 * GDN_W,
        tc=GDN_W, tm=1024, name=tag + "gdn_conv")
    (dsc_c, dsc_h), (dsc_b,), (g["w_sconv"],) = _conv_pointwise_bwd(
        [(s["proj"], P_SC + SC_W), (s["proj"], P_SC + 2 * SC_W)], [(p["w_sconv"], 0)], [(s["proj"], P_SC)],
        [(dycat, GDN_W)], _pre_product, _post_gate_mul, SC_W, tc=SC_W, tm=512, name=tag + "sconv")
    dproj = [dqkv, dsc_b, dsc_c, dsc_h, dsq, dsk, dsv, dgz, dgab]
    dh = _matmul_rows_parts(dproj, p["w_in"], "nt", name=tag + "dh")
    g["w_in"] = jnp.concatenate([_matmul_tn_parts(s["h"], dproj[:4], name=tag + "dw_in_a"),
                                 _matmul_tn_parts(s["h"], dproj[4:], name=tag + "dw_in_b")], axis=1)
    dx, g["wn_mix"] = _rmsnorm_bwd(dh, s["x"], p["wn_mix"], dx2, name=tag + "norm_mix")
    return dx, g


BIG = ("w_mix_in", "w_mix_out", "w_ffn_up", "w_ffn_down")
BIG_AXIS = {"w_mix_in": 2, "w_mix_out": 1, "w_ffn_up": 2, "w_ffn_down": 1}
SMALL_SHARDED = ("w_sconv", "w_gdn_conv", "w_ffn_conv")
SMALL_REPLICATED = ("w_norm_mix", "gdn_a_log", "gdn_dt_bias", "w_gdn_norm", "w_norm_ffn", "w_norm_final")
WEIGHTS = ("w_norm_mix", "w_mix_in", "w_sconv", "w_gdn_conv", "gdn_a_log", "gdn_dt_bias", "w_gdn_norm", "w_mix_out",
           "w_norm_ffn", "w_ffn_up", "w_ffn_conv", "w_ffn_down", "w_norm_final")


def kernel(x, w_norm_mix, w_mix_in, w_sconv, w_gdn_conv, gdn_a_log, gdn_dt_bias, w_gdn_norm, w_mix_out, w_norm_ffn, w_ffn_up, w_ffn_conv, w_ffn_down, w_norm_final, loss_target, m_w_norm_mix, m_w_mix_in, m_w_sconv, m_w_gdn_conv, m_gdn_a_log, m_gdn_dt_bias, m_w_gdn_norm, m_w_mix_out, m_w_norm_ffn, m_w_ffn_up, m_w_ffn_conv, m_w_ffn_down, m_w_norm_final, v_w_norm_mix, v_w_mix_in, v_w_sconv, v_w_gdn_conv, v_gdn_a_log, v_gdn_dt_bias, v_w_gdn_norm, v_w_mix_out, v_w_norm_ffn, v_w_ffn_up, v_w_ffn_conv, v_w_ffn_down, v_w_norm_final):
    W = dict(w_norm_mix=w_norm_mix, w_mix_in=w_mix_in, w_sconv=w_sconv, w_gdn_conv=w_gdn_conv, gdn_a_log=gdn_a_log,
             gdn_dt_bias=gdn_dt_bias, w_gdn_norm=w_gdn_norm, w_mix_out=w_mix_out, w_norm_ffn=w_norm_ffn,
             w_ffn_up=w_ffn_up, w_ffn_conv=w_ffn_conv, w_ffn_down=w_ffn_down, w_norm_final=w_norm_final)
    M = dict(w_norm_mix=m_w_norm_mix, w_mix_in=m_w_mix_in, w_sconv=m_w_sconv, w_gdn_conv=m_w_gdn_conv,
             gdn_a_log=m_gdn_a_log, gdn_dt_bias=m_gdn_dt_bias, w_gdn_norm=m_w_gdn_norm, w_mix_out=m_w_mix_out,
             w_norm_ffn=m_w_norm_ffn, w_ffn_up=m_w_ffn_up, w_ffn_conv=m_w_ffn_conv, w_ffn_down=m_w_ffn_down,
             w_norm_final=m_w_norm_final)
    V = dict(w_norm_mix=v_w_norm_mix, w_mix_in=v_w_mix_in, w_sconv=v_w_sconv, w_gdn_conv=v_w_gdn_conv,
             gdn_a_log=v_gdn_a_log, gdn_dt_bias=v_gdn_dt_bias, w_gdn_norm=v_w_gdn_norm, w_mix_out=v_w_mix_out,
             w_norm_ffn=v_w_norm_ffn, w_ffn_up=v_w_ffn_up, w_ffn_conv=v_w_ffn_conv, w_ffn_down=v_w_ffn_down,
             w_norm_final=v_w_norm_final)
    depth = w_mix_in.shape[0]
    L = x.shape[1]
    mx, my, mc = lax.axis_index("x"), lax.axis_index("y"), lax.axis_index("c")
    chip = 2 * mx + my

    assert depth == 2
    own = [W[n].astype(BF16) for n in BIG]
    gathered = _allgather_chips(own, name="gather_big")
    gathered = [lax.dynamic_update_slice(g, o[None], (chip, 0, 0, 0)) for g, o in zip(gathered, own)]
    full_big = [{n: jnp.concatenate([g[b, l] for b in range(N_CHIPS)], axis=BIG_AXIS[n] - 1)
                 for n, g in zip(BIG, gathered)} for l in range(depth)]

    small_sh_shapes = [W[n].shape for n in SMALL_SHARDED]
    n_small_sh = sum(int(np.prod(s)) for s in small_sh_shapes)
    small_rows = _round_up(n_small_sh, 8 * LANES) // LANES
    small_all = _allgather_devices(_pack_vec([W[n] for n in SMALL_SHARDED], small_rows), name="gather_small")
    small_chip = [_unpack_vec(small_all[2 * b], small_sh_shapes) for b in range(N_CHIPS)]
    full_small = {n: jnp.concatenate([small_chip[b][i] for b in range(N_CHIPS)], axis=2)
                  for i, n in enumerate(SMALL_SHARDED)}

    params = []
    for l in range(depth):
        w_up = full_big[l]["w_ffn_up"]
        fconv = full_small["w_ffn_conv"][l]
        params.append(dict(
            wn_mix=w_norm_mix[l], w_in=_proj_to_kernel_layout(full_big[l]["w_mix_in"]),
            w_sconv=full_small["w_sconv"][l], w_gdn_conv=full_small["w_gdn_conv"][l],
            a_log=gdn_a_log[l].reshape(GDN_HEADS, 1, 1), dt_bias=gdn_dt_bias[l].reshape(GDN_HEADS, 1, 1),
            wgn=w_gdn_norm[l], w_out=_mixout_to_kernel_layout(full_big[l]["w_mix_out"]), wn_ffn=w_norm_ffn[l],
            w_up_g=w_up[:, :D_FF], w_up_v=w_up[:, D_FF:], w_fconv_g=fconv[:, :D_FF], w_fconv_v=fconv[:, D_FF:],
            w_down=full_big[l]["w_ffn_down"]))

    xs = x[0]
    saved = []
    for l in range(depth):
        xs, s = _layer_fwd(xs, params[l], l)
        saved.append(s)
    loss_row, dx, g_norm_final = _final_loss(xs, w_norm_final, loss_target[0], name="final_loss")
    grads = [None] * depth
    for l in reversed(range(depth)):
        dx, grads[l] = _layer_bwd(dx, params[l], saved[l], l)
    loss = lax.psum(loss_row[0, 0], ("x", "y", "c"))

    G = {
        "w_sconv": jnp.stack([grads[l]["w_sconv"] for l in range(depth)]),
        "w_gdn_conv": jnp.stack([grads[l]["w_gdn_conv"] for l in range(depth)]),
        "w_ffn_conv": jnp.stack([jnp.concatenate([grads[l]["w_fconv_g"], grads[l]["w_fconv_v"]], axis=1)
                                 for l in range(depth)]),
        "w_norm_mix": jnp.stack([grads[l]["wn_mix"].reshape(-1) for l in range(depth)]),
        "gdn_a_log": jnp.stack([grads[l]["a_log"].reshape(-1) for l in range(depth)]),
        "gdn_dt_bias": jnp.stack([grads[l]["dt_bias"].reshape(-1) for l in range(depth)]),
        "w_gdn_norm": jnp.stack([grads[l]["wgn"].reshape(-1) for l in range(depth)]),
        "w_norm_ffn": jnp.stack([grads[l]["wn_ffn"].reshape(-1) for l in range(depth)]),
        "w_norm_final": g_norm_final.reshape(-1),
    }

    def by_shard(l):
        g_in = _proj_from_kernel_layout(grads[l]["w_in"])
        g_in = g_in.reshape(D_MODEL, N_CHIPS, -1).transpose(1, 0, 2)
        return [g_in, grads[l]["w_out"].reshape(N_CHIPS, -1, D_MODEL), grads[l]["w_up"],
                grads[l]["w_down"].reshape(N_CHIPS, -1, D_MODEL)]

    g_layers = [by_shard(l) for l in range(depth)]
    from_sibling = _send_other_layer_to_sibling(g_layers[0], g_layers[1], name="rs_sibling")
    chip_sums = [_add_layers(g0, g1, ra, mc, name="rs_add_layers_" + n)
                 for n, g0, g1, ra in zip(BIG, g_layers[0], g_layers[1], from_sibling)]
    from_chips = _scatter_to_chips(chip_sums, name="rs_chips")
    mine = [_add_chips(p, rb, chip, name="rs_add_chips_" + n) for n, p, rb in zip(BIG, chip_sums, from_chips)]
    other = _swap_with_sibling(mine, name="rs_result")
    out_g, out_d, out_m, out_v = {}, {}, {}, {}
    for n, g_mine, g_other in zip(BIG, mine, other):
        out_g[n], out_d[n], out_m[n], out_v[n] = _adamw_layers(W[n], g_mine, g_other, M[n], V[n], mc,
                                                               name="adamw_" + n)

    small_names = SMALL_SHARDED + SMALL_REPLICATED
    small_full_shapes = [G[n].shape for n in small_names]
    n_small = sum(int(np.prod(s)) for s in small_full_shapes)
    red_rows = _round_up(n_small, 8 * LANES) // LANES
    partials = _allgather_devices(_pack_vec([G[n] for n in small_names], red_rows), name="reduce_small")
    summed = _unpack_vec(_sum_slots(partials, name="reduce_small_sum"), small_full_shapes)
    g_small = {}
    for n, a in zip(small_names, summed):
        if n in SMALL_SHARDED:
            width = a.shape[2] // N_CHIPS
            a = lax.dynamic_slice_in_dim(a, chip * width, width, axis=2)
        g_small[n] = a
    own_shapes = [W[n].shape for n in small_names]
    n_own = sum(int(np.prod(s)) for s in own_shapes)
    own_rows = _round_up(n_own, 8 * LANES) // LANES
    packed = [_pack_vec([src[n] for n in small_names], own_rows) for src in (W, g_small, M, V)]
    d_s, nm_s, nv_s = _adamw(*packed, name="adamw_small", tm=own_rows)
    for mat, dst in ((packed[1], out_g), (d_s, out_d), (nm_s, out_m), (nv_s, out_v)):
        for n, a in zip(small_names, _unpack_vec(mat, own_shapes)):
            dst[n] = a

    outs = [loss, dx[None]]
    for dst in (out_g, out_d, out_m, out_v):
        outs += [dst[n] for n in WEIGHTS]
    return tuple(outs)
```

```python
import jax
import jax.numpy as jnp
import numpy as np
from jax import lax
from jax.experimental import pallas as pl
from jax.experimental.pallas import tpu as pltpu

F32 = jnp.float32
BF16 = jnp.bfloat16
MESH = pl.DeviceIdType.MESH

NORM_EPS = 1e-6
GDN_HEADS = 4
GDN_CHUNK = 64
GDN_SCAN_CHUNKS = 4
SB_HEADS = 4
SB_DEAD_LOG = -110.0
ADAM_LR = 0.001
ADAM_B1 = 0.9
ADAM_B2 = 0.999
ADAM_EPS = 1e-08
ADAM_WD = 0.01
ADAM_STEP = 10

VMEM_LIMIT_BYTES = 48 * 1024 * 1024
HALO = 8
CONV_SLAB = 128
LANES = 128
N_CHIPS = 4


def _pc(body, *, name, grid, in_specs, out_specs, out_shape, scratch_shapes=(), dims=None, aliases=None):
    params = dict(vmem_limit_bytes=VMEM_LIMIT_BYTES)
    if dims is not None:
        params["dimension_semantics"] = dims
    return pl.pallas_call(body, name=name, grid=grid, in_specs=in_specs, out_specs=out_specs, out_shape=out_shape,
                          scratch_shapes=list(scratch_shapes), input_output_aliases=aliases or {},
                          compiler_params=pltpu.CompilerParams(**params))


def _pc_prefetch(body, *, name, grid_spec, out_shape, dims):
    return pl.pallas_call(body, name=name, grid_spec=grid_spec, out_shape=out_shape,
                          compiler_params=pltpu.CompilerParams(vmem_limit_bytes=VMEM_LIMIT_BYTES,
                                                               dimension_semantics=dims))


def _pc_comm(body, *, name, in_specs, out_specs, out_shape, scratch_shapes):
    return pl.pallas_call(body, name=name, in_specs=in_specs, out_specs=out_specs, out_shape=out_shape,
                          scratch_shapes=list(scratch_shapes),
                          compiler_params=pltpu.CompilerParams(vmem_limit_bytes=VMEM_LIMIT_BYTES))


_DIMS = {"nn": (((1,), (0,)), ((), ())), "nt": (((1,), (1,)), ((), ())), "tn": (((0,), (0,)), ((), ()))}


def _matmul(a, b, mode, *, name, tm=512, tn=512, tk=512, out_dtype=F32, res=None, slabs=None):
    if mode == "nn":
        (M, K), (K2, N) = a.shape, b.shape
    elif mode == "nt":
        (M, K), (N, K2) = a.shape, b.shape
    else:
        (K, M), (K2, N) = a.shape, b.shape
    assert K == K2, (a.shape, b.shape, mode)
    tm, tn, tk = min(tm, M), min(tn, N), min(tk, K)
    assert M % tm == 0 and N % tn == 0 and K % tk == 0, (M, N, K, tm, tn, tk)
    nk = K // tk
    if mode == "tn":
        a_spec = pl.BlockSpec((tk, tm), lambda i, j, k: (k, i))
    else:
        a_spec = pl.BlockSpec((tm, tk), lambda i, j, k: (i, k))
    if mode == "nt":
        b_spec = pl.BlockSpec((tn, tk), lambda i, j, k: (j, k))
    else:
        b_spec = pl.BlockSpec((tk, tn), lambda i, j, k: (k, j))
    o_spec = pl.BlockSpec((tm, tn), lambda i, j, k: (i, j))
    has_res = res is not None
    dn = _DIMS[mode]

    def body(*refs):
        if has_res:
            a_ref, b_ref, r_ref, o_ref, acc = refs
        else:
            a_ref, b_ref, o_ref, acc = refs
        k = pl.program_id(2)
        p = lax.dot_general(a_ref[...].astype(BF16), b_ref[...].astype(BF16), dn, preferred_element_type=F32)

        def finish(total):
            if has_res:
                total = total + r_ref[...].astype(F32)
            o_ref[...] = total.astype(out_dtype)

        if nk == 1:
            finish(p)
        else:
            @pl.when(k == 0)
            def _():
                acc[...] = p

            @pl.when(k > 0)
            def _():
                acc[...] += p

            @pl.when(k == nk - 1)
            def _():
                finish(acc[...])

    in_specs = [a_spec, b_spec] + ([o_spec] if has_res else [])
    args = (a, b) + ((res,) if has_res else ())
    out_shape = jax.ShapeDtypeStruct((M, N), out_dtype)
    aliases = None
    if slabs is not None:
        n_slabs, first, into = slabs
        assert not has_res and tm == M
        o_spec = pl.BlockSpec((None, tm, tn), lambda i, j, k: (j + first, i, 0))
        out_shape = jax.ShapeDtypeStruct((n_slabs, M, tn), out_dtype)
        if into is not None:
            in_specs.append(pl.BlockSpec(memory_space=pl.ANY))
            args = args + (into,)
            aliases = {2: 0}
            inner = body

            def body(a_ref, b_ref, into_ref, o_ref, acc):
                inner(a_ref, b_ref, o_ref, acc)
    return _pc(body, name=name, grid=(M // tm, N // tn, nk), in_specs=in_specs, out_specs=o_spec,
               out_shape=out_shape, scratch_shapes=[pltpu.VMEM((tm, tn), F32)],
               dims=("parallel", "parallel", "arbitrary"), aliases=aliases)(*args)


def _offsets(parts, own_width_aligned):
    offs, at = [], 0
    for p in parts:
        assert at % (p.shape[1] if own_width_aligned else LANES) == 0, (at, p.shape)
        offs.append(at)
        at += p.shape[1]
    return offs, at


def _matmul_rows_parts(parts, w, mode, *, name, tm=512, res=None):
    M = parts[0].shape[0]
    offs, K = _offsets(parts, True)
    tm = min(tm, M)
    N = w.shape[1] if mode == "nn" else w.shape[0]
    assert (w.shape[0] if mode == "nn" else w.shape[1]) == K
    has_res = res is not None
    n = len(parts)

    def body(*refs):
        o_ref = refs[-1]
        total = None
        for s in range(n):
            p = lax.dot_general(refs[s][...].astype(BF16), refs[n + s][...].astype(BF16), _DIMS[mode],
                                preferred_element_type=F32)
            total = p if total is None else total + p
        if has_res:
            total = total + refs[2 * n][...]
        o_ref[...] = total

    in_specs = [pl.BlockSpec((tm, p.shape[1]), lambda i: (i, 0)) for p in parts]
    for p, off in zip(parts, offs):
        blk = off // p.shape[1]
        if mode == "nn":
            in_specs.append(pl.BlockSpec((p.shape[1], N), lambda i, blk=blk: (blk, 0)))
        else:
            in_specs.append(pl.BlockSpec((N, p.shape[1]), lambda i, blk=blk: (0, blk)))
    o_spec = pl.BlockSpec((tm, N), lambda i: (i, 0))
    args = tuple(parts) + (w,) * n + ((res,) if has_res else ())
    return _pc(body, name=name, grid=(M // tm,), in_specs=in_specs + ([o_spec] if has_res else []), out_specs=o_spec,
               out_shape=jax.ShapeDtypeStruct((M, N), F32), dims=("parallel",))(*args)


def _matmul_tn_parts(a, b, *, name, tk=1024):
    a_parts = list(a) if isinstance(a, (list, tuple)) else [a]
    b_parts = list(b) if isinstance(b, (list, tuple)) else [b]
    assert len(a_parts) == 1 or len(b_parts) == 1
    a_offs, M = _offsets(a_parts, False)
    b_offs, N = _offsets(b_parts, False)
    K = a_parts[0].shape[0]
    tk = min(tk, K)
    na, nb = len(a_parts), len(b_parts)

    def body(*refs):
        o_ref = refs[-1]
        first = pl.program_id(0) == 0
        for s in range(na):
            for t in range(nb):
                p = lax.dot_general(refs[s][...].astype(BF16), refs[na + t][...].astype(BF16), _DIMS["tn"],
                                    preferred_element_type=F32)
                rows = slice(a_offs[s], a_offs[s] + a_parts[s].shape[1])
                cols = slice(b_offs[t], b_offs[t] + b_parts[t].shape[1])

                @pl.when(first)
                def _(p=p, rows=rows, cols=cols):
                    o_ref[rows, cols] = p

                @pl.when(jnp.logical_not(first))
                def _(p=p, rows=rows, cols=cols):
                    o_ref[rows, cols] += p

    in_specs = [pl.BlockSpec((tk, p.shape[1]), lambda k: (k, 0)) for p in a_parts + b_parts]
    return _pc(body, name=name, grid=(K // tk,), in_specs=in_specs, out_specs=pl.BlockSpec((M, N), lambda k: (0, 0)),
               out_shape=jax.ShapeDtypeStruct((M, N), F32), dims=("arbitrary",))(*a_parts, *b_parts)


def _rmsnorm_fwd(x, w, *, name, tm=512):
    L, D = x.shape
    tm = min(tm, L)

    def body(x_ref, w_ref, h_ref):
        xv = x_ref[...]
        r = lax.rsqrt(jnp.mean(xv * xv, axis=-1, keepdims=True) + NORM_EPS)
        h_ref[...] = (xv * r * w_ref[...]).astype(BF16)

    return _pc(body, name=name, grid=(L // tm,),
               in_specs=[pl.BlockSpec((tm, D), lambda i: (i, 0)), pl.BlockSpec((1, D), lambda i: (0, 0))],
               out_specs=pl.BlockSpec((tm, D), lambda i: (i, 0)), out_shape=jax.ShapeDtypeStruct((L, D), BF16),
               dims=("parallel",))(x, w.reshape(1, D))


def _rmsnorm_bwd(dh, x, w, dres, *, name, tm=512):
    L, D = x.shape
    tm = min(tm, L)

    def body(dh_ref, x_ref, w_ref, dres_ref, dx_ref, dw_ref):
        xv = x_ref[...]
        r = lax.rsqrt(jnp.mean(xv * xv, axis=-1, keepdims=True) + NORM_EPS)
        xhat = xv * r
        dhv = dh_ref[...]
        g = dhv * w_ref[...]
        dx_ref[...] = dres_ref[...] + r * (g - xhat * jnp.mean(g * xhat, axis=-1, keepdims=True))
        part = jnp.sum(dhv * xhat, axis=0, keepdims=True)

        @pl.when(pl.program_id(0) == 0)
        def _():
            dw_ref[...] = part

        @pl.when(pl.program_id(0) > 0)
        def _():
            dw_ref[...] += part

    row = pl.BlockSpec((tm, D), lambda i: (i, 0))
    vec = pl.BlockSpec((1, D), lambda i: (0, 0))
    return _pc(body, name=name, grid=(L // tm,), in_specs=[row, row, vec, row], out_specs=[row, vec],
               out_shape=[jax.ShapeDtypeStruct((L, D), F32), jax.ShapeDtypeStruct((1, D), F32)],
               dims=("arbitrary",))(dh, x, w.reshape(1, D), dres)


def _final_loss(x, w, tgt, *, name, tm=512):
    L, D = x.shape
    tm = min(tm, L)

    def body(x_ref, w_ref, t_ref, loss_ref, dx_ref, dw_ref):
        xv = x_ref[...]
        r = lax.rsqrt(jnp.mean(xv * xv, axis=-1, keepdims=True) + NORM_EPS)
        xhat = xv * r
        e = xhat * w_ref[...] - t_ref[...]
        lpart = jnp.broadcast_to(0.5 * jnp.sum(jnp.mean(e * e, axis=-1, keepdims=True), axis=0, keepdims=True),
                                 (1, LANES))
        dy = e * (1.0 / D)
        g = dy * w_ref[...]
        dx_ref[...] = r * (g - xhat * jnp.mean(g * xhat, axis=-1, keepdims=True))
        part = jnp.sum(dy * xhat, axis=0, keepdims=True)

        @pl.when(pl.program_id(0) == 0)
        def _():
            dw_ref[...] = part
            loss_ref[...] = lpart

        @pl.when(pl.program_id(0) > 0)
        def _():
            dw_ref[...] += part
            loss_ref[...] += lpart

    row = pl.BlockSpec((tm, D), lambda i: (i, 0))
    vec = pl.BlockSpec((1, D), lambda i: (0, 0))
    lsp = pl.BlockSpec((1, LANES), lambda i: (0, 0))
    return _pc(body, name=name, grid=(L // tm,), in_specs=[row, vec, row], out_specs=[lsp, row, vec],
               out_shape=[jax.ShapeDtypeStruct((1, LANES), F32), jax.ShapeDtypeStruct((L, D), F32),
                          jax.ShapeDtypeStruct((1, D), F32)],
               dims=("arbitrary",))(x, w.reshape(1, D), tgt)


def _silu(x):
    return x * jax.nn.sigmoid(x)


def _conv_pointwise_fwd(xs, ws, es, pre, post, outs, *, tc, tm, name):
    L = xs[0][0].shape[0]
    tm = min(tm, L)
    ncol = outs[0][0] // tc
    nrow = L // tm
    hb = tm // HALO
    nx, nw, ne, no = len(xs), len(ws), len(es), len(outs)
    K = ws[0][0].shape[0]

    slab = min(CONV_SLAB, tm)
    win = slab + HALO
    assert tm % slab == 0 and K - 1 <= HALO

    def body(*refs):
        i = pl.program_id(1)
        first = (i > 0).astype(F32)
        n_in = 2 * nx + nw + ne
        o_refs = refs[n_in:n_in + no]
        x_pads = refs[n_in + no:]
        for n in range(nx):
            x_pads[n][0:HALO, :] = refs[2 * n + 1][...] * first
            x_pads[n][HALO:, :] = refs[2 * n][...]
        wv = [refs[2 * nx + n][...] for n in range(nw)]
        e_refs = refs[2 * nx + nw:n_in]

        @pl.loop(0, tm // slab)
        def _(t):
            r0 = pl.multiple_of(t * slab, HALO)
            ps = pre(*[x_pads[n][pl.ds(r0, win), :] for n in range(nx)])
            us = []
            for n in range(nw):
                u = None
                for k in range(K):
                    term = wv[n][k:k + 1, :] * (ps[n] if k == K - 1 else pltpu.roll(ps[n], K - 1 - k, 0))
                    u = term if u is None else u + term
                us.append(u[HALO:])
            rows = pl.ds(r0, slab)
            for o_ref, val in zip(o_refs, post(us, [e[rows, :] for e in e_refs])):
                o_ref[rows, :] = val.astype(o_ref.dtype)

    in_specs, args = [], []
    for arr, c0 in xs:
        off = c0 // tc
        in_specs.append(pl.BlockSpec((tm, tc), lambda j, i, off=off: (i, j + off)))
        in_specs.append(pl.BlockSpec((HALO, tc), lambda j, i, off=off: (jnp.maximum(i * hb - 1, 0), j + off)))
        args += [arr, arr]
    for arr, c0 in ws:
        off = c0 // tc
        in_specs.append(pl.BlockSpec((K, tc), lambda j, i, off=off: (0, j + off)))
        args.append(arr)
    for arr, c0 in es:
        off = c0 // tc
        in_specs.append(pl.BlockSpec((tm, tc), lambda j, i, off=off: (i, j + off)))
        args.append(arr)
    out_specs = [pl.BlockSpec((tm, tc), lambda j, i: (i, j)) for _ in range(no)]
    out_shape = [jax.ShapeDtypeStruct((L, c), dt) for c, dt in outs]
    return _pc(body, name=name, grid=(ncol, nrow), in_specs=in_specs, out_specs=out_specs, out_shape=out_shape,
               scratch_shapes=[pltpu.VMEM((tm + HALO, tc), F32)] * nx, dims=("parallel", "parallel"))(*args)


def _conv_pointwise_bwd(xs, ws, es, dys, pre, post, width, *, tc, tm, name, out_dtype=BF16):
    L = xs[0][0].shape[0]
    tm = min(tm, L)
    ncol = width // tc
    nrow = L // tm
    hb = tm // HALO
    nx, nw, ne, ny = len(xs), len(ws), len(es), len(dys)
    K = ws[0][0].shape[0]

    slab = min(CONV_SLAB, tm)
    win = slab + 2 * HALO
    assert tm % slab == 0 and K - 1 <= HALO

    def body(*refs):
        i = pl.program_id(1)
        first = (i > 0).astype(F32)
        more = (i < nrow - 1).astype(F32)
        n_in = 3 * nx + nw + 2 * ne + 2 * ny
        n_out = nx + ne + nw
        dx_refs = refs[n_in:n_in + nx]
        de_refs = refs[n_in + nx:n_in + nx + ne]
        dw_refs = refs[n_in + nx + ne:n_in + n_out]
        pads = refs[n_in + n_out:]
        x_pads, e_pads, dy_pads = pads[:nx], pads[nx:nx + ne], pads[nx + ne:]
        pos = 0
        for n in range(nx):
            x_pads[n][0:HALO, :] = refs[pos + 1][...] * first
            x_pads[n][HALO:HALO + tm, :] = refs[pos][...]
            x_pads[n][HALO + tm:, :] = refs[pos + 2][...]
            pos += 3
        wv = [refs[pos + n][...] for n in range(nw)]
        pos += nw
        for n in range(ne):
            e_pads[n][0:HALO, :] = jnp.zeros((HALO, tc), F32)
            e_pads[n][HALO:HALO + tm, :] = refs[pos][...]
            e_pads[n][HALO + tm:, :] = refs[pos + 1][...]
            pos += 2
        for n in range(ny):
            dy_pads[n][0:HALO, :] = jnp.zeros((HALO, tc), F32)
            dy_pads[n][HALO:HALO + tm, :] = refs[pos][...].astype(F32)
            dy_pads[n][HALO + tm:, :] = refs[pos + 1][...].astype(F32) * more
            pos += 2

        def one_slab(t, dw_acc):
            r0 = pl.multiple_of(t * slab, HALO)
            xw = [x_pads[n][pl.ds(r0, win), :] for n in range(nx)]
            ew = [e_pads[n][pl.ds(r0, win), :] for n in range(ne)]
            dyw = [dy_pads[n][pl.ds(r0, win), :] for n in range(ny)]
            ps, pre_vjp = jax.vjp(lambda *x_: pre(*x_), *xw)
            shifted = [[p if k == K - 1 else pltpu.roll(p, K - 1 - k, 0) for k in range(K)] for p in ps]
            us = []
            for n in range(nw):
                u = None
                for k in range(K):
                    term = wv[n][k:k + 1, :] * shifted[n][k]
                    u = term if u is None else u + term
                us.append(u)
            _, post_vjp = jax.vjp(lambda u_, e_: post(u_, e_), us, ew)
            dus, des = post_vjp(dyw)
            dps, dw_new = [], []
            for n in range(nw):
                dp = None
                for k in range(K):
                    term = wv[n][k:k + 1, :] * (dus[n] if k == K - 1 else pltpu.roll(dus[n], win - (K - 1 - k), 0))
                    dp = term if dp is None else dp + term
                dps.append(dp)
                inner = dus[n][HALO:HALO + slab]
                dw_new.append([dw_acc[n][k] + jnp.sum(inner * shifted[n][k][HALO:HALO + slab], axis=0, keepdims=True)
                               for k in range(K)])
            dxs = pre_vjp(dps)
            rows = pl.ds(r0, slab)
            for r, v in zip(dx_refs, dxs):
                r[rows, :] = v[HALO:HALO + slab].astype(out_dtype)
            for r, v in zip(de_refs, des):
                r[rows, :] = v[HALO:HALO + slab].astype(out_dtype)
            return dw_new

        zero = [[jnp.zeros((1, tc), F32) for _ in range(K)] for _ in range(nw)]
        dw_tile = lax.fori_loop(0, tm // slab, one_slab, zero)
        for n in range(nw):
            for k in range(K):
                @pl.when(i == 0)
                def _(n=n, k=k):
                    dw_refs[n][k:k + 1, :] = dw_tile[n][k]

                @pl.when(i > 0)
                def _(n=n, k=k):
                    dw_refs[n][k:k + 1, :] += dw_tile[n][k]

    in_specs, args = [], []

    def add_rows(arr, c0, prev, nxt):
        off = c0 // tc
        in_specs.append(pl.BlockSpec((tm, tc), lambda j, i, off=off: (i, j + off)))
        args.append(arr)
        if prev:
            in_specs.append(pl.BlockSpec((HALO, tc), lambda j, i, off=off: (jnp.maximum(i * hb - 1, 0), j + off)))
            args.append(arr)
        if nxt:
            last = L // HALO - 1
            in_specs.append(pl.BlockSpec((HALO, tc), lambda j, i, off=off: (jnp.minimum((i + 1) * hb, last), j + off)))
            args.append(arr)

    for arr, c0 in xs:
        add_rows(arr, c0, True, True)
    for arr, c0 in ws:
        off = c0 // tc
        in_specs.append(pl.BlockSpec((K, tc), lambda j, i, off=off: (0, j + off)))
        args.append(arr)
    for arr, c0 in es:
        add_rows(arr, c0, False, True)
    for arr, c0 in dys:
        add_rows(arr, c0, False, True)
    tile = pl.BlockSpec((tm, tc), lambda j, i: (i, j))
    wtile = pl.BlockSpec((K, tc), lambda j, i: (0, j))
    out_specs = [tile] * (nx + ne) + [wtile] * nw
    out_shape = [jax.ShapeDtypeStruct((L, width), out_dtype)] * (nx + ne) + \
        [jax.ShapeDtypeStruct((K, width), F32)] * nw
    res = _pc(body, name=name, grid=(ncol, nrow), in_specs=in_specs, out_specs=out_specs, out_shape=out_shape,
              scratch_shapes=[pltpu.VMEM((tm + 2 * HALO, tc), F32)] * (nx + ne + ny),
              dims=("parallel", "arbitrary"))(*args)
    return res[:nx], res[nx:nx + ne], res[nx + ne:]


def _pre_identity(*x):
    return list(x)


def _pre_product(c, h):
    return [c * h]


def _post_silu(us, es):
    return [_silu(us[0])]


def _post_gate_mul(us, es):
    return [es[0] * us[0]]


def _post_swiglu(us, es):
    return [_silu(us[0]) * us[1]]


def _make_dot(passes):
    def raw(a, b, dn):
        a_hi = a.astype(BF16)
        b_hi = b.astype(BF16)
        out = lax.dot_general(a_hi, b_hi, dn, preferred_element_type=F32)
        if passes == 3:
            a_lo = (a - a_hi.astype(F32)).astype(BF16)
            b_lo = (b - b_hi.astype(F32)).astype(BF16)
            out = out + lax.dot_general(a_hi, b_lo, dn, preferred_element_type=F32)
            out = out + lax.dot_general(a_lo, b_hi, dn, preferred_element_type=F32)
        return out

    @jax.custom_vjp
    def nn(a, b):
        return raw(a, b, _DIMS["nn"])

    @jax.custom_vjp
    def nt(a, b):
        return raw(a, b, _DIMS["nt"])

    @jax.custom_vjp
    def tn(a, b):
        return raw(a, b, _DIMS["tn"])

    nn.defvjp(lambda a, b: (nn(a, b), (a, b)), lambda r, g: (nt(g, r[1]), tn(r[0], g)))
    nt.defvjp(lambda a, b: (nt(a, b), (a, b)), lambda r, g: (nn(g, r[1]), tn(g, r[0])))
    tn.defvjp(lambda a, b: (tn(a, b), (a, b)), lambda r, g: (nt(r[1], g), nn(r[0], g)))
    return nn, nt, tn


_NN1, _NT1, _TN1 = _make_dot(1)
_NN3, _NT3, _TN3 = _make_dot(3)


def _l2norm(x):
    return x * lax.rsqrt(jnp.sum(x * x, axis=-1, keepdims=True) + NORM_EPS)


def _split_bf16(x):
    hi = x.astype(BF16)
    return hi, (x - hi.astype(F32)).astype(BF16)


def _products_with(lhs_list, rhs):
    n, rows = len(lhs_list), lhs_list[0].shape[0]
    r_hi, r_lo = _split_bf16(rhs)
    halves = [_split_bf16(l) for l in lhs_list]
    his = [h for h, _ in halves]
    o_hi = jnp.dot(jnp.concatenate(his + [lo for _, lo in halves], axis=0), r_hi, preferred_element_type=F32)
    o_lo = jnp.dot(jnp.concatenate(his, axis=0) if n > 1 else his[0], r_lo, preferred_element_type=F32)
    return [o_hi[i * rows:(i + 1) * rows] + o_hi[(n + i) * rows:(n + i + 1) * rows] + o_lo[i * rows:(i + 1) * rows]
            for i in range(n)]


def _unit_lower_inverse_raw(a_list):
    C = a_list[0].shape[0]
    ii = lax.broadcasted_iota(jnp.int32, (C, C), 0)
    jj = lax.broadcasted_iota(jnp.int32, (C, C), 1)
    eye = jnp.where(ii == jj, 1.0, 0.0)
    qs = [-a for a in a_list]
    ts = [eye + q for q in qs]
    qs = [_products_with([q], q)[0] for q in qs]
    n = 4
    while n <= C:
        last = n == C
        prods = [_products_with([t] if last else [t, q], q) for t, q in zip(ts, qs)]
        ts = [t + pr[0] for t, pr in zip(ts, prods)]
        if not last:
            qs = [pr[1] for pr in prods]
        n *= 2
    return ts


@jax.custom_vjp
def _unit_lower_inverse(a_list):
    return _unit_lower_inverse_raw(a_list)


def _unit_lower_inverse_fwd(a_list):
    ts = _unit_lower_inverse_raw(a_list)
    return ts, ts


def _unit_lower_inverse_bwd(ts, gs):
    xs = [_TN3(t, g) for t, g in zip(ts, gs)]
    return ([-_NT3(x, t) for x, t in zip(xs, ts)],)


_unit_lower_inverse.defvjp(_unit_lower_inverse_fwd, _unit_lower_inverse_bwd)


def _gdn_prep(units):
    C, Dh = units[0][0].shape
    ii = lax.broadcasted_iota(jnp.int32, (C, C), 0)
    jj = lax.broadcasted_iota(jnp.int32, (C, C), 1)
    lane = lax.broadcasted_iota(jnp.int32, (1, C), 1)
    causal = ii >= jj
    strict = ii > jj
    qs = [_l2norm(un[0]) * (Dh ** -0.5) for un in units]
    ks = [_l2norm(un[1]) for un in units]
    betas = [jax.nn.sigmoid(un[4]) for un in units]
    gs = [-jnp.exp(un[5]) * jax.nn.softplus(un[3] + un[6]) for un in units]
    gc_rows = [jnp.sum(jnp.where(ii <= jj, g, 0.0), axis=0, keepdims=True) for g in gs]
    gc_cols = [jnp.sum(jnp.where(ii == jj, r, 0.0), axis=1, keepdims=True) for r in gc_rows]
    decays = [jnp.where(causal, jnp.exp(jnp.where(causal, c - r, 0.0)), 0.0) for c, r in zip(gc_cols, gc_rows)]
    kbs = [k * b for k, b in zip(ks, betas)]
    kks = [_NT1(kb, k) for kb, k in zip(kbs, ks)]
    qks = [_NT1(q, k) for q, k in zip(qs, ks)]
    ts = _unit_lower_inverse([jnp.where(strict, kk * d, 0.0) for kk, d in zip(kks, decays)])
    eg_cols = [jnp.exp(c) for c in gc_cols]
    uws = [_NN3(t, jnp.concatenate([un[2] * b, kb * e], axis=1))
           for t, un, b, kb, e in zip(ts, units, betas, kbs, eg_cols)]
    out = []
    for q, k, qk, d, uw, e, r, c in zip(qs, ks, qks, decays, uws, eg_cols, gc_rows, gc_cols):
        g_last = jnp.sum(jnp.where(lane == C - 1, r, 0.0), axis=1, keepdims=True)
        out.append((q * e, k * jnp.exp(g_last - c), uw[:, :Dh], uw[:, Dh:], jnp.where(causal, qk * d, 0.0),
                    jnp.broadcast_to(jnp.exp(g_last), (1, Dh))))
    return out


def _gdn_step(units):
    v_news = [un[3] - _NN1(un[4], un[0]) for un in units]
    o_state = [_NN1(un[1], un[0]) for un in units]
    o_intra = [_NN1(un[5], vn) for un, vn in zip(units, v_news)]
    s_adds = [_TN1(un[2], vn) for un, vn in zip(units, v_news)]
    out = []
    for un, a, b, s_add in zip(units, o_state, o_intra, s_adds):
        o = a + b
        y = o * lax.rsqrt(jnp.mean(o * o, axis=-1, keepdims=True) + NORM_EPS) * un[8] * _silu(un[7])
        out.append((y, un[0] * un[6] + s_add))
    return out


def _gdn_prep_fwd(qkv, gab, gab_col, a_log, dt_bias, *, name, chunks=4):
    L = qkv.shape[0]
    H, C = GDN_HEADS, GDN_CHUNK
    W = qkv.shape[1] // 3
    Dh = W // H
    N = L // C
    chunks = min(chunks, N)
    R = chunks * C
    gab_off = gab_col // LANES

    def body(q_ref, k_ref, v_ref, gab_ref, al_ref, dt_ref, qe_ref, ke_ref, u_ref, w_ref, at_ref, eg_ref):
        where = [(cc, h) for cc in range(chunks) for h in range(H)]
        units = []
        for cc, h in where:
            rows, sl = slice(cc * C, (cc + 1) * C), slice(h * Dh, (h + 1) * Dh)
            units.append((q_ref[rows, sl], k_ref[rows, sl], v_ref[rows, sl], gab_ref[rows, h:h + 1],
                          gab_ref[rows, H + h:H + h + 1], al_ref[h], dt_ref[h]))
        for (cc, h), (qe, ke, u, w, attn, eg) in zip(where, _gdn_prep(units)):
            rows, sl = slice(cc * C, (cc + 1) * C), slice(h * Dh, (h + 1) * Dh)
            qe_ref[rows, sl] = qe
            ke_ref[rows, sl] = ke
            u_ref[rows, sl] = u
            w_ref[rows, sl] = w
            at_ref[h, rows, :] = attn
            eg_ref[cc, h:h + 1, :] = eg

    col = lambda c: pl.BlockSpec((R, W), lambda n, c=c: (n, c))
    tok = pl.BlockSpec((R, LANES), lambda n: (n, gab_off))
    par = pl.BlockSpec((H, 1, 1), lambda n: (0, 0, 0))
    wide = pl.BlockSpec((R, W), lambda n: (n, 0))
    return _pc(body, name=name, grid=(N // chunks,), in_specs=[col(0), col(1), col(2), tok, par, par],
               out_specs=[wide, wide, wide, wide, pl.BlockSpec((H, R, C), lambda n: (0, n, 0)),
                          pl.BlockSpec((chunks, H, Dh), lambda n: (n, 0, 0))],
               out_shape=[jax.ShapeDtypeStruct((L, W), F32)] * 4 + [jax.ShapeDtypeStruct((H, L, C), F32),
                                                                   jax.ShapeDtypeStruct((N, H, Dh), F32)],
               dims=("parallel",))(qkv, qkv, qkv, gab, a_log, dt_bias)


def _gdn_prep_bwd(qkv, gab, gab_col, a_log, dt_bias, dqe, dke, du, dw, dattn, deg, gab_width, *, name, chunks=4):
    L = qkv.shape[0]
    H, C = GDN_HEADS, GDN_CHUNK
    W = qkv.shape[1] // 3
    Dh = W // H
    N = L // C
    chunks = min(chunks, N)
    R = chunks * C
    gab_off = gab_col // LANES

    def body(q_ref, k_ref, v_ref, gab_ref, al_ref, dt_ref, dqe_ref, dke_ref, du_ref, dw_ref, dat_ref, deg_ref,
             dqkv_ref, dgab_ref, dal_ref, ddt_ref):
        first = pl.program_id(0) == 0
        lane = lax.broadcasted_iota(jnp.int32, (C, gab_width), 1)
        dal_sum, ddt_sum = [None] * H, [None] * H
        where = [(cc, h) for cc in range(chunks) for h in range(H)]
        units, cots = [], []
        for cc, h in where:
            rows, sl = slice(cc * C, (cc + 1) * C), slice(h * Dh, (h + 1) * Dh)
            units.append((q_ref[rows, sl], k_ref[rows, sl], v_ref[rows, sl], gab_ref[rows, h:h + 1],
                          gab_ref[rows, H + h:H + h + 1], al_ref[h], dt_ref[h]))
            cots.append((dqe_ref[rows, sl], dke_ref[rows, sl], du_ref[rows, sl], dw_ref[rows, sl],
                         dat_ref[h, rows, :], deg_ref[cc, h:h + 1, :]))
        _, vjp = jax.vjp(_gdn_prep, units)
        (d_units,) = vjp(cots)
        dgabs = [jnp.zeros((C, gab_width), F32) for _ in range(chunks)]
        for (cc, h), (dq, dk, dv, dga, dgb, dal, ddt) in zip(where, d_units):
            rows = slice(cc * C, (cc + 1) * C)
            dqkv_ref[rows, h * Dh:(h + 1) * Dh] = dq
            dqkv_ref[rows, W + h * Dh:W + (h + 1) * Dh] = dk
            dqkv_ref[rows, 2 * W + h * Dh:2 * W + (h + 1) * Dh] = dv
            dgabs[cc] = dgabs[cc] + jnp.where(lane == h, dga, 0.0) + jnp.where(lane == H + h, dgb, 0.0)
            dal_sum[h] = dal if dal_sum[h] is None else dal_sum[h] + dal
            ddt_sum[h] = ddt if ddt_sum[h] is None else ddt_sum[h] + ddt
        for cc in range(chunks):
            dgab_ref[cc * C:(cc + 1) * C, :] = dgabs[cc].astype(BF16)

        @pl.when(first)
        def _():
            for h in range(H):
                dal_ref[h] = dal_sum[h]
                ddt_ref[h] = ddt_sum[h]

        @pl.when(jnp.logical_not(first))
        def _():
            for h in range(H):
                dal_ref[h] += dal_sum[h]
                ddt_ref[h] += ddt_sum[h]

    col = lambda c: pl.BlockSpec((R, W), lambda n, c=c: (n, c))
    tok = pl.BlockSpec((R, LANES), lambda n: (n, gab_off))
    par = pl.BlockSpec((H, 1, 1), lambda n: (0, 0, 0))
    wide = pl.BlockSpec((R, W), lambda n: (n, 0))
    att = pl.BlockSpec((H, R, C), lambda n: (0, n, 0))
    egs = pl.BlockSpec((chunks, H, Dh), lambda n: (n, 0, 0))
    return _pc(body, name=name, grid=(N // chunks,),
               in_specs=[col(0), col(1), col(2), tok, par, par, wide, wide, wide, wide, att, egs],
               out_specs=[pl.BlockSpec((R, 3 * W), lambda n: (n, 0)), pl.BlockSpec((R, gab_width), lambda n: (n, 0)),
                          par, par],
               out_shape=[jax.ShapeDtypeStruct((L, 3 * W), F32), jax.ShapeDtypeStruct((L, gab_width), BF16)]
               + [jax.ShapeDtypeStruct((H, 1, 1), F32)] * 2,
               dims=("arbitrary",))(qkv, qkv, qkv, gab, a_log, dt_bias, dqe, dke, du, dw, dattn, deg)


def _gdn_scan_fwd(qe, ke, u, w, attn, eg, gz, gz_col, wgn, *, name):
    L, W = qe.shape
    H, C = GDN_HEADS, GDN_CHUNK
    Dh = W // H
    N = L // C
    gz_off = gz_col // W
    cps = min(GDN_SCAN_CHUNKS, N)
    R = cps * C

    def body(qe_ref, ke_ref, u_ref, w_ref, at_ref, eg_ref, gz_ref, wgn_ref, y_ref, st_ref, s_scr):
        @pl.when(pl.program_id(0) == 0)
        def _():
            s_scr[...] = jnp.zeros_like(s_scr)

        S = [s_scr[h] for h in range(H)]
        for cc in range(cps):
            rows = slice(cc * C, (cc + 1) * C)
            units = []
            for h in range(H):
                sl = slice(h * Dh, (h + 1) * Dh)
                st_ref[cc, h] = S[h]
                units.append((S[h], qe_ref[rows, sl], ke_ref[rows, sl], u_ref[rows, sl], w_ref[rows, sl],
                              at_ref[h, rows, :], eg_ref[cc, h:h + 1, :], gz_ref[rows, sl], wgn_ref[...]))
            for h, (y, S_new) in enumerate(_gdn_step(units)):
                y_ref[rows, h * Dh:(h + 1) * Dh] = y.astype(BF16)
                S[h] = S_new
        for h in range(H):
            s_scr[h] = S[h]

    wide = pl.BlockSpec((R, W), lambda n: (n, 0))
    return _pc(body, name=name, grid=(N // cps,),
               in_specs=[wide, wide, wide, wide, pl.BlockSpec((H, R, C), lambda n: (0, n, 0)),
                         pl.BlockSpec((cps, H, Dh), lambda n: (n, 0, 0)),
                         pl.BlockSpec((R, W), lambda n: (n, gz_off)), pl.BlockSpec((1, Dh), lambda n: (0, 0))],
               out_specs=[wide, pl.BlockSpec((cps, H, Dh, Dh), lambda n: (n, 0, 0, 0))],
               out_shape=[jax.ShapeDtypeStruct((L, W), BF16), jax.ShapeDtypeStruct((N, H, Dh, Dh), F32)],
               scratch_shapes=[pltpu.VMEM((H, Dh, Dh), F32)],
               dims=("arbitrary",))(qe, ke, u, w, attn, eg, gz, wgn.reshape(1, Dh))


def _gdn_scan_bwd(qe, ke, u, w, attn, eg, gz, gz_col, wgn, states, dy, dy_col, *, name):
    L, W = qe.shape
    H, C = GDN_HEADS, GDN_CHUNK
    Dh = W // H
    N = L // C
    gz_off = gz_col // W
    dy_off = dy_col // W
    cps = min(GDN_SCAN_CHUNKS, N)
    R = cps * C
    steps = N // cps

    def body(qe_ref, ke_ref, u_ref, w_ref, at_ref, eg_ref, gz_ref, wgn_ref, st_ref, dy_ref,
             dqe_ref, dke_ref, du_ref, dw_ref, dat_ref, deg_ref, dgz_ref, dwgn_ref, ds_scr):
        first = pl.program_id(0) == 0

        @pl.when(first)
        def _():
            ds_scr[...] = jnp.zeros_like(ds_scr)

        dwgn = None
        dS = [ds_scr[h] for h in range(H)]
        for cc in reversed(range(cps)):
            rows = slice(cc * C, (cc + 1) * C)
            units, cots = [], []
            for h in range(H):
                sl = slice(h * Dh, (h + 1) * Dh)
                units.append((st_ref[cc, h], qe_ref[rows, sl], ke_ref[rows, sl], u_ref[rows, sl], w_ref[rows, sl],
                              at_ref[h, rows, :], eg_ref[cc, h:h + 1, :], gz_ref[rows, sl], wgn_ref[...]))
                cots.append((dy_ref[rows, sl].astype(F32), dS[h]))
            _, vjp = jax.vjp(_gdn_step, units)
            (d_units,) = vjp(cots)
            for h, (dS_h, dqe, dke, du, dw, dat, deg, dgz, dwg) in enumerate(d_units):
                sl = slice(h * Dh, (h + 1) * Dh)
                dS[h] = dS_h
                dqe_ref[rows, sl] = dqe
                dke_ref[rows, sl] = dke
                du_ref[rows, sl] = du
                dw_ref[rows, sl] = dw
                dat_ref[h, rows, :] = dat
                deg_ref[cc, h:h + 1, :] = deg
                dgz_ref[rows, sl] = dgz.astype(BF16)
                dwgn = dwg if dwgn is None else dwgn + dwg
        for h in range(H):
            ds_scr[h] = dS[h]

        @pl.when(first)
        def _():
            dwgn_ref[...] = dwgn

        @pl.when(jnp.logical_not(first))
        def _():
            dwgn_ref[...] += dwgn

    rev = lambda n: steps - 1 - n
    wide = pl.BlockSpec((R, W), lambda n: (rev(n), 0))
    att = pl.BlockSpec((H, R, C), lambda n: (0, rev(n), 0))
    egs = pl.BlockSpec((cps, H, Dh), lambda n: (rev(n), 0, 0))
    vec = pl.BlockSpec((1, Dh), lambda n: (0, 0))
    return _pc(body, name=name, grid=(steps,),
               in_specs=[wide, wide, wide, wide, att, egs, pl.BlockSpec((R, W), lambda n: (rev(n), gz_off)), vec,
                         pl.BlockSpec((cps, H, Dh, Dh), lambda n: (rev(n), 0, 0, 0)),
                         pl.BlockSpec((R, W), lambda n: (rev(n), dy_off))],
               out_specs=[wide, wide, wide, wide, att, egs, wide, vec],
               out_shape=[jax.ShapeDtypeStruct((L, W), F32)] * 4 + [jax.ShapeDtypeStruct((H, L, C), F32),
                                                                   jax.ShapeDtypeStruct((N, H, Dh), F32),
                                                                   jax.ShapeDtypeStruct((L, W), BF16),
                                                                   jax.ShapeDtypeStruct((1, Dh), F32)],
               scratch_shapes=[pltpu.VMEM((H, Dh, Dh), F32)],
               dims=("arbitrary",))(qe, ke, u, w, attn, eg, gz, wgn.reshape(1, Dh), states, dy)


def _sb_scores(z, mask):
    sp = jnp.maximum(z, 0.0) + jnp.log(1.0 + jnp.exp(-jnp.abs(z)))
    lom = -sp if mask is None else jnp.where(mask, -sp, 0.0)
    return lom, z - sp


def _sb_alive(c_a, c_b):
    return jnp.maximum(jnp.max(c_a), jnp.max(c_b)) >= SB_DEAD_LOG


def _sb_masks(tq, width, dh):
    rr = lax.broadcasted_iota(jnp.int32, (tq, tq), 0)
    cc = lax.broadcasted_iota(jnp.int32, (tq, tq), 1)
    first_head = lax.broadcasted_iota(jnp.int32, (tq, width), 1) < dh
    return cc < rr, jnp.where(rr > cc, 1.0, 0.0).astype(BF16), first_head


def _sb_fwd(qkv, *, name, tq=256):
    L = qkv.shape[0]
    H = SB_HEADS
    width = 2 * (qkv.shape[1] // 3 // H)
    dh = width // 2
    npair = H // 2
    tq = min(tq, L)
    nq = L // tq

    def body(q_ref, k_ref, v_ref, o_ref):
        i = pl.program_id(1)
        diag, tri, first_head = _sb_masks(tq, width, dh)
        qp = q_ref[...]
        zero = jnp.zeros_like(qp)
        qs = (jnp.where(first_head, qp, zero), jnp.where(first_head, zero, qp))

        def blocks(js, carry, masks):
            units = [(b, hd) for b in range(len(js)) for hd in range(2)]
            starts = [pl.multiple_of(j * tq, tq) for j in js]
            ks = [k_ref[pl.ds(st, tq), :] for st in starts]
            vs = [v_ref[pl.ds(st, tq), :] for st in starts]
            zs = {(b, hd): lax.dot_general(qs[hd], ks[b], _DIMS["nt"], preferred_element_type=F32)
                  for b, hd in units}
            scores = {(b, hd): _sb_scores(zs[(b, hd)], masks[b]) for b, hd in units}
            later = {un: jnp.dot(scores[un][0].astype(BF16), tri, preferred_element_type=F32) for un in units}
            cs = [carry[hd][0] for hd in range(2)]
            accs = [carry[hd][1] for hd in range(2)]
            for b, hd in units:
                lom, lb = scores[(b, hd)]
                a = jnp.exp(lb + (cs[hd] + later[(b, hd)]))
                if masks[b] is not None:
                    a = jnp.where(masks[b], a, 0.0)
                accs[hd] = accs[hd] + jnp.dot(a.astype(BF16), vs[b], preferred_element_type=F32)
                cs[hd] = cs[hd] + jnp.sum(lom, axis=1, keepdims=True)
            return tuple((cs[hd], accs[hd]) for hd in range(2))

        init = tuple((jnp.zeros((tq, 1), F32), jnp.zeros((tq, width), F32)) for _ in range(2))
        carry = lax.cond(i > 0, lambda: blocks([i, i - 1], init, [diag, None]), lambda: blocks([i], init, [diag]))
        j_end, carry = lax.while_loop(lambda st: jnp.logical_and(st[0] >= 0, _sb_alive(st[1][0][0], st[1][1][0])),
                                      lambda st: (st[0] - 1, blocks([st[0]], st[1], [None])), (i - 2, carry))
        o_ref[...] = jnp.where(first_head, carry[0][1], carry[1][1]).astype(BF16)

    return _pc(body, name=name, grid=(npair, nq),
               in_specs=[pl.BlockSpec((tq, width), lambda p, i: (i, p)),
                         pl.BlockSpec((L, width), lambda p, i: (0, npair + p)),
                         pl.BlockSpec((L, width), lambda p, i: (0, 2 * npair + p))],
               out_specs=pl.BlockSpec((tq, width), lambda p, i: (i, p)),
               out_shape=jax.ShapeDtypeStruct((L, npair * width), BF16),
               dims=("parallel", "parallel"))(qkv, qkv, qkv)


def _sb_bwd(qkv, do, do_col, scale, *, name, tq=256):
    L = qkv.shape[0]
    H = SB_HEADS
    width = 2 * (qkv.shape[1] // 3 // H)
    dh = width // 2
    npair = H // 2
    tq = min(tq, L)
    nq = L // tq
    do_off = do_col // width

    def body(q_ref, k_ref, v_ref, do_ref, dq_ref, dk_ref, dv_ref):
        i = pl.program_id(1)

        @pl.when(i == 0)
        def _():
            dk_ref[...] = jnp.zeros_like(dk_ref)
            dv_ref[...] = jnp.zeros_like(dv_ref)

        diag, tri_later, first_head = _sb_masks(tq, width, dh)
        rr = lax.broadcasted_iota(jnp.int32, (tq, tq), 0)
        cc = lax.broadcasted_iota(jnp.int32, (tq, tq), 1)
        tri_before = jnp.where(rr < cc, 1.0, 0.0).astype(BF16)
        qp = q_ref[...]
        dop = do_ref[...].astype(BF16)
        zero = jnp.zeros_like(qp)
        qs = (jnp.where(first_head, qp, zero), jnp.where(first_head, zero, qp))
        dos = (jnp.where(first_head, dop, zero), jnp.where(first_head, zero, dop))
        ctots = []

        def blocks(js, carry, masks):
            nb = len(js)
            units = [(b, hd) for b in range(nb) for hd in range(2)]
            starts = [pl.multiple_of(j * tq, tq) for j in js]
            ks = [k_ref[pl.ds(st, tq), :] for st in starts]
            vs = [v_ref[pl.ds(st, tq), :] for st in starts]
            zs = {(b, hd): lax.dot_general(qs[hd], ks[b], _DIMS["nt"], preferred_element_type=F32)
                  for b, hd in units}
            das = {(b, hd): lax.dot_general(dos[hd], vs[b], _DIMS["nt"], preferred_element_type=F32)
                   for b, hd in units}
            scores = {(b, hd): _sb_scores(zs[(b, hd)], masks[b]) for b, hd in units}
            later = {un: jnp.dot(scores[un][0].astype(BF16), tri_later, preferred_element_type=F32) for un in units}
            pcs = [carry[hd][0] for hd in range(2)]
            avals = {}
            for b, hd in units:
                pcs[hd] = pcs[hd] + jnp.sum(scores[(b, hd)][0], axis=1, keepdims=True)
                a = jnp.exp(scores[(b, hd)][1] + ((ctots[hd] - pcs[hd]) + later[(b, hd)]))
                avals[(b, hd)] = a if masks[b] is None else jnp.where(masks[b], a, 0.0)
            gs = {un: das[un] * avals[un] for un in units}
            before = {un: jnp.dot(gs[un].astype(BF16), tri_before, preferred_element_type=F32) for un in units}
            pgs = [carry[hd][1] for hd in range(2)]
            dzs = {}
            for b, hd in units:
                sig = jnp.exp(scores[(b, hd)][1])
                dz = gs[(b, hd)] * (1.0 - sig) - (pgs[hd] + before[(b, hd)]) * sig
                dzs[(b, hd)] = (dz if masks[b] is None else jnp.where(masks[b], dz, 0.0)).astype(BF16)
                pgs[hd] = pgs[hd] + jnp.sum(gs[(b, hd)], axis=1, keepdims=True)
            dqs = [carry[hd][2] for hd in range(2)]
            for b, hd in units:
                dqs[hd] = dqs[hd] + jnp.dot(dzs[(b, hd)], ks[b], preferred_element_type=F32)
            for b in range(nb):
                dk_ref[pl.ds(starts[b], tq), :] += sum(
                    lax.dot_general(dzs[(b, hd)], qs[hd], _DIMS["tn"], preferred_element_type=F32) for hd in range(2))
                dv_ref[pl.ds(starts[b], tq), :] += sum(
                    lax.dot_general(avals[(b, hd)].astype(BF16), dos[hd], _DIMS["tn"], preferred_element_type=F32)
                    for hd in range(2))
            return tuple((pcs[hd], pgs[hd], dqs[hd]) for hd in range(2))

        def row_sums(j, mask):
            kj = k_ref[pl.ds(pl.multiple_of(j * tq, tq), tq), :]
            return tuple(jnp.sum(_sb_scores(lax.dot_general(qs[hd], kj, _DIMS["nt"], preferred_element_type=F32),
                                            mask)[0], axis=1, keepdims=True) for hd in range(2))

        near = row_sums(i, diag)
        near = lax.cond(i > 0, lambda: tuple(a + b for a, b in zip(near, row_sums(i - 1, None))), lambda: near)
        j_dead, live_sums = lax.while_loop(
            lambda st: jnp.logical_and(st[0] >= 0, _sb_alive(st[1][0], st[1][1])),
            lambda st: (st[0] - 1, tuple(a + b for a, b in zip(st[1], row_sums(st[0], None)))),
            (i - 2, near))
        ctots.extend(live_sums)
        col = jnp.zeros((tq, 1), F32)
        init = tuple((col, col, jnp.zeros((tq, width), F32)) for _ in range(2))
        carry = lax.fori_loop(jnp.maximum(j_dead, -1) + 1, i - 1, lambda j, cr: blocks([j], cr, [None]), init)
        carry = lax.cond(i > 0, lambda cr: blocks([i - 1, i], cr, [None, diag]), lambda cr: blocks([i], cr, [diag]),
                         carry)
        dq_ref[...] = (jnp.where(first_head, carry[0][2], carry[1][2]) * scale).astype(BF16)

    tile = pl.BlockSpec((tq, width), lambda p, i: (i, p))
    full = pl.BlockSpec((L, width), lambda p, i: (0, p))
    sds = jax.ShapeDtypeStruct((L, npair * width), F32)
    return _pc(body, name=name, grid=(npair, nq),
               in_specs=[tile, pl.BlockSpec((L, width), lambda p, i: (0, npair + p)),
                         pl.BlockSpec((L, width), lambda p, i: (0, 2 * npair + p)),
                         pl.BlockSpec((tq, width), lambda p, i: (i, do_off + p))],
               out_specs=[tile, full, full],
               out_shape=[jax.ShapeDtypeStruct((L, npair * width), BF16), sds, sds],
               dims=("parallel", "arbitrary"))(qkv, qkv, qkv, do)


def _adamw(w, g, m, v, *, name, tm=256):
    R, C = w.shape
    tm = min(tm, R)
    assert R % tm == 0, (R, tm)
    c1 = 1.0 - ADAM_B1 ** ADAM_STEP
    c2 = 1.0 - ADAM_B2 ** ADAM_STEP

    def body(w_ref, g_ref, m_ref, v_ref, d_ref, nm_ref, nv_ref):
        gv = g_ref[...]
        nm = ADAM_B1 * m_ref[...] + (1.0 - ADAM_B1) * gv
        nv = ADAM_B2 * v_ref[...] + (1.0 - ADAM_B2) * (gv * gv)
        d_ref[...] = -ADAM_LR * ((nm / c1) / (jnp.sqrt(nv / c2) + ADAM_EPS) + ADAM_WD * w_ref[...])
        nm_ref[...] = nm
        nv_ref[...] = nv

    blk = pl.BlockSpec((tm, C), lambda i: (i, 0))
    sds = jax.ShapeDtypeStruct((R, C), F32)
    return _pc(body, name=name, grid=(R // tm,), in_specs=[blk] * 4, out_specs=[blk] * 3, out_shape=[sds] * 3,
               dims=("parallel",))(w, g, m, v)


ELEMENTWISE_BLOCK_BYTES = 1 << 20


def _row_tile(rows, cols):
    for t in (512, 384, 352, 256, 176, 128, 88, 64, 32, 16, 8):
        if rows % t == 0 and t * cols * 4 <= ELEMENTWISE_BLOCK_BYTES:
            return t
    raise ValueError((rows, cols))


def _adamw_layers(w, g_mine, g_other, m, v, c, *, name):
    _, R, C = w.shape
    tm = _row_tile(R, C)
    c1 = 1.0 - ADAM_B1 ** ADAM_STEP
    c2 = 1.0 - ADAM_B2 ** ADAM_STEP

    def body(c_ref, w_ref, gm_ref, go_ref, m_ref, v_ref, g_ref, d_ref, nm_ref, nv_ref):
        gv = jnp.where(pl.program_id(0) == c_ref[0], gm_ref[...], go_ref[...])
        nm = ADAM_B1 * m_ref[...] + (1.0 - ADAM_B1) * gv
        nv = ADAM_B2 * v_ref[...] + (1.0 - ADAM_B2) * (gv * gv)
        g_ref[...] = gv
        d_ref[...] = -ADAM_LR * ((nm / c1) / (jnp.sqrt(nv / c2) + ADAM_EPS) + ADAM_WD * w_ref[...])
        nm_ref[...] = nm
        nv_ref[...] = nv

    slab = pl.BlockSpec((None, tm, C), lambda l, i, c_ref: (l, i, 0))
    mine = pl.BlockSpec((tm, C), lambda l, i, c_ref: (jnp.where(l == c_ref[0], i, 0), 0))
    other = pl.BlockSpec((tm, C), lambda l, i, c_ref: (jnp.where(l == c_ref[0], 0, i), 0))
    grid_spec = pltpu.PrefetchScalarGridSpec(num_scalar_prefetch=1, grid=(2, R // tm),
                                             in_specs=[slab, mine, other, slab, slab], out_specs=[slab] * 4)
    return _pc_prefetch(body, name=name, grid_spec=grid_spec, out_shape=[jax.ShapeDtypeStruct(w.shape, F32)] * 4,
                        dims=("parallel", "parallel"))(c.reshape(1).astype(jnp.int32), w, g_mine, g_other, m, v)


def _add_layers(g0, g1, ra, c, *, name):
    S, R, C = ra.shape
    tm = _row_tile(R, C)

    def body(c_ref, g0_ref, g1_ref, r_ref, o_ref):
        mine = jnp.where(c_ref[0] == 0, g0_ref[...], g1_ref[...])
        o_ref[...] = (mine + r_ref[...]).astype(BF16)

    def walked_if(layer):
        return lambda s, i, c_ref: (jnp.where(c_ref[0] == layer, s, 0), jnp.where(c_ref[0] == layer, i, 0), 0)

    blk = lambda s, i, c_ref: (s, i, 0)
    grid_spec = pltpu.PrefetchScalarGridSpec(
        num_scalar_prefetch=1, grid=(S, R // tm),
        in_specs=[pl.BlockSpec((None, tm, C), walked_if(0)), pl.BlockSpec((None, tm, C), walked_if(1)),
                  pl.BlockSpec((None, tm, C), blk)],
        out_specs=pl.BlockSpec((None, tm, C), blk))
    return _pc_prefetch(body, name=name, grid_spec=grid_spec, out_shape=jax.ShapeDtypeStruct((S, R, C), BF16),
                        dims=("parallel", "parallel"))(c.reshape(1).astype(jnp.int32), g0, g1, ra)


def _add_chips(p, rb, chip, *, name):
    S, Rh, C = p.shape
    tm = _row_tile(Rh, C)

    def body(s_ref, p_ref, r_ref, o_ref):
        o_ref[...] = ((p_ref[...].astype(F32) + r_ref[0].astype(F32)) + r_ref[1].astype(F32)) + r_ref[2].astype(F32)

    grid_spec = pltpu.PrefetchScalarGridSpec(
        num_scalar_prefetch=1, grid=(Rh // tm,),
        in_specs=[pl.BlockSpec((None, tm, C), lambda i, s_ref: (s_ref[0], i, 0)),
                  pl.BlockSpec((3, tm, C), lambda i, s_ref: (0, i, 0))],
        out_specs=pl.BlockSpec((tm, C), lambda i, s_ref: (i, 0)))
    return _pc_prefetch(body, name=name, grid_spec=grid_spec, out_shape=jax.ShapeDtypeStruct((Rh, C), F32),
                        dims=("parallel",))(chip.reshape(1).astype(jnp.int32), p, rb)


def _sum_slots(g, *, name):
    n, R, C = g.shape

    def body(g_ref, o_ref):
        acc = g_ref[0]
        for s in range(1, n):
            acc = acc + g_ref[s]
        o_ref[...] = acc

    return _pc(body, name=name, grid=(1,), in_specs=[pl.BlockSpec((n, R, C), lambda i: (0, 0, 0))],
               out_specs=pl.BlockSpec((R, C), lambda i: (0, 0)), out_shape=jax.ShapeDtypeStruct((R, C), F32),
               dims=("arbitrary",))(g)


ANY = pl.BlockSpec(memory_space=pl.ANY)


def _place():
    return lax.axis_index("x"), lax.axis_index("y"), lax.axis_index("c")


def _other_chips(x, y):
    return [(1 - x, y), (x, 1 - y), (1 - x, 1 - y)]


def _allgather_chips(ws, *, name):
    n = len(ws)

    def body(*refs):
        w_refs, out_refs, send_sems, recv_sems = refs[:n], refs[n:2 * n], refs[2 * n], refs[2 * n + 1]
        x, y, c = _place()
        sib = (x, y, 1 - c)
        south = c == 0
        first = (jnp.where(south, 1 - x, x), jnp.where(south, y, 1 - y))
        second = (jnp.where(south, x, 1 - x), jnp.where(south, 1 - y, y))
        chip_of = lambda p: 2 * p[0] + p[1]
        me, diagonal = 2 * x + y, 2 * (1 - x) + (1 - y)

        def copy(a, k, chip_id, layer, to, own=False):
            src = w_refs[a].at[layer] if own else out_refs[a].at[chip_id, layer]
            return pltpu.make_async_remote_copy(src_ref=src, dst_ref=out_refs[a].at[chip_id, layer],
                                                send_sem=send_sems.at[k * n + a], recv_sem=recv_sems.at[k * n + a],
                                                device_id=to, device_id_type=MESH)

        started = []

        def start(cp):
            cp.start()
            started.append(cp)

        for a in range(n):
            start(copy(a, 0, me, c, (*first, c), own=True))
            start(copy(a, 1, me, c, (*second, c), own=True))
        landed = [chip_of(first), chip_of(second), diagonal]
        for k in range(3):
            for a in range(n):
                copy(a, k, landed[k], c, sib).wait_recv()
                if k == 0:
                    start(copy(a, 2, landed[0], c, (*second, c)))
                start(copy(a, 3 + k, landed[k], c, sib))
        from_sibling = [chip_of(second), chip_of(first), diagonal]
        for k in range(3):
            for a in range(n):
                copy(a, 3 + k, from_sibling[k], 1 - c, sib).wait_recv()
        for cp in started:
            cp.wait_send()

    return _pc_comm(body, name=name, in_specs=[ANY] * n, out_specs=[ANY] * n,
                    out_shape=[jax.ShapeDtypeStruct((N_CHIPS,) + w.shape, w.dtype) for w in ws],
                    scratch_shapes=[pltpu.SemaphoreType.DMA((6 * n,)), pltpu.SemaphoreType.DMA((6 * n,))])(*ws)


def _send_other_layer_to_sibling(g0s, g1s, *, name):
    n = len(g0s)

    def body(*refs):
        g_refs = (refs[:n], refs[n:2 * n])
        out_refs, send_sems, recv_sems = refs[2 * n:3 * n], refs[3 * n], refs[3 * n + 1]
        x, y, c = _place()

        def copy(a, layer):
            return pltpu.make_async_remote_copy(src_ref=g_refs[layer][a], dst_ref=out_refs[a], send_sem=send_sems.at[a],
                                                recv_sem=recv_sems.at[a], device_id=(x, y, 1 - c), device_id_type=MESH)

        for layer in range(2):
            @pl.when(c == 1 - layer)
            def _(layer=layer):
                for a in range(n):
                    copy(a, layer).start()
        for a in range(n):
            copy(a, 0).wait()

    return _pc_comm(body, name=name, in_specs=[ANY] * (2 * n), out_specs=[ANY] * n,
                    out_shape=[jax.ShapeDtypeStruct(g.shape, g.dtype) for g in g0s],
                    scratch_shapes=[pltpu.SemaphoreType.DMA((n,)), pltpu.SemaphoreType.DMA((n,))])(*g0s, *g1s)


def _scatter_to_chips(ps, *, name):
    n = len(ps)

    def body(*refs):
        p_refs, rb_refs, send_sems, recv_sems = refs[:n], refs[n:2 * n], refs[2 * n], refs[2 * n + 1]
        x, y, c = _place()
        chips = _other_chips(x, y)
        sends = [pltpu.make_async_remote_copy(src_ref=p_refs[a].at[2 * px + py], dst_ref=rb_refs[a].at[j],
                                              send_sem=send_sems.at[j * n + a], recv_sem=recv_sems.at[j * n + a],
                                              device_id=(px, py, c), device_id_type=MESH)
                 for j, (px, py) in enumerate(chips) for a in range(n)]
        for cp in sends:
            cp.start()
        for cp in sends:
            cp.wait()

    return _pc_comm(body, name=name, in_specs=[ANY] * n, out_specs=[ANY] * n,
                    out_shape=[jax.ShapeDtypeStruct((3,) + p.shape[1:], p.dtype) for p in ps],
                    scratch_shapes=[pltpu.SemaphoreType.DMA((3 * n,)), pltpu.SemaphoreType.DMA((3 * n,))])(*ps)


def _swap_with_sibling(fs, *, name):
    n = len(fs)

    def body(*refs):
        f_refs, out_refs, send_sems, recv_sems = refs[:n], refs[n:2 * n], refs[2 * n], refs[2 * n + 1]
        x, y, c = _place()
        copies = [pltpu.make_async_remote_copy(src_ref=f_refs[a], dst_ref=out_refs[a], send_sem=send_sems.at[a],
                                               recv_sem=recv_sems.at[a], device_id=(x, y, 1 - c), device_id_type=MESH)
                  for a in range(n)]
        for cp in copies:
            cp.start()
        for cp in copies:
            cp.wait()

    return _pc_comm(body, name=name, in_specs=[ANY] * n, out_specs=[ANY] * n,
                    out_shape=[jax.ShapeDtypeStruct(f.shape, f.dtype) for f in fs],
                    scratch_shapes=[pltpu.SemaphoreType.DMA((n,)), pltpu.SemaphoreType.DMA((n,))])(*fs)


def _allgather_devices(v, *, name):
    R, C = v.shape

    def body(v_ref, out_ref, send_sems, recv_sems):
        x, y, c = _place()
        me = 4 * x + 2 * y + c
        out_ref[me] = v_ref[...]
        peers = []
        for k in range(1, 8):
            fx, fy, fc = (k >> 2) & 1, (k >> 1) & 1, k & 1
            px = 1 - x if fx else x
            py = 1 - y if fy else y
            pcc = 1 - c if fc else c
            peers.append((px, py, pcc))
        sends = []
        for k, peer in enumerate(peers):
            cp = pltpu.make_async_remote_copy(src_ref=v_ref, dst_ref=out_ref.at[me], send_sem=send_sems.at[k],
                                              recv_sem=recv_sems.at[k], device_id=peer, device_id_type=MESH)
            cp.start()
            sends.append(cp)
        for k, (px, py, pcc) in enumerate(peers):
            pltpu.make_async_remote_copy(src_ref=v_ref, dst_ref=out_ref.at[4 * px + 2 * py + pcc],
                                         send_sem=send_sems.at[k], recv_sem=recv_sems.at[k], device_id=peers[k],
                                         device_id_type=MESH).wait_recv()
        for cp in sends:
            cp.wait_send()

    vm = pl.BlockSpec(memory_space=pltpu.VMEM)
    return _pc_comm(body, name=name, in_specs=[vm], out_specs=vm, out_shape=jax.ShapeDtypeStruct((8, R, C), F32),
                    scratch_shapes=[pltpu.SemaphoreType.DMA((7,)), pltpu.SemaphoreType.DMA((7,))])(v)


D_MODEL = 1024
SC_W = D_MODEL // 4
GDN_W = D_MODEL // 2
SB_W = D_MODEL - SC_W - GDN_W
D_FF = 256 * ((8 * D_MODEL // 3 + 255) // 256)
O_SC, O_GQKV, O_GZ, O_GA, O_SB = 0, 3 * SC_W, 3 * SC_W + 3 * GDN_W, 3 * SC_W + 4 * GDN_W, \
    3 * SC_W + 4 * GDN_W + 2 * GDN_HEADS
P_GQKV, P_SC, P_SB = 0, 3 * GDN_W, 3 * GDN_W + 3 * SC_W
P_GZ = P_SB + 3 * SB_W
P_GAB = P_GZ + GDN_W
P_PAD = 256


def _proj_to_kernel_layout(w):
    pad = jnp.zeros((w.shape[0], P_PAD - 2 * GDN_HEADS), w.dtype)
    return jnp.concatenate([w[:, O_GQKV:O_GZ], w[:, O_SC:O_GQKV], w[:, O_SB:], w[:, O_GZ:O_GA], w[:, O_GA:O_SB], pad],
                           axis=1)


def _proj_from_kernel_layout(g):
    return jnp.concatenate([g[:, P_SC:P_SB], g[:, P_GQKV:P_SC], g[:, P_GZ:P_GAB], g[:, P_GAB:P_GAB + 2 * GDN_HEADS],
                            g[:, P_SB:P_GZ]], axis=1)


def _mixout_to_kernel_layout(w):
    return jnp.concatenate([w[SC_W:SC_W + GDN_W], w[:SC_W], w[SC_W + GDN_W:]], axis=0)


def _pack_vec(parts, rows_to):
    flat = jnp.concatenate([p.reshape(-1) for p in parts])
    return jnp.pad(flat, (0, rows_to * LANES - flat.shape[0])).reshape(rows_to, LANES)


def _unpack_vec(mat, shapes):
    flat = mat.reshape(-1)
    out, r = [], 0
    for shp in shapes:
        n = int(np.prod(shp))
        out.append(flat[r:r + n].reshape(shp))
        r += n
    return out


def _round_up(n, m):
    return (n + m - 1) // m * m


def _layer_fwd(x, p, l):
    L = x.shape[0]
    tag = "l%d_" % l
    h = _rmsnorm_fwd(x, p["wn_mix"], name=tag + "norm_mix")
    proj = _matmul(h, p["w_in"], "nn", tm=2048, tn=768, tk=D_MODEL, name=tag + "proj")
    (y_sc,) = _conv_pointwise_fwd([(proj, P_SC + SC_W), (proj, P_SC + 2 * SC_W)], [(p["w_sconv"], 0)], [(proj, P_SC)],
                                  _pre_product, _post_gate_mul, [(SC_W, BF16)], tc=SC_W, tm=512, name=tag + "sconv")
    (qkv,) = _conv_pointwise_fwd([(proj, P_GQKV)], [(p["w_gdn_conv"], 0)], [], _pre_identity, _post_silu,
                                 [(3 * GDN_W, F32)], tc=GDN_W, tm=1024, name=tag + "gdn_conv")
    qe, ke, u, w, attn, eg = _gdn_prep_fwd(qkv, proj, P_GAB, p["a_log"], p["dt_bias"], name=tag + "gdn_prep")
    y_gdn, states = _gdn_scan_fwd(qe, ke, u, w, attn, eg, proj, P_GZ, p["wgn"], name=tag + "gdn_scan")
    sb_scale = (SB_W // SB_HEADS) ** -0.5
    sbqkv = jnp.concatenate([proj[:, P_SB:P_SB + SB_W] * sb_scale, proj[:, P_SB + SB_W:P_SB + 3 * SB_W]],
                            axis=1).astype(BF16)
    y_sb = _sb_fwd(sbqkv, name=tag + "sb_fwd")
    y_cat = [y_gdn, y_sc, y_sb]
    x2 = _matmul_rows_parts(y_cat, p["w_out"], "nn", res=x, name=tag + "mix_out")
    h2 = _rmsnorm_fwd(x2, p["wn_ffn"], name=tag + "norm_ffn")
    up_g = _matmul(h2, p["w_up_g"], "nn", tm=1024, tn=D_FF // 2, tk=D_MODEL, name=tag + "up_gate")
    up_v = _matmul(h2, p["w_up_v"], "nn", tm=1024, tn=D_FF // 2, tk=D_MODEL, name=tag + "up_val")
    (act,) = _conv_pointwise_fwd([(up_g, 0), (up_v, 0)], [(p["w_fconv_g"], 0), (p["w_fconv_v"], 0)], [],
                                 _pre_identity, _post_swiglu, [(D_FF, BF16)], tc=256, tm=2048, name=tag + "ffn_act")
    x3 = _matmul(act, p["w_down"], "nn", tm=1024, tn=D_MODEL, tk=D_FF // 2, res=x2, name=tag + "ffn_down")
    saved = dict(x=x, h=h, proj=proj, qkv=qkv, qe=qe, ke=ke, u=u, w=w, attn=attn, eg=eg, states=states,
                 sbqkv=sbqkv, y_cat=y_cat, x2=x2, h2=h2, up_g=up_g, up_v=up_v, act=act)
    return x3, saved


def _layer_bwd(dx3, p, s, l):
    L = dx3.shape[0]
    tag = "l%d_b_" % l
    g = {}
    dact = _matmul(dx3, p["w_down"], "nt", tm=1024, tn=D_FF // 2, tk=D_MODEL, name=tag + "dact")
    g["w_down"] = _matmul(s["act"], dx3, "tn", tm=D_FF // 2, tn=D_MODEL, tk=1024, name=tag + "dw_down")
    (dup_g, dup_v), _, (g["w_fconv_g"], g["w_fconv_v"]) = _conv_pointwise_bwd(
        [(s["up_g"], 0), (s["up_v"], 0)], [(p["w_fconv_g"], 0), (p["w_fconv_v"], 0)], [], [(dact, 0)],
        _pre_identity, _post_swiglu, D_FF, tc=256, tm=2048, name=tag + "ffn_act")
    dh2 = _matmul(dup_g, p["w_up_g"], "nt", tm=1024, tn=D_MODEL, tk=D_FF // 2, name=tag + "dh2_gate")
    dh2 = _matmul(dup_v, p["w_up_v"], "nt", tm=1024, tn=D_MODEL, tk=D_FF // 2, res=dh2, name=tag + "dh2_val")
    g["w_up"] = _matmul(s["h2"], dup_g, "tn", tm=D_MODEL, tn=D_FF // 2, tk=1024, slabs=(N_CHIPS, 0, None),
                        name=tag + "dw_up_gate")
    g["w_up"] = _matmul(s["h2"], dup_v, "tn", tm=D_MODEL, tn=D_FF // 2, tk=1024, slabs=(N_CHIPS, 2, g["w_up"]),
                        name=tag + "dw_up_val")
    dx2, g["wn_ffn"] = _rmsnorm_bwd(dh2, s["x2"], p["wn_ffn"], dx3, name=tag + "norm_ffn")
    dycat = _matmul(dx2, p["w_out"], "nt", tm=512, tn=D_MODEL, tk=D_MODEL, name=tag + "dycat")
    y_gdn, y_sc, y_sb = s["y_cat"]
    g["w_out"] = _matmul_tn_parts([y_sc, y_gdn, y_sb], dx2, name=tag + "dw_out")
    sb_scale = (SB_W // SB_HEADS) ** -0.5
    dsq, dsk, dsv = _sb_bwd(s["sbqkv"], dycat, GDN_W + SC_W, sb_scale, name=tag + "sb_bwd")
    dqe, dke, du, dw, dattn, deg, dgz, g["wgn"] = _gdn_scan_bwd(
        s["qe"], s["ke"], s["u"], s["w"], s["attn"], s["eg"], s["proj"], P_GZ, p["wgn"], s["states"], dycat, 0,
        name=tag + "gdn_scan")
    dqkv_act, dgab, g["a_log"], g["dt_bias"] = _gdn_prep_bwd(
        s["qkv"], s["proj"], P_GAB, p["a_log"], p["dt_bias"], dqe, dke, du, dw, dattn, deg, P_PAD,
        name=tag + "gdn_prep")
    (dqkv,), _, (g["w_gdn_conv"],) = _conv_pointwise_bwd(
        [(s["proj"], P_GQKV)], [(p["w_gdn_conv"], 0)], [], [(dqkv_act, 0)], _pre_identity, _post_silu, 3 * GDN_W,
        tc=GDN_W, tm=1024, name=tag + "gdn_conv")
    (dsc_c, dsc_h), (dsc_b,), (g["w_sconv"],) = _conv_pointwise_bwd(
        [(s["proj"], P_SC + SC_W), (s["proj"], P_SC + 2 * SC_W)], [(p["w_sconv"], 0)], [(s["proj"], P_SC)],
        [(dycat, GDN_W)], _pre_product, _post_gate_mul, SC_W, tc=SC_W, tm=512, name=tag + "sconv")
    dproj = [dqkv, dsc_b, dsc_c, dsc_h, dsq, dsk, dsv, dgz, dgab]
    dh = _matmul_rows_parts(dproj, p["w_in"], "nt", name=tag + "dh")
    g["w_in"] = jnp.concatenate([_matmul_tn_parts(s["h"], dproj[:4], name=tag + "dw_in_a"),
                                 _matmul_tn_parts(s["h"], dproj[4:], name=tag + "dw_in_b")], axis=1)
    dx, g["wn_mix"] = _rmsnorm_bwd(dh, s["x"], p["wn_mix"], dx2, name=tag + "norm_mix")
    return dx, g


BIG = ("w_mix_in", "w_mix_out", "w_ffn_up", "w_ffn_down")
BIG_AXIS = {"w_mix_in": 2, "w_mix_out": 1, "w_ffn_up": 2, "w_ffn_down": 1}
SMALL_SHARDED = ("w_sconv", "w_gdn_conv", "w_ffn_conv")
SMALL_REPLICATED = ("w_norm_mix", "gdn_a_log", "gdn_dt_bias", "w_gdn_norm", "w_norm_ffn", "w_norm_final")
WEIGHTS = ("w_norm_mix", "w_mix_in", "w_sconv", "w_gdn_conv", "gdn_a_log", "gdn_dt_bias", "w_gdn_norm", "w_mix_out",
           "w_norm_ffn", "w_ffn_up", "w_ffn_conv", "w_ffn_down", "w_norm_final")


def kernel(x, w_norm_mix, w_mix_in, w_sconv, w_gdn_conv, gdn_a_log, gdn_dt_bias, w_gdn_norm, w_mix_out, w_norm_ffn, w_ffn_up, w_ffn_conv, w_ffn_down, w_norm_final, loss_target, m_w_norm_mix, m_w_mix_in, m_w_sconv, m_w_gdn_conv, m_gdn_a_log, m_gdn_dt_bias, m_w_gdn_norm, m_w_mix_out, m_w_norm_ffn, m_w_ffn_up, m_w_ffn_conv, m_w_ffn_down, m_w_norm_final, v_w_norm_mix, v_w_mix_in, v_w_sconv, v_w_gdn_conv, v_gdn_a_log, v_gdn_dt_bias, v_w_gdn_norm, v_w_mix_out, v_w_norm_ffn, v_w_ffn_up, v_w_ffn_conv, v_w_ffn_down, v_w_norm_final):
    W = dict(w_norm_mix=w_norm_mix, w_mix_in=w_mix_in, w_sconv=w_sconv, w_gdn_conv=w_gdn_conv, gdn_a_log=gdn_a_log,
             gdn_dt_bias=gdn_dt_bias, w_gdn_norm=w_gdn_norm, w_mix_out=w_mix_out, w_norm_ffn=w_norm_ffn,
             w_ffn_up=w_ffn_up, w_ffn_conv=w_ffn_conv, w_ffn_down=w_ffn_down, w_norm_final=w_norm_final)
    M = dict(w_norm_mix=m_w_norm_mix, w_mix_in=m_w_mix_in, w_sconv=m_w_sconv, w_gdn_conv=m_w_gdn_conv,
             gdn_a_log=m_gdn_a_log, gdn_dt_bias=m_gdn_dt_bias, w_gdn_norm=m_w_gdn_norm, w_mix_out=m_w_mix_out,
             w_norm_ffn=m_w_norm_ffn, w_ffn_up=m_w_ffn_up, w_ffn_conv=m_w_ffn_conv, w_ffn_down=m_w_ffn_down,
             w_norm_final=m_w_norm_final)
    V = dict(w_norm_mix=v_w_norm_mix, w_mix_in=v_w_mix_in, w_sconv=v_w_sconv, w_gdn_conv=v_w_gdn_conv,
             gdn_a_log=v_gdn_a_log, gdn_dt_bias=v_gdn_dt_bias, w_gdn_norm=v_w_gdn_norm, w_mix_out=v_w_mix_out,
             w_norm_ffn=v_w_norm_ffn, w_ffn_up=v_w_ffn_up, w_ffn_conv=v_w_ffn_conv, w_ffn_down=v_w_ffn_down,
             w_norm_final=v_w_norm_final)
    depth = w_mix_in.shape[0]
    L = x.shape[1]
    mx, my, mc = lax.axis_index("x"), lax.axis_index("y"), lax.axis_index("c")
    chip = 2 * mx + my

    assert depth == 2
    own = [W[n].astype(BF16) for n in BIG]
    gathered = _allgather_chips(own, name="gather_big")
    gathered = [lax.dynamic_update_slice(g, o[None], (chip, 0, 0, 0)) for g, o in zip(gathered, own)]
    full_big = [{n: jnp.concatenate([g[b, l] for b in range(N_CHIPS)], axis=BIG_AXIS[n] - 1)
                 for n, g in zip(BIG, gathered)} for l in range(depth)]

    small_sh_shapes = [W[n].shape for n in SMALL_SHARDED]
    n_small_sh = sum(int(np.prod(s)) for s in small_sh_shapes)
    small_rows = _round_up(n_small_sh, 8 * LANES) // LANES
    small_all = _allgather_devices(_pack_vec([W[n] for n in SMALL_SHARDED], small_rows), name="gather_small")
    small_chip = [_unpack_vec(small_all[2 * b], small_sh_shapes) for b in range(N_CHIPS)]
    full_small = {n: jnp.concatenate([small_chip[b][i] for b in range(N_CHIPS)], axis=2)
                  for i, n in enumerate(SMALL_SHARDED)}

    params = []
    for l in range(depth):
        w_up = full_big[l]["w_ffn_up"]
        fconv = full_small["w_ffn_conv"][l]
        params.append(dict(
            wn_mix=w_norm_mix[l], w_in=_proj_to_kernel_layout(full_big[l]["w_mix_in"]),
            w_sconv=full_small["w_sconv"][l], w_gdn_conv=full_small["w_gdn_conv"][l],
            a_log=gdn_a_log[l].reshape(GDN_HEADS, 1, 1), dt_bias=gdn_dt_bias[l].reshape(GDN_HEADS, 1, 1),
            wgn=w_gdn_norm[l], w_out=_mixout_to_kernel_layout(full_big[l]["w_mix_out"]), wn_ffn=w_norm_ffn[l],
            w_up_g=w_up[:, :D_FF], w_up_v=w_up[:, D_FF:], w_fconv_g=fconv[:, :D_FF], w_fconv_v=fconv[:, D_FF:],
            w_down=full_big[l]["w_ffn_down"]))

    xs = x[0]
    saved = []
    for l in range(depth):
        xs, s = _layer_fwd(xs, params[l], l)
        saved.append(s)
    loss_row, dx, g_norm_final = _final_loss(xs, w_norm_final, loss_target[0], name="final_loss")
    grads = [None] * depth
    for l in reversed(range(depth)):
        dx, grads[l] = _layer_bwd(dx, params[l], saved[l], l)
    loss = lax.psum(loss_row[0, 0], ("x", "y", "c"))

    G = {
        "w_sconv": jnp.stack([grads[l]["w_sconv"] for l in range(depth)]),
        "w_gdn_conv": jnp.stack([grads[l]["w_gdn_conv"] for l in range(depth)]),
        "w_ffn_conv": jnp.stack([jnp.concatenate([grads[l]["w_fconv_g"], grads[l]["w_fconv_v"]], axis=1)
                                 for l in range(depth)]),
        "w_norm_mix": jnp.stack([grads[l]["wn_mix"].reshape(-1) for l in range(depth)]),
        "gdn_a_log": jnp.stack([grads[l]["a_log"].reshape(-1) for l in range(depth)]),
        "gdn_dt_bias": jnp.stack([grads[l]["dt_bias"].reshape(-1) for l in range(depth)]),
        "w_gdn_norm": jnp.stack([grads[l]["wgn"].reshape(-1) for l in range(depth)]),
        "w_norm_ffn": jnp.stack([grads[l]["wn_ffn"].reshape(-1) for l in range(depth)]),
        "w_norm_final": g_norm_final.reshape(-1),
    }

    def by_shard(l):
        g_in = _proj_from_kernel_layout(grads[l]["w_in"])
        g_in = g_in.reshape(D_MODEL, N_CHIPS, -1).transpose(1, 0, 2)
        return [g_in, grads[l]["w_out"].reshape(N_CHIPS, -1, D_MODEL), grads[l]["w_up"],
                grads[l]["w_down"].reshape(N_CHIPS, -1, D_MODEL)]

    g_layers = [by_shard(l) for l in range(depth)]
    from_sibling = _send_other_layer_to_sibling(g_layers[0], g_layers[1], name="rs_sibling")
    chip_sums = [_add_layers(g0, g1, ra, mc, name="rs_add_layers_" + n)
                 for n, g0, g1, ra in zip(BIG, g_layers[0], g_layers[1], from_sibling)]
    from_chips = _scatter_to_chips(chip_sums, name="rs_chips")
    mine = [_add_chips(p, rb, chip, name="rs_add_chips_" + n) for n, p, rb in zip(BIG, chip_sums, from_chips)]
    other = _swap_with_sibling(mine, name="rs_result")
    out_g, out_d, out_m, out_v = {}, {}, {}, {}
    for n, g_mine, g_other in zip(BIG, mine, other):
        out_g[n], out_d[n], out_m[n], out_v[n] = _adamw_layers(W[n], g_mine, g_other, M[n], V[n], mc,
                                                               name="adamw_" + n)

    small_names = SMALL_SHARDED + SMALL_REPLICATED
    small_full_shapes = [G[n].shape for n in small_names]
    n_small = sum(int(np.prod(s)) for s in small_full_shapes)
    red_rows = _round_up(n_small, 8 * LANES) // LANES
    partials = _allgather_devices(_pack_vec([G[n] for n in small_names], red_rows), name="reduce_small")
    summed = _unpack_vec(_sum_slots(partials, name="reduce_small_sum"), small_full_shapes)
    g_small = {}
    for n, a in zip(small_names, summed):
        if n in SMALL_SHARDED:
            width = a.shape[2] // N_CHIPS
            a = lax.dynamic_slice_in_dim(a, chip * width, width, axis=2)
        g_small[n] = a
    own_shapes = [W[n].shape for n in small_names]
    n_own = sum(int(np.prod(s)) for s in own_shapes)
    own_rows = _round_up(n_own, 8 * LANES) // LANES
    packed = [_pack_vec([src[n] for n in small_names], own_rows) for src in (W, g_small, M, V)]
    d_s, nm_s, nv_s = _adamw(*packed, name="adamw_small", tm=own_rows)
    for mat, dst in ((packed[1], out_g), (d_s, out_d), (nm_s, out_m), (nv_s, out_v)):
        for n, a in zip(small_names, _unpack_vec(mat, own_shapes)):
            dst[n] = a

    outs = [loss, dx[None]]
    for dst in (out_g, out_d, out_m, out_v):
        outs += [dst[n] for n in WEIGHTS]
    return tuple(outs)
```

```python
import jax
import jax.numpy as jnp
import numpy as np
from jax import lax
from jax.experimental import pallas as pl
from jax.experimental.pallas import tpu as pltpu

F32 = jnp.float32
BF16 = jnp.bfloat16
MESH = pl.DeviceIdType.MESH

NORM_EPS = 1e-6
GDN_HEADS = 4
GDN_CHUNK = 64
GDN_SCAN_CHUNKS = 4
SB_HEADS = 4
SB_DEAD_LOG = -110.0
ADAM_LR = 0.001
ADAM_B1 = 0.9
ADAM_B2 = 0.999
ADAM_EPS = 1e-08
ADAM_WD = 0.01
ADAM_STEP = 10

VMEM_LIMIT_BYTES = 48 * 1024 * 1024
HALO = 8
CONV_SLAB = 128
LANES = 128
N_CHIPS = 4


def _pc(body, *, name, grid, in_specs, out_specs, out_shape, scratch_shapes=(), dims=None, aliases=None):
    params = dict(vmem_limit_bytes=VMEM_LIMIT_BYTES)
    if dims is not None:
        params["dimension_semantics"] = dims
    return pl.pallas_call(body, name=name, grid=grid, in_specs=in_specs, out_specs=out_specs, out_shape=out_shape,
                          scratch_shapes=list(scratch_shapes), input_output_aliases=aliases or {},
                          compiler_params=pltpu.CompilerParams(**params))


def _pc_prefetch(body, *, name, grid_spec, out_shape, dims):
    return pl.pallas_call(body, name=name, grid_spec=grid_spec, out_shape=out_shape,
                          compiler_params=pltpu.CompilerParams(vmem_limit_bytes=VMEM_LIMIT_BYTES,
                                                               dimension_semantics=dims))


def _pc_comm(body, *, name, in_specs, out_specs, out_shape, scratch_shapes):
    return pl.pallas_call(body, name=name, in_specs=in_specs, out_specs=out_specs, out_shape=out_shape,
                          scratch_shapes=list(scratch_shapes),
                          compiler_params=pltpu.CompilerParams(vmem_limit_bytes=VMEM_LIMIT_BYTES))


_DIMS = {"nn": (((1,), (0,)), ((), ())), "nt": (((1,), (1,)), ((), ())), "tn": (((0,), (0,)), ((), ()))}


def _matmul(a, b, mode, *, name, tm=512, tn=512, tk=512, out_dtype=F32, res=None, slabs=None):
    if mode == "nn":
        (M, K), (K2, N) = a.shape, b.shape
    elif mode == "nt":
        (M, K), (N, K2) = a.shape, b.shape
    else:
        (K, M), (K2, N) = a.shape, b.shape
    assert K == K2, (a.shape, b.shape, mode)
    tm, tn, tk = min(tm, M), min(tn, N), min(tk, K)
    assert M % tm == 0 and N % tn == 0 and K % tk == 0, (M, N, K, tm, tn, tk)
    nk = K // tk
    if mode == "tn":
        a_spec = pl.BlockSpec((tk, tm), lambda i, j, k: (k, i))
    else:
        a_spec = pl.BlockSpec((tm, tk), lambda i, j, k: (i, k))
    if mode == "nt":
        b_spec = pl.BlockSpec((tn, tk), lambda i, j, k: (j, k))
    else:
        b_spec = pl.BlockSpec((tk, tn), lambda i, j, k: (k, j))
    o_spec = pl.BlockSpec((tm, tn), lambda i, j, k: (i, j))
    has_res = res is not None
    dn = _DIMS[mode]

    def body(*refs):
        if has_res:
            a_ref, b_ref, r_ref, o_ref, acc = refs
        else:
            a_ref, b_ref, o_ref, acc = refs
        k = pl.program_id(2)
        p = lax.dot_general(a_ref[...].astype(BF16), b_ref[...].astype(BF16), dn, preferred_element_type=F32)

        def finish(total):
            if has_res:
                total = total + r_ref[...].astype(F32)
            o_ref[...] = total.astype(out_dtype)

        if nk == 1:
            finish(p)
        else:
            @pl.when(k == 0)
            def _():
                acc[...] = p

            @pl.when(k > 0)
            def _():
                acc[...] += p

            @pl.when(k == nk - 1)
            def _():
                finish(acc[...])

    in_specs = [a_spec, b_spec] + ([o_spec] if has_res else [])
    args = (a, b) + ((res,) if has_res else ())
    out_shape = jax.ShapeDtypeStruct((M, N), out_dtype)
    aliases = None
    if slabs is not None:
        n_slabs, first, into = slabs
        assert not has_res and tm == M
        o_spec = pl.BlockSpec((None, tm, tn), lambda i, j, k: (j + first, i, 0))
        out_shape = jax.ShapeDtypeStruct((n_slabs, M, tn), out_dtype)
        if into is not None:
            in_specs.append(pl.BlockSpec(memory_space=pl.ANY))
            args = args + (into,)
            aliases = {2: 0}
            inner = body

            def body(a_ref, b_ref, into_ref, o_ref, acc):
                inner(a_ref, b_ref, o_ref, acc)
    return _pc(body, name=name, grid=(M // tm, N // tn, nk), in_specs=in_specs, out_specs=o_spec,
               out_shape=out_shape, scratch_shapes=[pltpu.VMEM((tm, tn), F32)],
               dims=("parallel", "parallel", "arbitrary"), aliases=aliases)(*args)


def _offsets(parts, own_width_aligned):
    offs, at = [], 0
    for p in parts:
        assert at % (p.shape[1] if own_width_aligned else LANES) == 0, (at, p.shape)
        offs.append(at)
        at += p.shape[1]
    return offs, at


def _matmul_rows_parts(parts, w, mode, *, name, tm=512, res=None):
    M = parts[0].shape[0]
    offs, K = _offsets(parts, True)
    tm = min(tm, M)
    N = w.shape[1] if mode == "nn" else w.shape[0]
    assert (w.shape[0] if mode == "nn" else w.shape[1]) == K
    has_res = res is not None
    n = len(parts)

    def body(*refs):
        o_ref = refs[-1]
        total = None
        for s in range(n):
            p = lax.dot_general(refs[s][...].astype(BF16), refs[n + s][...].astype(BF16), _DIMS[mode],
                                preferred_element_type=F32)
            total = p if total is None else total + p
        if has_res:
            total = total + refs[2 * n][...]
        o_ref[...] = total

    in_specs = [pl.BlockSpec((tm, p.shape[1]), lambda i: (i, 0)) for p in parts]
    for p, off in zip(parts, offs):
        blk = off // p.shape[1]
        if mode == "nn":
            in_specs.append(pl.BlockSpec((p.shape[1], N), lambda i, blk=blk: (blk, 0)))
        else:
            in_specs.append(pl.BlockSpec((N, p.shape[1]), lambda i, blk=blk: (0, blk)))
    o_spec = pl.BlockSpec((tm, N), lambda i: (i, 0))
    args = tuple(parts) + (w,) * n + ((res,) if has_res else ())
    return _pc(body, name=name, grid=(M // tm,), in_specs=in_specs + ([o_spec] if has_res else []), out_specs=o_spec,
               out_shape=jax.ShapeDtypeStruct((M, N), F32), dims=("parallel",))(*args)


def _matmul_tn_parts(a, b, *, name, tk=1024):
    a_parts = list(a) if isinstance(a, (list, tuple)) else [a]
    b_parts = list(b) if isinstance(b, (list, tuple)) else [b]
    assert len(a_parts) == 1 or len(b_parts) == 1
    a_offs, M = _offsets(a_parts, False)
    b_offs, N = _offsets(b_parts, False)
    K = a_parts[0].shape[0]
    tk = min(tk, K)
    na, nb = len(a_parts), len(b_parts)

    def body(*refs):
        o_ref = refs[-1]
        first = pl.program_id(0) == 0
        for s in range(na):
            for t in range(nb):
                p = lax.dot_general(refs[s][...].astype(BF16), refs[na + t][...].astype(BF16), _DIMS["tn"],
                                    preferred_element_type=F32)
                rows = slice(a_offs[s], a_offs[s] + a_parts[s].shape[1])
                cols = slice(b_offs[t], b_offs[t] + b_parts[t].shape[1])

                @pl.when(first)
                def _(p=p, rows=rows, cols=cols):
                    o_ref[rows, cols] = p

                @pl.when(jnp.logical_not(first))
                def _(p=p, rows=rows, cols=cols):
                    o_ref[rows, cols] += p

    in_specs = [pl.BlockSpec((tk, p.shape[1]), lambda k: (k, 0)) for p in a_parts + b_parts]
    return _pc(body, name=name, grid=(K // tk,), in_specs=in_specs, out_specs=pl.BlockSpec((M, N), lambda k: (0, 0)),
               out_shape=jax.ShapeDtypeStruct((M, N), F32), dims=("arbitrary",))(*a_parts, *b_parts)


def _rmsnorm_fwd(x, w, *, name, tm=512):
    L, D = x.shape
    tm = min(tm, L)

    def body(x_ref, w_ref, h_ref):
        xv = x_ref[...]
        r = lax.rsqrt(jnp.mean(xv * xv, axis=-1, keepdims=True) + NORM_EPS)
        h_ref[...] = (xv * r * w_ref[...]).astype(BF16)

    return _pc(body, name=name, grid=(L // tm,),
               in_specs=[pl.BlockSpec((tm, D), lambda i: (i, 0)), pl.BlockSpec((1, D), lambda i: (0, 0))],
               out_specs=pl.BlockSpec((tm, D), lambda i: (i, 0)), out_shape=jax.ShapeDtypeStruct((L, D), BF16),
               dims=("parallel",))(x, w.reshape(1, D))


def _rmsnorm_bwd(dh, x, w, dres, *, name, tm=512):
    L, D = x.shape
    tm = min(tm, L)

    def body(dh_ref, x_ref, w_ref, dres_ref, dx_ref, dw_ref):
        xv = x_ref[...]
        r = lax.rsqrt(jnp.mean(xv * xv, axis=-1, keepdims=True) + NORM_EPS)
        xhat = xv * r
        dhv = dh_ref[...]
        g = dhv * w_ref[...]
        dx_ref[...] = dres_ref[...] + r * (g - xhat * jnp.mean(g * xhat, axis=-1, keepdims=True))
        part = jnp.sum(dhv * xhat, axis=0, keepdims=True)

        @pl.when(pl.program_id(0) == 0)
        def _():
            dw_ref[...] = part

        @pl.when(pl.program_id(0) > 0)
        def _():
            dw_ref[...] += part

    row = pl.BlockSpec((tm, D), lambda i: (i, 0))
    vec = pl.BlockSpec((1, D), lambda i: (0, 0))
    return _pc(body, name=name, grid=(L // tm,), in_specs=[row, row, vec, row], out_specs=[row, vec],
               out_shape=[jax.ShapeDtypeStruct((L, D), F32), jax.ShapeDtypeStruct((1, D), F32)],
               dims=("arbitrary",))(dh, x, w.reshape(1, D), dres)


def _final_loss(x, w, tgt, *, name, tm=512):
    L, D = x.shape
    tm = min(tm, L)

    def body(x_ref, w_ref, t_ref, loss_ref, dx_ref, dw_ref):
        xv = x_ref[...]
        r = lax.rsqrt(jnp.mean(xv * xv, axis=-1, keepdims=True) + NORM_EPS)
        xhat = xv * r
        e = xhat * w_ref[...] - t_ref[...]
        lpart = jnp.broadcast_to(0.5 * jnp.sum(jnp.mean(e * e, axis=-1, keepdims=True), axis=0, keepdims=True),
                                 (1, LANES))
        dy = e * (1.0 / D)
        g = dy * w_ref[...]
        dx_ref[...] = r * (g - xhat * jnp.mean(g * xhat, axis=-1, keepdims=True))
        part = jnp.sum(dy * xhat, axis=0, keepdims=True)

        @pl.when(pl.program_id(0) == 0)
        def _():
            dw_ref[...] = part
            loss_ref[...] = lpart

        @pl.when(pl.program_id(0) > 0)
        def _():
            dw_ref[...] += part
            loss_ref[...] += lpart

    row = pl.BlockSpec((tm, D), lambda i: (i, 0))
    vec = pl.BlockSpec((1, D), lambda i: (0, 0))
    lsp = pl.BlockSpec((1, LANES), lambda i: (0, 0))
    return _pc(body, name=name, grid=(L // tm,), in_specs=[row, vec, row], out_specs=[lsp, row, vec],
               out_shape=[jax.ShapeDtypeStruct((1, LANES), F32), jax.ShapeDtypeStruct((L, D), F32),
                          jax.ShapeDtypeStruct((1, D), F32)],
               dims=("arbitrary",))(x, w.reshape(1, D), tgt)


def _silu(x):
    return x * jax.nn.sigmoid(x)


def _conv_pointwise_fwd(xs, ws, es, pre, post, outs, *, tc, tm, name):
    L = xs[0][0].shape[0]
    tm = min(tm, L)
    ncol = outs[0][0] // tc
    nrow = L // tm
    hb = tm // HALO
    nx, nw, ne, no = len(xs), len(ws), len(es), len(outs)
    K = ws[0][0].shape[0]

    slab = min(CONV_SLAB, tm)
    win = slab + HALO
    assert tm % slab == 0 and K - 1 <= HALO

    def body(*refs):
        i = pl.program_id(1)
        first = (i > 0).astype(F32)
        n_in = 2 * nx + nw + ne
        o_refs = refs[n_in:n_in + no]
        x_pads = refs[n_in + no:]
        for n in range(nx):
            x_pads[n][0:HALO, :] = refs[2 * n + 1][...] * first
            x_pads[n][HALO:, :] = refs[2 * n][...]
        wv = [refs[2 * nx + n][...] for n in range(nw)]
        e_refs = refs[2 * nx + nw:n_in]

        @pl.loop(0, tm // slab)
        def _(t):
            r0 = pl.multiple_of(t * slab, HALO)
            ps = pre(*[x_pads[n][pl.ds(r0, win), :] for n in range(nx)])
            us = []
            for n in range(nw):
                u = None
                for k in range(K):
                    term = wv[n][k:k + 1, :] * (ps[n] if k == K - 1 else pltpu.roll(ps[n], K - 1 - k, 0))
                    u = term if u is None else u + term
                us.append(u[HALO:])
            rows = pl.ds(r0, slab)
            for o_ref, val in zip(o_refs, post(us, [e[rows, :] for e in e_refs])):
                o_ref[rows, :] = val.astype(o_ref.dtype)

    in_specs, args = [], []
    for arr, c0 in xs:
        off = c0 // tc
        in_specs.append(pl.BlockSpec((tm, tc), lambda j, i, off=off: (i, j + off)))
        in_specs.append(pl.BlockSpec((HALO, tc), lambda j, i, off=off: (jnp.maximum(i * hb - 1, 0), j + off)))
        args += [arr, arr]
    for arr, c0 in ws:
        off = c0 // tc
        in_specs.append(pl.BlockSpec((K, tc), lambda j, i, off=off: (0, j + off)))
        args.append(arr)
    for arr, c0 in es:
        off = c0 // tc
        in_specs.append(pl.BlockSpec((tm, tc), lambda j, i, off=off: (i, j + off)))
        args.append(arr)
    out_specs = [pl.BlockSpec((tm, tc), lambda j, i: (i, j)) for _ in range(no)]
    out_shape = [jax.ShapeDtypeStruct((L, c), dt) for c, dt in outs]
    return _pc(body, name=name, grid=(ncol, nrow), in_specs=in_specs, out_specs=out_specs, out_shape=out_shape,
               scratch_shapes=[pltpu.VMEM((tm + HALO, tc), F32)] * nx, dims=("parallel", "parallel"))(*args)


def _conv_pointwise_bwd(xs, ws, es, dys, pre, post, width, *, tc, tm, name, out_dtype=BF16):
    L = xs[0][0].shape[0]
    tm = min(tm, L)
    ncol = width // tc
    nrow = L // tm
    hb = tm // HALO
    nx, nw, ne, ny = len(xs), len(ws), len(es), len(dys)
    K = ws[0][0].shape[0]

    slab = min(CONV_SLAB, tm)
    win = slab + 2 * HALO
    assert tm % slab == 0 and K - 1 <= HALO

    def body(*refs):
        i = pl.program_id(1)
        first = (i > 0).astype(F32)
        more = (i < nrow - 1).astype(F32)
        n_in = 3 * nx + nw + 2 * ne + 2 * ny
        n_out = nx + ne + nw
        dx_refs = refs[n_in:n_in + nx]
        de_refs = refs[n_in + nx:n_in + nx + ne]
        dw_refs = refs[n_in + nx + ne:n_in + n_out]
        pads = refs[n_in + n_out:]
        x_pads, e_pads, dy_pads = pads[:nx], pads[nx:nx + ne], pads[nx + ne:]
        pos = 0
        for n in range(nx):
            x_pads[n][0:HALO, :] = refs[pos + 1][...] * first
            x_pads[n][HALO:HALO + tm, :] = refs[pos][...]
            x_pads[n][HALO + tm:, :] = refs[pos + 2][...]
            pos += 3
        wv = [refs[pos + n][...] for n in range(nw)]
        pos += nw
        for n in range(ne):
            e_pads[n][0:HALO, :] = jnp.zeros((HALO, tc), F32)
            e_pads[n][HALO:HALO + tm, :] = refs[pos][...]
            e_pads[n][HALO + tm:, :] = refs[pos + 1][...]
            pos += 2
        for n in range(ny):
            dy_pads[n][0:HALO, :] = jnp.zeros((HALO, tc), F32)
            dy_pads[n][HALO:HALO + tm, :] = refs[pos][...].astype(F32)
            dy_pads[n][HALO + tm:, :] = refs[pos + 1][...].astype(F32) * more
            pos += 2

        def one_slab(t, dw_acc):
            r0 = pl.multiple_of(t * slab, HALO)
            xw = [x_pads[n][pl.ds(r0, win), :] for n in range(nx)]
            ew = [e_pads[n][pl.ds(r0, win), :] for n in range(ne)]
            dyw = [dy_pads[n][pl.ds(r0, win), :] for n in range(ny)]
            ps, pre_vjp = jax.vjp(lambda *x_: pre(*x_), *xw)
            shifted = [[p if k == K - 1 else pltpu.roll(p, K - 1 - k, 0) for k in range(K)] for p in ps]
            us = []
            for n in range(nw):
                u = None
                for k in range(K):
                    term = wv[n][k:k + 1, :] * shifted[n][k]
                    u = term if u is None else u + term
                us.append(u)
            _, post_vjp = jax.vjp(lambda u_, e_: post(u_, e_), us, ew)
            dus, des = post_vjp(dyw)
            dps, dw_new = [], []
            for n in range(nw):
                dp = None
                for k in range(K):
                    term = wv[n][k:k + 1, :] * (dus[n] if k == K - 1 else pltpu.roll(dus[n], win - (K - 1 - k), 0))
                    dp = term if dp is None else dp + term
                dps.append(dp)
                inner = dus[n][HALO:HALO + slab]
                dw_new.append([dw_acc[n][k] + jnp.sum(inner * shifted[n][k][HALO:HALO + slab], axis=0, keepdims=True)
                               for k in range(K)])
            dxs = pre_vjp(dps)
            rows = pl.ds(r0, slab)
            for r, v in zip(dx_refs, dxs):
                r[rows, :] = v[HALO:HALO + slab].astype(out_dtype)
            for r, v in zip(de_refs, des):
                r[rows, :] = v[HALO:HALO + slab].astype(out_dtype)
            return dw_new

        zero = [[jnp.zeros((1, tc), F32) for _ in range(K)] for _ in range(nw)]
        dw_tile = lax.fori_loop(0, tm // slab, one_slab, zero)
        for n in range(nw):
            for k in range(K):
                @pl.when(i == 0)
                def _(n=n, k=k):
                    dw_refs[n][k:k + 1, :] = dw_tile[n][k]

                @pl.when(i > 0)
                def _(n=n, k=k):
                    dw_refs[n][k:k + 1, :] += dw_tile[n][k]

    in_specs, args = [], []

    def add_rows(arr, c0, prev, nxt):
        off = c0 // tc
        in_specs.append(pl.BlockSpec((tm, tc), lambda j, i, off=off: (i, j + off)))
        args.append(arr)
        if prev:
            in_specs.append(pl.BlockSpec((HALO, tc), lambda j, i, off=off: (jnp.maximum(i * hb - 1, 0), j + off)))
            args.append(arr)
        if nxt:
            last = L // HALO - 1
            in_specs.append(pl.BlockSpec((HALO, tc), lambda j, i, off=off: (jnp.minimum((i + 1) * hb, last), j + off)))
            args.append(arr)

    for arr, c0 in xs:
        add_rows(arr, c0, True, True)
    for arr, c0 in ws:
        off = c0 // tc
        in_specs.append(pl.BlockSpec((K, tc), lambda j, i, off=off: (0, j + off)))
        args.append(arr)
    for arr, c0 in es:
        add_rows(arr, c0, False, True)
    for arr, c0 in dys:
        add_rows(arr, c0, False, True)
    tile = pl.BlockSpec((tm, tc), lambda j, i: (i, j))
    wtile = pl.BlockSpec((K, tc), lambda j, i: (0, j))
    out_specs = [tile] * (nx + ne) + [wtile] * nw
    out_shape = [jax.ShapeDtypeStruct((L, width), out_dtype)] * (nx + ne) + \
        [jax.ShapeDtypeStruct((K, width), F32)] * nw
    res = _pc(body, name=name, grid=(ncol, nrow), in_specs=in_specs, out_specs=out_specs, out_shape=out_shape,
              scratch_shapes=[pltpu.VMEM((tm + 2 * HALO, tc), F32)] * (nx + ne + ny),
              dims=("parallel", "arbitrary"))(*args)
    return res[:nx], res[nx:nx + ne], res[nx + ne:]


def _pre_identity(*x):
    return list(x)


def _pre_product(c, h):
    return [c * h]


def _post_silu(us, es):
    return [_silu(us[0])]


def _post_gate_mul(us, es):
    return [es[0] * us[0]]


def _post_swiglu(us, es):
    return [_silu(us[0]) * us[1]]


def _make_dot(passes):
    def raw(a, b, dn):
        a_hi = a.astype(BF16)
        b_hi = b.astype(BF16)
        out = lax.dot_general(a_hi, b_hi, dn, preferred_element_type=F32)
        if passes == 3:
            a_lo = (a - a_hi.astype(F32)).astype(BF16)
            b_lo = (b - b_hi.astype(F32)).astype(BF16)
            out = out + lax.dot_general(a_hi, b_lo, dn, preferred_element_type=F32)
            out = out + lax.dot_general(a_lo, b_hi, dn, preferred_element_type=F32)
        return out

    @jax.custom_vjp
    def nn(a, b):
        return raw(a, b, _DIMS["nn"])

    @jax.custom_vjp
    def nt(a, b):
        return raw(a, b, _DIMS["nt"])

    @jax.custom_vjp
    def tn(a, b):
        return raw(a, b, _DIMS["tn"])

    nn.defvjp(lambda a, b: (nn(a, b), (a, b)), lambda r, g: (nt(g, r[1]), tn(r[0], g)))
    nt.defvjp(lambda a, b: (nt(a, b), (a, b)), lambda r, g: (nn(g, r[1]), tn(g, r[0])))
    tn.defvjp(lambda a, b: (tn(a, b), (a, b)), lambda r, g: (nt(r[1], g), nn(r[0], g)))
    return nn, nt, tn


_NN1, _NT1, _TN1 = _make_dot(1)
_NN3, _NT3, _TN3 = _make_dot(3)


def _l2norm(x):
    return x * lax.rsqrt(jnp.sum(x * x, axis=-1, keepdims=True) + NORM_EPS)


def _split_bf16(x):
    hi = x.astype(BF16)
    return hi, (x - hi.astype(F32)).astype(BF16)


def _products_with(lhs_list, rhs):
    n, rows = len(lhs_list), lhs_list[0].shape[0]
    r_hi, r_lo = _split_bf16(rhs)
    halves = [_split_bf16(l) for l in lhs_list]
    his = [h for h, _ in halves]
    o_hi = jnp.dot(jnp.concatenate(his + [lo for _, lo in halves], axis=0), r_hi, preferred_element_type=F32)
    o_lo = jnp.dot(jnp.concatenate(his, axis=0) if n > 1 else his[0], r_lo, preferred_element_type=F32)
    return [o_hi[i * rows:(i + 1) * rows] + o_hi[(n + i) * rows:(n + i + 1) * rows] + o_lo[i * rows:(i + 1) * rows]
            for i in range(n)]


def _unit_lower_inverse_raw(a_list):
    C = a_list[0].shape[0]
    ii = lax.broadcasted_iota(jnp.int32, (C, C), 0)
    jj = lax.broadcasted_iota(jnp.int32, (C, C), 1)
    eye = jnp.where(ii == jj, 1.0, 0.0)
    qs = [-a for a in a_list]
    ts = [eye + q for q in qs]
    qs = [_products_with([q], q)[0] for q in qs]
    n = 4
    while n <= C:
        last = n == C
        prods = [_products_with([t] if last else [t, q], q) for t, q in zip(ts, qs)]
        ts = [t + pr[0] for t, pr in zip(ts, prods)]
        if not last:
            qs = [pr[1] for pr in prods]
        n *= 2
    return ts


@jax.custom_vjp
def _unit_lower_inverse(a_list):
    return _unit_lower_inverse_raw(a_list)


def _unit_lower_inverse_fwd(a_list):
    ts = _unit_lower_inverse_raw(a_list)
    return ts, ts


def _unit_lower_inverse_bwd(ts, gs):
    xs = [_TN3(t, g) for t, g in zip(ts, gs)]
    return ([-_NT3(x, t) for x, t in zip(xs, ts)],)


_unit_lower_inverse.defvjp(_unit_lower_inverse_fwd, _unit_lower_inverse_bwd)


def _gdn_prep(units):
    C, Dh = units[0][0].shape
    ii = lax.broadcasted_iota(jnp.int32, (C, C), 0)
    jj = lax.broadcasted_iota(jnp.int32, (C, C), 1)
    lane = lax.broadcasted_iota(jnp.int32, (1, C), 1)
    causal = ii >= jj
    strict = ii > jj
    qs = [_l2norm(un[0]) * (Dh ** -0.5) for un in units]
    ks = [_l2norm(un[1]) for un in units]
    betas = [jax.nn.sigmoid(un[4]) for un in units]
    gs = [-jnp.exp(un[5]) * jax.nn.softplus(un[3] + un[6]) for un in units]
    gc_rows = [jnp.sum(jnp.where(ii <= jj, g, 0.0), axis=0, keepdims=True) for g in gs]
    gc_cols = [jnp.sum(jnp.where(ii == jj, r, 0.0), axis=1, keepdims=True) for r in gc_rows]
    decays = [jnp.where(causal, jnp.exp(jnp.where(causal, c - r, 0.0)), 0.0) for c, r in zip(gc_cols, gc_rows)]
    kbs = [k * b for k, b in zip(ks, betas)]
    kks = [_NT1(kb, k) for kb, k in zip(kbs, ks)]
    qks = [_NT1(q, k) for q, k in zip(qs, ks)]
    ts = _unit_lower_inverse([jnp.where(strict, kk * d, 0.0) for kk, d in zip(kks, decays)])
    eg_cols = [jnp.exp(c) for c in gc_cols]
    uws = [_NN3(t, jnp.concatenate([un[2] * b, kb * e], axis=1))
           for t, un, b, kb, e in zip(ts, units, betas, kbs, eg_cols)]
    out = []
    for q, k, qk, d, uw, e, r, c in zip(qs, ks, qks, decays, uws, eg_cols, gc_rows, gc_cols):
        g_last = jnp.sum(jnp.where(lane == C - 1, r, 0.0), axis=1, keepdims=True)
        out.append((q * e, k * jnp.exp(g_last - c), uw[:, :Dh], uw[:, Dh:], jnp.where(causal, qk * d, 0.0),
                    jnp.broadcast_to(jnp.exp(g_last), (1, Dh))))
    return out


def _gdn_step(units):
    v_news = [un[3] - _NN1(un[4], un[0]) for un in units]
    o_state = [_NN1(un[1], un[0]) for un in units]
    o_intra = [_NN1(un[5], vn) for un, vn in zip(units, v_news)]
    s_adds = [_TN1(un[2], vn) for un, vn in zip(units, v_news)]
    out = []
    for un, a, b, s_add in zip(units, o_state, o_intra, s_adds):
        o = a + b
        y = o * lax.rsqrt(jnp.mean(o * o, axis=-1, keepdims=True) + NORM_EPS) * un[8] * _silu(un[7])
        out.append((y, un[0] * un[6] + s_add))
    return out


def _gdn_prep_fwd(qkv, gab, gab_col, a_log, dt_bias, *, name, chunks=4):
    L = qkv.shape[0]
    H, C = GDN_HEADS, GDN_CHUNK
    W = qkv.shape[1] // 3
    Dh = W // H
    N = L // C
    chunks = min(chunks, N)
    R = chunks * C
    gab_off = gab_col // LANES

    def body(q_ref, k_ref, v_ref, gab_ref, al_ref, dt_ref, qe_ref, ke_ref, u_ref, w_ref, at_ref, eg_ref):
        where = [(cc, h) for cc in range(chunks) for h in range(H)]
        units = []
        for cc, h in where:
            rows, sl = slice(cc * C, (cc + 1) * C), slice(h * Dh, (h + 1) * Dh)
            units.append((q_ref[rows, sl], k_ref[rows, sl], v_ref[rows, sl], gab_ref[rows, h:h + 1],
                          gab_ref[rows, H + h:H + h + 1], al_ref[h], dt_ref[h]))
        for (cc, h), (qe, ke, u, w, attn, eg) in zip(where, _gdn_prep(units)):
            rows, sl = slice(cc * C, (cc + 1) * C), slice(h * Dh, (h + 1) * Dh)
            qe_ref[rows, sl] = qe
            ke_ref[rows, sl] = ke
            u_ref[rows, sl] = u
            w_ref[rows, sl] = w
            at_ref[h, rows, :] = attn
            eg_ref[cc, h:h + 1, :] = eg

    col = lambda c: pl.BlockSpec((R, W), lambda n, c=c: (n, c))
    tok = pl.BlockSpec((R, LANES), lambda n: (n, gab_off))
    par = pl.BlockSpec((H, 1, 1), lambda n: (0, 0, 0))
    wide = pl.BlockSpec((R, W), lambda n: (n, 0))
    return _pc(body, name=name, grid=(N // chunks,), in_specs=[col(0), col(1), col(2), tok, par, par],
               out_specs=[wide, wide, wide, wide, pl.BlockSpec((H, R, C), lambda n: (0, n, 0)),
                          pl.BlockSpec((chunks, H, Dh), lambda n: (n, 0, 0))],
               out_shape=[jax.ShapeDtypeStruct((L, W), F32)] * 4 + [jax.ShapeDtypeStruct((H, L, C), F32),
                                                                   jax.ShapeDtypeStruct((N, H, Dh), F32)],
               dims=("parallel",))(qkv, qkv, qkv, gab, a_log, dt_bias)


def _gdn_prep_bwd(qkv, gab, gab_col, a_log, dt_bias, dqe, dke, du, dw, dattn, deg, gab_width, *, name, chunks=4):
    L = qkv.shape[0]
    H, C = GDN_HEADS, GDN_CHUNK
    W = qkv.shape[1] // 3
    Dh = W // H
    N = L // C
    chunks = min(chunks, N)
    R = chunks * C
    gab_off = gab_col // LANES

    def body(q_ref, k_ref, v_ref, gab_ref, al_ref, dt_ref, dqe_ref, dke_ref, du_ref, dw_ref, dat_ref, deg_ref,
             dqkv_ref, dgab_ref, dal_ref, ddt_ref):
        first = pl.program_id(0) == 0
        lane = lax.broadcasted_iota(jnp.int32, (C, gab_width), 1)
        dal_sum, ddt_sum = [None] * H, [None] * H
        where = [(cc, h) for cc in range(chunks) for h in range(H)]
        units, cots = [], []
        for cc, h in where:
            rows, sl = slice(cc * C, (cc + 1) * C), slice(h * Dh, (h + 1) * Dh)
            units.append((q_ref[rows, sl], k_ref[rows, sl], v_ref[rows, sl], gab_ref[rows, h:h + 1],
                          gab_ref[rows, H + h:H + h + 1], al_ref[h], dt_ref[h]))
            cots.append((dqe_ref[rows, sl], dke_ref[rows, sl], du_ref[rows, sl], dw_ref[rows, sl],
                         dat_ref[h, rows, :], deg_ref[cc, h:h + 1, :]))
        _, vjp = jax.vjp(_gdn_prep, units)
        (d_units,) = vjp(cots)
        dgabs = [jnp.zeros((C, gab_width), F32) for _ in range(chunks)]
        for (cc, h), (dq, dk, dv, dga, dgb, dal, ddt) in zip(where, d_units):
            rows = slice(cc * C, (cc + 1) * C)
            dqkv_ref[rows, h * Dh:(h + 1) * Dh] = dq
            dqkv_ref[rows, W + h * Dh:W + (h + 1) * Dh] = dk
            dqkv_ref[rows, 2 * W + h * Dh:2 * W + (h + 1) * Dh] = dv
            dgabs[cc] = dgabs[cc] + jnp.where(lane == h, dga, 0.0) + jnp.where(lane == H + h, dgb, 0.0)
            dal_sum[h] = dal if dal_sum[h] is None else dal_sum[h] + dal
            ddt_sum[h] = ddt if ddt_sum[h] is None else ddt_sum[h] + ddt
        for cc in range(chunks):
            dgab_ref[cc * C:(cc + 1) * C, :] = dgabs[cc].astype(BF16)

        @pl.when(first)
        def _():
            for h in range(H):
                dal_ref[h] = dal_sum[h]
                ddt_ref[h] = ddt_sum[h]

        @pl.when(jnp.logical_not(first))
        def _():
            for h in range(H):
                dal_ref[h] += dal_sum[h]
                ddt_ref[h] += ddt_sum[h]

    col = lambda c: pl.BlockSpec((R, W), lambda n, c=c: (n, c))
    tok = pl.BlockSpec((R, LANES), lambda n: (n, gab_off))
    par = pl.BlockSpec((H, 1, 1), lambda n: (0, 0, 0))
    wide = pl.BlockSpec((R, W), lambda n: (n, 0))
    att = pl.BlockSpec((H, R, C), lambda n: (0, n, 0))
    egs = pl.BlockSpec((chunks, H, Dh), lambda n: (n, 0, 0))
    return _pc(body, name=name, grid=(N // chunks,),
               in_specs=[col(0), col(1), col(2), tok, par, par, wide, wide, wide, wide, att, egs],
               out_specs=[pl.BlockSpec((R, 3 * W), lambda n: (n, 0)), pl.BlockSpec((R, gab_width), lambda n: (n, 0)),
                          par, par],
               out_shape=[jax.ShapeDtypeStruct((L, 3 * W), F32), jax.ShapeDtypeStruct((L, gab_width), BF16)]
               + [jax.ShapeDtypeStruct((H, 1, 1), F32)] * 2,
               dims=("arbitrary",))(qkv, qkv, qkv, gab, a_log, dt_bias, dqe, dke, du, dw, dattn, deg)


def _gdn_scan_fwd(qe, ke, u, w, attn, eg, gz, gz_col, wgn, *, name):
    L, W = qe.shape
    H, C = GDN_HEADS, GDN_CHUNK
    Dh = W // H
    N = L // C
    gz_off = gz_col // W
    cps = min(GDN_SCAN_CHUNKS, N)
    R = cps * C

    def body(qe_ref, ke_ref, u_ref, w_ref, at_ref, eg_ref, gz_ref, wgn_ref, y_ref, st_ref, s_scr):
        @pl.when(pl.program_id(0) == 0)
        def _():
            s_scr[...] = jnp.zeros_like(s_scr)

        S = [s_scr[h] for h in range(H)]
        for cc in range(cps):
            rows = slice(cc * C, (cc + 1) * C)
            units = []
            for h in range(H):
                sl = slice(h * Dh, (h + 1) * Dh)
                st_ref[cc, h] = S[h]
                units.append((S[h], qe_ref[rows, sl], ke_ref[rows, sl], u_ref[rows, sl], w_ref[rows, sl],
                              at_ref[h, rows, :], eg_ref[cc, h:h + 1, :], gz_ref[rows, sl], wgn_ref[...]))
            for h, (y, S_new) in enumerate(_gdn_step(units)):
                y_ref[rows, h * Dh:(h + 1) * Dh] = y.astype(BF16)
                S[h] = S_new
        for h in range(H):
            s_scr[h] = S[h]

    wide = pl.BlockSpec((R, W), lambda n: (n, 0))
    return _pc(body, name=name, grid=(N // cps,),
               in_specs=[wide, wide, wide, wide, pl.BlockSpec((H, R, C), lambda n: (0, n, 0)),
                         pl.BlockSpec((cps, H, Dh), lambda n: (n, 0, 0)),
                         pl.BlockSpec((R, W), lambda n: (n, gz_off)), pl.BlockSpec((1, Dh), lambda n: (0, 0))],
               out_specs=[wide, pl.BlockSpec((cps, H, Dh, Dh), lambda n: (n, 0, 0, 0))],
               out_shape=[jax.ShapeDtypeStruct((L, W), BF16), jax.ShapeDtypeStruct((N, H, Dh, Dh), F32)],
               scratch_shapes=[pltpu.VMEM((H, Dh, Dh), F32)],
               dims=("arbitrary",))(qe, ke, u, w, attn, eg, gz, wgn.reshape(1, Dh))


def _gdn_scan_bwd(qe, ke, u, w, attn, eg, gz, gz_col, wgn, states, dy, dy_col, *, name):
    L, W = qe.shape
    H, C = GDN_HEADS, GDN_CHUNK
    Dh = W // H
    N = L // C
    gz_off = gz_col // W
    dy_off = dy_col // W
    cps = min(GDN_SCAN_CHUNKS, N)
    R = cps * C
    steps = N // cps

    def body(qe_ref, ke_ref, u_ref, w_ref, at_ref, eg_ref, gz_ref, wgn_ref, st_ref, dy_ref,
             dqe_ref, dke_ref, du_ref, dw_ref, dat_ref, deg_ref, dgz_ref, dwgn_ref, ds_scr):
        first = pl.program_id(0) == 0

        @pl.when(first)
        def _():
            ds_scr[...] = jnp.zeros_like(ds_scr)

        dwgn = None
        dS = [ds_scr[h] for h in range(H)]
        for cc in reversed(range(cps)):
            rows = slice(cc * C, (cc + 1) * C)
            units, cots = [], []
            for h in range(H):
                sl = slice(h * Dh, (h + 1) * Dh)
                units.append((st_ref[cc, h], qe_ref[rows, sl], ke_ref[rows, sl], u_ref[rows, sl], w_ref[rows, sl],
                              at_ref[h, rows, :], eg_ref[cc, h:h + 1, :], gz_ref[rows, sl], wgn_ref[...]))
                cots.append((dy_ref[rows, sl].astype(F32), dS[h]))
            _, vjp = jax.vjp(_gdn_step, units)
            (d_units,) = vjp(cots)
            for h, (dS_h, dqe, dke, du, dw, dat, deg, dgz, dwg) in enumerate(d_units):
                sl = slice(h * Dh, (h + 1) * Dh)
                dS[h] = dS_h
                dqe_ref[rows, sl] = dqe
                dke_ref[rows, sl] = dke
                du_ref[rows, sl] = du
                dw_ref[rows, sl] = dw
                dat_ref[h, rows, :] = dat
                deg_ref[cc, h:h + 1, :] = deg
                dgz_ref[rows, sl] = dgz.astype(BF16)
                dwgn = dwg if dwgn is None else dwgn + dwg
        for h in range(H):
            ds_scr[h] = dS[h]

        @pl.when(first)
        def _():
            dwgn_ref[...] = dwgn

        @pl.when(jnp.logical_not(first))
        def _():
            dwgn_ref[...] += dwgn

    rev = lambda n: steps - 1 - n
    wide = pl.BlockSpec((R, W), lambda n: (rev(n), 0))
    att = pl.BlockSpec((H, R, C), lambda n: (0, rev(n), 0))
    egs = pl.BlockSpec((cps, H, Dh), lambda n: (rev(n), 0, 0))
    vec = pl.BlockSpec((1, Dh), lambda n: (0, 0))
    return _pc(body, name=name, grid=(steps,),
               in_specs=[wide, wide, wide, wide, att, egs, pl.BlockSpec((R, W), lambda n: (rev(n), gz_off)), vec,
                         pl.BlockSpec((cps, H, Dh, Dh), lambda n: (rev(n), 0, 0, 0)),
                         pl.BlockSpec((R, W), lambda n: (rev(n), dy_off))],
               out_specs=[wide, wide, wide, wide, att, egs, wide, vec],
               out_shape=[jax.ShapeDtypeStruct((L, W), F32)] * 4 + [jax.ShapeDtypeStruct((H, L, C), F32),
                                                                   jax.ShapeDtypeStruct((N, H, Dh), F32),
                                                                   jax.ShapeDtypeStruct((L, W), BF16),
                                                                   jax.ShapeDtypeStruct((1, Dh), F32)],
               scratch_shapes=[pltpu.VMEM((H, Dh, Dh), F32)],
               dims=("arbitrary",))(qe, ke, u, w, attn, eg, gz, wgn.reshape(1, Dh), states, dy)


def _sb_scores(z, mask):
    sp = jnp.maximum(z, 0.0) + jnp.log(1.0 + jnp.exp(-jnp.abs(z)))
    lom = -sp if mask is None else jnp.where(mask, -sp, 0.0)
    return lom, z - sp


def _sb_alive(c_a, c_b):
    return jnp.maximum(jnp.max(c_a), jnp.max(c_b)) >= SB_DEAD_LOG


def _sb_masks(tq, width, dh):
    rr = lax.broadcasted_iota(jnp.int32, (tq, tq), 0)
    cc = lax.broadcasted_iota(jnp.int32, (tq, tq), 1)
    first_head = lax.broadcasted_iota(jnp.int32, (tq, width), 1) < dh
    return cc < rr, jnp.where(rr > cc, 1.0, 0.0).astype(BF16), first_head


def _sb_fwd(qkv, *, name, tq=256):
    L = qkv.shape[0]
    H = SB_HEADS
    width = 2 * (qkv.shape[1] // 3 // H)
    dh = width // 2
    npair = H // 2
    tq = min(tq, L)
    nq = L // tq

    def body(q_ref, k_ref, v_ref, o_ref):
        i = pl.program_id(1)
        diag, tri, first_head = _sb_masks(tq, width, dh)
        qp = q_ref[...]
        zero = jnp.zeros_like(qp)
        qs = (jnp.where(first_head, qp, zero), jnp.where(first_head, zero, qp))

        def blocks(js, carry, masks):
            units = [(b, hd) for b in range(len(js)) for hd in range(2)]
            starts = [pl.multiple_of(j * tq, tq) for j in js]
            ks = [k_ref[pl.ds(st, tq), :] for st in starts]
            vs = [v_ref[pl.ds(st, tq), :] for st in starts]
            zs = {(b, hd): lax.dot_general(qs[hd], ks[b], _DIMS["nt"], preferred_element_type=F32)
                  for b, hd in units}
            scores = {(b, hd): _sb_scores(zs[(b, hd)], masks[b]) for b, hd in units}
            later = {un: jnp.dot(scores[un][0].astype(BF16), tri, preferred_element_type=F32) for un in units}
            cs = [carry[hd][0] for hd in range(2)]
            accs = [carry[hd][1] for hd in range(2)]
            for b, hd in units:
                lom, lb = scores[(b, hd)]
                a = jnp.exp(lb + (cs[hd] + later[(b, hd)]))
                if masks[b] is not None:
                    a = jnp.where(masks[b], a, 0.0)
                accs[hd] = accs[hd] + jnp.dot(a.astype(BF16), vs[b], preferred_element_type=F32)
                cs[hd] = cs[hd] + jnp.sum(lom, axis=1, keepdims=True)
            return tuple((cs[hd], accs[hd]) for hd in range(2))

        init = tuple((jnp.zeros((tq, 1), F32), jnp.zeros((tq, width), F32)) for _ in range(2))
        carry = lax.cond(i > 0, lambda: blocks([i, i - 1], init, [diag, None]), lambda: blocks([i], init, [diag]))
        j_end, carry = lax.while_loop(lambda st: jnp.logical_and(st[0] >= 0, _sb_alive(st[1][0][0], st[1][1][0])),
                                      lambda st: (st[0] - 1, blocks([st[0]], st[1], [None])), (i - 2, carry))
        o_ref[...] = jnp.where(first_head, carry[0][1], carry[1][1]).astype(BF16)

    return _pc(body, name=name, grid=(npair, nq),
               in_specs=[pl.BlockSpec((tq, width), lambda p, i: (i, p)),
                         pl.BlockSpec((L, width), lambda p, i: (0, npair + p)),
                         pl.BlockSpec((L, width), lambda p, i: (0, 2 * npair + p))],
               out_specs=pl.BlockSpec((tq, width), lambda p, i: (i, p)),
               out_shape=jax.ShapeDtypeStruct((L, npair * width), BF16),
               dims=("parallel", "parallel"))(qkv, qkv, qkv)


def _sb_bwd(qkv, do, do_col, scale, *, name, tq=256):
    L = qkv.shape[0]
    H = SB_HEADS
    width = 2 * (qkv.shape[1] // 3 // H)
    dh = width // 2
    npair = H // 2
    tq = min(tq, L)
    nq = L // tq
    do_off = do_col // width

    def body(q_ref, k_ref, v_ref, do_ref, dq_ref, dk_ref, dv_ref):
        i = pl.program_id(1)

        @pl.when(i == 0)
        def _():
            dk_ref[...] = jnp.zeros_like(dk_ref)
            dv_ref[...] = jnp.zeros_like(dv_ref)

        diag, tri_later, first_head = _sb_masks(tq, width, dh)
        rr = lax.broadcasted_iota(jnp.int32, (tq, tq), 0)
        cc = lax.broadcasted_iota(jnp.int32, (tq, tq), 1)
        tri_before = jnp.where(rr < cc, 1.0, 0.0).astype(BF16)
        qp = q_ref[...]
        dop = do_ref[...].astype(BF16)
        zero = jnp.zeros_like(qp)
        qs = (jnp.where(first_head, qp, zero), jnp.where(first_head, zero, qp))
        dos = (jnp.where(first_head, dop, zero), jnp.where(first_head, zero, dop))
        ctots = []

        def blocks(js, carry, masks):
            nb = len(js)
            units = [(b, hd) for b in range(nb) for hd in range(2)]
            starts = [pl.multiple_of(j * tq, tq) for j in js]
            ks = [k_ref[pl.ds(st, tq), :] for st in starts]
            vs = [v_ref[pl.ds(st, tq), :] for st in starts]
            zs = {(b, hd): lax.dot_general(qs[hd], ks[b], _DIMS["nt"], preferred_element_type=F32)
                  for b, hd in units}
            das = {(b, hd): lax.dot_general(dos[hd], vs[b], _DIMS["nt"], preferred_element_type=F32)
                   for b, hd in units}
            scores = {(b, hd): _sb_scores(zs[(b, hd)], masks[b]) for b, hd in units}
            later = {un: jnp.dot(scores[un][0].astype(BF16), tri_later, preferred_element_type=F32) for un in units}
            pcs = [carry[hd][0] for hd in range(2)]
            avals = {}
            for b, hd in units:
                pcs[hd] = pcs[hd] + jnp.sum(scores[(b, hd)][0], axis=1, keepdims=True)
                a = jnp.exp(scores[(b, hd)][1] + ((ctots[hd] - pcs[hd]) + later[(b, hd)]))
                avals[(b, hd)] = a if masks[b] is None else jnp.where(masks[b], a, 0.0)
            gs = {un: das[un] * avals[un] for un in units}
            before = {un: jnp.dot(gs[un].astype(BF16), tri_before, preferred_element_type=F32) for un in units}
            pgs = [carry[hd][1] for hd in range(2)]
            dzs = {}
            for b, hd in units:
                sig = jnp.exp(scores[(b, hd)][1])
                dz = gs[(b, hd)] * (1.0 - sig) - (pgs[hd] + before[(b, hd)]) * sig
                dzs[(b, hd)] = (dz if masks[b] is None else jnp.where(masks[b], dz, 0.0)).astype(BF16)
                pgs[hd] = pgs[hd] + jnp.sum(gs[(b, hd)], axis=1, keepdims=True)
            dqs = [carry[hd][2] for hd in range(2)]
            for b, hd in units:
                dqs[hd] = dqs[hd] + jnp.dot(dzs[(b, hd)], ks[b], preferred_element_type=F32)
            for b in range(nb):
                dk_ref[pl.ds(starts[b], tq), :] += sum(
                    lax.dot_general(dzs[(b, hd)], qs[hd], _DIMS["tn"], preferred_element_type=F32) for hd in range(2))
                dv_ref[pl.ds(starts[b], tq), :] += sum(
                    lax.dot_general(avals[(b, hd)].astype(BF16), dos[hd], _DIMS["tn"], preferred_element_type=F32)
                    for hd in range(2))
            return tuple((pcs[hd], pgs[hd], dqs[hd]) for hd in range(2))

        def row_sums(j, mask):
            kj = k_ref[pl.ds(pl.multiple_of(j * tq, tq), tq), :]
            return tuple(jnp.sum(_sb_scores(lax.dot_general(qs[hd], kj, _DIMS["nt"], preferred_element_type=F32),
                                            mask)[0], axis=1, keepdims=True) for hd in range(2))

        near = row_sums(i, diag)
        near = lax.cond(i > 0, lambda: tuple(a + b for a, b in zip(near, row_sums(i - 1, None))), lambda: near)
        j_dead, live_sums = lax.while_loop(
            lambda st: jnp.logical_and(st[0] >= 0, _sb_alive(st[1][0], st[1][1])),
            lambda st: (st[0] - 1, tuple(a + b for a, b in zip(st[1], row_sums(st[0], None)))),
            (i - 2, near))
        ctots.extend(live_sums)
        col = jnp.zeros((tq, 1), F32)
        init = tuple((col, col, jnp.zeros((tq, width), F32)) for _ in range(2))
        carry = lax.fori_loop(jnp.maximum(j_dead, -1) + 1, i - 1, lambda j, cr: blocks([j], cr, [None]), init)
        carry = lax.cond(i > 0, lambda cr: blocks([i - 1, i], cr, [None, diag]), lambda cr: blocks([i], cr, [diag]),
                         carry)
        dq_ref[...] = (jnp.where(first_head, carry[0][2], carry[1][2]) * scale).astype(BF16)

    tile = pl.BlockSpec((tq, width), lambda p, i: (i, p))
    full = pl.BlockSpec((L, width), lambda p, i: (0, p))
    sds = jax.ShapeDtypeStruct((L, npair * width), F32)
    return _pc(body, name=name, grid=(npair, nq),
               in_specs=[tile, pl.BlockSpec((L, width), lambda p, i: (0, npair + p)),
                         pl.BlockSpec((L, width), lambda p, i: (0, 2 * npair + p)),
                         pl.BlockSpec((tq, width), lambda p, i: (i, do_off + p))],
               out_specs=[tile, full, full],
               out_shape=[jax.ShapeDtypeStruct((L, npair * width), BF16), sds, sds],
               dims=("parallel", "arbitrary"))(qkv, qkv, qkv, do)


def _adamw(w, g, m, v, *, name, tm=256):
    R, C = w.shape
    tm = min(tm, R)
    assert R % tm == 0, (R, tm)
    c1 = 1.0 - ADAM_B1 ** ADAM_STEP
    c2 = 1.0 - ADAM_B2 ** ADAM_STEP

    def body(w_ref, g_ref, m_ref, v_ref, d_ref, nm_ref, nv_ref):
        gv = g_ref[...]
        nm = ADAM_B1 * m_ref[...] + (1.0 - ADAM_B1) * gv
        nv = ADAM_B2 * v_ref[...] + (1.0 - ADAM_B2) * (gv * gv)
        d_ref[...] = -ADAM_LR * ((nm / c1) / (jnp.sqrt(nv / c2) + ADAM_EPS) + ADAM_WD * w_ref[...])
        nm_ref[...] = nm
        nv_ref[...] = nv

    blk = pl.BlockSpec((tm, C), lambda i: (i, 0))
    sds = jax.ShapeDtypeStruct((R, C), F32)
    return _pc(body, name=name, grid=(R // tm,), in_specs=[blk] * 4, out_specs=[blk] * 3, out_shape=[sds] * 3,
               dims=("parallel",))(w, g, m, v)


ELEMENTWISE_BLOCK_BYTES = 1 << 20


def _row_tile(rows, cols):
    for t in (512, 384, 352, 256, 176, 128, 88, 64, 32, 16, 8):
        if rows % t == 0 and t * cols * 4 <= ELEMENTWISE_BLOCK_BYTES:
            return t
    raise ValueError((rows, cols))


def _adamw_layers(w, g_mine, g_other, m, v, c, *, name):
    _, R, C = w.shape
    tm = _row_tile(R, C)
    c1 = 1.0 - ADAM_B1 ** ADAM_STEP
    c2 = 1.0 - ADAM_B2 ** ADAM_STEP

    def body(c_ref, w_ref, gm_ref, go_ref, m_ref, v_ref, g_ref, d_ref, nm_ref, nv_ref):
        gv = jnp.where(pl.program_id(0) == c_ref[0], gm_ref[...], go_ref[...])
        nm = ADAM_B1 * m_ref[...] + (1.0 - ADAM_B1) * gv
        nv = ADAM_B2 * v_ref[...] + (1.0 - ADAM_B2) * (gv * gv)
        g_ref[...] = gv
        d_ref[...] = -ADAM_LR * ((nm / c1) / (jnp.sqrt(nv / c2) + ADAM_EPS) + ADAM_WD * w_ref[...])
        nm_ref[...] = nm
        nv_ref[...] = nv

    slab = pl.BlockSpec((None, tm, C), lambda l, i, c_ref: (l, i, 0))
    mine = pl.BlockSpec((tm, C), lambda l, i, c_ref: (jnp.where(l == c_ref[0], i, 0), 0))
    other = pl.BlockSpec((tm, C), lambda l, i, c_ref: (jnp.where(l == c_ref[0], 0, i), 0))
    grid_spec = pltpu.PrefetchScalarGridSpec(num_scalar_prefetch=1, grid=(2, R // tm),
                                             in_specs=[slab, mine, other, slab, slab], out_specs=[slab] * 4)
    return _pc_prefetch(body, name=name, grid_spec=grid_spec, out_shape=[jax.ShapeDtypeStruct(w.shape, F32)] * 4,
                        dims=("parallel", "parallel"))(c.reshape(1).astype(jnp.int32), w, g_mine, g_other, m, v)


def _add_layers(g0, g1, ra, c, *, name):
    S, R, C = ra.shape
    tm = _row_tile(R, C)

    def body(c_ref, g0_ref, g1_ref, r_ref, o_ref):
        mine = jnp.where(c_ref[0] == 0, g0_ref[...], g1_ref[...])
        o_ref[...] = (mine + r_ref[...]).astype(BF16)

    def walked_if(layer):
        return lambda s, i, c_ref: (jnp.where(c_ref[0] == layer, s, 0), jnp.where(c_ref[0] == layer, i, 0), 0)

    blk = lambda s, i, c_ref: (s, i, 0)
    grid_spec = pltpu.PrefetchScalarGridSpec(
        num_scalar_prefetch=1, grid=(S, R // tm),
        in_specs=[pl.BlockSpec((None, tm, C), walked_if(0)), pl.BlockSpec((None, tm, C), walked_if(1)),
                  pl.BlockSpec((None, tm, C), blk)],
        out_specs=pl.BlockSpec((None, tm, C), blk))
    return _pc_prefetch(body, name=name, grid_spec=grid_spec, out_shape=jax.ShapeDtypeStruct((S, R, C), BF16),
                        dims=("parallel", "parallel"))(c.reshape(1).astype(jnp.int32), g0, g1, ra)


def _add_chips(p, rb, chip, *, name):
    S, Rh, C = p.shape
    tm = _row_tile(Rh, C)

    def body(s_ref, p_ref, r_ref, o_ref):
        o_ref[...] = ((p_ref[...].astype(F32) + r_ref[0].astype(F32)) + r_ref[1].astype(F32)) + r_ref[2].astype(F32)

    grid_spec = pltpu.PrefetchScalarGridSpec(
        num_scalar_prefetch=1, grid=(Rh // tm,),
        in_specs=[pl.BlockSpec((None, tm, C), lambda i, s_ref: (s_ref[0], i, 0)),
                  pl.BlockSpec((3, tm, C), lambda i, s_ref: (0, i, 0))],
        out_specs=pl.BlockSpec((tm, C), lambda i, s_ref: (i, 0)))
    return _pc_prefetch(body, name=name, grid_spec=grid_spec, out_shape=jax.ShapeDtypeStruct((Rh, C), F32),
                        dims=("parallel",))(chip.reshape(1).astype(jnp.int32), p, rb)


def _sum_slots(g, *, name):
    n, R, C = g.shape

    def body(g_ref, o_ref):
        acc = g_ref[0]
        for s in range(1, n):
            acc = acc + g_ref[s]
        o_ref[...] = acc

    return _pc(body, name=name, grid=(1,), in_specs=[pl.BlockSpec((n, R, C), lambda i: (0, 0, 0))],
               out_specs=pl.BlockSpec((R, C), lambda i: (0, 0)), out_shape=jax.ShapeDtypeStruct((R, C), F32),
               dims=("arbitrary",))(g)


ANY = pl.BlockSpec(memory_space=pl.ANY)


def _place():
    return lax.axis_index("x"), lax.axis_index("y"), lax.axis_index("c")


def _other_chips(x, y):
    return [(1 - x, y), (x, 1 - y), (1 - x, 1 - y)]


def _allgather_chips(ws, *, name):
    n = len(ws)

    def body(*refs):
        w_refs, out_refs, send_sems, recv_sems = refs[:n], refs[n:2 * n], refs[2 * n], refs[2 * n + 1]
        x, y, c = _place()
        sib = (x, y, 1 - c)
        south = c == 0
        first = (jnp.where(south, 1 - x, x), jnp.where(south, y, 1 - y))
        second = (jnp.where(south, x, 1 - x), jnp.where(south, 1 - y, y))
        chip_of = lambda p: 2 * p[0] + p[1]
        me, diagonal = 2 * x + y, 2 * (1 - x) + (1 - y)

        def copy(a, k, chip_id, layer, to, own=False):
            src = w_refs[a].at[layer] if own else out_refs[a].at[chip_id, layer]
            return pltpu.make_async_remote_copy(src_ref=src, dst_ref=out_refs[a].at[chip_id, layer],
                                                send_sem=send_sems.at[k * n + a], recv_sem=recv_sems.at[k * n + a],
                                                device_id=to, device_id_type=MESH)

        started = []

        def start(cp):
            cp.start()
            started.append(cp)

        for a in range(n):
            start(copy(a, 0, me, c, (*first, c), own=True))
            start(copy(a, 1, me, c, (*second, c), own=True))
        landed = [chip_of(first), chip_of(second), diagonal]
        for k in range(3):
            for a in range(n):
                copy(a, k, landed[k], c, sib).wait_recv()
                if k == 0:
                    start(copy(a, 2, landed[0], c, (*second, c)))
                start(copy(a, 3 + k, landed[k], c, sib))
        from_sibling = [chip_of(second), chip_of(first), diagonal]
        for k in range(3):
            for a in range(n):
                copy(a, 3 + k, from_sibling[k], 1 - c, sib).wait_recv()
        for cp in started:
            cp.wait_send()

    return _pc_comm(body, name=name, in_specs=[ANY] * n, out_specs=[ANY] * n,
                    out_shape=[jax.ShapeDtypeStruct((N_CHIPS,) + w.shape, w.dtype) for w in ws],
                    scratch_shapes=[pltpu.SemaphoreType.DMA((6 * n,)), pltpu.SemaphoreType.DMA((6 * n,))])(*ws)


def _send_other_layer_to_sibling(g0s, g1s, *, name):
    n = len(g0s)

    def body(*refs):
        g_refs = (refs[:n], refs[n:2 * n])
        out_refs, send_sems, recv_sems = refs[2 * n:3 * n], refs[3 * n], refs[3 * n + 1]
        x, y, c = _place()

        def copy(a, layer):
            return pltpu.make_async_remote_copy(src_ref=g_refs[layer][a], dst_ref=out_refs[a], send_sem=send_sems.at[a],
                                                recv_sem=recv_sems.at[a], device_id=(x, y, 1 - c), device_id_type=MESH)

        for layer in range(2):
            @pl.when(c == 1 - layer)
            def _(layer=layer):
                for a in range(n):
                    copy(a, layer).start()
        for a in range(n):
            copy(a, 0).wait()

    return _pc_comm(body, name=name, in_specs=[ANY] * (2 * n), out_specs=[ANY] * n,
                    out_shape=[jax.ShapeDtypeStruct(g.shape, g.dtype) for g in g0s],
                    scratch_shapes=[pltpu.SemaphoreType.DMA((n,)), pltpu.SemaphoreType.DMA((n,))])(*g0s, *g1s)


def _scatter_to_chips(ps, *, name):
    n = len(ps)

    def body(*refs):
        p_refs, rb_refs, send_sems, recv_sems = refs[:n], refs[n:2 * n], refs[2 * n], refs[2 * n + 1]
        x, y, c = _place()
        chips = _other_chips(x, y)
        sends = [pltpu.make_async_remote_copy(src_ref=p_refs[a].at[2 * px + py], dst_ref=rb_refs[a].at[j],
                                              send_sem=send_sems.at[j * n + a], recv_sem=recv_sems.at[j * n + a],
                                              device_id=(px, py, c), device_id_type=MESH)
                 for j, (px, py) in enumerate(chips) for a in range(n)]
        for cp in sends:
            cp.start()
        for cp in sends:
            cp.wait()

    return _pc_comm(body, name=name, in_specs=[ANY] * n, out_specs=[ANY] * n,
                    out_shape=[jax.ShapeDtypeStruct((3,) + p.shape[1:], p.dtype) for p in ps],
                    scratch_shapes=[pltpu.SemaphoreType.DMA((3 * n,)), pltpu.SemaphoreType.DMA((3 * n,))])(*ps)


def _swap_with_sibling(fs, *, name):
    n = len(fs)

    def body(*refs):
        f_refs, out_refs, send_sems, recv_sems = refs[:n], refs[n:2 * n], refs[2 * n], refs[2 * n + 1]
        x, y, c = _place()
        copies = [pltpu.make_async_remote_copy(src_ref=f_refs[a], dst_ref=out_refs[a], send_sem=send_sems.at[a],
                                               recv_sem=recv_sems.at[a], device_id=(x, y, 1 - c), device_id_type=MESH)
                  for a in range(n)]
        for cp in copies:
            cp.start()
        for cp in copies:
            cp.wait()

    return _pc_comm(body, name=name, in_specs=[ANY] * n, out_specs=[ANY] * n,
                    out_shape=[jax.ShapeDtypeStruct(f.shape, f.dtype) for f in fs],
                    scratch_shapes=[pltpu.SemaphoreType.DMA((n,)), pltpu.SemaphoreType.DMA((n,))])(*fs)


def _allgather_devices(v, *, name):
    R, C = v.shape

    def body(v_ref, out_ref, send_sems, recv_sems):
        x, y, c = _place()
        me = 4 * x + 2 * y + c
        out_ref[me] = v_ref[...]
        peers = []
        for k in range(1, 8):
            fx, fy, fc = (k >> 2) & 1, (k >> 1) & 1, k & 1
            px = 1 - x if fx else x
            py = 1 - y if fy else y
            pcc = 1 - c if fc else c
            peers.append((px, py, pcc))
        sends = []
        for k, peer in enumerate(peers):
            cp = pltpu.make_async_remote_copy(src_ref=v_ref, dst_ref=out_ref.at[me], send_sem=send_sems.at[k],
                                              recv_sem=recv_sems.at[k], device_id=peer, device_id_type=MESH)
            cp.start()
            sends.append(cp)
        for k, (px, py, pcc) in enumerate(peers):
            pltpu.make_async_remote_copy(src_ref=v_ref, dst_ref=out_ref.at[4 * px + 2 * py + pcc],
                                         send_sem=send_sems.at[k], recv_sem=recv_sems.at[k], device_id=peers[k],
                                         device_id_type=MESH).wait_recv()
        for cp in sends:
            cp.wait_send()

    vm = pl.BlockSpec(memory_space=pltpu.VMEM)
    return _pc_comm(body, name=name, in_specs=[vm], out_specs=vm, out_shape=jax.ShapeDtypeStruct((8, R, C), F32),
                    scratch_shapes=[pltpu.SemaphoreType.DMA((7,)), pltpu.SemaphoreType.DMA((7,))])(v)


D_MODEL = 1024
SC_W = D_MODEL // 4
GDN_W = D_MODEL // 2
SB_W = D_MODEL - SC_W - GDN_W
D_FF = 256 * ((8 * D_MODEL // 3 + 255) // 256)
O_SC, O_GQKV, O_GZ, O_GA, O_SB = 0, 3 * SC_W, 3 * SC_W + 3 * GDN_W, 3 * SC_W + 4 * GDN_W, \
    3 * SC_W + 4 * GDN_W + 2 * GDN_HEADS
P_GQKV, P_SC, P_SB = 0, 3 * GDN_W, 3 * GDN_W + 3 * SC_W
P_GZ = P_SB + 3 * SB_W
P_GAB = P_GZ + GDN_W
P_PAD = 256


def _proj_to_kernel_layout(w):
    pad = jnp.zeros((w.shape[0], P_PAD - 2 * GDN_HEADS), w.dtype)
    return jnp.concatenate([w[:, O_GQKV:O_GZ], w[:, O_SC:O_GQKV], w[:, O_SB:], w[:, O_GZ:O_GA], w[:, O_GA:O_SB], pad],
                           axis=1)


def _proj_from_kernel_layout(g):
    return jnp.concatenate([g[:, P_SC:P_SB], g[:, P_GQKV:P_SC], g[:, P_GZ:P_GAB], g[:, P_GAB:P_GAB + 2 * GDN_HEADS],
                            g[:, P_SB:P_GZ]], axis=1)


def _mixout_to_kernel_layout(w):
    return jnp.concatenate([w[SC_W:SC_W + GDN_W], w[:SC_W], w[SC_W + GDN_W:]], axis=0)


def _pack_vec(parts, rows_to):
    flat = jnp.concatenate([p.reshape(-1) for p in parts])
    return jnp.pad(flat, (0, rows_to * LANES - flat.shape[0])).reshape(rows_to, LANES)


def _unpack_vec(mat, shapes):
    flat = mat.reshape(-1)
    out, r = [], 0
    for shp in shapes:
        n = int(np.prod(shp))
        out.append(flat[r:r + n].reshape(shp))
        r += n
    return out


def _round_up(n, m):
    return (n + m - 1) // m * m


def _layer_fwd(x, p, l):
    L = x.shape[0]
    tag = "l%d_" % l
    h = _rmsnorm_fwd(x, p["wn_mix"], name=tag + "norm_mix")
    proj = _matmul(h, p["w_in"], "nn", tm=2048, tn=768, tk=D_MODEL, name=tag + "proj")
    (y_sc,) = _conv_pointwise_fwd([(proj, P_SC + SC_W), (proj, P_SC + 2 * SC_W)], [(p["w_sconv"], 0)], [(proj, P_SC)],
                                  _pre_product, _post_gate_mul, [(SC_W, BF16)], tc=SC_W, tm=512, name=tag + "sconv")
    (qkv,) = _conv_pointwise_fwd([(proj, P_GQKV)], [(p["w_gdn_conv"], 0)], [], _pre_identity, _post_silu,
                                 [(3 * GDN_W, F32)], tc=GDN_W, tm=1024, name=tag + "gdn_conv")
    qe, ke, u, w, attn, eg = _gdn_prep_fwd(qkv, proj, P_GAB, p["a_log"], p["dt_bias"], name=tag + "gdn_prep")
    y_gdn, states = _gdn_scan_fwd(qe, ke, u, w, attn, eg, proj, P_GZ, p["wgn"], name=tag + "gdn_scan")
    sb_scale = (SB_W // SB_HEADS) ** -0.5
    sbqkv = jnp.concatenate([proj[:, P_SB:P_SB + SB_W] * sb_scale, proj[:, P_SB + SB_W:P_SB + 3 * SB_W]],
                            axis=1).astype(BF16)
    y_sb = _sb_fwd(sbqkv, name=tag + "sb_fwd")
    y_cat = [y_gdn, y_sc, y_sb]
    x2 = _matmul_rows_parts(y_cat, p["w_out"], "nn", res=x, name=tag + "mix_out")
    h2 = _rmsnorm_fwd(x2, p["wn_ffn"], name=tag + "norm_ffn")
    up = _matmul(h2, p["w_up"], "nn", tm=1024, tn=D_FF // 2, tk=D_MODEL, name=tag + "up")
    (act,) = _conv_pointwise_fwd([(up, 0), (up, D_FF)], [(p["w_fconv_g"], 0), (p["w_fconv_v"], 0)], [],
                                 _pre_identity, _post_swiglu, [(D_FF, BF16)], tc=256, tm=2048, name=tag + "ffn_act")
    x3 = _matmul(act, p["w_down"], "nn", tm=1024, tn=D_MODEL, tk=D_FF // 2, res=x2, name=tag + "ffn_down")
    saved = dict(x=x, h=h, proj=proj, qkv=qkv, qe=qe, ke=ke, u=u, w=w, attn=attn, eg=eg, states=states,
                 sbqkv=sbqkv, y_cat=y_cat, x2=x2, h2=h2, up=up, act=act)
    return x3, saved


def _layer_bwd(dx3, p, s, l):
    L = dx3.shape[0]
    tag = "l%d_b_" % l
    g = {}
    dact = _matmul(dx3, p["w_down"], "nt", tm=1024, tn=D_FF // 2, tk=D_MODEL, name=tag + "dact")
    g["w_down"] = _matmul(s["act"], dx3, "tn", tm=D_FF // 2, tn=D_MODEL, tk=1024, name=tag + "dw_down")
    (dup_g, dup_v), _, (g["w_fconv_g"], g["w_fconv_v"]) = _conv_pointwise_bwd(
        [(s["up"], 0), (s["up"], D_FF)], [(p["w_fconv_g"], 0), (p["w_fconv_v"], 0)], [], [(dact, 0)],
        _pre_identity, _post_swiglu, D_FF, tc=256, tm=2048, name=tag + "ffn_act")
    dh2 = _matmul_rows_parts([dup_g, dup_v], p["w_up"], "nt", name=tag + "dh2")
    g["w_up"] = _matmul(s["h2"], dup_g, "tn", tm=D_MODEL, tn=D_FF // 2, tk=1024, slabs=(N_CHIPS, 0, None),
                        name=tag + "dw_up_gate")
    g["w_up"] = _matmul(s["h2"], dup_v, "tn", tm=D_MODEL, tn=D_FF // 2, tk=1024, slabs=(N_CHIPS, 2, g["w_up"]),
                        name=tag + "dw_up_val")
    dx2, g["wn_ffn"] = _rmsnorm_bwd(dh2, s["x2"], p["wn_ffn"], dx3, name=tag + "norm_ffn")
    dycat = _matmul(dx2, p["w_out"], "nt", tm=512, tn=D_MODEL, tk=D_MODEL, name=tag + "dycat")
    y_gdn, y_sc, y_sb = s["y_cat"]
    g["w_out"] = _matmul_tn_parts([y_sc, y_gdn, y_sb], dx2, name=tag + "dw_out")
    sb_scale = (SB_W // SB_HEADS) ** -0.5
    dsq, dsk, dsv = _sb_bwd(s["sbqkv"], dycat, GDN_W + SC_W, sb_scale, name=tag + "sb_bwd")
    dqe, dke, du, dw, dattn, deg, dgz, g["wgn"] = _gdn_scan_bwd(
        s["qe"], s["ke"], s["u"], s["w"], s["attn"], s["eg"], s["proj"], P_GZ, p["wgn"], s["states"], dycat, 0,
        name=tag + "gdn_scan")
    dqkv_act, dgab, g["a_log"], g["dt_bias"] = _gdn_prep_bwd(
        s["qkv"], s["proj"], P_GAB, p["a_log"], p["dt_bias"], dqe, dke, du, dw, dattn, deg, P_PAD,
        name=tag + "gdn_prep")
    (dqkv,), _, (g["w_gdn_conv"],) = _conv_pointwise_bwd(
        [(s["proj"], P_GQKV)], [(p["w_gdn_conv"], 0)], [], [(dqkv_act, 0)], _pre_identity, _post_silu, 3 * GDN_W,
        tc=GDN_W, tm=1024, name=tag + "gdn_conv")
    (dsc_c, dsc_h), (dsc_b,), (g["w_sconv"],) = _conv_pointwise_bwd(
        [(s["proj"], P_SC + SC_W), (s["proj"], P_SC + 2 * SC_W)], [(p["w_sconv"], 0)], [(s["proj"], P_SC)],
        [(dycat, GDN_W)], _pre_product, _post_gate_mul, SC_W, tc=SC_W, tm=512, name=tag + "sconv")
    dproj = [dqkv, dsc_b, dsc_c, dsc_h, dsq, dsk, dsv, dgz, dgab]
    dh = _matmul_rows_parts(dproj, p["w_in"], "nt", name=tag + "dh")
    g["w_in"] = jnp.concatenate([_matmul_tn_parts(s["h"], dproj[:4], name=tag + "dw_in_a"),
                                 _matmul_tn_parts(s["h"], dproj[4:], name=tag + "dw_in_b")], axis=1)
    dx, g["wn_mix"] = _rmsnorm_bwd(dh, s["x"], p["wn_mix"], dx2, name=tag + "norm_mix")
    return dx, g


BIG = ("w_mix_in", "w_mix_out", "w_ffn_up", "w_ffn_down")
BIG_AXIS = {"w_mix_in": 2, "w_mix_out": 1, "w_ffn_up": 2, "w_ffn_down": 1}
SMALL_SHARDED = ("w_sconv", "w_gdn_conv", "w_ffn_conv")
SMALL_REPLICATED = ("w_norm_mix", "gdn_a_log", "gdn_dt_bias", "w_gdn_norm", "w_norm_ffn", "w_norm_final")
WEIGHTS = ("w_norm_mix", "w_mix_in", "w_sconv", "w_gdn_conv", "gdn_a_log", "gdn_dt_bias", "w_gdn_norm", "w_mix_out",
           "w_norm_ffn", "w_ffn_up", "w_ffn_conv", "w_ffn_down", "w_norm_final")


def kernel(x, w_norm_mix, w_mix_in, w_sconv, w_gdn_conv, gdn_a_log, gdn_dt_bias, w_gdn_norm, w_mix_out, w_norm_ffn, w_ffn_up, w_ffn_conv, w_ffn_down, w_norm_final, loss_target, m_w_norm_mix, m_w_mix_in, m_w_sconv, m_w_gdn_conv, m_gdn_a_log, m_gdn_dt_bias, m_w_gdn_norm, m_w_mix_out, m_w_norm_ffn, m_w_ffn_up, m_w_ffn_conv, m_w_ffn_down, m_w_norm_final, v_w_norm_mix, v_w_mix_in, v_w_sconv, v_w_gdn_conv, v_gdn_a_log, v_gdn_dt_bias, v_w_gdn_norm, v_w_mix_out, v_w_norm_ffn, v_w_ffn_up, v_w_ffn_conv, v_w_ffn_down, v_w_norm_final):
    W = dict(w_norm_mix=w_norm_mix, w_mix_in=w_mix_in, w_sconv=w_sconv, w_gdn_conv=w_gdn_conv, gdn_a_log=gdn_a_log,
             gdn_dt_bias=gdn_dt_bias, w_gdn_norm=w_gdn_norm, w_mix_out=w_mix_out, w_norm_ffn=w_norm_ffn,
             w_ffn_up=w_ffn_up, w_ffn_conv=w_ffn_conv, w_ffn_down=w_ffn_down, w_norm_final=w_norm_final)
    M = dict(w_norm_mix=m_w_norm_mix, w_mix_in=m_w_mix_in, w_sconv=m_w_sconv, w_gdn_conv=m_w_gdn_conv,
             gdn_a_log=m_gdn_a_log, gdn_dt_bias=m_gdn_dt_bias, w_gdn_norm=m_w_gdn_norm, w_mix_out=m_w_mix_out,
             w_norm_ffn=m_w_norm_ffn, w_ffn_up=m_w_ffn_up, w_ffn_conv=m_w_ffn_conv, w_ffn_down=m_w_ffn_down,
             w_norm_final=m_w_norm_final)
    V = dict(w_norm_mix=v_w_norm_mix, w_mix_in=v_w_mix_in, w_sconv=v_w_sconv, w_gdn_conv=v_w_gdn_conv,
             gdn_a_log=v_gdn_a_log, gdn_dt_bias=v_gdn_dt_bias, w_gdn_norm=v_w_gdn_norm, w_mix_out=v_w_mix_out,
             w_norm_ffn=v_w_norm_ffn, w_ffn_up=v_w_ffn_up, w_ffn_conv=v_w_ffn_conv, w_ffn_down=v_w_ffn_down,
             w_norm_final=v_w_norm_final)
    depth = w_mix_in.shape[0]
    L = x.shape[1]
    mx, my, mc = lax.axis_index("x"), lax.axis_index("y"), lax.axis_index("c")
    chip = 2 * mx + my

    assert depth == 2
    own = [W[n].astype(BF16) for n in BIG]
    gathered = _allgather_chips(own, name="gather_big")
    gathered = [lax.dynamic_update_slice(g, o[None], (chip, 0, 0, 0)) for g, o in zip(gathered, own)]
    full_big = [{n: jnp.concatenate([g[b, l] for b in range(N_CHIPS)], axis=BIG_AXIS[n] - 1)
                 for n, g in zip(BIG, gathered)} for l in range(depth)]

    small_sh_shapes = [W[n].shape for n in SMALL_SHARDED]
    n_small_sh = sum(int(np.prod(s)) for s in small_sh_shapes)
    small_rows = _round_up(n_small_sh, 8 * LANES) // LANES
    small_all = _allgather_devices(_pack_vec([W[n] for n in SMALL_SHARDED], small_rows), name="gather_small")
    small_chip = [_unpack_vec(small_all[2 * b], small_sh_shapes) for b in range(N_CHIPS)]
    full_small = {n: jnp.concatenate([small_chip[b][i] for b in range(N_CHIPS)], axis=2)
                  for i, n in enumerate(SMALL_SHARDED)}

    params = []
    for l in range(depth):
        w_up = full_big[l]["w_ffn_up"]
        fconv = full_small["w_ffn_conv"][l]
        params.append(dict(
            wn_mix=w_norm_mix[l], w_in=_proj_to_kernel_layout(full_big[l]["w_mix_in"]),
            w_sconv=full_small["w_sconv"][l], w_gdn_conv=full_small["w_gdn_conv"][l],
            a_log=gdn_a_log[l].reshape(GDN_HEADS, 1, 1), dt_bias=gdn_dt_bias[l].reshape(GDN_HEADS, 1, 1),
            wgn=w_gdn_norm[l], w_out=_mixout_to_kernel_layout(full_big[l]["w_mix_out"]), wn_ffn=w_norm_ffn[l],
            w_up=w_up, w_fconv_g=fconv[:, :D_FF], w_fconv_v=fconv[:, D_FF:],
            w_down=full_big[l]["w_ffn_down"]))

    xs = x[0]
    saved = []
    for l in range(depth):
        xs, s = _layer_fwd(xs, params[l], l)
        saved.append(s)
    loss_row, dx, g_norm_final = _final_loss(xs, w_norm_final, loss_target[0], name="final_loss")
    grads = [None] * depth
    for l in reversed(range(depth)):
        dx, grads[l] = _layer_bwd(dx, params[l], saved[l], l)
    loss = lax.psum(loss_row[0, 0], ("x", "y", "c"))

    G = {
        "w_sconv": jnp.stack([grads[l]["w_sconv"] for l in range(depth)]),
        "w_gdn_conv": jnp.stack([grads[l]["w_gdn_conv"] for l in range(depth)]),
        "w_ffn_conv": jnp.stack([jnp.concatenate([grads[l]["w_fconv_g"], grads[l]["w_fconv_v"]], axis=1)
                                 for l in range(depth)]),
        "w_norm_mix": jnp.stack([grads[l]["wn_mix"].reshape(-1) for l in range(depth)]),
        "gdn_a_log": jnp.stack([grads[l]["a_log"].reshape(-1) for l in range(depth)]),
        "gdn_dt_bias": jnp.stack([grads[l]["dt_bias"].reshape(-1) for l in range(depth)]),
        "w_gdn_norm": jnp.stack([grads[l]["wgn"].reshape(-1) for l in range(depth)]),
        "w_norm_ffn": jnp.stack([grads[l]["wn_ffn"].reshape(-1) for l in range(depth)]),
        "w_norm_final": g_norm_final.reshape(-1),
    }

    def by_shard(l):
        g_in = _proj_from_kernel_layout(grads[l]["w_in"])
        g_in = g_in.reshape(D_MODEL, N_CHIPS, -1).transpose(1, 0, 2)
        return [g_in, grads[l]["w_out"].reshape(N_CHIPS, -1, D_MODEL), grads[l]["w_up"],
                grads[l]["w_down"].reshape(N_CHIPS, -1, D_MODEL)]

    g_layers = [by_shard(l) for l in range(depth)]
    from_sibling = _send_other_layer_to_sibling(g_layers[0], g_layers[1], name="rs_sibling")
    chip_sums = [_add_layers(g0, g1, ra, mc, name="rs_add_layers_" + n)
                 for n, g0, g1, ra in zip(BIG, g_layers[0], g_layers[1], from_sibling)]
    from_chips = _scatter_to_chips(chip_sums, name="rs_chips")
    mine = [_add_chips(p, rb, chip, name="rs_add_chips_" + n) for n, p, rb in zip(BIG, chip_sums, from_chips)]
    other = _swap_with_sibling(mine, name="rs_result")
    out_g, out_d, out_m, out_v = {}, {}, {}, {}
    for n, g_mine, g_other in zip(BIG, mine, other):
        out_g[n], out_d[n], out_m[n], out_v[n] = _adamw_layers(W[n], g_mine, g_other, M[n], V[n], mc,
                                                               name="adamw_" + n)

    small_names = SMALL_SHARDED + SMALL_REPLICATED
    small_full_shapes = [G[n].shape for n in small_names]
    n_small = sum(int(np.prod(s)) for s in small_full_shapes)
    red_rows = _round_up(n_small, 8 * LANES) // LANES
    partials = _allgather_devices(_pack_vec([G[n] for n in small_names], red_rows), name="reduce_small")
    summed = _unpack_vec(_sum_slots(partials, name="reduce_small_sum"), small_full_shapes)
    g_small = {}
    for n, a in zip(small_names, summed):
        if n in SMALL_SHARDED:
            width = a.shape[2] // N_CHIPS
            a = lax.dynamic_slice_in_dim(a, chip * width, width, axis=2)
        g_small[n] = a
    own_shapes = [W[n].shape for n in small_names]
    n_own = sum(int(np.prod(s)) for s in own_shapes)
    own_rows = _round_up(n_own, 8 * LANES) // LANES
    packed = [_pack_vec([src[n] for n in small_names], own_rows) for src in (W, g_small, M, V)]
    d_s, nm_s, nv_s = _adamw(*packed, name="adamw_small", tm=own_rows)
    for mat, dst in ((packed[1], out_g), (d_s, out_d), (nm_s, out_m), (nv_s, out_v)):
        for n, a in zip(small_names, _unpack_vec(mat, own_shapes)):
            dst[n] = a

    outs = [loss, dx[None]]
    for dst in (out_g, out_d, out_m, out_v):
        outs += [dst[n] for n in WEIGHTS]
    return tuple(outs)
```
